```python
import jax, jax.numpy as jnp
from jax import lax
import numpy as np


D_MODEL = 1024
BATCH = 8
SEQ = 8192
DEPTH = 1

D_MIX = D_MODEL
D_CONV = D_MIX // 2
D_GMLP = D_MIX - D_CONV
HEAD_DIM = 64
N_CONV_HEADS = D_CONV // HEAD_DIM
N_GMLP_HEADS = D_GMLP // HEAD_DIM
CONV_WIDTH = 31
CHUNK = 128
D_FF = 2816
FFN_CONV_WIDTH = 3
N_MOD = 6
RMS_EPS = 1e-6
LN_EPS = 1e-5

kernel_name = "hybrid_conv_gmlp_adaln_block"


def rms_norm(x, g):
    xf = x.astype(jnp.float32)
    y = xf * lax.rsqrt(jnp.mean(xf * xf, axis=-1, keepdims=True) + RMS_EPS)
    return (y * g.astype(jnp.float32)).astype(x.dtype)


def layer_norm(x, g, b):
    xf = x.astype(jnp.float32)
    mu = jnp.mean(xf, axis=-1, keepdims=True)
    var = jnp.mean(jnp.square(xf - mu), axis=-1, keepdims=True)
    y = (xf - mu) * lax.rsqrt(var + LN_EPS)
    return (y * g.astype(jnp.float32) + b.astype(jnp.float32)).astype(x.dtype)


def causal_dwconv(x, w, b):
    k = w.shape[0]
    ch = x.shape[-1]
    y = lax.conv_general_dilated(
        x, w[:, None, :].astype(x.dtype), window_strides=(1,), padding=[(k - 1, 0)],
        dimension_numbers=("NWC", "WIO", "NWC"), feature_group_count=ch)
    return y + b.astype(x.dtype)


def modulate(h, shift, scale):
    return h * (1.0 + scale[:, None, :]) + shift[:, None, :]


def _fwd_setup_inputs(seed: int = 0) -> dict:
    key = jax.random.key(seed)
    ks = jax.random.split(key, 24)
    f32 = jnp.float32
    L, D = DEPTH, D_MODEL

    def nrm(k, shape, scale):
        return jax.random.normal(k, shape, f32) * scale

    return {
        "x": nrm(ks[0], (BATCH, SEQ, D), 1.0),
        "c": nrm(ks[1], (BATCH, D), 1.0),
        "w_ada": nrm(ks[2], (L, D, N_MOD * D), 0.5 * D ** -0.5),
        "b_ada": nrm(ks[3], (L, N_MOD * D), 0.01),
        "norm1_gain": 1.0 + nrm(ks[4], (L, D), 0.02),
        "w_in": nrm(ks[5], (L, D, 2 * D_MIX), D ** -0.5),
        "conv_dw_w": nrm(ks[6], (L, CONV_WIDTH, D_CONV), CONV_WIDTH ** -0.5),
        "conv_dw_b": nrm(ks[7], (L, D_CONV), 0.01),
        "conv_ln_g": 1.0 + nrm(ks[8], (L, D_CONV), 0.02),
        "conv_ln_b": nrm(ks[9], (L, D_CONV), 0.01),
        "gm_ln_g": 1.0 + nrm(ks[10], (L, D_GMLP), 0.02),
        "gm_ln_b": nrm(ks[11], (L, D_GMLP), 0.01),
        "gm_ws": nrm(ks[12], (L, N_GMLP_HEADS, CHUNK, CHUNK), CHUNK ** -0.5),
        "gm_bs": 1.0 + nrm(ks[13], (L, N_GMLP_HEADS, CHUNK), 0.01),
        "mix_out_gain": 1.0 + nrm(ks[14], (L, D_MIX), 0.02),
        "w_out": nrm(ks[15], (L, D_MIX, D), D_MIX ** -0.5),
        "norm2_gain": 1.0 + nrm(ks[16], (L, D), 0.02),
        "w_up": nrm(ks[17], (L, D, 2 * D_FF), D ** -0.5),
        "ffn_dw_w": nrm(ks[18], (L, FFN_CONV_WIDTH, 2 * D_FF), FFN_CONV_WIDTH ** -0.5),
        "ffn_dw_b": nrm(ks[19], (L, 2 * D_FF), 0.01),
        "w_down": nrm(ks[20], (L, D_FF, D), D_FF ** -0.5),
        "final_gain": 1.0 + nrm(ks[21], (D,), 0.02),
    }


def _fwd_reference(x, c, w_ada, b_ada, norm1_gain, w_in, conv_dw_w, conv_dw_b, conv_ln_g, conv_ln_b,
              gm_ln_g, gm_ln_b, gm_ws, gm_bs, mix_out_gain, w_out, norm2_gain, w_up,
              ffn_dw_w, ffn_dw_b, w_down, final_gain):
    bsz, seq, _ = x.shape
    n_chunks = seq // CHUNK
    causal_mask = jnp.tril(jnp.ones((CHUNK, CHUNK), dtype=x.dtype))
    c_act = jax.nn.silu(c)

    for l in range(DEPTH):
        mod = c_act @ w_ada[l] + b_ada[l]
        sh1, sc1, gt1, sh2, sc2, gt2 = jnp.split(mod, N_MOD, axis=-1)

        h = modulate(rms_norm(x, norm1_gain[l]), sh1, sc1)
        z = h @ w_in[l]
        ca, cg, gu, gv = jnp.split(z, [D_CONV, 2 * D_CONV, 2 * D_CONV + D_GMLP], axis=-1)

        a = ca * jax.nn.sigmoid(cg)
        a = causal_dwconv(a, conv_dw_w[l], conv_dw_b[l])
        a = jax.nn.silu(layer_norm(a, conv_ln_g[l], conv_ln_b[l]))

        gu = jax.nn.gelu(gu)
        gv = layer_norm(jax.nn.gelu(gv), gm_ln_g[l], gm_ln_b[l])
        gv = gv.reshape(bsz, n_chunks, CHUNK, N_GMLP_HEADS, HEAD_DIM)
        ws = gm_ws[l] * causal_mask[None]
        sp = jnp.einsum("hts,bnshc->bnthc", ws, gv)
        sp = sp + jnp.transpose(gm_bs[l])[None, None, :, :, None]
        g = gu * sp.reshape(bsz, seq, D_GMLP)

        y = jnp.concatenate([rms_norm(a, mix_out_gain[l, :D_CONV]),
                             rms_norm(g, mix_out_gain[l, D_CONV:])], axis=-1)
        x = x + gt1[:, None, :] * (y @ w_out[l])

        h = modulate(rms_norm(x, norm2_gain[l]), sh2, sc2)
        up = causal_dwconv(h @ w_up[l], ffn_dw_w[l], ffn_dw_b[l])
        val, gate = jnp.split(up, 2, axis=-1)
        x = x + gt2[:, None, :] * ((jax.nn.silu(gate) * val) @ w_down[l])

    return rms_norm(x, final_gain)


import jax as _jax
import jax.numpy as _jnp

TWIN_FORMAT = 'train_step'
FWD_PARAMS = ['x', 'c', 'w_ada', 'b_ada', 'norm1_gain', 'w_in', 'conv_dw_w', 'conv_dw_b', 'conv_ln_g', 'conv_ln_b', 'gm_ln_g', 'gm_ln_b', 'gm_ws', 'gm_bs', 'mix_out_gain', 'w_out', 'norm2_gain', 'w_up', 'ffn_dw_w', 'ffn_dw_b', 'w_down', 'final_gain']
TWIN_WEIGHTS = ['w_ada', 'b_ada', 'norm1_gain', 'w_in', 'conv_dw_w', 'conv_dw_b', 'conv_ln_g', 'conv_ln_b', 'gm_ln_g', 'gm_ln_b', 'gm_ws', 'gm_bs', 'mix_out_gain', 'w_out', 'norm2_gain', 'w_up', 'ffn_dw_w', 'ffn_dw_b', 'w_down', 'final_gain']
TWIN_DIFF_INPUT = 'x'
TWIN_INPUTS = ['x', 'c', 'w_ada', 'b_ada', 'norm1_gain', 'w_in', 'conv_dw_w', 'conv_dw_b', 'conv_ln_g', 'conv_ln_b', 'gm_ln_g', 'gm_ln_b', 'gm_ws', 'gm_bs', 'mix_out_gain', 'w_out', 'norm2_gain', 'w_up', 'ffn_dw_w', 'ffn_dw_b', 'w_down', 'final_gain', 'loss_target', 'm_w_ada', 'm_b_ada', 'm_norm1_gain', 'm_w_in', 'm_conv_dw_w', 'm_conv_dw_b', 'm_conv_ln_g', 'm_conv_ln_b', 'm_gm_ln_g', 'm_gm_ln_b', 'm_gm_ws', 'm_gm_bs', 'm_mix_out_gain', 'm_w_out', 'm_norm2_gain', 'm_w_up', 'm_ffn_dw_w', 'm_ffn_dw_b', 'm_w_down', 'm_final_gain', 'v_w_ada', 'v_b_ada', 'v_norm1_gain', 'v_w_in', 'v_conv_dw_w', 'v_conv_dw_b', 'v_conv_ln_g', 'v_conv_ln_b', 'v_gm_ln_g', 'v_gm_ln_b', 'v_gm_ws', 'v_gm_bs', 'v_mix_out_gain', 'v_w_out', 'v_norm2_gain', 'v_w_up', 'v_ffn_dw_w', 'v_ffn_dw_b', 'v_w_down', 'v_final_gain']
TWIN_OUTPUTS = ['loss', 'grad_x', 'grad_w_ada', 'grad_b_ada', 'grad_norm1_gain', 'grad_w_in', 'grad_conv_dw_w', 'grad_conv_dw_b', 'grad_conv_ln_g', 'grad_conv_ln_b', 'grad_gm_ln_g', 'grad_gm_ln_b', 'grad_gm_ws', 'grad_gm_bs', 'grad_mix_out_gain', 'grad_w_out', 'grad_norm2_gain', 'grad_w_up', 'grad_ffn_dw_w', 'grad_ffn_dw_b', 'grad_w_down', 'grad_final_gain', 'delta_w_ada', 'delta_b_ada', 'delta_norm1_gain', 'delta_w_in', 'delta_conv_dw_w', 'delta_conv_dw_b', 'delta_conv_ln_g', 'delta_conv_ln_b', 'delta_gm_ln_g', 'delta_gm_ln_b', 'delta_gm_ws', 'delta_gm_bs', 'delta_mix_out_gain', 'delta_w_out', 'delta_norm2_gain', 'delta_w_up', 'delta_ffn_dw_w', 'delta_ffn_dw_b', 'delta_w_down', 'delta_final_gain', 'new_m_w_ada', 'new_m_b_ada', 'new_m_norm1_gain', 'new_m_w_in', 'new_m_conv_dw_w', 'new_m_conv_dw_b', 'new_m_conv_ln_g', 'new_m_conv_ln_b', 'new_m_gm_ln_g', 'new_m_gm_ln_b', 'new_m_gm_ws', 'new_m_gm_bs', 'new_m_mix_out_gain', 'new_m_w_out', 'new_m_norm2_gain', 'new_m_w_up', 'new_m_ffn_dw_w', 'new_m_ffn_dw_b', 'new_m_w_down', 'new_m_final_gain', 'new_v_w_ada', 'new_v_b_ada', 'new_v_norm1_gain', 'new_v_w_in', 'new_v_conv_dw_w', 'new_v_conv_dw_b', 'new_v_conv_ln_g', 'new_v_conv_ln_b', 'new_v_gm_ln_g', 'new_v_gm_ln_b', 'new_v_gm_ws', 'new_v_gm_bs', 'new_v_mix_out_gain', 'new_v_w_out', 'new_v_norm2_gain', 'new_v_w_up', 'new_v_ffn_dw_w', 'new_v_ffn_dw_b', 'new_v_w_down', 'new_v_final_gain']
TWIN_LEAF_KINDS = {'loss': 'loss', 'grad_x': 'grad_x', 'grad_w_ada': 'grad_w', 'grad_b_ada': 'grad_w', 'grad_norm1_gain': 'grad_w', 'grad_w_in': 'grad_w', 'grad_conv_dw_w': 'grad_w', 'grad_conv_dw_b': 'grad_w', 'grad_conv_ln_g': 'grad_w', 'grad_conv_ln_b': 'grad_w', 'grad_gm_ln_g': 'grad_w', 'grad_gm_ln_b': 'grad_w', 'grad_gm_ws': 'grad_w', 'grad_gm_bs': 'grad_w', 'grad_mix_out_gain': 'grad_w', 'grad_w_out': 'grad_w', 'grad_norm2_gain': 'grad_w', 'grad_w_up': 'grad_w', 'grad_ffn_dw_w': 'grad_w', 'grad_ffn_dw_b': 'grad_w', 'grad_w_down': 'grad_w', 'grad_final_gain': 'grad_w', 'delta_w_ada': 'delta_w', 'delta_b_ada': 'delta_w', 'delta_norm1_gain': 'delta_w', 'delta_w_in': 'delta_w', 'delta_conv_dw_w': 'delta_w', 'delta_conv_dw_b': 'delta_w', 'delta_conv_ln_g': 'delta_w', 'delta_conv_ln_b': 'delta_w', 'delta_gm_ln_g': 'delta_w', 'delta_gm_ln_b': 'delta_w', 'delta_gm_ws': 'delta_w', 'delta_gm_bs': 'delta_w', 'delta_mix_out_gain': 'delta_w', 'delta_w_out': 'delta_w', 'delta_norm2_gain': 'delta_w', 'delta_w_up': 'delta_w', 'delta_ffn_dw_w': 'delta_w', 'delta_ffn_dw_b': 'delta_w', 'delta_w_down': 'delta_w', 'delta_final_gain': 'delta_w', 'new_m_w_ada': 'new_m', 'new_m_b_ada': 'new_m', 'new_m_norm1_gain': 'new_m', 'new_m_w_in': 'new_m', 'new_m_conv_dw_w': 'new_m', 'new_m_conv_dw_b': 'new_m', 'new_m_conv_ln_g': 'new_m', 'new_m_conv_ln_b': 'new_m', 'new_m_gm_ln_g': 'new_m', 'new_m_gm_ln_b': 'new_m', 'new_m_gm_ws': 'new_m', 'new_m_gm_bs': 'new_m', 'new_m_mix_out_gain': 'new_m', 'new_m_w_out': 'new_m', 'new_m_norm2_gain': 'new_m', 'new_m_w_up': 'new_m', 'new_m_ffn_dw_w': 'new_m', 'new_m_ffn_dw_b': 'new_m', 'new_m_w_down': 'new_m', 'new_m_final_gain': 'new_m', 'new_v_w_ada': 'new_v', 'new_v_b_ada': 'new_v', 'new_v_norm1_gain': 'new_v', 'new_v_w_in': 'new_v', 'new_v_conv_dw_w': 'new_v', 'new_v_conv_dw_b': 'new_v', 'new_v_conv_ln_g': 'new_v', 'new_v_conv_ln_b': 'new_v', 'new_v_gm_ln_g': 'new_v', 'new_v_gm_ln_b': 'new_v', 'new_v_gm_ws': 'new_v', 'new_v_gm_bs': 'new_v', 'new_v_mix_out_gain': 'new_v', 'new_v_w_out': 'new_v', 'new_v_norm2_gain': 'new_v', 'new_v_w_up': 'new_v', 'new_v_ffn_dw_w': 'new_v', 'new_v_ffn_dw_b': 'new_v', 'new_v_w_down': 'new_v', 'new_v_final_gain': 'new_v'}


def _forward(args):
    return _fwd_reference(*[args[k] for k in FWD_PARAMS])


def _output_shape():
    def fwd():
        inp = _fwd_setup_inputs(0)
        return _fwd_reference(*[inp[k] for k in FWD_PARAMS])
    out = _jax.eval_shape(fwd)
    return out.shape, out.dtype

N_MICROBATCH = 1
ADAM_LR = 0.001
ADAM_B1 = 0.9
ADAM_B2 = 0.999
ADAM_EPS = 1e-08
ADAM_WD = 0.01
ADAM_STEP = 10
PER_EXAMPLE_BATCH_AXIS = {'x': 0, 'c': 0, 'loss_target': 0}
SHARED_INPUTS = []
_WEIGHT_DTYPES = {'w_ada': _jnp.float32, 'b_ada': _jnp.float32, 'norm1_gain': _jnp.float32, 'w_in': _jnp.float32, 'conv_dw_w': _jnp.float32, 'conv_dw_b': _jnp.float32, 'conv_ln_g': _jnp.float32, 'conv_ln_b': _jnp.float32, 'gm_ln_g': _jnp.float32, 'gm_ln_b': _jnp.float32, 'gm_ws': _jnp.float32, 'gm_bs': _jnp.float32, 'mix_out_gain': _jnp.float32, 'w_out': _jnp.float32, 'norm2_gain': _jnp.float32, 'w_up': _jnp.float32, 'ffn_dw_w': _jnp.float32, 'ffn_dw_b': _jnp.float32, 'w_down': _jnp.float32, 'final_gain': _jnp.float32}
MOMENT_SCALE = {'w_ada': 8.913741e-02, 'b_ada': 1.533361e-01, 'norm1_gain': 8.509973e-02, 'w_in': 6.143622e-02, 'conv_dw_w': 7.759549e-02, 'conv_dw_b': 1.599219e-01, 'conv_ln_g': 8.910516e-02, 'conv_ln_b': 8.176382e-02, 'gm_ln_g': 4.126624e-02, 'gm_ln_b': 3.947475e-02, 'gm_ws': 2.944649e-02, 'gm_bs': 4.204182e-02, 'mix_out_gain': 7.447235e-02, 'w_out': 7.533999e-02, 'norm2_gain': 7.161722e-02, 'w_up': 3.151081e-02, 'ffn_dw_w': 3.274886e-02, 'ffn_dw_b': 2.936446e-02, 'w_down': 5.194111e-02, 'final_gain': 6.412748e+01}


def _to_microbatches(a, axis):
    t = _jnp.moveaxis(a, axis, 0)
    t = t.reshape((N_MICROBATCH, t.shape[0] // N_MICROBATCH) + t.shape[1:])
    return _jnp.moveaxis(t, 1, axis + 1)


def setup_inputs(seed: int = 0) -> dict:
    inp = _fwd_setup_inputs(seed)
    key = _jax.random.fold_in(_jax.random.key(seed), 7919)
    shape, _ = _output_shape()
    out = dict(inp)
    out["loss_target"] = _jax.random.normal(_jax.random.fold_in(key, 0), shape, _jnp.float32)
    for i, name in enumerate(TWIN_WEIGHTS):
        w = inp[name].astype(_jnp.float32)
        if MOMENT_SCALE is None:
            s = _jnp.sqrt(_jnp.mean(_jnp.square(w)) + 1e-30)
        else:
            s = MOMENT_SCALE[name]
        km, kv = _jax.random.split(_jax.random.fold_in(key, i + 1))
        out[name] = w
        out["m_" + name] = s * _jax.random.normal(km, w.shape, _jnp.float32)
        out["v_" + name] = (s * s) * _jax.random.uniform(kv, w.shape, _jnp.float32, 0.5, 1.5)
    if N_MICROBATCH > 1:
        for name, axis in PER_EXAMPLE_BATCH_AXIS.items():
            out[name] = _to_microbatches(out[name], axis)
    return {'x': out['x'], 'c': out['c'], 'w_ada': out['w_ada'], 'b_ada': out['b_ada'], 'norm1_gain': out['norm1_gain'], 'w_in': out['w_in'], 'conv_dw_w': out['conv_dw_w'], 'conv_dw_b': out['conv_dw_b'], 'conv_ln_g': out['conv_ln_g'], 'conv_ln_b': out['conv_ln_b'], 'gm_ln_g': out['gm_ln_g'], 'gm_ln_b': out['gm_ln_b'], 'gm_ws': out['gm_ws'], 'gm_bs': out['gm_bs'], 'mix_out_gain': out['mix_out_gain'], 'w_out': out['w_out'], 'norm2_gain': out['norm2_gain'], 'w_up': out['w_up'], 'ffn_dw_w': out['ffn_dw_w'], 'ffn_dw_b': out['ffn_dw_b'], 'w_down': out['w_down'], 'final_gain': out['final_gain'], 'loss_target': out['loss_target'], 'm_w_ada': out['m_w_ada'], 'm_b_ada': out['m_b_ada'], 'm_norm1_gain': out['m_norm1_gain'], 'm_w_in': out['m_w_in'], 'm_conv_dw_w': out['m_conv_dw_w'], 'm_conv_dw_b': out['m_conv_dw_b'], 'm_conv_ln_g': out['m_conv_ln_g'], 'm_conv_ln_b': out['m_conv_ln_b'], 'm_gm_ln_g': out['m_gm_ln_g'], 'm_gm_ln_b': out['m_gm_ln_b'], 'm_gm_ws': out['m_gm_ws'], 'm_gm_bs': out['m_gm_bs'], 'm_mix_out_gain': out['m_mix_out_gain'], 'm_w_out': out['m_w_out'], 'm_norm2_gain': out['m_norm2_gain'], 'm_w_up': out['m_w_up'], 'm_ffn_dw_w': out['m_ffn_dw_w'], 'm_ffn_dw_b': out['m_ffn_dw_b'], 'm_w_down': out['m_w_down'], 'm_final_gain': out['m_final_gain'], 'v_w_ada': out['v_w_ada'], 'v_b_ada': out['v_b_ada'], 'v_norm1_gain': out['v_norm1_gain'], 'v_w_in': out['v_w_in'], 'v_conv_dw_w': out['v_conv_dw_w'], 'v_conv_dw_b': out['v_conv_dw_b'], 'v_conv_ln_g': out['v_conv_ln_g'], 'v_conv_ln_b': out['v_conv_ln_b'], 'v_gm_ln_g': out['v_gm_ln_g'], 'v_gm_ln_b': out['v_gm_ln_b'], 'v_gm_ws': out['v_gm_ws'], 'v_gm_bs': out['v_gm_bs'], 'v_mix_out_gain': out['v_mix_out_gain'], 'v_w_out': out['v_w_out'], 'v_norm2_gain': out['v_norm2_gain'], 'v_w_up': out['v_w_up'], 'v_ffn_dw_w': out['v_ffn_dw_w'], 'v_ffn_dw_b': out['v_ffn_dw_b'], 'v_w_down': out['v_w_down'], 'v_final_gain': out['v_final_gain']}


def _loss(weights, diff, rest, loss_target):
    with _jax.named_scope("forward"):
        args = {**rest, TWIN_DIFF_INPUT: diff, **{k: w.astype(_WEIGHT_DTYPES[k]) for k, w in weights.items()}}
        y = _forward(args)
    with _jax.named_scope("loss_head"):
        err = _jnp.square(y.astype(_jnp.float32) - loss_target)
        return 0.5 * _jnp.sum(_jnp.mean(err, axis=-1)) if err.ndim else 0.5 * err


def _adamw(w, g, m, v):
    m = ADAM_B1 * m + (1.0 - ADAM_B1) * g
    v = ADAM_B2 * v + (1.0 - ADAM_B2) * _jnp.square(g)
    m_hat = m / (1.0 - ADAM_B1 ** ADAM_STEP)
    v_hat = v / (1.0 - ADAM_B2 ** ADAM_STEP)
    delta = -ADAM_LR * (m_hat / (_jnp.sqrt(v_hat) + ADAM_EPS) + ADAM_WD * w)
    return delta, m, v


def reference(x, c, w_ada, b_ada, norm1_gain, w_in, conv_dw_w, conv_dw_b, conv_ln_g, conv_ln_b, gm_ln_g, gm_ln_b, gm_ws, gm_bs, mix_out_gain, w_out, norm2_gain, w_up, ffn_dw_w, ffn_dw_b, w_down, final_gain, loss_target, m_w_ada, m_b_ada, m_norm1_gain, m_w_in, m_conv_dw_w, m_conv_dw_b, m_conv_ln_g, m_conv_ln_b, m_gm_ln_g, m_gm_ln_b, m_gm_ws, m_gm_bs, m_mix_out_gain, m_w_out, m_norm2_gain, m_w_up, m_ffn_dw_w, m_ffn_dw_b, m_w_down, m_final_gain, v_w_ada, v_b_ada, v_norm1_gain, v_w_in, v_conv_dw_w, v_conv_dw_b, v_conv_ln_g, v_conv_ln_b, v_gm_ln_g, v_gm_ln_b, v_gm_ws, v_gm_bs, v_mix_out_gain, v_w_out, v_norm2_gain, v_w_up, v_ffn_dw_w, v_ffn_dw_b, v_w_down, v_final_gain):
    given = dict(x=x, c=c, w_ada=w_ada, b_ada=b_ada, norm1_gain=norm1_gain, w_in=w_in, conv_dw_w=conv_dw_w, conv_dw_b=conv_dw_b, conv_ln_g=conv_ln_g, conv_ln_b=conv_ln_b, gm_ln_g=gm_ln_g, gm_ln_b=gm_ln_b, gm_ws=gm_ws, gm_bs=gm_bs, mix_out_gain=mix_out_gain, w_out=w_out, norm2_gain=norm2_gain, w_up=w_up, ffn_dw_w=ffn_dw_w, ffn_dw_b=ffn_dw_b, w_down=w_down, final_gain=final_gain, loss_target=loss_target, m_w_ada=m_w_ada, m_b_ada=m_b_ada, m_norm1_gain=m_norm1_gain, m_w_in=m_w_in, m_conv_dw_w=m_conv_dw_w, m_conv_dw_b=m_conv_dw_b, m_conv_ln_g=m_conv_ln_g, m_conv_ln_b=m_conv_ln_b, m_gm_ln_g=m_gm_ln_g, m_gm_ln_b=m_gm_ln_b, m_gm_ws=m_gm_ws, m_gm_bs=m_gm_bs, m_mix_out_gain=m_mix_out_gain, m_w_out=m_w_out, m_norm2_gain=m_norm2_gain, m_w_up=m_w_up, m_ffn_dw_w=m_ffn_dw_w, m_ffn_dw_b=m_ffn_dw_b, m_w_down=m_w_down, m_final_gain=m_final_gain, v_w_ada=v_w_ada, v_b_ada=v_b_ada, v_norm1_gain=v_norm1_gain, v_w_in=v_w_in, v_conv_dw_w=v_conv_dw_w, v_conv_dw_b=v_conv_dw_b, v_conv_ln_g=v_conv_ln_g, v_conv_ln_b=v_conv_ln_b, v_gm_ln_g=v_gm_ln_g, v_gm_ln_b=v_gm_ln_b, v_gm_ws=v_gm_ws, v_gm_bs=v_gm_bs, v_mix_out_gain=v_mix_out_gain, v_w_out=v_w_out, v_norm2_gain=v_norm2_gain, v_w_up=v_w_up, v_ffn_dw_w=v_ffn_dw_w, v_ffn_dw_b=v_ffn_dw_b, v_w_down=v_w_down, v_final_gain=v_final_gain)
    weights = {n: given[n] for n in TWIN_WEIGHTS}
    shared = {n: given[n] for n in SHARED_INPUTS}
    per_example = {n: given[n] for n in ['x', 'c']}
    grad_fn = _jax.value_and_grad(_loss, argnums=(0, 1))

    def one_microbatch(ex, loss_target):
        ex = dict(ex)
        diff = ex.pop(TWIN_DIFF_INPUT)
        return grad_fn(weights, diff, {**shared, **ex}, loss_target)

    if N_MICROBATCH == 1:
        loss, (grad_w, grad_x) = one_microbatch(per_example, given["loss_target"])
    else:
        def body(carry, xs):
            loss_sum, grad_sum = carry
            l_k, (gw_k, gx_k) = one_microbatch(xs[0], xs[1])
            with _jax.named_scope("update"):
                return (loss_sum + l_k, _jax.tree.map(_jnp.add, grad_sum, gw_k)), gx_k

        init = (_jnp.zeros((), _jnp.float32), _jax.tree.map(_jnp.zeros_like, weights))
        (loss, grad_w), grad_x = _jax.lax.scan(body, init, (per_example, given["loss_target"]))
    with _jax.named_scope("update"):
        delta_w, new_m, new_v = {}, {}, {}
        for n in TWIN_WEIGHTS:
            delta_w[n], new_m[n], new_v[n] = _adamw(weights[n], grad_w[n], given["m_" + n], given["v_" + n])
    return (loss, grad_x, *[grad_w[n] for n in TWIN_WEIGHTS], *[delta_w[n] for n in TWIN_WEIGHTS],
            *[new_m[n] for n in TWIN_WEIGHTS], *[new_v[n] for n in TWIN_WEIGHTS])
```

```python
import functools

import jax
import jax.numpy as jnp
from jax import lax
from jax.experimental import pallas as pl
from jax.experimental.pallas import tpu as pltpu

F32 = jnp.float32
BF16 = jnp.bfloat16

D_MODEL = 1024
D_HALF = 512
D_FF = 2816
CONV_K = 31
FFN_K = 3
CHUNK = 128
N_HEADS = 8
HEAD_DIM = 64
N_SHARD = 4
N_DEV = 8
RMS_EPS = 1e-6
LN_EPS = 1e-5
ADAM_LR, ADAM_B1, ADAM_B2, ADAM_EPS, ADAM_WD, ADAM_STEP = 0.001, 0.9, 0.999, 1e-08, 0.01, 10

TILE = 256
HALO = 32
FFN_HALO = 8
FFN_BLK = 256
UP_SHARD = 2 * D_FF // N_SHARD
VMEM_LIMIT_BYTES = 56 * 1024 * 1024

ANY = pl.BlockSpec(memory_space=pl.ANY)
NT_DIMS = (((1,), (1,)), ((), ()))
TN_DIMS = (((0,), (0,)), ((), ()))


def _full(shape):
    return pl.BlockSpec(shape, lambda i: (0,) * len(shape))


def _nn(a, b):
    return jnp.dot(a, b, preferred_element_type=F32)


def _nt(a, b):
    return lax.dot_general(a, b, NT_DIMS, preferred_element_type=F32)


def _tn(a, b):
    return lax.dot_general(a, b, TN_DIMS, preferred_element_type=F32)


def _colsum(a):
    return jnp.sum(a, axis=0, keepdims=True)


def _params(semantics=("arbitrary",)):
    return pltpu.CompilerParams(dimension_semantics=semantics, vmem_limit_bytes=VMEM_LIMIT_BYTES)


def _rms(v, gain):
    return v * lax.rsqrt(jnp.mean(v * v, axis=-1, keepdims=True) + RMS_EPS) * gain


def _layer_norm(v, gain, bias):
    mu = jnp.mean(v, axis=-1, keepdims=True)
    var = jnp.mean(jnp.square(v - mu), axis=-1, keepdims=True)
    return (v - mu) * lax.rsqrt(var + LN_EPS) * gain + bias


def _mod_norm(v, gain, scale, shift):
    return _rms(v, gain) * (1.0 + scale) + shift


def _conv_branch(a1, ln_g, ln_b, out_gain):
    a2 = _layer_norm(a1, ln_g, ln_b)
    return _rms(a2 * jax.nn.sigmoid(a2), out_gain)


def _gate_branch(gu, sp, out_gain):
    return _rms(jax.nn.gelu(gu) * sp, out_gain)


def _gv_norm(gv, ln_g, ln_b):
    return _layer_norm(jax.nn.gelu(gv), ln_g, ln_b)


def _head_pair_matmul(wp_ref, v):
    lane = lax.broadcasted_iota(jnp.int32, (CHUNK, CHUNK), 1)
    rows = []
    for n in range(v.shape[0] // CHUNK):
        cols = []
        for j in range(N_HEADS // 2):
            r = _nn(wp_ref[j], v[n * CHUNK:(n + 1) * CHUNK, j * CHUNK:(j + 1) * CHUNK])
            cols.append(jnp.where(lane < HEAD_DIM, r[:CHUNK], r[CHUNK:]))
        rows.append(jnp.concatenate(cols, axis=1))
    return jnp.concatenate(rows, axis=0)


def _tile_bias(bs, tokens):
    return jnp.concatenate([bs] * (tokens // CHUNK), axis=0)


def _fwd_mixer_up(x, vec, conv_w, wpair, bs_full, w_in_g, w_out_g, w_up_g, u_dtype):
    seq = x.shape[0]
    n_tiles = seq // TILE
    t = TILE
    names = ["norm1_gain", "sc1", "sh1", "gt1", "norm2_gain", "sc2", "sh2", "conv_dw_b", "conv_ln_g", "conv_ln_b",
             "gm_ln_g", "gm_ln_b", "mix_out_gain"]
    vecs = [vec[k] for k in names]

    def body(x_ref, g1, sc1, sh1, gt1, g2, sc2, sh2, cb, clg, clb, vg, vb, mg, cw, wp, bs, win_hbm, wout_hbm, wup_hbm,
             z_ref, a1_ref, sp_ref, y_ref, o1_ref, x2_ref, u_ref, win_v, wout_v, wup_v, cbuf, sem):
        i = pl.program_id(0)

        @pl.when(i == 0)
        def _():
            cps = [pltpu.make_async_copy(win_hbm, win_v, sem.at[0]),
                   pltpu.make_async_copy(wout_hbm, wout_v, sem.at[1]),
                   pltpu.make_async_copy(wup_hbm, wup_v, sem.at[2])]
            for cp in cps:
                cp.start()
            for cp in cps:
                cp.wait()
            cbuf[pl.ds(0, HALO), :] = jnp.zeros((HALO, D_HALF), F32)

        xv = x_ref[...]
        h1b = _mod_norm(xv, g1[...], sc1[...], sh1[...]).astype(BF16)
        zs = [_nn(h1b, win_v[k]) for k in range(N_SHARD)]
        for k in range(N_SHARD):
            z_ref[:, k * D_HALF:(k + 1) * D_HALF] = zs[k]
        ca, cg, gu, gv = zs
        cbuf[pl.ds(HALO, t), :] = ca * jax.nn.sigmoid(cg)
        a1 = jnp.zeros((t, D_HALF), F32) + cb[...]
        for k in range(CONV_K):
            a1 = a1 + cbuf[pl.ds(HALO - (CONV_K - 1) + k, t), :] * cw[pl.ds(k, 1), :]
        cbuf[pl.ds(0, HALO), :] = cbuf[pl.ds(t, HALO), :]
        a1_ref[...] = a1
        mgv = mg[...]
        ya = _conv_branch(a1, clg[...], clb[...], mgv[:, :D_HALF])
        gvn = _gv_norm(gv, vg[...], vb[...]).astype(BF16)
        sp = _head_pair_matmul(wp, gvn) + _tile_bias(bs[...], t)
        sp_ref[...] = sp
        yg = _gate_branch(gu, sp, mgv[:, D_HALF:])
        yb = jnp.concatenate([ya, yg], axis=1).astype(BF16)
        y_ref[...] = yb
        o1 = _nn(yb, wout_v[...])
        o1_ref[...] = o1
        x2 = xv + gt1[...] * o1
        x2_ref[...] = x2
        h2b = _mod_norm(x2, g2[...], sc2[...], sh2[...]).astype(BF16)
        for k in range(N_SHARD):
            u_ref[:, k * UP_SHARD:(k + 1) * UP_SHARD] = _nn(h2b, wup_v[k]).astype(u_ref.dtype)

    def row(width):
        return pl.BlockSpec((t, width), lambda i: (i, 0))

    out_shape = [jax.ShapeDtypeStruct((seq, 4 * D_HALF), F32), jax.ShapeDtypeStruct((seq, D_HALF), F32),
                 jax.ShapeDtypeStruct((seq, D_HALF), F32), jax.ShapeDtypeStruct((seq, D_MODEL), BF16),
                 jax.ShapeDtypeStruct((seq, D_MODEL), F32), jax.ShapeDtypeStruct((seq, D_MODEL), F32),
                 jax.ShapeDtypeStruct((seq, 2 * D_FF), u_dtype)]
    return pl.pallas_call(
        body, grid=(n_tiles,), name="fwd_mixer_up",
        in_specs=[row(D_MODEL)] + [_full(v.shape) for v in vecs]
        + [_full(conv_w.shape), _full(wpair.shape), _full(bs_full.shape), ANY, ANY, ANY],
        out_specs=[row(4 * D_HALF), row(D_HALF), row(D_HALF), row(D_MODEL), row(D_MODEL), row(D_MODEL), row(2 * D_FF)],
        out_shape=out_shape,
        scratch_shapes=[pltpu.VMEM(w_in_g.shape, BF16), pltpu.VMEM(w_out_g.shape, BF16), pltpu.VMEM(w_up_g.shape, BF16),
                        pltpu.VMEM((t + HALO, D_HALF), F32), pltpu.SemaphoreType.DMA((3,))],
        compiler_params=_params(),
    )(x, *vecs, conv_w, wpair, bs_full, w_in_g, w_out_g, w_up_g)


def _ffn_tail(u, x2, target, ffn_w, ffn_b, gt2, final_gain, w_down_g):
    seq = x2.shape[0]
    n_tiles = seq // TILE
    t = TILE
    n_blk = D_FF // FFN_BLK
    halo_rows = 16 if u.dtype == BF16 else 8
    inv_d = 1.0 / D_MODEL

    def final_norm(x3, gain):
        return _rms(x3, gain)

    def body(u_ref, uh_ref, x2_ref, tgt_ref, fw, fb, gt2_ref, fg, wd_hbm,
             du_ref, dx3_ref, dfw_ref, dfb_ref, dfg_ref, dgt2_ref, loss_ref, dwd_hbm,
             wd_v, dwd_acc, bufv, bufg, dbuf, carry, sem):
        i = pl.program_id(0)
        tile = n_tiles - 1 - i

        @pl.when(i == 0)
        def _():
            cp = pltpu.make_async_copy(wd_hbm, wd_v, sem.at[0])
            cp.start()
            cp.wait()
            dwd_acc[...] = jnp.zeros_like(dwd_acc)
            carry[...] = jnp.zeros_like(carry)
            dfw_ref[...] = jnp.zeros_like(dfw_ref)
            dfb_ref[...] = jnp.zeros_like(dfb_ref)
            dfg_ref[...] = jnp.zeros_like(dfg_ref)
            dgt2_ref[...] = jnp.zeros_like(dgt2_ref)
            loss_ref[...] = jnp.zeros_like(loss_ref)

        def cols_of(j):
            return pl.ds(j * FFN_BLK, FFN_BLK), pl.ds(D_FF + j * FFN_BLK, FFN_BLK)

        def fill(buf, cols):
            prev = uh_ref[pl.ds(halo_rows - FFN_HALO, FFN_HALO), cols].astype(F32)
            buf[pl.ds(0, FFN_HALO), :] = jnp.where(tile > 0, prev, 0.0)
            buf[pl.ds(FFN_HALO, t), :] = u_ref[:, cols].astype(F32)

        def conv(buf, cols):
            acc = jnp.zeros((t, FFN_BLK), F32) + fb[:, cols]
            for k in range(FFN_K):
                acc = acc + buf[pl.ds(FFN_HALO - (FFN_K - 1) + k, t), :] * fw[pl.ds(k, 1), cols]
            return acc

        o2 = jnp.zeros((t, D_MODEL), F32)
        for j in range(n_blk):
            cv, cg = cols_of(j)
            fill(bufv, cv)
            fill(bufg, cg)
            val, gate = conv(bufv, cv), conv(bufg, cg)
            f = gate * jax.nn.sigmoid(gate) * val
            o2 = o2 + _nn(f.astype(BF16), wd_v[pl.ds(j * FFN_BLK, FFN_BLK), :])

        gt2v = gt2_ref[...]
        x3 = x2_ref[...] + gt2v * o2
        out, out_vjp = jax.vjp(final_norm, x3, fg[...])
        diff = out - tgt_ref[...]
        loss_ref[...] += jnp.zeros_like(loss_ref) + 0.5 * inv_d * jnp.sum(diff * diff)
        dx3, dfg = out_vjp(diff * inv_d)
        dfg_ref[...] += dfg
        dgt2_ref[...] += _colsum(dx3 * o2)
        dx3_ref[...] = dx3
        do2b = (gt2v * dx3).astype(BF16)

        for j in range(n_blk):
            cv, cg = cols_of(j)
            rows = pl.ds(j * FFN_BLK, FFN_BLK)
            fill(bufv, cv)
            fill(bufg, cg)
            val, gate = conv(bufv, cv), conv(bufg, cg)
            df = _nt(do2b, wd_v[rows, :])
            sig = jax.nn.sigmoid(gate)
            sil = gate * sig
            dval = df * sil
            dgate = df * val * (sig * (1.0 + gate * (1.0 - sig)))
            dwd_acc[rows, :] += _tn((sil * val).astype(BF16), do2b)
            for dd, buf, cols in ((dval, bufv, cv), (dgate, bufg, cg)):
                dfb_ref[:, cols] += _colsum(dd)
                for k in range(FFN_K):
                    dfw_ref[pl.ds(k, 1), cols] += _colsum(dd * buf[pl.ds(FFN_HALO - (FFN_K - 1) + k, t), :])
                dbuf[pl.ds(0, t), :] = dd
                dbuf[pl.ds(t, FFN_HALO), :] = carry[:, cols]
                du = jnp.zeros((t, FFN_BLK), F32)
                for k in range(FFN_K):
                    du = du + dbuf[pl.ds(FFN_K - 1 - k, t), :] * fw[pl.ds(k, 1), cols]
                carry[:, cols] = dbuf[pl.ds(0, FFN_HALO), :]
                du_ref[:, cols] = du.astype(BF16)

        @pl.when(i == n_tiles - 1)
        def _():
            cp = pltpu.make_async_copy(dwd_acc, dwd_hbm, sem.at[1])
            cp.start()
            cp.wait()

    def rev(width):
        return pl.BlockSpec((t, width), lambda i: (n_tiles - 1 - i, 0))

    halo_spec = pl.BlockSpec(
        (halo_rows, 2 * D_FF), lambda i: (jnp.maximum((n_tiles - 1 - i) * (t // halo_rows) - 1, 0), 0))
    out_shape = [jax.ShapeDtypeStruct((seq, 2 * D_FF), BF16), jax.ShapeDtypeStruct((seq, D_MODEL), F32),
                 jax.ShapeDtypeStruct((FFN_K, 2 * D_FF), F32), jax.ShapeDtypeStruct((1, 2 * D_FF), F32),
                 jax.ShapeDtypeStruct((1, D_MODEL), F32), jax.ShapeDtypeStruct((1, D_MODEL), F32),
                 jax.ShapeDtypeStruct((1, 128), F32), jax.ShapeDtypeStruct((D_FF, D_MODEL), F32)]
    return pl.pallas_call(
        body, grid=(n_tiles,), name="ffn_tail",
        in_specs=[rev(2 * D_FF), halo_spec, rev(D_MODEL), rev(D_MODEL), _full(ffn_w.shape), _full(ffn_b.shape),
                  _full(gt2.shape), _full(final_gain.shape), ANY],
        out_specs=[rev(2 * D_FF), rev(D_MODEL), _full((FFN_K, 2 * D_FF)), _full((1, 2 * D_FF)), _full((1, D_MODEL)),
                   _full((1, D_MODEL)), _full((1, 128)), ANY],
        out_shape=out_shape,
        scratch_shapes=[pltpu.VMEM((D_FF, D_MODEL), BF16), pltpu.VMEM((D_FF, D_MODEL), F32),
                        pltpu.VMEM((t + FFN_HALO, FFN_BLK), F32), pltpu.VMEM((t + FFN_HALO, FFN_BLK), F32),
                        pltpu.VMEM((t + FFN_HALO, FFN_BLK), F32), pltpu.VMEM((FFN_HALO, 2 * D_FF), F32),
                        pltpu.SemaphoreType.DMA((2,))],
        compiler_params=_params(),
    )(u, u, x2, target, ffn_w, ffn_b, gt2, final_gain, w_down_g)


def _bwd_up(du, x2, dx3, norm2_gain, sc2, sh2, w_up_g):
    seq = x2.shape[0]
    n_tiles = seq // TILE
    t = TILE

    def body(du_ref, x2_ref, dx3_ref, g2, sc2_ref, sh2_ref, wup_hbm, dx2_ref, dg2_ref, dsc2_ref, dsh2_ref, dwup_hbm,
             wup_v, dwup_acc, sem):
        i = pl.program_id(0)

        @pl.when(i == 0)
        def _():
            cp = pltpu.make_async_copy(wup_hbm, wup_v, sem.at[0])
            cp.start()
            cp.wait()
            dwup_acc[...] = jnp.zeros_like(dwup_acc)
            dg2_ref[...] = jnp.zeros_like(dg2_ref)
            dsc2_ref[...] = jnp.zeros_like(dsc2_ref)
            dsh2_ref[...] = jnp.zeros_like(dsh2_ref)

        h2, h2_vjp = jax.vjp(_mod_norm, x2_ref[...], g2[...], sc2_ref[...], sh2_ref[...])
        h2b = h2.astype(BF16)
        dh2 = jnp.zeros((t, D_MODEL), F32)
        for k in range(N_SHARD):
            dub = du_ref[:, k * UP_SHARD:(k + 1) * UP_SHARD]
            dh2 = dh2 + _nt(dub, wup_v[k])
            dwup_acc[k] += _tn(h2b, dub)
        dx2, dg2, dsc2, dsh2 = h2_vjp(dh2)
        dx2_ref[...] = dx3_ref[...] + dx2
        dg2_ref[...] += dg2
        dsc2_ref[...] += dsc2
        dsh2_ref[...] += dsh2

        @pl.when(i == n_tiles - 1)
        def _():
            cp = pltpu.make_async_copy(dwup_acc, dwup_hbm, sem.at[1])
            cp.start()
            cp.wait()

    def row(width):
        return pl.BlockSpec((t, width), lambda i: (i, 0))

    vec = jax.ShapeDtypeStruct((1, D_MODEL), F32)
    return pl.pallas_call(
        body, grid=(n_tiles,), name="bwd_up",
        in_specs=[row(2 * D_FF), row(D_MODEL), row(D_MODEL), _full((1, D_MODEL)), _full((1, D_MODEL)),
                  _full((1, D_MODEL)), ANY],
        out_specs=[row(D_MODEL), _full((1, D_MODEL)), _full((1, D_MODEL)), _full((1, D_MODEL)), ANY],
        out_shape=[jax.ShapeDtypeStruct((seq, D_MODEL), F32), vec, vec, vec,
                   jax.ShapeDtypeStruct(w_up_g.shape, F32)],
        scratch_shapes=[pltpu.VMEM(w_up_g.shape, BF16), pltpu.VMEM(w_up_g.shape, F32), pltpu.SemaphoreType.DMA((2,))],
        compiler_params=_params(),
    )(du, x2, dx3, norm2_gain, sc2, sh2, w_up_g)


def _bwd_mixer(dx2, x, z, a1, sp, yb, o1, vec, conv_w, wpair, wpair_t, causal_mask, w_in_g, w_out_g):
    seq = x.shape[0]
    n_tiles = seq // TILE
    t = TILE
    names = ["norm1_gain", "sc1", "sh1", "gt1", "conv_ln_g", "conv_ln_b", "gm_ln_g", "gm_ln_b", "mix_out_gain"]
    vecs = [vec[k] for k in names]

    def body(dx2_ref, x_ref, z_ref, zh_ref, a1_ref, sp_ref, y_ref, o1_ref, g1, sc1, sh1, gt1, clg, clb, vg, vb, mg,
             cw, wp, wpt, mask_ref, win_hbm, wout_hbm,
             gx_ref, dg1_ref, dsc1_ref, dsh1_ref, dgt1_ref, dcw_ref, dcb_ref, dclg_ref, dclb_ref, dvg_ref, dvb_ref,
             dmg_ref, dws_ref, dbs_ref, dwin_hbm, dwout_hbm,
             win_v, wout_v, dwin_acc, dwout_acc, abuf, dbuf, dbs_acc, sem):
        i = pl.program_id(0)
        tile = n_tiles - 1 - i
        small = [dg1_ref, dsc1_ref, dsh1_ref, dgt1_ref, dcw_ref, dcb_ref, dclg_ref, dclb_ref, dvg_ref, dvb_ref,
                 dmg_ref, dws_ref, dbs_acc]

        @pl.when(i == 0)
        def _():
            cps = [pltpu.make_async_copy(win_hbm, win_v, sem.at[0]),
                   pltpu.make_async_copy(wout_hbm, wout_v, sem.at[1])]
            for cp in cps:
                cp.start()
            for cp in cps:
                cp.wait()
            dwin_acc[...] = jnp.zeros_like(dwin_acc)
            dwout_acc[...] = jnp.zeros_like(dwout_acc)
            dbuf[pl.ds(t, HALO), :] = jnp.zeros((HALO, D_HALF), F32)
            for ref in small:
                ref[...] = jnp.zeros_like(ref)

        dx2v = dx2_ref[...]
        gt1v = gt1[...]
        dgt1_ref[...] += _colsum(dx2v * o1_ref[...])
        do1b = (gt1v * dx2v).astype(BF16)
        dy = _nt(do1b, wout_v[...])
        dwout_acc[...] += _tn(y_ref[...], do1b)

        mgv = mg[...]
        _, conv_vjp = jax.vjp(_conv_branch, a1_ref[...], clg[...], clb[...], mgv[:, :D_HALF])
        da1, dclg, dclb, dmg_a = conv_vjp(dy[:, :D_HALF])
        dclg_ref[...] += dclg
        dclb_ref[...] += dclb
        gu = z_ref[:, 2 * D_HALF:3 * D_HALF]
        gv = z_ref[:, 3 * D_HALF:]
        spv = sp_ref[...]
        _, gate_vjp = jax.vjp(_gate_branch, gu, spv, mgv[:, D_HALF:])
        dgu, dsp, dmg_g = gate_vjp(dy[:, D_HALF:])
        dmg_ref[...] += jnp.concatenate([dmg_a, dmg_g], axis=1)
        gvn, gv_vjp = jax.vjp(_gv_norm, gv, vg[...], vb[...])
        gvnb = gvn.astype(BF16)
        dspb = dsp.astype(BF16)
        dgvn = _head_pair_matmul(wpt, dspb)
        dgv, dvg, dvb = gv_vjp(dgvn)
        dvg_ref[...] += dvg
        dvb_ref[...] += dvb
        lane = lax.broadcasted_iota(jnp.int32, (CHUNK, CHUNK), 1)
        dbs = jnp.zeros((CHUNK, D_HALF), F32)
        for n in range(t // CHUNK):
            rows = slice(n * CHUNK, (n + 1) * CHUNK)
            dbs = dbs + dsp[rows, :]
            for j in range(N_HEADS // 2):
                cols = slice(j * CHUNK, (j + 1) * CHUNK)
                blk = dspb[rows, cols]
                zero = jnp.zeros_like(blk)
                vblk = gvnb[rows, cols]
                dws_ref[2 * j] += _nt(jnp.where(lane < HEAD_DIM, blk, zero), vblk)
                dws_ref[2 * j + 1] += _nt(jnp.where(lane < HEAD_DIM, zero, blk), vblk)
        dbs_acc[...] += dbs

        ca = z_ref[:, :D_HALF]
        cg = z_ref[:, D_HALF:2 * D_HALF]
        sig = jax.nn.sigmoid(cg)
        zh = zh_ref[...]
        a0_prev = zh[:, :D_HALF] * jax.nn.sigmoid(zh[:, D_HALF:])
        abuf[pl.ds(0, HALO), :] = jnp.where(tile > 0, a0_prev, 0.0)
        abuf[pl.ds(HALO, t), :] = ca * sig
        dbuf[pl.ds(0, t), :] = da1
        dcb_ref[...] += _colsum(da1)
        da0 = jnp.zeros((t, D_HALF), F32)
        for k in range(CONV_K):
            da0 = da0 + dbuf[pl.ds(CONV_K - 1 - k, t), :] * cw[pl.ds(k, 1), :]
            dcw_ref[pl.ds(k, 1), :] += _colsum(da1 * abuf[pl.ds(HALO - (CONV_K - 1) + k, t), :])
        dbuf[pl.ds(t, HALO), :] = dbuf[pl.ds(0, HALO), :]
        dca = da0 * sig
        dcg = da0 * ca * sig * (1.0 - sig)

        h1, h1_vjp = jax.vjp(_mod_norm, x_ref[...], g1[...], sc1[...], sh1[...])
        h1b = h1.astype(BF16)
        dh1 = jnp.zeros((t, D_MODEL), F32)
        for k, dzk in enumerate((dca, dcg, dgu, dgv)):
            dzb = dzk.astype(BF16)
            dh1 = dh1 + _nt(dzb, win_v[k])
            dwin_acc[k] += _tn(h1b, dzb)
        dx, dg1, dsc1, dsh1 = h1_vjp(dh1)
        gx_ref[...] = dx2v + dx
        dg1_ref[...] += dg1
        dsc1_ref[...] += dsc1
        dsh1_ref[...] += dsh1

        @pl.when(i == n_tiles - 1)
        def _():
            for h in range(N_HEADS):
                dws_ref[h] = dws_ref[h] * mask_ref[...]
            head_of_lane = lax.broadcasted_iota(jnp.int32, (N_HEADS, D_HALF), 1) // HEAD_DIM
            pick = (head_of_lane == lax.broadcasted_iota(jnp.int32, (N_HEADS, D_HALF), 0)).astype(F32)
            dbs_ref[...] = lax.dot_general(pick, dbs_acc[...], NT_DIMS, precision=lax.Precision.HIGHEST,
                                           preferred_element_type=F32)
            cps = [pltpu.make_async_copy(dwin_acc, dwin_hbm, sem.at[2]),
                   pltpu.make_async_copy(dwout_acc, dwout_hbm, sem.at[3])]
            for cp in cps:
                cp.start()
            for cp in cps:
                cp.wait()

    def rev(width):
        return pl.BlockSpec((t, width), lambda i: (n_tiles - 1 - i, 0))

    halo_spec = pl.BlockSpec((HALO, 2 * D_HALF), lambda i: (jnp.maximum((n_tiles - 1 - i) * (t // HALO) - 1, 0), 0))
    v1024 = jax.ShapeDtypeStruct((1, D_MODEL), F32)
    v512 = jax.ShapeDtypeStruct((1, D_HALF), F32)
    small_shapes = [v1024, v1024, v1024, v1024, jax.ShapeDtypeStruct((CONV_K, D_HALF), F32), v512, v512, v512, v512,
                    v512, v1024, jax.ShapeDtypeStruct((N_HEADS, CHUNK, CHUNK), F32),
                    jax.ShapeDtypeStruct((N_HEADS, CHUNK), F32)]
    return pl.pallas_call(
        body, grid=(n_tiles,), name="bwd_mixer",
        in_specs=[rev(D_MODEL), rev(D_MODEL), rev(4 * D_HALF), halo_spec, rev(D_HALF), rev(D_HALF), rev(D_MODEL),
                  rev(D_MODEL)] + [_full(v.shape) for v in vecs]
        + [_full(conv_w.shape), _full(wpair.shape), _full(wpair_t.shape), _full(causal_mask.shape), ANY, ANY],
        out_specs=[rev(D_MODEL)] + [_full(s.shape) for s in small_shapes] + [ANY, ANY],
        out_shape=[jax.ShapeDtypeStruct((seq, D_MODEL), F32)] + small_shapes
        + [jax.ShapeDtypeStruct(w_in_g.shape, F32), jax.ShapeDtypeStruct(w_out_g.shape, F32)],
        scratch_shapes=[pltpu.VMEM(w_in_g.shape, BF16), pltpu.VMEM(w_out_g.shape, BF16),
                        pltpu.VMEM(w_in_g.shape, F32), pltpu.VMEM(w_out_g.shape, F32),
                        pltpu.VMEM((t + HALO, D_HALF), F32), pltpu.VMEM((t + HALO, D_HALF), F32),
                        pltpu.VMEM((CHUNK, D_HALF), F32), pltpu.SemaphoreType.DMA((4,))],
        compiler_params=_params(),
    )(dx2, x, z, z, a1, sp, yb, o1, *vecs, conv_w, wpair, wpair_t, causal_mask, w_in_g, w_out_g)


def _gmlp_operands(gm_ws, gm_bs):
    mask = jnp.tril(jnp.ones((CHUNK, CHUNK), F32))
    ws = gm_ws * mask[None]
    wpair = ws.reshape(N_HEADS // 2, 2 * CHUNK, CHUNK).astype(BF16)
    wpair_t = jnp.swapaxes(ws, 1, 2).reshape(N_HEADS // 2, 2 * CHUNK, CHUNK).astype(BF16)
    bs_full = jnp.repeat(jnp.transpose(gm_bs), HEAD_DIM, axis=1)
    return wpair, wpair_t, bs_full, mask


def _local_step(x, target, mod, p, w_in_g, w_out_g, w_up_g, w_down_g, u_dtype=F32):
    sh1, sc1, gt1, sh2, sc2, gt2 = [mod[:, k * D_MODEL:(k + 1) * D_MODEL] for k in range(6)]
    vec = dict(p, sh1=sh1, sc1=sc1, gt1=gt1, sh2=sh2, sc2=sc2, gt2=gt2)
    wpair, wpair_t, bs_full, mask = _gmlp_operands(p["gm_ws"], p["gm_bs"])

    z, a1, sp, yb, o1, x2, u = _fwd_mixer_up(x, vec, p["conv_dw_w"], wpair, bs_full, w_in_g, w_out_g, w_up_g, u_dtype)
    du, dx3, d_ffn_w, d_ffn_b, d_fg, d_gt2, loss, d_wd = _ffn_tail(
        u, x2, target, p["ffn_dw_w"], p["ffn_dw_b"], gt2, p["final_gain"], w_down_g)
    dx2, d_g2, d_sc2, d_sh2, d_wup = _bwd_up(du, x2, dx3, p["norm2_gain"], sc2, sh2, w_up_g)
    (gx, d_g1, d_sc1, d_sh1, d_gt1, d_cw, d_cb, d_clg, d_clb, d_vg, d_vb, d_mg, d_ws, d_bs, d_win,
     d_wout) = _bwd_mixer(dx2, x, z, a1, sp, yb, o1, vec, p["conv_dw_w"], wpair, wpair_t, mask, w_in_g, w_out_g)
    d_mod = jnp.concatenate([d_sh1, d_sc1, d_gt1, d_sh2, d_sc2, d_gt2], axis=1)
    grads = dict(norm1_gain=d_g1, conv_dw_w=d_cw, conv_dw_b=d_cb, conv_ln_g=d_clg, conv_ln_b=d_clb, gm_ln_g=d_vg,
                 gm_ln_b=d_vb, gm_ws=d_ws, gm_bs=d_bs, mix_out_gain=d_mg, norm2_gain=d_g2, ffn_dw_w=d_ffn_w,
                 ffn_dw_b=d_ffn_b, final_gain=d_fg, w_in=d_win, w_out=d_wout, w_up=d_wup, w_down=d_wd)
    return gx, grads, d_mod, loss


MESH = pl.DeviceIdType.MESH
VMEM_SPEC = pl.BlockSpec(memory_space=pltpu.VMEM)
PEER_FLIPS = [(a, b, d) for a in (0, 1) for b in (0, 1) for d in (0, 1)][1:]
CHIP_FLIPS = [(1, 0), (0, 1), (1, 1)]


def _coords():
    return lax.axis_index("x"), lax.axis_index("y"), lax.axis_index("c")


def _flip(v, bit):
    return 1 - v if bit else v


def _rows8(block):
    return pl.ds(pl.multiple_of(8 * block, 8), 8)


def _ada_mod(c_row, w_ada_sh, b_ada_sh):
    cols = w_ada_sh.shape[1]

    def body(c_ref, w_ref, b_ref, call_ref, mod_ref, cpad, modall, send_sems, recv_sems):
        x, y, c = _coords()
        me = 4 * x + 2 * y + c
        cpad[...] = jnp.zeros_like(cpad)
        cpad[pl.ds(0, 1), :] = c_ref[...]

        def gather_copy(j, flip):
            peer = (_flip(x, flip[0]), _flip(y, flip[1]), _flip(c, flip[2]))
            return pltpu.make_async_remote_copy(
                src_ref=cpad, dst_ref=call_ref.at[_rows8(me)], send_sem=send_sems.at[j], recv_sem=recv_sems.at[j],
                device_id=peer, device_id_type=MESH)

        copies = [gather_copy(j, f) for j, f in enumerate(PEER_FLIPS)]
        for cp in copies:
            cp.start()
        call_ref[_rows8(me), :] = cpad[...]
        for cp in copies:
            cp.wait_recv()
        for cp in copies:
            cp.wait_send()
        cv = call_ref[...]
        c_act = (cv * jax.nn.sigmoid(cv)).astype(BF16)
        modall[...] = _nn(c_act, w_ref[...].astype(BF16)) + b_ref[...]

        slot = _rows8(2 * x + y)

        def piece_copy(j, flip):
            tx, ty = _flip(x, flip[0]), _flip(y, flip[1])
            return pltpu.make_async_remote_copy(
                src_ref=modall.at[_rows8(4 * tx + 2 * ty + c)], dst_ref=mod_ref.at[slot],
                send_sem=send_sems.at[len(PEER_FLIPS) + j], recv_sem=recv_sems.at[len(PEER_FLIPS) + j],
                device_id=(tx, ty, c), device_id_type=MESH)

        pieces = [piece_copy(j, f) for j, f in enumerate(CHIP_FLIPS)]
        for cp in pieces:
            cp.start()
        mod_ref[slot, :] = modall[_rows8(me), :]
        for cp in pieces:
            cp.wait_recv()
        for cp in pieces:
            cp.wait_send()

    n_sem = len(PEER_FLIPS) + len(CHIP_FLIPS)
    return pl.pallas_call(
        body, name="ada_mod",
        in_specs=[VMEM_SPEC, VMEM_SPEC, VMEM_SPEC], out_specs=[VMEM_SPEC, VMEM_SPEC],
        out_shape=[jax.ShapeDtypeStruct((8 * N_DEV, D_MODEL), F32), jax.ShapeDtypeStruct((8 * N_SHARD, cols), F32)],
        scratch_shapes=[pltpu.VMEM((8, D_MODEL), F32), pltpu.VMEM((8 * N_DEV, cols), F32),
                        pltpu.SemaphoreType.DMA((n_sem,)), pltpu.SemaphoreType.DMA((n_sem,))],
        compiler_params=pltpu.CompilerParams(vmem_limit_bytes=VMEM_LIMIT_BYTES),
    )(c_row, w_ada_sh, b_ada_sh)


def _gather_weights(shards, filters):
    n = len(shards)
    nf = len(filters)

    def body(*refs):
        ins, f_ins = refs[:n], refs[n:n + nf]
        outs, f_outs = refs[n + nf:2 * n + nf], refs[2 * n + nf:2 * (n + nf)]
        stage = refs[2 * (n + nf):3 * n + 2 * nf]
        send_sems, recv_sems, local_sems, f_send_sems, f_recv_sems = refs[3 * n + 2 * nf:]
        x, y, c = _coords()
        k = 2 * x + y
        sibling = (x, y, 1 - c)

        def filter_copy(w, j, slot):
            tx, ty = _flip(x, CHIP_FLIPS[j][0]), _flip(y, CHIP_FLIPS[j][1])
            return pltpu.make_async_remote_copy(
                src_ref=f_ins[w], dst_ref=f_outs[w].at[slot], send_sem=f_send_sems.at[w, j],
                recv_sem=f_recv_sems.at[w, j], device_id=(tx, ty, c), device_id_type=MESH)

        def half(w, which):
            h = shards[w].shape[0] // 2
            return pl.ds(pl.multiple_of(which * h, 16), h)

        def ici_copy(w, j, src, slot):
            tx, ty = _flip(x, CHIP_FLIPS[j][0]), _flip(y, CHIP_FLIPS[j][1])
            return pltpu.make_async_remote_copy(
                src_ref=src, dst_ref=outs[w].at[slot, half(w, c)], send_sem=send_sems.at[w, j],
                recv_sem=recv_sems.at[w, j], device_id=(tx, ty, c), device_id_type=MESH)

        def d2d_copy(w, j, slot, which):
            rows = outs[w].at[slot, half(w, which)]
            return pltpu.make_async_remote_copy(
                src_ref=rows, dst_ref=rows, send_sem=send_sems.at[w, len(CHIP_FLIPS) + j],
                recv_sem=recv_sems.at[w, len(CHIP_FLIPS) + j], device_id=sibling, device_id_type=MESH)

        def chip_of(j):
            return 2 * _flip(x, CHIP_FLIPS[j][0]) + _flip(y, CHIP_FLIPS[j][1])

        local, first, passed = [], [], []
        for w in range(nf):
            local.append(pltpu.make_async_copy(f_ins[w], f_outs[w].at[k], local_sems.at[n + w]))
            local[-1].start()
            for j in range(len(CHIP_FLIPS)):
                first.append(filter_copy(w, j, k))
                first[-1].start()
        for w in range(n):
            stage[w][...] = ins[w][...].astype(BF16)
            local.append(pltpu.make_async_copy(stage[w], outs[w].at[k], local_sems.at[w]))
            local[-1].start()
            for j in range(len(CHIP_FLIPS)):
                first.append(ici_copy(w, j, stage[w].at[half(w, c)], k))
                first[-1].start()
        for w in range(nf):
            for j in range(len(CHIP_FLIPS)):
                filter_copy(w, j, chip_of(j)).wait_recv()
        for w in range(n):
            for j in range(len(CHIP_FLIPS)):
                ici_copy(w, j, stage[w].at[half(w, c)], chip_of(j)).wait_recv()
                passed.append(d2d_copy(w, j, chip_of(j), c))
                passed[-1].start()
        for w in range(n):
            for j in range(len(CHIP_FLIPS)):
                d2d_copy(w, j, chip_of(j), 1 - c).wait_recv()
        for cp in first + passed:
            cp.wait_send()
        for cp in local:
            cp.wait()

    sem_shape = (n, 2 * len(CHIP_FLIPS))
    f_sem_shape = (nf, len(CHIP_FLIPS))
    outs = pl.pallas_call(
        body, name="gather_weights",
        in_specs=[VMEM_SPEC] * (n + nf), out_specs=[ANY] * (n + nf),
        out_shape=[jax.ShapeDtypeStruct((N_SHARD,) + s.shape, BF16) for s in shards]
        + [jax.ShapeDtypeStruct((N_SHARD,) + s.shape, F32) for s in filters],
        scratch_shapes=[pltpu.VMEM(s.shape, BF16) for s in shards]
        + [pltpu.SemaphoreType.DMA(sem_shape), pltpu.SemaphoreType.DMA(sem_shape), pltpu.SemaphoreType.DMA((n + nf,)),
           pltpu.SemaphoreType.DMA(f_sem_shape), pltpu.SemaphoreType.DMA(f_sem_shape)],
        compiler_params=pltpu.CompilerParams(vmem_limit_bytes=VMEM_LIMIT_BYTES),
    )(*shards, *filters)
    return outs[:n], outs[n:]


def _reduce_exchange(axis, big, small):
    n = len(big)

    def parts(ref, x, y, c):
        lead, rows = ref.shape[0], ref.shape[1]
        if axis == "c":
            h = rows // 2
            return (ref.at[:, pl.ds(pl.multiple_of((1 - c) * h, 8), h), :],
                    ref.at[:, pl.ds(pl.multiple_of(c * h, 8), h), :])
        own = x if axis == "x" else y
        h = lead // 2
        return ref.at[pl.ds((1 - own) * h, h)], ref.at[pl.ds(own * h, h)]

    def out_shape_of(a):
        lead, rows, cols = a.shape
        return (lead, rows // 2, cols) if axis == "c" else (lead // 2, rows, cols)

    def body(*refs):
        ins, small_ref = refs[:n], refs[n]
        keeps, recvs, small_recv = refs[n + 1:2 * n + 1], refs[2 * n + 1:3 * n + 1], refs[3 * n + 1]
        send_sems, recv_sems, local_sems = refs[3 * n + 2:]
        x, y, c = _coords()
        partner = (_flip(x, axis == "x"), _flip(y, axis == "y"), _flip(c, axis == "c"))
        copies = []
        for idx in range(n):
            send_part, keep_part = parts(ins[idx], x, y, c)
            copies.append(pltpu.make_async_remote_copy(
                src_ref=send_part, dst_ref=recvs[idx], send_sem=send_sems.at[idx], recv_sem=recv_sems.at[idx],
                device_id=partner, device_id_type=MESH))
            copies.append(pltpu.make_async_copy(keep_part, keeps[idx], local_sems.at[idx]))
        copies.append(pltpu.make_async_remote_copy(
            src_ref=small_ref, dst_ref=small_recv, send_sem=send_sems.at[n], recv_sem=recv_sems.at[n],
            device_id=partner, device_id_type=MESH))
        for cp in copies:
            cp.start()
        for cp in copies:
            cp.wait()

    part_shapes = [jax.ShapeDtypeStruct(out_shape_of(a), F32) for a in big]
    outs = pl.pallas_call(
        body, name="reduce_exchange_" + axis,
        in_specs=[ANY] * (n + 1), out_specs=[ANY] * (2 * n + 1),
        out_shape=part_shapes + part_shapes + [jax.ShapeDtypeStruct(small.shape, F32)],
        scratch_shapes=[pltpu.SemaphoreType.DMA((n + 1,)), pltpu.SemaphoreType.DMA((n + 1,)),
                        pltpu.SemaphoreType.DMA((n,))],
    )(*big, small)
    return outs[:n], outs[n:2 * n], outs[2 * n]


ADD_CHUNKS = 4


def _add_pairs(name, lhs, rhs):
    n = len(lhs)

    def body(*refs):
        for idx in range(n):
            refs[2 * n + idx][...] = refs[idx][...] + refs[n + idx][...]

    def spec(a):
        block = a.shape[:-2] + (a.shape[-2] // ADD_CHUNKS, a.shape[-1])
        lead = len(a.shape) - 2
        return pl.BlockSpec(block, lambda i: (0,) * lead + (i, 0))

    specs = [spec(a) for a in lhs]
    return pl.pallas_call(
        body, grid=(ADD_CHUNKS,), name=name,
        in_specs=specs + specs, out_specs=specs,
        out_shape=[jax.ShapeDtypeStruct(a.shape, F32) for a in lhs],
        compiler_params=_params(),
    )(*lhs, *rhs)


def _allgather_halves(halves):
    n = len(halves)

    def body(*refs):
        ins, outs = refs[:n], refs[n:2 * n]
        send_sems, recv_sems, local_sems = refs[2 * n:]
        x, y, c = _coords()
        copies = []
        for idx in range(n):
            mine = outs[idx].at[pl.ds(c, 1)]
            copies.append(pltpu.make_async_remote_copy(
                src_ref=ins[idx], dst_ref=mine, send_sem=send_sems.at[idx], recv_sem=recv_sems.at[idx],
                device_id=(x, y, 1 - c), device_id_type=MESH))
            copies.append(pltpu.make_async_copy(ins[idx], mine, local_sems.at[idx]))
        for cp in copies:
            cp.start()
        for cp in copies:
            cp.wait()

    return pl.pallas_call(
        body, name="allgather_halves",
        in_specs=[ANY] * n, out_specs=[ANY] * n,
        out_shape=[jax.ShapeDtypeStruct((2,) + a.shape[1:], F32) for a in halves],
        scratch_shapes=[pltpu.SemaphoreType.DMA((n,)), pltpu.SemaphoreType.DMA((n,)), pltpu.SemaphoreType.DMA((n,))],
    )(*halves)


def _adamw_math(w, g, m, v):
    m = ADAM_B1 * m + (1.0 - ADAM_B1) * g
    v = ADAM_B2 * v + (1.0 - ADAM_B2) * jnp.square(g)
    m_hat = m / (1.0 - ADAM_B1 ** ADAM_STEP)
    v_hat = v / (1.0 - ADAM_B2 ** ADAM_STEP)
    delta = -ADAM_LR * (m_hat / (jnp.sqrt(v_hat) + ADAM_EPS) + ADAM_WD * w)
    return delta, m, v


def _adamw(name, w, g, m, v, block_rows):
    rows, cols = w.shape

    def body(w_ref, g_ref, m_ref, v_ref, d_out, m_out, v_out):
        d_out[...], m_out[...], v_out[...] = _adamw_math(w_ref[...], g_ref[...], m_ref[...], v_ref[...])

    spec = pl.BlockSpec((block_rows, cols), lambda i: (i, 0))
    shape = jax.ShapeDtypeStruct((rows, cols), F32)
    return pl.pallas_call(
        body, grid=(rows // block_rows,), name=name, in_specs=[spec] * 4, out_specs=[spec] * 3,
        out_shape=[shape] * 3, compiler_params=_params(),
    )(w, g, m, v)


def _adamw_ada(c_all16, dmod16, w, m, v, block_rows):
    rows, cols = w.shape

    def body(c_ref, dm_ref, w_ref, m_ref, v_ref, g_out, d_out, m_out, v_out):
        cv = c_ref[...]
        g = _tn((cv * jax.nn.sigmoid(cv)).astype(BF16), dm_ref[...].astype(BF16))
        g_out[...] = g
        d_out[...], m_out[...], v_out[...] = _adamw_math(w_ref[...], g, m_ref[...], v_ref[...])

    spec = pl.BlockSpec((block_rows, cols), lambda i: (i, 0))
    shape = jax.ShapeDtypeStruct((rows, cols), F32)
    return pl.pallas_call(
        body, grid=(rows // block_rows,), name="adamw_w_ada",
        in_specs=[pl.BlockSpec((16, block_rows), lambda i: (0, i)), _full(dmod16.shape), spec, spec, spec],
        out_specs=[spec] * 4, out_shape=[shape] * 4, compiler_params=_params(),
    )(c_all16, dmod16, w, m, v)


SMALL_REPLICATED = ["b_ada", "norm1_gain", "conv_dw_b", "conv_ln_g", "conv_ln_b", "gm_ln_g", "gm_ln_b", "gm_ws", "gm_bs",
                    "mix_out_gain", "norm2_gain", "ffn_dw_b", "final_gain"]
SMALL_SHARDED = ["conv_dw_w", "ffn_dw_w"]
PACK_ROWS = 256
ADAM_PACK_ROWS = 160
WEIGHT_ORDER = ["w_ada", "b_ada", "norm1_gain", "w_in", "conv_dw_w", "conv_dw_b", "conv_ln_g", "conv_ln_b", "gm_ln_g",
                "gm_ln_b", "gm_ws", "gm_bs", "mix_out_gain", "w_out", "norm2_gain", "w_up", "ffn_dw_w", "ffn_dw_b",
                "w_down", "final_gain"]


def _pack(parts, rows):
    flat = jnp.concatenate([a.reshape(-1) for a in parts])
    return jnp.pad(flat, (0, rows * D_MODEL - flat.shape[0])).reshape(rows, D_MODEL)


def _unpack(packed, shapes):
    flat = packed.reshape(-1)
    out, pos = [], 0
    for s in shapes:
        size = 1
        for d in s:
            size *= d
        out.append(flat[pos:pos + size].reshape(s))
        pos += size
    return out


def kernel(x, c, w_ada, b_ada, norm1_gain, w_in, conv_dw_w, conv_dw_b, conv_ln_g, conv_ln_b, gm_ln_g, gm_ln_b, gm_ws, gm_bs, mix_out_gain, w_out, norm2_gain, w_up, ffn_dw_w, ffn_dw_b, w_down, final_gain, loss_target, m_w_ada, m_b_ada, m_norm1_gain, m_w_in, m_conv_dw_w, m_conv_dw_b, m_conv_ln_g, m_conv_ln_b, m_gm_ln_g, m_gm_ln_b, m_gm_ws, m_gm_bs, m_mix_out_gain, m_w_out, m_norm2_gain, m_w_up, m_ffn_dw_w, m_ffn_dw_b, m_w_down, m_final_gain, v_w_ada, v_b_ada, v_norm1_gain, v_w_in, v_conv_dw_w, v_conv_dw_b, v_conv_ln_g, v_conv_ln_b, v_gm_ln_g, v_gm_ln_b, v_gm_ws, v_gm_bs, v_mix_out_gain, v_w_out, v_norm2_gain, v_w_up, v_ffn_dw_w, v_ffn_dw_b, v_w_down, v_final_gain):
    weights = dict(w_ada=w_ada, b_ada=b_ada, norm1_gain=norm1_gain, w_in=w_in, conv_dw_w=conv_dw_w, conv_dw_b=conv_dw_b,
                   conv_ln_g=conv_ln_g, conv_ln_b=conv_ln_b, gm_ln_g=gm_ln_g, gm_ln_b=gm_ln_b, gm_ws=gm_ws, gm_bs=gm_bs,
                   mix_out_gain=mix_out_gain, w_out=w_out, norm2_gain=norm2_gain, w_up=w_up, ffn_dw_w=ffn_dw_w,
                   ffn_dw_b=ffn_dw_b, w_down=w_down, final_gain=final_gain)
    mom1 = dict(w_ada=m_w_ada, b_ada=m_b_ada, norm1_gain=m_norm1_gain, w_in=m_w_in, conv_dw_w=m_conv_dw_w,
                conv_dw_b=m_conv_dw_b, conv_ln_g=m_conv_ln_g, conv_ln_b=m_conv_ln_b, gm_ln_g=m_gm_ln_g, gm_ln_b=m_gm_ln_b,
                gm_ws=m_gm_ws, gm_bs=m_gm_bs, mix_out_gain=m_mix_out_gain, w_out=m_w_out, norm2_gain=m_norm2_gain,
                w_up=m_w_up, ffn_dw_w=m_ffn_dw_w, ffn_dw_b=m_ffn_dw_b, w_down=m_w_down, final_gain=m_final_gain)
    mom2 = dict(w_ada=v_w_ada, b_ada=v_b_ada, norm1_gain=v_norm1_gain, w_in=v_w_in, conv_dw_w=v_conv_dw_w,
                conv_dw_b=v_conv_dw_b, conv_ln_g=v_conv_ln_g, conv_ln_b=v_conv_ln_b, gm_ln_g=v_gm_ln_g, gm_ln_b=v_gm_ln_b,
                gm_ws=v_gm_ws, gm_bs=v_gm_bs, mix_out_gain=v_mix_out_gain, w_out=v_w_out, norm2_gain=v_norm2_gain,
                w_up=v_w_up, ffn_dw_w=v_ffn_dw_w, ffn_dw_b=v_ffn_dw_b, w_down=v_w_down, final_gain=v_final_gain)
    shard = 2 * lax.axis_index("x") + lax.axis_index("y")
    me = 2 * shard + lax.axis_index("c")

    ada_cols = w_ada.shape[2]
    b_ada_sh = lax.dynamic_slice(b_ada, (0, shard * ada_cols), (1, ada_cols))
    c_all64, mod32 = _ada_mod(c, w_ada[0], b_ada_sh)
    c_all = c_all64[::8]
    mod = mod32[::8].reshape(1, N_SHARD * ada_cols)

    (w_in_g, w_out_g, w_up_g, w_down_g), (conv_w_g, ffn_w_g) = _gather_weights(
        [w_in[0], w_out[0], w_up[0], w_down[0]], [conv_dw_w[0], ffn_dw_w[0]])
    conv_w_full = jnp.transpose(conv_w_g, (1, 0, 2)).reshape(CONV_K, D_HALF)
    ffn_w_full = jnp.transpose(ffn_w_g, (1, 0, 2)).reshape(FFN_K, 2 * D_FF)

    p = dict(norm1_gain=norm1_gain, conv_dw_w=conv_w_full, conv_dw_b=conv_dw_b, conv_ln_g=conv_ln_g,
             conv_ln_b=conv_ln_b, gm_ln_g=gm_ln_g, gm_ln_b=gm_ln_b, gm_ws=gm_ws[0], gm_bs=gm_bs[0],
             mix_out_gain=mix_out_gain, norm2_gain=norm2_gain, ffn_dw_w=ffn_w_full, ffn_dw_b=ffn_dw_b,
             final_gain=final_gain[None])
    grad_x, g, d_mod, loss = _local_step(
        x[0], loss_target[0], mod, p, w_in_g, w_out_g.reshape(D_MODEL, D_MODEL), w_up_g,
        w_down_g.reshape(D_FF, D_MODEL))

    n_mod = d_mod.shape[1]
    dmod_rows = lax.dynamic_update_slice(jnp.zeros((N_DEV, n_mod), F32), d_mod, (me, 0))
    g["b_ada"] = d_mod
    small = _pack([g[k] for k in SMALL_REPLICATED] + [g[k] for k in SMALL_SHARDED] + [dmod_rows, loss[0, :1]], PACK_ROWS)
    big = [g["w_in"], g["w_out"].reshape(N_SHARD, -1, D_MODEL), g["w_up"], g["w_down"].reshape(N_SHARD, -1, D_MODEL)]
    for axis in ("c", "x", "y"):
        keeps, recvs, small_recv = _reduce_exchange(axis, big, small)
        *big, small = _add_pairs("reduce_add_" + axis, list(keeps) + [small], list(recvs) + [small_recv])
    full = _allgather_halves(big)
    grads = dict(w_in=full[0].reshape(w_in.shape[1:]), w_out=full[1].reshape(w_out.shape[1:]),
                 w_up=full[2].reshape(w_up.shape[1:]), w_down=full[3].reshape(w_down.shape[1:]))

    small_shapes = ([weights[k].shape for k in SMALL_REPLICATED] + [(CONV_K, D_HALF), (FFN_K, 2 * D_FF)]
                    + [(N_DEV, n_mod), (1,)])
    *small_grads, conv_w_grad, ffn_w_grad, dmod_all, loss_sum = _unpack(small, small_shapes)
    grads.update(zip(SMALL_REPLICATED, small_grads))
    grads["conv_dw_w"] = lax.dynamic_slice(conv_w_grad, (0, shard * conv_dw_w.shape[2]), conv_dw_w.shape[1:])[None]
    grads["ffn_dw_w"] = lax.dynamic_slice(ffn_w_grad, (0, shard * ffn_dw_w.shape[2]), ffn_dw_w.shape[1:])[None]

    delta, new_m, new_v = {}, {}, {}
    for name, block_rows in (("w_in", 256), ("w_out", 128), ("w_up", 256), ("w_down", 352)):
        delta[name], new_m[name], new_v[name] = [a[None] for a in _adamw(
            "adamw_" + name, weights[name][0], grads[name], mom1[name][0], mom2[name][0], block_rows)]
        grads[name] = grads[name][None]
    dmod_sh = lax.dynamic_slice(dmod_all, (0, shard * ada_cols), (N_DEV, ada_cols))
    pad8 = ((0, 16 - N_DEV), (0, 0))
    grads["w_ada"], delta["w_ada"], new_m["w_ada"], new_v["w_ada"] = [a[None] for a in _adamw_ada(
        jnp.pad(c_all, pad8), jnp.pad(dmod_sh, pad8), w_ada[0], m_w_ada[0], v_w_ada[0], 256)]
    small_names = SMALL_REPLICATED + SMALL_SHARDED
    packed = [_pack([d[k] for k in small_names], ADAM_PACK_ROWS) for d in (weights, grads, mom1, mom2)]
    small_out = _adamw("adamw_small", *packed, ADAM_PACK_ROWS)
    for d, arr in zip((delta, new_m, new_v), small_out):
        d.update(zip(small_names, _unpack(arr, [weights[k].shape for k in small_names])))

    return (loss_sum.reshape(()), grad_x[None], *[grads[k] for k in WEIGHT_ORDER], *[delta[k] for k in WEIGHT_ORDER],
            *[new_m[k] for k in WEIGHT_ORDER], *[new_v[k] for k in WEIGHT_ORDER])
```

```python
import functools

import jax
import jax.numpy as jnp
from jax import lax
from jax.experimental import pallas as pl
from jax.experimental.pallas import tpu as pltpu

F32 = jnp.float32
BF16 = jnp.bfloat16

D_MODEL = 1024
D_HALF = 512
D_FF = 2816
CONV_K = 31
FFN_K = 3
CHUNK = 128
N_HEADS = 8
HEAD_DIM = 64
N_SHARD = 4
N_DEV = 8
RMS_EPS = 1e-6
LN_EPS = 1e-5
ADAM_LR, ADAM_B1, ADAM_B2, ADAM_EPS, ADAM_WD, ADAM_STEP = 0.001, 0.9, 0.999, 1e-08, 0.01, 10

TILE = 256
HALO = 32
FFN_HALO = 8
FFN_BLK = 256
UP_SHARD = 2 * D_FF // N_SHARD
VMEM_LIMIT_BYTES = 56 * 1024 * 1024

ANY = pl.BlockSpec(memory_space=pl.ANY)
NT_DIMS = (((1,), (1,)), ((), ()))
TN_DIMS = (((0,), (0,)), ((), ()))


def _full(shape):
    return pl.BlockSpec(shape, lambda i: (0,) * len(shape))


def _nn(a, b):
    return jnp.dot(a, b, preferred_element_type=F32)


def _nt(a, b):
    return lax.dot_general(a, b, NT_DIMS, preferred_element_type=F32)


def _tn(a, b):
    return lax.dot_general(a, b, TN_DIMS, preferred_element_type=F32)


def _colsum(a):
    return jnp.sum(a, axis=0, keepdims=True)


def _params(semantics=("arbitrary",)):
    return pltpu.CompilerParams(dimension_semantics=semantics, vmem_limit_bytes=VMEM_LIMIT_BYTES)


def _rms(v, gain):
    return v * lax.rsqrt(jnp.mean(v * v, axis=-1, keepdims=True) + RMS_EPS) * gain


def _layer_norm(v, gain, bias):
    mu = jnp.mean(v, axis=-1, keepdims=True)
    var = jnp.mean(jnp.square(v - mu), axis=-1, keepdims=True)
    return (v - mu) * lax.rsqrt(var + LN_EPS) * gain + bias


def _mod_norm(v, gain, scale, shift):
    return _rms(v, gain) * (1.0 + scale) + shift


def _conv_branch(a1, ln_g, ln_b, out_gain):
    a2 = _layer_norm(a1, ln_g, ln_b)
    return _rms(a2 * jax.nn.sigmoid(a2), out_gain)


def _gate_branch(gu, sp, out_gain):
    return _rms(jax.nn.gelu(gu) * sp, out_gain)


def _gv_norm(gv, ln_g, ln_b):
    return _layer_norm(jax.nn.gelu(gv), ln_g, ln_b)


def _head_pair_matmul(wp_ref, v):
    lane = lax.broadcasted_iota(jnp.int32, (CHUNK, CHUNK), 1)
    rows = []
    for n in range(v.shape[0] // CHUNK):
        cols = []
        for j in range(N_HEADS // 2):
            r = _nn(wp_ref[j], v[n * CHUNK:(n + 1) * CHUNK, j * CHUNK:(j + 1) * CHUNK])
            cols.append(jnp.where(lane < HEAD_DIM, r[:CHUNK], r[CHUNK:]))
        rows.append(jnp.concatenate(cols, axis=1))
    return jnp.concatenate(rows, axis=0)


def _tile_bias(bs, tokens):
    return jnp.concatenate([bs] * (tokens // CHUNK), axis=0)


def _fwd_mixer_up(x, vec, conv_w, wpair, bs_full, w_in_g, w_out_g, w_up_g, u_dtype):
    seq = x.shape[0]
    n_tiles = seq // TILE
    t = TILE
    names = ["norm1_gain", "sc1", "sh1", "gt1", "norm2_gain", "sc2", "sh2", "conv_dw_b", "conv_ln_g", "conv_ln_b",
             "gm_ln_g", "gm_ln_b", "mix_out_gain"]
    vecs = [vec[k] for k in names]

    def body(x_ref, g1, sc1, sh1, gt1, g2, sc2, sh2, cb, clg, clb, vg, vb, mg, cw, wp, bs, win_hbm, wout_hbm, wup_hbm,
             z_ref, a1_ref, sp_ref, y_ref, o1_ref, x2_ref, u_ref, win_v, wout_v, wup_v, cbuf, sem):
        i = pl.program_id(0)

        @pl.when(i == 0)
        def _():
            cps = [pltpu.make_async_copy(win_hbm, win_v, sem.at[0]),
                   pltpu.make_async_copy(wout_hbm, wout_v, sem.at[1]),
                   pltpu.make_async_copy(wup_hbm, wup_v, sem.at[2])]
            for cp in cps:
                cp.start()
            for cp in cps:
                cp.wait()
            cbuf[pl.ds(0, HALO), :] = jnp.zeros((HALO, D_HALF), F32)

        xv = x_ref[...]
        h1b = _mod_norm(xv, g1[...], sc1[...], sh1[...]).astype(BF16)
        zs = [_nn(h1b, win_v[k]) for k in range(N_SHARD)]
        for k in range(N_SHARD):
            z_ref[:, k * D_HALF:(k + 1) * D_HALF] = zs[k]
        ca, cg, gu, gv = zs
        cbuf[pl.ds(HALO, t), :] = ca * jax.nn.sigmoid(cg)
        a1 = jnp.zeros((t, D_HALF), F32) + cb[...]
        for k in range(CONV_K):
            a1 = a1 + cbuf[pl.ds(HALO - (CONV_K - 1) + k, t), :] * cw[pl.ds(k, 1), :]
        cbuf[pl.ds(0, HALO), :] = cbuf[pl.ds(t, HALO), :]
        a1_ref[...] = a1
        mgv = mg[...]
        ya = _conv_branch(a1, clg[...], clb[...], mgv[:, :D_HALF])
        gvn = _gv_norm(gv, vg[...], vb[...]).astype(BF16)
        sp = _head_pair_matmul(wp, gvn) + _tile_bias(bs[...], t)
        sp_ref[...] = sp
        yg = _gate_branch(gu, sp, mgv[:, D_HALF:])
        yb = jnp.concatenate([ya, yg], axis=1).astype(BF16)
        y_ref[...] = yb
        o1 = _nn(yb, wout_v[...])
        o1_ref[...] = o1
        x2 = xv + gt1[...] * o1
        x2_ref[...] = x2
        h2b = _mod_norm(x2, g2[...], sc2[...], sh2[...]).astype(BF16)
        for k in range(N_SHARD):
            u_ref[:, k * UP_SHARD:(k + 1) * UP_SHARD] = _nn(h2b, wup_v[k]).astype(u_ref.dtype)

    def row(width):
        return pl.BlockSpec((t, width), lambda i: (i, 0))

    out_shape = [jax.ShapeDtypeStruct((seq, 4 * D_HALF), F32), jax.ShapeDtypeStruct((seq, D_HALF), F32),
                 jax.ShapeDtypeStruct((seq, D_HALF), F32), jax.ShapeDtypeStruct((seq, D_MODEL), BF16),
                 jax.ShapeDtypeStruct((seq, D_MODEL), F32), jax.ShapeDtypeStruct((seq, D_MODEL), F32),
                 jax.ShapeDtypeStruct((seq, 2 * D_FF), u_dtype)]
    return pl.pallas_call(
        body, grid=(n_tiles,), name="fwd_mixer_up",
        in_specs=[row(D_MODEL)] + [_full(v.shape) for v in vecs]
        + [_full(conv_w.shape), _full(wpair.shape), _full(bs_full.shape), ANY, ANY, ANY],
        out_specs=[row(4 * D_HALF), row(D_HALF), row(D_HALF), row(D_MODEL), row(D_MODEL), row(D_MODEL), row(2 * D_FF)],
        out_shape=out_shape,
        scratch_shapes=[pltpu.VMEM(w_in_g.shape, BF16), pltpu.VMEM(w_out_g.shape, BF16), pltpu.VMEM(w_up_g.shape, BF16),
                        pltpu.VMEM((t + HALO, D_HALF), F32), pltpu.SemaphoreType.DMA((3,))],
        compiler_params=_params(),
    )(x, *vecs, conv_w, wpair, bs_full, w_in_g, w_out_g, w_up_g)


def _ffn_tail(u, x2, target, ffn_w, ffn_b, gt2, final_gain, w_down_g):
    seq = x2.shape[0]
    n_tiles = seq // TILE
    t = TILE
    n_blk = D_FF // FFN_BLK
    halo_rows = 16 if u.dtype == BF16 else 8
    inv_d = 1.0 / D_MODEL

    def final_norm(x3, gain):
        return _rms(x3, gain)

    def body(u_ref, uh_ref, x2_ref, tgt_ref, fw, fb, gt2_ref, fg, wd_hbm,
             du_ref, dx3_ref, dfw_ref, dfb_ref, dfg_ref, dgt2_ref, loss_ref, dwd_hbm,
             wd_v, dwd_acc, bufv, bufg, dbuf, carry, sem):
        i = pl.program_id(0)
        tile = n_tiles - 1 - i

        @pl.when(i == 0)
        def _():
            cp = pltpu.make_async_copy(wd_hbm, wd_v, sem.at[0])
            cp.start()
            cp.wait()
            dwd_acc[...] = jnp.zeros_like(dwd_acc)
            carry[...] = jnp.zeros_like(carry)
            dfw_ref[...] = jnp.zeros_like(dfw_ref)
            dfb_ref[...] = jnp.zeros_like(dfb_ref)
            dfg_ref[...] = jnp.zeros_like(dfg_ref)
            dgt2_ref[...] = jnp.zeros_like(dgt2_ref)
            loss_ref[...] = jnp.zeros_like(loss_ref)

        def cols_of(j):
            return pl.ds(j * FFN_BLK, FFN_BLK), pl.ds(D_FF + j * FFN_BLK, FFN_BLK)

        def fill(buf, cols):
            prev = uh_ref[pl.ds(halo_rows - FFN_HALO, FFN_HALO), cols].astype(F32)
            buf[pl.ds(0, FFN_HALO), :] = jnp.where(tile > 0, prev, 0.0)
            buf[pl.ds(FFN_HALO, t), :] = u_ref[:, cols].astype(F32)

        def conv(buf, cols):
            acc = jnp.zeros((t, FFN_BLK), F32) + fb[:, cols]
            for k in range(FFN_K):
                acc = acc + buf[pl.ds(FFN_HALO - (FFN_K - 1) + k, t), :] * fw[pl.ds(k, 1), cols]
            return acc

        o2 = jnp.zeros((t, D_MODEL), F32)
        for j in range(n_blk):
            cv, cg = cols_of(j)
            fill(bufv, cv)
            fill(bufg, cg)
            val, gate = conv(bufv, cv), conv(bufg, cg)
            f = gate * jax.nn.sigmoid(gate) * val
            o2 = o2 + _nn(f.astype(BF16), wd_v[pl.ds(j * FFN_BLK, FFN_BLK), :])

        gt2v = gt2_ref[...]
        x3 = x2_ref[...] + gt2v * o2
        out, out_vjp = jax.vjp(final_norm, x3, fg[...])
        diff = out - tgt_ref[...]
        loss_ref[...] += jnp.zeros_like(loss_ref) + 0.5 * inv_d * jnp.sum(diff * diff)
        dx3, dfg = out_vjp(diff * inv_d)
        dfg_ref[...] += dfg
        dgt2_ref[...] += _colsum(dx3 * o2)
        dx3_ref[...] = dx3
        do2b = (gt2v * dx3).astype(BF16)

        for j in range(n_blk):
            cv, cg = cols_of(j)
            rows = pl.ds(j * FFN_BLK, FFN_BLK)
            fill(bufv, cv)
            fill(bufg, cg)
            val, gate = conv(bufv, cv), conv(bufg, cg)
            df = _nt(do2b, wd_v[rows, :])
            sig = jax.nn.sigmoid(gate)
            sil = gate * sig
            dval = df * sil
            dgate = df * val * (sig * (1.0 + gate * (1.0 - sig)))
            dwd_acc[rows, :] += _tn((sil * val).astype(BF16), do2b)
            for dd, buf, cols in ((dval, bufv, cv), (dgate, bufg, cg)):
                dfb_ref[:, cols] += _colsum(dd)
                for k in range(FFN_K):
                    dfw_ref[pl.ds(k, 1), cols] += _colsum(dd * buf[pl.ds(FFN_HALO - (FFN_K - 1) + k, t), :])
                dbuf[pl.ds(0, t), :] = dd
                dbuf[pl.ds(t, FFN_HALO), :] = carry[:, cols]
                du = jnp.zeros((t, FFN_BLK), F32)
                for k in range(FFN_K):
                    du = du + dbuf[pl.ds(FFN_K - 1 - k, t), :] * fw[pl.ds(k, 1), cols]
                carry[:, cols] = dbuf[pl.ds(0, FFN_HALO), :]
                du_ref[:, cols] = du.astype(BF16)

        @pl.when(i == n_tiles - 1)
        def _():
            cp = pltpu.make_async_copy(dwd_acc, dwd_hbm, sem.at[1])
            cp.start()
            cp.wait()

    def rev(width):
        return pl.BlockSpec((t, width), lambda i: (n_tiles - 1 - i, 0))

    halo_spec = pl.BlockSpec(
        (halo_rows, 2 * D_FF), lambda i: (jnp.maximum((n_tiles - 1 - i) * (t // halo_rows) - 1, 0), 0))
    out_shape = [jax.ShapeDtypeStruct((seq, 2 * D_FF), BF16), jax.ShapeDtypeStruct((seq, D_MODEL), F32),
                 jax.ShapeDtypeStruct((FFN_K, 2 * D_FF), F32), jax.ShapeDtypeStruct((1, 2 * D_FF), F32),
                 jax.ShapeDtypeStruct((1, D_MODEL), F32), jax.ShapeDtypeStruct((1, D_MODEL), F32),
                 jax.ShapeDtypeStruct((1, 128), F32), jax.ShapeDtypeStruct((D_FF, D_MODEL), F32)]
    return pl.pallas_call(
        body, grid=(n_tiles,), name="ffn_tail",
        in_specs=[rev(2 * D_FF), halo_spec, rev(D_MODEL), rev(D_MODEL), _full(ffn_w.shape), _full(ffn_b.shape),
                  _full(gt2.shape), _full(final_gain.shape), ANY],
        out_specs=[rev(2 * D_FF), rev(D_MODEL), _full((FFN_K, 2 * D_FF)), _full((1, 2 * D_FF)), _full((1, D_MODEL)),
                   _full((1, D_MODEL)), _full((1, 128)), ANY],
        out_shape=out_shape,
        scratch_shapes=[pltpu.VMEM((D_FF, D_MODEL), BF16), pltpu.VMEM((D_FF, D_MODEL), F32),
                        pltpu.VMEM((t + FFN_HALO, FFN_BLK), F32), pltpu.VMEM((t + FFN_HALO, FFN_BLK), F32),
                        pltpu.VMEM((t + FFN_HALO, FFN_BLK), F32), pltpu.VMEM((FFN_HALO, 2 * D_FF), F32),
                        pltpu.SemaphoreType.DMA((2,))],
        compiler_params=_params(),
    )(u, u, x2, target, ffn_w, ffn_b, gt2, final_gain, w_down_g)


def _bwd_up(du, x2, dx3, norm2_gain, sc2, sh2, w_up_g):
    seq = x2.shape[0]
    n_tiles = seq // TILE
    t = TILE

    def body(du_ref, x2_ref, dx3_ref, g2, sc2_ref, sh2_ref, wup_hbm, dx2_ref, dg2_ref, dsc2_ref, dsh2_ref, dwup_hbm,
             wup_v, dwup_acc, sem):
        i = pl.program_id(0)

        @pl.when(i == 0)
        def _():
            cp = pltpu.make_async_copy(wup_hbm, wup_v, sem.at[0])
            cp.start()
            cp.wait()
            dwup_acc[...] = jnp.zeros_like(dwup_acc)
            dg2_ref[...] = jnp.zeros_like(dg2_ref)
            dsc2_ref[...] = jnp.zeros_like(dsc2_ref)
            dsh2_ref[...] = jnp.zeros_like(dsh2_ref)

        h2, h2_vjp = jax.vjp(_mod_norm, x2_ref[...], g2[...], sc2_ref[...], sh2_ref[...])
        h2b = h2.astype(BF16)
        dh2 = jnp.zeros((t, D_MODEL), F32)
        for k in range(N_SHARD):
            dub = du_ref[:, k * UP_SHARD:(k + 1) * UP_SHARD]
            dh2 = dh2 + _nt(dub, wup_v[k])
            dwup_acc[k] += _tn(h2b, dub)
        dx2, dg2, dsc2, dsh2 = h2_vjp(dh2)
        dx2_ref[...] = dx3_ref[...] + dx2
        dg2_ref[...] += dg2
        dsc2_ref[...] += dsc2
        dsh2_ref[...] += dsh2

        @pl.when(i == n_tiles - 1)
        def _():
            cp = pltpu.make_async_copy(dwup_acc, dwup_hbm, sem.at[1])
            cp.start()
            cp.wait()

    def row(width):
        return pl.BlockSpec((t, width), lambda i: (i, 0))

    vec = jax.ShapeDtypeStruct((1, D_MODEL), F32)
    return pl.pallas_call(
        body, grid=(n_tiles,), name="bwd_up",
        in_specs=[row(2 * D_FF), row(D_MODEL), row(D_MODEL), _full((1, D_MODEL)), _full((1, D_MODEL)),
                  _full((1, D_MODEL)), ANY],
        out_specs=[row(D_MODEL), _full((1, D_MODEL)), _full((1, D_MODEL)), _full((1, D_MODEL)), ANY],
        out_shape=[jax.ShapeDtypeStruct((seq, D_MODEL), F32), vec, vec, vec,
                   jax.ShapeDtypeStruct(w_up_g.shape, F32)],
        scratch_shapes=[pltpu.VMEM(w_up_g.shape, BF16), pltpu.VMEM(w_up_g.shape, F32), pltpu.SemaphoreType.DMA((2,))],
        compiler_params=_params(),
    )(du, x2, dx3, norm2_gain, sc2, sh2, w_up_g)


def _bwd_mixer(dx2, x, z, a1, sp, yb, o1, vec, conv_w, wpair, wpair_t, causal_mask, w_in_g, w_out_g):
    seq = x.shape[0]
    n_tiles = seq // TILE
    t = TILE
    names = ["norm1_gain", "sc1", "sh1", "gt1", "conv_ln_g", "conv_ln_b", "gm_ln_g", "gm_ln_b", "mix_out_gain"]
    vecs = [vec[k] for k in names]

    def body(dx2_ref, x_ref, z_ref, zh_ref, a1_ref, sp_ref, y_ref, o1_ref, g1, sc1, sh1, gt1, clg, clb, vg, vb, mg,
             cw, wp, wpt, mask_ref, win_hbm, wout_hbm,
             gx_ref, dg1_ref, dsc1_ref, dsh1_ref, dgt1_ref, dcw_ref, dcb_ref, dclg_ref, dclb_ref, dvg_ref, dvb_ref,
             dmg_ref, dws_ref, dbs_ref, dwin_hbm, dwout_hbm,
             win_v, wout_v, dwin_acc, dwout_acc, abuf, dbuf, dbs_acc, sem):
        i = pl.program_id(0)
        tile = n_tiles - 1 - i
        small = [dg1_ref, dsc1_ref, dsh1_ref, dgt1_ref, dcw_ref, dcb_ref, dclg_ref, dclb_ref, dvg_ref, dvb_ref,
                 dmg_ref, dws_ref, dbs_acc]

        @pl.when(i == 0)
        def _():
            cps = [pltpu.make_async_copy(win_hbm, win_v, sem.at[0]),
                   pltpu.make_async_copy(wout_hbm, wout_v, sem.at[1])]
            for cp in cps:
                cp.start()
            for cp in cps:
                cp.wait()
            dwin_acc[...] = jnp.zeros_like(dwin_acc)
            dwout_acc[...] = jnp.zeros_like(dwout_acc)
            dbuf[pl.ds(t, HALO), :] = jnp.zeros((HALO, D_HALF), F32)
            for ref in small:
                ref[...] = jnp.zeros_like(ref)

        dx2v = dx2_ref[...]
        gt1v = gt1[...]
        dgt1_ref[...] += _colsum(dx2v * o1_ref[...])
        do1b = (gt1v * dx2v).astype(BF16)
        dy = _nt(do1b, wout_v[...])
        dwout_acc[...] += _tn(y_ref[...], do1b)

        mgv = mg[...]
        _, conv_vjp = jax.vjp(_conv_branch, a1_ref[...], clg[...], clb[...], mgv[:, :D_HALF])
        da1, dclg, dclb, dmg_a = conv_vjp(dy[:, :D_HALF])
        dclg_ref[...] += dclg
        dclb_ref[...] += dclb
        gu = z_ref[:, 2 * D_HALF:3 * D_HALF]
        gv = z_ref[:, 3 * D_HALF:]
        spv = sp_ref[...]
        _, gate_vjp = jax.vjp(_gate_branch, gu, spv, mgv[:, D_HALF:])
        dgu, dsp, dmg_g = gate_vjp(dy[:, D_HALF:])
        dmg_ref[...] += jnp.concatenate([dmg_a, dmg_g], axis=1)
        gvn, gv_vjp = jax.vjp(_gv_norm, gv, vg[...], vb[...])
        gvnb = gvn.astype(BF16)
        dspb = dsp.astype(BF16)
        dgvn = _head_pair_matmul(wpt, dspb)
        dgv, dvg, dvb = gv_vjp(dgvn)
        dvg_ref[...] += dvg
        dvb_ref[...] += dvb
        lane = lax.broadcasted_iota(jnp.int32, (CHUNK, CHUNK), 1)
        dbs = jnp.zeros((CHUNK, D_HALF), F32)
        for n in range(t // CHUNK):
            rows = slice(n * CHUNK, (n + 1) * CHUNK)
            dbs = dbs + dsp[rows, :]
            for j in range(N_HEADS // 2):
                cols = slice(j * CHUNK, (j + 1) * CHUNK)
                blk = dspb[rows, cols]
                zero = jnp.zeros_like(blk)
                vblk = gvnb[rows, cols]
                dws_ref[2 * j] += _nt(jnp.where(lane < HEAD_DIM, blk, zero), vblk)
                dws_ref[2 * j + 1] += _nt(jnp.where(lane < HEAD_DIM, zero, blk), vblk)
        dbs_acc[...] += dbs

        ca = z_ref[:, :D_HALF]
        cg = z_ref[:, D_HALF:2 * D_HALF]
        sig = jax.nn.sigmoid(cg)
        zh = zh_ref[...]
        a0_prev = zh[:, :D_HALF] * jax.nn.sigmoid(zh[:, D_HALF:])
        abuf[pl.ds(0, HALO), :] = jnp.where(tile > 0, a0_prev, 0.0)
        abuf[pl.ds(HALO, t), :] = ca * sig
        dbuf[pl.ds(0, t), :] = da1
        dcb_ref[...] += _colsum(da1)
        da0 = jnp.zeros((t, D_HALF), F32)
        for k in range(CONV_K):
            da0 = da0 + dbuf[pl.ds(CONV_K - 1 - k, t), :] * cw[pl.ds(k, 1), :]
            dcw_ref[pl.ds(k, 1), :] += _colsum(da1 * abuf[pl.ds(HALO - (CONV_K - 1) + k, t), :])
        dbuf[pl.ds(t, HALO), :] = dbuf[pl.ds(0, HALO), :]
        dca = da0 * sig
        dcg = da0 * ca * sig * (1.0 - sig)

        h1, h1_vjp = jax.vjp(_mod_norm, x_ref[...], g1[...], sc1[...], sh1[...])
        h1b = h1.astype(BF16)
        dh1 = jnp.zeros((t, D_MODEL), F32)
        for k, dzk in enumerate((dca, dcg, dgu, dgv)):
            dzb = dzk.astype(BF16)
            dh1 = dh1 + _nt(dzb, win_v[k])
            dwin_acc[k] += _tn(h1b, dzb)
        dx, dg1, dsc1, dsh1 = h1_vjp(dh1)
        gx_ref[...] = dx2v + dx
        dg1_ref[...] += dg1
        dsc1_ref[...] += dsc1
        dsh1_ref[...] += dsh1

        @pl.when(i == n_tiles - 1)
        def _():
            for h in range(N_HEADS):
                dws_ref[h] = dws_ref[h] * mask_ref[...]
            head_of_lane = lax.broadcasted_iota(jnp.int32, (N_HEADS, D_HALF), 1) // HEAD_DIM
            pick = (head_of_lane == lax.broadcasted_iota(jnp.int32, (N_HEADS, D_HALF), 0)).astype(F32)
            dbs_ref[...] = lax.dot_general(pick, dbs_acc[...], NT_DIMS, precision=lax.Precision.HIGHEST,
                                           preferred_element_type=F32)
            cps = [pltpu.make_async_copy(dwin_acc, dwin_hbm, sem.at[2]),
                   pltpu.make_async_copy(dwout_acc, dwout_hbm, sem.at[3])]
            for cp in cps:
                cp.start()
            for cp in cps:
                cp.wait()

    def rev(width):
        return pl.BlockSpec((t, width), lambda i: (n_tiles - 1 - i, 0))

    halo_spec = pl.BlockSpec((HALO, 2 * D_HALF), lambda i: (jnp.maximum((n_tiles - 1 - i) * (t // HALO) - 1, 0), 0))
    v1024 = jax.ShapeDtypeStruct((1, D_MODEL), F32)
    v512 = jax.ShapeDtypeStruct((1, D_HALF), F32)
    small_shapes = [v1024, v1024, v1024, v1024, jax.ShapeDtypeStruct((CONV_K, D_HALF), F32), v512, v512, v512, v512,
                    v512, v1024, jax.ShapeDtypeStruct((N_HEADS, CHUNK, CHUNK), F32),
                    jax.ShapeDtypeStruct((N_HEADS, CHUNK), F32)]
    return pl.pallas_call(
        body, grid=(n_tiles,), name="bwd_mixer",
        in_specs=[rev(D_MODEL), rev(D_MODEL), rev(4 * D_HALF), halo_spec, rev(D_HALF), rev(D_HALF), rev(D_MODEL),
                  rev(D_MODEL)] + [_full(v.shape) for v in vecs]
        + [_full(conv_w.shape), _full(wpair.shape), _full(wpair_t.shape), _full(causal_mask.shape), ANY, ANY],
        out_specs=[rev(D_MODEL)] + [_full(s.shape) for s in small_shapes] + [ANY, ANY],
        out_shape=[jax.ShapeDtypeStruct((seq, D_MODEL), F32)] + small_shapes
        + [jax.ShapeDtypeStruct(w_in_g.shape, F32), jax.ShapeDtypeStruct(w_out_g.shape, F32)],
        scratch_shapes=[pltpu.VMEM(w_in_g.shape, BF16), pltpu.VMEM(w_out_g.shape, BF16),
                        pltpu.VMEM(w_in_g.shape, F32), pltpu.VMEM(w_out_g.shape, F32),
                        pltpu.VMEM((t + HALO, D_HALF), F32), pltpu.VMEM((t + HALO, D_HALF), F32),
                        pltpu.VMEM((CHUNK, D_HALF), F32), pltpu.SemaphoreType.DMA((4,))],
        compiler_params=_params(),
    )(dx2, x, z, z, a1, sp, yb, o1, *vecs, conv_w, wpair, wpair_t, causal_mask, w_in_g, w_out_g)


def _gmlp_operands(gm_ws, gm_bs):
    mask = jnp.tril(jnp.ones((CHUNK, CHUNK), F32))
    ws = gm_ws * mask[None]
    wpair = ws.reshape(N_HEADS // 2, 2 * CHUNK, CHUNK).astype(BF16)
    wpair_t = jnp.swapaxes(ws, 1, 2).reshape(N_HEADS // 2, 2 * CHUNK, CHUNK).astype(BF16)
    bs_full = jnp.repeat(jnp.transpose(gm_bs), HEAD_DIM, axis=1)
    return wpair, wpair_t, bs_full, mask


def _local_step(x, target, mod, p, w_in_g, w_out_g, w_up_g, w_down_g, u_dtype=F32):
    sh1, sc1, gt1, sh2, sc2, gt2 = [mod[:, k * D_MODEL:(k + 1) * D_MODEL] for k in range(6)]
    vec = dict(p, sh1=sh1, sc1=sc1, gt1=gt1, sh2=sh2, sc2=sc2, gt2=gt2)
    wpair, wpair_t, bs_full, mask = _gmlp_operands(p["gm_ws"], p["gm_bs"])

    z, a1, sp, yb, o1, x2, u = _fwd_mixer_up(x, vec, p["conv_dw_w"], wpair, bs_full, w_in_g, w_out_g, w_up_g, u_dtype)
    du, dx3, d_ffn_w, d_ffn_b, d_fg, d_gt2, loss, d_wd = _ffn_tail(
        u, x2, target, p["ffn_dw_w"], p["ffn_dw_b"], gt2, p["final_gain"], w_down_g)
    dx2, d_g2, d_sc2, d_sh2, d_wup = _bwd_up(du, x2, dx3, p["norm2_gain"], sc2, sh2, w_up_g)
    (gx, d_g1, d_sc1, d_sh1, d_gt1, d_cw, d_cb, d_clg, d_clb, d_vg, d_vb, d_mg, d_ws, d_bs, d_win,
     d_wout) = _bwd_mixer(dx2, x, z, a1, sp, yb, o1, vec, p["conv_dw_w"], wpair, wpair_t, mask, w_in_g, w_out_g)
    d_mod = jnp.concatenate([d_sh1, d_sc1, d_gt1, d_sh2, d_sc2, d_gt2], axis=1)
    grads = dict(norm1_gain=d_g1, conv_dw_w=d_cw, conv_dw_b=d_cb, conv_ln_g=d_clg, conv_ln_b=d_clb, gm_ln_g=d_vg,
                 gm_ln_b=d_vb, gm_ws=d_ws, gm_bs=d_bs, mix_out_gain=d_mg, norm2_gain=d_g2, ffn_dw_w=d_ffn_w,
                 ffn_dw_b=d_ffn_b, final_gain=d_fg, w_in=d_win, w_out=d_wout, w_up=d_wup, w_down=d_wd)
    return gx, grads, d_mod, loss


MESH = pl.DeviceIdType.MESH
VMEM_SPEC = pl.BlockSpec(memory_space=pltpu.VMEM)
PEER_FLIPS = [(a, b, d) for a in (0, 1) for b in (0, 1) for d in (0, 1)][1:]
CHIP_FLIPS = [(1, 0), (0, 1), (1, 1)]


def _coords():
    return lax.axis_index("x"), lax.axis_index("y"), lax.axis_index("c")


def _flip(v, bit):
    return 1 - v if bit else v


def _rows8(block):
    return pl.ds(pl.multiple_of(8 * block, 8), 8)


def _ada_mod(c_row, w_ada_sh, b_ada_sh):
    cols = w_ada_sh.shape[1]

    def body(c_ref, w_ref, b_ref, call_ref, mod_ref, cpad, modall, send_sems, recv_sems):
        x, y, c = _coords()
        me = 4 * x + 2 * y + c
        cpad[...] = jnp.zeros_like(cpad)
        cpad[pl.ds(0, 1), :] = c_ref[...]

        def gather_copy(j, flip):
            peer = (_flip(x, flip[0]), _flip(y, flip[1]), _flip(c, flip[2]))
            return pltpu.make_async_remote_copy(
                src_ref=cpad, dst_ref=call_ref.at[_rows8(me)], send_sem=send_sems.at[j], recv_sem=recv_sems.at[j],
                device_id=peer, device_id_type=MESH)

        copies = [gather_copy(j, f) for j, f in enumerate(PEER_FLIPS)]
        for cp in copies:
            cp.start()
        call_ref[_rows8(me), :] = cpad[...]
        for cp in copies:
            cp.wait_recv()
        for cp in copies:
            cp.wait_send()
        cv = call_ref[...]
        c_act = (cv * jax.nn.sigmoid(cv)).astype(BF16)
        modall[...] = _nn(c_act, w_ref[...].astype(BF16)) + b_ref[...]

        slot = _rows8(2 * x + y)

        def piece_copy(j, flip):
            tx, ty = _flip(x, flip[0]), _flip(y, flip[1])
            return pltpu.make_async_remote_copy(
                src_ref=modall.at[_rows8(4 * tx + 2 * ty + c)], dst_ref=mod_ref.at[slot],
                send_sem=send_sems.at[len(PEER_FLIPS) + j], recv_sem=recv_sems.at[len(PEER_FLIPS) + j],
                device_id=(tx, ty, c), device_id_type=MESH)

        pieces = [piece_copy(j, f) for j, f in enumerate(CHIP_FLIPS)]
        for cp in pieces:
            cp.start()
        mod_ref[slot, :] = modall[_rows8(me), :]
        for cp in pieces:
            cp.wait_recv()
        for cp in pieces:
            cp.wait_send()

    n_sem = len(PEER_FLIPS) + len(CHIP_FLIPS)
    return pl.pallas_call(
        body, name="ada_mod",
        in_specs=[VMEM_SPEC, VMEM_SPEC, VMEM_SPEC], out_specs=[VMEM_SPEC, VMEM_SPEC],
        out_shape=[jax.ShapeDtypeStruct((8 * N_DEV, D_MODEL), F32), jax.ShapeDtypeStruct((8 * N_SHARD, cols), F32)],
        scratch_shapes=[pltpu.VMEM((8, D_MODEL), F32), pltpu.VMEM((8 * N_DEV, cols), F32),
                        pltpu.SemaphoreType.DMA((n_sem,)), pltpu.SemaphoreType.DMA((n_sem,))],
        compiler_params=pltpu.CompilerParams(vmem_limit_bytes=VMEM_LIMIT_BYTES),
    )(c_row, w_ada_sh, b_ada_sh)


def _gather_weights(shards, filters):
    n = len(shards)
    nf = len(filters)

    def body(*refs):
        ins, f_ins = refs[:n], refs[n:n + nf]
        outs, f_outs = refs[n + nf:2 * n + nf], refs[2 * n + nf:2 * (n + nf)]
        stage = refs[2 * (n + nf):3 * n + 2 * nf]
        send_sems, recv_sems, local_sems, f_send_sems, f_recv_sems = refs[3 * n + 2 * nf:]
        x, y, c = _coords()
        k = 2 * x + y
        sibling = (x, y, 1 - c)

        def filter_copy(w, j, slot):
            tx, ty = _flip(x, CHIP_FLIPS[j][0]), _flip(y, CHIP_FLIPS[j][1])
            return pltpu.make_async_remote_copy(
                src_ref=f_ins[w], dst_ref=f_outs[w].at[slot], send_sem=f_send_sems.at[w, j],
                recv_sem=f_recv_sems.at[w, j], device_id=(tx, ty, c), device_id_type=MESH)

        def half(w, which):
            h = shards[w].shape[0] // 2
            return pl.ds(pl.multiple_of(which * h, 16), h)

        def ici_copy(w, j, src, slot):
            tx, ty = _flip(x, CHIP_FLIPS[j][0]), _flip(y, CHIP_FLIPS[j][1])
            return pltpu.make_async_remote_copy(
                src_ref=src, dst_ref=outs[w].at[slot, half(w, c)], send_sem=send_sems.at[w, j],
                recv_sem=recv_sems.at[w, j], device_id=(tx, ty, c), device_id_type=MESH)

        def d2d_copy(w, j, slot, which):
            rows = outs[w].at[slot, half(w, which)]
            return pltpu.make_async_remote_copy(
                src_ref=rows, dst_ref=rows, send_sem=send_sems.at[w, len(CHIP_FLIPS) + j],
                recv_sem=recv_sems.at[w, len(CHIP_FLIPS) + j], device_id=sibling, device_id_type=MESH)

        def chip_of(j):
            return 2 * _flip(x, CHIP_FLIPS[j][0]) + _flip(y, CHIP_FLIPS[j][1])

        local, first, passed = [], [], []
        for w in range(nf):
            local.append(pltpu.make_async_copy(f_ins[w], f_outs[w].at[k], local_sems.at[n + w]))
            local[-1].start()
            for j in range(len(CHIP_FLIPS)):
                first.append(filter_copy(w, j, k))
                first[-1].start()
        for w in range(n):
            stage[w][...] = ins[w][...].astype(BF16)
            local.append(pltpu.make_async_copy(stage[w], outs[w].at[k], local_sems.at[w]))
            local[-1].start()
            for j in range(len(CHIP_FLIPS)):
                first.append(ici_copy(w, j, stage[w].at[half(w, c)], k))
                first[-1].start()
        for w in range(nf):
            for j in range(len(CHIP_FLIPS)):
                filter_copy(w, j, chip_of(j)).wait_recv()
        for w in range(n):
            for j in range(len(CHIP_FLIPS)):
                ici_copy(w, j, stage[w].at[half(w, c)], chip_of(j)).wait_recv()
                passed.append(d2d_copy(w, j, chip_of(j), c))
                passed[-1].start()
        for w in range(n):
            for j in range(len(CHIP_FLIPS)):
                d2d_copy(w, j, chip_of(j), 1 - c).wait_recv()
        for cp in first + passed:
            cp.wait_send()
        for cp in local:
            cp.wait()

    sem_shape = (n, 2 * len(CHIP_FLIPS))
    f_sem_shape = (nf, len(CHIP_FLIPS))
    outs = pl.pallas_call(
        body, name="gather_weights",
        in_specs=[VMEM_SPEC] * (n + nf), out_specs=[ANY] * (n + nf),
        out_shape=[jax.ShapeDtypeStruct((N_SHARD,) + s.shape, BF16) for s in shards]
        + [jax.ShapeDtypeStruct((N_SHARD,) + s.shape, F32) for s in filters],
        scratch_shapes=[pltpu.VMEM(s.shape, BF16) for s in shards]
        + [pltpu.SemaphoreType.DMA(sem_shape), pltpu.SemaphoreType.DMA(sem_shape), pltpu.SemaphoreType.DMA((n + nf,)),
           pltpu.SemaphoreType.DMA(f_sem_shape), pltpu.SemaphoreType.DMA(f_sem_shape)],
        compiler_params=pltpu.CompilerParams(vmem_limit_bytes=VMEM_LIMIT_BYTES),
    )(*shards, *filters)
    return outs[:n], outs[n:]


def _reduce_exchange(axis, big, small):
    n = len(big)

    def send_part(ref, x, y, c):
        lead, rows = ref.shape[0], ref.shape[1]
        if axis == "c":
            h = rows // 2
            return ref.at[:, pl.ds(pl.multiple_of((1 - c) * h, 16), h), :]
        own = x if axis == "x" else y
        return ref.at[pl.ds((1 - own) * (lead // 2), lead // 2)]

    def out_shape_of(a):
        lead, rows, cols = a.shape
        return (lead, rows // 2, cols) if axis == "c" else (lead // 2, rows, cols)

    def body(*refs):
        ins, small_ref = refs[:n], refs[n]
        recvs, small_recv = refs[n + 1:2 * n + 1], refs[2 * n + 1]
        send_sems, recv_sems = refs[2 * n + 2:]
        x, y, c = _coords()
        partner = (_flip(x, axis == "x"), _flip(y, axis == "y"), _flip(c, axis == "c"))
        copies = [pltpu.make_async_remote_copy(
            src_ref=send_part(ins[idx], x, y, c), dst_ref=recvs[idx], send_sem=send_sems.at[idx],
            recv_sem=recv_sems.at[idx], device_id=partner, device_id_type=MESH) for idx in range(n)]
        copies.append(pltpu.make_async_remote_copy(
            src_ref=small_ref, dst_ref=small_recv, send_sem=send_sems.at[n], recv_sem=recv_sems.at[n],
            device_id=partner, device_id_type=MESH))
        for cp in copies:
            cp.start()
        for cp in copies:
            cp.wait()

    outs = pl.pallas_call(
        body, name="reduce_exchange_" + axis,
        in_specs=[ANY] * (n + 1), out_specs=[ANY] * (n + 1),
        out_shape=[jax.ShapeDtypeStruct(out_shape_of(a), a.dtype) for a in big]
        + [jax.ShapeDtypeStruct(small.shape, F32)],
        scratch_shapes=[pltpu.SemaphoreType.DMA((n + 1,)), pltpu.SemaphoreType.DMA((n + 1,))],
    )(*big, small)
    return outs[:n], outs[n]


ADD_CHUNKS = 4


def _reduce_add(axis, pos, big, recvs, small, small_recv):
    n = len(big)
    last = axis == "y"

    def chunked(shape):
        lead, rows, cols = shape
        if cols % (128 * ADD_CHUNKS) == 0:
            return (lead, rows, cols // ADD_CHUNKS), 2
        return (lead, rows // ADD_CHUNKS, cols), 1

    def kept_spec(a, part_shape):
        block, ax = chunked(part_shape)
        if axis == "c":
            if ax == 2:
                return pl.BlockSpec(block, lambda i, p: (0, p[2], i))
            return pl.BlockSpec(block, lambda i, p: (0, p[2] * ADD_CHUNKS + i, 0))
        which = 0 if axis == "x" else 1
        if ax == 2:
            return pl.BlockSpec(block, lambda i, p: (p[which], 0, i))
        return pl.BlockSpec(block, lambda i, p: (p[which], i, 0))

    def part_spec(part_shape, lead_of=None):
        block, ax = chunked(part_shape)
        lead = (lambda p: 0) if lead_of is None else lead_of
        if ax == 2:
            return pl.BlockSpec(block, lambda i, p: (lead(p), 0, i))
        return pl.BlockSpec(block, lambda i, p: (lead(p), i, 0))

    def body(pos_ref, *refs):
        kept, got = refs[:n], refs[n:2 * n]
        small_ref, small_got = refs[2 * n], refs[2 * n + 1]
        outs = refs[2 * n + 2:]
        for idx in range(n):
            total = kept[idx][...] + got[idx][...].astype(F32)
            outs[idx][...] = total
            if not last:
                outs[n + idx][...] = total.astype(BF16)
        outs[-1][...] = small_ref[...] + small_got[...]

    part_shapes = [r.shape for r in recvs]
    small_spec = pl.BlockSpec((small.shape[0] // ADD_CHUNKS, small.shape[1]), lambda i, p: (i, 0))
    in_specs = ([kept_spec(a, s) for a, s in zip(big, part_shapes)] + [part_spec(s) for s in part_shapes]
                + [small_spec, small_spec])
    if last:
        out_specs = [part_spec(s, lambda p: p[2]) for s in part_shapes]
        out_shape = [jax.ShapeDtypeStruct((2,) + s[1:], F32) for s in part_shapes]
    else:
        out_specs = [part_spec(s) for s in part_shapes] * 2
        out_shape = ([jax.ShapeDtypeStruct(s, F32) for s in part_shapes]
                     + [jax.ShapeDtypeStruct(s, BF16) for s in part_shapes])
    outs = pl.pallas_call(
        body, name="reduce_add_" + axis,
        grid_spec=pltpu.PrefetchScalarGridSpec(
            num_scalar_prefetch=1, grid=(ADD_CHUNKS,), in_specs=in_specs, out_specs=out_specs + [small_spec]),
        out_shape=out_shape + [jax.ShapeDtypeStruct(small.shape, F32)],
        compiler_params=_params(),
    )(pos, *big, *recvs, small, small_recv)
    return outs[:n], outs[n:-1], outs[-1]


def _swap_halves(halves):
    n = len(halves)

    def body(*refs):
        ins, outs = refs[:n], refs[n:2 * n]
        send_sems, recv_sems = refs[2 * n:]
        x, y, c = _coords()
        copies = [pltpu.make_async_remote_copy(
            src_ref=ins[idx].at[pl.ds(c, 1)], dst_ref=outs[idx].at[pl.ds(c, 1)], send_sem=send_sems.at[idx],
            recv_sem=recv_sems.at[idx], device_id=(x, y, 1 - c), device_id_type=MESH) for idx in range(n)]
        for cp in copies:
            cp.start()
        for cp in copies:
            cp.wait()

    return pl.pallas_call(
        body, name="swap_halves",
        in_specs=[ANY] * n, out_specs=[ANY] * n, input_output_aliases={idx: idx for idx in range(n)},
        out_shape=[jax.ShapeDtypeStruct(a.shape, F32) for a in halves],
        scratch_shapes=[pltpu.SemaphoreType.DMA((n,)), pltpu.SemaphoreType.DMA((n,))],
    )(*halves)


def _adamw_math(w, g, m, v):
    m = ADAM_B1 * m + (1.0 - ADAM_B1) * g
    v = ADAM_B2 * v + (1.0 - ADAM_B2) * jnp.square(g)
    m_hat = m / (1.0 - ADAM_B1 ** ADAM_STEP)
    v_hat = v / (1.0 - ADAM_B2 ** ADAM_STEP)
    delta = -ADAM_LR * (m_hat / (jnp.sqrt(v_hat) + ADAM_EPS) + ADAM_WD * w)
    return delta, m, v


def _adamw(name, w, g, m, v, block_rows):
    rows, cols = w.shape

    def body(w_ref, g_ref, m_ref, v_ref, d_out, m_out, v_out):
        d_out[...], m_out[...], v_out[...] = _adamw_math(w_ref[...], g_ref[...], m_ref[...], v_ref[...])

    spec = pl.BlockSpec((block_rows, cols), lambda i: (i, 0))
    shape = jax.ShapeDtypeStruct((rows, cols), F32)
    return pl.pallas_call(
        body, grid=(rows // block_rows,), name=name, in_specs=[spec] * 4, out_specs=[spec] * 3,
        out_shape=[shape] * 3, compiler_params=_params(),
    )(w, g, m, v)


def _adamw_ada(c_all16, dmod16, w, m, v, block_rows):
    rows, cols = w.shape

    def body(c_ref, dm_ref, w_ref, m_ref, v_ref, g_out, d_out, m_out, v_out):
        cv = c_ref[...]
        g = _tn((cv * jax.nn.sigmoid(cv)).astype(BF16), dm_ref[...].astype(BF16))
        g_out[...] = g
        d_out[...], m_out[...], v_out[...] = _adamw_math(w_ref[...], g, m_ref[...], v_ref[...])

    spec = pl.BlockSpec((block_rows, cols), lambda i: (i, 0))
    shape = jax.ShapeDtypeStruct((rows, cols), F32)
    return pl.pallas_call(
        body, grid=(rows // block_rows,), name="adamw_w_ada",
        in_specs=[pl.BlockSpec((16, block_rows), lambda i: (0, i)), _full(dmod16.shape), spec, spec, spec],
        out_specs=[spec] * 4, out_shape=[shape] * 4, compiler_params=_params(),
    )(c_all16, dmod16, w, m, v)


SMALL_REPLICATED = ["b_ada", "norm1_gain", "conv_dw_b", "conv_ln_g", "conv_ln_b", "gm_ln_g", "gm_ln_b", "gm_ws", "gm_bs",
                    "mix_out_gain", "norm2_gain", "ffn_dw_b", "final_gain"]
SMALL_SHARDED = ["conv_dw_w", "ffn_dw_w"]
PACK_ROWS = 256
ADAM_PACK_ROWS = 160
WEIGHT_ORDER = ["w_ada", "b_ada", "norm1_gain", "w_in", "conv_dw_w", "conv_dw_b", "conv_ln_g", "conv_ln_b", "gm_ln_g",
                "gm_ln_b", "gm_ws", "gm_bs", "mix_out_gain", "w_out", "norm2_gain", "w_up", "ffn_dw_w", "ffn_dw_b",
                "w_down", "final_gain"]


def _pack(parts, rows):
    flat = jnp.concatenate([a.reshape(-1) for a in parts])
    return jnp.pad(flat, (0, rows * D_MODEL - flat.shape[0])).reshape(rows, D_MODEL)


def _unpack(packed, shapes):
    flat = packed.reshape(-1)
    out, pos = [], 0
    for s in shapes:
        size = 1
        for d in s:
            size *= d
        out.append(flat[pos:pos + size].reshape(s))
        pos += size
    return out


def kernel(x, c, w_ada, b_ada, norm1_gain, w_in, conv_dw_w, conv_dw_b, conv_ln_g, conv_ln_b, gm_ln_g, gm_ln_b, gm_ws, gm_bs, mix_out_gain, w_out, norm2_gain, w_up, ffn_dw_w, ffn_dw_b, w_down, final_gain, loss_target, m_w_ada, m_b_ada, m_norm1_gain, m_w_in, m_conv_dw_w, m_conv_dw_b, m_conv_ln_g, m_conv_ln_b, m_gm_ln_g, m_gm_ln_b, m_gm_ws, m_gm_bs, m_mix_out_gain, m_w_out, m_norm2_gain, m_w_up, m_ffn_dw_w, m_ffn_dw_b, m_w_down, m_final_gain, v_w_ada, v_b_ada, v_norm1_gain, v_w_in, v_conv_dw_w, v_conv_dw_b, v_conv_ln_g, v_conv_ln_b, v_gm_ln_g, v_gm_ln_b, v_gm_ws, v_gm_bs, v_mix_out_gain, v_w_out, v_norm2_gain, v_w_up, v_ffn_dw_w, v_ffn_dw_b, v_w_down, v_final_gain):
    weights = dict(w_ada=w_ada, b_ada=b_ada, norm1_gain=norm1_gain, w_in=w_in, conv_dw_w=conv_dw_w, conv_dw_b=conv_dw_b,
                   conv_ln_g=conv_ln_g, conv_ln_b=conv_ln_b, gm_ln_g=gm_ln_g, gm_ln_b=gm_ln_b, gm_ws=gm_ws, gm_bs=gm_bs,
                   mix_out_gain=mix_out_gain, w_out=w_out, norm2_gain=norm2_gain, w_up=w_up, ffn_dw_w=ffn_dw_w,
                   ffn_dw_b=ffn_dw_b, w_down=w_down, final_gain=final_gain)
    mom1 = dict(w_ada=m_w_ada, b_ada=m_b_ada, norm1_gain=m_norm1_gain, w_in=m_w_in, conv_dw_w=m_conv_dw_w,
                conv_dw_b=m_conv_dw_b, conv_ln_g=m_conv_ln_g, conv_ln_b=m_conv_ln_b, gm_ln_g=m_gm_ln_g, gm_ln_b=m_gm_ln_b,
                gm_ws=m_gm_ws, gm_bs=m_gm_bs, mix_out_gain=m_mix_out_gain, w_out=m_w_out, norm2_gain=m_norm2_gain,
                w_up=m_w_up, ffn_dw_w=m_ffn_dw_w, ffn_dw_b=m_ffn_dw_b, w_down=m_w_down, final_gain=m_final_gain)
    mom2 = dict(w_ada=v_w_ada, b_ada=v_b_ada, norm1_gain=v_norm1_gain, w_in=v_w_in, conv_dw_w=v_conv_dw_w,
                conv_dw_b=v_conv_dw_b, conv_ln_g=v_conv_ln_g, conv_ln_b=v_conv_ln_b, gm_ln_g=v_gm_ln_g, gm_ln_b=v_gm_ln_b,
                gm_ws=v_gm_ws, gm_bs=v_gm_bs, mix_out_gain=v_mix_out_gain, w_out=v_w_out, norm2_gain=v_norm2_gain,
                w_up=v_w_up, ffn_dw_w=v_ffn_dw_w, ffn_dw_b=v_ffn_dw_b, w_down=v_w_down, final_gain=v_final_gain)
    shard = 2 * lax.axis_index("x") + lax.axis_index("y")
    me = 2 * shard + lax.axis_index("c")

    ada_cols = w_ada.shape[2]
    b_ada_sh = lax.dynamic_slice(b_ada, (0, shard * ada_cols), (1, ada_cols))
    c_all64, mod32 = _ada_mod(c, w_ada[0], b_ada_sh)
    c_all = c_all64[::8]
    mod = mod32[::8].reshape(1, N_SHARD * ada_cols)

    (w_in_g, w_out_g, w_up_g, w_down_g), (conv_w_g, ffn_w_g) = _gather_weights(
        [w_in[0], w_out[0], w_up[0], w_down[0]], [conv_dw_w[0], ffn_dw_w[0]])
    conv_w_full = jnp.transpose(conv_w_g, (1, 0, 2)).reshape(CONV_K, D_HALF)
    ffn_w_full = jnp.transpose(ffn_w_g, (1, 0, 2)).reshape(FFN_K, 2 * D_FF)

    p = dict(norm1_gain=norm1_gain, conv_dw_w=conv_w_full, conv_dw_b=conv_dw_b, conv_ln_g=conv_ln_g,
             conv_ln_b=conv_ln_b, gm_ln_g=gm_ln_g, gm_ln_b=gm_ln_b, gm_ws=gm_ws[0], gm_bs=gm_bs[0],
             mix_out_gain=mix_out_gain, norm2_gain=norm2_gain, ffn_dw_w=ffn_w_full, ffn_dw_b=ffn_dw_b,
             final_gain=final_gain[None])
    grad_x, g, d_mod, loss = _local_step(
        x[0], loss_target[0], mod, p, w_in_g, w_out_g.reshape(D_MODEL, D_MODEL), w_up_g,
        w_down_g.reshape(D_FF, D_MODEL))

    n_mod = d_mod.shape[1]
    dmod_rows = lax.dynamic_update_slice(jnp.zeros((N_DEV, n_mod), F32), d_mod, (me, 0))
    g["b_ada"] = d_mod
    small = _pack([g[k] for k in SMALL_REPLICATED] + [g[k] for k in SMALL_SHARDED] + [dmod_rows, loss[0, :1]], PACK_ROWS)
    big = [g["w_in"], g["w_out"].reshape(N_SHARD, -1, D_MODEL), g["w_up"], g["w_down"].reshape(N_SHARD, -1, D_MODEL)]
    pos = jnp.stack(_coords()).astype(jnp.int32)
    to_send = big
    for axis in ("c", "x", "y"):
        recvs, small_recv = _reduce_exchange(axis, to_send, small)
        big, to_send, small = _reduce_add(axis, pos, big, recvs, small, small_recv)
    full = _swap_halves(big)
    grads = dict(w_in=full[0].reshape(w_in.shape[1:]), w_out=full[1].reshape(w_out.shape[1:]),
                 w_up=full[2].reshape(w_up.shape[1:]), w_down=full[3].reshape(w_down.shape[1:]))

    small_shapes = ([weights[k].shape for k in SMALL_REPLICATED] + [(CONV_K, D_HALF), (FFN_K, 2 * D_FF)]
                    + [(N_DEV, n_mod), (1,)])
    *small_grads, conv_w_grad, ffn_w_grad, dmod_all, loss_sum = _unpack(small, small_shapes)
    grads.update(zip(SMALL_REPLICATED, small_grads))
    grads["conv_dw_w"] = lax.dynamic_slice(conv_w_grad, (0, shard * conv_dw_w.shape[2]), conv_dw_w.shape[1:])[None]
    grads["ffn_dw_w"] = lax.dynamic_slice(ffn_w_grad, (0, shard * ffn_dw_w.shape[2]), ffn_dw_w.shape[1:])[None]

    delta, new_m, new_v = {}, {}, {}
    for name, block_rows in (("w_in", 256), ("w_out", 128), ("w_up", 256), ("w_down", 352)):
        delta[name], new_m[name], new_v[name] = [a[None] for a in _adamw(
            "adamw_" + name, weights[name][0], grads[name], mom1[name][0], mom2[name][0], block_rows)]
        grads[name] = grads[name][None]
    dmod_sh = lax.dynamic_slice(dmod_all, (0, shard * ada_cols), (N_DEV, ada_cols))
    pad8 = ((0, 16 - N_DEV), (0, 0))
    grads["w_ada"], delta["w_ada"], new_m["w_ada"], new_v["w_ada"] = [a[None] for a in _adamw_ada(
        jnp.pad(c_all, pad8), jnp.pad(dmod_sh, pad8), w_ada[0], m_w_ada[0], v_w_ada[0], 256)]
    small_names = SMALL_REPLICATED + SMALL_SHARDED
    packed = [_pack([d[k] for k in small_names], ADAM_PACK_ROWS) for d in (weights, grads, mom1, mom2)]
    small_out = _adamw("adamw_small", *packed, ADAM_PACK_ROWS)
    for d, arr in zip((delta, new_m, new_v), small_out):
        d.update(zip(small_names, _unpack(arr, [weights[k].shape for k in small_names])))

    return (loss_sum.reshape(()), grad_x[None], *[grads[k] for k in WEIGHT_ORDER], *[delta[k] for k in WEIGHT_ORDER],
            *[new_m[k] for k in WEIGHT_ORDER], *[new_v[k] for k in WEIGHT_ORDER])
```

```python
import functools

import jax
import jax.numpy as jnp
from jax import lax
from jax.experimental import pallas as pl
from jax.experimental.pallas import tpu as pltpu

F32 = jnp.float32
BF16 = jnp.bfloat16

D_MODEL = 1024
D_HALF = 512
D_FF = 2816
CONV_K = 31
FFN_K = 3
CHUNK = 128
N_HEADS = 8
HEAD_DIM = 64
N_SHARD = 4
N_DEV = 8
RMS_EPS = 1e-6
LN_EPS = 1e-5
ADAM_LR, ADAM_B1, ADAM_B2, ADAM_EPS, ADAM_WD, ADAM_STEP = 0.001, 0.9, 0.999, 1e-08, 0.01, 10

TILE = 256
HALO = 32
FFN_HALO = 8
FFN_BLK = 256
UP_SHARD = 2 * D_FF // N_SHARD
VMEM_LIMIT_BYTES = 56 * 1024 * 1024

ANY = pl.BlockSpec(memory_space=pl.ANY)
NT_DIMS = (((1,), (1,)), ((), ()))
TN_DIMS = (((0,), (0,)), ((), ()))


def _full(shape):
    return pl.BlockSpec(shape, lambda i: (0,) * len(shape))


def _nn(a, b):
    return jnp.dot(a, b, preferred_element_type=F32)


def _nt(a, b):
    return lax.dot_general(a, b, NT_DIMS, preferred_element_type=F32)


def _tn(a, b):
    return lax.dot_general(a, b, TN_DIMS, preferred_element_type=F32)


def _colsum(a):
    return jnp.sum(a, axis=0, keepdims=True)


def _params(semantics=("arbitrary",)):
    return pltpu.CompilerParams(dimension_semantics=semantics, vmem_limit_bytes=VMEM_LIMIT_BYTES)


def _rms(v, gain):
    return v * lax.rsqrt(jnp.mean(v * v, axis=-1, keepdims=True) + RMS_EPS) * gain


def _layer_norm(v, gain, bias):
    mu = jnp.mean(v, axis=-1, keepdims=True)
    var = jnp.mean(jnp.square(v - mu), axis=-1, keepdims=True)
    return (v - mu) * lax.rsqrt(var + LN_EPS) * gain + bias


def _mod_norm(v, gain, scale, shift):
    return _rms(v, gain) * (1.0 + scale) + shift


def _conv_branch(a1, ln_g, ln_b, out_gain):
    a2 = _layer_norm(a1, ln_g, ln_b)
    return _rms(a2 * jax.nn.sigmoid(a2), out_gain)


def _gate_branch(gu, sp, out_gain):
    return _rms(jax.nn.gelu(gu) * sp, out_gain)


def _gv_norm(gv, ln_g, ln_b):
    return _layer_norm(jax.nn.gelu(gv), ln_g, ln_b)


def _head_pair_matmul(wp_ref, v):
    lane = lax.broadcasted_iota(jnp.int32, (CHUNK, CHUNK), 1)
    rows = []
    for n in range(v.shape[0] // CHUNK):
        cols = []
        for j in range(N_HEADS // 2):
            r = _nn(wp_ref[j], v[n * CHUNK:(n + 1) * CHUNK, j * CHUNK:(j + 1) * CHUNK])
            cols.append(jnp.where(lane < HEAD_DIM, r[:CHUNK], r[CHUNK:]))
        rows.append(jnp.concatenate(cols, axis=1))
    return jnp.concatenate(rows, axis=0)


def _tile_bias(bs, tokens):
    return jnp.concatenate([bs] * (tokens // CHUNK), axis=0)


def _fwd_mixer_up(x, vec, conv_w, wpair, bs_full, w_in_g, w_out_g, w_up_g, u_dtype):
    seq = x.shape[0]
    n_tiles = seq // TILE
    t = TILE
    names = ["norm1_gain", "sc1", "sh1", "gt1", "norm2_gain", "sc2", "sh2", "conv_dw_b", "conv_ln_g", "conv_ln_b",
             "gm_ln_g", "gm_ln_b", "mix_out_gain"]
    vecs = [vec[k] for k in names]

    def body(x_ref, g1, sc1, sh1, gt1, g2, sc2, sh2, cb, clg, clb, vg, vb, mg, cw, wp, bs, win_hbm, wout_hbm, wup_hbm,
             z_ref, a1_ref, sp_ref, y_ref, o1_ref, x2_ref, u_ref, win_v, wout_v, wup_v, halo, bank, sem):
        i = pl.program_id(0)

        @pl.when(i == 0)
        def _():
            cps = [pltpu.make_async_copy(win_hbm, win_v, sem.at[0]),
                   pltpu.make_async_copy(wout_hbm, wout_v, sem.at[1]),
                   pltpu.make_async_copy(wup_hbm, wup_v, sem.at[2])]
            for cp in cps:
                cp.start()
            for cp in cps:
                cp.wait()
            halo[...] = jnp.zeros_like(halo)

        xv = x_ref[...]
        h1b = _mod_norm(xv, g1[...], sc1[...], sh1[...]).astype(BF16)
        zs = [_nn(h1b, win_v[k]) for k in range(N_SHARD)]
        for k in range(N_SHARD):
            z_ref[:, k * D_HALF:(k + 1) * D_HALF] = zs[k]
        ca, cg, gu, gv = zs
        a0 = ca * jax.nn.sigmoid(cg)
        ext = jnp.concatenate([halo[...], a0], axis=0)
        halo[...] = a0[t - HALO:]
        bank[0] = ext
        for b in range(1, 8):
            bank[b] = pltpu.roll(ext, b, axis=0)
        a1 = jnp.zeros((t, D_HALF), F32) + cb[...]
        for s in range(CONV_K):
            q, b = divmod(s, 8)
            a1 = a1 + bank[b, pl.ds(HALO - 8 * q, t), :] * cw[pl.ds(CONV_K - 1 - s, 1), :]
        a1_ref[...] = a1
        mgv = mg[...]
        ya = _conv_branch(a1, clg[...], clb[...], mgv[:, :D_HALF])
        gvn = _gv_norm(gv, vg[...], vb[...]).astype(BF16)
        sp = _head_pair_matmul(wp, gvn) + _tile_bias(bs[...], t)
        sp_ref[...] = sp
        yg = _gate_branch(gu, sp, mgv[:, D_HALF:])
        yb = jnp.concatenate([ya, yg], axis=1).astype(BF16)
        y_ref[...] = yb
        o1 = _nn(yb, wout_v[...])
        o1_ref[...] = o1
        x2 = xv + gt1[...] * o1
        x2_ref[...] = x2
        h2b = _mod_norm(x2, g2[...], sc2[...], sh2[...]).astype(BF16)
        for k in range(N_SHARD):
            u_ref[:, k * UP_SHARD:(k + 1) * UP_SHARD] = _nn(h2b, wup_v[k]).astype(u_ref.dtype)

    def row(width):
        return pl.BlockSpec((t, width), lambda i: (i, 0))

    out_shape = [jax.ShapeDtypeStruct((seq, 4 * D_HALF), F32), jax.ShapeDtypeStruct((seq, D_HALF), F32),
                 jax.ShapeDtypeStruct((seq, D_HALF), F32), jax.ShapeDtypeStruct((seq, D_MODEL), BF16),
                 jax.ShapeDtypeStruct((seq, D_MODEL), F32), jax.ShapeDtypeStruct((seq, D_MODEL), F32),
                 jax.ShapeDtypeStruct((seq, 2 * D_FF), u_dtype)]
    return pl.pallas_call(
        body, grid=(n_tiles,), name="fwd_mixer_up",
        in_specs=[row(D_MODEL)] + [_full(v.shape) for v in vecs]
        + [_full(conv_w.shape), _full(wpair.shape), _full(bs_full.shape), ANY, ANY, ANY],
        out_specs=[row(4 * D_HALF), row(D_HALF), row(D_HALF), row(D_MODEL), row(D_MODEL), row(D_MODEL), row(2 * D_FF)],
        out_shape=out_shape,
        scratch_shapes=[pltpu.VMEM(w_in_g.shape, BF16), pltpu.VMEM(w_out_g.shape, BF16), pltpu.VMEM(w_up_g.shape, BF16),
                        pltpu.VMEM((HALO, D_HALF), F32), pltpu.VMEM((8, t + HALO, D_HALF), F32),
                        pltpu.SemaphoreType.DMA((3,))],
        compiler_params=_params(),
    )(x, *vecs, conv_w, wpair, bs_full, w_in_g, w_out_g, w_up_g)


def _ffn_tail(u, x2, target, ffn_w, ffn_b, gt2, final_gain, w_down_g):
    seq = x2.shape[0]
    n_tiles = seq // TILE
    t = TILE
    n_blk = D_FF // FFN_BLK
    halo_rows = 16 if u.dtype == BF16 else 8
    inv_d = 1.0 / D_MODEL

    def final_norm(x3, gain):
        return _rms(x3, gain)

    def body(u_ref, uh_ref, x2_ref, tgt_ref, fw, fb, gt2_ref, fg, wd_hbm,
             du_ref, dx3_ref, dfw_ref, dfb_ref, dfg_ref, dgt2_ref, loss_ref, dwd_hbm,
             wd_v, dwd_acc, carry, val_s, sil_s, dsil_s, f_s, sem):
        i = pl.program_id(0)
        tile = n_tiles - 1 - i

        @pl.when(i == 0)
        def _():
            cp = pltpu.make_async_copy(wd_hbm, wd_v, sem.at[0])
            cp.start()
            cp.wait()
            dwd_acc[...] = jnp.zeros_like(dwd_acc)
            carry[...] = jnp.zeros_like(carry)
            dfw_ref[...] = jnp.zeros_like(dfw_ref)
            dfb_ref[...] = jnp.zeros_like(dfb_ref)
            dfg_ref[...] = jnp.zeros_like(dfg_ref)
            dgt2_ref[...] = jnp.zeros_like(dgt2_ref)
            loss_ref[...] = jnp.zeros_like(loss_ref)

        def cols_of(j):
            return pl.ds(j * FFN_BLK, FFN_BLK), pl.ds(D_FF + j * FFN_BLK, FFN_BLK)

        def conv(cols):
            prev = uh_ref[pl.ds(halo_rows - FFN_HALO, FFN_HALO), cols].astype(F32)
            ext = jnp.concatenate([jnp.where(tile > 0, prev, 0.0), u_ref[:, cols].astype(F32)], axis=0)
            acc = fb[:, cols] + ext[FFN_HALO:] * fw[pl.ds(FFN_K - 1, 1), cols]
            for s in range(1, FFN_K):
                acc = acc + pltpu.roll(ext, s, axis=0)[FFN_HALO:] * fw[pl.ds(FFN_K - 1 - s, 1), cols]
            return acc

        o2 = jnp.zeros((t, D_MODEL), F32)
        for j in range(n_blk):
            cv, cg = cols_of(j)
            val, gate = conv(cv), conv(cg)
            sig = jax.nn.sigmoid(gate)
            sil = gate * sig
            fb16 = (sil * val).astype(BF16)
            val_s[:, cv] = val
            sil_s[:, cv] = sil
            dsil_s[:, cv] = sig + sil * (1.0 - sig)
            f_s[:, cv] = fb16
            o2 = o2 + _nn(fb16, wd_v[pl.ds(j * FFN_BLK, FFN_BLK), :])

        gt2v = gt2_ref[...]
        x3 = x2_ref[...] + gt2v * o2
        out, out_vjp = jax.vjp(final_norm, x3, fg[...])
        diff = out - tgt_ref[...]
        loss_ref[...] += jnp.zeros_like(loss_ref) + 0.5 * inv_d * jnp.sum(diff * diff)
        dx3, dfg = out_vjp(diff * inv_d)
        dfg_ref[...] += dfg
        dgt2_ref[...] += _colsum(dx3 * o2)
        dx3_ref[...] = dx3
        do2b = (gt2v * dx3).astype(BF16)

        for j in range(n_blk):
            cv, cg = cols_of(j)
            rows = pl.ds(j * FFN_BLK, FFN_BLK)
            df = _nt(do2b, wd_v[rows, :])
            dval = df * sil_s[:, cv]
            dgate = df * val_s[:, cv] * dsil_s[:, cv]
            dwd_acc[rows, :] += _tn(f_s[:, cv], do2b)
            for dd, cols in ((dval, cv), (dgate, cg)):
                dfb_ref[:, cols] += _colsum(dd)
                ext = jnp.concatenate([dd, carry[:, cols]], axis=0)
                uv = u_ref[:, cols].astype(F32)
                du = dd * fw[pl.ds(FFN_K - 1, 1), cols]
                dfw_ref[pl.ds(FFN_K - 1, 1), cols] += _colsum(dd * uv)
                for s in range(1, FFN_K):
                    shifted = pltpu.roll(ext, t + FFN_HALO - s, axis=0)[:t]
                    du = du + shifted * fw[pl.ds(FFN_K - 1 - s, 1), cols]
                    dfw_ref[pl.ds(FFN_K - 1 - s, 1), cols] += _colsum(shifted * uv)
                carry[:, cols] = dd[:FFN_HALO]
                du_ref[:, cols] = du.astype(BF16)

        @pl.when(i == n_tiles - 1)
        def _():
            cp = pltpu.make_async_copy(dwd_acc, dwd_hbm, sem.at[1])
            cp.start()
            cp.wait()

    def rev(width):
        return pl.BlockSpec((t, width), lambda i: (n_tiles - 1 - i, 0))

    halo_spec = pl.BlockSpec(
        (halo_rows, 2 * D_FF), lambda i: (jnp.maximum((n_tiles - 1 - i) * (t // halo_rows) - 1, 0), 0))
    out_shape = [jax.ShapeDtypeStruct((seq, 2 * D_FF), BF16), jax.ShapeDtypeStruct((seq, D_MODEL), F32),
                 jax.ShapeDtypeStruct((FFN_K, 2 * D_FF), F32), jax.ShapeDtypeStruct((1, 2 * D_FF), F32),
                 jax.ShapeDtypeStruct((1, D_MODEL), F32), jax.ShapeDtypeStruct((1, D_MODEL), F32),
                 jax.ShapeDtypeStruct((1, 128), F32), jax.ShapeDtypeStruct((D_FF, D_MODEL), F32)]
    return pl.pallas_call(
        body, grid=(n_tiles,), name="ffn_tail",
        in_specs=[rev(2 * D_FF), halo_spec, rev(D_MODEL), rev(D_MODEL), _full(ffn_w.shape), _full(ffn_b.shape),
                  _full(gt2.shape), _full(final_gain.shape), ANY],
        out_specs=[rev(2 * D_FF), rev(D_MODEL), _full((FFN_K, 2 * D_FF)), _full((1, 2 * D_FF)), _full((1, D_MODEL)),
                   _full((1, D_MODEL)), _full((1, 128)), ANY],
        out_shape=out_shape,
        scratch_shapes=[pltpu.VMEM((D_FF, D_MODEL), BF16), pltpu.VMEM((D_FF, D_MODEL), F32),
                        pltpu.VMEM((FFN_HALO, 2 * D_FF), F32),
                        pltpu.VMEM((t, D_FF), F32), pltpu.VMEM((t, D_FF), F32), pltpu.VMEM((t, D_FF), F32),
                        pltpu.VMEM((t, D_FF), BF16), pltpu.SemaphoreType.DMA((2,))],
        compiler_params=_params(),
    )(u, u, x2, target, ffn_w, ffn_b, gt2, final_gain, w_down_g)


def _bwd_up(du, x2, dx3, norm2_gain, sc2, sh2, w_up_g):
    seq = x2.shape[0]
    n_tiles = seq // TILE
    t = TILE

    def body(du_ref, x2_ref, dx3_ref, g2, sc2_ref, sh2_ref, wup_hbm, dx2_ref, dg2_ref, dsc2_ref, dsh2_ref, dwup_hbm,
             wup_v, dwup_acc, sem):
        i = pl.program_id(0)

        @pl.when(i == 0)
        def _():
            cp = pltpu.make_async_copy(wup_hbm, wup_v, sem.at[0])
            cp.start()
            cp.wait()
            dwup_acc[...] = jnp.zeros_like(dwup_acc)
            dg2_ref[...] = jnp.zeros_like(dg2_ref)
            dsc2_ref[...] = jnp.zeros_like(dsc2_ref)
            dsh2_ref[...] = jnp.zeros_like(dsh2_ref)

        h2, h2_vjp = jax.vjp(_mod_norm, x2_ref[...], g2[...], sc2_ref[...], sh2_ref[...])
        h2b = h2.astype(BF16)
        dh2 = jnp.zeros((t, D_MODEL), F32)
        for k in range(N_SHARD):
            dub = du_ref[:, k * UP_SHARD:(k + 1) * UP_SHARD]
            dh2 = dh2 + _nt(dub, wup_v[k])
            dwup_acc[k] += _tn(h2b, dub)
        dx2, dg2, dsc2, dsh2 = h2_vjp(dh2)
        dx2_ref[...] = dx3_ref[...] + dx2
        dg2_ref[...] += dg2
        dsc2_ref[...] += dsc2
        dsh2_ref[...] += dsh2

        @pl.when(i == n_tiles - 1)
        def _():
            cp = pltpu.make_async_copy(dwup_acc, dwup_hbm, sem.at[1])
            cp.start()
            cp.wait()

    def row(width):
        return pl.BlockSpec((t, width), lambda i: (i, 0))

    vec = jax.ShapeDtypeStruct((1, D_MODEL), F32)
    return pl.pallas_call(
        body, grid=(n_tiles,), name="bwd_up",
        in_specs=[row(2 * D_FF), row(D_MODEL), row(D_MODEL), _full((1, D_MODEL)), _full((1, D_MODEL)),
                  _full((1, D_MODEL)), ANY],
        out_specs=[row(D_MODEL), _full((1, D_MODEL)), _full((1, D_MODEL)), _full((1, D_MODEL)), ANY],
        out_shape=[jax.ShapeDtypeStruct((seq, D_MODEL), F32), vec, vec, vec,
                   jax.ShapeDtypeStruct(w_up_g.shape, F32)],
        scratch_shapes=[pltpu.VMEM(w_up_g.shape, BF16), pltpu.VMEM(w_up_g.shape, F32), pltpu.SemaphoreType.DMA((2,))],
        compiler_params=_params(),
    )(du, x2, dx3, norm2_gain, sc2, sh2, w_up_g)


def _bwd_mixer(dx2, x, z, a1, sp, yb, o1, vec, conv_w, wpair, wpair_t, causal_mask, w_in_g, w_out_g):
    seq = x.shape[0]
    n_tiles = seq // TILE
    t = TILE
    names = ["norm1_gain", "sc1", "sh1", "gt1", "conv_ln_g", "conv_ln_b", "gm_ln_g", "gm_ln_b", "mix_out_gain"]
    vecs = [vec[k] for k in names]

    def body(dx2_ref, x_ref, z_ref, a1_ref, sp_ref, y_ref, o1_ref, g1, sc1, sh1, gt1, clg, clb, vg, vb, mg,
             cw, wp, wpt, mask_ref, win_hbm, wout_hbm,
             gx_ref, dg1_ref, dsc1_ref, dsh1_ref, dgt1_ref, dcw_ref, dcb_ref, dclg_ref, dclb_ref, dvg_ref, dvb_ref,
             dmg_ref, dws_ref, dbs_ref, dwin_hbm, dwout_hbm,
             win_v, wout_v, dwin_acc, dwout_acc, carry, bank, dbs_acc, sem):
        i = pl.program_id(0)
        small = [dg1_ref, dsc1_ref, dsh1_ref, dgt1_ref, dcw_ref, dcb_ref, dclg_ref, dclb_ref, dvg_ref, dvb_ref,
                 dmg_ref, dws_ref, dbs_acc]

        @pl.when(i == 0)
        def _():
            cps = [pltpu.make_async_copy(win_hbm, win_v, sem.at[0]),
                   pltpu.make_async_copy(wout_hbm, wout_v, sem.at[1])]
            for cp in cps:
                cp.start()
            for cp in cps:
                cp.wait()
            dwin_acc[...] = jnp.zeros_like(dwin_acc)
            dwout_acc[...] = jnp.zeros_like(dwout_acc)
            carry[...] = jnp.zeros_like(carry)
            for ref in small:
                ref[...] = jnp.zeros_like(ref)

        dx2v = dx2_ref[...]
        gt1v = gt1[...]
        dgt1_ref[...] += _colsum(dx2v * o1_ref[...])
        do1b = (gt1v * dx2v).astype(BF16)
        dy = _nt(do1b, wout_v[...])
        dwout_acc[...] += _tn(y_ref[...], do1b)

        mgv = mg[...]
        _, conv_vjp = jax.vjp(_conv_branch, a1_ref[...], clg[...], clb[...], mgv[:, :D_HALF])
        da1, dclg, dclb, dmg_a = conv_vjp(dy[:, :D_HALF])
        dclg_ref[...] += dclg
        dclb_ref[...] += dclb
        gu = z_ref[:, 2 * D_HALF:3 * D_HALF]
        gv = z_ref[:, 3 * D_HALF:]
        spv = sp_ref[...]
        _, gate_vjp = jax.vjp(_gate_branch, gu, spv, mgv[:, D_HALF:])
        dgu, dsp, dmg_g = gate_vjp(dy[:, D_HALF:])
        dmg_ref[...] += jnp.concatenate([dmg_a, dmg_g], axis=1)
        gvn, gv_vjp = jax.vjp(_gv_norm, gv, vg[...], vb[...])
        gvnb = gvn.astype(BF16)
        dspb = dsp.astype(BF16)
        dgvn = _head_pair_matmul(wpt, dspb)
        dgv, dvg, dvb = gv_vjp(dgvn)
        dvg_ref[...] += dvg
        dvb_ref[...] += dvb
        lane = lax.broadcasted_iota(jnp.int32, (CHUNK, CHUNK), 1)
        dbs = jnp.zeros((CHUNK, D_HALF), F32)
        for n in range(t // CHUNK):
            rows = slice(n * CHUNK, (n + 1) * CHUNK)
            dbs = dbs + dsp[rows, :]
            for j in range(N_HEADS // 2):
                cols = slice(j * CHUNK, (j + 1) * CHUNK)
                blk = dspb[rows, cols]
                zero = jnp.zeros_like(blk)
                vblk = gvnb[rows, cols]
                dws_ref[2 * j] += _nt(jnp.where(lane < HEAD_DIM, blk, zero), vblk)
                dws_ref[2 * j + 1] += _nt(jnp.where(lane < HEAD_DIM, zero, blk), vblk)
        dbs_acc[...] += dbs

        ca = z_ref[:, :D_HALF]
        cg = z_ref[:, D_HALF:2 * D_HALF]
        sig = jax.nn.sigmoid(cg)
        a0 = ca * sig
        ext = jnp.concatenate([da1, carry[...]], axis=0)
        carry[...] = da1[:HALO]
        bank[0] = ext
        for b in range(1, 8):
            bank[b] = pltpu.roll(ext, t + HALO - b, axis=0)
        dcb_ref[...] += _colsum(da1)
        da0 = jnp.zeros((t, D_HALF), F32)
        for s in range(CONV_K):
            q, b = divmod(s, 8)
            shifted = bank[b, pl.ds(8 * q, t), :]
            da0 = da0 + shifted * cw[pl.ds(CONV_K - 1 - s, 1), :]
            dcw_ref[pl.ds(CONV_K - 1 - s, 1), :] += _colsum(shifted * a0)
        dca = da0 * sig
        dcg = da0 * ca * sig * (1.0 - sig)

        h1, h1_vjp = jax.vjp(_mod_norm, x_ref[...], g1[...], sc1[...], sh1[...])
        h1b = h1.astype(BF16)
        dh1 = jnp.zeros((t, D_MODEL), F32)
        for k, dzk in enumerate((dca, dcg, dgu, dgv)):
            dzb = dzk.astype(BF16)
            dh1 = dh1 + _nt(dzb, win_v[k])
            dwin_acc[k] += _tn(h1b, dzb)
        dx, dg1, dsc1, dsh1 = h1_vjp(dh1)
        gx_ref[...] = dx2v + dx
        dg1_ref[...] += dg1
        dsc1_ref[...] += dsc1
        dsh1_ref[...] += dsh1

        @pl.when(i == n_tiles - 1)
        def _():
            for h in range(N_HEADS):
                dws_ref[h] = dws_ref[h] * mask_ref[...]
            head_of_lane = lax.broadcasted_iota(jnp.int32, (N_HEADS, D_HALF), 1) // HEAD_DIM
            pick = (head_of_lane == lax.broadcasted_iota(jnp.int32, (N_HEADS, D_HALF), 0)).astype(F32)
            dbs_ref[...] = lax.dot_general(pick, dbs_acc[...], NT_DIMS, precision=lax.Precision.HIGHEST,
                                           preferred_element_type=F32)
            cps = [pltpu.make_async_copy(dwin_acc, dwin_hbm, sem.at[2]),
                   pltpu.make_async_copy(dwout_acc, dwout_hbm, sem.at[3])]
            for cp in cps:
                cp.start()
            for cp in cps:
                cp.wait()

    def rev(width):
        return pl.BlockSpec((t, width), lambda i: (n_tiles - 1 - i, 0))

    v1024 = jax.ShapeDtypeStruct((1, D_MODEL), F32)
    v512 = jax.ShapeDtypeStruct((1, D_HALF), F32)
    small_shapes = [v1024, v1024, v1024, v1024, jax.ShapeDtypeStruct((CONV_K, D_HALF), F32), v512, v512, v512, v512,
                    v512, v1024, jax.ShapeDtypeStruct((N_HEADS, CHUNK, CHUNK), F32),
                    jax.ShapeDtypeStruct((N_HEADS, CHUNK), F32)]
    return pl.pallas_call(
        body, grid=(n_tiles,), name="bwd_mixer",
        in_specs=[rev(D_MODEL), rev(D_MODEL), rev(4 * D_HALF), rev(D_HALF), rev(D_HALF), rev(D_MODEL),
                  rev(D_MODEL)] + [_full(v.shape) for v in vecs]
        + [_full(conv_w.shape), _full(wpair.shape), _full(wpair_t.shape), _full(causal_mask.shape), ANY, ANY],
        out_specs=[rev(D_MODEL)] + [_full(s.shape) for s in small_shapes] + [ANY, ANY],
        out_shape=[jax.ShapeDtypeStruct((seq, D_MODEL), F32)] + small_shapes
        + [jax.ShapeDtypeStruct(w_in_g.shape, F32), jax.ShapeDtypeStruct(w_out_g.shape, F32)],
        scratch_shapes=[pltpu.VMEM(w_in_g.shape, BF16), pltpu.VMEM(w_out_g.shape, BF16),
                        pltpu.VMEM(w_in_g.shape, F32), pltpu.VMEM(w_out_g.shape, F32),
                        pltpu.VMEM((HALO, D_HALF), F32), pltpu.VMEM((8, t + HALO, D_HALF), F32),
                        pltpu.VMEM((CHUNK, D_HALF), F32), pltpu.SemaphoreType.DMA((4,))],
        compiler_params=_params(),
    )(dx2, x, z, a1, sp, yb, o1, *vecs, conv_w, wpair, wpair_t, causal_mask, w_in_g, w_out_g)


def _gmlp_operands(gm_ws, gm_bs):
    mask = jnp.tril(jnp.ones((CHUNK, CHUNK), F32))
    ws = gm_ws * mask[None]
    wpair = ws.reshape(N_HEADS // 2, 2 * CHUNK, CHUNK).astype(BF16)
    wpair_t = jnp.swapaxes(ws, 1, 2).reshape(N_HEADS // 2, 2 * CHUNK, CHUNK).astype(BF16)
    bs_full = jnp.repeat(jnp.transpose(gm_bs), HEAD_DIM, axis=1)
    return wpair, wpair_t, bs_full, mask


def _local_step(x, target, mod, p, w_in_g, w_out_g, w_up_g, w_down_g, u_dtype=F32):
    sh1, sc1, gt1, sh2, sc2, gt2 = [mod[:, k * D_MODEL:(k + 1) * D_MODEL] for k in range(6)]
    vec = dict(p, sh1=sh1, sc1=sc1, gt1=gt1, sh2=sh2, sc2=sc2, gt2=gt2)
    wpair, wpair_t, bs_full, mask = _gmlp_operands(p["gm_ws"], p["gm_bs"])

    z, a1, sp, yb, o1, x2, u = _fwd_mixer_up(x, vec, p["conv_dw_w"], wpair, bs_full, w_in_g, w_out_g, w_up_g, u_dtype)
    du, dx3, d_ffn_w, d_ffn_b, d_fg, d_gt2, loss, d_wd = _ffn_tail(
        u, x2, target, p["ffn_dw_w"], p["ffn_dw_b"], gt2, p["final_gain"], w_down_g)
    dx2, d_g2, d_sc2, d_sh2, d_wup = _bwd_up(du, x2, dx3, p["norm2_gain"], sc2, sh2, w_up_g)
    (gx, d_g1, d_sc1, d_sh1, d_gt1, d_cw, d_cb, d_clg, d_clb, d_vg, d_vb, d_mg, d_ws, d_bs, d_win,
     d_wout) = _bwd_mixer(dx2, x, z, a1, sp, yb, o1, vec, p["conv_dw_w"], wpair, wpair_t, mask, w_in_g, w_out_g)
    d_mod = jnp.concatenate([d_sh1, d_sc1, d_gt1, d_sh2, d_sc2, d_gt2], axis=1)
    grads = dict(norm1_gain=d_g1, conv_dw_w=d_cw, conv_dw_b=d_cb, conv_ln_g=d_clg, conv_ln_b=d_clb, gm_ln_g=d_vg,
                 gm_ln_b=d_vb, gm_ws=d_ws, gm_bs=d_bs, mix_out_gain=d_mg, norm2_gain=d_g2, ffn_dw_w=d_ffn_w,
                 ffn_dw_b=d_ffn_b, final_gain=d_fg, w_in=d_win, w_out=d_wout, w_up=d_wup, w_down=d_wd)
    return gx, grads, d_mod, loss


MESH = pl.DeviceIdType.MESH
VMEM_SPEC = pl.BlockSpec(memory_space=pltpu.VMEM)
PEER_FLIPS = [(a, b, d) for a in (0, 1) for b in (0, 1) for d in (0, 1)][1:]
CHIP_FLIPS = [(1, 0), (0, 1), (1, 1)]


def _coords():
    return lax.axis_index("x"), lax.axis_index("y"), lax.axis_index("c")


def _flip(v, bit):
    return 1 - v if bit else v


def _rows8(block):
    return pl.ds(pl.multiple_of(8 * block, 8), 8)


def _ada_mod(c_row, w_ada_sh, b_ada_sh):
    cols = w_ada_sh.shape[1]

    def body(c_ref, w_ref, b_ref, call_ref, mod_ref, cpad, modall, send_sems, recv_sems):
        x, y, c = _coords()
        me = 4 * x + 2 * y + c
        cpad[...] = jnp.zeros_like(cpad)
        cpad[pl.ds(0, 1), :] = c_ref[...]

        def gather_copy(j, flip):
            peer = (_flip(x, flip[0]), _flip(y, flip[1]), _flip(c, flip[2]))
            return pltpu.make_async_remote_copy(
                src_ref=cpad, dst_ref=call_ref.at[_rows8(me)], send_sem=send_sems.at[j], recv_sem=recv_sems.at[j],
                device_id=peer, device_id_type=MESH)

        copies = [gather_copy(j, f) for j, f in enumerate(PEER_FLIPS)]
        for cp in copies:
            cp.start()
        call_ref[_rows8(me), :] = cpad[...]
        for cp in copies:
            cp.wait_recv()
        for cp in copies:
            cp.wait_send()
        cv = call_ref[...]
        c_act = (cv * jax.nn.sigmoid(cv)).astype(BF16)
        modall[...] = _nn(c_act, w_ref[...].astype(BF16)) + b_ref[...]

        slot = _rows8(2 * x + y)

        def piece_copy(j, flip):
            tx, ty = _flip(x, flip[0]), _flip(y, flip[1])
            return pltpu.make_async_remote_copy(
                src_ref=modall.at[_rows8(4 * tx + 2 * ty + c)], dst_ref=mod_ref.at[slot],
                send_sem=send_sems.at[len(PEER_FLIPS) + j], recv_sem=recv_sems.at[len(PEER_FLIPS) + j],
                device_id=(tx, ty, c), device_id_type=MESH)

        pieces = [piece_copy(j, f) for j, f in enumerate(CHIP_FLIPS)]
        for cp in pieces:
            cp.start()
        mod_ref[slot, :] = modall[_rows8(me), :]
        for cp in pieces:
            cp.wait_recv()
        for cp in pieces:
            cp.wait_send()

    n_sem = len(PEER_FLIPS) + len(CHIP_FLIPS)
    return pl.pallas_call(
        body, name="ada_mod",
        in_specs=[VMEM_SPEC, VMEM_SPEC, VMEM_SPEC], out_specs=[VMEM_SPEC, VMEM_SPEC],
        out_shape=[jax.ShapeDtypeStruct((8 * N_DEV, D_MODEL), F32), jax.ShapeDtypeStruct((8 * N_SHARD, cols), F32)],
        scratch_shapes=[pltpu.VMEM((8, D_MODEL), F32), pltpu.VMEM((8 * N_DEV, cols), F32),
                        pltpu.SemaphoreType.DMA((n_sem,)), pltpu.SemaphoreType.DMA((n_sem,))],
        compiler_params=pltpu.CompilerParams(vmem_limit_bytes=VMEM_LIMIT_BYTES),
    )(c_row, w_ada_sh, b_ada_sh)


def _gather_weights(shards, filters):
    n = len(shards)
    nf = len(filters)

    def body(*refs):
        ins, f_ins = refs[:n], refs[n:n + nf]
        outs, f_outs = refs[n + nf:2 * n + nf], refs[2 * n + nf:2 * (n + nf)]
        stage = refs[2 * (n + nf):3 * n + 2 * nf]
        send_sems, recv_sems, local_sems, f_send_sems, f_recv_sems = refs[3 * n + 2 * nf:]
        x, y, c = _coords()
        k = 2 * x + y
        sibling = (x, y, 1 - c)

        def filter_copy(w, j, slot):
            tx, ty = _flip(x, CHIP_FLIPS[j][0]), _flip(y, CHIP_FLIPS[j][1])
            return pltpu.make_async_remote_copy(
                src_ref=f_ins[w], dst_ref=f_outs[w].at[slot], send_sem=f_send_sems.at[w, j],
                recv_sem=f_recv_sems.at[w, j], device_id=(tx, ty, c), device_id_type=MESH)

        def half(w, which):
            h = shards[w].shape[0] // 2
            return pl.ds(pl.multiple_of(which * h, 16), h)

        def ici_copy(w, j, src, slot):
            tx, ty = _flip(x, CHIP_FLIPS[j][0]), _flip(y, CHIP_FLIPS[j][1])
            return pltpu.make_async_remote_copy(
                src_ref=src, dst_ref=outs[w].at[slot, half(w, c)], send_sem=send_sems.at[w, j],
                recv_sem=recv_sems.at[w, j], device_id=(tx, ty, c), device_id_type=MESH)

        def d2d_copy(w, j, slot, which):
            rows = outs[w].at[slot, half(w, which)]
            return pltpu.make_async_remote_copy(
                src_ref=rows, dst_ref=rows, send_sem=send_sems.at[w, len(CHIP_FLIPS) + j],
                recv_sem=recv_sems.at[w, len(CHIP_FLIPS) + j], device_id=sibling, device_id_type=MESH)

        def chip_of(j):
            return 2 * _flip(x, CHIP_FLIPS[j][0]) + _flip(y, CHIP_FLIPS[j][1])

        local, first, passed = [], [], []
        for w in range(nf):
            local.append(pltpu.make_async_copy(f_ins[w], f_outs[w].at[k], local_sems.at[n + w]))
            local[-1].start()
            for j in range(len(CHIP_FLIPS)):
                first.append(filter_copy(w, j, k))
                first[-1].start()
        for w in range(n):
            stage[w][...] = ins[w][...].astype(BF16)
            local.append(pltpu.make_async_copy(stage[w], outs[w].at[k], local_sems.at[w]))
            local[-1].start()
            for j in range(len(CHIP_FLIPS)):
                first.append(ici_copy(w, j, stage[w].at[half(w, c)], k))
                first[-1].start()
        for w in range(nf):
            for j in range(len(CHIP_FLIPS)):
                filter_copy(w, j, chip_of(j)).wait_recv()
        for w in range(n):
            for j in range(len(CHIP_FLIPS)):
                ici_copy(w, j, stage[w].at[half(w, c)], chip_of(j)).wait_recv()
                passed.append(d2d_copy(w, j, chip_of(j), c))
                passed[-1].start()
        for w in range(n):
            for j in range(len(CHIP_FLIPS)):
                d2d_copy(w, j, chip_of(j), 1 - c).wait_recv()
        for cp in first + passed:
            cp.wait_send()
        for cp in local:
            cp.wait()

    sem_shape = (n, 2 * len(CHIP_FLIPS))
    f_sem_shape = (nf, len(CHIP_FLIPS))
    outs = pl.pallas_call(
        body, name="gather_weights",
        in_specs=[VMEM_SPEC] * (n + nf), out_specs=[ANY] * (n + nf),
        out_shape=[jax.ShapeDtypeStruct((N_SHARD,) + s.shape, BF16) for s in shards]
        + [jax.ShapeDtypeStruct((N_SHARD,) + s.shape, F32) for s in filters],
        scratch_shapes=[pltpu.VMEM(s.shape, BF16) for s in shards]
        + [pltpu.SemaphoreType.DMA(sem_shape), pltpu.SemaphoreType.DMA(sem_shape), pltpu.SemaphoreType.DMA((n + nf,)),
           pltpu.SemaphoreType.DMA(f_sem_shape), pltpu.SemaphoreType.DMA(f_sem_shape)],
        compiler_params=pltpu.CompilerParams(vmem_limit_bytes=VMEM_LIMIT_BYTES),
    )(*shards, *filters)
    return outs[:n], outs[n:]


def _reduce_exchange(axis, big, small):
    n = len(big)

    def send_part(ref, x, y, c):
        lead, rows = ref.shape[0], ref.shape[1]
        if axis == "c":
            h = rows // 2
            return ref.at[:, pl.ds(pl.multiple_of((1 - c) * h, 16), h), :]
        own = x if axis == "x" else y
        return ref.at[pl.ds((1 - own) * (lead // 2), lead // 2)]

    def out_shape_of(a):
        lead, rows, cols = a.shape
        return (lead, rows // 2, cols) if axis == "c" else (lead // 2, rows, cols)

    def body(*refs):
        ins, small_ref = refs[:n], refs[n]
        recvs, small_recv = refs[n + 1:2 * n + 1], refs[2 * n + 1]
        send_sems, recv_sems = refs[2 * n + 2:]
        x, y, c = _coords()
        partner = (_flip(x, axis == "x"), _flip(y, axis == "y"), _flip(c, axis == "c"))
        copies = [pltpu.make_async_remote_copy(
            src_ref=send_part(ins[idx], x, y, c), dst_ref=recvs[idx], send_sem=send_sems.at[idx],
            recv_sem=recv_sems.at[idx], device_id=partner, device_id_type=MESH) for idx in range(n)]
        copies.append(pltpu.make_async_remote_copy(
            src_ref=small_ref, dst_ref=small_recv, send_sem=send_sems.at[n], recv_sem=recv_sems.at[n],
            device_id=partner, device_id_type=MESH))
        for cp in copies:
            cp.start()
        for cp in copies:
            cp.wait()

    outs = pl.pallas_call(
        body, name="reduce_exchange_" + axis,
        in_specs=[ANY] * (n + 1), out_specs=[ANY] * (n + 1),
        out_shape=[jax.ShapeDtypeStruct(out_shape_of(a), a.dtype) for a in big]
        + [jax.ShapeDtypeStruct(small.shape, F32)],
        scratch_shapes=[pltpu.SemaphoreType.DMA((n + 1,)), pltpu.SemaphoreType.DMA((n + 1,))],
    )(*big, small)
    return outs[:n], outs[n]


ADD_CHUNKS = 4


def _reduce_add(axis, pos, big, recvs, small, small_recv):
    n = len(big)
    last = axis == "y"

    def chunked(shape):
        lead, rows, cols = shape
        if cols % (128 * ADD_CHUNKS) == 0:
            return (lead, rows, cols // ADD_CHUNKS), 2
        return (lead, rows // ADD_CHUNKS, cols), 1

    def kept_spec(a, part_shape):
        block, ax = chunked(part_shape)
        if axis == "c":
            if ax == 2:
                return pl.BlockSpec(block, lambda i, p: (0, p[2], i))
            return pl.BlockSpec(block, lambda i, p: (0, p[2] * ADD_CHUNKS + i, 0))
        which = 0 if axis == "x" else 1
        if ax == 2:
            return pl.BlockSpec(block, lambda i, p: (p[which], 0, i))
        return pl.BlockSpec(block, lambda i, p: (p[which], i, 0))

    def part_spec(part_shape, lead_of=None):
        block, ax = chunked(part_shape)
        lead = (lambda p: 0) if lead_of is None else lead_of
        if ax == 2:
            return pl.BlockSpec(block, lambda i, p: (lead(p), 0, i))
        return pl.BlockSpec(block, lambda i, p: (lead(p), i, 0))

    def body(pos_ref, *refs):
        kept, got = refs[:n], refs[n:2 * n]
        small_ref, small_got = refs[2 * n], refs[2 * n + 1]
        outs = refs[2 * n + 2:]
        for idx in range(n):
            total = kept[idx][...] + got[idx][...].astype(F32)
            outs[idx][...] = total
            if not last:
                outs[n + idx][...] = total.astype(BF16)
        outs[-1][...] = small_ref[...] + small_got[...]

    part_shapes = [r.shape for r in recvs]
    small_spec = pl.BlockSpec((small.shape[0] // ADD_CHUNKS, small.shape[1]), lambda i, p: (i, 0))
    in_specs = ([kept_spec(a, s) for a, s in zip(big, part_shapes)] + [part_spec(s) for s in part_shapes]
                + [small_spec, small_spec])
    if last:
        out_specs = [part_spec(s, lambda p: p[2]) for s in part_shapes]
        out_shape = [jax.ShapeDtypeStruct((2,) + s[1:], F32) for s in part_shapes]
    else:
        out_specs = [part_spec(s) for s in part_shapes] * 2
        out_shape = ([jax.ShapeDtypeStruct(s, F32) for s in part_shapes]
                     + [jax.ShapeDtypeStruct(s, BF16) for s in part_shapes])
    outs = pl.pallas_call(
        body, name="reduce_add_" + axis,
        grid_spec=pltpu.PrefetchScalarGridSpec(
            num_scalar_prefetch=1, grid=(ADD_CHUNKS,), in_specs=in_specs, out_specs=out_specs + [small_spec]),
        out_shape=out_shape + [jax.ShapeDtypeStruct(small.shape, F32)],
        compiler_params=_params(),
    )(pos, *big, *recvs, small, small_recv)
    return outs[:n], outs[n:-1], outs[-1]


def _swap_halves(halves):
    n = len(halves)

    def body(*refs):
        ins, outs = refs[:n], refs[n:2 * n]
        send_sems, recv_sems = refs[2 * n:]
        x, y, c = _coords()
        copies = [pltpu.make_async_remote_copy(
            src_ref=ins[idx].at[pl.ds(c, 1)], dst_ref=outs[idx].at[pl.ds(c, 1)], send_sem=send_sems.at[idx],
            recv_sem=recv_sems.at[idx], device_id=(x, y, 1 - c), device_id_type=MESH) for idx in range(n)]
        for cp in copies:
            cp.start()
        for cp in copies:
            cp.wait()

    return pl.pallas_call(
        body, name="swap_halves",
        in_specs=[ANY] * n, out_specs=[ANY] * n, input_output_aliases={idx: idx for idx in range(n)},
        out_shape=[jax.ShapeDtypeStruct(a.shape, F32) for a in halves],
        scratch_shapes=[pltpu.SemaphoreType.DMA((n,)), pltpu.SemaphoreType.DMA((n,))],
    )(*halves)


def _adamw_math(w, g, m, v):
    m = ADAM_B1 * m + (1.0 - ADAM_B1) * g
    v = ADAM_B2 * v + (1.0 - ADAM_B2) * jnp.square(g)
    m_hat = m / (1.0 - ADAM_B1 ** ADAM_STEP)
    v_hat = v / (1.0 - ADAM_B2 ** ADAM_STEP)
    delta = -ADAM_LR * (m_hat / (jnp.sqrt(v_hat) + ADAM_EPS) + ADAM_WD * w)
    return delta, m, v


def _adamw(name, w, g, m, v, block_rows):
    rows, cols = w.shape

    def body(w_ref, g_ref, m_ref, v_ref, d_out, m_out, v_out):
        d_out[...], m_out[...], v_out[...] = _adamw_math(w_ref[...], g_ref[...], m_ref[...], v_ref[...])

    spec = pl.BlockSpec((block_rows, cols), lambda i: (i, 0))
    shape = jax.ShapeDtypeStruct((rows, cols), F32)
    return pl.pallas_call(
        body, grid=(rows // block_rows,), name=name, in_specs=[spec] * 4, out_specs=[spec] * 3,
        out_shape=[shape] * 3, compiler_params=_params(),
    )(w, g, m, v)


def _adamw_ada(c_all16, dmod16, w, m, v, block_rows):
    rows, cols = w.shape

    def body(c_ref, dm_ref, w_ref, m_ref, v_ref, g_out, d_out, m_out, v_out):
        cv = c_ref[...]
        g = _tn((cv * jax.nn.sigmoid(cv)).astype(BF16), dm_ref[...].astype(BF16))
        g_out[...] = g
        d_out[...], m_out[...], v_out[...] = _adamw_math(w_ref[...], g, m_ref[...], v_ref[...])

    spec = pl.BlockSpec((block_rows, cols), lambda i: (i, 0))
    shape = jax.ShapeDtypeStruct((rows, cols), F32)
    return pl.pallas_call(
        body, grid=(rows // block_rows,), name="adamw_w_ada",
        in_specs=[pl.BlockSpec((16, block_rows), lambda i: (0, i)), _full(dmod16.shape), spec, spec, spec],
        out_specs=[spec] * 4, out_shape=[shape] * 4, compiler_params=_params(),
    )(c_all16, dmod16, w, m, v)


SMALL_REPLICATED = ["b_ada", "norm1_gain", "conv_dw_b", "conv_ln_g", "conv_ln_b", "gm_ln_g", "gm_ln_b", "gm_ws", "gm_bs",
                    "mix_out_gain", "norm2_gain", "ffn_dw_b", "final_gain"]
SMALL_SHARDED = ["conv_dw_w", "ffn_dw_w"]
PACK_ROWS = 256
ADAM_PACK_ROWS = 160
WEIGHT_ORDER = ["w_ada", "b_ada", "norm1_gain", "w_in", "conv_dw_w", "conv_dw_b", "conv_ln_g", "conv_ln_b", "gm_ln_g",
                "gm_ln_b", "gm_ws", "gm_bs", "mix_out_gain", "w_out", "norm2_gain", "w_up", "ffn_dw_w", "ffn_dw_b",
                "w_down", "final_gain"]


def _pack(parts, rows):
    flat = jnp.concatenate([a.reshape(-1) for a in parts])
    return jnp.pad(flat, (0, rows * D_MODEL - flat.shape[0])).reshape(rows, D_MODEL)


def _unpack(packed, shapes):
    flat = packed.reshape(-1)
    out, pos = [], 0
    for s in shapes:
        size = 1
        for d in s:
            size *= d
        out.append(flat[pos:pos + size].reshape(s))
        pos += size
    return out


def kernel(x, c, w_ada, b_ada, norm1_gain, w_in, conv_dw_w, conv_dw_b, conv_ln_g, conv_ln_b, gm_ln_g, gm_ln_b, gm_ws, gm_bs, mix_out_gain, w_out, norm2_gain, w_up, ffn_dw_w, ffn_dw_b, w_down, final_gain, loss_target, m_w_ada, m_b_ada, m_norm1_gain, m_w_in, m_conv_dw_w, m_conv_dw_b, m_conv_ln_g, m_conv_ln_b, m_gm_ln_g, m_gm_ln_b, m_gm_ws, m_gm_bs, m_mix_out_gain, m_w_out, m_norm2_gain, m_w_up, m_ffn_dw_w, m_ffn_dw_b, m_w_down, m_final_gain, v_w_ada, v_b_ada, v_norm1_gain, v_w_in, v_conv_dw_w, v_conv_dw_b, v_conv_ln_g, v_conv_ln_b, v_gm_ln_g, v_gm_ln_b, v_gm_ws, v_gm_bs, v_mix_out_gain, v_w_out, v_norm2_gain, v_w_up, v_ffn_dw_w, v_ffn_dw_b, v_w_down, v_final_gain):
    weights = dict(w_ada=w_ada, b_ada=b_ada, norm1_gain=norm1_gain, w_in=w_in, conv_dw_w=conv_dw_w, conv_dw_b=conv_dw_b,
                   conv_ln_g=conv_ln_g, conv_ln_b=conv_ln_b, gm_ln_g=gm_ln_g, gm_ln_b=gm_ln_b, gm_ws=gm_ws, gm_bs=gm_bs,
                   mix_out_gain=mix_out_gain, w_out=w_out, norm2_gain=norm2_gain, w_up=w_up, ffn_dw_w=ffn_dw_w,
                   ffn_dw_b=ffn_dw_b, w_down=w_down, final_gain=final_gain)
    mom1 = dict(w_ada=m_w_ada, b_ada=m_b_ada, norm1_gain=m_norm1_gain, w_in=m_w_in, conv_dw_w=m_conv_dw_w,
                conv_dw_b=m_conv_dw_b, conv_ln_g=m_conv_ln_g, conv_ln_b=m_conv_ln_b, gm_ln_g=m_gm_ln_g, gm_ln_b=m_gm_ln_b,
                gm_ws=m_gm_ws, gm_bs=m_gm_bs, mix_out_gain=m_mix_out_gain, w_out=m_w_out, norm2_gain=m_norm2_gain,
                w_up=m_w_up, ffn_dw_w=m_ffn_dw_w, ffn_dw_b=m_ffn_dw_b, w_down=m_w_down, final_gain=m_final_gain)
    mom2 = dict(w_ada=v_w_ada, b_ada=v_b_ada, norm1_gain=v_norm1_gain, w_in=v_w_in, conv_dw_w=v_conv_dw_w,
                conv_dw_b=v_conv_dw_b, conv_ln_g=v_conv_ln_g, conv_ln_b=v_conv_ln_b, gm_ln_g=v_gm_ln_g, gm_ln_b=v_gm_ln_b,
                gm_ws=v_gm_ws, gm_bs=v_gm_bs, mix_out_gain=v_mix_out_gain, w_out=v_w_out, norm2_gain=v_norm2_gain,
                w_up=v_w_up, ffn_dw_w=v_ffn_dw_w, ffn_dw_b=v_ffn_dw_b, w_down=v_w_down, final_gain=v_final_gain)
    shard = 2 * lax.axis_index("x") + lax.axis_index("y")
    me = 2 * shard + lax.axis_index("c")

    ada_cols = w_ada.shape[2]
    b_ada_sh = lax.dynamic_slice(b_ada, (0, shard * ada_cols), (1, ada_cols))
    c_all64, mod32 = _ada_mod(c, w_ada[0], b_ada_sh)
    c_all = c_all64[::8]
    mod = mod32[::8].reshape(1, N_SHARD * ada_cols)

    (w_in_g, w_out_g, w_up_g, w_down_g), (conv_w_g, ffn_w_g) = _gather_weights(
        [w_in[0], w_out[0], w_up[0], w_down[0]], [conv_dw_w[0], ffn_dw_w[0]])
    conv_w_full = jnp.transpose(conv_w_g, (1, 0, 2)).reshape(CONV_K, D_HALF)
    ffn_w_full = jnp.transpose(ffn_w_g, (1, 0, 2)).reshape(FFN_K, 2 * D_FF)

    p = dict(norm1_gain=norm1_gain, conv_dw_w=conv_w_full, conv_dw_b=conv_dw_b, conv_ln_g=conv_ln_g,
             conv_ln_b=conv_ln_b, gm_ln_g=gm_ln_g, gm_ln_b=gm_ln_b, gm_ws=gm_ws[0], gm_bs=gm_bs[0],
             mix_out_gain=mix_out_gain, norm2_gain=norm2_gain, ffn_dw_w=ffn_w_full, ffn_dw_b=ffn_dw_b,
             final_gain=final_gain[None])
    grad_x, g, d_mod, loss = _local_step(
        x[0], loss_target[0], mod, p, w_in_g, w_out_g.reshape(D_MODEL, D_MODEL), w_up_g,
        w_down_g.reshape(D_FF, D_MODEL))

    n_mod = d_mod.shape[1]
    dmod_rows = lax.dynamic_update_slice(jnp.zeros((N_DEV, n_mod), F32), d_mod, (me, 0))
    g["b_ada"] = d_mod
    small = _pack([g[k] for k in SMALL_REPLICATED] + [g[k] for k in SMALL_SHARDED] + [dmod_rows, loss[0, :1]], PACK_ROWS)
    big = [g["w_in"], g["w_out"].reshape(N_SHARD, -1, D_MODEL), g["w_up"], g["w_down"].reshape(N_SHARD, -1, D_MODEL)]
    pos = jnp.stack(_coords()).astype(jnp.int32)
    to_send = big
    for axis in ("c", "x", "y"):
        recvs, small_recv = _reduce_exchange(axis, to_send, small)
        big, to_send, small = _reduce_add(axis, pos, big, recvs, small, small_recv)
    full = _swap_halves(big)
    grads = dict(w_in=full[0].reshape(w_in.shape[1:]), w_out=full[1].reshape(w_out.shape[1:]),
                 w_up=full[2].reshape(w_up.shape[1:]), w_down=full[3].reshape(w_down.shape[1:]))

    small_shapes = ([weights[k].shape for k in SMALL_REPLICATED] + [(CONV_K, D_HALF), (FFN_K, 2 * D_FF)]
                    + [(N_DEV, n_mod), (1,)])
    *small_grads, conv_w_grad, ffn_w_grad, dmod_all, loss_sum = _unpack(small, small_shapes)
    grads.update(zip(SMALL_REPLICATED, small_grads))
    grads["conv_dw_w"] = lax.dynamic_slice(conv_w_grad, (0, shard * conv_dw_w.shape[2]), conv_dw_w.shape[1:])[None]
    grads["ffn_dw_w"] = lax.dynamic_slice(ffn_w_grad, (0, shard * ffn_dw_w.shape[2]), ffn_dw_w.shape[1:])[None]

    delta, new_m, new_v = {}, {}, {}
    for name, block_rows in (("w_in", 256), ("w_out", 128), ("w_up", 256), ("w_down", 352)):
        delta[name], new_m[name], new_v[name] = [a[None] for a in _adamw(
            "adamw_" + name, weights[name][0], grads[name], mom1[name][0], mom2[name][0], block_rows)]
        grads[name] = grads[name][None]
    dmod_sh = lax.dynamic_slice(dmod_all, (0, shard * ada_cols), (N_DEV, ada_cols))
    pad8 = ((0, 16 - N_DEV), (0, 0))
    grads["w_ada"], delta["w_ada"], new_m["w_ada"], new_v["w_ada"] = [a[None] for a in _adamw_ada(
        jnp.pad(c_all, pad8), jnp.pad(dmod_sh, pad8), w_ada[0], m_w_ada[0], v_w_ada[0], 256)]
    small_names = SMALL_REPLICATED + SMALL_SHARDED
    packed = [_pack([d[k] for k in small_names], ADAM_PACK_ROWS) for d in (weights, grads, mom1, mom2)]
    small_out = _adamw("adamw_small", *packed, ADAM_PACK_ROWS)
    for d, arr in zip((delta, new_m, new_v), small_out):
        d.update(zip(small_names, _unpack(arr, [weights[k].shape for k in small_names])))

    return (loss_sum.reshape(()), grad_x[None], *[grads[k] for k in WEIGHT_ORDER], *[delta[k] for k in WEIGHT_ORDER],
            *[new_m[k] for k in WEIGHT_ORDER], *[new_v[k] for k in WEIGHT_ORDER])
```

```python
import functools

import jax
import jax.numpy as jnp
from jax import lax
from jax.experimental import pallas as pl
from jax.experimental.pallas import tpu as pltpu

F32 = jnp.float32
BF16 = jnp.bfloat16

D_MODEL = 1024
D_HALF = 512
D_FF = 2816
CONV_K = 31
FFN_K = 3
CHUNK = 128
N_HEADS = 8
HEAD_DIM = 64
N_SHARD = 4
N_DEV = 8
RMS_EPS = 1e-6
LN_EPS = 1e-5
ADAM_LR, ADAM_B1, ADAM_B2, ADAM_EPS, ADAM_WD, ADAM_STEP = 0.001, 0.9, 0.999, 1e-08, 0.01, 10

TILE = 256
HALO = 32
FFN_HALO = 8
FFN_BLK = 256
UP_SHARD = 2 * D_FF // N_SHARD
VMEM_LIMIT_BYTES = 56 * 1024 * 1024

ANY = pl.BlockSpec(memory_space=pl.ANY)
NT_DIMS = (((1,), (1,)), ((), ()))
TN_DIMS = (((0,), (0,)), ((), ()))


def _full(shape):
    return pl.BlockSpec(shape, lambda i: (0,) * len(shape))


def _nn(a, b):
    return jnp.dot(a, b, preferred_element_type=F32)


def _nt(a, b):
    return lax.dot_general(a, b, NT_DIMS, preferred_element_type=F32)


def _tn(a, b):
    return lax.dot_general(a, b, TN_DIMS, preferred_element_type=F32)


def _colsum(a):
    return jnp.sum(a, axis=0, keepdims=True)


def _params(semantics=("arbitrary",)):
    return pltpu.CompilerParams(dimension_semantics=semantics, vmem_limit_bytes=VMEM_LIMIT_BYTES)


def _rms(v, gain):
    return v * lax.rsqrt(jnp.mean(v * v, axis=-1, keepdims=True) + RMS_EPS) * gain


def _layer_norm(v, gain, bias):
    mu = jnp.mean(v, axis=-1, keepdims=True)
    var = jnp.mean(jnp.square(v - mu), axis=-1, keepdims=True)
    return (v - mu) * lax.rsqrt(var + LN_EPS) * gain + bias


def _mod_norm(v, gain, scale, shift):
    return _rms(v, gain) * (1.0 + scale) + shift


def _conv_branch(a1, ln_g, ln_b, out_gain):
    a2 = _layer_norm(a1, ln_g, ln_b)
    return _rms(a2 * jax.nn.sigmoid(a2), out_gain)


def _gate_branch(gu, sp, out_gain):
    return _rms(jax.nn.gelu(gu) * sp, out_gain)


def _gv_norm(gv, ln_g, ln_b):
    return _layer_norm(jax.nn.gelu(gv), ln_g, ln_b)


def _head_pair_matmul(wp_ref, v):
    lane = lax.broadcasted_iota(jnp.int32, (CHUNK, CHUNK), 1)
    rows = []
    for n in range(v.shape[0] // CHUNK):
        cols = []
        for j in range(N_HEADS // 2):
            r = _nn(wp_ref[j], v[n * CHUNK:(n + 1) * CHUNK, j * CHUNK:(j + 1) * CHUNK])
            cols.append(jnp.where(lane < HEAD_DIM, r[:CHUNK], r[CHUNK:]))
        rows.append(jnp.concatenate(cols, axis=1))
    return jnp.concatenate(rows, axis=0)


def _tile_bias(bs, tokens):
    return jnp.concatenate([bs] * (tokens // CHUNK), axis=0)


def _fwd_mixer_up(x, vec, conv_w, wpair, bs_full, w_in_g, w_out_g, w_up_g, u_dtype):
    seq = x.shape[0]
    n_tiles = seq // TILE
    t = TILE
    names = ["norm1_gain", "sc1", "sh1", "gt1", "norm2_gain", "sc2", "sh2", "conv_dw_b", "conv_ln_g", "conv_ln_b",
             "gm_ln_g", "gm_ln_b", "mix_out_gain"]
    vecs = [vec[k] for k in names]

    def body(x_ref, g1, sc1, sh1, gt1, g2, sc2, sh2, cb, clg, clb, vg, vb, mg, cw, wp, bs, win_hbm, wout_hbm, wup_hbm,
             z_ref, a1_ref, sp_ref, y_ref, o1_ref, x2_ref, u_ref, win_v, wout_v, wup_v, halo, bank, sem):
        i = pl.program_id(0)

        @pl.when(i == 0)
        def _():
            cps = [pltpu.make_async_copy(win_hbm, win_v, sem.at[0]),
                   pltpu.make_async_copy(wout_hbm, wout_v, sem.at[1]),
                   pltpu.make_async_copy(wup_hbm, wup_v, sem.at[2])]
            for cp in cps:
                cp.start()
            for cp in cps:
                cp.wait()
            halo[...] = jnp.zeros_like(halo)

        xv = x_ref[...]
        h1b = _mod_norm(xv, g1[...], sc1[...], sh1[...]).astype(BF16)
        zs = [_nn(h1b, win_v[k]) for k in range(N_SHARD)]
        for k in range(N_SHARD):
            z_ref[:, k * D_HALF:(k + 1) * D_HALF] = zs[k]
        ca, cg, gu, gv = zs
        a0 = ca * jax.nn.sigmoid(cg)
        ext = jnp.concatenate([halo[...], a0], axis=0)
        halo[...] = a0[t - HALO:]
        bank[0] = ext
        for b in range(1, 8):
            bank[b] = pltpu.roll(ext, b, axis=0)
        a1 = jnp.zeros((t, D_HALF), F32) + cb[...]
        for s in range(CONV_K):
            q, b = divmod(s, 8)
            a1 = a1 + bank[b, pl.ds(HALO - 8 * q, t), :] * cw[pl.ds(CONV_K - 1 - s, 1), :]
        a1_ref[...] = a1
        mgv = mg[...]
        ya = _conv_branch(a1, clg[...], clb[...], mgv[:, :D_HALF])
        gvn = _gv_norm(gv, vg[...], vb[...]).astype(BF16)
        sp = _head_pair_matmul(wp, gvn) + _tile_bias(bs[...], t)
        sp_ref[...] = sp
        yg = _gate_branch(gu, sp, mgv[:, D_HALF:])
        yb = jnp.concatenate([ya, yg], axis=1).astype(BF16)
        y_ref[...] = yb
        o1 = _nn(yb, wout_v[...])
        o1_ref[...] = o1
        x2 = xv + gt1[...] * o1
        x2_ref[...] = x2
        h2b = _mod_norm(x2, g2[...], sc2[...], sh2[...]).astype(BF16)
        for k in range(N_SHARD):
            u_ref[:, k * UP_SHARD:(k + 1) * UP_SHARD] = _nn(h2b, wup_v[k]).astype(u_ref.dtype)

    def row(width):
        return pl.BlockSpec((t, width), lambda i: (i, 0))

    out_shape = [jax.ShapeDtypeStruct((seq, 4 * D_HALF), F32), jax.ShapeDtypeStruct((seq, D_HALF), F32),
                 jax.ShapeDtypeStruct((seq, D_HALF), F32), jax.ShapeDtypeStruct((seq, D_MODEL), BF16),
                 jax.ShapeDtypeStruct((seq, D_MODEL), F32), jax.ShapeDtypeStruct((seq, D_MODEL), F32),
                 jax.ShapeDtypeStruct((seq, 2 * D_FF), u_dtype)]
    return pl.pallas_call(
        body, grid=(n_tiles,), name="fwd_mixer_up",
        in_specs=[row(D_MODEL)] + [_full(v.shape) for v in vecs]
        + [_full(conv_w.shape), _full(wpair.shape), _full(bs_full.shape), ANY, ANY, ANY],
        out_specs=[row(4 * D_HALF), row(D_HALF), row(D_HALF), row(D_MODEL), row(D_MODEL), row(D_MODEL), row(2 * D_FF)],
        out_shape=out_shape,
        scratch_shapes=[pltpu.VMEM(w_in_g.shape, BF16), pltpu.VMEM(w_out_g.shape, BF16), pltpu.VMEM(w_up_g.shape, BF16),
                        pltpu.VMEM((HALO, D_HALF), F32), pltpu.VMEM((8, t + HALO, D_HALF), F32),
                        pltpu.SemaphoreType.DMA((3,))],
        compiler_params=_params(),
    )(x, *vecs, conv_w, wpair, bs_full, w_in_g, w_out_g, w_up_g)


def _ffn_tail(u, x2, target, ffn_w, ffn_b, gt2, final_gain, w_down_g):
    seq = x2.shape[0]
    n_tiles = seq // TILE
    t = TILE
    n_blk = D_FF // FFN_BLK
    halo_rows = 16 if u.dtype == BF16 else 8
    inv_d = 1.0 / D_MODEL

    def final_norm(x3, gain):
        return _rms(x3, gain)

    def body(u_ref, uh_ref, x2_ref, tgt_ref, fw, fb, gt2_ref, fg, wd_hbm,
             du_ref, dx3_ref, dfw_ref, dfb_ref, dfg_ref, dgt2_ref, loss_ref, dwd_hbm, dwd16_hbm,
             wd_v, dwd_acc, carry, val_s, sil_s, dsil_s, f_s, sem):
        i = pl.program_id(0)
        tile = n_tiles - 1 - i

        @pl.when(i == 0)
        def _():
            cp = pltpu.make_async_copy(wd_hbm, wd_v, sem.at[0])
            cp.start()
            cp.wait()
            dwd_acc[...] = jnp.zeros_like(dwd_acc)
            carry[...] = jnp.zeros_like(carry)
            dfw_ref[...] = jnp.zeros_like(dfw_ref)
            dfb_ref[...] = jnp.zeros_like(dfb_ref)
            dfg_ref[...] = jnp.zeros_like(dfg_ref)
            dgt2_ref[...] = jnp.zeros_like(dgt2_ref)
            loss_ref[...] = jnp.zeros_like(loss_ref)

        def cols_of(j):
            return pl.ds(j * FFN_BLK, FFN_BLK), pl.ds(D_FF + j * FFN_BLK, FFN_BLK)

        def conv(cols):
            prev = uh_ref[pl.ds(halo_rows - FFN_HALO, FFN_HALO), cols].astype(F32)
            ext = jnp.concatenate([jnp.where(tile > 0, prev, 0.0), u_ref[:, cols].astype(F32)], axis=0)
            acc = fb[:, cols] + ext[FFN_HALO:] * fw[pl.ds(FFN_K - 1, 1), cols]
            for s in range(1, FFN_K):
                acc = acc + pltpu.roll(ext, s, axis=0)[FFN_HALO:] * fw[pl.ds(FFN_K - 1 - s, 1), cols]
            return acc

        o2 = jnp.zeros((t, D_MODEL), F32)
        for j in range(n_blk):
            cv, cg = cols_of(j)
            val, gate = conv(cv), conv(cg)
            sig = jax.nn.sigmoid(gate)
            sil = gate * sig
            fb16 = (sil * val).astype(BF16)
            val_s[:, cv] = val
            sil_s[:, cv] = sil
            dsil_s[:, cv] = sig + sil * (1.0 - sig)
            f_s[:, cv] = fb16
            o2 = o2 + _nn(fb16, wd_v[pl.ds(j * FFN_BLK, FFN_BLK), :])

        gt2v = gt2_ref[...]
        x3 = x2_ref[...] + gt2v * o2
        out, out_vjp = jax.vjp(final_norm, x3, fg[...])
        diff = out - tgt_ref[...]
        loss_ref[...] += jnp.zeros_like(loss_ref) + 0.5 * inv_d * jnp.sum(diff * diff)
        dx3, dfg = out_vjp(diff * inv_d)
        dfg_ref[...] += dfg
        dgt2_ref[...] += _colsum(dx3 * o2)
        dx3_ref[...] = dx3
        do2b = (gt2v * dx3).astype(BF16)

        for j in range(n_blk):
            cv, cg = cols_of(j)
            rows = pl.ds(j * FFN_BLK, FFN_BLK)
            df = _nt(do2b, wd_v[rows, :])
            dval = df * sil_s[:, cv]
            dgate = df * val_s[:, cv] * dsil_s[:, cv]
            dwd_acc[rows, :] += _tn(f_s[:, cv], do2b)
            for dd, cols in ((dval, cv), (dgate, cg)):
                dfb_ref[:, cols] += _colsum(dd)
                ext = jnp.concatenate([dd, carry[:, cols]], axis=0)
                uv = u_ref[:, cols].astype(F32)
                du = dd * fw[pl.ds(FFN_K - 1, 1), cols]
                dfw_ref[pl.ds(FFN_K - 1, 1), cols] += _colsum(dd * uv)
                for s in range(1, FFN_K):
                    shifted = pltpu.roll(ext, t + FFN_HALO - s, axis=0)[:t]
                    du = du + shifted * fw[pl.ds(FFN_K - 1 - s, 1), cols]
                    dfw_ref[pl.ds(FFN_K - 1 - s, 1), cols] += _colsum(shifted * uv)
                carry[:, cols] = dd[:FFN_HALO]
                du_ref[:, cols] = du.astype(BF16)

        @pl.when(i == n_tiles - 1)
        def _():
            cp = pltpu.make_async_copy(dwd_acc, dwd_hbm, sem.at[1])
            cp.start()
            wd_v[...] = dwd_acc[...].astype(BF16)
            cp16 = pltpu.make_async_copy(wd_v, dwd16_hbm, sem.at[2])
            cp16.start()
            cp.wait()
            cp16.wait()

    def rev(width):
        return pl.BlockSpec((t, width), lambda i: (n_tiles - 1 - i, 0))

    halo_spec = pl.BlockSpec(
        (halo_rows, 2 * D_FF), lambda i: (jnp.maximum((n_tiles - 1 - i) * (t // halo_rows) - 1, 0), 0))
    out_shape = [jax.ShapeDtypeStruct((seq, 2 * D_FF), BF16), jax.ShapeDtypeStruct((seq, D_MODEL), F32),
                 jax.ShapeDtypeStruct((FFN_K, 2 * D_FF), F32), jax.ShapeDtypeStruct((1, 2 * D_FF), F32),
                 jax.ShapeDtypeStruct((1, D_MODEL), F32), jax.ShapeDtypeStruct((1, D_MODEL), F32),
                 jax.ShapeDtypeStruct((1, 128), F32), jax.ShapeDtypeStruct((D_FF, D_MODEL), F32),
                 jax.ShapeDtypeStruct((D_FF, D_MODEL), BF16)]
    return pl.pallas_call(
        body, grid=(n_tiles,), name="ffn_tail",
        in_specs=[rev(2 * D_FF), halo_spec, rev(D_MODEL), rev(D_MODEL), _full(ffn_w.shape), _full(ffn_b.shape),
                  _full(gt2.shape), _full(final_gain.shape), ANY],
        out_specs=[rev(2 * D_FF), rev(D_MODEL), _full((FFN_K, 2 * D_FF)), _full((1, 2 * D_FF)), _full((1, D_MODEL)),
                   _full((1, D_MODEL)), _full((1, 128)), ANY, ANY],
        out_shape=out_shape,
        scratch_shapes=[pltpu.VMEM((D_FF, D_MODEL), BF16), pltpu.VMEM((D_FF, D_MODEL), F32),
                        pltpu.VMEM((FFN_HALO, 2 * D_FF), F32),
                        pltpu.VMEM((t, D_FF), F32), pltpu.VMEM((t, D_FF), F32), pltpu.VMEM((t, D_FF), F32),
                        pltpu.VMEM((t, D_FF), BF16), pltpu.SemaphoreType.DMA((3,))],
        compiler_params=_params(),
    )(u, u, x2, target, ffn_w, ffn_b, gt2, final_gain, w_down_g)


def _scatter_copies(src16, land, send_sems, recv_sems):
    x, y, c = _coords()
    h = src16.shape[1] // 2
    copies = []
    for f, flip in enumerate(PEER_FLIPS):
        tx, ty, tc = _flip(x, flip[0]), _flip(y, flip[1]), _flip(c, flip[2])
        copies.append(pltpu.make_async_remote_copy(
            src_ref=src16.at[2 * tx + ty, pl.ds(pl.multiple_of(tc * h, 16), h)], dst_ref=land.at[f],
            send_sem=send_sems.at[f], recv_sem=recv_sems.at[f], device_id=(tx, ty, tc), device_id_type=MESH))
    return copies


def _land_shape(src16):
    return jax.ShapeDtypeStruct((len(PEER_FLIPS), src16.shape[1] // 2, src16.shape[2]), BF16)


def _bwd_up(du, x2, dx3, norm2_gain, sc2, sh2, w_up_g, dwd16):
    seq = x2.shape[0]
    n_tiles = seq // TILE
    t = TILE

    def body(du_ref, x2_ref, dx3_ref, g2, sc2_ref, sh2_ref, wup_hbm, dwd16_hbm,
             dx2_ref, dg2_ref, dsc2_ref, dsh2_ref, dwup_hbm, dwup16_hbm, land_hbm,
             wup_v, dwup_acc, sem, send_sems, recv_sems):
        i = pl.program_id(0)

        @pl.when(i == 0)
        def _():
            for cp in _scatter_copies(dwd16_hbm, land_hbm, send_sems, recv_sems):
                cp.start()
            cp = pltpu.make_async_copy(wup_hbm, wup_v, sem.at[0])
            cp.start()
            cp.wait()
            dwup_acc[...] = jnp.zeros_like(dwup_acc)
            dg2_ref[...] = jnp.zeros_like(dg2_ref)
            dsc2_ref[...] = jnp.zeros_like(dsc2_ref)
            dsh2_ref[...] = jnp.zeros_like(dsh2_ref)

        h2, h2_vjp = jax.vjp(_mod_norm, x2_ref[...], g2[...], sc2_ref[...], sh2_ref[...])
        h2b = h2.astype(BF16)
        dh2 = jnp.zeros((t, D_MODEL), F32)
        for k in range(N_SHARD):
            dub = du_ref[:, k * UP_SHARD:(k + 1) * UP_SHARD]
            dh2 = dh2 + _nt(dub, wup_v[k])
            dwup_acc[k] += _tn(h2b, dub)
        dx2, dg2, dsc2, dsh2 = h2_vjp(dh2)
        dx2_ref[...] = dx3_ref[...] + dx2
        dg2_ref[...] += dg2
        dsc2_ref[...] += dsc2
        dsh2_ref[...] += dsh2

        @pl.when(i == n_tiles - 1)
        def _():
            cp = pltpu.make_async_copy(dwup_acc, dwup_hbm, sem.at[1])
            cp.start()
            for k in range(N_SHARD):
                wup_v[k] = dwup_acc[k].astype(BF16)
            cp16 = pltpu.make_async_copy(wup_v, dwup16_hbm, sem.at[2])
            cp16.start()
            cp.wait()
            cp16.wait()
            for rc in _scatter_copies(dwd16_hbm, land_hbm, send_sems, recv_sems):
                rc.wait()

    def row(width):
        return pl.BlockSpec((t, width), lambda i: (i, 0))

    vec = jax.ShapeDtypeStruct((1, D_MODEL), F32)
    n_peer = len(PEER_FLIPS)
    return pl.pallas_call(
        body, grid=(n_tiles,), name="bwd_up",
        in_specs=[row(2 * D_FF), row(D_MODEL), row(D_MODEL), _full((1, D_MODEL)), _full((1, D_MODEL)),
                  _full((1, D_MODEL)), ANY, ANY],
        out_specs=[row(D_MODEL), _full((1, D_MODEL)), _full((1, D_MODEL)), _full((1, D_MODEL)), ANY, ANY, ANY],
        out_shape=[jax.ShapeDtypeStruct((seq, D_MODEL), F32), vec, vec, vec,
                   jax.ShapeDtypeStruct(w_up_g.shape, F32), jax.ShapeDtypeStruct(w_up_g.shape, BF16),
                   _land_shape(dwd16)],
        scratch_shapes=[pltpu.VMEM(w_up_g.shape, BF16), pltpu.VMEM(w_up_g.shape, F32), pltpu.SemaphoreType.DMA((3,)),
                        pltpu.SemaphoreType.DMA((n_peer,)), pltpu.SemaphoreType.DMA((n_peer,))],
        compiler_params=_params(),
    )(du, x2, dx3, norm2_gain, sc2, sh2, w_up_g, dwd16)


def _bwd_mixer(dx2, x, z, a1, sp, yb, o1, vec, conv_w, wpair, wpair_t, causal_mask, w_in_g, w_out_g, dwup16):
    seq = x.shape[0]
    n_tiles = seq // TILE
    t = TILE
    names = ["norm1_gain", "sc1", "sh1", "gt1", "conv_ln_g", "conv_ln_b", "gm_ln_g", "gm_ln_b", "mix_out_gain"]
    vecs = [vec[k] for k in names]

    def body(dx2_ref, x_ref, z_ref, a1_ref, sp_ref, y_ref, o1_ref, g1, sc1, sh1, gt1, clg, clb, vg, vb, mg,
             cw, wp, wpt, mask_ref, win_hbm, wout_hbm, dwup16_hbm,
             gx_ref, dg1_ref, dsc1_ref, dsh1_ref, dgt1_ref, dcw_ref, dcb_ref, dclg_ref, dclb_ref, dvg_ref, dvb_ref,
             dmg_ref, dws_ref, dbs_ref, dwin_hbm, dwout_hbm, land_hbm, dwin16_hbm, dwout16_hbm,
             win_v, wout_v, dwin_acc, dwout_acc, carry, bank, dbs_acc, sem, send_sems, recv_sems):
        i = pl.program_id(0)
        small = [dg1_ref, dsc1_ref, dsh1_ref, dgt1_ref, dcw_ref, dcb_ref, dclg_ref, dclb_ref, dvg_ref, dvb_ref,
                 dmg_ref, dws_ref, dbs_acc]

        @pl.when(i == 0)
        def _():
            for cp in _scatter_copies(dwup16_hbm, land_hbm, send_sems, recv_sems):
                cp.start()
            cps = [pltpu.make_async_copy(win_hbm, win_v, sem.at[0]),
                   pltpu.make_async_copy(wout_hbm, wout_v, sem.at[1])]
            for cp in cps:
                cp.start()
            for cp in cps:
                cp.wait()
            dwin_acc[...] = jnp.zeros_like(dwin_acc)
            dwout_acc[...] = jnp.zeros_like(dwout_acc)
            carry[...] = jnp.zeros_like(carry)
            for ref in small:
                ref[...] = jnp.zeros_like(ref)

        dx2v = dx2_ref[...]
        gt1v = gt1[...]
        dgt1_ref[...] += _colsum(dx2v * o1_ref[...])
        do1b = (gt1v * dx2v).astype(BF16)
        dy = _nt(do1b, wout_v[...])
        dwout_acc[...] += _tn(y_ref[...], do1b)

        mgv = mg[...]
        _, conv_vjp = jax.vjp(_conv_branch, a1_ref[...], clg[...], clb[...], mgv[:, :D_HALF])
        da1, dclg, dclb, dmg_a = conv_vjp(dy[:, :D_HALF])
        dclg_ref[...] += dclg
        dclb_ref[...] += dclb
        gu = z_ref[:, 2 * D_HALF:3 * D_HALF]
        gv = z_ref[:, 3 * D_HALF:]
        spv = sp_ref[...]
        _, gate_vjp = jax.vjp(_gate_branch, gu, spv, mgv[:, D_HALF:])
        dgu, dsp, dmg_g = gate_vjp(dy[:, D_HALF:])
        dmg_ref[...] += jnp.concatenate([dmg_a, dmg_g], axis=1)
        gvn, gv_vjp = jax.vjp(_gv_norm, gv, vg[...], vb[...])
        gvnb = gvn.astype(BF16)
        dspb = dsp.astype(BF16)
        dgvn = _head_pair_matmul(wpt, dspb)
        dgv, dvg, dvb = gv_vjp(dgvn)
        dvg_ref[...] += dvg
        dvb_ref[...] += dvb
        lane = lax.broadcasted_iota(jnp.int32, (CHUNK, CHUNK), 1)
        dbs = jnp.zeros((CHUNK, D_HALF), F32)
        for n in range(t // CHUNK):
            rows = slice(n * CHUNK, (n + 1) * CHUNK)
            dbs = dbs + dsp[rows, :]
            for j in range(N_HEADS // 2):
                cols = slice(j * CHUNK, (j + 1) * CHUNK)
                blk = dspb[rows, cols]
                zero = jnp.zeros_like(blk)
                vblk = gvnb[rows, cols]
                dws_ref[2 * j] += _nt(jnp.where(lane < HEAD_DIM, blk, zero), vblk)
                dws_ref[2 * j + 1] += _nt(jnp.where(lane < HEAD_DIM, zero, blk), vblk)
        dbs_acc[...] += dbs

        ca = z_ref[:, :D_HALF]
        cg = z_ref[:, D_HALF:2 * D_HALF]
        sig = jax.nn.sigmoid(cg)
        a0 = ca * sig
        ext = jnp.concatenate([da1, carry[...]], axis=0)
        carry[...] = da1[:HALO]
        bank[0] = ext
        for b in range(1, 8):
            bank[b] = pltpu.roll(ext, t + HALO - b, axis=0)
        dcb_ref[...] += _colsum(da1)
        da0 = jnp.zeros((t, D_HALF), F32)
        for s in range(CONV_K):
            q, b = divmod(s, 8)
            shifted = bank[b, pl.ds(8 * q, t), :]
            da0 = da0 + shifted * cw[pl.ds(CONV_K - 1 - s, 1), :]
            dcw_ref[pl.ds(CONV_K - 1 - s, 1), :] += _colsum(shifted * a0)
        dca = da0 * sig
        dcg = da0 * ca * sig * (1.0 - sig)

        h1, h1_vjp = jax.vjp(_mod_norm, x_ref[...], g1[...], sc1[...], sh1[...])
        h1b = h1.astype(BF16)
        dh1 = jnp.zeros((t, D_MODEL), F32)
        for k, dzk in enumerate((dca, dcg, dgu, dgv)):
            dzb = dzk.astype(BF16)
            dh1 = dh1 + _nt(dzb, win_v[k])
            dwin_acc[k] += _tn(h1b, dzb)
        dx, dg1, dsc1, dsh1 = h1_vjp(dh1)
        gx_ref[...] = dx2v + dx
        dg1_ref[...] += dg1
        dsc1_ref[...] += dsc1
        dsh1_ref[...] += dsh1

        @pl.when(i == n_tiles - 1)
        def _():
            for h in range(N_HEADS):
                dws_ref[h] = dws_ref[h] * mask_ref[...]
            head_of_lane = lax.broadcasted_iota(jnp.int32, (N_HEADS, D_HALF), 1) // HEAD_DIM
            pick = (head_of_lane == lax.broadcasted_iota(jnp.int32, (N_HEADS, D_HALF), 0)).astype(F32)
            dbs_ref[...] = lax.dot_general(pick, dbs_acc[...], NT_DIMS, precision=lax.Precision.HIGHEST,
                                           preferred_element_type=F32)
            cps = [pltpu.make_async_copy(dwin_acc, dwin_hbm, sem.at[2]),
                   pltpu.make_async_copy(dwout_acc, dwout_hbm, sem.at[3])]
            for cp in cps:
                cp.start()
            for k in range(N_SHARD):
                win_v[k] = dwin_acc[k].astype(BF16)
            wout_v[...] = dwout_acc[...].astype(BF16)
            cps += [pltpu.make_async_copy(win_v, dwin16_hbm, sem.at[4]),
                    pltpu.make_async_copy(wout_v, dwout16_hbm, sem.at[5])]
            for cp in cps[2:]:
                cp.start()
            for cp in cps:
                cp.wait()
            for rc in _scatter_copies(dwup16_hbm, land_hbm, send_sems, recv_sems):
                rc.wait()

    def rev(width):
        return pl.BlockSpec((t, width), lambda i: (n_tiles - 1 - i, 0))

    v1024 = jax.ShapeDtypeStruct((1, D_MODEL), F32)
    v512 = jax.ShapeDtypeStruct((1, D_HALF), F32)
    small_shapes = [v1024, v1024, v1024, v1024, jax.ShapeDtypeStruct((CONV_K, D_HALF), F32), v512, v512, v512, v512,
                    v512, v1024, jax.ShapeDtypeStruct((N_HEADS, CHUNK, CHUNK), F32),
                    jax.ShapeDtypeStruct((N_HEADS, CHUNK), F32)]
    n_peer = len(PEER_FLIPS)
    return pl.pallas_call(
        body, grid=(n_tiles,), name="bwd_mixer",
        in_specs=[rev(D_MODEL), rev(D_MODEL), rev(4 * D_HALF), rev(D_HALF), rev(D_HALF), rev(D_MODEL),
                  rev(D_MODEL)] + [_full(v.shape) for v in vecs]
        + [_full(conv_w.shape), _full(wpair.shape), _full(wpair_t.shape), _full(causal_mask.shape), ANY, ANY, ANY],
        out_specs=[rev(D_MODEL)] + [_full(s.shape) for s in small_shapes] + [ANY] * 5,
        out_shape=[jax.ShapeDtypeStruct((seq, D_MODEL), F32)] + small_shapes
        + [jax.ShapeDtypeStruct(w_in_g.shape, F32), jax.ShapeDtypeStruct(w_out_g.shape, F32), _land_shape(dwup16),
           jax.ShapeDtypeStruct(w_in_g.shape, BF16), jax.ShapeDtypeStruct(w_out_g.shape, BF16)],
        scratch_shapes=[pltpu.VMEM(w_in_g.shape, BF16), pltpu.VMEM(w_out_g.shape, BF16),
                        pltpu.VMEM(w_in_g.shape, F32), pltpu.VMEM(w_out_g.shape, F32),
                        pltpu.VMEM((HALO, D_HALF), F32), pltpu.VMEM((8, t + HALO, D_HALF), F32),
                        pltpu.VMEM((CHUNK, D_HALF), F32), pltpu.SemaphoreType.DMA((6,)),
                        pltpu.SemaphoreType.DMA((n_peer,)), pltpu.SemaphoreType.DMA((n_peer,))],
        compiler_params=_params(),
    )(dx2, x, z, a1, sp, yb, o1, *vecs, conv_w, wpair, wpair_t, causal_mask, w_in_g, w_out_g, dwup16)


def _gmlp_operands(gm_ws, gm_bs):
    mask = jnp.tril(jnp.ones((CHUNK, CHUNK), F32))
    ws = gm_ws * mask[None]
    wpair = ws.reshape(N_HEADS // 2, 2 * CHUNK, CHUNK).astype(BF16)
    wpair_t = jnp.swapaxes(ws, 1, 2).reshape(N_HEADS // 2, 2 * CHUNK, CHUNK).astype(BF16)
    bs_full = jnp.repeat(jnp.transpose(gm_bs), HEAD_DIM, axis=1)
    return wpair, wpair_t, bs_full, mask


def _local_step(x, target, mod, p, w_in_g, w_out_g, w_up_g, w_down_g, u_dtype=F32):
    sh1, sc1, gt1, sh2, sc2, gt2 = [mod[:, k * D_MODEL:(k + 1) * D_MODEL] for k in range(6)]
    vec = dict(p, sh1=sh1, sc1=sc1, gt1=gt1, sh2=sh2, sc2=sc2, gt2=gt2)
    wpair, wpair_t, bs_full, mask = _gmlp_operands(p["gm_ws"], p["gm_bs"])

    z, a1, sp, yb, o1, x2, u = _fwd_mixer_up(x, vec, p["conv_dw_w"], wpair, bs_full, w_in_g, w_out_g, w_up_g, u_dtype)
    du, dx3, d_ffn_w, d_ffn_b, d_fg, d_gt2, loss, d_wd, d_wd16 = _ffn_tail(
        u, x2, target, p["ffn_dw_w"], p["ffn_dw_b"], gt2, p["final_gain"], w_down_g)
    by_shard = (N_SHARD, -1, D_MODEL)
    dx2, d_g2, d_sc2, d_sh2, d_wup, d_wup16, land_wd = _bwd_up(
        du, x2, dx3, p["norm2_gain"], sc2, sh2, w_up_g, d_wd16.reshape(by_shard))
    (gx, d_g1, d_sc1, d_sh1, d_gt1, d_cw, d_cb, d_clg, d_clb, d_vg, d_vb, d_mg, d_ws, d_bs, d_win, d_wout, land_wup,
     d_win16, d_wout16) = _bwd_mixer(dx2, x, z, a1, sp, yb, o1, vec, p["conv_dw_w"], wpair, wpair_t, mask, w_in_g,
                                     w_out_g, d_wup16)
    d_mod = jnp.concatenate([d_sh1, d_sc1, d_gt1, d_sh2, d_sc2, d_gt2], axis=1)
    grads = dict(norm1_gain=d_g1, conv_dw_w=d_cw, conv_dw_b=d_cb, conv_ln_g=d_clg, conv_ln_b=d_clb, gm_ln_g=d_vg,
                 gm_ln_b=d_vb, gm_ws=d_ws, gm_bs=d_bs, mix_out_gain=d_mg, norm2_gain=d_g2, ffn_dw_w=d_ffn_w,
                 ffn_dw_b=d_ffn_b, final_gain=d_fg, w_in=d_win, w_out=d_wout.reshape(by_shard), w_up=d_wup,
                 w_down=d_wd.reshape(by_shard))
    in_flight = dict(w_in16=d_win16, w_out16=d_wout16.reshape(by_shard), land_w_up=land_wup, land_w_down=land_wd)
    return gx, grads, d_mod, loss, in_flight


MESH = pl.DeviceIdType.MESH
VMEM_SPEC = pl.BlockSpec(memory_space=pltpu.VMEM)
PEER_FLIPS = [(a, b, d) for a in (0, 1) for b in (0, 1) for d in (0, 1)][1:]
CHIP_FLIPS = [(1, 0), (0, 1), (1, 1)]


def _coords():
    return lax.axis_index("x"), lax.axis_index("y"), lax.axis_index("c")


def _flip(v, bit):
    return 1 - v if bit else v


def _rows8(block):
    return pl.ds(pl.multiple_of(8 * block, 8), 8)


def _ada_mod(c_row, w_ada_sh, b_ada_sh):
    cols = w_ada_sh.shape[1]

    def body(c_ref, w_ref, b_ref, call_ref, mod_ref, cpad, modall, send_sems, recv_sems):
        x, y, c = _coords()
        me = 4 * x + 2 * y + c
        cpad[...] = jnp.zeros_like(cpad)
        cpad[pl.ds(0, 1), :] = c_ref[...]

        def gather_copy(j, flip):
            peer = (_flip(x, flip[0]), _flip(y, flip[1]), _flip(c, flip[2]))
            return pltpu.make_async_remote_copy(
                src_ref=cpad, dst_ref=call_ref.at[_rows8(me)], send_sem=send_sems.at[j], recv_sem=recv_sems.at[j],
                device_id=peer, device_id_type=MESH)

        copies = [gather_copy(j, f) for j, f in enumerate(PEER_FLIPS)]
        for cp in copies:
            cp.start()
        call_ref[_rows8(me), :] = cpad[...]
        for cp in copies:
            cp.wait_recv()
        for cp in copies:
            cp.wait_send()
        cv = call_ref[...]
        c_act = (cv * jax.nn.sigmoid(cv)).astype(BF16)
        modall[...] = _nn(c_act, w_ref[...].astype(BF16)) + b_ref[...]

        slot = _rows8(2 * x + y)

        def piece_copy(j, flip):
            tx, ty = _flip(x, flip[0]), _flip(y, flip[1])
            return pltpu.make_async_remote_copy(
                src_ref=modall.at[_rows8(4 * tx + 2 * ty + c)], dst_ref=mod_ref.at[slot],
                send_sem=send_sems.at[len(PEER_FLIPS) + j], recv_sem=recv_sems.at[len(PEER_FLIPS) + j],
                device_id=(tx, ty, c), device_id_type=MESH)

        pieces = [piece_copy(j, f) for j, f in enumerate(CHIP_FLIPS)]
        for cp in pieces:
            cp.start()
        mod_ref[slot, :] = modall[_rows8(me), :]
        for cp in pieces:
            cp.wait_recv()
        for cp in pieces:
            cp.wait_send()

    n_sem = len(PEER_FLIPS) + len(CHIP_FLIPS)
    return pl.pallas_call(
        body, name="ada_mod",
        in_specs=[VMEM_SPEC, VMEM_SPEC, VMEM_SPEC], out_specs=[VMEM_SPEC, VMEM_SPEC],
        out_shape=[jax.ShapeDtypeStruct((8 * N_DEV, D_MODEL), F32), jax.ShapeDtypeStruct((8 * N_SHARD, cols), F32)],
        scratch_shapes=[pltpu.VMEM((8, D_MODEL), F32), pltpu.VMEM((8 * N_DEV, cols), F32),
                        pltpu.SemaphoreType.DMA((n_sem,)), pltpu.SemaphoreType.DMA((n_sem,))],
        compiler_params=pltpu.CompilerParams(vmem_limit_bytes=VMEM_LIMIT_BYTES),
    )(c_row, w_ada_sh, b_ada_sh)


def _gather_weights(shards, filters):
    n = len(shards)
    nf = len(filters)

    def body(*refs):
        ins, f_ins = refs[:n], refs[n:n + nf]
        outs, f_outs = refs[n + nf:2 * n + nf], refs[2 * n + nf:2 * (n + nf)]
        stage = refs[2 * (n + nf):3 * n + 2 * nf]
        send_sems, recv_sems, local_sems, f_send_sems, f_recv_sems = refs[3 * n + 2 * nf:]
        x, y, c = _coords()
        k = 2 * x + y
        sibling = (x, y, 1 - c)

        def filter_copy(w, j, slot):
            tx, ty = _flip(x, CHIP_FLIPS[j][0]), _flip(y, CHIP_FLIPS[j][1])
            return pltpu.make_async_remote_copy(
                src_ref=f_ins[w], dst_ref=f_outs[w].at[slot], send_sem=f_send_sems.at[w, j],
                recv_sem=f_recv_sems.at[w, j], device_id=(tx, ty, c), device_id_type=MESH)

        def half(w, which):
            h = shards[w].shape[0] // 2
            return pl.ds(pl.multiple_of(which * h, 16), h)

        def ici_copy(w, j, src, slot):
            tx, ty = _flip(x, CHIP_FLIPS[j][0]), _flip(y, CHIP_FLIPS[j][1])
            return pltpu.make_async_remote_copy(
                src_ref=src, dst_ref=outs[w].at[slot, half(w, c)], send_sem=send_sems.at[w, j],
                recv_sem=recv_sems.at[w, j], device_id=(tx, ty, c), device_id_type=MESH)

        def d2d_copy(w, j, slot, which):
            rows = outs[w].at[slot, half(w, which)]
            return pltpu.make_async_remote_copy(
                src_ref=rows, dst_ref=rows, send_sem=send_sems.at[w, len(CHIP_FLIPS) + j],
                recv_sem=recv_sems.at[w, len(CHIP_FLIPS) + j], device_id=sibling, device_id_type=MESH)

        def chip_of(j):
            return 2 * _flip(x, CHIP_FLIPS[j][0]) + _flip(y, CHIP_FLIPS[j][1])

        local, first, passed = [], [], []
        for w in range(nf):
            local.append(pltpu.make_async_copy(f_ins[w], f_outs[w].at[k], local_sems.at[n + w]))
            local[-1].start()
            for j in range(len(CHIP_FLIPS)):
                first.append(filter_copy(w, j, k))
                first[-1].start()
        for w in range(n):
            stage[w][...] = ins[w][...].astype(BF16)
            local.append(pltpu.make_async_copy(stage[w], outs[w].at[k], local_sems.at[w]))
            local[-1].start()
            for j in range(len(CHIP_FLIPS)):
                first.append(ici_copy(w, j, stage[w].at[half(w, c)], k))
                first[-1].start()
        for w in range(nf):
            for j in range(len(CHIP_FLIPS)):
                filter_copy(w, j, chip_of(j)).wait_recv()
        for w in range(n):
            for j in range(len(CHIP_FLIPS)):
                ici_copy(w, j, stage[w].at[half(w, c)], chip_of(j)).wait_recv()
                passed.append(d2d_copy(w, j, chip_of(j), c))
                passed[-1].start()
        for w in range(n):
            for j in range(len(CHIP_FLIPS)):
                d2d_copy(w, j, chip_of(j), 1 - c).wait_recv()
        for cp in first + passed:
            cp.wait_send()
        for cp in local:
            cp.wait()

    sem_shape = (n, 2 * len(CHIP_FLIPS))
    f_sem_shape = (nf, len(CHIP_FLIPS))
    outs = pl.pallas_call(
        body, name="gather_weights",
        in_specs=[VMEM_SPEC] * (n + nf), out_specs=[ANY] * (n + nf),
        out_shape=[jax.ShapeDtypeStruct((N_SHARD,) + s.shape, BF16) for s in shards]
        + [jax.ShapeDtypeStruct((N_SHARD,) + s.shape, F32) for s in filters],
        scratch_shapes=[pltpu.VMEM(s.shape, BF16) for s in shards]
        + [pltpu.SemaphoreType.DMA(sem_shape), pltpu.SemaphoreType.DMA(sem_shape), pltpu.SemaphoreType.DMA((n + nf,)),
           pltpu.SemaphoreType.DMA(f_sem_shape), pltpu.SemaphoreType.DMA(f_sem_shape)],
        compiler_params=pltpu.CompilerParams(vmem_limit_bytes=VMEM_LIMIT_BYTES),
    )(*shards, *filters)
    return outs[:n], outs[n:]


def _final_comm(srcs16, small):
    n = len(srcs16)
    axes = ("c", "x", "y")

    def body(*refs):
        srcs, small_ref = refs[:n], refs[n]
        lands, small_out = refs[n + 1:2 * n + 1], refs[2 * n + 1]
        sent, got, send_sems, recv_sems, small_send_sems, small_recv_sems = refs[2 * n + 2:]
        x, y, c = _coords()
        copies = []
        for w in range(n):
            copies += _scatter_copies(srcs[w], lands[w], send_sems.at[w], recv_sems.at[w])
        for cp in copies:
            cp.start()
        current = small_ref
        for stage, axis in enumerate(axes):
            partner = (_flip(x, axis == "x"), _flip(y, axis == "y"), _flip(c, axis == "c"))
            rc = pltpu.make_async_remote_copy(
                src_ref=current, dst_ref=got.at[stage], send_sem=small_send_sems.at[stage],
                recv_sem=small_recv_sems.at[stage], device_id=partner, device_id_type=MESH)
            rc.start()
            rc.wait()
            target = small_out if stage == len(axes) - 1 else sent.at[stage]
            target[...] = current[...] + got[stage]
            current = target
        for cp in copies:
            cp.wait()

    n_peer = len(PEER_FLIPS)
    outs = pl.pallas_call(
        body, name="final_comm",
        in_specs=[ANY] * n + [VMEM_SPEC], out_specs=[ANY] * n + [VMEM_SPEC],
        out_shape=[_land_shape(a) for a in srcs16] + [jax.ShapeDtypeStruct(small.shape, F32)],
        scratch_shapes=[pltpu.VMEM((len(axes) - 1,) + small.shape, F32), pltpu.VMEM((len(axes),) + small.shape, F32),
                        pltpu.SemaphoreType.DMA((n, n_peer)), pltpu.SemaphoreType.DMA((n, n_peer)),
                        pltpu.SemaphoreType.DMA((len(axes),)), pltpu.SemaphoreType.DMA((len(axes),))],
        compiler_params=pltpu.CompilerParams(vmem_limit_bytes=VMEM_LIMIT_BYTES),
    )(*srcs16, small)
    return outs[:n], outs[n]


ADD_CHUNKS = 4


def _scatter_sum(pos, owns, lands):
    n = len(owns)

    def specs(land_shape):
        peers, rows, cols = land_shape
        if cols % (128 * ADD_CHUNKS) == 0:
            blk = (rows, cols // ADD_CHUNKS)
            return (pl.BlockSpec((1,) + blk, lambda i, p: (2 * p[0] + p[1], p[2], i)),
                    pl.BlockSpec((peers,) + blk, lambda i, p: (0, 0, i)),
                    pl.BlockSpec((1,) + blk, lambda i, p: (p[2], 0, i)))
        blk = (rows // ADD_CHUNKS, cols)
        return (pl.BlockSpec((1,) + blk, lambda i, p: (2 * p[0] + p[1], p[2] * ADD_CHUNKS + i, 0)),
                pl.BlockSpec((peers,) + blk, lambda i, p: (0, i, 0)),
                pl.BlockSpec((1,) + blk, lambda i, p: (p[2], i, 0)))

    def body(pos_ref, *refs):
        for idx in range(n):
            own, land, out = refs[idx], refs[n + idx], refs[2 * n + idx]
            total = own[0]
            for f in range(land.shape[0]):
                total = total + land[f].astype(F32)
            out[0] = total

    all_specs = [specs(l.shape) for l in lands]
    return pl.pallas_call(
        body, name="scatter_sum",
        grid_spec=pltpu.PrefetchScalarGridSpec(
            num_scalar_prefetch=1, grid=(ADD_CHUNKS,),
            in_specs=[s[0] for s in all_specs] + [s[1] for s in all_specs], out_specs=[s[2] for s in all_specs]),
        out_shape=[jax.ShapeDtypeStruct((2,) + l.shape[1:], F32) for l in lands],
        compiler_params=_params(),
    )(pos, *owns, *lands)


def _swap_halves(halves):
    n = len(halves)

    def body(*refs):
        ins, outs = refs[:n], refs[n:2 * n]
        send_sems, recv_sems = refs[2 * n:]
        x, y, c = _coords()
        copies = [pltpu.make_async_remote_copy(
            src_ref=ins[idx].at[pl.ds(c, 1)], dst_ref=outs[idx].at[pl.ds(c, 1)], send_sem=send_sems.at[idx],
            recv_sem=recv_sems.at[idx], device_id=(x, y, 1 - c), device_id_type=MESH) for idx in range(n)]
        for cp in copies:
            cp.start()
        for cp in copies:
            cp.wait()

    return pl.pallas_call(
        body, name="swap_halves",
        in_specs=[ANY] * n, out_specs=[ANY] * n, input_output_aliases={idx: idx for idx in range(n)},
        out_shape=[jax.ShapeDtypeStruct(a.shape, F32) for a in halves],
        scratch_shapes=[pltpu.SemaphoreType.DMA((n,)), pltpu.SemaphoreType.DMA((n,))],
    )(*halves)


def _adamw_math(w, g, m, v):
    m = ADAM_B1 * m + (1.0 - ADAM_B1) * g
    v = ADAM_B2 * v + (1.0 - ADAM_B2) * jnp.square(g)
    m_hat = m / (1.0 - ADAM_B1 ** ADAM_STEP)
    v_hat = v / (1.0 - ADAM_B2 ** ADAM_STEP)
    delta = -ADAM_LR * (m_hat / (jnp.sqrt(v_hat) + ADAM_EPS) + ADAM_WD * w)
    return delta, m, v


def _adamw(name, w, g, m, v, block_rows):
    rows, cols = w.shape

    def body(w_ref, g_ref, m_ref, v_ref, d_out, m_out, v_out):
        d_out[...], m_out[...], v_out[...] = _adamw_math(w_ref[...], g_ref[...], m_ref[...], v_ref[...])

    spec = pl.BlockSpec((block_rows, cols), lambda i: (i, 0))
    shape = jax.ShapeDtypeStruct((rows, cols), F32)
    return pl.pallas_call(
        body, grid=(rows // block_rows,), name=name, in_specs=[spec] * 4, out_specs=[spec] * 3,
        out_shape=[shape] * 3, compiler_params=_params(),
    )(w, g, m, v)


def _adamw_ada(c_all16, dmod16, w, m, v, block_rows):
    rows, cols = w.shape

    def body(c_ref, dm_ref, w_ref, m_ref, v_ref, g_out, d_out, m_out, v_out):
        cv = c_ref[...]
        g = _tn((cv * jax.nn.sigmoid(cv)).astype(BF16), dm_ref[...].astype(BF16))
        g_out[...] = g
        d_out[...], m_out[...], v_out[...] = _adamw_math(w_ref[...], g, m_ref[...], v_ref[...])

    spec = pl.BlockSpec((block_rows, cols), lambda i: (i, 0))
    shape = jax.ShapeDtypeStruct((rows, cols), F32)
    return pl.pallas_call(
        body, grid=(rows // block_rows,), name="adamw_w_ada",
        in_specs=[pl.BlockSpec((16, block_rows), lambda i: (0, i)), _full(dmod16.shape), spec, spec, spec],
        out_specs=[spec] * 4, out_shape=[shape] * 4, compiler_params=_params(),
    )(c_all16, dmod16, w, m, v)


SMALL_REPLICATED = ["b_ada", "norm1_gain", "conv_dw_b", "conv_ln_g", "conv_ln_b", "gm_ln_g", "gm_ln_b", "gm_ws", "gm_bs",
                    "mix_out_gain", "norm2_gain", "ffn_dw_b", "final_gain"]
SMALL_SHARDED = ["conv_dw_w", "ffn_dw_w"]
PACK_ROWS = 256
ADAM_PACK_ROWS = 160
WEIGHT_ORDER = ["w_ada", "b_ada", "norm1_gain", "w_in", "conv_dw_w", "conv_dw_b", "conv_ln_g", "conv_ln_b", "gm_ln_g",
                "gm_ln_b", "gm_ws", "gm_bs", "mix_out_gain", "w_out", "norm2_gain", "w_up", "ffn_dw_w", "ffn_dw_b",
                "w_down", "final_gain"]


def _pack(parts, rows):
    flat = jnp.concatenate([a.reshape(-1) for a in parts])
    return jnp.pad(flat, (0, rows * D_MODEL - flat.shape[0])).reshape(rows, D_MODEL)


def _unpack(packed, shapes):
    flat = packed.reshape(-1)
    out, pos = [], 0
    for s in shapes:
        size = 1
        for d in s:
            size *= d
        out.append(flat[pos:pos + size].reshape(s))
        pos += size
    return out


def kernel(x, c, w_ada, b_ada, norm1_gain, w_in, conv_dw_w, conv_dw_b, conv_ln_g, conv_ln_b, gm_ln_g, gm_ln_b, gm_ws, gm_bs, mix_out_gain, w_out, norm2_gain, w_up, ffn_dw_w, ffn_dw_b, w_down, final_gain, loss_target, m_w_ada, m_b_ada, m_norm1_gain, m_w_in, m_conv_dw_w, m_conv_dw_b, m_conv_ln_g, m_conv_ln_b, m_gm_ln_g, m_gm_ln_b, m_gm_ws, m_gm_bs, m_mix_out_gain, m_w_out, m_norm2_gain, m_w_up, m_ffn_dw_w, m_ffn_dw_b, m_w_down, m_final_gain, v_w_ada, v_b_ada, v_norm1_gain, v_w_in, v_conv_dw_w, v_conv_dw_b, v_conv_ln_g, v_conv_ln_b, v_gm_ln_g, v_gm_ln_b, v_gm_ws, v_gm_bs, v_mix_out_gain, v_w_out, v_norm2_gain, v_w_up, v_ffn_dw_w, v_ffn_dw_b, v_w_down, v_final_gain):
    weights = dict(w_ada=w_ada, b_ada=b_ada, norm1_gain=norm1_gain, w_in=w_in, conv_dw_w=conv_dw_w, conv_dw_b=conv_dw_b,
                   conv_ln_g=conv_ln_g, conv_ln_b=conv_ln_b, gm_ln_g=gm_ln_g, gm_ln_b=gm_ln_b, gm_ws=gm_ws, gm_bs=gm_bs,
                   mix_out_gain=mix_out_gain, w_out=w_out, norm2_gain=norm2_gain, w_up=w_up, ffn_dw_w=ffn_dw_w,
                   ffn_dw_b=ffn_dw_b, w_down=w_down, final_gain=final_gain)
    mom1 = dict(w_ada=m_w_ada, b_ada=m_b_ada, norm1_gain=m_norm1_gain, w_in=m_w_in, conv_dw_w=m_conv_dw_w,
                conv_dw_b=m_conv_dw_b, conv_ln_g=m_conv_ln_g, conv_ln_b=m_conv_ln_b, gm_ln_g=m_gm_ln_g, gm_ln_b=m_gm_ln_b,
                gm_ws=m_gm_ws, gm_bs=m_gm_bs, mix_out_gain=m_mix_out_gain, w_out=m_w_out, norm2_gain=m_norm2_gain,
                w_up=m_w_up, ffn_dw_w=m_ffn_dw_w, ffn_dw_b=m_ffn_dw_b, w_down=m_w_down, final_gain=m_final_gain)
    mom2 = dict(w_ada=v_w_ada, b_ada=v_b_ada, norm1_gain=v_norm1_gain, w_in=v_w_in, conv_dw_w=v_conv_dw_w,
                conv_dw_b=v_conv_dw_b, conv_ln_g=v_conv_ln_g, conv_ln_b=v_conv_ln_b, gm_ln_g=v_gm_ln_g, gm_ln_b=v_gm_ln_b,
                gm_ws=v_gm_ws, gm_bs=v_gm_bs, mix_out_gain=v_mix_out_gain, w_out=v_w_out, norm2_gain=v_norm2_gain,
                w_up=v_w_up, ffn_dw_w=v_ffn_dw_w, ffn_dw_b=v_ffn_dw_b, w_down=v_w_down, final_gain=v_final_gain)
    shard = 2 * lax.axis_index("x") + lax.axis_index("y")
    me = 2 * shard + lax.axis_index("c")

    ada_cols = w_ada.shape[2]
    b_ada_sh = lax.dynamic_slice(b_ada, (0, shard * ada_cols), (1, ada_cols))
    c_all64, mod32 = _ada_mod(c, w_ada[0], b_ada_sh)
    c_all = c_all64[::8]
    mod = mod32[::8].reshape(1, N_SHARD * ada_cols)

    (w_in_g, w_out_g, w_up_g, w_down_g), (conv_w_g, ffn_w_g) = _gather_weights(
        [w_in[0], w_out[0], w_up[0], w_down[0]], [conv_dw_w[0], ffn_dw_w[0]])
    conv_w_full = jnp.transpose(conv_w_g, (1, 0, 2)).reshape(CONV_K, D_HALF)
    ffn_w_full = jnp.transpose(ffn_w_g, (1, 0, 2)).reshape(FFN_K, 2 * D_FF)

    p = dict(norm1_gain=norm1_gain, conv_dw_w=conv_w_full, conv_dw_b=conv_dw_b, conv_ln_g=conv_ln_g,
             conv_ln_b=conv_ln_b, gm_ln_g=gm_ln_g, gm_ln_b=gm_ln_b, gm_ws=gm_ws[0], gm_bs=gm_bs[0],
             mix_out_gain=mix_out_gain, norm2_gain=norm2_gain, ffn_dw_w=ffn_w_full, ffn_dw_b=ffn_dw_b,
             final_gain=final_gain[None])
    grad_x, g, d_mod, loss, in_flight = _local_step(
        x[0], loss_target[0], mod, p, w_in_g, w_out_g.reshape(D_MODEL, D_MODEL), w_up_g,
        w_down_g.reshape(D_FF, D_MODEL))

    n_mod = d_mod.shape[1]
    dmod_rows = lax.dynamic_update_slice(jnp.zeros((N_DEV, n_mod), F32), d_mod, (me, 0))
    g["b_ada"] = d_mod
    small = _pack([g[k] for k in SMALL_REPLICATED] + [g[k] for k in SMALL_SHARDED] + [dmod_rows, loss[0, :1]], PACK_ROWS)
    (land_w_in, land_w_out), small = _final_comm([in_flight["w_in16"], in_flight["w_out16"]], small)
    pos = jnp.stack(_coords()).astype(jnp.int32)
    halves = _scatter_sum(pos, [g["w_in"], g["w_out"], g["w_up"], g["w_down"]],
                          [land_w_in, land_w_out, in_flight["land_w_up"], in_flight["land_w_down"]])
    full = _swap_halves(halves)
    grads = dict(w_in=full[0].reshape(w_in.shape[1:]), w_out=full[1].reshape(w_out.shape[1:]),
                 w_up=full[2].reshape(w_up.shape[1:]), w_down=full[3].reshape(w_down.shape[1:]))

    small_shapes = ([weights[k].shape for k in SMALL_REPLICATED] + [(CONV_K, D_HALF), (FFN_K, 2 * D_FF)]
                    + [(N_DEV, n_mod), (1,)])
    *small_grads, conv_w_grad, ffn_w_grad, dmod_all, loss_sum = _unpack(small, small_shapes)
    grads.update(zip(SMALL_REPLICATED, small_grads))
    grads["conv_dw_w"] = lax.dynamic_slice(conv_w_grad, (0, shard * conv_dw_w.shape[2]), conv_dw_w.shape[1:])[None]
    grads["ffn_dw_w"] = lax.dynamic_slice(ffn_w_grad, (0, shard * ffn_dw_w.shape[2]), ffn_dw_w.shape[1:])[None]

    delta, new_m, new_v = {}, {}, {}
    for name, block_rows in (("w_in", 256), ("w_out", 128), ("w_up", 256), ("w_down", 352)):
        delta[name], new_m[name], new_v[name] = [a[None] for a in _adamw(
            "adamw_" + name, weights[name][0], grads[name], mom1[name][0], mom2[name][0], block_rows)]
        grads[name] = grads[name][None]
    dmod_sh = lax.dynamic_slice(dmod_all, (0, shard * ada_cols), (N_DEV, ada_cols))
    pad8 = ((0, 16 - N_DEV), (0, 0))
    grads["w_ada"], delta["w_ada"], new_m["w_ada"], new_v["w_ada"] = [a[None] for a in _adamw_ada(
        jnp.pad(c_all, pad8), jnp.pad(dmod_sh, pad8), w_ada[0], m_w_ada[0], v_w_ada[0], 256)]
    small_names = SMALL_REPLICATED + SMALL_SHARDED
    packed = [_pack([d[k] for k in small_names], ADAM_PACK_ROWS) for d in (weights, grads, mom1, mom2)]
    small_out = _adamw("adamw_small", *packed, ADAM_PACK_ROWS)
    for d, arr in zip((delta, new_m, new_v), small_out):
        d.update(zip(small_names, _unpack(arr, [weights[k].shape for k in small_names])))

    return (loss_sum.reshape(()), grad_x[None], *[grads[k] for k in WEIGHT_ORDER], *[delta[k] for k in WEIGHT_ORDER],
            *[new_m[k] for k in WEIGHT_ORDER], *[new_v[k] for k in WEIGHT_ORDER])
```

```python
import functools

import jax
import jax.numpy as jnp
from jax import lax
from jax.experimental import pallas as pl
from jax.experimental.pallas import tpu as pltpu

F32 = jnp.float32
BF16 = jnp.bfloat16

D_MODEL = 1024
D_HALF = 512
D_FF = 2816
CONV_K = 31
FFN_K = 3
CHUNK = 128
N_HEADS = 8
HEAD_DIM = 64
N_SHARD = 4
N_DEV = 8
RMS_EPS = 1e-6
LN_EPS = 1e-5
ADAM_LR, ADAM_B1, ADAM_B2, ADAM_EPS, ADAM_WD, ADAM_STEP = 0.001, 0.9, 0.999, 1e-08, 0.01, 10

TILE = 256
HALO = 32
FFN_HALO = 8
FFN_BLK = 256
UP_SHARD = 2 * D_FF // N_SHARD
VMEM_LIMIT_BYTES = 56 * 1024 * 1024

ANY = pl.BlockSpec(memory_space=pl.ANY)
NT_DIMS = (((1,), (1,)), ((), ()))
TN_DIMS = (((0,), (0,)), ((), ()))


def _full(shape):
    return pl.BlockSpec(shape, lambda i: (0,) * len(shape))


def _nn(a, b):
    return jnp.dot(a, b, preferred_element_type=F32)


def _nt(a, b):
    return lax.dot_general(a, b, NT_DIMS, preferred_element_type=F32)


def _tn(a, b):
    return lax.dot_general(a, b, TN_DIMS, preferred_element_type=F32)


def _colsum(a):
    return jnp.sum(a, axis=0, keepdims=True)


def _params(semantics=("arbitrary",)):
    return pltpu.CompilerParams(dimension_semantics=semantics, vmem_limit_bytes=VMEM_LIMIT_BYTES)


def _rms(v, gain):
    return v * lax.rsqrt(jnp.mean(v * v, axis=-1, keepdims=True) + RMS_EPS) * gain


def _layer_norm(v, gain, bias):
    mu = jnp.mean(v, axis=-1, keepdims=True)
    var = jnp.mean(jnp.square(v - mu), axis=-1, keepdims=True)
    return (v - mu) * lax.rsqrt(var + LN_EPS) * gain + bias


def _mod_norm(v, gain, scale, shift):
    return _rms(v, gain) * (1.0 + scale) + shift


def _conv_branch(a1, ln_g, ln_b, out_gain):
    a2 = _layer_norm(a1, ln_g, ln_b)
    return _rms(a2 * jax.nn.sigmoid(a2), out_gain)


def _gate_branch(gu, sp, out_gain):
    return _rms(jax.nn.gelu(gu) * sp, out_gain)


def _gv_norm(gv, ln_g, ln_b):
    return _layer_norm(jax.nn.gelu(gv), ln_g, ln_b)


def _head_pair_matmul(wp_ref, v):
    lane = lax.broadcasted_iota(jnp.int32, (CHUNK, CHUNK), 1)
    rows = []
    for n in range(v.shape[0] // CHUNK):
        cols = []
        for j in range(N_HEADS // 2):
            r = _nn(wp_ref[j], v[n * CHUNK:(n + 1) * CHUNK, j * CHUNK:(j + 1) * CHUNK])
            cols.append(jnp.where(lane < HEAD_DIM, r[:CHUNK], r[CHUNK:]))
        rows.append(jnp.concatenate(cols, axis=1))
    return jnp.concatenate(rows, axis=0)


def _tile_bias(bs, tokens):
    return jnp.concatenate([bs] * (tokens // CHUNK), axis=0)


FORWARD_LEAD = 8


def _fwd_mixer(x, vec, conv_w, wpair, bs_full, w_in_g, w_out_g, late_parts):
    seq = x.shape[0]
    n_tiles = seq // TILE
    t = TILE
    n_late = len(late_parts)
    forward_step = max(n_tiles - FORWARD_LEAD, 0)
    names = ["norm1_gain", "sc1", "sh1", "gt1", "conv_dw_b", "conv_ln_g", "conv_ln_b", "gm_ln_g", "gm_ln_b",
             "mix_out_gain"]
    vecs = [vec[k] for k in names]

    def body(x_ref, g1, sc1, sh1, gt1, cb, clg, clb, vg, vb, mg, cw, wp, bs, win_hbm, wout_hbm, *rest):
        late = rest[n_late:2 * n_late]
        z_ref, a1_ref, sp_ref, y_ref, o1_ref, x2_ref = rest[2 * n_late:2 * n_late + 6]
        win_v, wout_v, halo, bank, sem, send_sems, recv_sems = rest[2 * n_late + 6:]
        i = pl.program_id(0)
        mx, my, mc = _coords()
        shard = 2 * mx + my

        def half(w, which):
            h = late[w].shape[1] // 2
            return pl.ds(pl.multiple_of(which * h, 16), h)

        def chip_of(j):
            return 2 * _flip(mx, CHIP_FLIPS[j][0]) + _flip(my, CHIP_FLIPS[j][1])

        def ici_copy(w, j, slot):
            rows = late[w].at[slot, half(w, mc)]
            return pltpu.make_async_remote_copy(
                src_ref=rows, dst_ref=rows, send_sem=send_sems.at[w, j], recv_sem=recv_sems.at[w, j],
                device_id=(_flip(mx, CHIP_FLIPS[j][0]), _flip(my, CHIP_FLIPS[j][1]), mc), device_id_type=MESH)

        def d2d_copy(w, j, which):
            rows = late[w].at[chip_of(j), half(w, which)]
            return pltpu.make_async_remote_copy(
                src_ref=rows, dst_ref=rows, send_sem=send_sems.at[w, len(CHIP_FLIPS) + j],
                recv_sem=recv_sems.at[w, len(CHIP_FLIPS) + j], device_id=(mx, my, 1 - mc), device_id_type=MESH)

        pairs = [(w, j) for w in range(n_late) for j in range(len(CHIP_FLIPS))]

        @pl.when(i == 0)
        def _():
            for w, j in pairs:
                ici_copy(w, j, shard).start()
            cps = [pltpu.make_async_copy(win_hbm, win_v, sem.at[0]),
                   pltpu.make_async_copy(wout_hbm, wout_v, sem.at[1])]
            for cp in cps:
                cp.start()
            for cp in cps:
                cp.wait()
            halo[...] = jnp.zeros_like(halo)

        @pl.when(i == forward_step)
        def _():
            for w, j in pairs:
                ici_copy(w, j, chip_of(j)).wait_recv()
                d2d_copy(w, j, mc).start()

        xv = x_ref[...]
        h1b = _mod_norm(xv, g1[...], sc1[...], sh1[...]).astype(BF16)
        zs = [_nn(h1b, win_v[k]) for k in range(N_SHARD)]
        for k in range(N_SHARD):
            z_ref[:, k * D_HALF:(k + 1) * D_HALF] = zs[k]
        ca, cg, gu, gv = zs
        a0 = ca * jax.nn.sigmoid(cg)
        ext = jnp.concatenate([halo[...], a0], axis=0)
        halo[...] = a0[t - HALO:]
        bank[0] = ext
        for b in range(1, 8):
            bank[b] = pltpu.roll(ext, b, axis=0)
        a1 = jnp.zeros((t, D_HALF), F32) + cb[...]
        for s in range(CONV_K):
            q, b = divmod(s, 8)
            a1 = a1 + bank[b, pl.ds(HALO - 8 * q, t), :] * cw[pl.ds(CONV_K - 1 - s, 1), :]
        a1_ref[...] = a1
        mgv = mg[...]
        ya = _conv_branch(a1, clg[...], clb[...], mgv[:, :D_HALF])
        gvn = _gv_norm(gv, vg[...], vb[...]).astype(BF16)
        sp = _head_pair_matmul(wp, gvn) + _tile_bias(bs[...], t)
        sp_ref[...] = sp
        yg = _gate_branch(gu, sp, mgv[:, D_HALF:])
        yb = jnp.concatenate([ya, yg], axis=1).astype(BF16)
        y_ref[...] = yb
        o1 = _nn(yb, wout_v[...])
        o1_ref[...] = o1
        x2_ref[...] = xv + gt1[...] * o1

        @pl.when(i == n_tiles - 1)
        def _():
            for w, j in pairs:
                d2d_copy(w, j, 1 - mc).wait_recv()
            for w, j in pairs:
                ici_copy(w, j, shard).wait_send()
                d2d_copy(w, j, mc).wait_send()

    def row(width):
        return pl.BlockSpec((t, width), lambda i: (i, 0))

    out_shape = [jax.ShapeDtypeStruct((seq, 4 * D_HALF), F32), jax.ShapeDtypeStruct((seq, D_HALF), F32),
                 jax.ShapeDtypeStruct((seq, D_HALF), F32), jax.ShapeDtypeStruct((seq, D_MODEL), BF16),
                 jax.ShapeDtypeStruct((seq, D_MODEL), F32), jax.ShapeDtypeStruct((seq, D_MODEL), F32)]
    n_in = 1 + len(vecs) + 3 + 2
    sem_shape = (n_late, 2 * len(CHIP_FLIPS))
    outs = pl.pallas_call(
        body, grid=(n_tiles,), name="fwd_mixer",
        in_specs=[row(D_MODEL)] + [_full(v.shape) for v in vecs]
        + [_full(conv_w.shape), _full(wpair.shape), _full(bs_full.shape), ANY, ANY] + [ANY] * n_late,
        out_specs=[ANY] * n_late + [row(4 * D_HALF), row(D_HALF), row(D_HALF), row(D_MODEL), row(D_MODEL),
                                    row(D_MODEL)],
        out_shape=[jax.ShapeDtypeStruct(a.shape, a.dtype) for a in late_parts] + out_shape,
        input_output_aliases={n_in + w: w for w in range(n_late)},
        scratch_shapes=[pltpu.VMEM(w_in_g.shape, BF16), pltpu.VMEM(w_out_g.shape, BF16),
                        pltpu.VMEM((HALO, D_HALF), F32), pltpu.VMEM((8, t + HALO, D_HALF), F32),
                        pltpu.SemaphoreType.DMA((2,)), pltpu.SemaphoreType.DMA(sem_shape),
                        pltpu.SemaphoreType.DMA(sem_shape)],
        compiler_params=_params(),
    )(x, *vecs, conv_w, wpair, bs_full, w_in_g, w_out_g, *late_parts)
    return outs[n_late:], outs[:n_late]


def _fwd_up(x2, norm2_gain, sc2, sh2, w_up_g, u_dtype):
    seq = x2.shape[0]
    n_tiles = seq // TILE
    t = TILE

    def body(x2_ref, g2, sc2_ref, sh2_ref, wup_hbm, u_ref, wup_v, sem):
        @pl.when(pl.program_id(0) == 0)
        def _():
            cp = pltpu.make_async_copy(wup_hbm, wup_v, sem.at[0])
            cp.start()
            cp.wait()

        h2b = _mod_norm(x2_ref[...], g2[...], sc2_ref[...], sh2_ref[...]).astype(BF16)
        for k in range(N_SHARD):
            u_ref[:, k * UP_SHARD:(k + 1) * UP_SHARD] = _nn(h2b, wup_v[k]).astype(u_ref.dtype)

    return pl.pallas_call(
        body, grid=(n_tiles,), name="fwd_up",
        in_specs=[pl.BlockSpec((t, D_MODEL), lambda i: (i, 0)), _full((1, D_MODEL)), _full((1, D_MODEL)),
                  _full((1, D_MODEL)), ANY],
        out_specs=pl.BlockSpec((t, 2 * D_FF), lambda i: (i, 0)),
        out_shape=jax.ShapeDtypeStruct((seq, 2 * D_FF), u_dtype),
        scratch_shapes=[pltpu.VMEM(w_up_g.shape, BF16), pltpu.SemaphoreType.DMA((1,))],
        compiler_params=_params(),
    )(x2, norm2_gain, sc2, sh2, w_up_g)


def _ffn_tail(u, x2, target, ffn_w, ffn_b, gt2, final_gain, w_down_g):
    seq = x2.shape[0]
    n_tiles = seq // TILE
    t = TILE
    n_blk = D_FF // FFN_BLK
    halo_rows = 16 if u.dtype == BF16 else 8
    inv_d = 1.0 / D_MODEL

    def final_norm(x3, gain):
        return _rms(x3, gain)

    def body(u_ref, uh_ref, x2_ref, tgt_ref, fw, fb, gt2_ref, fg, wd_hbm,
             du_ref, dx3_ref, dfw_ref, dfb_ref, dfg_ref, dgt2_ref, loss_ref, dwd_hbm, dwd16_hbm,
             wd_v, dwd_acc, carry, val_s, sil_s, dsil_s, f_s, sem):
        i = pl.program_id(0)
        tile = n_tiles - 1 - i

        @pl.when(i == 0)
        def _():
            cp = pltpu.make_async_copy(wd_hbm, wd_v, sem.at[0])
            cp.start()
            cp.wait()
            dwd_acc[...] = jnp.zeros_like(dwd_acc)
            carry[...] = jnp.zeros_like(carry)
            dfw_ref[...] = jnp.zeros_like(dfw_ref)
            dfb_ref[...] = jnp.zeros_like(dfb_ref)
            dfg_ref[...] = jnp.zeros_like(dfg_ref)
            dgt2_ref[...] = jnp.zeros_like(dgt2_ref)
            loss_ref[...] = jnp.zeros_like(loss_ref)

        def cols_of(j):
            return pl.ds(j * FFN_BLK, FFN_BLK), pl.ds(D_FF + j * FFN_BLK, FFN_BLK)

        def conv(cols):
            prev = uh_ref[pl.ds(halo_rows - FFN_HALO, FFN_HALO), cols].astype(F32)
            ext = jnp.concatenate([jnp.where(tile > 0, prev, 0.0), u_ref[:, cols].astype(F32)], axis=0)
            acc = fb[:, cols] + ext[FFN_HALO:] * fw[pl.ds(FFN_K - 1, 1), cols]
            for s in range(1, FFN_K):
                acc = acc + pltpu.roll(ext, s, axis=0)[FFN_HALO:] * fw[pl.ds(FFN_K - 1 - s, 1), cols]
            return acc

        o2 = jnp.zeros((t, D_MODEL), F32)
        for j in range(n_blk):
            cv, cg = cols_of(j)
            val, gate = conv(cv), conv(cg)
            sig = jax.nn.sigmoid(gate)
            sil = gate * sig
            fb16 = (sil * val).astype(BF16)
            val_s[:, cv] = val
            sil_s[:, cv] = sil
            dsil_s[:, cv] = sig + sil * (1.0 - sig)
            f_s[:, cv] = fb16
            o2 = o2 + _nn(fb16, wd_v[pl.ds(j * FFN_BLK, FFN_BLK), :])

        gt2v = gt2_ref[...]
        x3 = x2_ref[...] + gt2v * o2
        out, out_vjp = jax.vjp(final_norm, x3, fg[...])
        diff = out - tgt_ref[...]
        loss_ref[...] += jnp.zeros_like(loss_ref) + 0.5 * inv_d * jnp.sum(diff * diff)
        dx3, dfg = out_vjp(diff * inv_d)
        dfg_ref[...] += dfg
        dgt2_ref[...] += _colsum(dx3 * o2)
        dx3_ref[...] = dx3
        do2b = (gt2v * dx3).astype(BF16)

        for j in range(n_blk):
            cv, cg = cols_of(j)
            rows = pl.ds(j * FFN_BLK, FFN_BLK)
            df = _nt(do2b, wd_v[rows, :])
            dval = df * sil_s[:, cv]
            dgate = df * val_s[:, cv] * dsil_s[:, cv]
            dwd_acc[rows, :] += _tn(f_s[:, cv], do2b)
            for dd, cols in ((dval, cv), (dgate, cg)):
                dfb_ref[:, cols] += _colsum(dd)
                ext = jnp.concatenate([dd, carry[:, cols]], axis=0)
                uv = u_ref[:, cols].astype(F32)
                du = dd * fw[pl.ds(FFN_K - 1, 1), cols]
                dfw_ref[pl.ds(FFN_K - 1, 1), cols] += _colsum(dd * uv)
                for s in range(1, FFN_K):
                    shifted = pltpu.roll(ext, t + FFN_HALO - s, axis=0)[:t]
                    du = du + shifted * fw[pl.ds(FFN_K - 1 - s, 1), cols]
                    dfw_ref[pl.ds(FFN_K - 1 - s, 1), cols] += _colsum(shifted * uv)
                carry[:, cols] = dd[:FFN_HALO]
                du_ref[:, cols] = du.astype(BF16)

        @pl.when(i == n_tiles - 1)
        def _():
            cp = pltpu.make_async_copy(dwd_acc, dwd_hbm, sem.at[1])
            cp.start()
            wd_v[...] = dwd_acc[...].astype(BF16)
            cp16 = pltpu.make_async_copy(wd_v, dwd16_hbm, sem.at[2])
            cp16.start()
            cp.wait()
            cp16.wait()

    def rev(width):
        return pl.BlockSpec((t, width), lambda i: (n_tiles - 1 - i, 0))

    halo_spec = pl.BlockSpec(
        (halo_rows, 2 * D_FF), lambda i: (jnp.maximum((n_tiles - 1 - i) * (t // halo_rows) - 1, 0), 0))
    out_shape = [jax.ShapeDtypeStruct((seq, 2 * D_FF), BF16), jax.ShapeDtypeStruct((seq, D_MODEL), F32),
                 jax.ShapeDtypeStruct((FFN_K, 2 * D_FF), F32), jax.ShapeDtypeStruct((1, 2 * D_FF), F32),
                 jax.ShapeDtypeStruct((1, D_MODEL), F32), jax.ShapeDtypeStruct((1, D_MODEL), F32),
                 jax.ShapeDtypeStruct((1, 128), F32), jax.ShapeDtypeStruct((D_FF, D_MODEL), F32),
                 jax.ShapeDtypeStruct((D_FF, D_MODEL), BF16)]
    return pl.pallas_call(
        body, grid=(n_tiles,), name="ffn_tail",
        in_specs=[rev(2 * D_FF), halo_spec, rev(D_MODEL), rev(D_MODEL), _full(ffn_w.shape), _full(ffn_b.shape),
                  _full(gt2.shape), _full(final_gain.shape), ANY],
        out_specs=[rev(2 * D_FF), rev(D_MODEL), _full((FFN_K, 2 * D_FF)), _full((1, 2 * D_FF)), _full((1, D_MODEL)),
                   _full((1, D_MODEL)), _full((1, 128)), ANY, ANY],
        out_shape=out_shape,
        scratch_shapes=[pltpu.VMEM((D_FF, D_MODEL), BF16), pltpu.VMEM((D_FF, D_MODEL), F32),
                        pltpu.VMEM((FFN_HALO, 2 * D_FF), F32),
                        pltpu.VMEM((t, D_FF), F32), pltpu.VMEM((t, D_FF), F32), pltpu.VMEM((t, D_FF), F32),
                        pltpu.VMEM((t, D_FF), BF16), pltpu.SemaphoreType.DMA((3,))],
        compiler_params=_params(),
    )(u, u, x2, target, ffn_w, ffn_b, gt2, final_gain, w_down_g)


def _scatter_copies(src16, land, send_sems, recv_sems):
    x, y, c = _coords()
    h = src16.shape[1] // 2
    copies = []
    for f, flip in enumerate(PEER_FLIPS):
        tx, ty, tc = _flip(x, flip[0]), _flip(y, flip[1]), _flip(c, flip[2])
        copies.append(pltpu.make_async_remote_copy(
            src_ref=src16.at[2 * tx + ty, pl.ds(pl.multiple_of(tc * h, 16), h)], dst_ref=land.at[f],
            send_sem=send_sems.at[f], recv_sem=recv_sems.at[f], device_id=(tx, ty, tc), device_id_type=MESH))
    return copies


def _land_shape(src16):
    return jax.ShapeDtypeStruct((len(PEER_FLIPS), src16.shape[1] // 2, src16.shape[2]), BF16)


def _bwd_up(du, x2, dx3, norm2_gain, sc2, sh2, w_up_g, dwd16):
    seq = x2.shape[0]
    n_tiles = seq // TILE
    t = TILE

    def body(du_ref, x2_ref, dx3_ref, g2, sc2_ref, sh2_ref, wup_hbm, dwd16_hbm,
             dx2_ref, dg2_ref, dsc2_ref, dsh2_ref, dwup_hbm, dwup16_hbm, land_hbm,
             wup_v, dwup_acc, sem, send_sems, recv_sems):
        i = pl.program_id(0)

        @pl.when(i == 0)
        def _():
            for cp in _scatter_copies(dwd16_hbm, land_hbm, send_sems, recv_sems):
                cp.start()
            cp = pltpu.make_async_copy(wup_hbm, wup_v, sem.at[0])
            cp.start()
            cp.wait()
            dwup_acc[...] = jnp.zeros_like(dwup_acc)
            dg2_ref[...] = jnp.zeros_like(dg2_ref)
            dsc2_ref[...] = jnp.zeros_like(dsc2_ref)
            dsh2_ref[...] = jnp.zeros_like(dsh2_ref)

        h2, h2_vjp = jax.vjp(_mod_norm, x2_ref[...], g2[...], sc2_ref[...], sh2_ref[...])
        h2b = h2.astype(BF16)
        dh2 = jnp.zeros((t, D_MODEL), F32)
        for k in range(N_SHARD):
            dub = du_ref[:, k * UP_SHARD:(k + 1) * UP_SHARD]
            dh2 = dh2 + _nt(dub, wup_v[k])
            dwup_acc[k] += _tn(h2b, dub)
        dx2, dg2, dsc2, dsh2 = h2_vjp(dh2)
        dx2_ref[...] = dx3_ref[...] + dx2
        dg2_ref[...] += dg2
        dsc2_ref[...] += dsc2
        dsh2_ref[...] += dsh2

        @pl.when(i == n_tiles - 1)
        def _():
            cp = pltpu.make_async_copy(dwup_acc, dwup_hbm, sem.at[1])
            cp.start()
            for k in range(N_SHARD):
                wup_v[k] = dwup_acc[k].astype(BF16)
            cp16 = pltpu.make_async_copy(wup_v, dwup16_hbm, sem.at[2])
            cp16.start()
            cp.wait()
            cp16.wait()
            for rc in _scatter_copies(dwd16_hbm, land_hbm, send_sems, recv_sems):
                rc.wait()

    def row(width):
        return pl.BlockSpec((t, width), lambda i: (i, 0))

    vec = jax.ShapeDtypeStruct((1, D_MODEL), F32)
    n_peer = len(PEER_FLIPS)
    return pl.pallas_call(
        body, grid=(n_tiles,), name="bwd_up",
        in_specs=[row(2 * D_FF), row(D_MODEL), row(D_MODEL), _full((1, D_MODEL)), _full((1, D_MODEL)),
                  _full((1, D_MODEL)), ANY, ANY],
        out_specs=[row(D_MODEL), _full((1, D_MODEL)), _full((1, D_MODEL)), _full((1, D_MODEL)), ANY, ANY, ANY],
        out_shape=[jax.ShapeDtypeStruct((seq, D_MODEL), F32), vec, vec, vec,
                   jax.ShapeDtypeStruct(w_up_g.shape, F32), jax.ShapeDtypeStruct(w_up_g.shape, BF16),
                   _land_shape(dwd16)],
        scratch_shapes=[pltpu.VMEM(w_up_g.shape, BF16), pltpu.VMEM(w_up_g.shape, F32), pltpu.SemaphoreType.DMA((3,)),
                        pltpu.SemaphoreType.DMA((n_peer,)), pltpu.SemaphoreType.DMA((n_peer,))],
        compiler_params=_params(),
    )(du, x2, dx3, norm2_gain, sc2, sh2, w_up_g, dwd16)


def _bwd_mixer(dx2, x, z, a1, sp, yb, o1, vec, conv_w, wpair, wpair_t, causal_mask, w_in_g, w_out_g, dwup16):
    seq = x.shape[0]
    n_tiles = seq // TILE
    t = TILE
    names = ["norm1_gain", "sc1", "sh1", "gt1", "conv_ln_g", "conv_ln_b", "gm_ln_g", "gm_ln_b", "mix_out_gain"]
    vecs = [vec[k] for k in names]

    def body(dx2_ref, x_ref, z_ref, a1_ref, sp_ref, y_ref, o1_ref, g1, sc1, sh1, gt1, clg, clb, vg, vb, mg,
             cw, wp, wpt, mask_ref, win_hbm, wout_hbm, dwup16_hbm,
             gx_ref, dg1_ref, dsc1_ref, dsh1_ref, dgt1_ref, dcw_ref, dcb_ref, dclg_ref, dclb_ref, dvg_ref, dvb_ref,
             dmg_ref, dws_ref, dbs_ref, dwin_hbm, dwout_hbm, land_hbm, dwin16_hbm, dwout16_hbm,
             win_v, wout_v, dwin_acc, dwout_acc, carry, bank, dbs_acc, sem, send_sems, recv_sems):
        i = pl.program_id(0)
        small = [dg1_ref, dsc1_ref, dsh1_ref, dgt1_ref, dcw_ref, dcb_ref, dclg_ref, dclb_ref, dvg_ref, dvb_ref,
                 dmg_ref, dws_ref, dbs_acc]

        @pl.when(i == 0)
        def _():
            for cp in _scatter_copies(dwup16_hbm, land_hbm, send_sems, recv_sems):
                cp.start()
            cps = [pltpu.make_async_copy(win_hbm, win_v, sem.at[0]),
                   pltpu.make_async_copy(wout_hbm, wout_v, sem.at[1])]
            for cp in cps:
                cp.start()
            for cp in cps:
                cp.wait()
            dwin_acc[...] = jnp.zeros_like(dwin_acc)
            dwout_acc[...] = jnp.zeros_like(dwout_acc)
            carry[...] = jnp.zeros_like(carry)
            for ref in small:
                ref[...] = jnp.zeros_like(ref)

        dx2v = dx2_ref[...]
        gt1v = gt1[...]
        dgt1_ref[...] += _colsum(dx2v * o1_ref[...])
        do1b = (gt1v * dx2v).astype(BF16)
        dy = _nt(do1b, wout_v[...])
        dwout_acc[...] += _tn(y_ref[...], do1b)

        mgv = mg[...]
        _, conv_vjp = jax.vjp(_conv_branch, a1_ref[...], clg[...], clb[...], mgv[:, :D_HALF])
        da1, dclg, dclb, dmg_a = conv_vjp(dy[:, :D_HALF])
        dclg_ref[...] += dclg
        dclb_ref[...] += dclb
        gu = z_ref[:, 2 * D_HALF:3 * D_HALF]
        gv = z_ref[:, 3 * D_HALF:]
        spv = sp_ref[...]
        _, gate_vjp = jax.vjp(_gate_branch, gu, spv, mgv[:, D_HALF:])
        dgu, dsp, dmg_g = gate_vjp(dy[:, D_HALF:])
        dmg_ref[...] += jnp.concatenate([dmg_a, dmg_g], axis=1)
        gvn, gv_vjp = jax.vjp(_gv_norm, gv, vg[...], vb[...])
        gvnb = gvn.astype(BF16)
        dspb = dsp.astype(BF16)
        dgvn = _head_pair_matmul(wpt, dspb)
        dgv, dvg, dvb = gv_vjp(dgvn)
        dvg_ref[...] += dvg
        dvb_ref[...] += dvb
        lane = lax.broadcasted_iota(jnp.int32, (CHUNK, CHUNK), 1)
        dbs = jnp.zeros((CHUNK, D_HALF), F32)
        for n in range(t // CHUNK):
            rows = slice(n * CHUNK, (n + 1) * CHUNK)
            dbs = dbs + dsp[rows, :]
            for j in range(N_HEADS // 2):
                cols = slice(j * CHUNK, (j + 1) * CHUNK)
                blk = dspb[rows, cols]
                zero = jnp.zeros_like(blk)
                vblk = gvnb[rows, cols]
                dws_ref[2 * j] += _nt(jnp.where(lane < HEAD_DIM, blk, zero), vblk)
                dws_ref[2 * j + 1] += _nt(jnp.where(lane < HEAD_DIM, zero, blk), vblk)
        dbs_acc[...] += dbs

        ca = z_ref[:, :D_HALF]
        cg = z_ref[:, D_HALF:2 * D_HALF]
        sig = jax.nn.sigmoid(cg)
        a0 = ca * sig
        ext = jnp.concatenate([da1, carry[...]], axis=0)
        carry[...] = da1[:HALO]
        bank[0] = ext
        for b in range(1, 8):
            bank[b] = pltpu.roll(ext, t + HALO - b, axis=0)
        dcb_ref[...] += _colsum(da1)
        da0 = jnp.zeros((t, D_HALF), F32)
        for s in range(CONV_K):
            q, b = divmod(s, 8)
            shifted = bank[b, pl.ds(8 * q, t), :]
            da0 = da0 + shifted * cw[pl.ds(CONV_K - 1 - s, 1), :]
            dcw_ref[pl.ds(CONV_K - 1 - s, 1), :] += _colsum(shifted * a0)
        dca = da0 * sig
        dcg = da0 * ca * sig * (1.0 - sig)

        h1, h1_vjp = jax.vjp(_mod_norm, x_ref[...], g1[...], sc1[...], sh1[...])
        h1b = h1.astype(BF16)
        dh1 = jnp.zeros((t, D_MODEL), F32)
        for k, dzk in enumerate((dca, dcg, dgu, dgv)):
            dzb = dzk.astype(BF16)
            dh1 = dh1 + _nt(dzb, win_v[k])
            dwin_acc[k] += _tn(h1b, dzb)
        dx, dg1, dsc1, dsh1 = h1_vjp(dh1)
        gx_ref[...] = dx2v + dx
        dg1_ref[...] += dg1
        dsc1_ref[...] += dsc1
        dsh1_ref[...] += dsh1

        @pl.when(i == n_tiles - 1)
        def _():
            for h in range(N_HEADS):
                dws_ref[h] = dws_ref[h] * mask_ref[...]
            head_of_lane = lax.broadcasted_iota(jnp.int32, (N_HEADS, D_HALF), 1) // HEAD_DIM
            pick = (head_of_lane == lax.broadcasted_iota(jnp.int32, (N_HEADS, D_HALF), 0)).astype(F32)
            dbs_ref[...] = lax.dot_general(pick, dbs_acc[...], NT_DIMS, precision=lax.Precision.HIGHEST,
                                           preferred_element_type=F32)
            cps = [pltpu.make_async_copy(dwin_acc, dwin_hbm, sem.at[2]),
                   pltpu.make_async_copy(dwout_acc, dwout_hbm, sem.at[3])]
            for cp in cps:
                cp.start()
            for k in range(N_SHARD):
                win_v[k] = dwin_acc[k].astype(BF16)
            wout_v[...] = dwout_acc[...].astype(BF16)
            cps += [pltpu.make_async_copy(win_v, dwin16_hbm, sem.at[4]),
                    pltpu.make_async_copy(wout_v, dwout16_hbm, sem.at[5])]
            for cp in cps[2:]:
                cp.start()
            for cp in cps:
                cp.wait()
            for rc in _scatter_copies(dwup16_hbm, land_hbm, send_sems, recv_sems):
                rc.wait()

    def rev(width):
        return pl.BlockSpec((t, width), lambda i: (n_tiles - 1 - i, 0))

    v1024 = jax.ShapeDtypeStruct((1, D_MODEL), F32)
    v512 = jax.ShapeDtypeStruct((1, D_HALF), F32)
    small_shapes = [v1024, v1024, v1024, v1024, jax.ShapeDtypeStruct((CONV_K, D_HALF), F32), v512, v512, v512, v512,
                    v512, v1024, jax.ShapeDtypeStruct((N_HEADS, CHUNK, CHUNK), F32),
                    jax.ShapeDtypeStruct((N_HEADS, CHUNK), F32)]
    n_peer = len(PEER_FLIPS)
    return pl.pallas_call(
        body, grid=(n_tiles,), name="bwd_mixer",
        in_specs=[rev(D_MODEL), rev(D_MODEL), rev(4 * D_HALF), rev(D_HALF), rev(D_HALF), rev(D_MODEL),
                  rev(D_MODEL)] + [_full(v.shape) for v in vecs]
        + [_full(conv_w.shape), _full(wpair.shape), _full(wpair_t.shape), _full(causal_mask.shape), ANY, ANY, ANY],
        out_specs=[rev(D_MODEL)] + [_full(s.shape) for s in small_shapes] + [ANY] * 5,
        out_shape=[jax.ShapeDtypeStruct((seq, D_MODEL), F32)] + small_shapes
        + [jax.ShapeDtypeStruct(w_in_g.shape, F32), jax.ShapeDtypeStruct(w_out_g.shape, F32), _land_shape(dwup16),
           jax.ShapeDtypeStruct(w_in_g.shape, BF16), jax.ShapeDtypeStruct(w_out_g.shape, BF16)],
        scratch_shapes=[pltpu.VMEM(w_in_g.shape, BF16), pltpu.VMEM(w_out_g.shape, BF16),
                        pltpu.VMEM(w_in_g.shape, F32), pltpu.VMEM(w_out_g.shape, F32),
                        pltpu.VMEM((HALO, D_HALF), F32), pltpu.VMEM((8, t + HALO, D_HALF), F32),
                        pltpu.VMEM((CHUNK, D_HALF), F32), pltpu.SemaphoreType.DMA((6,)),
                        pltpu.SemaphoreType.DMA((n_peer,)), pltpu.SemaphoreType.DMA((n_peer,))],
        compiler_params=_params(),
    )(dx2, x, z, a1, sp, yb, o1, *vecs, conv_w, wpair, wpair_t, causal_mask, w_in_g, w_out_g, dwup16)


def _gmlp_operands(gm_ws, gm_bs):
    mask = jnp.tril(jnp.ones((CHUNK, CHUNK), F32))
    ws = gm_ws * mask[None]
    wpair = ws.reshape(N_HEADS // 2, 2 * CHUNK, CHUNK).astype(BF16)
    wpair_t = jnp.swapaxes(ws, 1, 2).reshape(N_HEADS // 2, 2 * CHUNK, CHUNK).astype(BF16)
    bs_full = jnp.repeat(jnp.transpose(gm_bs), HEAD_DIM, axis=1)
    return wpair, wpair_t, bs_full, mask


def _local_step(x, target, mod, p, w_in_g, w_out_g, w_up_part, w_down_part, u_dtype=F32):
    sh1, sc1, gt1, sh2, sc2, gt2 = [mod[:, k * D_MODEL:(k + 1) * D_MODEL] for k in range(6)]
    vec = dict(p, sh1=sh1, sc1=sc1, gt1=gt1, sh2=sh2, sc2=sc2, gt2=gt2)
    wpair, wpair_t, bs_full, mask = _gmlp_operands(p["gm_ws"], p["gm_bs"])

    (z, a1, sp, yb, o1, x2), (w_up_g, w_down_g) = _fwd_mixer(
        x, vec, p["conv_dw_w"], wpair, bs_full, w_in_g, w_out_g, [w_up_part, w_down_part])
    w_down_g = w_down_g.reshape(D_FF, D_MODEL)
    u = _fwd_up(x2, p["norm2_gain"], sc2, sh2, w_up_g, u_dtype)
    du, dx3, d_ffn_w, d_ffn_b, d_fg, d_gt2, loss, d_wd, d_wd16 = _ffn_tail(
        u, x2, target, p["ffn_dw_w"], p["ffn_dw_b"], gt2, p["final_gain"], w_down_g)
    by_shard = (N_SHARD, -1, D_MODEL)
    dx2, d_g2, d_sc2, d_sh2, d_wup, d_wup16, land_wd = _bwd_up(
        du, x2, dx3, p["norm2_gain"], sc2, sh2, w_up_g, d_wd16.reshape(by_shard))
    (gx, d_g1, d_sc1, d_sh1, d_gt1, d_cw, d_cb, d_clg, d_clb, d_vg, d_vb, d_mg, d_ws, d_bs, d_win, d_wout, land_wup,
     d_win16, d_wout16) = _bwd_mixer(dx2, x, z, a1, sp, yb, o1, vec, p["conv_dw_w"], wpair, wpair_t, mask, w_in_g,
                                     w_out_g, d_wup16)
    d_mod = jnp.concatenate([d_sh1, d_sc1, d_gt1, d_sh2, d_sc2, d_gt2], axis=1)
    grads = dict(norm1_gain=d_g1, conv_dw_w=d_cw, conv_dw_b=d_cb, conv_ln_g=d_clg, conv_ln_b=d_clb, gm_ln_g=d_vg,
                 gm_ln_b=d_vb, gm_ws=d_ws, gm_bs=d_bs, mix_out_gain=d_mg, norm2_gain=d_g2, ffn_dw_w=d_ffn_w,
                 ffn_dw_b=d_ffn_b, final_gain=d_fg, w_in=d_win, w_out=d_wout.reshape(by_shard), w_up=d_wup,
                 w_down=d_wd.reshape(by_shard))
    in_flight = dict(w_in16=d_win16, w_out16=d_wout16.reshape(by_shard), land_w_up=land_wup, land_w_down=land_wd)
    return gx, grads, d_mod, loss, in_flight


MESH = pl.DeviceIdType.MESH
VMEM_SPEC = pl.BlockSpec(memory_space=pltpu.VMEM)
PEER_FLIPS = [(a, b, d) for a in (0, 1) for b in (0, 1) for d in (0, 1)][1:]
CHIP_FLIPS = [(1, 0), (0, 1), (1, 1)]


def _coords():
    return lax.axis_index("x"), lax.axis_index("y"), lax.axis_index("c")


def _flip(v, bit):
    return 1 - v if bit else v


def _rows8(block):
    return pl.ds(pl.multiple_of(8 * block, 8), 8)


def _ada_mod(c_row, w_ada_sh, b_ada_sh):
    cols = w_ada_sh.shape[1]

    def body(c_ref, w_ref, b_ref, call_ref, mod_ref, cpad, modall, send_sems, recv_sems):
        x, y, c = _coords()
        me = 4 * x + 2 * y + c
        cpad[...] = jnp.zeros_like(cpad)
        cpad[pl.ds(0, 1), :] = c_ref[...]

        def gather_copy(j, flip):
            peer = (_flip(x, flip[0]), _flip(y, flip[1]), _flip(c, flip[2]))
            return pltpu.make_async_remote_copy(
                src_ref=cpad, dst_ref=call_ref.at[_rows8(me)], send_sem=send_sems.at[j], recv_sem=recv_sems.at[j],
                device_id=peer, device_id_type=MESH)

        copies = [gather_copy(j, f) for j, f in enumerate(PEER_FLIPS)]
        for cp in copies:
            cp.start()
        call_ref[_rows8(me), :] = cpad[...]
        for cp in copies:
            cp.wait_recv()
        for cp in copies:
            cp.wait_send()
        cv = call_ref[...]
        c_act = (cv * jax.nn.sigmoid(cv)).astype(BF16)
        modall[...] = _nn(c_act, w_ref[...].astype(BF16)) + b_ref[...]

        slot = _rows8(2 * x + y)

        def piece_copy(j, flip):
            tx, ty = _flip(x, flip[0]), _flip(y, flip[1])
            return pltpu.make_async_remote_copy(
                src_ref=modall.at[_rows8(4 * tx + 2 * ty + c)], dst_ref=mod_ref.at[slot],
                send_sem=send_sems.at[len(PEER_FLIPS) + j], recv_sem=recv_sems.at[len(PEER_FLIPS) + j],
                device_id=(tx, ty, c), device_id_type=MESH)

        pieces = [piece_copy(j, f) for j, f in enumerate(CHIP_FLIPS)]
        for cp in pieces:
            cp.start()
        mod_ref[slot, :] = modall[_rows8(me), :]
        for cp in pieces:
            cp.wait_recv()
        for cp in pieces:
            cp.wait_send()

    n_sem = len(PEER_FLIPS) + len(CHIP_FLIPS)
    return pl.pallas_call(
        body, name="ada_mod",
        in_specs=[VMEM_SPEC, VMEM_SPEC, VMEM_SPEC], out_specs=[VMEM_SPEC, VMEM_SPEC],
        out_shape=[jax.ShapeDtypeStruct((8 * N_DEV, D_MODEL), F32), jax.ShapeDtypeStruct((8 * N_SHARD, cols), F32)],
        scratch_shapes=[pltpu.VMEM((8, D_MODEL), F32), pltpu.VMEM((8 * N_DEV, cols), F32),
                        pltpu.SemaphoreType.DMA((n_sem,)), pltpu.SemaphoreType.DMA((n_sem,))],
        compiler_params=pltpu.CompilerParams(vmem_limit_bytes=VMEM_LIMIT_BYTES),
    )(c_row, w_ada_sh, b_ada_sh)


def _gather_weights(shards, filters, n_now):
    n = len(shards)
    nf = len(filters)

    def body(*refs):
        ins, f_ins = refs[:n], refs[n:n + nf]
        outs, f_outs = refs[n + nf:2 * n + nf], refs[2 * n + nf:2 * (n + nf)]
        stage = refs[2 * (n + nf):3 * n + 2 * nf]
        send_sems, recv_sems, local_sems, f_send_sems, f_recv_sems = refs[3 * n + 2 * nf:]
        x, y, c = _coords()
        k = 2 * x + y
        sibling = (x, y, 1 - c)

        def filter_copy(w, j, slot):
            tx, ty = _flip(x, CHIP_FLIPS[j][0]), _flip(y, CHIP_FLIPS[j][1])
            return pltpu.make_async_remote_copy(
                src_ref=f_ins[w], dst_ref=f_outs[w].at[slot], send_sem=f_send_sems.at[w, j],
                recv_sem=f_recv_sems.at[w, j], device_id=(tx, ty, c), device_id_type=MESH)

        def half(w, which):
            h = shards[w].shape[0] // 2
            return pl.ds(pl.multiple_of(which * h, 16), h)

        def ici_copy(w, j, src, slot):
            tx, ty = _flip(x, CHIP_FLIPS[j][0]), _flip(y, CHIP_FLIPS[j][1])
            return pltpu.make_async_remote_copy(
                src_ref=src, dst_ref=outs[w].at[slot, half(w, c)], send_sem=send_sems.at[w, j],
                recv_sem=recv_sems.at[w, j], device_id=(tx, ty, c), device_id_type=MESH)

        def d2d_copy(w, j, slot, which):
            rows = outs[w].at[slot, half(w, which)]
            return pltpu.make_async_remote_copy(
                src_ref=rows, dst_ref=rows, send_sem=send_sems.at[w, len(CHIP_FLIPS) + j],
                recv_sem=recv_sems.at[w, len(CHIP_FLIPS) + j], device_id=sibling, device_id_type=MESH)

        def chip_of(j):
            return 2 * _flip(x, CHIP_FLIPS[j][0]) + _flip(y, CHIP_FLIPS[j][1])

        local, first, passed = [], [], []
        for w in range(nf):
            local.append(pltpu.make_async_copy(f_ins[w], f_outs[w].at[k], local_sems.at[n + w]))
            local[-1].start()
            for j in range(len(CHIP_FLIPS)):
                first.append(filter_copy(w, j, k))
                first[-1].start()
        for w in range(n):
            stage[w][...] = ins[w][...].astype(BF16)
            local.append(pltpu.make_async_copy(stage[w], outs[w].at[k], local_sems.at[w]))
            local[-1].start()
            if w < n_now:
                for j in range(len(CHIP_FLIPS)):
                    first.append(ici_copy(w, j, stage[w].at[half(w, c)], k))
                    first[-1].start()
        for w in range(nf):
            for j in range(len(CHIP_FLIPS)):
                filter_copy(w, j, chip_of(j)).wait_recv()
        for w in range(n_now):
            for j in range(len(CHIP_FLIPS)):
                ici_copy(w, j, stage[w].at[half(w, c)], chip_of(j)).wait_recv()
                passed.append(d2d_copy(w, j, chip_of(j), c))
                passed[-1].start()
        for w in range(n_now):
            for j in range(len(CHIP_FLIPS)):
                d2d_copy(w, j, chip_of(j), 1 - c).wait_recv()
        for cp in first + passed:
            cp.wait_send()
        for cp in local:
            cp.wait()

    sem_shape = (n_now, 2 * len(CHIP_FLIPS))
    f_sem_shape = (nf, len(CHIP_FLIPS))
    outs = pl.pallas_call(
        body, name="gather_weights",
        in_specs=[VMEM_SPEC] * (n + nf), out_specs=[ANY] * (n + nf),
        out_shape=[jax.ShapeDtypeStruct((N_SHARD,) + s.shape, BF16) for s in shards]
        + [jax.ShapeDtypeStruct((N_SHARD,) + s.shape, F32) for s in filters],
        scratch_shapes=[pltpu.VMEM(s.shape, BF16) for s in shards]
        + [pltpu.SemaphoreType.DMA(sem_shape), pltpu.SemaphoreType.DMA(sem_shape), pltpu.SemaphoreType.DMA((n + nf,)),
           pltpu.SemaphoreType.DMA(f_sem_shape), pltpu.SemaphoreType.DMA(f_sem_shape)],
        compiler_params=pltpu.CompilerParams(vmem_limit_bytes=VMEM_LIMIT_BYTES),
    )(*shards, *filters)
    return outs[:n], outs[n:]


def _final_comm(srcs16, small):
    n = len(srcs16)
    axes = ("c", "x", "y")

    def body(*refs):
        srcs, small_ref = refs[:n], refs[n]
        lands, small_out = refs[n + 1:2 * n + 1], refs[2 * n + 1]
        sent, got, send_sems, recv_sems, small_send_sems, small_recv_sems = refs[2 * n + 2:]
        x, y, c = _coords()
        copies = []
        for w in range(n):
            copies += _scatter_copies(srcs[w], lands[w], send_sems.at[w], recv_sems.at[w])
        for cp in copies:
            cp.start()
        current = small_ref
        for stage, axis in enumerate(axes):
            partner = (_flip(x, axis == "x"), _flip(y, axis == "y"), _flip(c, axis == "c"))
            rc = pltpu.make_async_remote_copy(
                src_ref=current, dst_ref=got.at[stage], send_sem=small_send_sems.at[stage],
                recv_sem=small_recv_sems.at[stage], device_id=partner, device_id_type=MESH)
            rc.start()
            rc.wait()
            target = small_out if stage == len(axes) - 1 else sent.at[stage]
            target[...] = current[...] + got[stage]
            current = target
        for cp in copies:
            cp.wait()

    n_peer = len(PEER_FLIPS)
    outs = pl.pallas_call(
        body, name="final_comm",
        in_specs=[ANY] * n + [VMEM_SPEC], out_specs=[ANY] * n + [VMEM_SPEC],
        out_shape=[_land_shape(a) for a in srcs16] + [jax.ShapeDtypeStruct(small.shape, F32)],
        scratch_shapes=[pltpu.VMEM((len(axes) - 1,) + small.shape, F32), pltpu.VMEM((len(axes),) + small.shape, F32),
                        pltpu.SemaphoreType.DMA((n, n_peer)), pltpu.SemaphoreType.DMA((n, n_peer)),
                        pltpu.SemaphoreType.DMA((len(axes),)), pltpu.SemaphoreType.DMA((len(axes),))],
        compiler_params=pltpu.CompilerParams(vmem_limit_bytes=VMEM_LIMIT_BYTES),
    )(*srcs16, small)
    return outs[:n], outs[n]


ADD_CHUNKS = 4


def _scatter_sum(pos, owns, lands):
    n = len(owns)

    def specs(land_shape):
        peers, rows, cols = land_shape
        if cols % (128 * ADD_CHUNKS) == 0:
            blk = (rows, cols // ADD_CHUNKS)
            return (pl.BlockSpec((1,) + blk, lambda i, p: (2 * p[0] + p[1], p[2], i)),
                    pl.BlockSpec((peers,) + blk, lambda i, p: (0, 0, i)),
                    pl.BlockSpec((1,) + blk, lambda i, p: (p[2], 0, i)))
        blk = (rows // ADD_CHUNKS, cols)
        return (pl.BlockSpec((1,) + blk, lambda i, p: (2 * p[0] + p[1], p[2] * ADD_CHUNKS + i, 0)),
                pl.BlockSpec((peers,) + blk, lambda i, p: (0, i, 0)),
                pl.BlockSpec((1,) + blk, lambda i, p: (p[2], i, 0)))

    def body(pos_ref, *refs):
        for idx in range(n):
            own, land, out = refs[idx], refs[n + idx], refs[2 * n + idx]
            total = own[0]
            for f in range(land.shape[0]):
                total = total + land[f].astype(F32)
            out[0] = total

    all_specs = [specs(l.shape) for l in lands]
    return pl.pallas_call(
        body, name="scatter_sum",
        grid_spec=pltpu.PrefetchScalarGridSpec(
            num_scalar_prefetch=1, grid=(ADD_CHUNKS,),
            in_specs=[s[0] for s in all_specs] + [s[1] for s in all_specs], out_specs=[s[2] for s in all_specs]),
        out_shape=[jax.ShapeDtypeStruct((2,) + l.shape[1:], F32) for l in lands],
        compiler_params=_params(),
    )(pos, *owns, *lands)


def _swap_halves(halves):
    n = len(halves)

    def body(*refs):
        ins, outs = refs[:n], refs[n:2 * n]
        send_sems, recv_sems = refs[2 * n:]
        x, y, c = _coords()
        copies = [pltpu.make_async_remote_copy(
            src_ref=ins[idx].at[pl.ds(c, 1)], dst_ref=outs[idx].at[pl.ds(c, 1)], send_sem=send_sems.at[idx],
            recv_sem=recv_sems.at[idx], device_id=(x, y, 1 - c), device_id_type=MESH) for idx in range(n)]
        for cp in copies:
            cp.start()
        for cp in copies:
            cp.wait()

    return pl.pallas_call(
        body, name="swap_halves",
        in_specs=[ANY] * n, out_specs=[ANY] * n, input_output_aliases={idx: idx for idx in range(n)},
        out_shape=[jax.ShapeDtypeStruct(a.shape, F32) for a in halves],
        scratch_shapes=[pltpu.SemaphoreType.DMA((n,)), pltpu.SemaphoreType.DMA((n,))],
    )(*halves)


def _adamw_math(w, g, m, v):
    m = ADAM_B1 * m + (1.0 - ADAM_B1) * g
    v = ADAM_B2 * v + (1.0 - ADAM_B2) * jnp.square(g)
    m_hat = m / (1.0 - ADAM_B1 ** ADAM_STEP)
    v_hat = v / (1.0 - ADAM_B2 ** ADAM_STEP)
    delta = -ADAM_LR * (m_hat / (jnp.sqrt(v_hat) + ADAM_EPS) + ADAM_WD * w)
    return delta, m, v


def _adamw(name, w, g, m, v, block_rows):
    rows, cols = w.shape

    def body(w_ref, g_ref, m_ref, v_ref, d_out, m_out, v_out):
        d_out[...], m_out[...], v_out[...] = _adamw_math(w_ref[...], g_ref[...], m_ref[...], v_ref[...])

    spec = pl.BlockSpec((block_rows, cols), lambda i: (i, 0))
    shape = jax.ShapeDtypeStruct((rows, cols), F32)
    return pl.pallas_call(
        body, grid=(rows // block_rows,), name=name, in_specs=[spec] * 4, out_specs=[spec] * 3,
        out_shape=[shape] * 3, compiler_params=_params(),
    )(w, g, m, v)


def _adamw_ada(c_all16, dmod16, w, m, v, block_rows):
    rows, cols = w.shape

    def body(c_ref, dm_ref, w_ref, m_ref, v_ref, g_out, d_out, m_out, v_out):
        cv = c_ref[...]
        g = _tn((cv * jax.nn.sigmoid(cv)).astype(BF16), dm_ref[...].astype(BF16))
        g_out[...] = g
        d_out[...], m_out[...], v_out[...] = _adamw_math(w_ref[...], g, m_ref[...], v_ref[...])

    spec = pl.BlockSpec((block_rows, cols), lambda i: (i, 0))
    shape = jax.ShapeDtypeStruct((rows, cols), F32)
    return pl.pallas_call(
        body, grid=(rows // block_rows,), name="adamw_w_ada",
        in_specs=[pl.BlockSpec((16, block_rows), lambda i: (0, i)), _full(dmod16.shape), spec, spec, spec],
        out_specs=[spec] * 4, out_shape=[shape] * 4, compiler_params=_params(),
    )(c_all16, dmod16, w, m, v)


SMALL_REPLICATED = ["b_ada", "norm1_gain", "conv_dw_b", "conv_ln_g", "conv_ln_b", "gm_ln_g", "gm_ln_b", "gm_ws", "gm_bs",
                    "mix_out_gain", "norm2_gain", "ffn_dw_b", "final_gain"]
SMALL_SHARDED = ["conv_dw_w", "ffn_dw_w"]
PACK_ROWS = 256
ADAM_PACK_ROWS = 160
WEIGHT_ORDER = ["w_ada", "b_ada", "norm1_gain", "w_in", "conv_dw_w", "conv_dw_b", "conv_ln_g", "conv_ln_b", "gm_ln_g",
                "gm_ln_b", "gm_ws", "gm_bs", "mix_out_gain", "w_out", "norm2_gain", "w_up", "ffn_dw_w", "ffn_dw_b",
                "w_down", "final_gain"]


def _pack(parts, rows):
    flat = jnp.concatenate([a.reshape(-1) for a in parts])
    return jnp.pad(flat, (0, rows * D_MODEL - flat.shape[0])).reshape(rows, D_MODEL)


def _unpack(packed, shapes):
    flat = packed.reshape(-1)
    out, pos = [], 0
    for s in shapes:
        size = 1
        for d in s:
            size *= d
        out.append(flat[pos:pos + size].reshape(s))
        pos += size
    return out


def kernel(x, c, w_ada, b_ada, norm1_gain, w_in, conv_dw_w, conv_dw_b, conv_ln_g, conv_ln_b, gm_ln_g, gm_ln_b, gm_ws, gm_bs, mix_out_gain, w_out, norm2_gain, w_up, ffn_dw_w, ffn_dw_b, w_down, final_gain, loss_target, m_w_ada, m_b_ada, m_norm1_gain, m_w_in, m_conv_dw_w, m_conv_dw_b, m_conv_ln_g, m_conv_ln_b, m_gm_ln_g, m_gm_ln_b, m_gm_ws, m_gm_bs, m_mix_out_gain, m_w_out, m_norm2_gain, m_w_up, m_ffn_dw_w, m_ffn_dw_b, m_w_down, m_final_gain, v_w_ada, v_b_ada, v_norm1_gain, v_w_in, v_conv_dw_w, v_conv_dw_b, v_conv_ln_g, v_conv_ln_b, v_gm_ln_g, v_gm_ln_b, v_gm_ws, v_gm_bs, v_mix_out_gain, v_w_out, v_norm2_gain, v_w_up, v_ffn_dw_w, v_ffn_dw_b, v_w_down, v_final_gain):
    weights = dict(w_ada=w_ada, b_ada=b_ada, norm1_gain=norm1_gain, w_in=w_in, conv_dw_w=conv_dw_w, conv_dw_b=conv_dw_b,
                   conv_ln_g=conv_ln_g, conv_ln_b=conv_ln_b, gm_ln_g=gm_ln_g, gm_ln_b=gm_ln_b, gm_ws=gm_ws, gm_bs=gm_bs,
                   mix_out_gain=mix_out_gain, w_out=w_out, norm2_gain=norm2_gain, w_up=w_up, ffn_dw_w=ffn_dw_w,
                   ffn_dw_b=ffn_dw_b, w_down=w_down, final_gain=final_gain)
    mom1 = dict(w_ada=m_w_ada, b_ada=m_b_ada, norm1_gain=m_norm1_gain, w_in=m_w_in, conv_dw_w=m_conv_dw_w,
                conv_dw_b=m_conv_dw_b, conv_ln_g=m_conv_ln_g, conv_ln_b=m_conv_ln_b, gm_ln_g=m_gm_ln_g, gm_ln_b=m_gm_ln_b,
                gm_ws=m_gm_ws, gm_bs=m_gm_bs, mix_out_gain=m_mix_out_gain, w_out=m_w_out, norm2_gain=m_norm2_gain,
                w_up=m_w_up, ffn_dw_w=m_ffn_dw_w, ffn_dw_b=m_ffn_dw_b, w_down=m_w_down, final_gain=m_final_gain)
    mom2 = dict(w_ada=v_w_ada, b_ada=v_b_ada, norm1_gain=v_norm1_gain, w_in=v_w_in, conv_dw_w=v_conv_dw_w,
                conv_dw_b=v_conv_dw_b, conv_ln_g=v_conv_ln_g, conv_ln_b=v_conv_ln_b, gm_ln_g=v_gm_ln_g, gm_ln_b=v_gm_ln_b,
                gm_ws=v_gm_ws, gm_bs=v_gm_bs, mix_out_gain=v_mix_out_gain, w_out=v_w_out, norm2_gain=v_norm2_gain,
                w_up=v_w_up, ffn_dw_w=v_ffn_dw_w, ffn_dw_b=v_ffn_dw_b, w_down=v_w_down, final_gain=v_final_gain)
    shard = 2 * lax.axis_index("x") + lax.axis_index("y")
    me = 2 * shard + lax.axis_index("c")

    ada_cols = w_ada.shape[2]
    b_ada_sh = lax.dynamic_slice(b_ada, (0, shard * ada_cols), (1, ada_cols))
    c_all64, mod32 = _ada_mod(c, w_ada[0], b_ada_sh)
    c_all = c_all64[::8]
    mod = mod32[::8].reshape(1, N_SHARD * ada_cols)

    (w_in_g, w_out_g, w_up_part, w_down_part), (conv_w_g, ffn_w_g) = _gather_weights(
        [w_in[0], w_out[0], w_up[0], w_down[0]], [conv_dw_w[0], ffn_dw_w[0]], n_now=2)
    conv_w_full = jnp.transpose(conv_w_g, (1, 0, 2)).reshape(CONV_K, D_HALF)
    ffn_w_full = jnp.transpose(ffn_w_g, (1, 0, 2)).reshape(FFN_K, 2 * D_FF)

    p = dict(norm1_gain=norm1_gain, conv_dw_w=conv_w_full, conv_dw_b=conv_dw_b, conv_ln_g=conv_ln_g,
             conv_ln_b=conv_ln_b, gm_ln_g=gm_ln_g, gm_ln_b=gm_ln_b, gm_ws=gm_ws[0], gm_bs=gm_bs[0],
             mix_out_gain=mix_out_gain, norm2_gain=norm2_gain, ffn_dw_w=ffn_w_full, ffn_dw_b=ffn_dw_b,
             final_gain=final_gain[None])
    grad_x, g, d_mod, loss, in_flight = _local_step(
        x[0], loss_target[0], mod, p, w_in_g, w_out_g.reshape(D_MODEL, D_MODEL), w_up_part, w_down_part)

    n_mod = d_mod.shape[1]
    dmod_rows = lax.dynamic_update_slice(jnp.zeros((N_DEV, n_mod), F32), d_mod, (me, 0))
    g["b_ada"] = d_mod
    small = _pack([g[k] for k in SMALL_REPLICATED] + [g[k] for k in SMALL_SHARDED] + [dmod_rows, loss[0, :1]], PACK_ROWS)
    (land_w_in, land_w_out), small = _final_comm([in_flight["w_in16"], in_flight["w_out16"]], small)
    pos = jnp.stack(_coords()).astype(jnp.int32)
    halves = _scatter_sum(pos, [g["w_in"], g["w_out"], g["w_up"], g["w_down"]],
                          [land_w_in, land_w_out, in_flight["land_w_up"], in_flight["land_w_down"]])
    full = _swap_halves(halves)
    grads = dict(w_in=full[0].reshape(w_in.shape[1:]), w_out=full[1].reshape(w_out.shape[1:]),
                 w_up=full[2].reshape(w_up.shape[1:]), w_down=full[3].reshape(w_down.shape[1:]))

    small_shapes = ([weights[k].shape for k in SMALL_REPLICATED] + [(CONV_K, D_HALF), (FFN_K, 2 * D_FF)]
                    + [(N_DEV, n_mod), (1,)])
    *small_grads, conv_w_grad, ffn_w_grad, dmod_all, loss_sum = _unpack(small, small_shapes)
    grads.update(zip(SMALL_REPLICATED, small_grads))
    grads["conv_dw_w"] = lax.dynamic_slice(conv_w_grad, (0, shard * conv_dw_w.shape[2]), conv_dw_w.shape[1:])[None]
    grads["ffn_dw_w"] = lax.dynamic_slice(ffn_w_grad, (0, shard * ffn_dw_w.shape[2]), ffn_dw_w.shape[1:])[None]

    delta, new_m, new_v = {}, {}, {}
    for name, block_rows in (("w_in", 256), ("w_out", 128), ("w_up", 256), ("w_down", 352)):
        delta[name], new_m[name], new_v[name] = [a[None] for a in _adamw(
            "adamw_" + name, weights[name][0], grads[name], mom1[name][0], mom2[name][0], block_rows)]
        grads[name] = grads[name][None]
    dmod_sh = lax.dynamic_slice(dmod_all, (0, shard * ada_cols), (N_DEV, ada_cols))
    pad8 = ((0, 16 - N_DEV), (0, 0))
    grads["w_ada"], delta["w_ada"], new_m["w_ada"], new_v["w_ada"] = [a[None] for a in _adamw_ada(
        jnp.pad(c_all, pad8), jnp.pad(dmod_sh, pad8), w_ada[0], m_w_ada[0], v_w_ada[0], 256)]
    small_names = SMALL_REPLICATED + SMALL_SHARDED
    packed = [_pack([d[k] for k in small_names], ADAM_PACK_ROWS) for d in (weights, grads, mom1, mom2)]
    small_out = _adamw("adamw_small", *packed, ADAM_PACK_ROWS)
    for d, arr in zip((delta, new_m, new_v), small_out):
        d.update(zip(small_names, _unpack(arr, [weights[k].shape for k in small_names])))

    return (loss_sum.reshape(()), grad_x[None], *[grads[k] for k in WEIGHT_ORDER], *[delta[k] for k in WEIGHT_ORDER],
            *[new_m[k] for k in WEIGHT_ORDER], *[new_v[k] for k in WEIGHT_ORDER])
```

```python
import functools

import jax
import jax.numpy as jnp
from jax import lax
from jax.experimental import pallas as pl
from jax.experimental.pallas import tpu as pltpu

F32 = jnp.float32
BF16 = jnp.bfloat16

D_MODEL = 1024
D_HALF = 512
D_FF = 2816
CONV_K = 31
FFN_K = 3
CHUNK = 128
N_HEADS = 8
HEAD_DIM = 64
N_SHARD = 4
N_DEV = 8
RMS_EPS = 1e-6
LN_EPS = 1e-5
ADAM_LR, ADAM_B1, ADAM_B2, ADAM_EPS, ADAM_WD, ADAM_STEP = 0.001, 0.9, 0.999, 1e-08, 0.01, 10

TILE = 256
HALO = 32
FFN_HALO = 16
FFN_BLK = 256
UP_SHARD = 2 * D_FF // N_SHARD
VMEM_LIMIT_BYTES = 56 * 1024 * 1024

ANY = pl.BlockSpec(memory_space=pl.ANY)
NT_DIMS = (((1,), (1,)), ((), ()))
TN_DIMS = (((0,), (0,)), ((), ()))


def _full(shape):
    return pl.BlockSpec(shape, lambda i: (0,) * len(shape))


def _nn(a, b):
    return jnp.dot(a, b, preferred_element_type=F32)


def _nt(a, b):
    return lax.dot_general(a, b, NT_DIMS, preferred_element_type=F32)


def _tn(a, b):
    return lax.dot_general(a, b, TN_DIMS, preferred_element_type=F32)


def _colsum(a):
    return jnp.sum(a, axis=0, keepdims=True)


def _params(semantics=("arbitrary",)):
    return pltpu.CompilerParams(dimension_semantics=semantics, vmem_limit_bytes=VMEM_LIMIT_BYTES)


def _rms(v, gain):
    return v * lax.rsqrt(jnp.mean(v * v, axis=-1, keepdims=True) + RMS_EPS) * gain


def _layer_norm(v, gain, bias):
    mu = jnp.mean(v, axis=-1, keepdims=True)
    var = jnp.mean(jnp.square(v - mu), axis=-1, keepdims=True)
    return (v - mu) * lax.rsqrt(var + LN_EPS) * gain + bias


def _mod_norm(v, gain, scale, shift):
    return _rms(v, gain) * (1.0 + scale) + shift


def _conv_branch(a1, ln_g, ln_b, out_gain):
    a2 = _layer_norm(a1, ln_g, ln_b)
    return _rms(a2 * jax.nn.sigmoid(a2), out_gain)


def _gate_branch(gu, sp, out_gain):
    return _rms(jax.nn.gelu(gu) * sp, out_gain)


def _gv_norm(gv, ln_g, ln_b):
    return _layer_norm(jax.nn.gelu(gv), ln_g, ln_b)


def _head_pair_matmul(wp_ref, v):
    lane = lax.broadcasted_iota(jnp.int32, (CHUNK, CHUNK), 1)
    rows = []
    for n in range(v.shape[0] // CHUNK):
        cols = []
        for j in range(N_HEADS // 2):
            r = _nn(wp_ref[j], v[n * CHUNK:(n + 1) * CHUNK, j * CHUNK:(j + 1) * CHUNK])
            cols.append(jnp.where(lane < HEAD_DIM, r[:CHUNK], r[CHUNK:]))
        rows.append(jnp.concatenate(cols, axis=1))
    return jnp.concatenate(rows, axis=0)


def _tile_bias(bs, tokens):
    return jnp.concatenate([bs] * (tokens // CHUNK), axis=0)


FORWARD_LEAD = 8


def _fwd_mixer(x, vec, conv_w, wpair, bs_full, w_in_g, w_out_g, late_parts):
    seq = x.shape[0]
    n_tiles = seq // TILE
    t = TILE
    n_late = len(late_parts)
    forward_step = max(n_tiles - FORWARD_LEAD, 0)
    names = ["norm1_gain", "sc1", "sh1", "gt1", "conv_dw_b", "conv_ln_g", "conv_ln_b", "gm_ln_g", "gm_ln_b",
             "mix_out_gain"]
    vecs = [vec[k] for k in names]

    def body(x_ref, g1, sc1, sh1, gt1, cb, clg, clb, vg, vb, mg, cw, wp, bs, win_hbm, wout_hbm, *rest):
        late = rest[n_late:2 * n_late]
        z_ref, a1_ref, sp_ref, y_ref, o1_ref, x2_ref = rest[2 * n_late:2 * n_late + 6]
        win_v, wout_v, halo, bank, sem, send_sems, recv_sems = rest[2 * n_late + 6:]
        i = pl.program_id(0)
        mx, my, mc = _coords()
        shard = 2 * mx + my

        def half(w, which):
            h = late[w].shape[1] // 2
            return pl.ds(pl.multiple_of(which * h, 16), h)

        def chip_of(j):
            return 2 * _flip(mx, CHIP_FLIPS[j][0]) + _flip(my, CHIP_FLIPS[j][1])

        def ici_copy(w, j, slot):
            rows = late[w].at[slot, half(w, mc)]
            return pltpu.make_async_remote_copy(
                src_ref=rows, dst_ref=rows, send_sem=send_sems.at[w, j], recv_sem=recv_sems.at[w, j],
                device_id=(_flip(mx, CHIP_FLIPS[j][0]), _flip(my, CHIP_FLIPS[j][1]), mc), device_id_type=MESH)

        def d2d_copy(w, j, which):
            rows = late[w].at[chip_of(j), half(w, which)]
            return pltpu.make_async_remote_copy(
                src_ref=rows, dst_ref=rows, send_sem=send_sems.at[w, len(CHIP_FLIPS) + j],
                recv_sem=recv_sems.at[w, len(CHIP_FLIPS) + j], device_id=(mx, my, 1 - mc), device_id_type=MESH)

        pairs = [(w, j) for w in range(n_late) for j in range(len(CHIP_FLIPS))]

        @pl.when(i == 0)
        def _():
            for w, j in pairs:
                ici_copy(w, j, shard).start()
            cps = [pltpu.make_async_copy(win_hbm, win_v, sem.at[0]),
                   pltpu.make_async_copy(wout_hbm, wout_v, sem.at[1])]
            for cp in cps:
                cp.start()
            for cp in cps:
                cp.wait()
            halo[...] = jnp.zeros_like(halo)

        @pl.when(i == forward_step)
        def _():
            for w, j in pairs:
                ici_copy(w, j, chip_of(j)).wait_recv()
                d2d_copy(w, j, mc).start()

        xv = x_ref[...]
        h1b = _mod_norm(xv, g1[...], sc1[...], sh1[...]).astype(BF16)
        zs = [_nn(h1b, win_v[k]) for k in range(N_SHARD)]
        for k in range(N_SHARD):
            z_ref[:, k * D_HALF:(k + 1) * D_HALF] = zs[k]
        ca, cg, gu, gv = zs
        a0 = ca * jax.nn.sigmoid(cg)
        ext = jnp.concatenate([halo[...], a0], axis=0)
        halo[...] = a0[t - HALO:]
        bank[0] = ext
        for b in range(1, 8):
            bank[b] = pltpu.roll(ext, b, axis=0)
        a1 = jnp.zeros((t, D_HALF), F32) + cb[...]
        for s in range(CONV_K):
            q, b = divmod(s, 8)
            a1 = a1 + bank[b, pl.ds(HALO - 8 * q, t), :] * cw[pl.ds(CONV_K - 1 - s, 1), :]
        a1_ref[...] = a1
        mgv = mg[...]
        ya = _conv_branch(a1, clg[...], clb[...], mgv[:, :D_HALF])
        gvn = _gv_norm(gv, vg[...], vb[...]).astype(BF16)
        sp = _head_pair_matmul(wp, gvn) + _tile_bias(bs[...], t)
        sp_ref[...] = sp
        yg = _gate_branch(gu, sp, mgv[:, D_HALF:])
        yb = jnp.concatenate([ya, yg], axis=1).astype(BF16)
        y_ref[...] = yb
        o1 = _nn(yb, wout_v[...])
        o1_ref[...] = o1
        x2_ref[...] = xv + gt1[...] * o1

        @pl.when(i == n_tiles - 1)
        def _():
            for w, j in pairs:
                d2d_copy(w, j, 1 - mc).wait_recv()
            for w, j in pairs:
                ici_copy(w, j, shard).wait_send()
                d2d_copy(w, j, mc).wait_send()

    def row(width):
        return pl.BlockSpec((t, width), lambda i: (i, 0))

    out_shape = [jax.ShapeDtypeStruct((seq, 4 * D_HALF), F32), jax.ShapeDtypeStruct((seq, D_HALF), F32),
                 jax.ShapeDtypeStruct((seq, D_HALF), F32), jax.ShapeDtypeStruct((seq, D_MODEL), BF16),
                 jax.ShapeDtypeStruct((seq, D_MODEL), F32), jax.ShapeDtypeStruct((seq, D_MODEL), F32)]
    n_in = 1 + len(vecs) + 3 + 2
    sem_shape = (n_late, 2 * len(CHIP_FLIPS))
    outs = pl.pallas_call(
        body, grid=(n_tiles,), name="fwd_mixer",
        in_specs=[row(D_MODEL)] + [_full(v.shape) for v in vecs]
        + [_full(conv_w.shape), _full(wpair.shape), _full(bs_full.shape), ANY, ANY] + [ANY] * n_late,
        out_specs=[ANY] * n_late + [row(4 * D_HALF), row(D_HALF), row(D_HALF), row(D_MODEL), row(D_MODEL),
                                    row(D_MODEL)],
        out_shape=[jax.ShapeDtypeStruct(a.shape, a.dtype) for a in late_parts] + out_shape,
        input_output_aliases={n_in + w: w for w in range(n_late)},
        scratch_shapes=[pltpu.VMEM(w_in_g.shape, BF16), pltpu.VMEM(w_out_g.shape, BF16),
                        pltpu.VMEM((HALO, D_HALF), F32), pltpu.VMEM((8, t + HALO, D_HALF), F32),
                        pltpu.SemaphoreType.DMA((2,)), pltpu.SemaphoreType.DMA(sem_shape),
                        pltpu.SemaphoreType.DMA(sem_shape)],
        compiler_params=_params(),
    )(x, *vecs, conv_w, wpair, bs_full, w_in_g, w_out_g, *late_parts)
    return outs[n_late:], outs[:n_late]


def _fwd_up(x2, norm2_gain, sc2, sh2, w_up_g, to_inter, u_dtype):
    seq = x2.shape[0]
    n_tiles = seq // TILE
    t = TILE

    def body(x2_ref, g2, sc2_ref, sh2_ref, pm_ref, wup_hbm, u_ref, wup_v, sem):
        @pl.when(pl.program_id(0) == 0)
        def _():
            cp = pltpu.make_async_copy(wup_hbm, wup_v, sem.at[0])
            cp.start()
            cp.wait()

        h2b = _mod_norm(x2_ref[...], g2[...], sc2_ref[...], sh2_ref[...]).astype(BF16)
        h2b = _nn(pm_ref[...], h2b).astype(BF16)
        for k in range(N_SHARD):
            u_ref[:, k * UP_SHARD:(k + 1) * UP_SHARD] = _nn(h2b, wup_v[k]).astype(u_ref.dtype)

    return pl.pallas_call(
        body, grid=(n_tiles,), name="fwd_up",
        in_specs=[pl.BlockSpec((t, D_MODEL), lambda i: (i, 0)), _full((1, D_MODEL)), _full((1, D_MODEL)),
                  _full((1, D_MODEL)), _full(to_inter.shape), ANY],
        out_specs=pl.BlockSpec((t, 2 * D_FF), lambda i: (i, 0)),
        out_shape=jax.ShapeDtypeStruct((seq, 2 * D_FF), u_dtype),
        scratch_shapes=[pltpu.VMEM(w_up_g.shape, BF16), pltpu.SemaphoreType.DMA((1,))],
        compiler_params=_params(),
    )(x2, norm2_gain, sc2, sh2, to_inter, w_up_g)


def _interleave_matrices():
    row = jnp.arange(TILE)
    token_of_row = (row % 8) * (TILE // 8) + row // 8
    to_inter = (token_of_row[:, None] == row[None, :]).astype(BF16)
    return to_inter, jnp.transpose(to_inter)


def _ffn_tail(u, x2, target, ffn_w, ffn_b, gt2, final_gain, w_down_g, to_inter, to_natural):
    seq = x2.shape[0]
    n_tiles = seq // TILE
    t = TILE
    n_blk = D_FF // FFN_BLK
    inv_d = 1.0 / D_MODEL

    def final_norm(x3, gain):
        return _rms(x3, gain)

    def body(u_ref, uh_ref, x2_ref, tgt_ref, fw, fb, gt2_ref, fg, pm_ref, pmt_ref, wd_hbm,
             du_ref, dx3_ref, dfw_ref, dfb_ref, dfg_ref, dgt2_ref, loss_ref, dwd_hbm, dwd16_hbm,
             wd_v, dwd_acc, carry, sil_s, vds_s, f_s, du_s, sem):
        i = pl.program_id(0)
        tile = n_tiles - 1 - i
        sublane = lax.broadcasted_iota(jnp.int32, (8, FFN_BLK), 0)

        @pl.when(i == 0)
        def _():
            cp = pltpu.make_async_copy(wd_hbm, wd_v, sem.at[0])
            cp.start()
            cp.wait()
            dwd_acc[...] = jnp.zeros_like(dwd_acc)
            carry[...] = jnp.zeros_like(carry)
            dfw_ref[...] = jnp.zeros_like(dfw_ref)
            dfb_ref[...] = jnp.zeros_like(dfb_ref)
            dfg_ref[...] = jnp.zeros_like(dfg_ref)
            dgt2_ref[...] = jnp.zeros_like(dgt2_ref)
            loss_ref[...] = jnp.zeros_like(loss_ref)

        def cols_of(j):
            return pl.ds(j * FFN_BLK, FFN_BLK), pl.ds(D_FF + j * FFN_BLK, FFN_BLK)

        def wrap_down(last, before):
            return jnp.where(sublane == 0, pltpu.roll(before, 1, axis=0), pltpu.roll(last, 1, axis=0))

        def wrap_up(first, after):
            return jnp.where(sublane == 7, pltpu.roll(after, 7, axis=0), pltpu.roll(first, 7, axis=0))

        def conv(cols):
            cur = u_ref[:, cols]
            prev = jnp.where(tile > 0, uh_ref[:, cols], 0.0)
            w1 = wrap_down(cur[t - 8:], prev[8:])
            w2 = wrap_down(cur[t - 16:t - 8], prev[:8])
            back1 = jnp.concatenate([w1, cur[:t - 8]], axis=0)
            back2 = jnp.concatenate([w2, w1, cur[:t - 16]], axis=0)
            return (fb[:, cols] + cur * fw[pl.ds(2, 1), cols] + back1 * fw[pl.ds(1, 1), cols]
                    + back2 * fw[pl.ds(0, 1), cols])

        pm_t = pmt_ref[...]
        o2 = jnp.zeros((t, D_MODEL), F32)
        for j in range(n_blk):
            cv, cg = cols_of(j)
            val, gate = conv(cv), conv(cg)
            sig = jax.nn.sigmoid(gate)
            sil = gate * sig
            fb16 = (sil * val).astype(BF16)
            sil_s[:, cv] = sil
            vds_s[:, cv] = val * (sig + sil * (1.0 - sig))
            f_s[:, cv] = fb16
            o2 = o2 + _nn(fb16, wd_v[pl.ds(j * FFN_BLK, FFN_BLK), :])
        hi = o2.astype(BF16)
        rest = o2 - hi.astype(F32)
        mid = rest.astype(BF16)
        low = (rest - mid.astype(F32)).astype(BF16)
        o2 = _nn(jnp.concatenate([pm_t, pm_t, pm_t], axis=1), jnp.concatenate([hi, mid, low], axis=0))

        gt2v = gt2_ref[...]
        x3 = x2_ref[...] + gt2v * o2
        out, out_vjp = jax.vjp(final_norm, x3, fg[...])
        diff = out - tgt_ref[...]
        loss_ref[...] += jnp.zeros_like(loss_ref) + 0.5 * inv_d * jnp.sum(diff * diff)
        dx3, dfg = out_vjp(diff * inv_d)
        dfg_ref[...] += dfg
        dgt2_ref[...] += _colsum(dx3 * o2)
        dx3_ref[...] = dx3
        do2b = _nn(pm_ref[...], (gt2v * dx3).astype(BF16)).astype(BF16)

        for j in range(n_blk):
            cv, cg = cols_of(j)
            rows = pl.ds(j * FFN_BLK, FFN_BLK)
            df = _nt(do2b, wd_v[rows, :])
            dwd_acc[rows, :] += _tn(f_s[:, cv], do2b)
            for dd, cols in ((df * sil_s[:, cv], cv), (df * vds_s[:, cv], cg)):
                dfb_ref[:, cols] += _colsum(dd)
                nxt = carry[:, cols]
                w1 = wrap_up(dd[:8], nxt[:8])
                w2 = wrap_up(dd[8:16], nxt[8:])
                ahead = (dd, jnp.concatenate([dd[8:], w1], axis=0), jnp.concatenate([dd[16:], w1, w2], axis=0))
                carry[:, cols] = dd[:16]
                uv = u_ref[:, cols]
                du = jnp.zeros((t, FFN_BLK), F32)
                for s in range(FFN_K):
                    du = du + ahead[s] * fw[pl.ds(FFN_K - 1 - s, 1), cols]
                    dfw_ref[pl.ds(FFN_K - 1 - s, 1), cols] += _colsum(ahead[s] * uv)
                du_s[:, cols] = du.astype(BF16)
        du_ref[...] = _nn(pm_t, du_s[...]).astype(BF16)

        @pl.when(i == n_tiles - 1)
        def _():
            cp = pltpu.make_async_copy(dwd_acc, dwd_hbm, sem.at[1])
            cp.start()
            wd_v[...] = dwd_acc[...].astype(BF16)
            cp16 = pltpu.make_async_copy(wd_v, dwd16_hbm, sem.at[2])
            cp16.start()
            cp.wait()
            cp16.wait()

    def rev(width):
        return pl.BlockSpec((t, width), lambda i: (n_tiles - 1 - i, 0))

    assert FFN_K == 3 and u.dtype == F32
    halo_spec = pl.BlockSpec(
        (FFN_HALO, 2 * D_FF), lambda i: (jnp.maximum((n_tiles - 1 - i) * (t // FFN_HALO) - 1, 0), 0))
    out_shape = [jax.ShapeDtypeStruct((seq, 2 * D_FF), BF16), jax.ShapeDtypeStruct((seq, D_MODEL), F32),
                 jax.ShapeDtypeStruct((FFN_K, 2 * D_FF), F32), jax.ShapeDtypeStruct((1, 2 * D_FF), F32),
                 jax.ShapeDtypeStruct((1, D_MODEL), F32), jax.ShapeDtypeStruct((1, D_MODEL), F32),
                 jax.ShapeDtypeStruct((1, 128), F32), jax.ShapeDtypeStruct((D_FF, D_MODEL), F32),
                 jax.ShapeDtypeStruct((D_FF, D_MODEL), BF16)]
    return pl.pallas_call(
        body, grid=(n_tiles,), name="ffn_tail",
        in_specs=[rev(2 * D_FF), halo_spec, rev(D_MODEL), rev(D_MODEL), _full(ffn_w.shape), _full(ffn_b.shape),
                  _full(gt2.shape), _full(final_gain.shape), _full(to_inter.shape), _full(to_natural.shape), ANY],
        out_specs=[rev(2 * D_FF), rev(D_MODEL), _full((FFN_K, 2 * D_FF)), _full((1, 2 * D_FF)), _full((1, D_MODEL)),
                   _full((1, D_MODEL)), _full((1, 128)), ANY, ANY],
        out_shape=out_shape,
        scratch_shapes=[pltpu.VMEM((D_FF, D_MODEL), BF16), pltpu.VMEM((D_FF, D_MODEL), F32),
                        pltpu.VMEM((FFN_HALO, 2 * D_FF), F32),
                        pltpu.VMEM((t, D_FF), F32), pltpu.VMEM((t, D_FF), F32),
                        pltpu.VMEM((t, D_FF), BF16), pltpu.VMEM((t, 2 * D_FF), BF16), pltpu.SemaphoreType.DMA((3,))],
        compiler_params=_params(),
    )(u, u, x2, target, ffn_w, ffn_b, gt2, final_gain, to_inter, to_natural, w_down_g)


def _scatter_copies(src16, land, send_sems, recv_sems):
    x, y, c = _coords()
    h = src16.shape[1] // 2
    copies = []
    for f, flip in enumerate(PEER_FLIPS):
        tx, ty, tc = _flip(x, flip[0]), _flip(y, flip[1]), _flip(c, flip[2])
        copies.append(pltpu.make_async_remote_copy(
            src_ref=src16.at[2 * tx + ty, pl.ds(pl.multiple_of(tc * h, 16), h)], dst_ref=land.at[f],
            send_sem=send_sems.at[f], recv_sem=recv_sems.at[f], device_id=(tx, ty, tc), device_id_type=MESH))
    return copies


def _land_shape(src16):
    return jax.ShapeDtypeStruct((len(PEER_FLIPS), src16.shape[1] // 2, src16.shape[2]), BF16)


def _bwd_up(du, x2, dx3, norm2_gain, sc2, sh2, w_up_g, dwd16):
    seq = x2.shape[0]
    n_tiles = seq // TILE
    t = TILE

    def body(du_ref, x2_ref, dx3_ref, g2, sc2_ref, sh2_ref, wup_hbm, dwd16_hbm,
             dx2_ref, dg2_ref, dsc2_ref, dsh2_ref, dwup_hbm, dwup16_hbm, land_hbm,
             wup_v, dwup_acc, sem, send_sems, recv_sems):
        i = pl.program_id(0)

        @pl.when(i == 0)
        def _():
            for cp in _scatter_copies(dwd16_hbm, land_hbm, send_sems, recv_sems):
                cp.start()
            cp = pltpu.make_async_copy(wup_hbm, wup_v, sem.at[0])
            cp.start()
            cp.wait()
            dwup_acc[...] = jnp.zeros_like(dwup_acc)
            dg2_ref[...] = jnp.zeros_like(dg2_ref)
            dsc2_ref[...] = jnp.zeros_like(dsc2_ref)
            dsh2_ref[...] = jnp.zeros_like(dsh2_ref)

        h2, h2_vjp = jax.vjp(_mod_norm, x2_ref[...], g2[...], sc2_ref[...], sh2_ref[...])
        h2b = h2.astype(BF16)
        dh2 = jnp.zeros((t, D_MODEL), F32)
        for k in range(N_SHARD):
            dub = du_ref[:, k * UP_SHARD:(k + 1) * UP_SHARD]
            dh2 = dh2 + _nt(dub, wup_v[k])
            dwup_acc[k] += _tn(h2b, dub)
        dx2, dg2, dsc2, dsh2 = h2_vjp(dh2)
        dx2_ref[...] = dx3_ref[...] + dx2
        dg2_ref[...] += dg2
        dsc2_ref[...] += dsc2
        dsh2_ref[...] += dsh2

        @pl.when(i == n_tiles - 1)
        def _():
            cp = pltpu.make_async_copy(dwup_acc, dwup_hbm, sem.at[1])
            cp.start()
            for k in range(N_SHARD):
                wup_v[k] = dwup_acc[k].astype(BF16)
            cp16 = pltpu.make_async_copy(wup_v, dwup16_hbm, sem.at[2])
            cp16.start()
            cp.wait()
            cp16.wait()
            for rc in _scatter_copies(dwd16_hbm, land_hbm, send_sems, recv_sems):
                rc.wait()

    def row(width):
        return pl.BlockSpec((t, width), lambda i: (i, 0))

    vec = jax.ShapeDtypeStruct((1, D_MODEL), F32)
    n_peer = len(PEER_FLIPS)
    return pl.pallas_call(
        body, grid=(n_tiles,), name="bwd_up",
        in_specs=[row(2 * D_FF), row(D_MODEL), row(D_MODEL), _full((1, D_MODEL)), _full((1, D_MODEL)),
                  _full((1, D_MODEL)), ANY, ANY],
        out_specs=[row(D_MODEL), _full((1, D_MODEL)), _full((1, D_MODEL)), _full((1, D_MODEL)), ANY, ANY, ANY],
        out_shape=[jax.ShapeDtypeStruct((seq, D_MODEL), F32), vec, vec, vec,
                   jax.ShapeDtypeStruct(w_up_g.shape, F32), jax.ShapeDtypeStruct(w_up_g.shape, BF16),
                   _land_shape(dwd16)],
        scratch_shapes=[pltpu.VMEM(w_up_g.shape, BF16), pltpu.VMEM(w_up_g.shape, F32), pltpu.SemaphoreType.DMA((3,)),
                        pltpu.SemaphoreType.DMA((n_peer,)), pltpu.SemaphoreType.DMA((n_peer,))],
        compiler_params=_params(),
    )(du, x2, dx3, norm2_gain, sc2, sh2, w_up_g, dwd16)


def _bwd_mixer(dx2, x, z, a1, sp, yb, o1, vec, conv_w, wpair, wpair_t, causal_mask, w_in_g, w_out_g, dwup16):
    seq = x.shape[0]
    n_tiles = seq // TILE
    t = TILE
    names = ["norm1_gain", "sc1", "sh1", "gt1", "conv_ln_g", "conv_ln_b", "gm_ln_g", "gm_ln_b", "mix_out_gain"]
    vecs = [vec[k] for k in names]

    def body(dx2_ref, x_ref, z_ref, a1_ref, sp_ref, y_ref, o1_ref, g1, sc1, sh1, gt1, clg, clb, vg, vb, mg,
             cw, wp, wpt, mask_ref, win_hbm, wout_hbm, dwup16_hbm,
             gx_ref, dg1_ref, dsc1_ref, dsh1_ref, dgt1_ref, dcw_ref, dcb_ref, dclg_ref, dclb_ref, dvg_ref, dvb_ref,
             dmg_ref, dws_ref, dbs_ref, dwin_hbm, dwout_hbm, land_hbm, dwin16_hbm, dwout16_hbm,
             win_v, wout_v, dwin_acc, dwout_acc, carry, bank, dbs_acc, sem, send_sems, recv_sems):
        i = pl.program_id(0)
        small = [dg1_ref, dsc1_ref, dsh1_ref, dgt1_ref, dcw_ref, dcb_ref, dclg_ref, dclb_ref, dvg_ref, dvb_ref,
                 dmg_ref, dws_ref, dbs_acc]

        @pl.when(i == 0)
        def _():
            for cp in _scatter_copies(dwup16_hbm, land_hbm, send_sems, recv_sems):
                cp.start()
            cps = [pltpu.make_async_copy(win_hbm, win_v, sem.at[0]),
                   pltpu.make_async_copy(wout_hbm, wout_v, sem.at[1])]
            for cp in cps:
                cp.start()
            for cp in cps:
                cp.wait()
            dwin_acc[...] = jnp.zeros_like(dwin_acc)
            dwout_acc[...] = jnp.zeros_like(dwout_acc)
            carry[...] = jnp.zeros_like(carry)
            for ref in small:
                ref[...] = jnp.zeros_like(ref)

        dx2v = dx2_ref[...]
        gt1v = gt1[...]
        dgt1_ref[...] += _colsum(dx2v * o1_ref[...])
        do1b = (gt1v * dx2v).astype(BF16)
        dy = _nt(do1b, wout_v[...])
        dwout_acc[...] += _tn(y_ref[...], do1b)

        mgv = mg[...]
        _, conv_vjp = jax.vjp(_conv_branch, a1_ref[...], clg[...], clb[...], mgv[:, :D_HALF])
        da1, dclg, dclb, dmg_a = conv_vjp(dy[:, :D_HALF])
        dclg_ref[...] += dclg
        dclb_ref[...] += dclb
        gu = z_ref[:, 2 * D_HALF:3 * D_HALF]
        gv = z_ref[:, 3 * D_HALF:]
        spv = sp_ref[...]
        _, gate_vjp = jax.vjp(_gate_branch, gu, spv, mgv[:, D_HALF:])
        dgu, dsp, dmg_g = gate_vjp(dy[:, D_HALF:])
        dmg_ref[...] += jnp.concatenate([dmg_a, dmg_g], axis=1)
        gvn, gv_vjp = jax.vjp(_gv_norm, gv, vg[...], vb[...])
        gvnb = gvn.astype(BF16)
        dspb = dsp.astype(BF16)
        dgvn = _head_pair_matmul(wpt, dspb)
        dgv, dvg, dvb = gv_vjp(dgvn)
        dvg_ref[...] += dvg
        dvb_ref[...] += dvb
        lane = lax.broadcasted_iota(jnp.int32, (CHUNK, CHUNK), 1)
        dbs = jnp.zeros((CHUNK, D_HALF), F32)
        for n in range(t // CHUNK):
            rows = slice(n * CHUNK, (n + 1) * CHUNK)
            dbs = dbs + dsp[rows, :]
            for j in range(N_HEADS // 2):
                cols = slice(j * CHUNK, (j + 1) * CHUNK)
                blk = dspb[rows, cols]
                zero = jnp.zeros_like(blk)
                vblk = gvnb[rows, cols]
                dws_ref[2 * j] += _nt(jnp.where(lane < HEAD_DIM, blk, zero), vblk)
                dws_ref[2 * j + 1] += _nt(jnp.where(lane < HEAD_DIM, zero, blk), vblk)
        dbs_acc[...] += dbs

        ca = z_ref[:, :D_HALF]
        cg = z_ref[:, D_HALF:2 * D_HALF]
        sig = jax.nn.sigmoid(cg)
        a0 = ca * sig
        ext = jnp.concatenate([da1, carry[...]], axis=0)
        carry[...] = da1[:HALO]
        bank[0] = ext
        for b in range(1, 8):
            bank[b] = pltpu.roll(ext, t + HALO - b, axis=0)
        dcb_ref[...] += _colsum(da1)
        da0 = jnp.zeros((t, D_HALF), F32)
        for s in range(CONV_K):
            q, b = divmod(s, 8)
            shifted = bank[b, pl.ds(8 * q, t), :]
            da0 = da0 + shifted * cw[pl.ds(CONV_K - 1 - s, 1), :]
            dcw_ref[pl.ds(CONV_K - 1 - s, 1), :] += _colsum(shifted * a0)
        dca = da0 * sig
        dcg = da0 * ca * sig * (1.0 - sig)

        h1, h1_vjp = jax.vjp(_mod_norm, x_ref[...], g1[...], sc1[...], sh1[...])
        h1b = h1.astype(BF16)
        dh1 = jnp.zeros((t, D_MODEL), F32)
        for k, dzk in enumerate((dca, dcg, dgu, dgv)):
            dzb = dzk.astype(BF16)
            dh1 = dh1 + _nt(dzb, win_v[k])
            dwin_acc[k] += _tn(h1b, dzb)
        dx, dg1, dsc1, dsh1 = h1_vjp(dh1)
        gx_ref[...] = dx2v + dx
        dg1_ref[...] += dg1
        dsc1_ref[...] += dsc1
        dsh1_ref[...] += dsh1

        @pl.when(i == n_tiles - 1)
        def _():
            for h in range(N_HEADS):
                dws_ref[h] = dws_ref[h] * mask_ref[...]
            head_of_lane = lax.broadcasted_iota(jnp.int32, (N_HEADS, D_HALF), 1) // HEAD_DIM
            pick = (head_of_lane == lax.broadcasted_iota(jnp.int32, (N_HEADS, D_HALF), 0)).astype(F32)
            dbs_ref[...] = lax.dot_general(pick, dbs_acc[...], NT_DIMS, precision=lax.Precision.HIGHEST,
                                           preferred_element_type=F32)
            cps = [pltpu.make_async_copy(dwin_acc, dwin_hbm, sem.at[2]),
                   pltpu.make_async_copy(dwout_acc, dwout_hbm, sem.at[3])]
            for cp in cps:
                cp.start()
            for k in range(N_SHARD):
                win_v[k] = dwin_acc[k].astype(BF16)
            wout_v[...] = dwout_acc[...].astype(BF16)
            cps += [pltpu.make_async_copy(win_v, dwin16_hbm, sem.at[4]),
                    pltpu.make_async_copy(wout_v, dwout16_hbm, sem.at[5])]
            for cp in cps[2:]:
                cp.start()
            for cp in cps:
                cp.wait()
            for rc in _scatter_copies(dwup16_hbm, land_hbm, send_sems, recv_sems):
                rc.wait()

    def rev(width):
        return pl.BlockSpec((t, width), lambda i: (n_tiles - 1 - i, 0))

    v1024 = jax.ShapeDtypeStruct((1, D_MODEL), F32)
    v512 = jax.ShapeDtypeStruct((1, D_HALF), F32)
    small_shapes = [v1024, v1024, v1024, v1024, jax.ShapeDtypeStruct((CONV_K, D_HALF), F32), v512, v512, v512, v512,
                    v512, v1024, jax.ShapeDtypeStruct((N_HEADS, CHUNK, CHUNK), F32),
                    jax.ShapeDtypeStruct((N_HEADS, CHUNK), F32)]
    n_peer = len(PEER_FLIPS)
    return pl.pallas_call(
        body, grid=(n_tiles,), name="bwd_mixer",
        in_specs=[rev(D_MODEL), rev(D_MODEL), rev(4 * D_HALF), rev(D_HALF), rev(D_HALF), rev(D_MODEL),
                  rev(D_MODEL)] + [_full(v.shape) for v in vecs]
        + [_full(conv_w.shape), _full(wpair.shape), _full(wpair_t.shape), _full(causal_mask.shape), ANY, ANY, ANY],
        out_specs=[rev(D_MODEL)] + [_full(s.shape) for s in small_shapes] + [ANY] * 5,
        out_shape=[jax.ShapeDtypeStruct((seq, D_MODEL), F32)] + small_shapes
        + [jax.ShapeDtypeStruct(w_in_g.shape, F32), jax.ShapeDtypeStruct(w_out_g.shape, F32), _land_shape(dwup16),
           jax.ShapeDtypeStruct(w_in_g.shape, BF16), jax.ShapeDtypeStruct(w_out_g.shape, BF16)],
        scratch_shapes=[pltpu.VMEM(w_in_g.shape, BF16), pltpu.VMEM(w_out_g.shape, BF16),
                        pltpu.VMEM(w_in_g.shape, F32), pltpu.VMEM(w_out_g.shape, F32),
                        pltpu.VMEM((HALO, D_HALF), F32), pltpu.VMEM((8, t + HALO, D_HALF), F32),
                        pltpu.VMEM((CHUNK, D_HALF), F32), pltpu.SemaphoreType.DMA((6,)),
                        pltpu.SemaphoreType.DMA((n_peer,)), pltpu.SemaphoreType.DMA((n_peer,))],
        compiler_params=_params(),
    )(dx2, x, z, a1, sp, yb, o1, *vecs, conv_w, wpair, wpair_t, causal_mask, w_in_g, w_out_g, dwup16)


def _gmlp_operands(gm_ws, gm_bs):
    mask = jnp.tril(jnp.ones((CHUNK, CHUNK), F32))
    ws = gm_ws * mask[None]
    wpair = ws.reshape(N_HEADS // 2, 2 * CHUNK, CHUNK).astype(BF16)
    wpair_t = jnp.swapaxes(ws, 1, 2).reshape(N_HEADS // 2, 2 * CHUNK, CHUNK).astype(BF16)
    bs_full = jnp.repeat(jnp.transpose(gm_bs), HEAD_DIM, axis=1)
    return wpair, wpair_t, bs_full, mask


def _local_step(x, target, mod, p, w_in_g, w_out_g, w_up_part, w_down_part, u_dtype=F32):
    sh1, sc1, gt1, sh2, sc2, gt2 = [mod[:, k * D_MODEL:(k + 1) * D_MODEL] for k in range(6)]
    vec = dict(p, sh1=sh1, sc1=sc1, gt1=gt1, sh2=sh2, sc2=sc2, gt2=gt2)
    wpair, wpair_t, bs_full, mask = _gmlp_operands(p["gm_ws"], p["gm_bs"])

    (z, a1, sp, yb, o1, x2), (w_up_g, w_down_g) = _fwd_mixer(
        x, vec, p["conv_dw_w"], wpair, bs_full, w_in_g, w_out_g, [w_up_part, w_down_part])
    w_down_g = w_down_g.reshape(D_FF, D_MODEL)
    to_inter, to_natural = _interleave_matrices()
    u = _fwd_up(x2, p["norm2_gain"], sc2, sh2, w_up_g, to_inter, u_dtype)
    du, dx3, d_ffn_w, d_ffn_b, d_fg, d_gt2, loss, d_wd, d_wd16 = _ffn_tail(
        u, x2, target, p["ffn_dw_w"], p["ffn_dw_b"], gt2, p["final_gain"], w_down_g, to_inter, to_natural)
    by_shard = (N_SHARD, -1, D_MODEL)
    dx2, d_g2, d_sc2, d_sh2, d_wup, d_wup16, land_wd = _bwd_up(
        du, x2, dx3, p["norm2_gain"], sc2, sh2, w_up_g, d_wd16.reshape(by_shard))
    (gx, d_g1, d_sc1, d_sh1, d_gt1, d_cw, d_cb, d_clg, d_clb, d_vg, d_vb, d_mg, d_ws, d_bs, d_win, d_wout, land_wup,
     d_win16, d_wout16) = _bwd_mixer(dx2, x, z, a1, sp, yb, o1, vec, p["conv_dw_w"], wpair, wpair_t, mask, w_in_g,
                                     w_out_g, d_wup16)
    d_mod = jnp.concatenate([d_sh1, d_sc1, d_gt1, d_sh2, d_sc2, d_gt2], axis=1)
    grads = dict(norm1_gain=d_g1, conv_dw_w=d_cw, conv_dw_b=d_cb, conv_ln_g=d_clg, conv_ln_b=d_clb, gm_ln_g=d_vg,
                 gm_ln_b=d_vb, gm_ws=d_ws, gm_bs=d_bs, mix_out_gain=d_mg, norm2_gain=d_g2, ffn_dw_w=d_ffn_w,
                 ffn_dw_b=d_ffn_b, final_gain=d_fg, w_in=d_win, w_out=d_wout.reshape(by_shard), w_up=d_wup,
                 w_down=d_wd.reshape(by_shard))
    in_flight = dict(w_in16=d_win16, w_out16=d_wout16.reshape(by_shard), land_w_up=land_wup, land_w_down=land_wd)
    return gx, grads, d_mod, loss, in_flight


MESH = pl.DeviceIdType.MESH
VMEM_SPEC = pl.BlockSpec(memory_space=pltpu.VMEM)
PEER_FLIPS = [(a, b, d) for a in (0, 1) for b in (0, 1) for d in (0, 1)][1:]
CHIP_FLIPS = [(1, 0), (0, 1), (1, 1)]


def _coords():
    return lax.axis_index("x"), lax.axis_index("y"), lax.axis_index("c")


def _flip(v, bit):
    return 1 - v if bit else v


def _rows8(block):
    return pl.ds(pl.multiple_of(8 * block, 8), 8)


def _ada_mod(c_row, w_ada_sh, b_ada_sh):
    cols = w_ada_sh.shape[1]

    def body(c_ref, w_ref, b_ref, call_ref, mod_ref, cpad, modall, send_sems, recv_sems):
        x, y, c = _coords()
        me = 4 * x + 2 * y + c
        cpad[...] = jnp.zeros_like(cpad)
        cpad[pl.ds(0, 1), :] = c_ref[...]

        def gather_copy(j, flip):
            peer = (_flip(x, flip[0]), _flip(y, flip[1]), _flip(c, flip[2]))
            return pltpu.make_async_remote_copy(
                src_ref=cpad, dst_ref=call_ref.at[_rows8(me)], send_sem=send_sems.at[j], recv_sem=recv_sems.at[j],
                device_id=peer, device_id_type=MESH)

        copies = [gather_copy(j, f) for j, f in enumerate(PEER_FLIPS)]
        for cp in copies:
            cp.start()
        call_ref[_rows8(me), :] = cpad[...]
        for cp in copies:
            cp.wait_recv()
        for cp in copies:
            cp.wait_send()
        cv = call_ref[...]
        c_act = (cv * jax.nn.sigmoid(cv)).astype(BF16)
        modall[...] = _nn(c_act, w_ref[...].astype(BF16)) + b_ref[...]

        slot = _rows8(2 * x + y)

        def piece_copy(j, flip):
            tx, ty = _flip(x, flip[0]), _flip(y, flip[1])
            return pltpu.make_async_remote_copy(
                src_ref=modall.at[_rows8(4 * tx + 2 * ty + c)], dst_ref=mod_ref.at[slot],
                send_sem=send_sems.at[len(PEER_FLIPS) + j], recv_sem=recv_sems.at[len(PEER_FLIPS) + j],
                device_id=(tx, ty, c), device_id_type=MESH)

        pieces = [piece_copy(j, f) for j, f in enumerate(CHIP_FLIPS)]
        for cp in pieces:
            cp.start()
        mod_ref[slot, :] = modall[_rows8(me), :]
        for cp in pieces:
            cp.wait_recv()
        for cp in pieces:
            cp.wait_send()

    n_sem = len(PEER_FLIPS) + len(CHIP_FLIPS)
    return pl.pallas_call(
        body, name="ada_mod",
        in_specs=[VMEM_SPEC, VMEM_SPEC, VMEM_SPEC], out_specs=[VMEM_SPEC, VMEM_SPEC],
        out_shape=[jax.ShapeDtypeStruct((8 * N_DEV, D_MODEL), F32), jax.ShapeDtypeStruct((8 * N_SHARD, cols), F32)],
        scratch_shapes=[pltpu.VMEM((8, D_MODEL), F32), pltpu.VMEM((8 * N_DEV, cols), F32),
                        pltpu.SemaphoreType.DMA((n_sem,)), pltpu.SemaphoreType.DMA((n_sem,))],
        compiler_params=pltpu.CompilerParams(vmem_limit_bytes=VMEM_LIMIT_BYTES),
    )(c_row, w_ada_sh, b_ada_sh)


def _gather_weights(shards, filters, n_now):
    n = len(shards)
    nf = len(filters)

    def body(*refs):
        ins, f_ins = refs[:n], refs[n:n + nf]
        outs, f_outs = refs[n + nf:2 * n + nf], refs[2 * n + nf:2 * (n + nf)]
        stage = refs[2 * (n + nf):3 * n + 2 * nf]
        send_sems, recv_sems, local_sems, f_send_sems, f_recv_sems = refs[3 * n + 2 * nf:]
        x, y, c = _coords()
        k = 2 * x + y
        sibling = (x, y, 1 - c)

        def filter_copy(w, j, slot):
            tx, ty = _flip(x, CHIP_FLIPS[j][0]), _flip(y, CHIP_FLIPS[j][1])
            return pltpu.make_async_remote_copy(
                src_ref=f_ins[w], dst_ref=f_outs[w].at[slot], send_sem=f_send_sems.at[w, j],
                recv_sem=f_recv_sems.at[w, j], device_id=(tx, ty, c), device_id_type=MESH)

        def half(w, which):
            h = shards[w].shape[0] // 2
            return pl.ds(pl.multiple_of(which * h, 16), h)

        def ici_copy(w, j, src, slot):
            tx, ty = _flip(x, CHIP_FLIPS[j][0]), _flip(y, CHIP_FLIPS[j][1])
            return pltpu.make_async_remote_copy(
                src_ref=src, dst_ref=outs[w].at[slot, half(w, c)], send_sem=send_sems.at[w, j],
                recv_sem=recv_sems.at[w, j], device_id=(tx, ty, c), device_id_type=MESH)

        def d2d_copy(w, j, slot, which):
            rows = outs[w].at[slot, half(w, which)]
            return pltpu.make_async_remote_copy(
                src_ref=rows, dst_ref=rows, send_sem=send_sems.at[w, len(CHIP_FLIPS) + j],
                recv_sem=recv_sems.at[w, len(CHIP_FLIPS) + j], device_id=sibling, device_id_type=MESH)

        def chip_of(j):
            return 2 * _flip(x, CHIP_FLIPS[j][0]) + _flip(y, CHIP_FLIPS[j][1])

        local, first, passed = [], [], []
        for w in range(nf):
            local.append(pltpu.make_async_copy(f_ins[w], f_outs[w].at[k], local_sems.at[n + w]))
            local[-1].start()
            for j in range(len(CHIP_FLIPS)):
                first.append(filter_copy(w, j, k))
                first[-1].start()
        for w in range(n):
            stage[w][...] = ins[w][...].astype(BF16)
            local.append(pltpu.make_async_copy(stage[w], outs[w].at[k], local_sems.at[w]))
            local[-1].start()
            if w < n_now:
                for j in range(len(CHIP_FLIPS)):
                    first.append(ici_copy(w, j, stage[w].at[half(w, c)], k))
                    first[-1].start()
        for w in range(nf):
            for j in range(len(CHIP_FLIPS)):
                filter_copy(w, j, chip_of(j)).wait_recv()
        for w in range(n_now):
            for j in range(len(CHIP_FLIPS)):
                ici_copy(w, j, stage[w].at[half(w, c)], chip_of(j)).wait_recv()
                passed.append(d2d_copy(w, j, chip_of(j), c))
                passed[-1].start()
        for w in range(n_now):
            for j in range(len(CHIP_FLIPS)):
                d2d_copy(w, j, chip_of(j), 1 - c).wait_recv()
        for cp in first + passed:
            cp.wait_send()
        for cp in local:
            cp.wait()

    sem_shape = (n_now, 2 * len(CHIP_FLIPS))
    f_sem_shape = (nf, len(CHIP_FLIPS))
    outs = pl.pallas_call(
        body, name="gather_weights",
        in_specs=[VMEM_SPEC] * (n + nf), out_specs=[ANY] * (n + nf),
        out_shape=[jax.ShapeDtypeStruct((N_SHARD,) + s.shape, BF16) for s in shards]
        + [jax.ShapeDtypeStruct((N_SHARD,) + s.shape, F32) for s in filters],
        scratch_shapes=[pltpu.VMEM(s.shape, BF16) for s in shards]
        + [pltpu.SemaphoreType.DMA(sem_shape), pltpu.SemaphoreType.DMA(sem_shape), pltpu.SemaphoreType.DMA((n + nf,)),
           pltpu.SemaphoreType.DMA(f_sem_shape), pltpu.SemaphoreType.DMA(f_sem_shape)],
        compiler_params=pltpu.CompilerParams(vmem_limit_bytes=VMEM_LIMIT_BYTES),
    )(*shards, *filters)
    return outs[:n], outs[n:]


def _final_comm(srcs16, small):
    n = len(srcs16)
    axes = ("c", "x", "y")

    def body(*refs):
        srcs, small_ref = refs[:n], refs[n]
        lands, small_out = refs[n + 1:2 * n + 1], refs[2 * n + 1]
        sent, got, send_sems, recv_sems, small_send_sems, small_recv_sems = refs[2 * n + 2:]
        x, y, c = _coords()
        copies = []
        for w in range(n):
            copies += _scatter_copies(srcs[w], lands[w], send_sems.at[w], recv_sems.at[w])
        for cp in copies:
            cp.start()
        current = small_ref
        for stage, axis in enumerate(axes):
            partner = (_flip(x, axis == "x"), _flip(y, axis == "y"), _flip(c, axis == "c"))
            rc = pltpu.make_async_remote_copy(
                src_ref=current, dst_ref=got.at[stage], send_sem=small_send_sems.at[stage],
                recv_sem=small_recv_sems.at[stage], device_id=partner, device_id_type=MESH)
            rc.start()
            rc.wait()
            target = small_out if stage == len(axes) - 1 else sent.at[stage]
            target[...] = current[...] + got[stage]
            current = target
        for cp in copies:
            cp.wait()

    n_peer = len(PEER_FLIPS)
    outs = pl.pallas_call(
        body, name="final_comm",
        in_specs=[ANY] * n + [VMEM_SPEC], out_specs=[ANY] * n + [VMEM_SPEC],
        out_shape=[_land_shape(a) for a in srcs16] + [jax.ShapeDtypeStruct(small.shape, F32)],
        scratch_shapes=[pltpu.VMEM((len(axes) - 1,) + small.shape, F32), pltpu.VMEM((len(axes),) + small.shape, F32),
                        pltpu.SemaphoreType.DMA((n, n_peer)), pltpu.SemaphoreType.DMA((n, n_peer)),
                        pltpu.SemaphoreType.DMA((len(axes),)), pltpu.SemaphoreType.DMA((len(axes),))],
        compiler_params=pltpu.CompilerParams(vmem_limit_bytes=VMEM_LIMIT_BYTES),
    )(*srcs16, small)
    return outs[:n], outs[n]


ADD_CHUNKS = 4


def _scatter_sum(pos, owns, lands):
    n = len(owns)

    def specs(land_shape):
        peers, rows, cols = land_shape
        if cols % (128 * ADD_CHUNKS) == 0:
            blk = (rows, cols // ADD_CHUNKS)
            return (pl.BlockSpec((1,) + blk, lambda i, p: (2 * p[0] + p[1], p[2], i)),
                    pl.BlockSpec((peers,) + blk, lambda i, p: (0, 0, i)),
                    pl.BlockSpec((1,) + blk, lambda i, p: (p[2], 0, i)))
        blk = (rows // ADD_CHUNKS, cols)
        return (pl.BlockSpec((1,) + blk, lambda i, p: (2 * p[0] + p[1], p[2] * ADD_CHUNKS + i, 0)),
                pl.BlockSpec((peers,) + blk, lambda i, p: (0, i, 0)),
                pl.BlockSpec((1,) + blk, lambda i, p: (p[2], i, 0)))

    def body(pos_ref, *refs):
        for idx in range(n):
            own, land, out = refs[idx], refs[n + idx], refs[2 * n + idx]
            total = own[0]
            for f in range(land.shape[0]):
                total = total + land[f].astype(F32)
            out[0] = total

    all_specs = [specs(l.shape) for l in lands]
    return pl.pallas_call(
        body, name="scatter_sum",
        grid_spec=pltpu.PrefetchScalarGridSpec(
            num_scalar_prefetch=1, grid=(ADD_CHUNKS,),
            in_specs=[s[0] for s in all_specs] + [s[1] for s in all_specs], out_specs=[s[2] for s in all_specs]),
        out_shape=[jax.ShapeDtypeStruct((2,) + l.shape[1:], F32) for l in lands],
        compiler_params=_params(),
    )(pos, *owns, *lands)


def _swap_halves(halves):
    n = len(halves)

    def body(*refs):
        ins, outs = refs[:n], refs[n:2 * n]
        send_sems, recv_sems = refs[2 * n:]
        x, y, c = _coords()
        copies = [pltpu.make_async_remote_copy(
            src_ref=ins[idx].at[pl.ds(c, 1)], dst_ref=outs[idx].at[pl.ds(c, 1)], send_sem=send_sems.at[idx],
            recv_sem=recv_sems.at[idx], device_id=(x, y, 1 - c), device_id_type=MESH) for idx in range(n)]
        for cp in copies:
            cp.start()
        for cp in copies:
            cp.wait()

    return pl.pallas_call(
        body, name="swap_halves",
        in_specs=[ANY] * n, out_specs=[ANY] * n, input_output_aliases={idx: idx for idx in range(n)},
        out_shape=[jax.ShapeDtypeStruct(a.shape, F32) for a in halves],
        scratch_shapes=[pltpu.SemaphoreType.DMA((n,)), pltpu.SemaphoreType.DMA((n,))],
    )(*halves)


def _adamw_math(w, g, m, v):
    m = ADAM_B1 * m + (1.0 - ADAM_B1) * g
    v = ADAM_B2 * v + (1.0 - ADAM_B2) * jnp.square(g)
    m_hat = m / (1.0 - ADAM_B1 ** ADAM_STEP)
    v_hat = v / (1.0 - ADAM_B2 ** ADAM_STEP)
    delta = -ADAM_LR * (m_hat / (jnp.sqrt(v_hat) + ADAM_EPS) + ADAM_WD * w)
    return delta, m, v


def _adamw(name, w, g, m, v, block_rows):
    rows, cols = w.shape

    def body(w_ref, g_ref, m_ref, v_ref, d_out, m_out, v_out):
        d_out[...], m_out[...], v_out[...] = _adamw_math(w_ref[...], g_ref[...], m_ref[...], v_ref[...])

    spec = pl.BlockSpec((block_rows, cols), lambda i: (i, 0))
    shape = jax.ShapeDtypeStruct((rows, cols), F32)
    return pl.pallas_call(
        body, grid=(rows // block_rows,), name=name, in_specs=[spec] * 4, out_specs=[spec] * 3,
        out_shape=[shape] * 3, compiler_params=_params(),
    )(w, g, m, v)


def _adamw_ada(c_all16, dmod16, w, m, v, block_rows):
    rows, cols = w.shape

    def body(c_ref, dm_ref, w_ref, m_ref, v_ref, g_out, d_out, m_out, v_out):
        cv = c_ref[...]
        g = _tn((cv * jax.nn.sigmoid(cv)).astype(BF16), dm_ref[...].astype(BF16))
        g_out[...] = g
        d_out[...], m_out[...], v_out[...] = _adamw_math(w_ref[...], g, m_ref[...], v_ref[...])

    spec = pl.BlockSpec((block_rows, cols), lambda i: (i, 0))
    shape = jax.ShapeDtypeStruct((rows, cols), F32)
    return pl.pallas_call(
        body, grid=(rows // block_rows,), name="adamw_w_ada",
        in_specs=[pl.BlockSpec((16, block_rows), lambda i: (0, i)), _full(dmod16.shape), spec, spec, spec],
        out_specs=[spec] * 4, out_shape=[shape] * 4, compiler_params=_params(),
    )(c_all16, dmod16, w, m, v)


SMALL_REPLICATED = ["b_ada", "norm1_gain", "conv_dw_b", "conv_ln_g", "conv_ln_b", "gm_ln_g", "gm_ln_b", "gm_ws", "gm_bs",
                    "mix_out_gain", "norm2_gain", "ffn_dw_b", "final_gain"]
SMALL_SHARDED = ["conv_dw_w", "ffn_dw_w"]
PACK_ROWS = 256
ADAM_PACK_ROWS = 160
WEIGHT_ORDER = ["w_ada", "b_ada", "norm1_gain", "w_in", "conv_dw_w", "conv_dw_b", "conv_ln_g", "conv_ln_b", "gm_ln_g",
                "gm_ln_b", "gm_ws", "gm_bs", "mix_out_gain", "w_out", "norm2_gain", "w_up", "ffn_dw_w", "ffn_dw_b",
                "w_down", "final_gain"]


def _pack(parts, rows):
    flat = jnp.concatenate([a.reshape(-1) for a in parts])
    return jnp.pad(flat, (0, rows * D_MODEL - flat.shape[0])).reshape(rows, D_MODEL)


def _unpack(packed, shapes):
    flat = packed.reshape(-1)
    out, pos = [], 0
    for s in shapes:
        size = 1
        for d in s:
            size *= d
        out.append(flat[pos:pos + size].reshape(s))
        pos += size
    return out


def kernel(x, c, w_ada, b_ada, norm1_gain, w_in, conv_dw_w, conv_dw_b, conv_ln_g, conv_ln_b, gm_ln_g, gm_ln_b, gm_ws, gm_bs, mix_out_gain, w_out, norm2_gain, w_up, ffn_dw_w, ffn_dw_b, w_down, final_gain, loss_target, m_w_ada, m_b_ada, m_norm1_gain, m_w_in, m_conv_dw_w, m_conv_dw_b, m_conv_ln_g, m_conv_ln_b, m_gm_ln_g, m_gm_ln_b, m_gm_ws, m_gm_bs, m_mix_out_gain, m_w_out, m_norm2_gain, m_w_up, m_ffn_dw_w, m_ffn_dw_b, m_w_down, m_final_gain, v_w_ada, v_b_ada, v_norm1_gain, v_w_in, v_conv_dw_w, v_conv_dw_b, v_conv_ln_g, v_conv_ln_b, v_gm_ln_g, v_gm_ln_b, v_gm_ws, v_gm_bs, v_mix_out_gain, v_w_out, v_norm2_gain, v_w_up, v_ffn_dw_w, v_ffn_dw_b, v_w_down, v_final_gain):
    weights = dict(w_ada=w_ada, b_ada=b_ada, norm1_gain=norm1_gain, w_in=w_in, conv_dw_w=conv_dw_w, conv_dw_b=conv_dw_b,
                   conv_ln_g=conv_ln_g, conv_ln_b=conv_ln_b, gm_ln_g=gm_ln_g, gm_ln_b=gm_ln_b, gm_ws=gm_ws, gm_bs=gm_bs,
                   mix_out_gain=mix_out_gain, w_out=w_out, norm2_gain=norm2_gain, w_up=w_up, ffn_dw_w=ffn_dw_w,
                   ffn_dw_b=ffn_dw_b, w_down=w_down, final_gain=final_gain)
    mom1 = dict(w_ada=m_w_ada, b_ada=m_b_ada, norm1_gain=m_norm1_gain, w_in=m_w_in, conv_dw_w=m_conv_dw_w,
                conv_dw_b=m_conv_dw_b, conv_ln_g=m_conv_ln_g, conv_ln_b=m_conv_ln_b, gm_ln_g=m_gm_ln_g, gm_ln_b=m_gm_ln_b,
                gm_ws=m_gm_ws, gm_bs=m_gm_bs, mix_out_gain=m_mix_out_gain, w_out=m_w_out, norm2_gain=m_norm2_gain,
                w_up=m_w_up, ffn_dw_w=m_ffn_dw_w, ffn_dw_b=m_ffn_dw_b, w_down=m_w_down, final_gain=m_final_gain)
    mom2 = dict(w_ada=v_w_ada, b_ada=v_b_ada, norm1_gain=v_norm1_gain, w_in=v_w_in, conv_dw_w=v_conv_dw_w,
                conv_dw_b=v_conv_dw_b, conv_ln_g=v_conv_ln_g, conv_ln_b=v_conv_ln_b, gm_ln_g=v_gm_ln_g, gm_ln_b=v_gm_ln_b,
                gm_ws=v_gm_ws, gm_bs=v_gm_bs, mix_out_gain=v_mix_out_gain, w_out=v_w_out, norm2_gain=v_norm2_gain,
                w_up=v_w_up, ffn_dw_w=v_ffn_dw_w, ffn_dw_b=v_ffn_dw_b, w_down=v_w_down, final_gain=v_final_gain)
    shard = 2 * lax.axis_index("x") + lax.axis_index("y")
    me = 2 * shard + lax.axis_index("c")

    ada_cols = w_ada.shape[2]
    b_ada_sh = lax.dynamic_slice(b_ada, (0, shard * ada_cols), (1, ada_cols))
    c_all64, mod32 = _ada_mod(c, w_ada[0], b_ada_sh)
    c_all = c_all64[::8]
    mod = mod32[::8].reshape(1, N_SHARD * ada_cols)

    (w_in_g, w_out_g, w_up_part, w_down_part), (conv_w_g, ffn_w_g) = _gather_weights(
        [w_in[0], w_out[0], w_up[0], w_down[0]], [conv_dw_w[0], ffn_dw_w[0]], n_now=2)
    conv_w_full = jnp.transpose(conv_w_g, (1, 0, 2)).reshape(CONV_K, D_HALF)
    ffn_w_full = jnp.transpose(ffn_w_g, (1, 0, 2)).reshape(FFN_K, 2 * D_FF)

    p = dict(norm1_gain=norm1_gain, conv_dw_w=conv_w_full, conv_dw_b=conv_dw_b, conv_ln_g=conv_ln_g,
             conv_ln_b=conv_ln_b, gm_ln_g=gm_ln_g, gm_ln_b=gm_ln_b, gm_ws=gm_ws[0], gm_bs=gm_bs[0],
             mix_out_gain=mix_out_gain, norm2_gain=norm2_gain, ffn_dw_w=ffn_w_full, ffn_dw_b=ffn_dw_b,
             final_gain=final_gain[None])
    grad_x, g, d_mod, loss, in_flight = _local_step(
        x[0], loss_target[0], mod, p, w_in_g, w_out_g.reshape(D_MODEL, D_MODEL), w_up_part, w_down_part)

    n_mod = d_mod.shape[1]
    dmod_rows = lax.dynamic_update_slice(jnp.zeros((N_DEV, n_mod), F32), d_mod, (me, 0))
    g["b_ada"] = d_mod
    small = _pack([g[k] for k in SMALL_REPLICATED] + [g[k] for k in SMALL_SHARDED] + [dmod_rows, loss[0, :1]], PACK_ROWS)
    (land_w_in, land_w_out), small = _final_comm([in_flight["w_in16"], in_flight["w_out16"]], small)
    pos = jnp.stack(_coords()).astype(jnp.int32)
    halves = _scatter_sum(pos, [g["w_in"], g["w_out"], g["w_up"], g["w_down"]],
                          [land_w_in, land_w_out, in_flight["land_w_up"], in_flight["land_w_down"]])
    full = _swap_halves(halves)
    grads = dict(w_in=full[0].reshape(w_in.shape[1:]), w_out=full[1].reshape(w_out.shape[1:]),
                 w_up=full[2].reshape(w_up.shape[1:]), w_down=full[3].reshape(w_down.shape[1:]))

    small_shapes = ([weights[k].shape for k in SMALL_REPLICATED] + [(CONV_K, D_HALF), (FFN_K, 2 * D_FF)]
                    + [(N_DEV, n_mod), (1,)])
    *small_grads, conv_w_grad, ffn_w_grad, dmod_all, loss_sum = _unpack(small, small_shapes)
    grads.update(zip(SMALL_REPLICATED, small_grads))
    grads["conv_dw_w"] = lax.dynamic_slice(conv_w_grad, (0, shard * conv_dw_w.shape[2]), conv_dw_w.shape[1:])[None]
    grads["ffn_dw_w"] = lax.dynamic_slice(ffn_w_grad, (0, shard * ffn_dw_w.shape[2]), ffn_dw_w.shape[1:])[None]

    delta, new_m, new_v = {}, {}, {}
    for name, block_rows in (("w_in", 256), ("w_out", 128), ("w_up", 256), ("w_down", 352)):
        delta[name], new_m[name], new_v[name] = [a[None] for a in _adamw(
            "adamw_" + name, weights[name][0], grads[name], mom1[name][0], mom2[name][0], block_rows)]
        grads[name] = grads[name][None]
    dmod_sh = lax.dynamic_slice(dmod_all, (0, shard * ada_cols), (N_DEV, ada_cols))
    pad8 = ((0, 16 - N_DEV), (0, 0))
    grads["w_ada"], delta["w_ada"], new_m["w_ada"], new_v["w_ada"] = [a[None] for a in _adamw_ada(
        jnp.pad(c_all, pad8), jnp.pad(dmod_sh, pad8), w_ada[0], m_w_ada[0], v_w_ada[0], 256)]
    small_names = SMALL_REPLICATED + SMALL_SHARDED
    packed = [_pack([d[k] for k in small_names], ADAM_PACK_ROWS) for d in (weights, grads, mom1, mom2)]
    small_out = _adamw("adamw_small", *packed, ADAM_PACK_ROWS)
    for d, arr in zip((delta, new_m, new_v), small_out):
        d.update(zip(small_names, _unpack(arr, [weights[k].shape for k in small_names])))

    return (loss_sum.reshape(()), grad_x[None], *[grads[k] for k in WEIGHT_ORDER], *[delta[k] for k in WEIGHT_ORDER],
            *[new_m[k] for k in WEIGHT_ORDER], *[new_v[k] for k in WEIGHT_ORDER])
```

```python
import functools

import jax
import jax.numpy as jnp
from jax import lax
from jax.experimental import pallas as pl
from jax.experimental.pallas import tpu as pltpu

F32 = jnp.float32
BF16 = jnp.bfloat16

D_MODEL = 1024
D_HALF = 512
D_FF = 2816
CONV_K = 31
FFN_K = 3
CHUNK = 128
N_HEADS = 8
HEAD_DIM = 64
N_SHARD = 4
N_DEV = 8
RMS_EPS = 1e-6
LN_EPS = 1e-5
ADAM_LR, ADAM_B1, ADAM_B2, ADAM_EPS, ADAM_WD, ADAM_STEP = 0.001, 0.9, 0.999, 1e-08, 0.01, 10

TILE = 256
HALO = 32
FFN_HALO = 16
FFN_BLK = 256
UP_SHARD = 2 * D_FF // N_SHARD
VMEM_LIMIT_BYTES = 56 * 1024 * 1024

ANY = pl.BlockSpec(memory_space=pl.ANY)
NT_DIMS = (((1,), (1,)), ((), ()))
TN_DIMS = (((0,), (0,)), ((), ()))


def _full(shape):
    return pl.BlockSpec(shape, lambda i: (0,) * len(shape))


def _nn(a, b):
    return jnp.dot(a, b, preferred_element_type=F32)


def _nt(a, b):
    return lax.dot_general(a, b, NT_DIMS, preferred_element_type=F32)


def _tn(a, b):
    return lax.dot_general(a, b, TN_DIMS, preferred_element_type=F32)


def _colsum(a):
    return jnp.sum(a, axis=0, keepdims=True)


def _params(semantics=("arbitrary",)):
    return pltpu.CompilerParams(dimension_semantics=semantics, vmem_limit_bytes=VMEM_LIMIT_BYTES)


def _rms(v, gain):
    return v * lax.rsqrt(jnp.mean(v * v, axis=-1, keepdims=True) + RMS_EPS) * gain


def _layer_norm(v, gain, bias):
    mu = jnp.mean(v, axis=-1, keepdims=True)
    var = jnp.mean(jnp.square(v - mu), axis=-1, keepdims=True)
    return (v - mu) * lax.rsqrt(var + LN_EPS) * gain + bias


def _mod_norm(v, gain, scale, shift):
    return _rms(v, gain) * (1.0 + scale) + shift


def _conv_branch(a1, ln_g, ln_b, out_gain):
    a2 = _layer_norm(a1, ln_g, ln_b)
    return _rms(a2 * jax.nn.sigmoid(a2), out_gain)


def _gate_branch(gu, sp, out_gain):
    return _rms(jax.nn.gelu(gu) * sp, out_gain)


def _gv_norm(gv, ln_g, ln_b):
    return _layer_norm(jax.nn.gelu(gv), ln_g, ln_b)


def _head_pair_matmul(wp_ref, v):
    lane = lax.broadcasted_iota(jnp.int32, (CHUNK, CHUNK), 1)
    rows = []
    for n in range(v.shape[0] // CHUNK):
        cols = []
        for j in range(N_HEADS // 2):
            r = _nn(wp_ref[j], v[n * CHUNK:(n + 1) * CHUNK, j * CHUNK:(j + 1) * CHUNK])
            cols.append(jnp.where(lane < HEAD_DIM, r[:CHUNK], r[CHUNK:]))
        rows.append(jnp.concatenate(cols, axis=1))
    return jnp.concatenate(rows, axis=0)


def _tile_bias(bs, tokens):
    return jnp.concatenate([bs] * (tokens // CHUNK), axis=0)


FORWARD_LEAD = 8


def _fwd_mixer(x, vec, conv_w, wpair, bs_full, w_in_g, w_out_g, late_parts):
    seq = x.shape[0]
    n_tiles = seq // TILE
    t = TILE
    n_late = len(late_parts)
    forward_step = max(n_tiles - FORWARD_LEAD, 0)
    names = ["norm1_gain", "sc1", "sh1", "gt1", "conv_dw_b", "conv_ln_g", "conv_ln_b", "gm_ln_g", "gm_ln_b",
             "mix_out_gain"]
    vecs = [vec[k] for k in names]

    def body(x_ref, g1, sc1, sh1, gt1, cb, clg, clb, vg, vb, mg, cw, wp, bs, win_hbm, wout_hbm, *rest):
        late = rest[n_late:2 * n_late]
        z_ref, a1_ref, sp_ref, y_ref, o1_ref, x2_ref = rest[2 * n_late:2 * n_late + 6]
        win_v, wout_v, halo, bank, sem, send_sems, recv_sems = rest[2 * n_late + 6:]
        i = pl.program_id(0)
        mx, my, mc = _coords()
        shard = 2 * mx + my

        def half(w, which):
            h = late[w].shape[1] // 2
            return pl.ds(pl.multiple_of(which * h, 16), h)

        def chip_of(j):
            return 2 * _flip(mx, CHIP_FLIPS[j][0]) + _flip(my, CHIP_FLIPS[j][1])

        def ici_copy(w, j, slot):
            rows = late[w].at[slot, half(w, mc)]
            return pltpu.make_async_remote_copy(
                src_ref=rows, dst_ref=rows, send_sem=send_sems.at[w, j], recv_sem=recv_sems.at[w, j],
                device_id=(_flip(mx, CHIP_FLIPS[j][0]), _flip(my, CHIP_FLIPS[j][1]), mc), device_id_type=MESH)

        def d2d_copy(w, j, which):
            rows = late[w].at[chip_of(j), half(w, which)]
            return pltpu.make_async_remote_copy(
                src_ref=rows, dst_ref=rows, send_sem=send_sems.at[w, len(CHIP_FLIPS) + j],
                recv_sem=recv_sems.at[w, len(CHIP_FLIPS) + j], device_id=(mx, my, 1 - mc), device_id_type=MESH)

        pairs = [(w, j) for w in range(n_late) for j in range(len(CHIP_FLIPS))]

        @pl.when(i == 0)
        def _():
            for w, j in pairs:
                ici_copy(w, j, shard).start()
            cps = [pltpu.make_async_copy(win_hbm, win_v, sem.at[0]),
                   pltpu.make_async_copy(wout_hbm, wout_v, sem.at[1])]
            for cp in cps:
                cp.start()
            for cp in cps:
                cp.wait()
            halo[...] = jnp.zeros_like(halo)

        @pl.when(i == forward_step)
        def _():
            for w, j in pairs:
                ici_copy(w, j, chip_of(j)).wait_recv()
                d2d_copy(w, j, mc).start()

        xv = x_ref[...]
        h1b = _mod_norm(xv, g1[...], sc1[...], sh1[...]).astype(BF16)
        zs = [_nn(h1b, win_v[k]) for k in range(N_SHARD)]
        for k in range(N_SHARD):
            z_ref[:, k * D_HALF:(k + 1) * D_HALF] = zs[k]
        ca, cg, gu, gv = zs
        a0 = ca * jax.nn.sigmoid(cg)
        ext = jnp.concatenate([halo[...], a0], axis=0)
        halo[...] = a0[t - HALO:]
        bank[0] = ext
        for b in range(1, 8):
            bank[b] = pltpu.roll(ext, b, axis=0)
        a1 = jnp.zeros((t, D_HALF), F32) + cb[...]
        for s in range(CONV_K):
            q, b = divmod(s, 8)
            a1 = a1 + bank[b, pl.ds(HALO - 8 * q, t), :] * cw[pl.ds(CONV_K - 1 - s, 1), :]
        a1_ref[...] = a1
        mgv = mg[...]
        ya = _conv_branch(a1, clg[...], clb[...], mgv[:, :D_HALF])
        gvn = _gv_norm(gv, vg[...], vb[...]).astype(BF16)
        sp = _head_pair_matmul(wp, gvn) + _tile_bias(bs[...], t)
        sp_ref[...] = sp
        yg = _gate_branch(gu, sp, mgv[:, D_HALF:])
        yb = jnp.concatenate([ya, yg], axis=1).astype(BF16)
        y_ref[...] = yb
        o1 = _nn(yb, wout_v[...])
        o1_ref[...] = o1
        x2_ref[...] = xv + gt1[...] * o1

        @pl.when(i == n_tiles - 1)
        def _():
            for w, j in pairs:
                d2d_copy(w, j, 1 - mc).wait_recv()
            for w, j in pairs:
                ici_copy(w, j, shard).wait_send()
                d2d_copy(w, j, mc).wait_send()

    def row(width):
        return pl.BlockSpec((t, width), lambda i: (i, 0))

    out_shape = [jax.ShapeDtypeStruct((seq, 4 * D_HALF), F32), jax.ShapeDtypeStruct((seq, D_HALF), F32),
                 jax.ShapeDtypeStruct((seq, D_HALF), F32), jax.ShapeDtypeStruct((seq, D_MODEL), BF16),
                 jax.ShapeDtypeStruct((seq, D_MODEL), F32), jax.ShapeDtypeStruct((seq, D_MODEL), F32)]
    n_in = 1 + len(vecs) + 3 + 2
    sem_shape = (n_late, 2 * len(CHIP_FLIPS))
    outs = pl.pallas_call(
        body, grid=(n_tiles,), name="fwd_mixer",
        in_specs=[row(D_MODEL)] + [_full(v.shape) for v in vecs]
        + [_full(conv_w.shape), _full(wpair.shape), _full(bs_full.shape), ANY, ANY] + [ANY] * n_late,
        out_specs=[ANY] * n_late + [row(4 * D_HALF), row(D_HALF), row(D_HALF), row(D_MODEL), row(D_MODEL),
                                    row(D_MODEL)],
        out_shape=[jax.ShapeDtypeStruct(a.shape, a.dtype) for a in late_parts] + out_shape,
        input_output_aliases={n_in + w: w for w in range(n_late)},
        scratch_shapes=[pltpu.VMEM(w_in_g.shape, BF16), pltpu.VMEM(w_out_g.shape, BF16),
                        pltpu.VMEM((HALO, D_HALF), F32), pltpu.VMEM((8, t + HALO, D_HALF), F32),
                        pltpu.SemaphoreType.DMA((2,)), pltpu.SemaphoreType.DMA(sem_shape),
                        pltpu.SemaphoreType.DMA(sem_shape)],
        compiler_params=_params(),
    )(x, *vecs, conv_w, wpair, bs_full, w_in_g, w_out_g, *late_parts)
    return outs[n_late:], outs[:n_late]


def _fwd_up(x2, norm2_gain, sc2, sh2, w_up_g, to_inter, u_dtype):
    seq = x2.shape[0]
    n_tiles = seq // TILE
    t = TILE

    def body(x2_ref, g2, sc2_ref, sh2_ref, pm_ref, wup_hbm, u_ref, wup_v, sem):
        @pl.when(pl.program_id(0) == 0)
        def _():
            cp = pltpu.make_async_copy(wup_hbm, wup_v, sem.at[0])
            cp.start()
            cp.wait()

        h2b = _mod_norm(x2_ref[...], g2[...], sc2_ref[...], sh2_ref[...]).astype(BF16)
        h2b = _nn(pm_ref[...], h2b).astype(BF16)
        for k in range(N_SHARD):
            u_ref[:, k * UP_SHARD:(k + 1) * UP_SHARD] = _nn(h2b, wup_v[k]).astype(u_ref.dtype)

    return pl.pallas_call(
        body, grid=(n_tiles,), name="fwd_up",
        in_specs=[pl.BlockSpec((t, D_MODEL), lambda i: (i, 0)), _full((1, D_MODEL)), _full((1, D_MODEL)),
                  _full((1, D_MODEL)), _full(to_inter.shape), ANY],
        out_specs=pl.BlockSpec((t, 2 * D_FF), lambda i: (i, 0)),
        out_shape=jax.ShapeDtypeStruct((seq, 2 * D_FF), u_dtype),
        scratch_shapes=[pltpu.VMEM(w_up_g.shape, BF16), pltpu.SemaphoreType.DMA((1,))],
        compiler_params=_params(),
    )(x2, norm2_gain, sc2, sh2, to_inter, w_up_g)


def _interleave_matrices():
    row = jnp.arange(TILE)
    token_of_row = (row % 8) * (TILE // 8) + row // 8
    to_inter = (token_of_row[:, None] == row[None, :]).astype(BF16)
    return to_inter, jnp.transpose(to_inter)


def _ffn_tail(u, x2, target, ffn_w, ffn_b, gt2, final_gain, w_down_g, to_inter, to_natural):
    seq = x2.shape[0]
    n_tiles = seq // TILE
    t = TILE
    n_blk = D_FF // FFN_BLK
    inv_d = 1.0 / D_MODEL

    def final_norm(x3, gain):
        return _rms(x3, gain)

    def body(u_ref, uh_ref, x2_ref, tgt_ref, fw, fb, gt2_ref, fg, pm_ref, pmt_ref, wd_hbm,
             du_ref, dx3_ref, dfw_ref, dfb_ref, dfg_ref, dgt2_ref, loss_ref, dwd_hbm, dwd16_hbm,
             wd_v, dwd_acc, carry, sil_s, vds_s, f_s, du_s, sem):
        i = pl.program_id(0)
        tile = n_tiles - 1 - i
        sublane = lax.broadcasted_iota(jnp.int32, (8, FFN_BLK), 0)

        @pl.when(i == 0)
        def _():
            cp = pltpu.make_async_copy(wd_hbm, wd_v, sem.at[0])
            cp.start()
            cp.wait()
            dwd_acc[...] = jnp.zeros_like(dwd_acc)
            carry[...] = jnp.zeros_like(carry)
            dfw_ref[...] = jnp.zeros_like(dfw_ref)
            dfb_ref[...] = jnp.zeros_like(dfb_ref)
            dfg_ref[...] = jnp.zeros_like(dfg_ref)
            dgt2_ref[...] = jnp.zeros_like(dgt2_ref)
            loss_ref[...] = jnp.zeros_like(loss_ref)

        def cols_of(j):
            return pl.ds(j * FFN_BLK, FFN_BLK), pl.ds(D_FF + j * FFN_BLK, FFN_BLK)

        def wrap_down(last, before):
            return jnp.where(sublane == 0, pltpu.roll(before, 1, axis=0), pltpu.roll(last, 1, axis=0))

        def wrap_up(first, after):
            return jnp.where(sublane == 7, pltpu.roll(after, 7, axis=0), pltpu.roll(first, 7, axis=0))

        def conv(cols):
            cur = u_ref[:, cols]
            prev = jnp.where(tile > 0, uh_ref[:, cols], 0.0)
            w1 = wrap_down(cur[t - 8:], prev[8:])
            w2 = wrap_down(cur[t - 16:t - 8], prev[:8])
            back1 = jnp.concatenate([w1, cur[:t - 8]], axis=0)
            back2 = jnp.concatenate([w2, w1, cur[:t - 16]], axis=0)
            return (fb[:, cols] + cur * fw[pl.ds(2, 1), cols] + back1 * fw[pl.ds(1, 1), cols]
                    + back2 * fw[pl.ds(0, 1), cols])

        pm_t = pmt_ref[...]
        o2 = jnp.zeros((t, D_MODEL), F32)
        for j in range(n_blk):
            cv, cg = cols_of(j)
            val, gate = conv(cv), conv(cg)
            sig = jax.nn.sigmoid(gate)
            sil = gate * sig
            fb16 = (sil * val).astype(BF16)
            sil_s[:, cv] = sil
            vds_s[:, cv] = val * (sig + sil * (1.0 - sig))
            f_s[:, cv] = fb16
            o2 = o2 + _nn(fb16, wd_v[pl.ds(j * FFN_BLK, FFN_BLK), :])
        hi = o2.astype(BF16)
        rest = o2 - hi.astype(F32)
        mid = rest.astype(BF16)
        low = (rest - mid.astype(F32)).astype(BF16)
        o2 = _nn(jnp.concatenate([pm_t, pm_t, pm_t], axis=1), jnp.concatenate([hi, mid, low], axis=0))

        gt2v = gt2_ref[...]
        x3 = x2_ref[...] + gt2v * o2
        out, out_vjp = jax.vjp(final_norm, x3, fg[...])
        diff = out - tgt_ref[...]
        loss_ref[...] += jnp.zeros_like(loss_ref) + 0.5 * inv_d * jnp.sum(diff * diff)
        dx3, dfg = out_vjp(diff * inv_d)
        dfg_ref[...] += dfg
        dgt2_ref[...] += _colsum(dx3 * o2)
        dx3_ref[...] = dx3
        do2b = _nn(pm_ref[...], (gt2v * dx3).astype(BF16)).astype(BF16)

        for j in range(n_blk):
            cv, cg = cols_of(j)
            rows = pl.ds(j * FFN_BLK, FFN_BLK)
            df = _nt(do2b, wd_v[rows, :])
            dwd_acc[rows, :] += _tn(f_s[:, cv], do2b)
            for dd, cols in ((df * sil_s[:, cv], cv), (df * vds_s[:, cv], cg)):
                dfb_ref[:, cols] += _colsum(dd)
                nxt = carry[:, cols]
                w1 = wrap_up(dd[:8], nxt[:8])
                w2 = wrap_up(dd[8:16], nxt[8:])
                ahead = (dd, jnp.concatenate([dd[8:], w1], axis=0), jnp.concatenate([dd[16:], w1, w2], axis=0))
                carry[:, cols] = dd[:16]
                uv = u_ref[:, cols]
                du = jnp.zeros((t, FFN_BLK), F32)
                for s in range(FFN_K):
                    du = du + ahead[s] * fw[pl.ds(FFN_K - 1 - s, 1), cols]
                    dfw_ref[pl.ds(FFN_K - 1 - s, 1), cols] += _colsum(ahead[s] * uv)
                du_s[:, cols] = du.astype(BF16)
        du_ref[...] = _nn(pm_t, du_s[...]).astype(BF16)

        @pl.when(i == n_tiles - 1)
        def _():
            cp = pltpu.make_async_copy(dwd_acc, dwd_hbm, sem.at[1])
            cp.start()
            wd_v[...] = dwd_acc[...].astype(BF16)
            cp16 = pltpu.make_async_copy(wd_v, dwd16_hbm, sem.at[2])
            cp16.start()
            cp.wait()
            cp16.wait()

    def rev(width):
        return pl.BlockSpec((t, width), lambda i: (n_tiles - 1 - i, 0))

    assert FFN_K == 3 and u.dtype == F32
    halo_spec = pl.BlockSpec(
        (FFN_HALO, 2 * D_FF), lambda i: (jnp.maximum((n_tiles - 1 - i) * (t // FFN_HALO) - 1, 0), 0))
    out_shape = [jax.ShapeDtypeStruct((seq, 2 * D_FF), BF16), jax.ShapeDtypeStruct((seq, D_MODEL), F32),
                 jax.ShapeDtypeStruct((FFN_K, 2 * D_FF), F32), jax.ShapeDtypeStruct((1, 2 * D_FF), F32),
                 jax.ShapeDtypeStruct((1, D_MODEL), F32), jax.ShapeDtypeStruct((1, D_MODEL), F32),
                 jax.ShapeDtypeStruct((1, 128), F32), jax.ShapeDtypeStruct((D_FF, D_MODEL), F32),
                 jax.ShapeDtypeStruct((D_FF, D_MODEL), BF16)]
    return pl.pallas_call(
        body, grid=(n_tiles,), name="ffn_tail",
        in_specs=[rev(2 * D_FF), halo_spec, rev(D_MODEL), rev(D_MODEL), _full(ffn_w.shape), _full(ffn_b.shape),
                  _full(gt2.shape), _full(final_gain.shape), _full(to_inter.shape), _full(to_natural.shape), ANY],
        out_specs=[rev(2 * D_FF), rev(D_MODEL), _full((FFN_K, 2 * D_FF)), _full((1, 2 * D_FF)), _full((1, D_MODEL)),
                   _full((1, D_MODEL)), _full((1, 128)), ANY, ANY],
        out_shape=out_shape,
        scratch_shapes=[pltpu.VMEM((D_FF, D_MODEL), BF16), pltpu.VMEM((D_FF, D_MODEL), F32),
                        pltpu.VMEM((FFN_HALO, 2 * D_FF), F32),
                        pltpu.VMEM((t, D_FF), F32), pltpu.VMEM((t, D_FF), F32),
                        pltpu.VMEM((t, D_FF), BF16), pltpu.VMEM((t, 2 * D_FF), BF16), pltpu.SemaphoreType.DMA((3,))],
        compiler_params=_params(),
    )(u, u, x2, target, ffn_w, ffn_b, gt2, final_gain, to_inter, to_natural, w_down_g)


def _scatter_copies(src16, land, send_sems, recv_sems):
    x, y, c = _coords()
    h = src16.shape[1] // 2
    copies = []
    for f, flip in enumerate(PEER_FLIPS):
        tx, ty, tc = _flip(x, flip[0]), _flip(y, flip[1]), _flip(c, flip[2])
        copies.append(pltpu.make_async_remote_copy(
            src_ref=src16.at[2 * tx + ty, pl.ds(pl.multiple_of(tc * h, 16), h)], dst_ref=land.at[f],
            send_sem=send_sems.at[f], recv_sem=recv_sems.at[f], device_id=(tx, ty, tc), device_id_type=MESH))
    return copies


def _land_shape(src16):
    return jax.ShapeDtypeStruct((len(PEER_FLIPS), src16.shape[1] // 2, src16.shape[2]), BF16)


def _bwd_up(du, x2, dx3, norm2_gain, sc2, sh2, w_up_g, dwd16):
    seq = x2.shape[0]
    n_tiles = seq // TILE
    t = TILE

    def body(du_ref, x2_ref, dx3_ref, g2, sc2_ref, sh2_ref, wup_hbm, dwd16_hbm,
             dx2_ref, dg2_ref, dsc2_ref, dsh2_ref, dwup_hbm, dwup16_hbm, land_hbm,
             wup_v, dwup_acc, sem, send_sems, recv_sems):
        i = pl.program_id(0)

        @pl.when(i == 0)
        def _():
            for cp in _scatter_copies(dwd16_hbm, land_hbm, send_sems, recv_sems):
                cp.start()
            cp = pltpu.make_async_copy(wup_hbm, wup_v, sem.at[0])
            cp.start()
            cp.wait()
            dwup_acc[...] = jnp.zeros_like(dwup_acc)
            dg2_ref[...] = jnp.zeros_like(dg2_ref)
            dsc2_ref[...] = jnp.zeros_like(dsc2_ref)
            dsh2_ref[...] = jnp.zeros_like(dsh2_ref)

        h2, h2_vjp = jax.vjp(_mod_norm, x2_ref[...], g2[...], sc2_ref[...], sh2_ref[...])
        h2b = h2.astype(BF16)
        dh2 = jnp.zeros((t, D_MODEL), F32)
        for k in range(N_SHARD):
            dub = du_ref[:, k * UP_SHARD:(k + 1) * UP_SHARD]
            dh2 = dh2 + _nt(dub, wup_v[k])
            dwup_acc[k] += _tn(h2b, dub)
        dx2, dg2, dsc2, dsh2 = h2_vjp(dh2)
        dx2_ref[...] = dx3_ref[...] + dx2
        dg2_ref[...] += dg2
        dsc2_ref[...] += dsc2
        dsh2_ref[...] += dsh2

        @pl.when(i == n_tiles - 1)
        def _():
            cp = pltpu.make_async_copy(dwup_acc, dwup_hbm, sem.at[1])
            cp.start()
            for k in range(N_SHARD):
                wup_v[k] = dwup_acc[k].astype(BF16)
            cp16 = pltpu.make_async_copy(wup_v, dwup16_hbm, sem.at[2])
            cp16.start()
            cp.wait()
            cp16.wait()
            for rc in _scatter_copies(dwd16_hbm, land_hbm, send_sems, recv_sems):
                rc.wait()

    def row(width):
        return pl.BlockSpec((t, width), lambda i: (i, 0))

    vec = jax.ShapeDtypeStruct((1, D_MODEL), F32)
    n_peer = len(PEER_FLIPS)
    return pl.pallas_call(
        body, grid=(n_tiles,), name="bwd_up",
        in_specs=[row(2 * D_FF), row(D_MODEL), row(D_MODEL), _full((1, D_MODEL)), _full((1, D_MODEL)),
                  _full((1, D_MODEL)), ANY, ANY],
        out_specs=[row(D_MODEL), _full((1, D_MODEL)), _full((1, D_MODEL)), _full((1, D_MODEL)), ANY, ANY, ANY],
        out_shape=[jax.ShapeDtypeStruct((seq, D_MODEL), F32), vec, vec, vec,
                   jax.ShapeDtypeStruct(w_up_g.shape, F32), jax.ShapeDtypeStruct(w_up_g.shape, BF16),
                   _land_shape(dwd16)],
        scratch_shapes=[pltpu.VMEM(w_up_g.shape, BF16), pltpu.VMEM(w_up_g.shape, F32), pltpu.SemaphoreType.DMA((3,)),
                        pltpu.SemaphoreType.DMA((n_peer,)), pltpu.SemaphoreType.DMA((n_peer,))],
        compiler_params=_params(),
    )(du, x2, dx3, norm2_gain, sc2, sh2, w_up_g, dwd16)


def _bwd_mixer(dx2, x, z, a1, sp, yb, o1, vec, conv_w, wpair, wpair_t, causal_mask, w_in_g, w_out_g, dwup16):
    seq = x.shape[0]
    n_tiles = seq // TILE
    t = TILE
    names = ["norm1_gain", "sc1", "sh1", "gt1", "conv_ln_g", "conv_ln_b", "gm_ln_g", "gm_ln_b", "mix_out_gain"]
    vecs = [vec[k] for k in names]

    def body(dx2_ref, x_ref, z_ref, a1_ref, sp_ref, y_ref, o1_ref, g1, sc1, sh1, gt1, clg, clb, vg, vb, mg,
             cw, wp, wpt, mask_ref, win_hbm, wout_hbm, dwup16_hbm,
             gx_ref, dg1_ref, dsc1_ref, dsh1_ref, dgt1_ref, dcw_ref, dcb_ref, dclg_ref, dclb_ref, dvg_ref, dvb_ref,
             dmg_ref, dws_ref, dbs_ref, dwin_hbm, dwout_hbm, land_hbm, dwin16_hbm, dwout16_hbm,
             win_v, wout_v, dwin_acc, dwout_acc, carry, bank, dbs_acc, lwin, lwout, sem, send_sems, recv_sems,
             pair_send, pair_recv):
        i = pl.program_id(0)
        small = [dg1_ref, dsc1_ref, dsh1_ref, dgt1_ref, dcw_ref, dcb_ref, dclg_ref, dclb_ref, dvg_ref, dvb_ref,
                 dmg_ref, dws_ref, dbs_acc]

        @pl.when(i == 0)
        def _():
            for cp in _scatter_copies(dwup16_hbm, land_hbm, send_sems, recv_sems):
                cp.start()
            cps = [pltpu.make_async_copy(win_hbm, win_v, sem.at[0]),
                   pltpu.make_async_copy(wout_hbm, wout_v, sem.at[1])]
            for cp in cps:
                cp.start()
            for cp in cps:
                cp.wait()
            dwin_acc[...] = jnp.zeros_like(dwin_acc)
            dwout_acc[...] = jnp.zeros_like(dwout_acc)
            carry[...] = jnp.zeros_like(carry)
            for ref in small:
                ref[...] = jnp.zeros_like(ref)

        dx2v = dx2_ref[...]
        gt1v = gt1[...]
        dgt1_ref[...] += _colsum(dx2v * o1_ref[...])
        do1b = (gt1v * dx2v).astype(BF16)
        dy = _nt(do1b, wout_v[...])
        dwout_acc[...] += _tn(y_ref[...], do1b)

        mgv = mg[...]
        _, conv_vjp = jax.vjp(_conv_branch, a1_ref[...], clg[...], clb[...], mgv[:, :D_HALF])
        da1, dclg, dclb, dmg_a = conv_vjp(dy[:, :D_HALF])
        dclg_ref[...] += dclg
        dclb_ref[...] += dclb
        gu = z_ref[:, 2 * D_HALF:3 * D_HALF]
        gv = z_ref[:, 3 * D_HALF:]
        spv = sp_ref[...]
        _, gate_vjp = jax.vjp(_gate_branch, gu, spv, mgv[:, D_HALF:])
        dgu, dsp, dmg_g = gate_vjp(dy[:, D_HALF:])
        dmg_ref[...] += jnp.concatenate([dmg_a, dmg_g], axis=1)
        gvn, gv_vjp = jax.vjp(_gv_norm, gv, vg[...], vb[...])
        gvnb = gvn.astype(BF16)
        dspb = dsp.astype(BF16)
        dgvn = _head_pair_matmul(wpt, dspb)
        dgv, dvg, dvb = gv_vjp(dgvn)
        dvg_ref[...] += dvg
        dvb_ref[...] += dvb
        lane = lax.broadcasted_iota(jnp.int32, (CHUNK, CHUNK), 1)
        dbs = jnp.zeros((CHUNK, D_HALF), F32)
        for n in range(t // CHUNK):
            rows = slice(n * CHUNK, (n + 1) * CHUNK)
            dbs = dbs + dsp[rows, :]
            for j in range(N_HEADS // 2):
                cols = slice(j * CHUNK, (j + 1) * CHUNK)
                blk = dspb[rows, cols]
                zero = jnp.zeros_like(blk)
                vblk = gvnb[rows, cols]
                dws_ref[2 * j] += _nt(jnp.where(lane < HEAD_DIM, blk, zero), vblk)
                dws_ref[2 * j + 1] += _nt(jnp.where(lane < HEAD_DIM, zero, blk), vblk)
        dbs_acc[...] += dbs

        ca = z_ref[:, :D_HALF]
        cg = z_ref[:, D_HALF:2 * D_HALF]
        sig = jax.nn.sigmoid(cg)
        a0 = ca * sig
        ext = jnp.concatenate([da1, carry[...]], axis=0)
        carry[...] = da1[:HALO]
        bank[0] = ext
        for b in range(1, 8):
            bank[b] = pltpu.roll(ext, t + HALO - b, axis=0)
        dcb_ref[...] += _colsum(da1)
        da0 = jnp.zeros((t, D_HALF), F32)
        for s in range(CONV_K):
            q, b = divmod(s, 8)
            shifted = bank[b, pl.ds(8 * q, t), :]
            da0 = da0 + shifted * cw[pl.ds(CONV_K - 1 - s, 1), :]
            dcw_ref[pl.ds(CONV_K - 1 - s, 1), :] += _colsum(shifted * a0)
        dca = da0 * sig
        dcg = da0 * ca * sig * (1.0 - sig)

        h1, h1_vjp = jax.vjp(_mod_norm, x_ref[...], g1[...], sc1[...], sh1[...])
        h1b = h1.astype(BF16)
        dh1 = jnp.zeros((t, D_MODEL), F32)
        for k, dzk in enumerate((dca, dcg, dgu, dgv)):
            dzb = dzk.astype(BF16)
            dh1 = dh1 + _nt(dzb, win_v[k])
            dwin_acc[k] += _tn(h1b, dzb)
        dx, dg1, dsc1, dsh1 = h1_vjp(dh1)
        gx_ref[...] = dx2v + dx
        dg1_ref[...] += dg1
        dsc1_ref[...] += dsc1
        dsh1_ref[...] += dsh1

        @pl.when(i == n_tiles - 1)
        def _():
            for h in range(N_HEADS):
                dws_ref[h] = dws_ref[h] * mask_ref[...]
            head_of_lane = lax.broadcasted_iota(jnp.int32, (N_HEADS, D_HALF), 1) // HEAD_DIM
            pick = (head_of_lane == lax.broadcasted_iota(jnp.int32, (N_HEADS, D_HALF), 0)).astype(F32)
            dbs_ref[...] = lax.dot_general(pick, dbs_acc[...], NT_DIMS, precision=lax.Precision.HIGHEST,
                                           preferred_element_type=F32)
            for k in range(N_SHARD):
                win_v[k] = dwin_acc[k].astype(BF16)
            wout_v[...] = dwout_acc[...].astype(BF16)
            mx, my, mc = _coords()
            h_in, h_out = dwin_acc.shape[1] // 2, dwout_acc.shape[0] // (2 * N_SHARD)

            def in_rows(ref, k, which):
                return ref.at[k, pl.ds(pl.multiple_of(which * h_in, 16), h_in), :]

            def out_rows(ref, k, which):
                return ref.at[pl.ds(pl.multiple_of((2 * k + which) * h_out, 16), h_out), :]

            pairs = ((win_v, dwin_acc, lwin, in_rows, dwin_hbm, dwin16_hbm),
                     (wout_v, dwout_acc, lwout, out_rows, dwout_hbm, dwout16_hbm))
            swaps = [pltpu.make_async_remote_copy(
                src_ref=rows_of(v16, k, 1 - mc), dst_ref=land.at[k], send_sem=pair_send.at[w, k],
                recv_sem=pair_recv.at[w, k], device_id=(mx, my, 1 - mc), device_id_type=MESH)
                for w, (v16, _, land, rows_of, _, _) in enumerate(pairs) for k in range(N_SHARD)]
            for cp in swaps:
                cp.start()
            for cp in swaps:
                cp.wait()
            outs = []
            for w, (v16, acc, land, rows_of, half_hbm, half16_hbm) in enumerate(pairs):
                for k in range(N_SHARD):
                    total = rows_of(acc, k, mc)[...] + land[k].astype(F32)
                    rows_of(acc, k, 0)[...] = total
                    rows_of(v16, k, 0)[...] = total.astype(BF16)
                    outs.append(pltpu.make_async_copy(rows_of(acc, k, 0), half_hbm.at[k], sem.at[2 + 8 * w + k]))
                    outs.append(pltpu.make_async_copy(rows_of(v16, k, 0), half16_hbm.at[k], sem.at[6 + 8 * w + k]))
            for cp in outs:
                cp.start()
            for cp in outs:
                cp.wait()
            for rc in _scatter_copies(dwup16_hbm, land_hbm, send_sems, recv_sems):
                rc.wait()

    def rev(width):
        return pl.BlockSpec((t, width), lambda i: (n_tiles - 1 - i, 0))

    v1024 = jax.ShapeDtypeStruct((1, D_MODEL), F32)
    v512 = jax.ShapeDtypeStruct((1, D_HALF), F32)
    small_shapes = [v1024, v1024, v1024, v1024, jax.ShapeDtypeStruct((CONV_K, D_HALF), F32), v512, v512, v512, v512,
                    v512, v1024, jax.ShapeDtypeStruct((N_HEADS, CHUNK, CHUNK), F32),
                    jax.ShapeDtypeStruct((N_HEADS, CHUNK), F32)]
    n_peer = len(PEER_FLIPS)
    half_in = (N_SHARD, w_in_g.shape[1] // 2, w_in_g.shape[2])
    half_out = (N_SHARD, w_out_g.shape[0] // (2 * N_SHARD), w_out_g.shape[1])
    return pl.pallas_call(
        body, grid=(n_tiles,), name="bwd_mixer",
        in_specs=[rev(D_MODEL), rev(D_MODEL), rev(4 * D_HALF), rev(D_HALF), rev(D_HALF), rev(D_MODEL),
                  rev(D_MODEL)] + [_full(v.shape) for v in vecs]
        + [_full(conv_w.shape), _full(wpair.shape), _full(wpair_t.shape), _full(causal_mask.shape), ANY, ANY, ANY],
        out_specs=[rev(D_MODEL)] + [_full(s.shape) for s in small_shapes] + [ANY] * 5,
        out_shape=[jax.ShapeDtypeStruct((seq, D_MODEL), F32)] + small_shapes
        + [jax.ShapeDtypeStruct(half_in, F32), jax.ShapeDtypeStruct(half_out, F32), _land_shape(dwup16),
           jax.ShapeDtypeStruct(half_in, BF16), jax.ShapeDtypeStruct(half_out, BF16)],
        scratch_shapes=[pltpu.VMEM(w_in_g.shape, BF16), pltpu.VMEM(w_out_g.shape, BF16),
                        pltpu.VMEM(w_in_g.shape, F32), pltpu.VMEM(w_out_g.shape, F32),
                        pltpu.VMEM((HALO, D_HALF), F32), pltpu.VMEM((8, t + HALO, D_HALF), F32),
                        pltpu.VMEM((CHUNK, D_HALF), F32), pltpu.VMEM(half_in, BF16), pltpu.VMEM(half_out, BF16),
                        pltpu.SemaphoreType.DMA((2 + 4 * N_SHARD,)),
                        pltpu.SemaphoreType.DMA((n_peer,)), pltpu.SemaphoreType.DMA((n_peer,)),
                        pltpu.SemaphoreType.DMA((2, N_SHARD)), pltpu.SemaphoreType.DMA((2, N_SHARD))],
        compiler_params=_params(),
    )(dx2, x, z, a1, sp, yb, o1, *vecs, conv_w, wpair, wpair_t, causal_mask, w_in_g, w_out_g, dwup16)


def _gmlp_operands(gm_ws, gm_bs):
    mask = jnp.tril(jnp.ones((CHUNK, CHUNK), F32))
    ws = gm_ws * mask[None]
    wpair = ws.reshape(N_HEADS // 2, 2 * CHUNK, CHUNK).astype(BF16)
    wpair_t = jnp.swapaxes(ws, 1, 2).reshape(N_HEADS // 2, 2 * CHUNK, CHUNK).astype(BF16)
    bs_full = jnp.repeat(jnp.transpose(gm_bs), HEAD_DIM, axis=1)
    return wpair, wpair_t, bs_full, mask


def _local_step(x, target, mod, p, w_in_g, w_out_g, w_up_part, w_down_part, u_dtype=F32):
    sh1, sc1, gt1, sh2, sc2, gt2 = [mod[:, k * D_MODEL:(k + 1) * D_MODEL] for k in range(6)]
    vec = dict(p, sh1=sh1, sc1=sc1, gt1=gt1, sh2=sh2, sc2=sc2, gt2=gt2)
    wpair, wpair_t, bs_full, mask = _gmlp_operands(p["gm_ws"], p["gm_bs"])

    (z, a1, sp, yb, o1, x2), (w_up_g, w_down_g) = _fwd_mixer(
        x, vec, p["conv_dw_w"], wpair, bs_full, w_in_g, w_out_g, [w_up_part, w_down_part])
    w_down_g = w_down_g.reshape(D_FF, D_MODEL)
    to_inter, to_natural = _interleave_matrices()
    u = _fwd_up(x2, p["norm2_gain"], sc2, sh2, w_up_g, to_inter, u_dtype)
    du, dx3, d_ffn_w, d_ffn_b, d_fg, d_gt2, loss, d_wd, d_wd16 = _ffn_tail(
        u, x2, target, p["ffn_dw_w"], p["ffn_dw_b"], gt2, p["final_gain"], w_down_g, to_inter, to_natural)
    by_shard = (N_SHARD, -1, D_MODEL)
    dx2, d_g2, d_sc2, d_sh2, d_wup, d_wup16, land_wd = _bwd_up(
        du, x2, dx3, p["norm2_gain"], sc2, sh2, w_up_g, d_wd16.reshape(by_shard))
    (gx, d_g1, d_sc1, d_sh1, d_gt1, d_cw, d_cb, d_clg, d_clb, d_vg, d_vb, d_mg, d_ws, d_bs, d_win, d_wout, land_wup,
     d_win16, d_wout16) = _bwd_mixer(dx2, x, z, a1, sp, yb, o1, vec, p["conv_dw_w"], wpair, wpair_t, mask, w_in_g,
                                     w_out_g, d_wup16)
    d_mod = jnp.concatenate([d_sh1, d_sc1, d_gt1, d_sh2, d_sc2, d_gt2], axis=1)
    grads = dict(norm1_gain=d_g1, conv_dw_w=d_cw, conv_dw_b=d_cb, conv_ln_g=d_clg, conv_ln_b=d_clb, gm_ln_g=d_vg,
                 gm_ln_b=d_vb, gm_ws=d_ws, gm_bs=d_bs, mix_out_gain=d_mg, norm2_gain=d_g2, ffn_dw_w=d_ffn_w,
                 ffn_dw_b=d_ffn_b, final_gain=d_fg, w_in=d_win, w_out=d_wout, w_up=d_wup, w_down=d_wd.reshape(by_shard))
    in_flight = dict(w_in16=d_win16, w_out16=d_wout16, land_w_up=land_wup, land_w_down=land_wd)
    return gx, grads, d_mod, loss, in_flight


MESH = pl.DeviceIdType.MESH
VMEM_SPEC = pl.BlockSpec(memory_space=pltpu.VMEM)
PEER_FLIPS = [(a, b, d) for a in (0, 1) for b in (0, 1) for d in (0, 1)][1:]
CHIP_FLIPS = [(1, 0), (0, 1), (1, 1)]


def _coords():
    return lax.axis_index("x"), lax.axis_index("y"), lax.axis_index("c")


def _flip(v, bit):
    return 1 - v if bit else v


def _rows8(block):
    return pl.ds(pl.multiple_of(8 * block, 8), 8)


def _ada_mod(c_row, w_ada_sh, b_ada_sh):
    cols = w_ada_sh.shape[1]

    def body(c_ref, w_ref, b_ref, call_ref, mod_ref, cpad, modall, send_sems, recv_sems):
        x, y, c = _coords()
        me = 4 * x + 2 * y + c
        cpad[...] = jnp.zeros_like(cpad)
        cpad[pl.ds(0, 1), :] = c_ref[...]

        def gather_copy(j, flip):
            peer = (_flip(x, flip[0]), _flip(y, flip[1]), _flip(c, flip[2]))
            return pltpu.make_async_remote_copy(
                src_ref=cpad, dst_ref=call_ref.at[_rows8(me)], send_sem=send_sems.at[j], recv_sem=recv_sems.at[j],
                device_id=peer, device_id_type=MESH)

        copies = [gather_copy(j, f) for j, f in enumerate(PEER_FLIPS)]
        for cp in copies:
            cp.start()
        call_ref[_rows8(me), :] = cpad[...]
        for cp in copies:
            cp.wait_recv()
        for cp in copies:
            cp.wait_send()
        cv = call_ref[...]
        c_act = (cv * jax.nn.sigmoid(cv)).astype(BF16)
        modall[...] = _nn(c_act, w_ref[...].astype(BF16)) + b_ref[...]

        slot = _rows8(2 * x + y)

        def piece_copy(j, flip):
            tx, ty = _flip(x, flip[0]), _flip(y, flip[1])
            return pltpu.make_async_remote_copy(
                src_ref=modall.at[_rows8(4 * tx + 2 * ty + c)], dst_ref=mod_ref.at[slot],
                send_sem=send_sems.at[len(PEER_FLIPS) + j], recv_sem=recv_sems.at[len(PEER_FLIPS) + j],
                device_id=(tx, ty, c), device_id_type=MESH)

        pieces = [piece_copy(j, f) for j, f in enumerate(CHIP_FLIPS)]
        for cp in pieces:
            cp.start()
        mod_ref[slot, :] = modall[_rows8(me), :]
        for cp in pieces:
            cp.wait_recv()
        for cp in pieces:
            cp.wait_send()

    n_sem = len(PEER_FLIPS) + len(CHIP_FLIPS)
    return pl.pallas_call(
        body, name="ada_mod",
        in_specs=[VMEM_SPEC, VMEM_SPEC, VMEM_SPEC], out_specs=[VMEM_SPEC, VMEM_SPEC],
        out_shape=[jax.ShapeDtypeStruct((8 * N_DEV, D_MODEL), F32), jax.ShapeDtypeStruct((8 * N_SHARD, cols), F32)],
        scratch_shapes=[pltpu.VMEM((8, D_MODEL), F32), pltpu.VMEM((8 * N_DEV, cols), F32),
                        pltpu.SemaphoreType.DMA((n_sem,)), pltpu.SemaphoreType.DMA((n_sem,))],
        compiler_params=pltpu.CompilerParams(vmem_limit_bytes=VMEM_LIMIT_BYTES),
    )(c_row, w_ada_sh, b_ada_sh)


def _gather_weights(shards, filters, n_now):
    n = len(shards)
    nf = len(filters)

    def body(*refs):
        ins, f_ins = refs[:n], refs[n:n + nf]
        outs, f_outs = refs[n + nf:2 * n + nf], refs[2 * n + nf:2 * (n + nf)]
        stage = refs[2 * (n + nf):3 * n + 2 * nf]
        send_sems, recv_sems, local_sems, f_send_sems, f_recv_sems = refs[3 * n + 2 * nf:]
        x, y, c = _coords()
        k = 2 * x + y
        sibling = (x, y, 1 - c)

        def filter_copy(w, j, slot):
            tx, ty = _flip(x, CHIP_FLIPS[j][0]), _flip(y, CHIP_FLIPS[j][1])
            return pltpu.make_async_remote_copy(
                src_ref=f_ins[w], dst_ref=f_outs[w].at[slot], send_sem=f_send_sems.at[w, j],
                recv_sem=f_recv_sems.at[w, j], device_id=(tx, ty, c), device_id_type=MESH)

        def half(w, which):
            h = shards[w].shape[0] // 2
            return pl.ds(pl.multiple_of(which * h, 16), h)

        def ici_copy(w, j, src, slot):
            tx, ty = _flip(x, CHIP_FLIPS[j][0]), _flip(y, CHIP_FLIPS[j][1])
            return pltpu.make_async_remote_copy(
                src_ref=src, dst_ref=outs[w].at[slot, half(w, c)], send_sem=send_sems.at[w, j],
                recv_sem=recv_sems.at[w, j], device_id=(tx, ty, c), device_id_type=MESH)

        def d2d_copy(w, j, slot, which):
            rows = outs[w].at[slot, half(w, which)]
            return pltpu.make_async_remote_copy(
                src_ref=rows, dst_ref=rows, send_sem=send_sems.at[w, len(CHIP_FLIPS) + j],
                recv_sem=recv_sems.at[w, len(CHIP_FLIPS) + j], device_id=sibling, device_id_type=MESH)

        def chip_of(j):
            return 2 * _flip(x, CHIP_FLIPS[j][0]) + _flip(y, CHIP_FLIPS[j][1])

        local, first, passed = [], [], []
        for w in range(nf):
            local.append(pltpu.make_async_copy(f_ins[w], f_outs[w].at[k], local_sems.at[n + w]))
            local[-1].start()
            for j in range(len(CHIP_FLIPS)):
                first.append(filter_copy(w, j, k))
                first[-1].start()
        for w in range(n):
            stage[w][...] = ins[w][...].astype(BF16)
            local.append(pltpu.make_async_copy(stage[w], outs[w].at[k], local_sems.at[w]))
            local[-1].start()
            if w < n_now:
                for j in range(len(CHIP_FLIPS)):
                    first.append(ici_copy(w, j, stage[w].at[half(w, c)], k))
                    first[-1].start()
        for w in range(nf):
            for j in range(len(CHIP_FLIPS)):
                filter_copy(w, j, chip_of(j)).wait_recv()
        for w in range(n_now):
            for j in range(len(CHIP_FLIPS)):
                ici_copy(w, j, stage[w].at[half(w, c)], chip_of(j)).wait_recv()
                passed.append(d2d_copy(w, j, chip_of(j), c))
                passed[-1].start()
        for w in range(n_now):
            for j in range(len(CHIP_FLIPS)):
                d2d_copy(w, j, chip_of(j), 1 - c).wait_recv()
        for cp in first + passed:
            cp.wait_send()
        for cp in local:
            cp.wait()

    sem_shape = (n_now, 2 * len(CHIP_FLIPS))
    f_sem_shape = (nf, len(CHIP_FLIPS))
    outs = pl.pallas_call(
        body, name="gather_weights",
        in_specs=[VMEM_SPEC] * (n + nf), out_specs=[ANY] * (n + nf),
        out_shape=[jax.ShapeDtypeStruct((N_SHARD,) + s.shape, BF16) for s in shards]
        + [jax.ShapeDtypeStruct((N_SHARD,) + s.shape, F32) for s in filters],
        scratch_shapes=[pltpu.VMEM(s.shape, BF16) for s in shards]
        + [pltpu.SemaphoreType.DMA(sem_shape), pltpu.SemaphoreType.DMA(sem_shape), pltpu.SemaphoreType.DMA((n + nf,)),
           pltpu.SemaphoreType.DMA(f_sem_shape), pltpu.SemaphoreType.DMA(f_sem_shape)],
        compiler_params=pltpu.CompilerParams(vmem_limit_bytes=VMEM_LIMIT_BYTES),
    )(*shards, *filters)
    return outs[:n], outs[n:]


def _final_comm(srcs16, small):
    n = len(srcs16)
    rows = small.shape[0]
    half = rows // 2

    def body(*refs):
        srcs, small_ref = refs[:n], refs[n]
        lands, small_out = refs[n + 1:2 * n + 1], refs[2 * n + 1]
        chip_sum, got_c, got_x, got_y, part_x, send_sems, recv_sems, small_send_sems, small_recv_sems = refs[2 * n + 2:]
        x, y, c = _coords()
        sibling = (x, y, 1 - c)
        mine = pl.ds(pl.multiple_of(c * half, 8), half)
        copies = []
        for w in range(n):
            for j, flip in enumerate(CHIP_FLIPS):
                tx, ty = _flip(x, flip[0]), _flip(y, flip[1])
                copies.append(pltpu.make_async_remote_copy(
                    src_ref=srcs[w].at[2 * tx + ty], dst_ref=lands[w].at[j], send_sem=send_sems.at[w, j],
                    recv_sem=recv_sems.at[w, j], device_id=(tx, ty, c), device_id_type=MESH))
        for cp in copies:
            cp.start()

        def exchange(stage, src, dst, peer):
            rc = pltpu.make_async_remote_copy(
                src_ref=src, dst_ref=dst, send_sem=small_send_sems.at[stage], recv_sem=small_recv_sems.at[stage],
                device_id=peer, device_id_type=MESH)
            rc.start()
            rc.wait()

        exchange(0, small_ref, got_c, sibling)
        chip_sum[...] = small_ref[...] + got_c[...]
        exchange(1, chip_sum.at[mine], got_x, (1 - x, y, c))
        part_x[...] = chip_sum[mine, :] + got_x[...]
        exchange(2, part_x, got_y, (x, 1 - y, c))
        small_out[mine, :] = part_x[...] + got_y[...]
        exchange(3, small_out.at[mine], small_out.at[mine], sibling)
        for cp in copies:
            cp.wait()

    n_chip = len(CHIP_FLIPS)
    half_shape = (half, small.shape[1])
    outs = pl.pallas_call(
        body, name="final_comm",
        in_specs=[ANY] * n + [VMEM_SPEC], out_specs=[ANY] * n + [VMEM_SPEC],
        out_shape=[jax.ShapeDtypeStruct((n_chip,) + a.shape[1:], BF16) for a in srcs16]
        + [jax.ShapeDtypeStruct(small.shape, F32)],
        scratch_shapes=[pltpu.VMEM(small.shape, F32), pltpu.VMEM(small.shape, F32), pltpu.VMEM(half_shape, F32),
                        pltpu.VMEM(half_shape, F32), pltpu.VMEM(half_shape, F32),
                        pltpu.SemaphoreType.DMA((n, n_chip)), pltpu.SemaphoreType.DMA((n, n_chip)),
                        pltpu.SemaphoreType.DMA((4,)), pltpu.SemaphoreType.DMA((4,))],
        compiler_params=pltpu.CompilerParams(vmem_limit_bytes=VMEM_LIMIT_BYTES),
    )(*srcs16, small)
    return outs[:n], outs[n]


ADD_CHUNKS = 4


def _scatter_sum(pos, owns, lands):
    n = len(owns)

    def specs(own_shape, land_shape):
        peers, rows, cols = land_shape
        pick = 1 if own_shape[1] == 2 * rows else 0
        if cols % (128 * ADD_CHUNKS) == 0:
            blk = (rows, cols // ADD_CHUNKS)
            return (pl.BlockSpec((1,) + blk, lambda i, p: (2 * p[0] + p[1], pick * p[2], i)),
                    pl.BlockSpec((peers,) + blk, lambda i, p: (0, 0, i)),
                    pl.BlockSpec((1,) + blk, lambda i, p: (p[2], 0, i)))
        blk = (rows // ADD_CHUNKS, cols)
        return (pl.BlockSpec((1,) + blk, lambda i, p: (2 * p[0] + p[1], pick * p[2] * ADD_CHUNKS + i, 0)),
                pl.BlockSpec((peers,) + blk, lambda i, p: (0, i, 0)),
                pl.BlockSpec((1,) + blk, lambda i, p: (p[2], i, 0)))

    def body(pos_ref, *refs):
        for idx in range(n):
            own, land, out = refs[idx], refs[n + idx], refs[2 * n + idx]
            total = own[0]
            for f in range(land.shape[0]):
                total = total + land[f].astype(F32)
            out[0] = total

    all_specs = [specs(o.shape, l.shape) for o, l in zip(owns, lands)]
    return pl.pallas_call(
        body, name="scatter_sum",
        grid_spec=pltpu.PrefetchScalarGridSpec(
            num_scalar_prefetch=1, grid=(ADD_CHUNKS,),
            in_specs=[s[0] for s in all_specs] + [s[1] for s in all_specs], out_specs=[s[2] for s in all_specs]),
        out_shape=[jax.ShapeDtypeStruct((2,) + l.shape[1:], F32) for l in lands],
        compiler_params=_params(),
    )(pos, *owns, *lands)


def _swap_halves(halves):
    n = len(halves)

    def body(*refs):
        ins, outs = refs[:n], refs[n:2 * n]
        send_sems, recv_sems = refs[2 * n:]
        x, y, c = _coords()
        copies = [pltpu.make_async_remote_copy(
            src_ref=ins[idx].at[pl.ds(c, 1)], dst_ref=outs[idx].at[pl.ds(c, 1)], send_sem=send_sems.at[idx],
            recv_sem=recv_sems.at[idx], device_id=(x, y, 1 - c), device_id_type=MESH) for idx in range(n)]
        for cp in copies:
            cp.start()
        for cp in copies:
            cp.wait()

    return pl.pallas_call(
        body, name="swap_halves",
        in_specs=[ANY] * n, out_specs=[ANY] * n, input_output_aliases={idx: idx for idx in range(n)},
        out_shape=[jax.ShapeDtypeStruct(a.shape, F32) for a in halves],
        scratch_shapes=[pltpu.SemaphoreType.DMA((n,)), pltpu.SemaphoreType.DMA((n,))],
    )(*halves)


def _adamw_math(w, g, m, v):
    m = ADAM_B1 * m + (1.0 - ADAM_B1) * g
    v = ADAM_B2 * v + (1.0 - ADAM_B2) * jnp.square(g)
    m_hat = m / (1.0 - ADAM_B1 ** ADAM_STEP)
    v_hat = v / (1.0 - ADAM_B2 ** ADAM_STEP)
    delta = -ADAM_LR * (m_hat / (jnp.sqrt(v_hat) + ADAM_EPS) + ADAM_WD * w)
    return delta, m, v


def _adamw(name, w, g, m, v, block_rows):
    rows, cols = w.shape

    def body(w_ref, g_ref, m_ref, v_ref, d_out, m_out, v_out):
        d_out[...], m_out[...], v_out[...] = _adamw_math(w_ref[...], g_ref[...], m_ref[...], v_ref[...])

    spec = pl.BlockSpec((block_rows, cols), lambda i: (i, 0))
    shape = jax.ShapeDtypeStruct((rows, cols), F32)
    return pl.pallas_call(
        body, grid=(rows // block_rows,), name=name, in_specs=[spec] * 4, out_specs=[spec] * 3,
        out_shape=[shape] * 3, compiler_params=_params(),
    )(w, g, m, v)


def _adamw_ada(c_all16, dmod16, w, m, v, block_rows):
    rows, cols = w.shape

    def body(c_ref, dm_ref, w_ref, m_ref, v_ref, g_out, d_out, m_out, v_out):
        cv = c_ref[...]
        g = _tn((cv * jax.nn.sigmoid(cv)).astype(BF16), dm_ref[...].astype(BF16))
        g_out[...] = g
        d_out[...], m_out[...], v_out[...] = _adamw_math(w_ref[...], g, m_ref[...], v_ref[...])

    spec = pl.BlockSpec((block_rows, cols), lambda i: (i, 0))
    shape = jax.ShapeDtypeStruct((rows, cols), F32)
    return pl.pallas_call(
        body, grid=(rows // block_rows,), name="adamw_w_ada",
        in_specs=[pl.BlockSpec((16, block_rows), lambda i: (0, i)), _full(dmod16.shape), spec, spec, spec],
        out_specs=[spec] * 4, out_shape=[shape] * 4, compiler_params=_params(),
    )(c_all16, dmod16, w, m, v)


SMALL_REPLICATED = ["b_ada", "norm1_gain", "conv_dw_b", "conv_ln_g", "conv_ln_b", "gm_ln_g", "gm_ln_b", "gm_ws", "gm_bs",
                    "mix_out_gain", "norm2_gain", "ffn_dw_b", "final_gain"]
SMALL_SHARDED = ["conv_dw_w", "ffn_dw_w"]
PACK_ROWS = 256
ADAM_PACK_ROWS = 160
WEIGHT_ORDER = ["w_ada", "b_ada", "norm1_gain", "w_in", "conv_dw_w", "conv_dw_b", "conv_ln_g", "conv_ln_b", "gm_ln_g",
                "gm_ln_b", "gm_ws", "gm_bs", "mix_out_gain", "w_out", "norm2_gain", "w_up", "ffn_dw_w", "ffn_dw_b",
                "w_down", "final_gain"]


def _pack(parts, rows):
    flat = jnp.concatenate([a.reshape(-1) for a in parts])
    return jnp.pad(flat, (0, rows * D_MODEL - flat.shape[0])).reshape(rows, D_MODEL)


def _unpack(packed, shapes):
    flat = packed.reshape(-1)
    out, pos = [], 0
    for s in shapes:
        size = 1
        for d in s:
            size *= d
        out.append(flat[pos:pos + size].reshape(s))
        pos += size
    return out


def kernel(x, c, w_ada, b_ada, norm1_gain, w_in, conv_dw_w, conv_dw_b, conv_ln_g, conv_ln_b, gm_ln_g, gm_ln_b, gm_ws, gm_bs, mix_out_gain, w_out, norm2_gain, w_up, ffn_dw_w, ffn_dw_b, w_down, final_gain, loss_target, m_w_ada, m_b_ada, m_norm1_gain, m_w_in, m_conv_dw_w, m_conv_dw_b, m_conv_ln_g, m_conv_ln_b, m_gm_ln_g, m_gm_ln_b, m_gm_ws, m_gm_bs, m_mix_out_gain, m_w_out, m_norm2_gain, m_w_up, m_ffn_dw_w, m_ffn_dw_b, m_w_down, m_final_gain, v_w_ada, v_b_ada, v_norm1_gain, v_w_in, v_conv_dw_w, v_conv_dw_b, v_conv_ln_g, v_conv_ln_b, v_gm_ln_g, v_gm_ln_b, v_gm_ws, v_gm_bs, v_mix_out_gain, v_w_out, v_norm2_gain, v_w_up, v_ffn_dw_w, v_ffn_dw_b, v_w_down, v_final_gain):
    weights = dict(w_ada=w_ada, b_ada=b_ada, norm1_gain=norm1_gain, w_in=w_in, conv_dw_w=conv_dw_w, conv_dw_b=conv_dw_b,
                   conv_ln_g=conv_ln_g, conv_ln_b=conv_ln_b, gm_ln_g=gm_ln_g, gm_ln_b=gm_ln_b, gm_ws=gm_ws, gm_bs=gm_bs,
                   mix_out_gain=mix_out_gain, w_out=w_out, norm2_gain=norm2_gain, w_up=w_up, ffn_dw_w=ffn_dw_w,
                   ffn_dw_b=ffn_dw_b, w_down=w_down, final_gain=final_gain)
    mom1 = dict(w_ada=m_w_ada, b_ada=m_b_ada, norm1_gain=m_norm1_gain, w_in=m_w_in, conv_dw_w=m_conv_dw_w,
                conv_dw_b=m_conv_dw_b, conv_ln_g=m_conv_ln_g, conv_ln_b=m_conv_ln_b, gm_ln_g=m_gm_ln_g, gm_ln_b=m_gm_ln_b,
                gm_ws=m_gm_ws, gm_bs=m_gm_bs, mix_out_gain=m_mix_out_gain, w_out=m_w_out, norm2_gain=m_norm2_gain,
                w_up=m_w_up, ffn_dw_w=m_ffn_dw_w, ffn_dw_b=m_ffn_dw_b, w_down=m_w_down, final_gain=m_final_gain)
    mom2 = dict(w_ada=v_w_ada, b_ada=v_b_ada, norm1_gain=v_norm1_gain, w_in=v_w_in, conv_dw_w=v_conv_dw_w,
                conv_dw_b=v_conv_dw_b, conv_ln_g=v_conv_ln_g, conv_ln_b=v_conv_ln_b, gm_ln_g=v_gm_ln_g, gm_ln_b=v_gm_ln_b,
                gm_ws=v_gm_ws, gm_bs=v_gm_bs, mix_out_gain=v_mix_out_gain, w_out=v_w_out, norm2_gain=v_norm2_gain,
                w_up=v_w_up, ffn_dw_w=v_ffn_dw_w, ffn_dw_b=v_ffn_dw_b, w_down=v_w_down, final_gain=v_final_gain)
    shard = 2 * lax.axis_index("x") + lax.axis_index("y")
    me = 2 * shard + lax.axis_index("c")

    ada_cols = w_ada.shape[2]
    b_ada_sh = lax.dynamic_slice(b_ada, (0, shard * ada_cols), (1, ada_cols))
    c_all64, mod32 = _ada_mod(c, w_ada[0], b_ada_sh)
    c_all = c_all64[::8]
    mod = mod32[::8].reshape(1, N_SHARD * ada_cols)

    (w_in_g, w_out_g, w_up_part, w_down_part), (conv_w_g, ffn_w_g) = _gather_weights(
        [w_in[0], w_out[0], w_up[0], w_down[0]], [conv_dw_w[0], ffn_dw_w[0]], n_now=2)
    conv_w_full = jnp.transpose(conv_w_g, (1, 0, 2)).reshape(CONV_K, D_HALF)
    ffn_w_full = jnp.transpose(ffn_w_g, (1, 0, 2)).reshape(FFN_K, 2 * D_FF)

    p = dict(norm1_gain=norm1_gain, conv_dw_w=conv_w_full, conv_dw_b=conv_dw_b, conv_ln_g=conv_ln_g,
             conv_ln_b=conv_ln_b, gm_ln_g=gm_ln_g, gm_ln_b=gm_ln_b, gm_ws=gm_ws[0], gm_bs=gm_bs[0],
             mix_out_gain=mix_out_gain, norm2_gain=norm2_gain, ffn_dw_w=ffn_w_full, ffn_dw_b=ffn_dw_b,
             final_gain=final_gain[None])
    grad_x, g, d_mod, loss, in_flight = _local_step(
        x[0], loss_target[0], mod, p, w_in_g, w_out_g.reshape(D_MODEL, D_MODEL), w_up_part, w_down_part)

    n_mod = d_mod.shape[1]
    dmod_rows = lax.dynamic_update_slice(jnp.zeros((N_DEV, n_mod), F32), d_mod, (me, 0))
    g["b_ada"] = d_mod
    small = _pack([g[k] for k in SMALL_REPLICATED] + [g[k] for k in SMALL_SHARDED] + [dmod_rows, loss[0, :1]], PACK_ROWS)
    (land_w_in, land_w_out), small = _final_comm([in_flight["w_in16"], in_flight["w_out16"]], small)
    pos = jnp.stack(_coords()).astype(jnp.int32)
    halves = _scatter_sum(pos, [g["w_in"], g["w_out"], g["w_up"], g["w_down"]],
                          [land_w_in, land_w_out, in_flight["land_w_up"], in_flight["land_w_down"]])
    full = _swap_halves(halves)
    grads = dict(w_in=full[0].reshape(w_in.shape[1:]), w_out=full[1].reshape(w_out.shape[1:]),
                 w_up=full[2].reshape(w_up.shape[1:]), w_down=full[3].reshape(w_down.shape[1:]))

    small_shapes = ([weights[k].shape for k in SMALL_REPLICATED] + [(CONV_K, D_HALF), (FFN_K, 2 * D_FF)]
                    + [(N_DEV, n_mod), (1,)])
    *small_grads, conv_w_grad, ffn_w_grad, dmod_all, loss_sum = _unpack(small, small_shapes)
    grads.update(zip(SMALL_REPLICATED, small_grads))
    grads["conv_dw_w"] = lax.dynamic_slice(conv_w_grad, (0, shard * conv_dw_w.shape[2]), conv_dw_w.shape[1:])[None]
    grads["ffn_dw_w"] = lax.dynamic_slice(ffn_w_grad, (0, shard * ffn_dw_w.shape[2]), ffn_dw_w.shape[1:])[None]

    delta, new_m, new_v = {}, {}, {}
    for name, block_rows in (("w_in", 256), ("w_out", 128), ("w_up", 256), ("w_down", 352)):
        delta[name], new_m[name], new_v[name] = [a[None] for a in _adamw(
            "adamw_" + name, weights[name][0], grads[name], mom1[name][0], mom2[name][0], block_rows)]
        grads[name] = grads[name][None]
    dmod_sh = lax.dynamic_slice(dmod_all, (0, shard * ada_cols), (N_DEV, ada_cols))
    pad8 = ((0, 16 - N_DEV), (0, 0))
    grads["w_ada"], delta["w_ada"], new_m["w_ada"], new_v["w_ada"] = [a[None] for a in _adamw_ada(
        jnp.pad(c_all, pad8), jnp.pad(dmod_sh, pad8), w_ada[0], m_w_ada[0], v_w_ada[0], 256)]
    small_names = SMALL_REPLICATED + SMALL_SHARDED
    packed = [_pack([d[k] for k in small_names], ADAM_PACK_ROWS) for d in (weights, grads, mom1, mom2)]
    small_out = _adamw("adamw_small", *packed, ADAM_PACK_ROWS)
    for d, arr in zip((delta, new_m, new_v), small_out):
        d.update(zip(small_names, _unpack(arr, [weights[k].shape for k in small_names])))

    return (loss_sum.reshape(()), grad_x[None], *[grads[k] for k in WEIGHT_ORDER], *[delta[k] for k in WEIGHT_ORDER],
            *[new_m[k] for k in WEIGHT_ORDER], *[new_v[k] for k in WEIGHT_ORDER])
```

```python
import functools

import jax
import jax.numpy as jnp
from jax import lax
from jax.experimental import pallas as pl
from jax.experimental.pallas import tpu as pltpu

F32 = jnp.float32
BF16 = jnp.bfloat16

D_MODEL = 1024
D_HALF = 512
D_FF = 2816
CONV_K = 31
FFN_K = 3
CHUNK = 128
N_HEADS = 8
HEAD_DIM = 64
N_SHARD = 4
N_DEV = 8
RMS_EPS = 1e-6
LN_EPS = 1e-5
ADAM_LR, ADAM_B1, ADAM_B2, ADAM_EPS, ADAM_WD, ADAM_STEP = 0.001, 0.9, 0.999, 1e-08, 0.01, 10

TILE = 256
HALO = 32
FFN_HALO = 16
FFN_BLK = 256
UP_SHARD = 2 * D_FF // N_SHARD
VMEM_LIMIT_BYTES = 56 * 1024 * 1024
FFN_VMEM_LIMIT_BYTES = 58 * 1024 * 1024

ANY = pl.BlockSpec(memory_space=pl.ANY)
NT_DIMS = (((1,), (1,)), ((), ()))
TN_DIMS = (((0,), (0,)), ((), ()))


def _full(shape):
    return pl.BlockSpec(shape, lambda i: (0,) * len(shape))


def _nn(a, b):
    return jnp.dot(a, b, preferred_element_type=F32)


def _nt(a, b):
    return lax.dot_general(a, b, NT_DIMS, preferred_element_type=F32)


def _tn(a, b):
    return lax.dot_general(a, b, TN_DIMS, preferred_element_type=F32)


def _colsum(a):
    return jnp.sum(a, axis=0, keepdims=True)


def _params(semantics=("arbitrary",)):
    return pltpu.CompilerParams(dimension_semantics=semantics, vmem_limit_bytes=VMEM_LIMIT_BYTES)


def _rms(v, gain):
    return v * lax.rsqrt(jnp.mean(v * v, axis=-1, keepdims=True) + RMS_EPS) * gain


def _layer_norm(v, gain, bias):
    mu = jnp.mean(v, axis=-1, keepdims=True)
    var = jnp.mean(jnp.square(v - mu), axis=-1, keepdims=True)
    return (v - mu) * lax.rsqrt(var + LN_EPS) * gain + bias


def _mod_norm(v, gain, scale, shift):
    return _rms(v, gain) * (1.0 + scale) + shift


def _conv_branch(a1, ln_g, ln_b, out_gain):
    a2 = _layer_norm(a1, ln_g, ln_b)
    return _rms(a2 * jax.nn.sigmoid(a2), out_gain)


def _gate_branch(gu, sp, out_gain):
    return _rms(jax.nn.gelu(gu) * sp, out_gain)


def _gv_norm(gv, ln_g, ln_b):
    return _layer_norm(jax.nn.gelu(gv), ln_g, ln_b)


def _head_pair_matmul(wp_ref, v):
    lane = lax.broadcasted_iota(jnp.int32, (CHUNK, CHUNK), 1)
    rows = []
    for n in range(v.shape[0] // CHUNK):
        cols = []
        for j in range(N_HEADS // 2):
            r = _nn(wp_ref[j], v[n * CHUNK:(n + 1) * CHUNK, j * CHUNK:(j + 1) * CHUNK])
            cols.append(jnp.where(lane < HEAD_DIM, r[:CHUNK], r[CHUNK:]))
        rows.append(jnp.concatenate(cols, axis=1))
    return jnp.concatenate(rows, axis=0)


def _tile_bias(bs, tokens):
    return jnp.concatenate([bs] * (tokens // CHUNK), axis=0)


FORWARD_LEAD = 8


def _fwd_mixer(x, vec, conv_w, wpair, bs_full, w_in_g, w_out_g, late_parts):
    seq = x.shape[0]
    n_tiles = seq // TILE
    t = TILE
    n_late = len(late_parts)
    forward_step = max(n_tiles - FORWARD_LEAD, 0)
    names = ["norm1_gain", "sc1", "sh1", "gt1", "conv_dw_b", "conv_ln_g", "conv_ln_b", "gm_ln_g", "gm_ln_b",
             "mix_out_gain"]
    vecs = [vec[k] for k in names]

    def body(x_ref, g1, sc1, sh1, gt1, cb, clg, clb, vg, vb, mg, cw, wp, bs, win_hbm, wout_hbm, *rest):
        late = rest[n_late:2 * n_late]
        z_ref, a1_ref, sp_ref, y_ref, o1_ref, x2_ref = rest[2 * n_late:2 * n_late + 6]
        win_v, wout_v, halo, bank, sem, send_sems, recv_sems = rest[2 * n_late + 6:]
        i = pl.program_id(0)
        mx, my, mc = _coords()
        shard = 2 * mx + my

        def half(w, which):
            h = late[w].shape[1] // 2
            return pl.ds(pl.multiple_of(which * h, 16), h)

        def chip_of(j):
            return 2 * _flip(mx, CHIP_FLIPS[j][0]) + _flip(my, CHIP_FLIPS[j][1])

        def ici_copy(w, j, slot):
            rows = late[w].at[slot, half(w, mc)]
            return pltpu.make_async_remote_copy(
                src_ref=rows, dst_ref=rows, send_sem=send_sems.at[w, j], recv_sem=recv_sems.at[w, j],
                device_id=(_flip(mx, CHIP_FLIPS[j][0]), _flip(my, CHIP_FLIPS[j][1]), mc), device_id_type=MESH)

        def d2d_copy(w, j, which):
            rows = late[w].at[chip_of(j), half(w, which)]
            return pltpu.make_async_remote_copy(
                src_ref=rows, dst_ref=rows, send_sem=send_sems.at[w, len(CHIP_FLIPS) + j],
                recv_sem=recv_sems.at[w, len(CHIP_FLIPS) + j], device_id=(mx, my, 1 - mc), device_id_type=MESH)

        pairs = [(w, j) for w in range(n_late) for j in range(len(CHIP_FLIPS))]

        @pl.when(i == 0)
        def _():
            for w, j in pairs:
                ici_copy(w, j, shard).start()
            cps = [pltpu.make_async_copy(win_hbm, win_v, sem.at[0]),
                   pltpu.make_async_copy(wout_hbm, wout_v, sem.at[1])]
            for cp in cps:
                cp.start()
            for cp in cps:
                cp.wait()
            halo[...] = jnp.zeros_like(halo)

        @pl.when(i == forward_step)
        def _():
            for w, j in pairs:
                ici_copy(w, j, chip_of(j)).wait_recv()
                d2d_copy(w, j, mc).start()

        xv = x_ref[...]
        h1b = _mod_norm(xv, g1[...], sc1[...], sh1[...]).astype(BF16)
        zs = [_nn(h1b, win_v[k]) for k in range(N_SHARD)]
        for k in range(N_SHARD):
            z_ref[:, k * D_HALF:(k + 1) * D_HALF] = zs[k]
        ca, cg, gu, gv = zs
        a0 = ca * jax.nn.sigmoid(cg)
        ext = jnp.concatenate([halo[...], a0], axis=0)
        halo[...] = a0[t - HALO:]
        bank[0] = ext
        for b in range(1, 8):
            bank[b] = pltpu.roll(ext, b, axis=0)
        a1 = jnp.zeros((t, D_HALF), F32) + cb[...]
        for s in range(CONV_K):
            q, b = divmod(s, 8)
            a1 = a1 + bank[b, pl.ds(HALO - 8 * q, t), :] * cw[pl.ds(CONV_K - 1 - s, 1), :]
        a1_ref[...] = a1
        mgv = mg[...]
        ya = _conv_branch(a1, clg[...], clb[...], mgv[:, :D_HALF])
        gvn = _gv_norm(gv, vg[...], vb[...]).astype(BF16)
        sp = _head_pair_matmul(wp, gvn) + _tile_bias(bs[...], t)
        sp_ref[...] = sp
        yg = _gate_branch(gu, sp, mgv[:, D_HALF:])
        yb = jnp.concatenate([ya, yg], axis=1).astype(BF16)
        y_ref[...] = yb
        o1 = _nn(yb, wout_v[...])
        o1_ref[...] = o1
        x2_ref[...] = xv + gt1[...] * o1

        @pl.when(i == n_tiles - 1)
        def _():
            for w, j in pairs:
                d2d_copy(w, j, 1 - mc).wait_recv()
            for w, j in pairs:
                ici_copy(w, j, shard).wait_send()
                d2d_copy(w, j, mc).wait_send()

    def row(width):
        return pl.BlockSpec((t, width), lambda i: (i, 0))

    out_shape = [jax.ShapeDtypeStruct((seq, 4 * D_HALF), F32), jax.ShapeDtypeStruct((seq, D_HALF), F32),
                 jax.ShapeDtypeStruct((seq, D_HALF), F32), jax.ShapeDtypeStruct((seq, D_MODEL), BF16),
                 jax.ShapeDtypeStruct((seq, D_MODEL), F32), jax.ShapeDtypeStruct((seq, D_MODEL), F32)]
    n_in = 1 + len(vecs) + 3 + 2
    sem_shape = (n_late, 2 * len(CHIP_FLIPS))
    outs = pl.pallas_call(
        body, grid=(n_tiles,), name="fwd_mixer",
        in_specs=[row(D_MODEL)] + [_full(v.shape) for v in vecs]
        + [_full(conv_w.shape), _full(wpair.shape), _full(bs_full.shape), ANY, ANY] + [ANY] * n_late,
        out_specs=[ANY] * n_late + [row(4 * D_HALF), row(D_HALF), row(D_HALF), row(D_MODEL), row(D_MODEL),
                                    row(D_MODEL)],
        out_shape=[jax.ShapeDtypeStruct(a.shape, a.dtype) for a in late_parts] + out_shape,
        input_output_aliases={n_in + w: w for w in range(n_late)},
        scratch_shapes=[pltpu.VMEM(w_in_g.shape, BF16), pltpu.VMEM(w_out_g.shape, BF16),
                        pltpu.VMEM((HALO, D_HALF), F32), pltpu.VMEM((8, t + HALO, D_HALF), F32),
                        pltpu.SemaphoreType.DMA((2,)), pltpu.SemaphoreType.DMA(sem_shape),
                        pltpu.SemaphoreType.DMA(sem_shape)],
        compiler_params=_params(),
    )(x, *vecs, conv_w, wpair, bs_full, w_in_g, w_out_g, *late_parts)
    return outs[n_late:], outs[:n_late]


def _interleave_matrices():
    row = jnp.arange(TILE)
    token_of_row = (row % 8) * (TILE // 8) + row // 8
    to_inter = (token_of_row[:, None] == row[None, :]).astype(BF16)
    return to_inter, jnp.transpose(to_inter)


def _ffn(x2, target, norm2_gain, sc2, sh2, ffn_w, ffn_b, gt2, final_gain, w_up_g, w_down_g, to_inter, to_natural):
    seq = x2.shape[0]
    n_tiles = seq // TILE
    t = TILE
    n_blk = D_FF // FFN_BLK
    inv_d = 1.0 / D_MODEL

    def final_norm(x3, gain):
        return _rms(x3, gain)

    def body(x2_ref, x2h_ref, tgt_ref, g2, sc2_ref, sh2_ref, fw, fb, gt2_ref, fg, pm_ref, pmt_ref, wup_hbm, wd_hbm,
             du_ref, dx3_ref, dfw_ref, dfb_ref, dfg_ref, dgt2_ref, loss_ref, dwd_hbm, dwd16_hbm,
             wup_v, wd_v, dwd_acc, carry, u_s, sil_s, vds_s, f_s, du_s, sem):
        i = pl.program_id(0)
        tile = n_tiles - 1 - i
        sublane = lax.broadcasted_iota(jnp.int32, (8, FFN_BLK), 0)

        @pl.when(i == 0)
        def _():
            cps = [pltpu.make_async_copy(wd_hbm, wd_v, sem.at[0])]
            cps += [pltpu.make_async_copy(wup_hbm.at[k], wup_v.at[:, pl.ds(k * UP_SHARD, UP_SHARD)], sem.at[3 + k])
                    for k in range(N_SHARD)]
            for cp in cps:
                cp.start()
            for cp in cps:
                cp.wait()
            dwd_acc[...] = jnp.zeros_like(dwd_acc)
            carry[...] = jnp.zeros_like(carry)
            dfw_ref[...] = jnp.zeros_like(dfw_ref)
            dfb_ref[...] = jnp.zeros_like(dfb_ref)
            dfg_ref[...] = jnp.zeros_like(dfg_ref)
            dgt2_ref[...] = jnp.zeros_like(dgt2_ref)
            loss_ref[...] = jnp.zeros_like(loss_ref)

        def cols_of(j):
            return pl.ds(j * FFN_BLK, FFN_BLK), pl.ds(D_FF + j * FFN_BLK, FFN_BLK)

        def wrap_down(last, before):
            return jnp.where(sublane == 0, pltpu.roll(before, 1, axis=0), pltpu.roll(last, 1, axis=0))

        def wrap_up(first, after):
            return jnp.where(sublane == 7, pltpu.roll(after, 7, axis=0), pltpu.roll(first, 7, axis=0))

        x2v = x2_ref[...]
        h2b = _mod_norm(x2v, g2[...], sc2_ref[...], sh2_ref[...]).astype(BF16)
        h2_before = _mod_norm(x2h_ref[...], g2[...], sc2_ref[...], sh2_ref[...]).astype(BF16)
        lhs = jnp.concatenate([_nn(pm_ref[...], h2b).astype(BF16), h2_before], axis=0)

        def up(j):
            cv, cg = cols_of(j)
            return _nn(lhs, wup_v[:, cv]), _nn(lhs, wup_v[:, cg])

        def conv(both, cols):
            cur = both[:t]
            u_s[:, cols] = cur.astype(BF16)
            before = jnp.where(tile > 0, both[t:], 0.0)
            w1 = wrap_down(cur[t - 8:], before)
            w2 = wrap_down(cur[t - 16:t - 8], pltpu.roll(before, 1, axis=0))
            back1 = jnp.concatenate([w1, cur[:t - 8]], axis=0)
            back2 = jnp.concatenate([w2, w1, cur[:t - 16]], axis=0)
            return (fb[:, cols] + cur * fw[pl.ds(2, 1), cols] + back1 * fw[pl.ds(1, 1), cols]
                    + back2 * fw[pl.ds(0, 1), cols])

        pm_t = pmt_ref[...]
        o2 = jnp.zeros((t, D_MODEL), F32)
        ahead_uv = up(0)
        for j in range(n_blk):
            cv, cg = cols_of(j)
            both_v, both_g = ahead_uv
            if j + 1 < n_blk:
                ahead_uv = up(j + 1)
            val, gate = conv(both_v, cv), conv(both_g, cg)
            sig = jax.nn.sigmoid(gate)
            sil = gate * sig
            fb16 = (sil * val).astype(BF16)
            sil_s[:, cv] = sil
            vds_s[:, cv] = val * (sig + sil * (1.0 - sig))
            f_s[:, cv] = fb16
            o2 = o2 + _nn(fb16, wd_v[pl.ds(j * FFN_BLK, FFN_BLK), :])
        hi = o2.astype(BF16)
        rest = o2 - hi.astype(F32)
        mid = rest.astype(BF16)
        low = (rest - mid.astype(F32)).astype(BF16)
        o2 = _nn(jnp.concatenate([pm_t, pm_t, pm_t], axis=1), jnp.concatenate([hi, mid, low], axis=0))

        gt2v = gt2_ref[...]
        x3 = x2v + gt2v * o2
        out, out_vjp = jax.vjp(final_norm, x3, fg[...])
        diff = out - tgt_ref[...]
        loss_ref[...] += jnp.zeros_like(loss_ref) + 0.5 * inv_d * jnp.sum(diff * diff)
        dx3, dfg = out_vjp(diff * inv_d)
        dfg_ref[...] += dfg
        dgt2_ref[...] += _colsum(dx3 * o2)
        dx3_ref[...] = dx3
        do2b = _nn(pm_ref[...], (gt2v * dx3).astype(BF16)).astype(BF16)

        for j in range(n_blk):
            cv, cg = cols_of(j)
            rows = pl.ds(j * FFN_BLK, FFN_BLK)
            df = _nt(do2b, wd_v[rows, :])
            dwd_acc[rows, :] += _tn(f_s[:, cv], do2b)
            for dd, cols in ((df * sil_s[:, cv], cv), (df * vds_s[:, cv], cg)):
                dfb_ref[:, cols] += _colsum(dd)
                nxt = carry[:, cols]
                w1 = wrap_up(dd[:8], nxt[:8])
                w2 = wrap_up(dd[8:16], nxt[8:])
                ahead = (dd, jnp.concatenate([dd[8:], w1], axis=0), jnp.concatenate([dd[16:], w1, w2], axis=0))
                carry[:, cols] = dd[:16]
                uv = u_s[:, cols].astype(F32)
                du = jnp.zeros((t, FFN_BLK), F32)
                for s in range(FFN_K):
                    du = du + ahead[s] * fw[pl.ds(FFN_K - 1 - s, 1), cols]
                    dfw_ref[pl.ds(FFN_K - 1 - s, 1), cols] += _colsum(ahead[s] * uv)
                du_s[:, cols] = du.astype(BF16)
        du_ref[...] = _nn(pm_t, du_s[...]).astype(BF16)

        @pl.when(i == n_tiles - 1)
        def _():
            cp = pltpu.make_async_copy(dwd_acc, dwd_hbm, sem.at[1])
            cp.start()
            wd_v[...] = dwd_acc[...].astype(BF16)
            cp16 = pltpu.make_async_copy(wd_v, dwd16_hbm, sem.at[2])
            cp16.start()
            cp.wait()
            cp16.wait()

    def rev(width):
        return pl.BlockSpec((t, width), lambda i: (n_tiles - 1 - i, 0))

    assert FFN_K == 3
    halo_spec = pl.BlockSpec((8, D_MODEL), lambda i: (jnp.maximum((n_tiles - 1 - i) * (t // 8) - 1, 0), 0))
    vec_spec = _full((1, D_MODEL))
    out_shape = [jax.ShapeDtypeStruct((seq, 2 * D_FF), BF16), jax.ShapeDtypeStruct((seq, D_MODEL), F32),
                 jax.ShapeDtypeStruct((FFN_K, 2 * D_FF), F32), jax.ShapeDtypeStruct((1, 2 * D_FF), F32),
                 jax.ShapeDtypeStruct((1, D_MODEL), F32), jax.ShapeDtypeStruct((1, D_MODEL), F32),
                 jax.ShapeDtypeStruct((1, 128), F32), jax.ShapeDtypeStruct((D_FF, D_MODEL), F32),
                 jax.ShapeDtypeStruct((D_FF, D_MODEL), BF16)]
    return pl.pallas_call(
        body, grid=(n_tiles,), name="ffn",
        in_specs=[rev(D_MODEL), halo_spec, rev(D_MODEL), vec_spec, vec_spec, vec_spec, _full(ffn_w.shape),
                  _full(ffn_b.shape), _full(gt2.shape), _full(final_gain.shape), _full(to_inter.shape),
                  _full(to_natural.shape), ANY, ANY],
        out_specs=[rev(2 * D_FF), rev(D_MODEL), _full((FFN_K, 2 * D_FF)), _full((1, 2 * D_FF)), _full((1, D_MODEL)),
                   _full((1, D_MODEL)), _full((1, 128)), ANY, ANY],
        out_shape=out_shape,
        scratch_shapes=[pltpu.VMEM((D_MODEL, 2 * D_FF), BF16), pltpu.VMEM((D_FF, D_MODEL), BF16),
                        pltpu.VMEM((D_FF, D_MODEL), F32), pltpu.VMEM((FFN_HALO, 2 * D_FF), F32),
                        pltpu.VMEM((t, 2 * D_FF), BF16), pltpu.VMEM((t, D_FF), F32), pltpu.VMEM((t, D_FF), F32),
                        pltpu.VMEM((t, D_FF), BF16), pltpu.VMEM((t, 2 * D_FF), BF16),
                        pltpu.SemaphoreType.DMA((3 + N_SHARD,))],
        compiler_params=pltpu.CompilerParams(dimension_semantics=("arbitrary",), vmem_limit_bytes=FFN_VMEM_LIMIT_BYTES),
    )(x2, x2, target, norm2_gain, sc2, sh2, ffn_w, ffn_b, gt2, final_gain, to_inter, to_natural, w_up_g, w_down_g)


def _scatter_copies(src16, land, send_sems, recv_sems):
    x, y, c = _coords()
    h = src16.shape[1] // 2
    copies = []
    for f, flip in enumerate(PEER_FLIPS):
        tx, ty, tc = _flip(x, flip[0]), _flip(y, flip[1]), _flip(c, flip[2])
        copies.append(pltpu.make_async_remote_copy(
            src_ref=src16.at[2 * tx + ty, pl.ds(pl.multiple_of(tc * h, 16), h)], dst_ref=land.at[f],
            send_sem=send_sems.at[f], recv_sem=recv_sems.at[f], device_id=(tx, ty, tc), device_id_type=MESH))
    return copies


def _land_shape(src16):
    return jax.ShapeDtypeStruct((len(PEER_FLIPS), src16.shape[1] // 2, src16.shape[2]), BF16)


def _bwd_up(du, x2, dx3, norm2_gain, sc2, sh2, w_up_g, dwd16):
    seq = x2.shape[0]
    n_tiles = seq // TILE
    t = TILE

    def body(du_ref, x2_ref, dx3_ref, g2, sc2_ref, sh2_ref, wup_hbm, dwd16_hbm,
             dx2_ref, dg2_ref, dsc2_ref, dsh2_ref, dwup_hbm, dwup16_hbm, land_hbm,
             wup_v, dwup_acc, sem, send_sems, recv_sems):
        i = pl.program_id(0)

        @pl.when(i == 0)
        def _():
            for cp in _scatter_copies(dwd16_hbm, land_hbm, send_sems, recv_sems):
                cp.start()
            cp = pltpu.make_async_copy(wup_hbm, wup_v, sem.at[0])
            cp.start()
            cp.wait()
            dwup_acc[...] = jnp.zeros_like(dwup_acc)
            dg2_ref[...] = jnp.zeros_like(dg2_ref)
            dsc2_ref[...] = jnp.zeros_like(dsc2_ref)
            dsh2_ref[...] = jnp.zeros_like(dsh2_ref)

        h2, h2_vjp = jax.vjp(_mod_norm, x2_ref[...], g2[...], sc2_ref[...], sh2_ref[...])
        h2b = h2.astype(BF16)
        dh2 = jnp.zeros((t, D_MODEL), F32)
        for k in range(N_SHARD):
            dub = du_ref[:, k * UP_SHARD:(k + 1) * UP_SHARD]
            dh2 = dh2 + _nt(dub, wup_v[k])
            dwup_acc[k] += _tn(h2b, dub)
        dx2, dg2, dsc2, dsh2 = h2_vjp(dh2)
        dx2_ref[...] = dx3_ref[...] + dx2
        dg2_ref[...] += dg2
        dsc2_ref[...] += dsc2
        dsh2_ref[...] += dsh2

        @pl.when(i == n_tiles - 1)
        def _():
            cp = pltpu.make_async_copy(dwup_acc, dwup_hbm, sem.at[1])
            cp.start()
            for k in range(N_SHARD):
                wup_v[k] = dwup_acc[k].astype(BF16)
            cp16 = pltpu.make_async_copy(wup_v, dwup16_hbm, sem.at[2])
            cp16.start()
            cp.wait()
            cp16.wait()
            for rc in _scatter_copies(dwd16_hbm, land_hbm, send_sems, recv_sems):
                rc.wait()

    def row(width):
        return pl.BlockSpec((t, width), lambda i: (i, 0))

    vec = jax.ShapeDtypeStruct((1, D_MODEL), F32)
    n_peer = len(PEER_FLIPS)
    return pl.pallas_call(
        body, grid=(n_tiles,), name="bwd_up",
        in_specs=[row(2 * D_FF), row(D_MODEL), row(D_MODEL), _full((1, D_MODEL)), _full((1, D_MODEL)),
                  _full((1, D_MODEL)), ANY, ANY],
        out_specs=[row(D_MODEL), _full((1, D_MODEL)), _full((1, D_MODEL)), _full((1, D_MODEL)), ANY, ANY, ANY],
        out_shape=[jax.ShapeDtypeStruct((seq, D_MODEL), F32), vec, vec, vec,
                   jax.ShapeDtypeStruct(w_up_g.shape, F32), jax.ShapeDtypeStruct(w_up_g.shape, BF16),
                   _land_shape(dwd16)],
        scratch_shapes=[pltpu.VMEM(w_up_g.shape, BF16), pltpu.VMEM(w_up_g.shape, F32), pltpu.SemaphoreType.DMA((3,)),
                        pltpu.SemaphoreType.DMA((n_peer,)), pltpu.SemaphoreType.DMA((n_peer,))],
        compiler_params=_params(),
    )(du, x2, dx3, norm2_gain, sc2, sh2, w_up_g, dwd16)


def _bwd_mixer(dx2, x, z, a1, sp, yb, o1, vec, conv_w, wpair, wpair_t, causal_mask, w_in_g, w_out_g, dwup16):
    seq = x.shape[0]
    n_tiles = seq // TILE
    t = TILE
    names = ["norm1_gain", "sc1", "sh1", "gt1", "conv_ln_g", "conv_ln_b", "gm_ln_g", "gm_ln_b", "mix_out_gain"]
    vecs = [vec[k] for k in names]

    def body(dx2_ref, x_ref, z_ref, a1_ref, sp_ref, y_ref, o1_ref, g1, sc1, sh1, gt1, clg, clb, vg, vb, mg,
             cw, wp, wpt, mask_ref, win_hbm, wout_hbm, dwup16_hbm,
             gx_ref, dg1_ref, dsc1_ref, dsh1_ref, dgt1_ref, dcw_ref, dcb_ref, dclg_ref, dclb_ref, dvg_ref, dvb_ref,
             dmg_ref, dws_ref, dbs_ref, dwin_hbm, dwout_hbm, land_hbm, dwin16_hbm, dwout16_hbm,
             win_v, wout_v, dwin_acc, dwout_acc, carry, bank, dbs_acc, lwin, lwout, sem, send_sems, recv_sems,
             pair_send, pair_recv):
        i = pl.program_id(0)
        small = [dg1_ref, dsc1_ref, dsh1_ref, dgt1_ref, dcw_ref, dcb_ref, dclg_ref, dclb_ref, dvg_ref, dvb_ref,
                 dmg_ref, dws_ref, dbs_acc]

        @pl.when(i == 0)
        def _():
            for cp in _scatter_copies(dwup16_hbm, land_hbm, send_sems, recv_sems):
                cp.start()
            cps = [pltpu.make_async_copy(win_hbm, win_v, sem.at[0]),
                   pltpu.make_async_copy(wout_hbm, wout_v, sem.at[1])]
            for cp in cps:
                cp.start()
            for cp in cps:
                cp.wait()
            dwin_acc[...] = jnp.zeros_like(dwin_acc)
            dwout_acc[...] = jnp.zeros_like(dwout_acc)
            carry[...] = jnp.zeros_like(carry)
            for ref in small:
                ref[...] = jnp.zeros_like(ref)

        dx2v = dx2_ref[...]
        gt1v = gt1[...]
        dgt1_ref[...] += _colsum(dx2v * o1_ref[...])
        do1b = (gt1v * dx2v).astype(BF16)
        dy = _nt(do1b, wout_v[...])
        dwout_acc[...] += _tn(y_ref[...], do1b)

        mgv = mg[...]
        _, conv_vjp = jax.vjp(_conv_branch, a1_ref[...], clg[...], clb[...], mgv[:, :D_HALF])
        da1, dclg, dclb, dmg_a = conv_vjp(dy[:, :D_HALF])
        dclg_ref[...] += dclg
        dclb_ref[...] += dclb
        gu = z_ref[:, 2 * D_HALF:3 * D_HALF]
        gv = z_ref[:, 3 * D_HALF:]
        spv = sp_ref[...]
        _, gate_vjp = jax.vjp(_gate_branch, gu, spv, mgv[:, D_HALF:])
        dgu, dsp, dmg_g = gate_vjp(dy[:, D_HALF:])
        dmg_ref[...] += jnp.concatenate([dmg_a, dmg_g], axis=1)
        gvn, gv_vjp = jax.vjp(_gv_norm, gv, vg[...], vb[...])
        gvnb = gvn.astype(BF16)
        dspb = dsp.astype(BF16)
        dgvn = _head_pair_matmul(wpt, dspb)
        dgv, dvg, dvb = gv_vjp(dgvn)
        dvg_ref[...] += dvg
        dvb_ref[...] += dvb
        lane = lax.broadcasted_iota(jnp.int32, (CHUNK, CHUNK), 1)
        dbs = jnp.zeros((CHUNK, D_HALF), F32)
        for n in range(t // CHUNK):
            rows = slice(n * CHUNK, (n + 1) * CHUNK)
            dbs = dbs + dsp[rows, :]
            for j in range(N_HEADS // 2):
                cols = slice(j * CHUNK, (j + 1) * CHUNK)
                blk = dspb[rows, cols]
                zero = jnp.zeros_like(blk)
                vblk = gvnb[rows, cols]
                dws_ref[2 * j] += _nt(jnp.where(lane < HEAD_DIM, blk, zero), vblk)
                dws_ref[2 * j + 1] += _nt(jnp.where(lane < HEAD_DIM, zero, blk), vblk)
        dbs_acc[...] += dbs

        ca = z_ref[:, :D_HALF]
        cg = z_ref[:, D_HALF:2 * D_HALF]
        sig = jax.nn.sigmoid(cg)
        a0 = ca * sig
        ext = jnp.concatenate([da1, carry[...]], axis=0)
        carry[...] = da1[:HALO]
        bank[0] = ext
        for b in range(1, 8):
            bank[b] = pltpu.roll(ext, t + HALO - b, axis=0)
        dcb_ref[...] += _colsum(da1)
        da0 = jnp.zeros((t, D_HALF), F32)
        for s in range(CONV_K):
            q, b = divmod(s, 8)
            shifted = bank[b, pl.ds(8 * q, t), :]
            da0 = da0 + shifted * cw[pl.ds(CONV_K - 1 - s, 1), :]
            dcw_ref[pl.ds(CONV_K - 1 - s, 1), :] += _colsum(shifted * a0)
        dca = da0 * sig
        dcg = da0 * ca * sig * (1.0 - sig)

        h1, h1_vjp = jax.vjp(_mod_norm, x_ref[...], g1[...], sc1[...], sh1[...])
        h1b = h1.astype(BF16)
        dh1 = jnp.zeros((t, D_MODEL), F32)
        for k, dzk in enumerate((dca, dcg, dgu, dgv)):
            dzb = dzk.astype(BF16)
            dh1 = dh1 + _nt(dzb, win_v[k])
            dwin_acc[k] += _tn(h1b, dzb)
        dx, dg1, dsc1, dsh1 = h1_vjp(dh1)
        gx_ref[...] = dx2v + dx
        dg1_ref[...] += dg1
        dsc1_ref[...] += dsc1
        dsh1_ref[...] += dsh1

        @pl.when(i == n_tiles - 1)
        def _():
            for h in range(N_HEADS):
                dws_ref[h] = dws_ref[h] * mask_ref[...]
            head_of_lane = lax.broadcasted_iota(jnp.int32, (N_HEADS, D_HALF), 1) // HEAD_DIM
            pick = (head_of_lane == lax.broadcasted_iota(jnp.int32, (N_HEADS, D_HALF), 0)).astype(F32)
            dbs_ref[...] = lax.dot_general(pick, dbs_acc[...], NT_DIMS, precision=lax.Precision.HIGHEST,
                                           preferred_element_type=F32)
            for k in range(N_SHARD):
                win_v[k] = dwin_acc[k].astype(BF16)
            wout_v[...] = dwout_acc[...].astype(BF16)
            mx, my, mc = _coords()
            h_in, h_out = dwin_acc.shape[1] // 2, dwout_acc.shape[0] // (2 * N_SHARD)

            def in_rows(ref, k, which):
                return ref.at[k, pl.ds(pl.multiple_of(which * h_in, 16), h_in), :]

            def out_rows(ref, k, which):
                return ref.at[pl.ds(pl.multiple_of((2 * k + which) * h_out, 16), h_out), :]

            pairs = ((win_v, dwin_acc, lwin, in_rows, dwin_hbm, dwin16_hbm),
                     (wout_v, dwout_acc, lwout, out_rows, dwout_hbm, dwout16_hbm))
            swaps = [pltpu.make_async_remote_copy(
                src_ref=rows_of(v16, k, 1 - mc), dst_ref=land.at[k], send_sem=pair_send.at[w, k],
                recv_sem=pair_recv.at[w, k], device_id=(mx, my, 1 - mc), device_id_type=MESH)
                for w, (v16, _, land, rows_of, _, _) in enumerate(pairs) for k in range(N_SHARD)]
            for cp in swaps:
                cp.start()
            for cp in swaps:
                cp.wait()
            outs = []
            for w, (v16, acc, land, rows_of, half_hbm, half16_hbm) in enumerate(pairs):
                for k in range(N_SHARD):
                    total = rows_of(acc, k, mc)[...] + land[k].astype(F32)
                    rows_of(acc, k, 0)[...] = total
                    rows_of(v16, k, 0)[...] = total.astype(BF16)
                    outs.append(pltpu.make_async_copy(rows_of(acc, k, 0), half_hbm.at[k], sem.at[2 + 8 * w + k]))
                    outs.append(pltpu.make_async_copy(rows_of(v16, k, 0), half16_hbm.at[k], sem.at[6 + 8 * w + k]))
            for cp in outs:
                cp.start()
            for cp in outs:
                cp.wait()
            for rc in _scatter_copies(dwup16_hbm, land_hbm, send_sems, recv_sems):
                rc.wait()

    def rev(width):
        return pl.BlockSpec((t, width), lambda i: (n_tiles - 1 - i, 0))

    v1024 = jax.ShapeDtypeStruct((1, D_MODEL), F32)
    v512 = jax.ShapeDtypeStruct((1, D_HALF), F32)
    small_shapes = [v1024, v1024, v1024, v1024, jax.ShapeDtypeStruct((CONV_K, D_HALF), F32), v512, v512, v512, v512,
                    v512, v1024, jax.ShapeDtypeStruct((N_HEADS, CHUNK, CHUNK), F32),
                    jax.ShapeDtypeStruct((N_HEADS, CHUNK), F32)]
    n_peer = len(PEER_FLIPS)
    half_in = (N_SHARD, w_in_g.shape[1] // 2, w_in_g.shape[2])
    half_out = (N_SHARD, w_out_g.shape[0] // (2 * N_SHARD), w_out_g.shape[1])
    return pl.pallas_call(
        body, grid=(n_tiles,), name="bwd_mixer",
        in_specs=[rev(D_MODEL), rev(D_MODEL), rev(4 * D_HALF), rev(D_HALF), rev(D_HALF), rev(D_MODEL),
                  rev(D_MODEL)] + [_full(v.shape) for v in vecs]
        + [_full(conv_w.shape), _full(wpair.shape), _full(wpair_t.shape), _full(causal_mask.shape), ANY, ANY, ANY],
        out_specs=[rev(D_MODEL)] + [_full(s.shape) for s in small_shapes] + [ANY] * 5,
        out_shape=[jax.ShapeDtypeStruct((seq, D_MODEL), F32)] + small_shapes
        + [jax.ShapeDtypeStruct(half_in, F32), jax.ShapeDtypeStruct(half_out, F32), _land_shape(dwup16),
           jax.ShapeDtypeStruct(half_in, BF16), jax.ShapeDtypeStruct(half_out, BF16)],
        scratch_shapes=[pltpu.VMEM(w_in_g.shape, BF16), pltpu.VMEM(w_out_g.shape, BF16),
                        pltpu.VMEM(w_in_g.shape, F32), pltpu.VMEM(w_out_g.shape, F32),
                        pltpu.VMEM((HALO, D_HALF), F32), pltpu.VMEM((8, t + HALO, D_HALF), F32),
                        pltpu.VMEM((CHUNK, D_HALF), F32), pltpu.VMEM(half_in, BF16), pltpu.VMEM(half_out, BF16),
                        pltpu.SemaphoreType.DMA((2 + 4 * N_SHARD,)),
                        pltpu.SemaphoreType.DMA((n_peer,)), pltpu.SemaphoreType.DMA((n_peer,)),
                        pltpu.SemaphoreType.DMA((2, N_SHARD)), pltpu.SemaphoreType.DMA((2, N_SHARD))],
        compiler_params=_params(),
    )(dx2, x, z, a1, sp, yb, o1, *vecs, conv_w, wpair, wpair_t, causal_mask, w_in_g, w_out_g, dwup16)


def _gmlp_operands(gm_ws, gm_bs):
    mask = jnp.tril(jnp.ones((CHUNK, CHUNK), F32))
    ws = gm_ws * mask[None]
    wpair = ws.reshape(N_HEADS // 2, 2 * CHUNK, CHUNK).astype(BF16)
    wpair_t = jnp.swapaxes(ws, 1, 2).reshape(N_HEADS // 2, 2 * CHUNK, CHUNK).astype(BF16)
    bs_full = jnp.repeat(jnp.transpose(gm_bs), HEAD_DIM, axis=1)
    return wpair, wpair_t, bs_full, mask


def _local_step(x, target, mod, p, w_in_g, w_out_g, w_up_part, w_down_part):
    sh1, sc1, gt1, sh2, sc2, gt2 = [mod[:, k * D_MODEL:(k + 1) * D_MODEL] for k in range(6)]
    vec = dict(p, sh1=sh1, sc1=sc1, gt1=gt1, sh2=sh2, sc2=sc2, gt2=gt2)
    wpair, wpair_t, bs_full, mask = _gmlp_operands(p["gm_ws"], p["gm_bs"])

    (z, a1, sp, yb, o1, x2), (w_up_g, w_down_g) = _fwd_mixer(
        x, vec, p["conv_dw_w"], wpair, bs_full, w_in_g, w_out_g, [w_up_part, w_down_part])
    w_down_g = w_down_g.reshape(D_FF, D_MODEL)
    to_inter, to_natural = _interleave_matrices()
    du, dx3, d_ffn_w, d_ffn_b, d_fg, d_gt2, loss, d_wd, d_wd16 = _ffn(
        x2, target, p["norm2_gain"], sc2, sh2, p["ffn_dw_w"], p["ffn_dw_b"], gt2, p["final_gain"], w_up_g, w_down_g,
        to_inter, to_natural)
    by_shard = (N_SHARD, -1, D_MODEL)
    dx2, d_g2, d_sc2, d_sh2, d_wup, d_wup16, land_wd = _bwd_up(
        du, x2, dx3, p["norm2_gain"], sc2, sh2, w_up_g, d_wd16.reshape(by_shard))
    (gx, d_g1, d_sc1, d_sh1, d_gt1, d_cw, d_cb, d_clg, d_clb, d_vg, d_vb, d_mg, d_ws, d_bs, d_win, d_wout, land_wup,
     d_win16, d_wout16) = _bwd_mixer(dx2, x, z, a1, sp, yb, o1, vec, p["conv_dw_w"], wpair, wpair_t, mask, w_in_g,
                                     w_out_g, d_wup16)
    d_mod = jnp.concatenate([d_sh1, d_sc1, d_gt1, d_sh2, d_sc2, d_gt2], axis=1)
    grads = dict(norm1_gain=d_g1, conv_dw_w=d_cw, conv_dw_b=d_cb, conv_ln_g=d_clg, conv_ln_b=d_clb, gm_ln_g=d_vg,
                 gm_ln_b=d_vb, gm_ws=d_ws, gm_bs=d_bs, mix_out_gain=d_mg, norm2_gain=d_g2, ffn_dw_w=d_ffn_w,
                 ffn_dw_b=d_ffn_b, final_gain=d_fg, w_in=d_win, w_out=d_wout, w_up=d_wup, w_down=d_wd.reshape(by_shard))
    in_flight = dict(w_in16=d_win16, w_out16=d_wout16, land_w_up=land_wup, land_w_down=land_wd)
    return gx, grads, d_mod, loss, in_flight


MESH = pl.DeviceIdType.MESH
VMEM_SPEC = pl.BlockSpec(memory_space=pltpu.VMEM)
PEER_FLIPS = [(a, b, d) for a in (0, 1) for b in (0, 1) for d in (0, 1)][1:]
CHIP_FLIPS = [(1, 0), (0, 1), (1, 1)]


def _coords():
    return lax.axis_index("x"), lax.axis_index("y"), lax.axis_index("c")


def _flip(v, bit):
    return 1 - v if bit else v


def _rows8(block):
    return pl.ds(pl.multiple_of(8 * block, 8), 8)


def _ada_mod(c_row, w_ada_sh, b_ada_sh):
    cols = w_ada_sh.shape[1]

    def body(c_ref, w_ref, b_ref, call_ref, mod_ref, cpad, modall, send_sems, recv_sems):
        x, y, c = _coords()
        me = 4 * x + 2 * y + c
        cpad[...] = jnp.zeros_like(cpad)
        cpad[pl.ds(0, 1), :] = c_ref[...]

        def gather_copy(j, flip):
            peer = (_flip(x, flip[0]), _flip(y, flip[1]), _flip(c, flip[2]))
            return pltpu.make_async_remote_copy(
                src_ref=cpad, dst_ref=call_ref.at[_rows8(me)], send_sem=send_sems.at[j], recv_sem=recv_sems.at[j],
                device_id=peer, device_id_type=MESH)

        copies = [gather_copy(j, f) for j, f in enumerate(PEER_FLIPS)]
        for cp in copies:
            cp.start()
        call_ref[_rows8(me), :] = cpad[...]
        for cp in copies:
            cp.wait_recv()
        for cp in copies:
            cp.wait_send()
        cv = call_ref[...]
        c_act = (cv * jax.nn.sigmoid(cv)).astype(BF16)
        modall[...] = _nn(c_act, w_ref[...].astype(BF16)) + b_ref[...]

        slot = _rows8(2 * x + y)

        def piece_copy(j, flip):
            tx, ty = _flip(x, flip[0]), _flip(y, flip[1])
            return pltpu.make_async_remote_copy(
                src_ref=modall.at[_rows8(4 * tx + 2 * ty + c)], dst_ref=mod_ref.at[slot],
                send_sem=send_sems.at[len(PEER_FLIPS) + j], recv_sem=recv_sems.at[len(PEER_FLIPS) + j],
                device_id=(tx, ty, c), device_id_type=MESH)

        pieces = [piece_copy(j, f) for j, f in enumerate(CHIP_FLIPS)]
        for cp in pieces:
            cp.start()
        mod_ref[slot, :] = modall[_rows8(me), :]
        for cp in pieces:
            cp.wait_recv()
        for cp in pieces:
            cp.wait_send()

    n_sem = len(PEER_FLIPS) + len(CHIP_FLIPS)
    return pl.pallas_call(
        body, name="ada_mod",
        in_specs=[VMEM_SPEC, VMEM_SPEC, VMEM_SPEC], out_specs=[VMEM_SPEC, VMEM_SPEC],
        out_shape=[jax.ShapeDtypeStruct((8 * N_DEV, D_MODEL), F32), jax.ShapeDtypeStruct((8 * N_SHARD, cols), F32)],
        scratch_shapes=[pltpu.VMEM((8, D_MODEL), F32), pltpu.VMEM((8 * N_DEV, cols), F32),
                        pltpu.SemaphoreType.DMA((n_sem,)), pltpu.SemaphoreType.DMA((n_sem,))],
        compiler_params=pltpu.CompilerParams(vmem_limit_bytes=VMEM_LIMIT_BYTES),
    )(c_row, w_ada_sh, b_ada_sh)


def _gather_weights(shards, filters, n_now):
    n = len(shards)
    nf = len(filters)

    def body(*refs):
        ins, f_ins = refs[:n], refs[n:n + nf]
        outs, f_outs = refs[n + nf:2 * n + nf], refs[2 * n + nf:2 * (n + nf)]
        stage = refs[2 * (n + nf):3 * n + 2 * nf]
        send_sems, recv_sems, local_sems, f_send_sems, f_recv_sems = refs[3 * n + 2 * nf:]
        x, y, c = _coords()
        k = 2 * x + y
        sibling = (x, y, 1 - c)

        def filter_copy(w, j, slot):
            tx, ty = _flip(x, CHIP_FLIPS[j][0]), _flip(y, CHIP_FLIPS[j][1])
            return pltpu.make_async_remote_copy(
                src_ref=f_ins[w], dst_ref=f_outs[w].at[slot], send_sem=f_send_sems.at[w, j],
                recv_sem=f_recv_sems.at[w, j], device_id=(tx, ty, c), device_id_type=MESH)

        def half(w, which):
            h = shards[w].shape[0] // 2
            return pl.ds(pl.multiple_of(which * h, 16), h)

        def ici_copy(w, j, src, slot):
            tx, ty = _flip(x, CHIP_FLIPS[j][0]), _flip(y, CHIP_FLIPS[j][1])
            return pltpu.make_async_remote_copy(
                src_ref=src, dst_ref=outs[w].at[slot, half(w, c)], send_sem=send_sems.at[w, j],
                recv_sem=recv_sems.at[w, j], device_id=(tx, ty, c), device_id_type=MESH)

        def d2d_copy(w, j, slot, which):
            rows = outs[w].at[slot, half(w, which)]
            return pltpu.make_async_remote_copy(
                src_ref=rows, dst_ref=rows, send_sem=send_sems.at[w, len(CHIP_FLIPS) + j],
                recv_sem=recv_sems.at[w, len(CHIP_FLIPS) + j], device_id=sibling, device_id_type=MESH)

        def chip_of(j):
            return 2 * _flip(x, CHIP_FLIPS[j][0]) + _flip(y, CHIP_FLIPS[j][1])

        local, first, passed = [], [], []
        for w in range(nf):
            local.append(pltpu.make_async_copy(f_ins[w], f_outs[w].at[k], local_sems.at[n + w]))
            local[-1].start()
            for j in range(len(CHIP_FLIPS)):
                first.append(filter_copy(w, j, k))
                first[-1].start()
        for w in range(n):
            stage[w][...] = ins[w][...].astype(BF16)
            local.append(pltpu.make_async_copy(stage[w], outs[w].at[k], local_sems.at[w]))
            local[-1].start()
            if w < n_now:
                for j in range(len(CHIP_FLIPS)):
                    first.append(ici_copy(w, j, stage[w].at[half(w, c)], k))
                    first[-1].start()
        for w in range(nf):
            for j in range(len(CHIP_FLIPS)):
                filter_copy(w, j, chip_of(j)).wait_recv()
        for w in range(n_now):
            for j in range(len(CHIP_FLIPS)):
                ici_copy(w, j, stage[w].at[half(w, c)], chip_of(j)).wait_recv()
                passed.append(d2d_copy(w, j, chip_of(j), c))
                passed[-1].start()
        for w in range(n_now):
            for j in range(len(CHIP_FLIPS)):
                d2d_copy(w, j, chip_of(j), 1 - c).wait_recv()
        for cp in first + passed:
            cp.wait_send()
        for cp in local:
            cp.wait()

    sem_shape = (n_now, 2 * len(CHIP_FLIPS))
    f_sem_shape = (nf, len(CHIP_FLIPS))
    outs = pl.pallas_call(
        body, name="gather_weights",
        in_specs=[VMEM_SPEC] * (n + nf), out_specs=[ANY] * (n + nf),
        out_shape=[jax.ShapeDtypeStruct((N_SHARD,) + s.shape, BF16) for s in shards]
        + [jax.ShapeDtypeStruct((N_SHARD,) + s.shape, F32) for s in filters],
        scratch_shapes=[pltpu.VMEM(s.shape, BF16) for s in shards]
        + [pltpu.SemaphoreType.DMA(sem_shape), pltpu.SemaphoreType.DMA(sem_shape), pltpu.SemaphoreType.DMA((n + nf,)),
           pltpu.SemaphoreType.DMA(f_sem_shape), pltpu.SemaphoreType.DMA(f_sem_shape)],
        compiler_params=pltpu.CompilerParams(vmem_limit_bytes=VMEM_LIMIT_BYTES),
    )(*shards, *filters)
    return outs[:n], outs[n:]


def _final_comm(srcs16, small):
    n = len(srcs16)
    rows = small.shape[0]
    half = rows // 2

    def body(*refs):
        srcs, small_ref = refs[:n], refs[n]
        lands, small_out = refs[n + 1:2 * n + 1], refs[2 * n + 1]
        chip_sum, got_c, got_x, got_y, part_x, send_sems, recv_sems, small_send_sems, small_recv_sems = refs[2 * n + 2:]
        x, y, c = _coords()
        sibling = (x, y, 1 - c)
        mine = pl.ds(pl.multiple_of(c * half, 8), half)
        copies = []
        for w in range(n):
            for j, flip in enumerate(CHIP_FLIPS):
                tx, ty = _flip(x, flip[0]), _flip(y, flip[1])
                copies.append(pltpu.make_async_remote_copy(
                    src_ref=srcs[w].at[2 * tx + ty], dst_ref=lands[w].at[j], send_sem=send_sems.at[w, j],
                    recv_sem=recv_sems.at[w, j], device_id=(tx, ty, c), device_id_type=MESH))
        for cp in copies:
            cp.start()

        def exchange(stage, src, dst, peer):
            rc = pltpu.make_async_remote_copy(
                src_ref=src, dst_ref=dst, send_sem=small_send_sems.at[stage], recv_sem=small_recv_sems.at[stage],
                device_id=peer, device_id_type=MESH)
            rc.start()
            rc.wait()

        exchange(0, small_ref, got_c, sibling)
        chip_sum[...] = small_ref[...] + got_c[...]
        exchange(1, chip_sum.at[mine], got_x, (1 - x, y, c))
        part_x[...] = chip_sum[mine, :] + got_x[...]
        exchange(2, part_x, got_y, (x, 1 - y, c))
        small_out[mine, :] = part_x[...] + got_y[...]
        exchange(3, small_out.at[mine], small_out.at[mine], sibling)
        for cp in copies:
            cp.wait()

    n_chip = len(CHIP_FLIPS)
    half_shape = (half, small.shape[1])
    outs = pl.pallas_call(
        body, name="final_comm",
        in_specs=[ANY] * n + [VMEM_SPEC], out_specs=[ANY] * n + [VMEM_SPEC],
        out_shape=[jax.ShapeDtypeStruct((n_chip,) + a.shape[1:], BF16) for a in srcs16]
        + [jax.ShapeDtypeStruct(small.shape, F32)],
        scratch_shapes=[pltpu.VMEM(small.shape, F32), pltpu.VMEM(small.shape, F32), pltpu.VMEM(half_shape, F32),
                        pltpu.VMEM(half_shape, F32), pltpu.VMEM(half_shape, F32),
                        pltpu.SemaphoreType.DMA((n, n_chip)), pltpu.SemaphoreType.DMA((n, n_chip)),
                        pltpu.SemaphoreType.DMA((4,)), pltpu.SemaphoreType.DMA((4,))],
        compiler_params=pltpu.CompilerParams(vmem_limit_bytes=VMEM_LIMIT_BYTES),
    )(*srcs16, small)
    return outs[:n], outs[n]


ADD_CHUNKS = 4


def _scatter_sum(pos, owns, lands):
    n = len(owns)

    def specs(own_shape, land_shape):
        peers, rows, cols = land_shape
        pick = 1 if own_shape[1] == 2 * rows else 0
        if cols % (128 * ADD_CHUNKS) == 0:
            blk = (rows, cols // ADD_CHUNKS)
            return (pl.BlockSpec((1,) + blk, lambda i, p: (2 * p[0] + p[1], pick * p[2], i)),
                    pl.BlockSpec((peers,) + blk, lambda i, p: (0, 0, i)),
                    pl.BlockSpec((1,) + blk, lambda i, p: (p[2], 0, i)))
        blk = (rows // ADD_CHUNKS, cols)
        return (pl.BlockSpec((1,) + blk, lambda i, p: (2 * p[0] + p[1], pick * p[2] * ADD_CHUNKS + i, 0)),
                pl.BlockSpec((peers,) + blk, lambda i, p: (0, i, 0)),
                pl.BlockSpec((1,) + blk, lambda i, p: (p[2], i, 0)))

    def body(pos_ref, *refs):
        for idx in range(n):
            own, land, out = refs[idx], refs[n + idx], refs[2 * n + idx]
            total = own[0]
            for f in range(land.shape[0]):
                total = total + land[f].astype(F32)
            out[0] = total

    all_specs = [specs(o.shape, l.shape) for o, l in zip(owns, lands)]
    return pl.pallas_call(
        body, name="scatter_sum",
        grid_spec=pltpu.PrefetchScalarGridSpec(
            num_scalar_prefetch=1, grid=(ADD_CHUNKS,),
            in_specs=[s[0] for s in all_specs] + [s[1] for s in all_specs], out_specs=[s[2] for s in all_specs]),
        out_shape=[jax.ShapeDtypeStruct((2,) + l.shape[1:], F32) for l in lands],
        compiler_params=_params(),
    )(pos, *owns, *lands)


def _swap_halves(halves):
    n = len(halves)

    def body(*refs):
        ins, outs = refs[:n], refs[n:2 * n]
        send_sems, recv_sems = refs[2 * n:]
        x, y, c = _coords()
        copies = [pltpu.make_async_remote_copy(
            src_ref=ins[idx].at[pl.ds(c, 1)], dst_ref=outs[idx].at[pl.ds(c, 1)], send_sem=send_sems.at[idx],
            recv_sem=recv_sems.at[idx], device_id=(x, y, 1 - c), device_id_type=MESH) for idx in range(n)]
        for cp in copies:
            cp.start()
        for cp in copies:
            cp.wait()

    return pl.pallas_call(
        body, name="swap_halves",
        in_specs=[ANY] * n, out_specs=[ANY] * n, input_output_aliases={idx: idx for idx in range(n)},
        out_shape=[jax.ShapeDtypeStruct(a.shape, F32) for a in halves],
        scratch_shapes=[pltpu.SemaphoreType.DMA((n,)), pltpu.SemaphoreType.DMA((n,))],
    )(*halves)


def _adamw_math(w, g, m, v):
    m = ADAM_B1 * m + (1.0 - ADAM_B1) * g
    v = ADAM_B2 * v + (1.0 - ADAM_B2) * jnp.square(g)
    m_hat = m / (1.0 - ADAM_B1 ** ADAM_STEP)
    v_hat = v / (1.0 - ADAM_B2 ** ADAM_STEP)
    delta = -ADAM_LR * (m_hat / (jnp.sqrt(v_hat) + ADAM_EPS) + ADAM_WD * w)
    return delta, m, v


def _adamw(name, w, g, m, v, block_rows):
    rows, cols = w.shape

    def body(w_ref, g_ref, m_ref, v_ref, d_out, m_out, v_out):
        d_out[...], m_out[...], v_out[...] = _adamw_math(w_ref[...], g_ref[...], m_ref[...], v_ref[...])

    spec = pl.BlockSpec((block_rows, cols), lambda i: (i, 0))
    shape = jax.ShapeDtypeStruct((rows, cols), F32)
    return pl.pallas_call(
        body, grid=(rows // block_rows,), name=name, in_specs=[spec] * 4, out_specs=[spec] * 3,
        out_shape=[shape] * 3, compiler_params=_params(),
    )(w, g, m, v)


def _adamw_many(ws, gs, ms, vs):
    n = len(ws)

    def body(*refs):
        w_refs, g_refs, m_refs, v_refs = (refs[q * n:(q + 1) * n] for q in range(4))
        d_outs, m_outs, v_outs = (refs[(4 + q) * n:(5 + q) * n] for q in range(3))
        for idx in range(n):
            d_outs[idx][...], m_outs[idx][...], v_outs[idx][...] = _adamw_math(
                w_refs[idx][...], g_refs[idx][...], m_refs[idx][...], v_refs[idx][...])

    shapes = [jax.ShapeDtypeStruct(w.shape, F32) for w in ws]
    outs = pl.pallas_call(
        body, name="adamw_small", in_specs=[VMEM_SPEC] * (4 * n), out_specs=[VMEM_SPEC] * (3 * n),
        out_shape=shapes * 3, compiler_params=pltpu.CompilerParams(vmem_limit_bytes=VMEM_LIMIT_BYTES),
    )(*ws, *gs, *ms, *vs)
    return outs[:n], outs[n:2 * n], outs[2 * n:]


def _adamw_ada(c_all16, dmod16, w, m, v, block_rows):
    rows, cols = w.shape

    def body(c_ref, dm_ref, w_ref, m_ref, v_ref, g_out, d_out, m_out, v_out):
        cv = c_ref[...]
        g = _tn((cv * jax.nn.sigmoid(cv)).astype(BF16), dm_ref[...].astype(BF16))
        g_out[...] = g
        d_out[...], m_out[...], v_out[...] = _adamw_math(w_ref[...], g, m_ref[...], v_ref[...])

    spec = pl.BlockSpec((block_rows, cols), lambda i: (i, 0))
    shape = jax.ShapeDtypeStruct((rows, cols), F32)
    return pl.pallas_call(
        body, grid=(rows // block_rows,), name="adamw_w_ada",
        in_specs=[pl.BlockSpec((16, block_rows), lambda i: (0, i)), _full(dmod16.shape), spec, spec, spec],
        out_specs=[spec] * 4, out_shape=[shape] * 4, compiler_params=_params(),
    )(c_all16, dmod16, w, m, v)


SMALL_REPLICATED = ["b_ada", "norm1_gain", "conv_dw_b", "conv_ln_g", "conv_ln_b", "gm_ln_g", "gm_ln_b", "gm_ws", "gm_bs",
                    "mix_out_gain", "norm2_gain", "ffn_dw_b", "final_gain"]
SMALL_SHARDED = ["conv_dw_w", "ffn_dw_w"]
PACK_ROWS = 256
WEIGHT_ORDER = ["w_ada", "b_ada", "norm1_gain", "w_in", "conv_dw_w", "conv_dw_b", "conv_ln_g", "conv_ln_b", "gm_ln_g",
                "gm_ln_b", "gm_ws", "gm_bs", "mix_out_gain", "w_out", "norm2_gain", "w_up", "ffn_dw_w", "ffn_dw_b",
                "w_down", "final_gain"]


def _pack(parts, rows):
    flat = jnp.concatenate([a.reshape(-1) for a in parts])
    return jnp.pad(flat, (0, rows * D_MODEL - flat.shape[0])).reshape(rows, D_MODEL)


def _unpack(packed, shapes):
    flat = packed.reshape(-1)
    out, pos = [], 0
    for s in shapes:
        size = 1
        for d in s:
            size *= d
        out.append(flat[pos:pos + size].reshape(s))
        pos += size
    return out


def kernel(x, c, w_ada, b_ada, norm1_gain, w_in, conv_dw_w, conv_dw_b, conv_ln_g, conv_ln_b, gm_ln_g, gm_ln_b, gm_ws, gm_bs, mix_out_gain, w_out, norm2_gain, w_up, ffn_dw_w, ffn_dw_b, w_down, final_gain, loss_target, m_w_ada, m_b_ada, m_norm1_gain, m_w_in, m_conv_dw_w, m_conv_dw_b, m_conv_ln_g, m_conv_ln_b, m_gm_ln_g, m_gm_ln_b, m_gm_ws, m_gm_bs, m_mix_out_gain, m_w_out, m_norm2_gain, m_w_up, m_ffn_dw_w, m_ffn_dw_b, m_w_down, m_final_gain, v_w_ada, v_b_ada, v_norm1_gain, v_w_in, v_conv_dw_w, v_conv_dw_b, v_conv_ln_g, v_conv_ln_b, v_gm_ln_g, v_gm_ln_b, v_gm_ws, v_gm_bs, v_mix_out_gain, v_w_out, v_norm2_gain, v_w_up, v_ffn_dw_w, v_ffn_dw_b, v_w_down, v_final_gain):
    weights = dict(w_ada=w_ada, b_ada=b_ada, norm1_gain=norm1_gain, w_in=w_in, conv_dw_w=conv_dw_w, conv_dw_b=conv_dw_b,
                   conv_ln_g=conv_ln_g, conv_ln_b=conv_ln_b, gm_ln_g=gm_ln_g, gm_ln_b=gm_ln_b, gm_ws=gm_ws, gm_bs=gm_bs,
                   mix_out_gain=mix_out_gain, w_out=w_out, norm2_gain=norm2_gain, w_up=w_up, ffn_dw_w=ffn_dw_w,
                   ffn_dw_b=ffn_dw_b, w_down=w_down, final_gain=final_gain)
    mom1 = dict(w_ada=m_w_ada, b_ada=m_b_ada, norm1_gain=m_norm1_gain, w_in=m_w_in, conv_dw_w=m_conv_dw_w,
                conv_dw_b=m_conv_dw_b, conv_ln_g=m_conv_ln_g, conv_ln_b=m_conv_ln_b, gm_ln_g=m_gm_ln_g, gm_ln_b=m_gm_ln_b,
                gm_ws=m_gm_ws, gm_bs=m_gm_bs, mix_out_gain=m_mix_out_gain, w_out=m_w_out, norm2_gain=m_norm2_gain,
                w_up=m_w_up, ffn_dw_w=m_ffn_dw_w, ffn_dw_b=m_ffn_dw_b, w_down=m_w_down, final_gain=m_final_gain)
    mom2 = dict(w_ada=v_w_ada, b_ada=v_b_ada, norm1_gain=v_norm1_gain, w_in=v_w_in, conv_dw_w=v_conv_dw_w,
                conv_dw_b=v_conv_dw_b, conv_ln_g=v_conv_ln_g, conv_ln_b=v_conv_ln_b, gm_ln_g=v_gm_ln_g, gm_ln_b=v_gm_ln_b,
                gm_ws=v_gm_ws, gm_bs=v_gm_bs, mix_out_gain=v_mix_out_gain, w_out=v_w_out, norm2_gain=v_norm2_gain,
                w_up=v_w_up, ffn_dw_w=v_ffn_dw_w, ffn_dw_b=v_ffn_dw_b, w_down=v_w_down, final_gain=v_final_gain)
    shard = 2 * lax.axis_index("x") + lax.axis_index("y")
    me = 2 * shard + lax.axis_index("c")

    ada_cols = w_ada.shape[2]
    b_ada_sh = lax.dynamic_slice(b_ada, (0, shard * ada_cols), (1, ada_cols))
    c_all64, mod32 = _ada_mod(c, w_ada[0], b_ada_sh)
    c_all = c_all64[::8]
    mod = mod32[::8].reshape(1, N_SHARD * ada_cols)

    (w_in_g, w_out_g, w_up_part, w_down_part), (conv_w_g, ffn_w_g) = _gather_weights(
        [w_in[0], w_out[0], w_up[0], w_down[0]], [conv_dw_w[0], ffn_dw_w[0]], n_now=2)
    conv_w_full = jnp.transpose(conv_w_g, (1, 0, 2)).reshape(CONV_K, D_HALF)
    ffn_w_full = jnp.transpose(ffn_w_g, (1, 0, 2)).reshape(FFN_K, 2 * D_FF)

    p = dict(norm1_gain=norm1_gain, conv_dw_w=conv_w_full, conv_dw_b=conv_dw_b, conv_ln_g=conv_ln_g,
             conv_ln_b=conv_ln_b, gm_ln_g=gm_ln_g, gm_ln_b=gm_ln_b, gm_ws=gm_ws[0], gm_bs=gm_bs[0],
             mix_out_gain=mix_out_gain, norm2_gain=norm2_gain, ffn_dw_w=ffn_w_full, ffn_dw_b=ffn_dw_b,
             final_gain=final_gain[None])
    grad_x, g, d_mod, loss, in_flight = _local_step(
        x[0], loss_target[0], mod, p, w_in_g, w_out_g.reshape(D_MODEL, D_MODEL), w_up_part, w_down_part)

    n_mod = d_mod.shape[1]
    dmod_rows = lax.dynamic_update_slice(jnp.zeros((N_DEV, n_mod), F32), d_mod, (me, 0))
    g["b_ada"] = d_mod
    small = _pack([g[k] for k in SMALL_REPLICATED] + [g[k] for k in SMALL_SHARDED] + [dmod_rows, loss[0, :1]], PACK_ROWS)
    (land_w_in, land_w_out), small = _final_comm([in_flight["w_in16"], in_flight["w_out16"]], small)
    pos = jnp.stack(_coords()).astype(jnp.int32)
    halves = _scatter_sum(pos, [g["w_in"], g["w_out"], g["w_up"], g["w_down"]],
                          [land_w_in, land_w_out, in_flight["land_w_up"], in_flight["land_w_down"]])
    full = _swap_halves(halves)
    grads = dict(w_in=full[0].reshape(w_in.shape[1:]), w_out=full[1].reshape(w_out.shape[1:]),
                 w_up=full[2].reshape(w_up.shape[1:]), w_down=full[3].reshape(w_down.shape[1:]))

    small_shapes = ([weights[k].shape for k in SMALL_REPLICATED] + [(CONV_K, D_HALF), (FFN_K, 2 * D_FF)]
                    + [(N_DEV, n_mod), (1,)])
    *small_grads, conv_w_grad, ffn_w_grad, dmod_all, loss_sum = _unpack(small, small_shapes)
    grads.update(zip(SMALL_REPLICATED, small_grads))
    grads["conv_dw_w"] = lax.dynamic_slice(conv_w_grad, (0, shard * conv_dw_w.shape[2]), conv_dw_w.shape[1:])[None]
    grads["ffn_dw_w"] = lax.dynamic_slice(ffn_w_grad, (0, shard * ffn_dw_w.shape[2]), ffn_dw_w.shape[1:])[None]

    delta, new_m, new_v = {}, {}, {}
    for name, block_rows in (("w_in", 256), ("w_out", 128), ("w_up", 256), ("w_down", 352)):
        delta[name], new_m[name], new_v[name] = [a[None] for a in _adamw(
            "adamw_" + name, weights[name][0], grads[name], mom1[name][0], mom2[name][0], block_rows)]
        grads[name] = grads[name][None]
    dmod_sh = lax.dynamic_slice(dmod_all, (0, shard * ada_cols), (N_DEV, ada_cols))
    pad8 = ((0, 16 - N_DEV), (0, 0))
    grads["w_ada"], delta["w_ada"], new_m["w_ada"], new_v["w_ada"] = [a[None] for a in _adamw_ada(
        jnp.pad(c_all, pad8), jnp.pad(dmod_sh, pad8), w_ada[0], m_w_ada[0], v_w_ada[0], 256)]
    small_names = SMALL_REPLICATED + SMALL_SHARDED

    def two_d(a):
        return a.reshape(1, -1) if a.ndim == 1 else a

    small_out = _adamw_many(*[[two_d(d[k]) for k in small_names] for d in (weights, grads, mom1, mom2)])
    for d, arrs in zip((delta, new_m, new_v), small_out):
        d.update({k: a.reshape(weights[k].shape) for k, a in zip(small_names, arrs)})

    return (loss_sum.reshape(()), grad_x[None], *[grads[k] for k in WEIGHT_ORDER], *[delta[k] for k in WEIGHT_ORDER],
            *[new_m[k] for k in WEIGHT_ORDER], *[new_v[k] for k in WEIGHT_ORDER])
```

```python
import functools

import jax
import jax.numpy as jnp
from jax import lax
from jax.experimental import pallas as pl
from jax.experimental.pallas import tpu as pltpu

F32 = jnp.float32
BF16 = jnp.bfloat16

D_MODEL = 1024
D_HALF = 512
D_FF = 2816
CONV_K = 31
FFN_K = 3
CHUNK = 128
N_HEADS = 8
HEAD_DIM = 64
N_SHARD = 4
N_DEV = 8
RMS_EPS = 1e-6
LN_EPS = 1e-5
ADAM_LR, ADAM_B1, ADAM_B2, ADAM_EPS, ADAM_WD, ADAM_STEP = 0.001, 0.9, 0.999, 1e-08, 0.01, 10

TILE = 256
HALO = 32
FFN_HALO = 16
FFN_BLK = 256
UP_SHARD = 2 * D_FF // N_SHARD
VMEM_LIMIT_BYTES = 56 * 1024 * 1024
FFN_VMEM_LIMIT_BYTES = 58 * 1024 * 1024

ANY = pl.BlockSpec(memory_space=pl.ANY)
NT_DIMS = (((1,), (1,)), ((), ()))
TN_DIMS = (((0,), (0,)), ((), ()))


def _full(shape):
    return pl.BlockSpec(shape, lambda i: (0,) * len(shape))


def _nn(a, b):
    return jnp.dot(a, b, preferred_element_type=F32)


def _nt(a, b):
    return lax.dot_general(a, b, NT_DIMS, preferred_element_type=F32)


def _tn(a, b):
    return lax.dot_general(a, b, TN_DIMS, preferred_element_type=F32)


def _colsum(a):
    return jnp.sum(a, axis=0, keepdims=True)


def _params(semantics=("arbitrary",)):
    return pltpu.CompilerParams(dimension_semantics=semantics, vmem_limit_bytes=VMEM_LIMIT_BYTES)


def _rms(v, gain):
    return v * lax.rsqrt(jnp.mean(v * v, axis=-1, keepdims=True) + RMS_EPS) * gain


def _layer_norm(v, gain, bias):
    mu = jnp.mean(v, axis=-1, keepdims=True)
    var = jnp.mean(jnp.square(v - mu), axis=-1, keepdims=True)
    return (v - mu) * lax.rsqrt(var + LN_EPS) * gain + bias


def _mod_norm(v, gain, scale, shift):
    return _rms(v, gain) * (1.0 + scale) + shift


def _conv_branch(a1, ln_g, ln_b, out_gain):
    a2 = _layer_norm(a1, ln_g, ln_b)
    return _rms(a2 * jax.nn.sigmoid(a2), out_gain)


def _gate_branch(gu, sp, out_gain):
    return _rms(jax.nn.gelu(gu) * sp, out_gain)


def _gv_norm(gv, ln_g, ln_b):
    return _layer_norm(jax.nn.gelu(gv), ln_g, ln_b)


def _head_pair_matmul(wp_ref, v):
    lane = lax.broadcasted_iota(jnp.int32, (CHUNK, CHUNK), 1)
    rows = []
    for n in range(v.shape[0] // CHUNK):
        cols = []
        for j in range(N_HEADS // 2):
            r = _nn(wp_ref[j], v[n * CHUNK:(n + 1) * CHUNK, j * CHUNK:(j + 1) * CHUNK])
            cols.append(jnp.where(lane < HEAD_DIM, r[:CHUNK], r[CHUNK:]))
        rows.append(jnp.concatenate(cols, axis=1))
    return jnp.concatenate(rows, axis=0)


def _tile_bias(bs, tokens):
    return jnp.concatenate([bs] * (tokens // CHUNK), axis=0)


FORWARD_LEAD = 8


def _fwd_mixer(x, vec, conv_w, wpair, bs_full, w_in_g, w_out_g, late_parts):
    seq = x.shape[0]
    n_tiles = seq // TILE
    t = TILE
    n_late = len(late_parts)
    forward_step = max(n_tiles - FORWARD_LEAD, 0)
    names = ["norm1_gain", "sc1", "sh1", "gt1", "conv_dw_b", "conv_ln_g", "conv_ln_b", "gm_ln_g", "gm_ln_b",
             "mix_out_gain"]
    vecs = [vec[k] for k in names]

    def body(x_ref, g1, sc1, sh1, gt1, cb, clg, clb, vg, vb, mg, cw, wp, bs, win_hbm, wout_hbm, *rest):
        late = rest[n_late:2 * n_late]
        z_ref, a1_ref, sp_ref, y_ref, o1_ref, x2_ref = rest[2 * n_late:2 * n_late + 6]
        win_v, wout_v, halo, bank, sem, send_sems, recv_sems = rest[2 * n_late + 6:]
        i = pl.program_id(0)
        mx, my, mc = _coords()
        shard = 2 * mx + my

        def half(w, which):
            h = late[w].shape[1] // 2
            return pl.ds(pl.multiple_of(which * h, 16), h)

        def chip_of(j):
            return 2 * _flip(mx, CHIP_FLIPS[j][0]) + _flip(my, CHIP_FLIPS[j][1])

        def ici_copy(w, j, slot):
            rows = late[w].at[slot, half(w, mc)]
            return pltpu.make_async_remote_copy(
                src_ref=rows, dst_ref=rows, send_sem=send_sems.at[w, j], recv_sem=recv_sems.at[w, j],
                device_id=(_flip(mx, CHIP_FLIPS[j][0]), _flip(my, CHIP_FLIPS[j][1]), mc), device_id_type=MESH)

        def d2d_copy(w, j, which):
            rows = late[w].at[chip_of(j), half(w, which)]
            return pltpu.make_async_remote_copy(
                src_ref=rows, dst_ref=rows, send_sem=send_sems.at[w, len(CHIP_FLIPS) + j],
                recv_sem=recv_sems.at[w, len(CHIP_FLIPS) + j], device_id=(mx, my, 1 - mc), device_id_type=MESH)

        pairs = [(w, j) for w in range(n_late) for j in range(len(CHIP_FLIPS))]

        @pl.when(i == 0)
        def _():
            for w, j in pairs:
                ici_copy(w, j, shard).start()
            cps = [pltpu.make_async_copy(win_hbm, win_v, sem.at[0]),
                   pltpu.make_async_copy(wout_hbm, wout_v, sem.at[1])]
            for cp in cps:
                cp.start()
            for cp in cps:
                cp.wait()
            halo[...] = jnp.zeros_like(halo)

        @pl.when(i == forward_step)
        def _():
            for w, j in pairs:
                ici_copy(w, j, chip_of(j)).wait_recv()
                d2d_copy(w, j, mc).start()

        xv = x_ref[...]
        h1b = _mod_norm(xv, g1[...], sc1[...], sh1[...]).astype(BF16)
        zs = [_nn(h1b, win_v[k]) for k in range(N_SHARD)]
        for k in range(N_SHARD):
            z_ref[:, k * D_HALF:(k + 1) * D_HALF] = zs[k]
        ca, cg, gu, gv = zs
        a0 = ca * jax.nn.sigmoid(cg)
        ext = jnp.concatenate([halo[...], a0], axis=0)
        halo[...] = a0[t - HALO:]
        bank[0] = ext
        for b in range(1, 8):
            bank[b] = pltpu.roll(ext, b, axis=0)
        a1 = jnp.zeros((t, D_HALF), F32) + cb[...]
        for s in range(CONV_K):
            q, b = divmod(s, 8)
            a1 = a1 + bank[b, pl.ds(HALO - 8 * q, t), :] * cw[pl.ds(CONV_K - 1 - s, 1), :]
        a1_ref[...] = a1
        mgv = mg[...]
        ya = _conv_branch(a1, clg[...], clb[...], mgv[:, :D_HALF])
        gvn = _gv_norm(gv, vg[...], vb[...]).astype(BF16)
        sp = _head_pair_matmul(wp, gvn) + _tile_bias(bs[...], t)
        sp_ref[...] = sp
        yg = _gate_branch(gu, sp, mgv[:, D_HALF:])
        yb = jnp.concatenate([ya, yg], axis=1).astype(BF16)
        y_ref[...] = yb
        o1 = _nn(yb, wout_v[...])
        o1_ref[...] = o1
        x2_ref[...] = xv + gt1[...] * o1

        @pl.when(i == n_tiles - 1)
        def _():
            for w, j in pairs:
                d2d_copy(w, j, 1 - mc).wait_recv()
            for w, j in pairs:
                ici_copy(w, j, shard).wait_send()
                d2d_copy(w, j, mc).wait_send()

    def row(width):
        return pl.BlockSpec((t, width), lambda i: (i, 0))

    out_shape = [jax.ShapeDtypeStruct((seq, 4 * D_HALF), F32), jax.ShapeDtypeStruct((seq, D_HALF), F32),
                 jax.ShapeDtypeStruct((seq, D_HALF), F32), jax.ShapeDtypeStruct((seq, D_MODEL), BF16),
                 jax.ShapeDtypeStruct((seq, D_MODEL), F32), jax.ShapeDtypeStruct((seq, D_MODEL), F32)]
    n_in = 1 + len(vecs) + 3 + 2
    sem_shape = (n_late, 2 * len(CHIP_FLIPS))
    outs = pl.pallas_call(
        body, grid=(n_tiles,), name="fwd_mixer",
        in_specs=[row(D_MODEL)] + [_full(v.shape) for v in vecs]
        + [_full(conv_w.shape), _full(wpair.shape), _full(bs_full.shape), ANY, ANY] + [ANY] * n_late,
        out_specs=[ANY] * n_late + [row(4 * D_HALF), row(D_HALF), row(D_HALF), row(D_MODEL), row(D_MODEL),
                                    row(D_MODEL)],
        out_shape=[jax.ShapeDtypeStruct(a.shape, a.dtype) for a in late_parts] + out_shape,
        input_output_aliases={n_in + w: w for w in range(n_late)},
        scratch_shapes=[pltpu.VMEM(w_in_g.shape, BF16), pltpu.VMEM(w_out_g.shape, BF16),
                        pltpu.VMEM((HALO, D_HALF), F32), pltpu.VMEM((8, t + HALO, D_HALF), F32),
                        pltpu.SemaphoreType.DMA((2,)), pltpu.SemaphoreType.DMA(sem_shape),
                        pltpu.SemaphoreType.DMA(sem_shape)],
        compiler_params=_params(),
    )(x, *vecs, conv_w, wpair, bs_full, w_in_g, w_out_g, *late_parts)
    return outs[n_late:], outs[:n_late]


def _interleave_matrices():
    row = jnp.arange(TILE)
    token_of_row = (row % 8) * (TILE // 8) + row // 8
    to_inter = (token_of_row[:, None] == row[None, :]).astype(BF16)
    return to_inter, jnp.transpose(to_inter)


def _ffn(x2, target, norm2_gain, sc2, sh2, ffn_w, ffn_b, gt2, final_gain, w_up_g, w_down_g, to_inter, to_natural):
    seq = x2.shape[0]
    n_tiles = seq // TILE
    t = TILE
    n_blk = D_FF // FFN_BLK
    inv_d = 1.0 / D_MODEL

    def final_norm(x3, gain):
        return _rms(x3, gain)

    def body(x2_ref, x2h_ref, tgt_ref, g2, sc2_ref, sh2_ref, fw, fb, gt2_ref, fg, pm_ref, pmt_ref, wup_hbm, wd_hbm,
             du_ref, dx2_ref, dfw_ref, dfb_ref, dfg_ref, dgt2_ref, dg2_ref, dsc2_ref, dsh2_ref, loss_ref, dwd_hbm,
             dwd16_hbm, wup_v, wd_v, dwd_acc, carry, u_s, sil_s, vds_s, f_s, du_s, sem):
        i = pl.program_id(0)
        tile = n_tiles - 1 - i
        sublane = lax.broadcasted_iota(jnp.int32, (8, FFN_BLK), 0)

        @pl.when(i == 0)
        def _():
            cps = [pltpu.make_async_copy(wd_hbm, wd_v, sem.at[0])]
            cps += [pltpu.make_async_copy(wup_hbm.at[k], wup_v.at[:, pl.ds(k * UP_SHARD, UP_SHARD)], sem.at[3 + k])
                    for k in range(N_SHARD)]
            for cp in cps:
                cp.start()
            for cp in cps:
                cp.wait()
            dwd_acc[...] = jnp.zeros_like(dwd_acc)
            carry[...] = jnp.zeros_like(carry)
            dfw_ref[...] = jnp.zeros_like(dfw_ref)
            dfb_ref[...] = jnp.zeros_like(dfb_ref)
            dfg_ref[...] = jnp.zeros_like(dfg_ref)
            dgt2_ref[...] = jnp.zeros_like(dgt2_ref)
            dg2_ref[...] = jnp.zeros_like(dg2_ref)
            dsc2_ref[...] = jnp.zeros_like(dsc2_ref)
            dsh2_ref[...] = jnp.zeros_like(dsh2_ref)
            loss_ref[...] = jnp.zeros_like(loss_ref)

        def cols_of(j):
            return pl.ds(j * FFN_BLK, FFN_BLK), pl.ds(D_FF + j * FFN_BLK, FFN_BLK)

        def wrap_down(last, before):
            return jnp.where(sublane == 0, pltpu.roll(before, 1, axis=0), pltpu.roll(last, 1, axis=0))

        def wrap_up(first, after):
            return jnp.where(sublane == 7, pltpu.roll(after, 7, axis=0), pltpu.roll(first, 7, axis=0))

        x2v = x2_ref[...]
        h2, h2_vjp = jax.vjp(_mod_norm, x2v, g2[...], sc2_ref[...], sh2_ref[...])
        h2b = h2.astype(BF16)
        h2_before = _mod_norm(x2h_ref[...], g2[...], sc2_ref[...], sh2_ref[...]).astype(BF16)
        lhs = jnp.concatenate([_nn(pm_ref[...], h2b).astype(BF16), h2_before], axis=0)

        def up(j):
            cv, cg = cols_of(j)
            return _nn(lhs, wup_v[:, cv]), _nn(lhs, wup_v[:, cg])

        def conv(both, cols):
            cur = both[:t]
            u_s[:, cols] = cur.astype(BF16)
            before = jnp.where(tile > 0, both[t:], 0.0)
            w1 = wrap_down(cur[t - 8:], before)
            w2 = wrap_down(cur[t - 16:t - 8], pltpu.roll(before, 1, axis=0))
            back1 = jnp.concatenate([w1, cur[:t - 8]], axis=0)
            back2 = jnp.concatenate([w2, w1, cur[:t - 16]], axis=0)
            return (fb[:, cols] + cur * fw[pl.ds(2, 1), cols] + back1 * fw[pl.ds(1, 1), cols]
                    + back2 * fw[pl.ds(0, 1), cols])

        pm_t = pmt_ref[...]

        def to_natural_f32(a):
            hi = a.astype(BF16)
            rest = a - hi.astype(F32)
            mid = rest.astype(BF16)
            low = (rest - mid.astype(F32)).astype(BF16)
            return _nn(jnp.concatenate([pm_t, pm_t, pm_t], axis=1), jnp.concatenate([hi, mid, low], axis=0))

        o2 = jnp.zeros((t, D_MODEL), F32)
        ahead_uv = up(0)
        for j in range(n_blk):
            cv, cg = cols_of(j)
            both_v, both_g = ahead_uv
            if j + 1 < n_blk:
                ahead_uv = up(j + 1)
            val, gate = conv(both_v, cv), conv(both_g, cg)
            sig = jax.nn.sigmoid(gate)
            sil = gate * sig
            fb16 = (sil * val).astype(BF16)
            sil_s[:, cv] = sil
            vds_s[:, cv] = val * (sig + sil * (1.0 - sig))
            f_s[:, cv] = fb16
            o2 = o2 + _nn(fb16, wd_v[pl.ds(j * FFN_BLK, FFN_BLK), :])
        o2 = to_natural_f32(o2)

        gt2v = gt2_ref[...]
        x3 = x2v + gt2v * o2
        out, out_vjp = jax.vjp(final_norm, x3, fg[...])
        diff = out - tgt_ref[...]
        loss_ref[...] += jnp.zeros_like(loss_ref) + 0.5 * inv_d * jnp.sum(diff * diff)
        dx3, dfg = out_vjp(diff * inv_d)
        dfg_ref[...] += dfg
        dgt2_ref[...] += _colsum(dx3 * o2)
        do2b = _nn(pm_ref[...], (gt2v * dx3).astype(BF16)).astype(BF16)

        for j in range(n_blk):
            cv, cg = cols_of(j)
            rows = pl.ds(j * FFN_BLK, FFN_BLK)
            df = _nt(do2b, wd_v[rows, :])
            dwd_acc[rows, :] += _tn(f_s[:, cv], do2b)
            for dd, cols in ((df * sil_s[:, cv], cv), (df * vds_s[:, cv], cg)):
                dfb_ref[:, cols] += _colsum(dd)
                nxt = carry[:, cols]
                w1 = wrap_up(dd[:8], nxt[:8])
                w2 = wrap_up(dd[8:16], nxt[8:])
                ahead = (dd, jnp.concatenate([dd[8:], w1], axis=0), jnp.concatenate([dd[16:], w1, w2], axis=0))
                carry[:, cols] = dd[:16]
                uv = u_s[:, cols].astype(F32)
                du = jnp.zeros((t, FFN_BLK), F32)
                for s in range(FFN_K):
                    du = du + ahead[s] * fw[pl.ds(FFN_K - 1 - s, 1), cols]
                    dfw_ref[pl.ds(FFN_K - 1 - s, 1), cols] += _colsum(ahead[s] * uv)
                du_s[:, cols] = du.astype(BF16)
        du16 = _nn(pm_t, du_s[...]).astype(BF16)
        du_ref[...] = du16
        dx2, dg2, dsc2, dsh2 = h2_vjp(_nt(du16, wup_v[...]))
        dx2_ref[...] = dx3 + dx2
        dg2_ref[...] += dg2
        dsc2_ref[...] += dsc2
        dsh2_ref[...] += dsh2

        @pl.when(i == n_tiles - 1)
        def _():
            cp = pltpu.make_async_copy(dwd_acc, dwd_hbm, sem.at[1])
            cp.start()
            wd_v[...] = dwd_acc[...].astype(BF16)
            cp16 = pltpu.make_async_copy(wd_v, dwd16_hbm, sem.at[2])
            cp16.start()
            cp.wait()
            cp16.wait()

    def rev(width):
        return pl.BlockSpec((t, width), lambda i: (n_tiles - 1 - i, 0))

    assert FFN_K == 3
    halo_spec = pl.BlockSpec((8, D_MODEL), lambda i: (jnp.maximum((n_tiles - 1 - i) * (t // 8) - 1, 0), 0))
    vec_spec = _full((1, D_MODEL))
    out_shape = [jax.ShapeDtypeStruct((seq, 2 * D_FF), BF16), jax.ShapeDtypeStruct((seq, D_MODEL), F32),
                 jax.ShapeDtypeStruct((FFN_K, 2 * D_FF), F32), jax.ShapeDtypeStruct((1, 2 * D_FF), F32),
                 jax.ShapeDtypeStruct((1, D_MODEL), F32), jax.ShapeDtypeStruct((1, D_MODEL), F32),
                 jax.ShapeDtypeStruct((1, D_MODEL), F32), jax.ShapeDtypeStruct((1, D_MODEL), F32),
                 jax.ShapeDtypeStruct((1, D_MODEL), F32),
                 jax.ShapeDtypeStruct((1, 128), F32), jax.ShapeDtypeStruct((D_FF, D_MODEL), F32),
                 jax.ShapeDtypeStruct((D_FF, D_MODEL), BF16)]
    return pl.pallas_call(
        body, grid=(n_tiles,), name="ffn",
        in_specs=[rev(D_MODEL), halo_spec, rev(D_MODEL), vec_spec, vec_spec, vec_spec, _full(ffn_w.shape),
                  _full(ffn_b.shape), _full(gt2.shape), _full(final_gain.shape), _full(to_inter.shape),
                  _full(to_natural.shape), ANY, ANY],
        out_specs=[rev(2 * D_FF), rev(D_MODEL), _full((FFN_K, 2 * D_FF)), _full((1, 2 * D_FF)), vec_spec, vec_spec,
                   vec_spec, vec_spec, vec_spec, _full((1, 128)), ANY, ANY],
        out_shape=out_shape,
        scratch_shapes=[pltpu.VMEM((D_MODEL, 2 * D_FF), BF16), pltpu.VMEM((D_FF, D_MODEL), BF16),
                        pltpu.VMEM((D_FF, D_MODEL), F32), pltpu.VMEM((FFN_HALO, 2 * D_FF), F32),
                        pltpu.VMEM((t, 2 * D_FF), BF16), pltpu.VMEM((t, D_FF), F32), pltpu.VMEM((t, D_FF), F32),
                        pltpu.VMEM((t, D_FF), BF16), pltpu.VMEM((t, 2 * D_FF), BF16),
                        pltpu.SemaphoreType.DMA((3 + N_SHARD,))],
        compiler_params=pltpu.CompilerParams(dimension_semantics=("arbitrary",), vmem_limit_bytes=FFN_VMEM_LIMIT_BYTES),
    )(x2, x2, target, norm2_gain, sc2, sh2, ffn_w, ffn_b, gt2, final_gain, to_inter, to_natural, w_up_g, w_down_g)


def _scatter_copies(src16, land, send_sems, recv_sems):
    x, y, c = _coords()
    h = src16.shape[1] // 2
    copies = []
    for f, flip in enumerate(PEER_FLIPS):
        tx, ty, tc = _flip(x, flip[0]), _flip(y, flip[1]), _flip(c, flip[2])
        copies.append(pltpu.make_async_remote_copy(
            src_ref=src16.at[2 * tx + ty, pl.ds(pl.multiple_of(tc * h, 16), h)], dst_ref=land.at[f],
            send_sem=send_sems.at[f], recv_sem=recv_sems.at[f], device_id=(tx, ty, tc), device_id_type=MESH))
    return copies


def _land_shape(src16):
    return jax.ShapeDtypeStruct((len(PEER_FLIPS), src16.shape[1] // 2, src16.shape[2]), BF16)


UP_TILE = 512


def _bwd_up(du, x2, norm2_gain, sc2, sh2, dwd16):
    seq = x2.shape[0]
    t = UP_TILE if seq % UP_TILE == 0 else TILE
    n_tiles = seq // t
    acc_shape = (N_SHARD, D_MODEL, UP_SHARD)

    def body(du_ref, x2_ref, g2, sc2_ref, sh2_ref, dwd16_hbm, dwup_hbm, dwup16_hbm, land_hbm,
             stage16, dwup_acc, sem, send_sems, recv_sems):
        i = pl.program_id(0)

        @pl.when(i == 0)
        def _():
            for cp in _scatter_copies(dwd16_hbm, land_hbm, send_sems, recv_sems):
                cp.start()
            dwup_acc[...] = jnp.zeros_like(dwup_acc)

        h2b = _mod_norm(x2_ref[...], g2[...], sc2_ref[...], sh2_ref[...]).astype(BF16)
        for k in range(N_SHARD):
            dwup_acc[k] += _tn(h2b, du_ref[:, k * UP_SHARD:(k + 1) * UP_SHARD])

        @pl.when(i == n_tiles - 1)
        def _():
            cp = pltpu.make_async_copy(dwup_acc, dwup_hbm, sem.at[0])
            cp.start()
            for k in range(N_SHARD):
                stage16[k] = dwup_acc[k].astype(BF16)
            cp16 = pltpu.make_async_copy(stage16, dwup16_hbm, sem.at[1])
            cp16.start()
            cp.wait()
            cp16.wait()
            for rc in _scatter_copies(dwd16_hbm, land_hbm, send_sems, recv_sems):
                rc.wait()

    def row(width):
        return pl.BlockSpec((t, width), lambda i: (i, 0))

    n_peer = len(PEER_FLIPS)
    return pl.pallas_call(
        body, grid=(n_tiles,), name="bwd_up",
        in_specs=[row(2 * D_FF), row(D_MODEL), _full((1, D_MODEL)), _full((1, D_MODEL)), _full((1, D_MODEL)), ANY],
        out_specs=[ANY, ANY, ANY],
        out_shape=[jax.ShapeDtypeStruct(acc_shape, F32), jax.ShapeDtypeStruct(acc_shape, BF16), _land_shape(dwd16)],
        scratch_shapes=[pltpu.VMEM(acc_shape, BF16), pltpu.VMEM(acc_shape, F32), pltpu.SemaphoreType.DMA((2,)),
                        pltpu.SemaphoreType.DMA((n_peer,)), pltpu.SemaphoreType.DMA((n_peer,))],
        compiler_params=_params(),
    )(du, x2, norm2_gain, sc2, sh2, dwd16)


def _bwd_mixer(dx2, x, z, a1, sp, yb, o1, vec, conv_w, wpair, wpair_t, causal_mask, w_in_g, w_out_g, dwup16):
    seq = x.shape[0]
    n_tiles = seq // TILE
    t = TILE
    names = ["norm1_gain", "sc1", "sh1", "gt1", "conv_ln_g", "conv_ln_b", "gm_ln_g", "gm_ln_b", "mix_out_gain"]
    vecs = [vec[k] for k in names]

    def body(dx2_ref, x_ref, z_ref, a1_ref, sp_ref, y_ref, o1_ref, g1, sc1, sh1, gt1, clg, clb, vg, vb, mg,
             cw, wp, wpt, mask_ref, win_hbm, wout_hbm, dwup16_hbm,
             gx_ref, dg1_ref, dsc1_ref, dsh1_ref, dgt1_ref, dcw_ref, dcb_ref, dclg_ref, dclb_ref, dvg_ref, dvb_ref,
             dmg_ref, dws_ref, dbs_ref, dwin_hbm, dwout_hbm, land_hbm, dwin16_hbm, dwout16_hbm,
             win_v, wout_v, dwin_acc, dwout_acc, carry, bank, dbs_acc, lwin, lwout, sem, send_sems, recv_sems,
             pair_send, pair_recv):
        i = pl.program_id(0)
        small = [dg1_ref, dsc1_ref, dsh1_ref, dgt1_ref, dcw_ref, dcb_ref, dclg_ref, dclb_ref, dvg_ref, dvb_ref,
                 dmg_ref, dws_ref, dbs_acc]

        @pl.when(i == 0)
        def _():
            for cp in _scatter_copies(dwup16_hbm, land_hbm, send_sems, recv_sems):
                cp.start()
            cps = [pltpu.make_async_copy(win_hbm, win_v, sem.at[0]),
                   pltpu.make_async_copy(wout_hbm, wout_v, sem.at[1])]
            for cp in cps:
                cp.start()
            for cp in cps:
                cp.wait()
            dwin_acc[...] = jnp.zeros_like(dwin_acc)
            dwout_acc[...] = jnp.zeros_like(dwout_acc)
            carry[...] = jnp.zeros_like(carry)
            for ref in small:
                ref[...] = jnp.zeros_like(ref)

        dx2v = dx2_ref[...]
        gt1v = gt1[...]
        dgt1_ref[...] += _colsum(dx2v * o1_ref[...])
        do1b = (gt1v * dx2v).astype(BF16)
        dy = _nt(do1b, wout_v[...])
        dwout_acc[...] += _tn(y_ref[...], do1b)

        mgv = mg[...]
        _, conv_vjp = jax.vjp(_conv_branch, a1_ref[...], clg[...], clb[...], mgv[:, :D_HALF])
        da1, dclg, dclb, dmg_a = conv_vjp(dy[:, :D_HALF])
        dclg_ref[...] += dclg
        dclb_ref[...] += dclb
        gu = z_ref[:, 2 * D_HALF:3 * D_HALF]
        gv = z_ref[:, 3 * D_HALF:]
        spv = sp_ref[...]
        _, gate_vjp = jax.vjp(_gate_branch, gu, spv, mgv[:, D_HALF:])
        dgu, dsp, dmg_g = gate_vjp(dy[:, D_HALF:])
        dmg_ref[...] += jnp.concatenate([dmg_a, dmg_g], axis=1)
        gvn, gv_vjp = jax.vjp(_gv_norm, gv, vg[...], vb[...])
        gvnb = gvn.astype(BF16)
        dspb = dsp.astype(BF16)
        dgvn = _head_pair_matmul(wpt, dspb)
        dgv, dvg, dvb = gv_vjp(dgvn)
        dvg_ref[...] += dvg
        dvb_ref[...] += dvb
        lane = lax.broadcasted_iota(jnp.int32, (CHUNK, CHUNK), 1)
        dbs = jnp.zeros((CHUNK, D_HALF), F32)
        for n in range(t // CHUNK):
            rows = slice(n * CHUNK, (n + 1) * CHUNK)
            dbs = dbs + dsp[rows, :]
            for j in range(N_HEADS // 2):
                cols = slice(j * CHUNK, (j + 1) * CHUNK)
                blk = dspb[rows, cols]
                zero = jnp.zeros_like(blk)
                vblk = gvnb[rows, cols]
                dws_ref[2 * j] += _nt(jnp.where(lane < HEAD_DIM, blk, zero), vblk)
                dws_ref[2 * j + 1] += _nt(jnp.where(lane < HEAD_DIM, zero, blk), vblk)
        dbs_acc[...] += dbs

        ca = z_ref[:, :D_HALF]
        cg = z_ref[:, D_HALF:2 * D_HALF]
        sig = jax.nn.sigmoid(cg)
        a0 = ca * sig
        ext = jnp.concatenate([da1, carry[...]], axis=0)
        carry[...] = da1[:HALO]
        bank[0] = ext
        for b in range(1, 8):
            bank[b] = pltpu.roll(ext, t + HALO - b, axis=0)
        dcb_ref[...] += _colsum(da1)
        da0 = jnp.zeros((t, D_HALF), F32)
        for s in range(CONV_K):
            q, b = divmod(s, 8)
            shifted = bank[b, pl.ds(8 * q, t), :]
            da0 = da0 + shifted * cw[pl.ds(CONV_K - 1 - s, 1), :]
            dcw_ref[pl.ds(CONV_K - 1 - s, 1), :] += _colsum(shifted * a0)
        dca = da0 * sig
        dcg = da0 * ca * sig * (1.0 - sig)

        h1, h1_vjp = jax.vjp(_mod_norm, x_ref[...], g1[...], sc1[...], sh1[...])
        h1b = h1.astype(BF16)
        dh1 = jnp.zeros((t, D_MODEL), F32)
        for k, dzk in enumerate((dca, dcg, dgu, dgv)):
            dzb = dzk.astype(BF16)
            dh1 = dh1 + _nt(dzb, win_v[k])
            dwin_acc[k] += _tn(h1b, dzb)
        dx, dg1, dsc1, dsh1 = h1_vjp(dh1)
        gx_ref[...] = dx2v + dx
        dg1_ref[...] += dg1
        dsc1_ref[...] += dsc1
        dsh1_ref[...] += dsh1

        @pl.when(i == n_tiles - 1)
        def _():
            for h in range(N_HEADS):
                dws_ref[h] = dws_ref[h] * mask_ref[...]
            head_of_lane = lax.broadcasted_iota(jnp.int32, (N_HEADS, D_HALF), 1) // HEAD_DIM
            pick = (head_of_lane == lax.broadcasted_iota(jnp.int32, (N_HEADS, D_HALF), 0)).astype(F32)
            dbs_ref[...] = lax.dot_general(pick, dbs_acc[...], NT_DIMS, precision=lax.Precision.HIGHEST,
                                           preferred_element_type=F32)
            for k in range(N_SHARD):
                win_v[k] = dwin_acc[k].astype(BF16)
            wout_v[...] = dwout_acc[...].astype(BF16)
            mx, my, mc = _coords()
            h_in, h_out = dwin_acc.shape[1] // 2, dwout_acc.shape[0] // (2 * N_SHARD)

            def in_rows(ref, k, which):
                return ref.at[k, pl.ds(pl.multiple_of(which * h_in, 16), h_in), :]

            def out_rows(ref, k, which):
                return ref.at[pl.ds(pl.multiple_of((2 * k + which) * h_out, 16), h_out), :]

            pairs = ((win_v, dwin_acc, lwin, in_rows, dwin_hbm, dwin16_hbm),
                     (wout_v, dwout_acc, lwout, out_rows, dwout_hbm, dwout16_hbm))
            swaps = [pltpu.make_async_remote_copy(
                src_ref=rows_of(v16, k, 1 - mc), dst_ref=land.at[k], send_sem=pair_send.at[w, k],
                recv_sem=pair_recv.at[w, k], device_id=(mx, my, 1 - mc), device_id_type=MESH)
                for w, (v16, _, land, rows_of, _, _) in enumerate(pairs) for k in range(N_SHARD)]
            for cp in swaps:
                cp.start()
            for cp in swaps:
                cp.wait()
            outs = []
            for w, (v16, acc, land, rows_of, half_hbm, half16_hbm) in enumerate(pairs):
                for k in range(N_SHARD):
                    total = rows_of(acc, k, mc)[...] + land[k].astype(F32)
                    rows_of(acc, k, 0)[...] = total
                    rows_of(v16, k, 0)[...] = total.astype(BF16)
                    outs.append(pltpu.make_async_copy(rows_of(acc, k, 0), half_hbm.at[k], sem.at[2 + 8 * w + k]))
                    outs.append(pltpu.make_async_copy(rows_of(v16, k, 0), half16_hbm.at[k], sem.at[6 + 8 * w + k]))
            for cp in outs:
                cp.start()
            for cp in outs:
                cp.wait()
            for rc in _scatter_copies(dwup16_hbm, land_hbm, send_sems, recv_sems):
                rc.wait()

    def rev(width):
        return pl.BlockSpec((t, width), lambda i: (n_tiles - 1 - i, 0))

    v1024 = jax.ShapeDtypeStruct((1, D_MODEL), F32)
    v512 = jax.ShapeDtypeStruct((1, D_HALF), F32)
    small_shapes = [v1024, v1024, v1024, v1024, jax.ShapeDtypeStruct((CONV_K, D_HALF), F32), v512, v512, v512, v512,
                    v512, v1024, jax.ShapeDtypeStruct((N_HEADS, CHUNK, CHUNK), F32),
                    jax.ShapeDtypeStruct((N_HEADS, CHUNK), F32)]
    n_peer = len(PEER_FLIPS)
    half_in = (N_SHARD, w_in_g.shape[1] // 2, w_in_g.shape[2])
    half_out = (N_SHARD, w_out_g.shape[0] // (2 * N_SHARD), w_out_g.shape[1])
    return pl.pallas_call(
        body, grid=(n_tiles,), name="bwd_mixer",
        in_specs=[rev(D_MODEL), rev(D_MODEL), rev(4 * D_HALF), rev(D_HALF), rev(D_HALF), rev(D_MODEL),
                  rev(D_MODEL)] + [_full(v.shape) for v in vecs]
        + [_full(conv_w.shape), _full(wpair.shape), _full(wpair_t.shape), _full(causal_mask.shape), ANY, ANY, ANY],
        out_specs=[rev(D_MODEL)] + [_full(s.shape) for s in small_shapes] + [ANY] * 5,
        out_shape=[jax.ShapeDtypeStruct((seq, D_MODEL), F32)] + small_shapes
        + [jax.ShapeDtypeStruct(half_in, F32), jax.ShapeDtypeStruct(half_out, F32), _land_shape(dwup16),
           jax.ShapeDtypeStruct(half_in, BF16), jax.ShapeDtypeStruct(half_out, BF16)],
        scratch_shapes=[pltpu.VMEM(w_in_g.shape, BF16), pltpu.VMEM(w_out_g.shape, BF16),
                        pltpu.VMEM(w_in_g.shape, F32), pltpu.VMEM(w_out_g.shape, F32),
                        pltpu.VMEM((HALO, D_HALF), F32), pltpu.VMEM((8, t + HALO, D_HALF), F32),
                        pltpu.VMEM((CHUNK, D_HALF), F32), pltpu.VMEM(half_in, BF16), pltpu.VMEM(half_out, BF16),
                        pltpu.SemaphoreType.DMA((2 + 4 * N_SHARD,)),
                        pltpu.SemaphoreType.DMA((n_peer,)), pltpu.SemaphoreType.DMA((n_peer,)),
                        pltpu.SemaphoreType.DMA((2, N_SHARD)), pltpu.SemaphoreType.DMA((2, N_SHARD))],
        compiler_params=_params(),
    )(dx2, x, z, a1, sp, yb, o1, *vecs, conv_w, wpair, wpair_t, causal_mask, w_in_g, w_out_g, dwup16)


def _gmlp_operands(gm_ws, gm_bs):
    mask = jnp.tril(jnp.ones((CHUNK, CHUNK), F32))
    ws = gm_ws * mask[None]
    wpair = ws.reshape(N_HEADS // 2, 2 * CHUNK, CHUNK).astype(BF16)
    wpair_t = jnp.swapaxes(ws, 1, 2).reshape(N_HEADS // 2, 2 * CHUNK, CHUNK).astype(BF16)
    bs_full = jnp.repeat(jnp.transpose(gm_bs), HEAD_DIM, axis=1)
    return wpair, wpair_t, bs_full, mask


def _local_step(x, target, mod, p, w_in_g, w_out_g, w_up_part, w_down_part):
    sh1, sc1, gt1, sh2, sc2, gt2 = [mod[:, k * D_MODEL:(k + 1) * D_MODEL] for k in range(6)]
    vec = dict(p, sh1=sh1, sc1=sc1, gt1=gt1, sh2=sh2, sc2=sc2, gt2=gt2)
    wpair, wpair_t, bs_full, mask = _gmlp_operands(p["gm_ws"], p["gm_bs"])

    (z, a1, sp, yb, o1, x2), (w_up_g, w_down_g) = _fwd_mixer(
        x, vec, p["conv_dw_w"], wpair, bs_full, w_in_g, w_out_g, [w_up_part, w_down_part])
    w_down_g = w_down_g.reshape(D_FF, D_MODEL)
    to_inter, to_natural = _interleave_matrices()
    du, dx2, d_ffn_w, d_ffn_b, d_fg, d_gt2, d_g2, d_sc2, d_sh2, loss, d_wd, d_wd16 = _ffn(
        x2, target, p["norm2_gain"], sc2, sh2, p["ffn_dw_w"], p["ffn_dw_b"], gt2, p["final_gain"], w_up_g, w_down_g,
        to_inter, to_natural)
    by_shard = (N_SHARD, -1, D_MODEL)
    d_wup, d_wup16, land_wd = _bwd_up(du, x2, p["norm2_gain"], sc2, sh2, d_wd16.reshape(by_shard))
    (gx, d_g1, d_sc1, d_sh1, d_gt1, d_cw, d_cb, d_clg, d_clb, d_vg, d_vb, d_mg, d_ws, d_bs, d_win, d_wout, land_wup,
     d_win16, d_wout16) = _bwd_mixer(dx2, x, z, a1, sp, yb, o1, vec, p["conv_dw_w"], wpair, wpair_t, mask, w_in_g,
                                     w_out_g, d_wup16)
    d_mod = jnp.concatenate([d_sh1, d_sc1, d_gt1, d_sh2, d_sc2, d_gt2], axis=1)
    grads = dict(norm1_gain=d_g1, conv_dw_w=d_cw, conv_dw_b=d_cb, conv_ln_g=d_clg, conv_ln_b=d_clb, gm_ln_g=d_vg,
                 gm_ln_b=d_vb, gm_ws=d_ws, gm_bs=d_bs, mix_out_gain=d_mg, norm2_gain=d_g2, ffn_dw_w=d_ffn_w,
                 ffn_dw_b=d_ffn_b, final_gain=d_fg, w_in=d_win, w_out=d_wout, w_up=d_wup, w_down=d_wd.reshape(by_shard))
    in_flight = dict(w_in16=d_win16, w_out16=d_wout16, land_w_up=land_wup, land_w_down=land_wd)
    return gx, grads, d_mod, loss, in_flight


MESH = pl.DeviceIdType.MESH
VMEM_SPEC = pl.BlockSpec(memory_space=pltpu.VMEM)
PEER_FLIPS = [(a, b, d) for a in (0, 1) for b in (0, 1) for d in (0, 1)][1:]
CHIP_FLIPS = [(1, 0), (0, 1), (1, 1)]


def _coords():
    return lax.axis_index("x"), lax.axis_index("y"), lax.axis_index("c")


def _flip(v, bit):
    return 1 - v if bit else v


def _rows8(block):
    return pl.ds(pl.multiple_of(8 * block, 8), 8)


def _ada_mod(c_row, w_ada_sh, b_ada_sh):
    cols = w_ada_sh.shape[1]

    def body(c_ref, w_ref, b_ref, call_ref, mod_ref, cpad, modall, send_sems, recv_sems):
        x, y, c = _coords()
        me = 4 * x + 2 * y + c
        cpad[...] = jnp.zeros_like(cpad)
        cpad[pl.ds(0, 1), :] = c_ref[...]

        def gather_copy(j, flip):
            peer = (_flip(x, flip[0]), _flip(y, flip[1]), _flip(c, flip[2]))
            return pltpu.make_async_remote_copy(
                src_ref=cpad, dst_ref=call_ref.at[_rows8(me)], send_sem=send_sems.at[j], recv_sem=recv_sems.at[j],
                device_id=peer, device_id_type=MESH)

        copies = [gather_copy(j, f) for j, f in enumerate(PEER_FLIPS)]
        for cp in copies:
            cp.start()
        call_ref[_rows8(me), :] = cpad[...]
        for cp in copies:
            cp.wait_recv()
        for cp in copies:
            cp.wait_send()
        cv = call_ref[...]
        c_act = (cv * jax.nn.sigmoid(cv)).astype(BF16)
        modall[...] = _nn(c_act, w_ref[...].astype(BF16)) + b_ref[...]

        slot = _rows8(2 * x + y)

        def piece_copy(j, flip):
            tx, ty = _flip(x, flip[0]), _flip(y, flip[1])
            return pltpu.make_async_remote_copy(
                src_ref=modall.at[_rows8(4 * tx + 2 * ty + c)], dst_ref=mod_ref.at[slot],
                send_sem=send_sems.at[len(PEER_FLIPS) + j], recv_sem=recv_sems.at[len(PEER_FLIPS) + j],
                device_id=(tx, ty, c), device_id_type=MESH)

        pieces = [piece_copy(j, f) for j, f in enumerate(CHIP_FLIPS)]
        for cp in pieces:
            cp.start()
        mod_ref[slot, :] = modall[_rows8(me), :]
        for cp in pieces:
            cp.wait_recv()
        for cp in pieces:
            cp.wait_send()

    n_sem = len(PEER_FLIPS) + len(CHIP_FLIPS)
    return pl.pallas_call(
        body, name="ada_mod",
        in_specs=[VMEM_SPEC, VMEM_SPEC, VMEM_SPEC], out_specs=[VMEM_SPEC, VMEM_SPEC],
        out_shape=[jax.ShapeDtypeStruct((8 * N_DEV, D_MODEL), F32), jax.ShapeDtypeStruct((8 * N_SHARD, cols), F32)],
        scratch_shapes=[pltpu.VMEM((8, D_MODEL), F32), pltpu.VMEM((8 * N_DEV, cols), F32),
                        pltpu.SemaphoreType.DMA((n_sem,)), pltpu.SemaphoreType.DMA((n_sem,))],
        compiler_params=pltpu.CompilerParams(vmem_limit_bytes=VMEM_LIMIT_BYTES),
    )(c_row, w_ada_sh, b_ada_sh)


def _gather_weights(shards, filters, n_now):
    n = len(shards)
    nf = len(filters)

    def body(*refs):
        ins, f_ins = refs[:n], refs[n:n + nf]
        outs, f_outs = refs[n + nf:2 * n + nf], refs[2 * n + nf:2 * (n + nf)]
        stage = refs[2 * (n + nf):3 * n + 2 * nf]
        send_sems, recv_sems, local_sems, f_send_sems, f_recv_sems = refs[3 * n + 2 * nf:]
        x, y, c = _coords()
        k = 2 * x + y
        sibling = (x, y, 1 - c)

        def filter_copy(w, j, slot):
            tx, ty = _flip(x, CHIP_FLIPS[j][0]), _flip(y, CHIP_FLIPS[j][1])
            return pltpu.make_async_remote_copy(
                src_ref=f_ins[w], dst_ref=f_outs[w].at[slot], send_sem=f_send_sems.at[w, j],
                recv_sem=f_recv_sems.at[w, j], device_id=(tx, ty, c), device_id_type=MESH)

        def half(w, which):
            h = shards[w].shape[0] // 2
            return pl.ds(pl.multiple_of(which * h, 16), h)

        def ici_copy(w, j, src, slot):
            tx, ty = _flip(x, CHIP_FLIPS[j][0]), _flip(y, CHIP_FLIPS[j][1])
            return pltpu.make_async_remote_copy(
                src_ref=src, dst_ref=outs[w].at[slot, half(w, c)], send_sem=send_sems.at[w, j],
                recv_sem=recv_sems.at[w, j], device_id=(tx, ty, c), device_id_type=MESH)

        def d2d_copy(w, j, slot, which):
            rows = outs[w].at[slot, half(w, which)]
            return pltpu.make_async_remote_copy(
                src_ref=rows, dst_ref=rows, send_sem=send_sems.at[w, len(CHIP_FLIPS) + j],
                recv_sem=recv_sems.at[w, len(CHIP_FLIPS) + j], device_id=sibling, device_id_type=MESH)

        def chip_of(j):
            return 2 * _flip(x, CHIP_FLIPS[j][0]) + _flip(y, CHIP_FLIPS[j][1])

        local, first, passed = [], [], []
        for w in range(nf):
            local.append(pltpu.make_async_copy(f_ins[w], f_outs[w].at[k], local_sems.at[n + w]))
            local[-1].start()
            for j in range(len(CHIP_FLIPS)):
                first.append(filter_copy(w, j, k))
                first[-1].start()
        for w in range(n):
            stage[w][...] = ins[w][...].astype(BF16)
            local.append(pltpu.make_async_copy(stage[w], outs[w].at[k], local_sems.at[w]))
            local[-1].start()
            if w < n_now:
                for j in range(len(CHIP_FLIPS)):
                    first.append(ici_copy(w, j, stage[w].at[half(w, c)], k))
                    first[-1].start()
        for w in range(nf):
            for j in range(len(CHIP_FLIPS)):
                filter_copy(w, j, chip_of(j)).wait_recv()
        for w in range(n_now):
            for j in range(len(CHIP_FLIPS)):
                ici_copy(w, j, stage[w].at[half(w, c)], chip_of(j)).wait_recv()
                passed.append(d2d_copy(w, j, chip_of(j), c))
                passed[-1].start()
        for w in range(n_now):
            for j in range(len(CHIP_FLIPS)):
                d2d_copy(w, j, chip_of(j), 1 - c).wait_recv()
        for cp in first + passed:
            cp.wait_send()
        for cp in local:
            cp.wait()

    sem_shape = (n_now, 2 * len(CHIP_FLIPS))
    f_sem_shape = (nf, len(CHIP_FLIPS))
    outs = pl.pallas_call(
        body, name="gather_weights",
        in_specs=[VMEM_SPEC] * (n + nf), out_specs=[ANY] * (n + nf),
        out_shape=[jax.ShapeDtypeStruct((N_SHARD,) + s.shape, BF16) for s in shards]
        + [jax.ShapeDtypeStruct((N_SHARD,) + s.shape, F32) for s in filters],
        scratch_shapes=[pltpu.VMEM(s.shape, BF16) for s in shards]
        + [pltpu.SemaphoreType.DMA(sem_shape), pltpu.SemaphoreType.DMA(sem_shape), pltpu.SemaphoreType.DMA((n + nf,)),
           pltpu.SemaphoreType.DMA(f_sem_shape), pltpu.SemaphoreType.DMA(f_sem_shape)],
        compiler_params=pltpu.CompilerParams(vmem_limit_bytes=VMEM_LIMIT_BYTES),
    )(*shards, *filters)
    return outs[:n], outs[n:]


def _final_comm(srcs16, small):
    n = len(srcs16)
    rows = small.shape[0]
    half = rows // 2

    def body(*refs):
        srcs, small_ref = refs[:n], refs[n]
        lands, small_out = refs[n + 1:2 * n + 1], refs[2 * n + 1]
        chip_sum, got_c, got_x, got_y, part_x, send_sems, recv_sems, small_send_sems, small_recv_sems = refs[2 * n + 2:]
        x, y, c = _coords()
        sibling = (x, y, 1 - c)
        mine = pl.ds(pl.multiple_of(c * half, 8), half)
        copies = []
        for w in range(n):
            for j, flip in enumerate(CHIP_FLIPS):
                tx, ty = _flip(x, flip[0]), _flip(y, flip[1])
                copies.append(pltpu.make_async_remote_copy(
                    src_ref=srcs[w].at[2 * tx + ty], dst_ref=lands[w].at[j], send_sem=send_sems.at[w, j],
                    recv_sem=recv_sems.at[w, j], device_id=(tx, ty, c), device_id_type=MESH))
        for cp in copies:
            cp.start()

        def exchange(stage, src, dst, peer):
            rc = pltpu.make_async_remote_copy(
                src_ref=src, dst_ref=dst, send_sem=small_send_sems.at[stage], recv_sem=small_recv_sems.at[stage],
                device_id=peer, device_id_type=MESH)
            rc.start()
            rc.wait()

        exchange(0, small_ref, got_c, sibling)
        chip_sum[...] = small_ref[...] + got_c[...]
        exchange(1, chip_sum.at[mine], got_x, (1 - x, y, c))
        part_x[...] = chip_sum[mine, :] + got_x[...]
        exchange(2, part_x, got_y, (x, 1 - y, c))
        small_out[mine, :] = part_x[...] + got_y[...]
        exchange(3, small_out.at[mine], small_out.at[mine], sibling)
        for cp in copies:
            cp.wait()

    n_chip = len(CHIP_FLIPS)
    half_shape = (half, small.shape[1])
    outs = pl.pallas_call(
        body, name="final_comm",
        in_specs=[ANY] * n + [VMEM_SPEC], out_specs=[ANY] * n + [VMEM_SPEC],
        out_shape=[jax.ShapeDtypeStruct((n_chip,) + a.shape[1:], BF16) for a in srcs16]
        + [jax.ShapeDtypeStruct(small.shape, F32)],
        scratch_shapes=[pltpu.VMEM(small.shape, F32), pltpu.VMEM(small.shape, F32), pltpu.VMEM(half_shape, F32),
                        pltpu.VMEM(half_shape, F32), pltpu.VMEM(half_shape, F32),
                        pltpu.SemaphoreType.DMA((n, n_chip)), pltpu.SemaphoreType.DMA((n, n_chip)),
                        pltpu.SemaphoreType.DMA((4,)), pltpu.SemaphoreType.DMA((4,))],
        compiler_params=pltpu.CompilerParams(vmem_limit_bytes=VMEM_LIMIT_BYTES),
    )(*srcs16, small)
    return outs[:n], outs[n]


ADD_CHUNKS = 4


def _scatter_sum(pos, owns, lands):
    n = len(owns)

    def specs(own_shape, land_shape):
        peers, rows, cols = land_shape
        pick = 1 if own_shape[1] == 2 * rows else 0
        if cols % (128 * ADD_CHUNKS) == 0:
            blk = (rows, cols // ADD_CHUNKS)
            return (pl.BlockSpec((1,) + blk, lambda i, p: (2 * p[0] + p[1], pick * p[2], i)),
                    pl.BlockSpec((peers,) + blk, lambda i, p: (0, 0, i)),
                    pl.BlockSpec((1,) + blk, lambda i, p: (p[2], 0, i)))
        blk = (rows // ADD_CHUNKS, cols)
        return (pl.BlockSpec((1,) + blk, lambda i, p: (2 * p[0] + p[1], pick * p[2] * ADD_CHUNKS + i, 0)),
                pl.BlockSpec((peers,) + blk, lambda i, p: (0, i, 0)),
                pl.BlockSpec((1,) + blk, lambda i, p: (p[2], i, 0)))

    def body(pos_ref, *refs):
        for idx in range(n):
            own, land, out = refs[idx], refs[n + idx], refs[2 * n + idx]
            total = own[0]
            for f in range(land.shape[0]):
                total = total + land[f].astype(F32)
            out[0] = total

    all_specs = [specs(o.shape, l.shape) for o, l in zip(owns, lands)]
    return pl.pallas_call(
        body, name="scatter_sum",
        grid_spec=pltpu.PrefetchScalarGridSpec(
            num_scalar_prefetch=1, grid=(ADD_CHUNKS,),
            in_specs=[s[0] for s in all_specs] + [s[1] for s in all_specs], out_specs=[s[2] for s in all_specs]),
        out_shape=[jax.ShapeDtypeStruct((2,) + l.shape[1:], F32) for l in lands],
        compiler_params=_params(),
    )(pos, *owns, *lands)


def _swap_halves(halves):
    n = len(halves)

    def body(*refs):
        ins, outs = refs[:n], refs[n:2 * n]
        send_sems, recv_sems = refs[2 * n:]
        x, y, c = _coords()
        copies = [pltpu.make_async_remote_copy(
            src_ref=ins[idx].at[pl.ds(c, 1)], dst_ref=outs[idx].at[pl.ds(c, 1)], send_sem=send_sems.at[idx],
            recv_sem=recv_sems.at[idx], device_id=(x, y, 1 - c), device_id_type=MESH) for idx in range(n)]
        for cp in copies:
            cp.start()
        for cp in copies:
            cp.wait()

    return pl.pallas_call(
        body, name="swap_halves",
        in_specs=[ANY] * n, out_specs=[ANY] * n, input_output_aliases={idx: idx for idx in range(n)},
        out_shape=[jax.ShapeDtypeStruct(a.shape, F32) for a in halves],
        scratch_shapes=[pltpu.SemaphoreType.DMA((n,)), pltpu.SemaphoreType.DMA((n,))],
    )(*halves)


def _adamw_math(w, g, m, v):
    m = ADAM_B1 * m + (1.0 - ADAM_B1) * g
    v = ADAM_B2 * v + (1.0 - ADAM_B2) * jnp.square(g)
    m_hat = m / (1.0 - ADAM_B1 ** ADAM_STEP)
    v_hat = v / (1.0 - ADAM_B2 ** ADAM_STEP)
    delta = -ADAM_LR * (m_hat / (jnp.sqrt(v_hat) + ADAM_EPS) + ADAM_WD * w)
    return delta, m, v


def _adamw(name, w, g, m, v, block_rows):
    rows, cols = w.shape

    def body(w_ref, g_ref, m_ref, v_ref, d_out, m_out, v_out):
        d_out[...], m_out[...], v_out[...] = _adamw_math(w_ref[...], g_ref[...], m_ref[...], v_ref[...])

    spec = pl.BlockSpec((block_rows, cols), lambda i: (i, 0))
    shape = jax.ShapeDtypeStruct((rows, cols), F32)
    return pl.pallas_call(
        body, grid=(rows // block_rows,), name=name, in_specs=[spec] * 4, out_specs=[spec] * 3,
        out_shape=[shape] * 3, compiler_params=_params(),
    )(w, g, m, v)


def _adamw_many(ws, gs, ms, vs):
    n = len(ws)

    def body(*refs):
        w_refs, g_refs, m_refs, v_refs = (refs[q * n:(q + 1) * n] for q in range(4))
        d_outs, m_outs, v_outs = (refs[(4 + q) * n:(5 + q) * n] for q in range(3))
        for idx in range(n):
            d_outs[idx][...], m_outs[idx][...], v_outs[idx][...] = _adamw_math(
                w_refs[idx][...], g_refs[idx][...], m_refs[idx][...], v_refs[idx][...])

    shapes = [jax.ShapeDtypeStruct(w.shape, F32) for w in ws]
    outs = pl.pallas_call(
        body, name="adamw_small", in_specs=[VMEM_SPEC] * (4 * n), out_specs=[VMEM_SPEC] * (3 * n),
        out_shape=shapes * 3, compiler_params=pltpu.CompilerParams(vmem_limit_bytes=VMEM_LIMIT_BYTES),
    )(*ws, *gs, *ms, *vs)
    return outs[:n], outs[n:2 * n], outs[2 * n:]


def _adamw_ada(c_all16, dmod16, w, m, v, block_rows):
    rows, cols = w.shape

    def body(c_ref, dm_ref, w_ref, m_ref, v_ref, g_out, d_out, m_out, v_out):
        cv = c_ref[...]
        g = _tn((cv * jax.nn.sigmoid(cv)).astype(BF16), dm_ref[...].astype(BF16))
        g_out[...] = g
        d_out[...], m_out[...], v_out[...] = _adamw_math(w_ref[...], g, m_ref[...], v_ref[...])

    spec = pl.BlockSpec((block_rows, cols), lambda i: (i, 0))
    shape = jax.ShapeDtypeStruct((rows, cols), F32)
    return pl.pallas_call(
        body, grid=(rows // block_rows,), name="adamw_w_ada",
        in_specs=[pl.BlockSpec((16, block_rows), lambda i: (0, i)), _full(dmod16.shape), spec, spec, spec],
        out_specs=[spec] * 4, out_shape=[shape] * 4, compiler_params=_params(),
    )(c_all16, dmod16, w, m, v)


SMALL_REPLICATED = ["b_ada", "norm1_gain", "conv_dw_b", "conv_ln_g", "conv_ln_b", "gm_ln_g", "gm_ln_b", "gm_ws", "gm_bs",
                    "mix_out_gain", "norm2_gain", "ffn_dw_b", "final_gain"]
SMALL_SHARDED = ["conv_dw_w", "ffn_dw_w"]
PACK_ROWS = 256
WEIGHT_ORDER = ["w_ada", "b_ada", "norm1_gain", "w_in", "conv_dw_w", "conv_dw_b", "conv_ln_g", "conv_ln_b", "gm_ln_g",
                "gm_ln_b", "gm_ws", "gm_bs", "mix_out_gain", "w_out", "norm2_gain", "w_up", "ffn_dw_w", "ffn_dw_b",
                "w_down", "final_gain"]


def _pack(parts, rows):
    flat = jnp.concatenate([a.reshape(-1) for a in parts])
    return jnp.pad(flat, (0, rows * D_MODEL - flat.shape[0])).reshape(rows, D_MODEL)


def _unpack(packed, shapes):
    flat = packed.reshape(-1)
    out, pos = [], 0
    for s in shapes:
        size = 1
        for d in s:
            size *= d
        out.append(flat[pos:pos + size].reshape(s))
        pos += size
    return out


def kernel(x, c, w_ada, b_ada, norm1_gain, w_in, conv_dw_w, conv_dw_b, conv_ln_g, conv_ln_b, gm_ln_g, gm_ln_b, gm_ws, gm_bs, mix_out_gain, w_out, norm2_gain, w_up, ffn_dw_w, ffn_dw_b, w_down, final_gain, loss_target, m_w_ada, m_b_ada, m_norm1_gain, m_w_in, m_conv_dw_w, m_conv_dw_b, m_conv_ln_g, m_conv_ln_b, m_gm_ln_g, m_gm_ln_b, m_gm_ws, m_gm_bs, m_mix_out_gain, m_w_out, m_norm2_gain, m_w_up, m_ffn_dw_w, m_ffn_dw_b, m_w_down, m_final_gain, v_w_ada, v_b_ada, v_norm1_gain, v_w_in, v_conv_dw_w, v_conv_dw_b, v_conv_ln_g, v_conv_ln_b, v_gm_ln_g, v_gm_ln_b, v_gm_ws, v_gm_bs, v_mix_out_gain, v_w_out, v_norm2_gain, v_w_up, v_ffn_dw_w, v_ffn_dw_b, v_w_down, v_final_gain):
    weights = dict(w_ada=w_ada, b_ada=b_ada, norm1_gain=norm1_gain, w_in=w_in, conv_dw_w=conv_dw_w, conv_dw_b=conv_dw_b,
                   conv_ln_g=conv_ln_g, conv_ln_b=conv_ln_b, gm_ln_g=gm_ln_g, gm_ln_b=gm_ln_b, gm_ws=gm_ws, gm_bs=gm_bs,
                   mix_out_gain=mix_out_gain, w_out=w_out, norm2_gain=norm2_gain, w_up=w_up, ffn_dw_w=ffn_dw_w,
                   ffn_dw_b=ffn_dw_b, w_down=w_down, final_gain=final_gain)
    mom1 = dict(w_ada=m_w_ada, b_ada=m_b_ada, norm1_gain=m_norm1_gain, w_in=m_w_in, conv_dw_w=m_conv_dw_w,
                conv_dw_b=m_conv_dw_b, conv_ln_g=m_conv_ln_g, conv_ln_b=m_conv_ln_b, gm_ln_g=m_gm_ln_g, gm_ln_b=m_gm_ln_b,
                gm_ws=m_gm_ws, gm_bs=m_gm_bs, mix_out_gain=m_mix_out_gain, w_out=m_w_out, norm2_gain=m_norm2_gain,
                w_up=m_w_up, ffn_dw_w=m_ffn_dw_w, ffn_dw_b=m_ffn_dw_b, w_down=m_w_down, final_gain=m_final_gain)
    mom2 = dict(w_ada=v_w_ada, b_ada=v_b_ada, norm1_gain=v_norm1_gain, w_in=v_w_in, conv_dw_w=v_conv_dw_w,
                conv_dw_b=v_conv_dw_b, conv_ln_g=v_conv_ln_g, conv_ln_b=v_conv_ln_b, gm_ln_g=v_gm_ln_g, gm_ln_b=v_gm_ln_b,
                gm_ws=v_gm_ws, gm_bs=v_gm_bs, mix_out_gain=v_mix_out_gain, w_out=v_w_out, norm2_gain=v_norm2_gain,
                w_up=v_w_up, ffn_dw_w=v_ffn_dw_w, ffn_dw_b=v_ffn_dw_b, w_down=v_w_down, final_gain=v_final_gain)
    shard = 2 * lax.axis_index("x") + lax.axis_index("y")
    me = 2 * shard + lax.axis_index("c")

    ada_cols = w_ada.shape[2]
    b_ada_sh = lax.dynamic_slice(b_ada, (0, shard * ada_cols), (1, ada_cols))
    c_all64, mod32 = _ada_mod(c, w_ada[0], b_ada_sh)
    c_all = c_all64[::8]
    mod = mod32[::8].reshape(1, N_SHARD * ada_cols)

    (w_in_g, w_out_g, w_up_part, w_down_part), (conv_w_g, ffn_w_g) = _gather_weights(
        [w_in[0], w_out[0], w_up[0], w_down[0]], [conv_dw_w[0], ffn_dw_w[0]], n_now=2)
    conv_w_full = jnp.transpose(conv_w_g, (1, 0, 2)).reshape(CONV_K, D_HALF)
    ffn_w_full = jnp.transpose(ffn_w_g, (1, 0, 2)).reshape(FFN_K, 2 * D_FF)

    p = dict(norm1_gain=norm1_gain, conv_dw_w=conv_w_full, conv_dw_b=conv_dw_b, conv_ln_g=conv_ln_g,
             conv_ln_b=conv_ln_b, gm_ln_g=gm_ln_g, gm_ln_b=gm_ln_b, gm_ws=gm_ws[0], gm_bs=gm_bs[0],
             mix_out_gain=mix_out_gain, norm2_gain=norm2_gain, ffn_dw_w=ffn_w_full, ffn_dw_b=ffn_dw_b,
             final_gain=final_gain[None])
    grad_x, g, d_mod, loss, in_flight = _local_step(
        x[0], loss_target[0], mod, p, w_in_g, w_out_g.reshape(D_MODEL, D_MODEL), w_up_part, w_down_part)

    n_mod = d_mod.shape[1]
    dmod_rows = lax.dynamic_update_slice(jnp.zeros((N_DEV, n_mod), F32), d_mod, (me, 0))
    g["b_ada"] = d_mod
    small = _pack([g[k] for k in SMALL_REPLICATED] + [g[k] for k in SMALL_SHARDED] + [dmod_rows, loss[0, :1]], PACK_ROWS)
    (land_w_in, land_w_out), small = _final_comm([in_flight["w_in16"], in_flight["w_out16"]], small)
    pos = jnp.stack(_coords()).astype(jnp.int32)
    halves = _scatter_sum(pos, [g["w_in"], g["w_out"], g["w_up"], g["w_down"]],
                          [land_w_in, land_w_out, in_flight["land_w_up"], in_flight["land_w_down"]])
    full = _swap_halves(halves)
    grads = dict(w_in=full[0].reshape(w_in.shape[1:]), w_out=full[1].reshape(w_out.shape[1:]),
                 w_up=full[2].reshape(w_up.shape[1:]), w_down=full[3].reshape(w_down.shape[1:]))

    small_shapes = ([weights[k].shape for k in SMALL_REPLICATED] + [(CONV_K, D_HALF), (FFN_K, 2 * D_FF)]
                    + [(N_DEV, n_mod), (1,)])
    *small_grads, conv_w_grad, ffn_w_grad, dmod_all, loss_sum = _unpack(small, small_shapes)
    grads.update(zip(SMALL_REPLICATED, small_grads))
    grads["conv_dw_w"] = lax.dynamic_slice(conv_w_grad, (0, shard * conv_dw_w.shape[2]), conv_dw_w.shape[1:])[None]
    grads["ffn_dw_w"] = lax.dynamic_slice(ffn_w_grad, (0, shard * ffn_dw_w.shape[2]), ffn_dw_w.shape[1:])[None]

    delta, new_m, new_v = {}, {}, {}
    for name, block_rows in (("w_in", 256), ("w_out", 128), ("w_up", 256), ("w_down", 352)):
        delta[name], new_m[name], new_v[name] = [a[None] for a in _adamw(
            "adamw_" + name, weights[name][0], grads[name], mom1[name][0], mom2[name][0], block_rows)]
        grads[name] = grads[name][None]
    dmod_sh = lax.dynamic_slice(dmod_all, (0, shard * ada_cols), (N_DEV, ada_cols))
    pad8 = ((0, 16 - N_DEV), (0, 0))
    grads["w_ada"], delta["w_ada"], new_m["w_ada"], new_v["w_ada"] = [a[None] for a in _adamw_ada(
        jnp.pad(c_all, pad8), jnp.pad(dmod_sh, pad8), w_ada[0], m_w_ada[0], v_w_ada[0], 256)]
    small_names = SMALL_REPLICATED + SMALL_SHARDED

    def two_d(a):
        return a.reshape(1, -1) if a.ndim == 1 else a

    small_out = _adamw_many(*[[two_d(d[k]) for k in small_names] for d in (weights, grads, mom1, mom2)])
    for d, arrs in zip((delta, new_m, new_v), small_out):
        d.update({k: a.reshape(weights[k].shape) for k, a in zip(small_names, arrs)})

    return (loss_sum.reshape(()), grad_x[None], *[grads[k] for k in WEIGHT_ORDER], *[delta[k] for k in WEIGHT_ORDER],
            *[new_m[k] for k in WEIGHT_ORDER], *[new_v[k] for k in WEIGHT_ORDER])
```

```python
import functools

import jax
import jax.numpy as jnp
from jax import lax
from jax.experimental import pallas as pl
from jax.experimental.pallas import tpu as pltpu

F32 = jnp.float32
BF16 = jnp.bfloat16

D_MODEL = 1024
D_HALF = 512
D_FF = 2816
CONV_K = 31
FFN_K = 3
CHUNK = 128
N_HEADS = 8
HEAD_DIM = 64
N_SHARD = 4
N_DEV = 8
RMS_EPS = 1e-6
LN_EPS = 1e-5
ADAM_LR, ADAM_B1, ADAM_B2, ADAM_EPS, ADAM_WD, ADAM_STEP = 0.001, 0.9, 0.999, 1e-08, 0.01, 10

TILE = 256
HALO = 32
FFN_HALO = 16
FFN_BLK = 256
UP_SHARD = 2 * D_FF // N_SHARD
VMEM_LIMIT_BYTES = 56 * 1024 * 1024
FFN_VMEM_LIMIT_BYTES = 58 * 1024 * 1024

ANY = pl.BlockSpec(memory_space=pl.ANY)
NT_DIMS = (((1,), (1,)), ((), ()))
TN_DIMS = (((0,), (0,)), ((), ()))


def _full(shape):
    return pl.BlockSpec(shape, lambda i: (0,) * len(shape))


def _nn(a, b):
    return jnp.dot(a, b, preferred_element_type=F32)


def _nt(a, b):
    return lax.dot_general(a, b, NT_DIMS, preferred_element_type=F32)


def _tn(a, b):
    return lax.dot_general(a, b, TN_DIMS, preferred_element_type=F32)


def _colsum(a):
    return jnp.sum(a, axis=0, keepdims=True)


def _params(semantics=("arbitrary",)):
    return pltpu.CompilerParams(dimension_semantics=semantics, vmem_limit_bytes=VMEM_LIMIT_BYTES)


def _rms(v, gain):
    return v * lax.rsqrt(jnp.mean(v * v, axis=-1, keepdims=True) + RMS_EPS) * gain


def _layer_norm(v, gain, bias):
    mu = jnp.mean(v, axis=-1, keepdims=True)
    var = jnp.mean(jnp.square(v - mu), axis=-1, keepdims=True)
    return (v - mu) * lax.rsqrt(var + LN_EPS) * gain + bias


def _mod_norm(v, gain, scale, shift):
    return _rms(v, gain) * (1.0 + scale) + shift


def _conv_branch(a1, ln_g, ln_b, out_gain):
    a2 = _layer_norm(a1, ln_g, ln_b)
    return _rms(a2 * jax.nn.sigmoid(a2), out_gain)


def _gate_branch(gu, sp, out_gain):
    return _rms(jax.nn.gelu(gu) * sp, out_gain)


def _gv_norm(gv, ln_g, ln_b):
    return _layer_norm(jax.nn.gelu(gv), ln_g, ln_b)


def _rms_parts(v):
    r = lax.rsqrt(jnp.mean(v * v, axis=-1, keepdims=True) + RMS_EPS)
    return v * r, r


def _rms_back(dn, n, r):
    return r * (dn - n * jnp.mean(dn * n, axis=-1, keepdims=True))


def _ln_parts(v):
    mu = jnp.mean(v, axis=-1, keepdims=True)
    rs = lax.rsqrt(jnp.mean(jnp.square(v - mu), axis=-1, keepdims=True) + LN_EPS)
    return (v - mu) * rs, rs


def _ln_back(dn, n, rs):
    return rs * (dn - jnp.mean(dn, axis=-1, keepdims=True) - n * jnp.mean(dn * n, axis=-1, keepdims=True))


GELU_C = 0.7978845608028654
GELU_A = 0.044715


def _gelu_parts(v):
    v2 = v * v
    th = jnp.tanh(GELU_C * (v + GELU_A * (v2 * v)))
    cdf = 0.5 * (1.0 + th)
    return v * cdf, cdf + (0.5 * GELU_C) * v * (1.0 - th * th) * (1.0 + (3.0 * GELU_A) * v2)


def _rms_vjp(v, gain):
    n, r = _rms_parts(v)
    return n * gain, lambda dy: (_rms_back(dy * gain, n, r), _colsum(dy * n))


def _mod_norm_vjp(v, gain, scale, shift):
    n, r = _rms_parts(v)

    def back(dy):
        q = _colsum(dy * n)
        return _rms_back(dy * (gain * (1.0 + scale)), n, r), q * (1.0 + scale), q * gain, _colsum(dy)

    return n * gain * (1.0 + scale) + shift, back


def _conv_branch_vjp(a1, ln_g, ln_b, out_gain):
    n1, rs1 = _ln_parts(a1)
    a2 = n1 * ln_g + ln_b
    s = jax.nn.sigmoid(a2)
    a3 = a2 * s
    n3, r3 = _rms_parts(a3)

    def back(dy):
        da2 = _rms_back(dy * out_gain, n3, r3) * (s + a3 * (1.0 - s))
        return _ln_back(da2 * ln_g, n1, rs1), _colsum(da2 * n1), _colsum(da2), _colsum(dy * n3)

    return n3 * out_gain, back


def _gate_branch_vjp(gu, sp, out_gain):
    ge, dge = _gelu_parts(gu)
    n, r = _rms_parts(ge * sp)

    def back(dy):
        dg = _rms_back(dy * out_gain, n, r)
        return dg * sp * dge, dg * ge, _colsum(dy * n)

    return n * out_gain, back


def _gv_norm_vjp(gv, ln_g, ln_b):
    ge, dge = _gelu_parts(gv)
    n, rs = _ln_parts(ge)
    return n * ln_g + ln_b, lambda dy: (_ln_back(dy * ln_g, n, rs) * dge, _colsum(dy * n), _colsum(dy))


def _head_pair_matmul(wp_ref, v):
    lane = lax.broadcasted_iota(jnp.int32, (CHUNK, CHUNK), 1)
    rows = []
    for n in range(v.shape[0] // CHUNK):
        cols = []
        for j in range(N_HEADS // 2):
            r = _nn(wp_ref[j], v[n * CHUNK:(n + 1) * CHUNK, j * CHUNK:(j + 1) * CHUNK])
            cols.append(jnp.where(lane < HEAD_DIM, r[:CHUNK], r[CHUNK:]))
        rows.append(jnp.concatenate(cols, axis=1))
    return jnp.concatenate(rows, axis=0)


def _tile_bias(bs, tokens):
    return jnp.concatenate([bs] * (tokens // CHUNK), axis=0)


FORWARD_LEAD = 8


def _fwd_mixer(x, vec, conv_w, wpair, bs_full, w_in_g, w_out_g, late_parts):
    seq = x.shape[0]
    n_tiles = seq // TILE
    t = TILE
    n_late = len(late_parts)
    forward_step = max(n_tiles - FORWARD_LEAD, 0)
    names = ["norm1_gain", "sc1", "sh1", "gt1", "conv_dw_b", "conv_ln_g", "conv_ln_b", "gm_ln_g", "gm_ln_b",
             "mix_out_gain"]
    vecs = [vec[k] for k in names]

    def body(x_ref, g1, sc1, sh1, gt1, cb, clg, clb, vg, vb, mg, cw, wp, bs, win_hbm, wout_hbm, *rest):
        late = rest[n_late:2 * n_late]
        z_ref, a1_ref, sp_ref, y_ref, o1_ref, x2_ref = rest[2 * n_late:2 * n_late + 6]
        win_v, wout_v, halo, bank, sem, send_sems, recv_sems = rest[2 * n_late + 6:]
        i = pl.program_id(0)
        mx, my, mc = _coords()
        shard = 2 * mx + my

        def half(w, which):
            h = late[w].shape[1] // 2
            return pl.ds(pl.multiple_of(which * h, 16), h)

        def chip_of(j):
            return 2 * _flip(mx, CHIP_FLIPS[j][0]) + _flip(my, CHIP_FLIPS[j][1])

        def ici_copy(w, j, slot):
            rows = late[w].at[slot, half(w, mc)]
            return pltpu.make_async_remote_copy(
                src_ref=rows, dst_ref=rows, send_sem=send_sems.at[w, j], recv_sem=recv_sems.at[w, j],
                device_id=(_flip(mx, CHIP_FLIPS[j][0]), _flip(my, CHIP_FLIPS[j][1]), mc), device_id_type=MESH)

        def d2d_copy(w, j, which):
            rows = late[w].at[chip_of(j), half(w, which)]
            return pltpu.make_async_remote_copy(
                src_ref=rows, dst_ref=rows, send_sem=send_sems.at[w, len(CHIP_FLIPS) + j],
                recv_sem=recv_sems.at[w, len(CHIP_FLIPS) + j], device_id=(mx, my, 1 - mc), device_id_type=MESH)

        pairs = [(w, j) for w in range(n_late) for j in range(len(CHIP_FLIPS))]

        @pl.when(i == 0)
        def _():
            for w, j in pairs:
                ici_copy(w, j, shard).start()
            cps = [pltpu.make_async_copy(win_hbm, win_v, sem.at[0]),
                   pltpu.make_async_copy(wout_hbm, wout_v, sem.at[1])]
            for cp in cps:
                cp.start()
            for cp in cps:
                cp.wait()
            halo[...] = jnp.zeros_like(halo)

        @pl.when(i == forward_step)
        def _():
            for w, j in pairs:
                ici_copy(w, j, chip_of(j)).wait_recv()
                d2d_copy(w, j, mc).start()

        xv = x_ref[...]
        h1b = _mod_norm(xv, g1[...], sc1[...], sh1[...]).astype(BF16)
        zs = [_nn(h1b, win_v[k]) for k in range(N_SHARD)]
        for k in range(N_SHARD):
            z_ref[:, k * D_HALF:(k + 1) * D_HALF] = zs[k]
        ca, cg, gu, gv = zs
        a0 = ca * jax.nn.sigmoid(cg)
        ext = jnp.concatenate([halo[...], a0], axis=0)
        halo[...] = a0[t - HALO:]
        bank[0] = ext
        for b in range(1, 8):
            bank[b] = pltpu.roll(ext, b, axis=0)
        a1 = jnp.zeros((t, D_HALF), F32) + cb[...]
        for s in range(CONV_K):
            q, b = divmod(s, 8)
            a1 = a1 + bank[b, pl.ds(HALO - 8 * q, t), :] * cw[pl.ds(CONV_K - 1 - s, 1), :]
        a1_ref[...] = a1
        mgv = mg[...]
        ya = _conv_branch(a1, clg[...], clb[...], mgv[:, :D_HALF])
        gvn = _gv_norm(gv, vg[...], vb[...]).astype(BF16)
        sp = _head_pair_matmul(wp, gvn) + _tile_bias(bs[...], t)
        sp_ref[...] = sp
        yg = _gate_branch(gu, sp, mgv[:, D_HALF:])
        yb = jnp.concatenate([ya, yg], axis=1).astype(BF16)
        y_ref[...] = yb
        o1 = _nn(yb, wout_v[...])
        o1_ref[...] = o1
        x2_ref[...] = xv + gt1[...] * o1

        @pl.when(i == n_tiles - 1)
        def _():
            for w, j in pairs:
                d2d_copy(w, j, 1 - mc).wait_recv()
            for w, j in pairs:
                ici_copy(w, j, shard).wait_send()
                d2d_copy(w, j, mc).wait_send()

    def row(width):
        return pl.BlockSpec((t, width), lambda i: (i, 0))

    out_shape = [jax.ShapeDtypeStruct((seq, 4 * D_HALF), F32), jax.ShapeDtypeStruct((seq, D_HALF), F32),
                 jax.ShapeDtypeStruct((seq, D_HALF), F32), jax.ShapeDtypeStruct((seq, D_MODEL), BF16),
                 jax.ShapeDtypeStruct((seq, D_MODEL), F32), jax.ShapeDtypeStruct((seq, D_MODEL), F32)]
    n_in = 1 + len(vecs) + 3 + 2
    sem_shape = (n_late, 2 * len(CHIP_FLIPS))
    outs = pl.pallas_call(
        body, grid=(n_tiles,), name="fwd_mixer",
        in_specs=[row(D_MODEL)] + [_full(v.shape) for v in vecs]
        + [_full(conv_w.shape), _full(wpair.shape), _full(bs_full.shape), ANY, ANY] + [ANY] * n_late,
        out_specs=[ANY] * n_late + [row(4 * D_HALF), row(D_HALF), row(D_HALF), row(D_MODEL), row(D_MODEL),
                                    row(D_MODEL)],
        out_shape=[jax.ShapeDtypeStruct(a.shape, a.dtype) for a in late_parts] + out_shape,
        input_output_aliases={n_in + w: w for w in range(n_late)},
        scratch_shapes=[pltpu.VMEM(w_in_g.shape, BF16), pltpu.VMEM(w_out_g.shape, BF16),
                        pltpu.VMEM((HALO, D_HALF), F32), pltpu.VMEM((8, t + HALO, D_HALF), F32),
                        pltpu.SemaphoreType.DMA((2,)), pltpu.SemaphoreType.DMA(sem_shape),
                        pltpu.SemaphoreType.DMA(sem_shape)],
        compiler_params=_params(),
    )(x, *vecs, conv_w, wpair, bs_full, w_in_g, w_out_g, *late_parts)
    return outs[n_late:], outs[:n_late]


def _interleave_matrices():
    row = jnp.arange(TILE)
    token_of_row = (row % 8) * (TILE // 8) + row // 8
    to_inter = (token_of_row[:, None] == row[None, :]).astype(BF16)
    return to_inter, jnp.transpose(to_inter)


def _ffn(x2, target, norm2_gain, sc2, sh2, ffn_w, ffn_b, gt2, final_gain, w_up_g, w_down_g, to_inter, to_natural):
    seq = x2.shape[0]
    n_tiles = seq // TILE
    t = TILE
    n_blk = D_FF // FFN_BLK
    inv_d = 1.0 / D_MODEL

    def body(x2_ref, x2h_ref, tgt_ref, g2, sc2_ref, sh2_ref, fw, fb, gt2_ref, fg, pm_ref, pmt_ref, wup_hbm, wd_hbm,
             du_ref, dx2_ref, dfw_ref, dfb_ref, dfg_ref, dgt2_ref, dg2_ref, dsc2_ref, dsh2_ref, loss_ref, dwd_hbm,
             dwd16_hbm, wup_v, wd_v, dwd_acc, carry, u_s, sil_s, vds_s, f_s, du_s, sem):
        i = pl.program_id(0)
        tile = n_tiles - 1 - i
        sublane = lax.broadcasted_iota(jnp.int32, (8, FFN_BLK), 0)

        @pl.when(i == 0)
        def _():
            cps = [pltpu.make_async_copy(wd_hbm, wd_v, sem.at[0])]
            cps += [pltpu.make_async_copy(wup_hbm.at[k], wup_v.at[:, pl.ds(k * UP_SHARD, UP_SHARD)], sem.at[3 + k])
                    for k in range(N_SHARD)]
            for cp in cps:
                cp.start()
            for cp in cps:
                cp.wait()
            dwd_acc[...] = jnp.zeros_like(dwd_acc)
            carry[...] = jnp.zeros_like(carry)
            dfw_ref[...] = jnp.zeros_like(dfw_ref)
            dfb_ref[...] = jnp.zeros_like(dfb_ref)
            dfg_ref[...] = jnp.zeros_like(dfg_ref)
            dgt2_ref[...] = jnp.zeros_like(dgt2_ref)
            dg2_ref[...] = jnp.zeros_like(dg2_ref)
            dsc2_ref[...] = jnp.zeros_like(dsc2_ref)
            dsh2_ref[...] = jnp.zeros_like(dsh2_ref)
            loss_ref[...] = jnp.zeros_like(loss_ref)

        def cols_of(j):
            return pl.ds(j * FFN_BLK, FFN_BLK), pl.ds(D_FF + j * FFN_BLK, FFN_BLK)

        def wrap_down(last, before):
            return jnp.where(sublane == 0, pltpu.roll(before, 1, axis=0), pltpu.roll(last, 1, axis=0))

        def wrap_up(first, after):
            return jnp.where(sublane == 7, pltpu.roll(after, 7, axis=0), pltpu.roll(first, 7, axis=0))

        x2v = x2_ref[...]
        h2, h2_vjp = _mod_norm_vjp(x2v, g2[...], sc2_ref[...], sh2_ref[...])
        h2b = h2.astype(BF16)
        h2_before = _mod_norm(x2h_ref[...], g2[...], sc2_ref[...], sh2_ref[...]).astype(BF16)
        lhs = jnp.concatenate([_nn(pm_ref[...], h2b).astype(BF16), h2_before], axis=0)

        def up(j):
            cv, cg = cols_of(j)
            return _nn(lhs, wup_v[:, cv]), _nn(lhs, wup_v[:, cg])

        def conv(both, cols):
            cur = both[:t]
            u_s[:, cols] = cur.astype(BF16)
            before = jnp.where(tile > 0, both[t:], 0.0)
            w1 = wrap_down(cur[t - 8:], before)
            w2 = wrap_down(cur[t - 16:t - 8], pltpu.roll(before, 1, axis=0))
            back1 = jnp.concatenate([w1, cur[:t - 8]], axis=0)
            back2 = jnp.concatenate([w2, w1, cur[:t - 16]], axis=0)
            return (fb[:, cols] + cur * fw[pl.ds(2, 1), cols] + back1 * fw[pl.ds(1, 1), cols]
                    + back2 * fw[pl.ds(0, 1), cols])

        pm_t = pmt_ref[...]

        def to_natural_f32(a):
            hi = a.astype(BF16)
            rest = a - hi.astype(F32)
            mid = rest.astype(BF16)
            low = (rest - mid.astype(F32)).astype(BF16)
            return _nn(jnp.concatenate([pm_t, pm_t, pm_t], axis=1), jnp.concatenate([hi, mid, low], axis=0))

        o2 = jnp.zeros((t, D_MODEL), F32)
        ahead_uv = up(0)
        for j in range(n_blk):
            cv, cg = cols_of(j)
            both_v, both_g = ahead_uv
            if j + 1 < n_blk:
                ahead_uv = up(j + 1)
            val, gate = conv(both_v, cv), conv(both_g, cg)
            sig = jax.nn.sigmoid(gate)
            sil = gate * sig
            fb16 = (sil * val).astype(BF16)
            sil_s[:, cv] = sil
            vds_s[:, cv] = val * (sig + sil * (1.0 - sig))
            f_s[:, cv] = fb16
            o2 = o2 + _nn(fb16, wd_v[pl.ds(j * FFN_BLK, FFN_BLK), :])
        o2 = to_natural_f32(o2)

        gt2v = gt2_ref[...]
        x3 = x2v + gt2v * o2
        out, out_vjp = _rms_vjp(x3, fg[...])
        diff = out - tgt_ref[...]
        loss_ref[...] += jnp.zeros_like(loss_ref) + 0.5 * inv_d * jnp.sum(diff * diff)
        dx3, dfg = out_vjp(diff * inv_d)
        dfg_ref[...] += dfg
        dgt2_ref[...] += _colsum(dx3 * o2)
        do2b = _nn(pm_ref[...], (gt2v * dx3).astype(BF16)).astype(BF16)

        for j in range(n_blk):
            cv, cg = cols_of(j)
            rows = pl.ds(j * FFN_BLK, FFN_BLK)
            df = _nt(do2b, wd_v[rows, :])
            dwd_acc[rows, :] += _tn(f_s[:, cv], do2b)
            for dd, cols in ((df * sil_s[:, cv], cv), (df * vds_s[:, cv], cg)):
                dfb_ref[:, cols] += _colsum(dd)
                nxt = carry[:, cols]
                w1 = wrap_up(dd[:8], nxt[:8])
                w2 = wrap_up(dd[8:16], nxt[8:])
                ahead = (dd, jnp.concatenate([dd[8:], w1], axis=0), jnp.concatenate([dd[16:], w1, w2], axis=0))
                carry[:, cols] = dd[:16]
                uv = u_s[:, cols].astype(F32)
                du = jnp.zeros((t, FFN_BLK), F32)
                for s in range(FFN_K):
                    du = du + ahead[s] * fw[pl.ds(FFN_K - 1 - s, 1), cols]
                    dfw_ref[pl.ds(FFN_K - 1 - s, 1), cols] += _colsum(ahead[s] * uv)
                du_s[:, cols] = du.astype(BF16)
        du16 = _nn(pm_t, du_s[...]).astype(BF16)
        du_ref[...] = du16
        dx2, dg2, dsc2, dsh2 = h2_vjp(_nt(du16, wup_v[...]))
        dx2_ref[...] = dx3 + dx2
        dg2_ref[...] += dg2
        dsc2_ref[...] += dsc2
        dsh2_ref[...] += dsh2

        @pl.when(i == n_tiles - 1)
        def _():
            cp = pltpu.make_async_copy(dwd_acc, dwd_hbm, sem.at[1])
            cp.start()
            wd_v[...] = dwd_acc[...].astype(BF16)
            cp16 = pltpu.make_async_copy(wd_v, dwd16_hbm, sem.at[2])
            cp16.start()
            cp.wait()
            cp16.wait()

    def rev(width):
        return pl.BlockSpec((t, width), lambda i: (n_tiles - 1 - i, 0))

    assert FFN_K == 3
    halo_spec = pl.BlockSpec((8, D_MODEL), lambda i: (jnp.maximum((n_tiles - 1 - i) * (t // 8) - 1, 0), 0))
    vec_spec = _full((1, D_MODEL))
    out_shape = [jax.ShapeDtypeStruct((seq, 2 * D_FF), BF16), jax.ShapeDtypeStruct((seq, D_MODEL), F32),
                 jax.ShapeDtypeStruct((FFN_K, 2 * D_FF), F32), jax.ShapeDtypeStruct((1, 2 * D_FF), F32),
                 jax.ShapeDtypeStruct((1, D_MODEL), F32), jax.ShapeDtypeStruct((1, D_MODEL), F32),
                 jax.ShapeDtypeStruct((1, D_MODEL), F32), jax.ShapeDtypeStruct((1, D_MODEL), F32),
                 jax.ShapeDtypeStruct((1, D_MODEL), F32),
                 jax.ShapeDtypeStruct((1, 128), F32), jax.ShapeDtypeStruct((D_FF, D_MODEL), F32),
                 jax.ShapeDtypeStruct((D_FF, D_MODEL), BF16)]
    return pl.pallas_call(
        body, grid=(n_tiles,), name="ffn",
        in_specs=[rev(D_MODEL), halo_spec, rev(D_MODEL), vec_spec, vec_spec, vec_spec, _full(ffn_w.shape),
                  _full(ffn_b.shape), _full(gt2.shape), _full(final_gain.shape), _full(to_inter.shape),
                  _full(to_natural.shape), ANY, ANY],
        out_specs=[rev(2 * D_FF), rev(D_MODEL), _full((FFN_K, 2 * D_FF)), _full((1, 2 * D_FF)), vec_spec, vec_spec,
                   vec_spec, vec_spec, vec_spec, _full((1, 128)), ANY, ANY],
        out_shape=out_shape,
        scratch_shapes=[pltpu.VMEM((D_MODEL, 2 * D_FF), BF16), pltpu.VMEM((D_FF, D_MODEL), BF16),
                        pltpu.VMEM((D_FF, D_MODEL), F32), pltpu.VMEM((FFN_HALO, 2 * D_FF), F32),
                        pltpu.VMEM((t, 2 * D_FF), BF16), pltpu.VMEM((t, D_FF), F32), pltpu.VMEM((t, D_FF), F32),
                        pltpu.VMEM((t, D_FF), BF16), pltpu.VMEM((t, 2 * D_FF), BF16),
                        pltpu.SemaphoreType.DMA((3 + N_SHARD,))],
        compiler_params=pltpu.CompilerParams(dimension_semantics=("arbitrary",), vmem_limit_bytes=FFN_VMEM_LIMIT_BYTES),
    )(x2, x2, target, norm2_gain, sc2, sh2, ffn_w, ffn_b, gt2, final_gain, to_inter, to_natural, w_up_g, w_down_g)


def _scatter_copies(src16, land, send_sems, recv_sems):
    x, y, c = _coords()
    h = src16.shape[1] // 2
    copies = []
    for f, flip in enumerate(PEER_FLIPS):
        tx, ty, tc = _flip(x, flip[0]), _flip(y, flip[1]), _flip(c, flip[2])
        copies.append(pltpu.make_async_remote_copy(
            src_ref=src16.at[2 * tx + ty, pl.ds(pl.multiple_of(tc * h, 16), h)], dst_ref=land.at[f],
            send_sem=send_sems.at[f], recv_sem=recv_sems.at[f], device_id=(tx, ty, tc), device_id_type=MESH))
    return copies


def _land_shape(src16):
    return jax.ShapeDtypeStruct((len(PEER_FLIPS), src16.shape[1] // 2, src16.shape[2]), BF16)


UP_TILE = 512


def _bwd_up(du, x2, norm2_gain, sc2, sh2, dwd16):
    seq = x2.shape[0]
    t = UP_TILE if seq % UP_TILE == 0 else TILE
    n_tiles = seq // t
    acc_shape = (N_SHARD, D_MODEL, UP_SHARD)

    def body(du_ref, x2_ref, g2, sc2_ref, sh2_ref, dwd16_hbm, dwup_hbm, dwup16_hbm, land_hbm,
             stage16, dwup_acc, sem, send_sems, recv_sems):
        i = pl.program_id(0)

        @pl.when(i == 0)
        def _():
            for cp in _scatter_copies(dwd16_hbm, land_hbm, send_sems, recv_sems):
                cp.start()
            dwup_acc[...] = jnp.zeros_like(dwup_acc)

        h2b = _mod_norm(x2_ref[...], g2[...], sc2_ref[...], sh2_ref[...]).astype(BF16)
        for k in range(N_SHARD):
            dwup_acc[k] += _tn(h2b, du_ref[:, k * UP_SHARD:(k + 1) * UP_SHARD])

        @pl.when(i == n_tiles - 1)
        def _():
            cp = pltpu.make_async_copy(dwup_acc, dwup_hbm, sem.at[0])
            cp.start()
            for k in range(N_SHARD):
                stage16[k] = dwup_acc[k].astype(BF16)
            cp16 = pltpu.make_async_copy(stage16, dwup16_hbm, sem.at[1])
            cp16.start()
            cp.wait()
            cp16.wait()
            for rc in _scatter_copies(dwd16_hbm, land_hbm, send_sems, recv_sems):
                rc.wait()

    def row(width):
        return pl.BlockSpec((t, width), lambda i: (i, 0))

    n_peer = len(PEER_FLIPS)
    return pl.pallas_call(
        body, grid=(n_tiles,), name="bwd_up",
        in_specs=[row(2 * D_FF), row(D_MODEL), _full((1, D_MODEL)), _full((1, D_MODEL)), _full((1, D_MODEL)), ANY],
        out_specs=[ANY, ANY, ANY],
        out_shape=[jax.ShapeDtypeStruct(acc_shape, F32), jax.ShapeDtypeStruct(acc_shape, BF16), _land_shape(dwd16)],
        scratch_shapes=[pltpu.VMEM(acc_shape, BF16), pltpu.VMEM(acc_shape, F32), pltpu.SemaphoreType.DMA((2,)),
                        pltpu.SemaphoreType.DMA((n_peer,)), pltpu.SemaphoreType.DMA((n_peer,))],
        compiler_params=_params(),
    )(du, x2, norm2_gain, sc2, sh2, dwd16)


def _bwd_mixer(dx2, x, z, a1, sp, yb, o1, vec, conv_w, wpair, wpair_t, causal_mask, w_in_g, w_out_g, dwup16):
    seq = x.shape[0]
    n_tiles = seq // TILE
    t = TILE
    names = ["norm1_gain", "sc1", "sh1", "gt1", "conv_ln_g", "conv_ln_b", "gm_ln_g", "gm_ln_b", "mix_out_gain"]
    vecs = [vec[k] for k in names]

    def body(dx2_ref, x_ref, z_ref, a1_ref, sp_ref, y_ref, o1_ref, g1, sc1, sh1, gt1, clg, clb, vg, vb, mg,
             cw, wp, wpt, mask_ref, win_hbm, wout_hbm, dwup16_hbm,
             gx_ref, dg1_ref, dsc1_ref, dsh1_ref, dgt1_ref, dcw_ref, dcb_ref, dclg_ref, dclb_ref, dvg_ref, dvb_ref,
             dmg_ref, dws_ref, dbs_ref, dwin_hbm, dwout_hbm, land_hbm, dwin16_hbm, dwout16_hbm,
             win_v, wout_v, dwin_acc, dwout_acc, carry, bank, dbs_acc, lwin, lwout, sem, send_sems, recv_sems,
             pair_send, pair_recv):
        i = pl.program_id(0)
        small = [dg1_ref, dsc1_ref, dsh1_ref, dgt1_ref, dcw_ref, dcb_ref, dclg_ref, dclb_ref, dvg_ref, dvb_ref,
                 dmg_ref, dws_ref, dbs_acc]

        @pl.when(i == 0)
        def _():
            for cp in _scatter_copies(dwup16_hbm, land_hbm, send_sems, recv_sems):
                cp.start()
            cps = [pltpu.make_async_copy(win_hbm, win_v, sem.at[0]),
                   pltpu.make_async_copy(wout_hbm, wout_v, sem.at[1])]
            for cp in cps:
                cp.start()
            for cp in cps:
                cp.wait()
            dwin_acc[...] = jnp.zeros_like(dwin_acc)
            dwout_acc[...] = jnp.zeros_like(dwout_acc)
            carry[...] = jnp.zeros_like(carry)
            for ref in small:
                ref[...] = jnp.zeros_like(ref)

        dx2v = dx2_ref[...]
        gt1v = gt1[...]
        dgt1_ref[...] += _colsum(dx2v * o1_ref[...])
        do1b = (gt1v * dx2v).astype(BF16)
        dy = _nt(do1b, wout_v[...])
        dwout_acc[...] += _tn(y_ref[...], do1b)

        mgv = mg[...]
        _, conv_vjp = _conv_branch_vjp(a1_ref[...], clg[...], clb[...], mgv[:, :D_HALF])
        da1, dclg, dclb, dmg_a = conv_vjp(dy[:, :D_HALF])
        dclg_ref[...] += dclg
        dclb_ref[...] += dclb
        gu = z_ref[:, 2 * D_HALF:3 * D_HALF]
        gv = z_ref[:, 3 * D_HALF:]
        spv = sp_ref[...]
        _, gate_vjp = _gate_branch_vjp(gu, spv, mgv[:, D_HALF:])
        dgu, dsp, dmg_g = gate_vjp(dy[:, D_HALF:])
        dmg_ref[...] += jnp.concatenate([dmg_a, dmg_g], axis=1)
        gvn, gv_vjp = _gv_norm_vjp(gv, vg[...], vb[...])
        gvnb = gvn.astype(BF16)
        dspb = dsp.astype(BF16)
        dgvn = _head_pair_matmul(wpt, dspb)
        dgv, dvg, dvb = gv_vjp(dgvn)
        dvg_ref[...] += dvg
        dvb_ref[...] += dvb
        lane = lax.broadcasted_iota(jnp.int32, (CHUNK, CHUNK), 1)
        dbs = jnp.zeros((CHUNK, D_HALF), F32)
        for n in range(t // CHUNK):
            rows = slice(n * CHUNK, (n + 1) * CHUNK)
            dbs = dbs + dsp[rows, :]
            for j in range(N_HEADS // 2):
                cols = slice(j * CHUNK, (j + 1) * CHUNK)
                blk = dspb[rows, cols]
                zero = jnp.zeros_like(blk)
                vblk = gvnb[rows, cols]
                dws_ref[2 * j] += _nt(jnp.where(lane < HEAD_DIM, blk, zero), vblk)
                dws_ref[2 * j + 1] += _nt(jnp.where(lane < HEAD_DIM, zero, blk), vblk)
        dbs_acc[...] += dbs

        h1, h1_vjp = _mod_norm_vjp(x_ref[...], g1[...], sc1[...], sh1[...])
        h1b = h1.astype(BF16)
        dh1 = jnp.zeros((t, D_MODEL), F32)
        for k, dzk in ((2, dgu), (3, dgv)):
            dzb = dzk.astype(BF16)
            dh1 = dh1 + _nt(dzb, win_v[k])
            dwin_acc[k] += _tn(h1b, dzb)

        ca = z_ref[:, :D_HALF]
        cg = z_ref[:, D_HALF:2 * D_HALF]
        sig = jax.nn.sigmoid(cg)
        a0 = ca * sig
        ext = jnp.concatenate([da1, carry[...]], axis=0)
        carry[...] = da1[:HALO]
        bank[0] = ext
        for b in range(1, 8):
            bank[b] = pltpu.roll(ext, t + HALO - b, axis=0)
        dcb_ref[...] += _colsum(da1)
        da0 = jnp.zeros((t, D_HALF), F32)
        for s in range(CONV_K):
            q, b = divmod(s, 8)
            shifted = bank[b, pl.ds(8 * q, t), :]
            da0 = da0 + shifted * cw[pl.ds(CONV_K - 1 - s, 1), :]
            dcw_ref[pl.ds(CONV_K - 1 - s, 1), :] += _colsum(shifted * a0)
        dca = da0 * sig
        dcg = da0 * ca * sig * (1.0 - sig)

        for k, dzk in ((0, dca), (1, dcg)):
            dzb = dzk.astype(BF16)
            dh1 = dh1 + _nt(dzb, win_v[k])
            dwin_acc[k] += _tn(h1b, dzb)
        dx, dg1, dsc1, dsh1 = h1_vjp(dh1)
        gx_ref[...] = dx2v + dx
        dg1_ref[...] += dg1
        dsc1_ref[...] += dsc1
        dsh1_ref[...] += dsh1

        @pl.when(i == n_tiles - 1)
        def _():
            for h in range(N_HEADS):
                dws_ref[h] = dws_ref[h] * mask_ref[...]
            head_of_lane = lax.broadcasted_iota(jnp.int32, (N_HEADS, D_HALF), 1) // HEAD_DIM
            pick = (head_of_lane == lax.broadcasted_iota(jnp.int32, (N_HEADS, D_HALF), 0)).astype(F32)
            dbs_ref[...] = lax.dot_general(pick, dbs_acc[...], NT_DIMS, precision=lax.Precision.HIGHEST,
                                           preferred_element_type=F32)
            for k in range(N_SHARD):
                win_v[k] = dwin_acc[k].astype(BF16)
            wout_v[...] = dwout_acc[...].astype(BF16)
            mx, my, mc = _coords()
            h_in, h_out = dwin_acc.shape[1] // 2, dwout_acc.shape[0] // (2 * N_SHARD)

            def in_rows(ref, k, which):
                return ref.at[k, pl.ds(pl.multiple_of(which * h_in, 16), h_in), :]

            def out_rows(ref, k, which):
                return ref.at[pl.ds(pl.multiple_of((2 * k + which) * h_out, 16), h_out), :]

            pairs = ((win_v, dwin_acc, lwin, in_rows, dwin_hbm, dwin16_hbm),
                     (wout_v, dwout_acc, lwout, out_rows, dwout_hbm, dwout16_hbm))
            swaps = [pltpu.make_async_remote_copy(
                src_ref=rows_of(v16, k, 1 - mc), dst_ref=land.at[k], send_sem=pair_send.at[w, k],
                recv_sem=pair_recv.at[w, k], device_id=(mx, my, 1 - mc), device_id_type=MESH)
                for w, (v16, _, land, rows_of, _, _) in enumerate(pairs) for k in range(N_SHARD)]
            for cp in swaps:
                cp.start()
            for cp in swaps:
                cp.wait()
            outs = []
            for w, (v16, acc, land, rows_of, half_hbm, half16_hbm) in enumerate(pairs):
                for k in range(N_SHARD):
                    total = rows_of(acc, k, mc)[...] + land[k].astype(F32)
                    rows_of(acc, k, 0)[...] = total
                    rows_of(v16, k, 0)[...] = total.astype(BF16)
                    outs.append(pltpu.make_async_copy(rows_of(acc, k, 0), half_hbm.at[k], sem.at[2 + 8 * w + k]))
                    outs.append(pltpu.make_async_copy(rows_of(v16, k, 0), half16_hbm.at[k], sem.at[6 + 8 * w + k]))
            for cp in outs:
                cp.start()
            for cp in outs:
                cp.wait()
            for rc in _scatter_copies(dwup16_hbm, land_hbm, send_sems, recv_sems):
                rc.wait()

    def rev(width):
        return pl.BlockSpec((t, width), lambda i: (n_tiles - 1 - i, 0))

    v1024 = jax.ShapeDtypeStruct((1, D_MODEL), F32)
    v512 = jax.ShapeDtypeStruct((1, D_HALF), F32)
    small_shapes = [v1024, v1024, v1024, v1024, jax.ShapeDtypeStruct((CONV_K, D_HALF), F32), v512, v512, v512, v512,
                    v512, v1024, jax.ShapeDtypeStruct((N_HEADS, CHUNK, CHUNK), F32),
                    jax.ShapeDtypeStruct((N_HEADS, CHUNK), F32)]
    n_peer = len(PEER_FLIPS)
    half_in = (N_SHARD, w_in_g.shape[1] // 2, w_in_g.shape[2])
    half_out = (N_SHARD, w_out_g.shape[0] // (2 * N_SHARD), w_out_g.shape[1])
    return pl.pallas_call(
        body, grid=(n_tiles,), name="bwd_mixer",
        in_specs=[rev(D_MODEL), rev(D_MODEL), rev(4 * D_HALF), rev(D_HALF), rev(D_HALF), rev(D_MODEL),
                  rev(D_MODEL)] + [_full(v.shape) for v in vecs]
        + [_full(conv_w.shape), _full(wpair.shape), _full(wpair_t.shape), _full(causal_mask.shape), ANY, ANY, ANY],
        out_specs=[rev(D_MODEL)] + [_full(s.shape) for s in small_shapes] + [ANY] * 5,
        out_shape=[jax.ShapeDtypeStruct((seq, D_MODEL), F32)] + small_shapes
        + [jax.ShapeDtypeStruct(half_in, F32), jax.ShapeDtypeStruct(half_out, F32), _land_shape(dwup16),
           jax.ShapeDtypeStruct(half_in, BF16), jax.ShapeDtypeStruct(half_out, BF16)],
        scratch_shapes=[pltpu.VMEM(w_in_g.shape, BF16), pltpu.VMEM(w_out_g.shape, BF16),
                        pltpu.VMEM(w_in_g.shape, F32), pltpu.VMEM(w_out_g.shape, F32),
                        pltpu.VMEM((HALO, D_HALF), F32), pltpu.VMEM((8, t + HALO, D_HALF), F32),
                        pltpu.VMEM((CHUNK, D_HALF), F32), pltpu.VMEM(half_in, BF16), pltpu.VMEM(half_out, BF16),
                        pltpu.SemaphoreType.DMA((2 + 4 * N_SHARD,)),
                        pltpu.SemaphoreType.DMA((n_peer,)), pltpu.SemaphoreType.DMA((n_peer,)),
                        pltpu.SemaphoreType.DMA((2, N_SHARD)), pltpu.SemaphoreType.DMA((2, N_SHARD))],
        compiler_params=_params(),
    )(dx2, x, z, a1, sp, yb, o1, *vecs, conv_w, wpair, wpair_t, causal_mask, w_in_g, w_out_g, dwup16)


def _gmlp_operands(gm_ws, gm_bs):
    mask = jnp.tril(jnp.ones((CHUNK, CHUNK), F32))
    ws = gm_ws * mask[None]
    wpair = ws.reshape(N_HEADS // 2, 2 * CHUNK, CHUNK).astype(BF16)
    wpair_t = jnp.swapaxes(ws, 1, 2).reshape(N_HEADS // 2, 2 * CHUNK, CHUNK).astype(BF16)
    bs_full = jnp.repeat(jnp.transpose(gm_bs), HEAD_DIM, axis=1)
    return wpair, wpair_t, bs_full, mask


def _local_step(x, target, mod, p, w_in_g, w_out_g, w_up_part, w_down_part):
    sh1, sc1, gt1, sh2, sc2, gt2 = [mod[:, k * D_MODEL:(k + 1) * D_MODEL] for k in range(6)]
    vec = dict(p, sh1=sh1, sc1=sc1, gt1=gt1, sh2=sh2, sc2=sc2, gt2=gt2)
    wpair, wpair_t, bs_full, mask = _gmlp_operands(p["gm_ws"], p["gm_bs"])

    (z, a1, sp, yb, o1, x2), (w_up_g, w_down_g) = _fwd_mixer(
        x, vec, p["conv_dw_w"], wpair, bs_full, w_in_g, w_out_g, [w_up_part, w_down_part])
    w_down_g = w_down_g.reshape(D_FF, D_MODEL)
    to_inter, to_natural = _interleave_matrices()
    du, dx2, d_ffn_w, d_ffn_b, d_fg, d_gt2, d_g2, d_sc2, d_sh2, loss, d_wd, d_wd16 = _ffn(
        x2, target, p["norm2_gain"], sc2, sh2, p["ffn_dw_w"], p["ffn_dw_b"], gt2, p["final_gain"], w_up_g, w_down_g,
        to_inter, to_natural)
    by_shard = (N_SHARD, -1, D_MODEL)
    d_wup, d_wup16, land_wd = _bwd_up(du, x2, p["norm2_gain"], sc2, sh2, d_wd16.reshape(by_shard))
    (gx, d_g1, d_sc1, d_sh1, d_gt1, d_cw, d_cb, d_clg, d_clb, d_vg, d_vb, d_mg, d_ws, d_bs, d_win, d_wout, land_wup,
     d_win16, d_wout16) = _bwd_mixer(dx2, x, z, a1, sp, yb, o1, vec, p["conv_dw_w"], wpair, wpair_t, mask, w_in_g,
                                     w_out_g, d_wup16)
    d_mod = jnp.concatenate([d_sh1, d_sc1, d_gt1, d_sh2, d_sc2, d_gt2], axis=1)
    grads = dict(norm1_gain=d_g1, conv_dw_w=d_cw, conv_dw_b=d_cb, conv_ln_g=d_clg, conv_ln_b=d_clb, gm_ln_g=d_vg,
                 gm_ln_b=d_vb, gm_ws=d_ws, gm_bs=d_bs, mix_out_gain=d_mg, norm2_gain=d_g2, ffn_dw_w=d_ffn_w,
                 ffn_dw_b=d_ffn_b, final_gain=d_fg, w_in=d_win, w_out=d_wout, w_up=d_wup, w_down=d_wd.reshape(by_shard))
    in_flight = dict(w_in16=d_win16, w_out16=d_wout16, land_w_up=land_wup, land_w_down=land_wd)
    return gx, grads, d_mod, loss, in_flight


MESH = pl.DeviceIdType.MESH
VMEM_SPEC = pl.BlockSpec(memory_space=pltpu.VMEM)
PEER_FLIPS = [(a, b, d) for a in (0, 1) for b in (0, 1) for d in (0, 1)][1:]
CHIP_FLIPS = [(1, 0), (0, 1), (1, 1)]


def _coords():
    return lax.axis_index("x"), lax.axis_index("y"), lax.axis_index("c")


def _flip(v, bit):
    return 1 - v if bit else v


def _rows8(block):
    return pl.ds(pl.multiple_of(8 * block, 8), 8)


def _ada_mod(c_row, w_ada_sh, b_ada_sh):
    cols = w_ada_sh.shape[1]

    def body(c_ref, w_ref, b_ref, call_ref, mod_ref, cpad, modall, send_sems, recv_sems):
        x, y, c = _coords()
        me = 4 * x + 2 * y + c
        cpad[...] = jnp.zeros_like(cpad)
        cpad[pl.ds(0, 1), :] = c_ref[...]

        def gather_copy(j, flip):
            peer = (_flip(x, flip[0]), _flip(y, flip[1]), _flip(c, flip[2]))
            return pltpu.make_async_remote_copy(
                src_ref=cpad, dst_ref=call_ref.at[_rows8(me)], send_sem=send_sems.at[j], recv_sem=recv_sems.at[j],
                device_id=peer, device_id_type=MESH)

        copies = [gather_copy(j, f) for j, f in enumerate(PEER_FLIPS)]
        for cp in copies:
            cp.start()
        call_ref[_rows8(me), :] = cpad[...]
        for cp in copies:
            cp.wait_recv()
        for cp in copies:
            cp.wait_send()
        cv = call_ref[...]
        c_act = (cv * jax.nn.sigmoid(cv)).astype(BF16)
        modall[...] = _nn(c_act, w_ref[...].astype(BF16)) + b_ref[...]

        slot = _rows8(2 * x + y)

        def piece_copy(j, flip):
            tx, ty = _flip(x, flip[0]), _flip(y, flip[1])
            return pltpu.make_async_remote_copy(
                src_ref=modall.at[_rows8(4 * tx + 2 * ty + c)], dst_ref=mod_ref.at[slot],
                send_sem=send_sems.at[len(PEER_FLIPS) + j], recv_sem=recv_sems.at[len(PEER_FLIPS) + j],
                device_id=(tx, ty, c), device_id_type=MESH)

        pieces = [piece_copy(j, f) for j, f in enumerate(CHIP_FLIPS)]
        for cp in pieces:
            cp.start()
        mod_ref[slot, :] = modall[_rows8(me), :]
        for cp in pieces:
            cp.wait_recv()
        for cp in pieces:
            cp.wait_send()

    n_sem = len(PEER_FLIPS) + len(CHIP_FLIPS)
    return pl.pallas_call(
        body, name="ada_mod",
        in_specs=[VMEM_SPEC, VMEM_SPEC, VMEM_SPEC], out_specs=[VMEM_SPEC, VMEM_SPEC],
        out_shape=[jax.ShapeDtypeStruct((8 * N_DEV, D_MODEL), F32), jax.ShapeDtypeStruct((8 * N_SHARD, cols), F32)],
        scratch_shapes=[pltpu.VMEM((8, D_MODEL), F32), pltpu.VMEM((8 * N_DEV, cols), F32),
                        pltpu.SemaphoreType.DMA((n_sem,)), pltpu.SemaphoreType.DMA((n_sem,))],
        compiler_params=pltpu.CompilerParams(vmem_limit_bytes=VMEM_LIMIT_BYTES),
    )(c_row, w_ada_sh, b_ada_sh)


def _gather_weights(shards, filters, n_now):
    n = len(shards)
    nf = len(filters)

    def body(*refs):
        ins, f_ins = refs[:n], refs[n:n + nf]
        outs, f_outs = refs[n + nf:2 * n + nf], refs[2 * n + nf:2 * (n + nf)]
        stage = refs[2 * (n + nf):3 * n + 2 * nf]
        send_sems, recv_sems, local_sems, f_send_sems, f_recv_sems = refs[3 * n + 2 * nf:]
        x, y, c = _coords()
        k = 2 * x + y
        sibling = (x, y, 1 - c)

        def filter_copy(w, j, slot):
            tx, ty = _flip(x, CHIP_FLIPS[j][0]), _flip(y, CHIP_FLIPS[j][1])
            return pltpu.make_async_remote_copy(
                src_ref=f_ins[w], dst_ref=f_outs[w].at[slot], send_sem=f_send_sems.at[w, j],
                recv_sem=f_recv_sems.at[w, j], device_id=(tx, ty, c), device_id_type=MESH)

        def half(w, which):
            h = shards[w].shape[0] // 2
            return pl.ds(pl.multiple_of(which * h, 16), h)

        def ici_copy(w, j, src, slot):
            tx, ty = _flip(x, CHIP_FLIPS[j][0]), _flip(y, CHIP_FLIPS[j][1])
            return pltpu.make_async_remote_copy(
                src_ref=src, dst_ref=outs[w].at[slot, half(w, c)], send_sem=send_sems.at[w, j],
                recv_sem=recv_sems.at[w, j], device_id=(tx, ty, c), device_id_type=MESH)

        def d2d_copy(w, j, slot, which):
            rows = outs[w].at[slot, half(w, which)]
            return pltpu.make_async_remote_copy(
                src_ref=rows, dst_ref=rows, send_sem=send_sems.at[w, len(CHIP_FLIPS) + j],
                recv_sem=recv_sems.at[w, len(CHIP_FLIPS) + j], device_id=sibling, device_id_type=MESH)

        def chip_of(j):
            return 2 * _flip(x, CHIP_FLIPS[j][0]) + _flip(y, CHIP_FLIPS[j][1])

        local, first, passed = [], [], []
        for w in range(nf):
            local.append(pltpu.make_async_copy(f_ins[w], f_outs[w].at[k], local_sems.at[n + w]))
            local[-1].start()
            for j in range(len(CHIP_FLIPS)):
                first.append(filter_copy(w, j, k))
                first[-1].start()
        for w in range(n):
            stage[w][...] = ins[w][...].astype(BF16)
            local.append(pltpu.make_async_copy(stage[w], outs[w].at[k], local_sems.at[w]))
            local[-1].start()
            if w < n_now:
                for j in range(len(CHIP_FLIPS)):
                    first.append(ici_copy(w, j, stage[w].at[half(w, c)], k))
                    first[-1].start()
        for w in range(nf):
            for j in range(len(CHIP_FLIPS)):
                filter_copy(w, j, chip_of(j)).wait_recv()
        for w in range(n_now):
            for j in range(len(CHIP_FLIPS)):
                ici_copy(w, j, stage[w].at[half(w, c)], chip_of(j)).wait_recv()
                passed.append(d2d_copy(w, j, chip_of(j), c))
                passed[-1].start()
        for w in range(n_now):
            for j in range(len(CHIP_FLIPS)):
                d2d_copy(w, j, chip_of(j), 1 - c).wait_recv()
        for cp in first + passed:
            cp.wait_send()
        for cp in local:
            cp.wait()

    sem_shape = (n_now, 2 * len(CHIP_FLIPS))
    f_sem_shape = (nf, len(CHIP_FLIPS))
    outs = pl.pallas_call(
        body, name="gather_weights",
        in_specs=[VMEM_SPEC] * (n + nf), out_specs=[ANY] * (n + nf),
        out_shape=[jax.ShapeDtypeStruct((N_SHARD,) + s.shape, BF16) for s in shards]
        + [jax.ShapeDtypeStruct((N_SHARD,) + s.shape, F32) for s in filters],
        scratch_shapes=[pltpu.VMEM(s.shape, BF16) for s in shards]
        + [pltpu.SemaphoreType.DMA(sem_shape), pltpu.SemaphoreType.DMA(sem_shape), pltpu.SemaphoreType.DMA((n + nf,)),
           pltpu.SemaphoreType.DMA(f_sem_shape), pltpu.SemaphoreType.DMA(f_sem_shape)],
        compiler_params=pltpu.CompilerParams(vmem_limit_bytes=VMEM_LIMIT_BYTES),
    )(*shards, *filters)
    return outs[:n], outs[n:]


def _final_comm(srcs16, small):
    n = len(srcs16)
    rows = small.shape[0]
    half = rows // 2

    def body(*refs):
        srcs, small_ref = refs[:n], refs[n]
        lands, small_out = refs[n + 1:2 * n + 1], refs[2 * n + 1]
        chip_sum, got_c, got_x, got_y, part_x, send_sems, recv_sems, small_send_sems, small_recv_sems = refs[2 * n + 2:]
        x, y, c = _coords()
        sibling = (x, y, 1 - c)
        mine = pl.ds(pl.multiple_of(c * half, 8), half)
        copies = []
        for w in range(n):
            for j, flip in enumerate(CHIP_FLIPS):
                tx, ty = _flip(x, flip[0]), _flip(y, flip[1])
                copies.append(pltpu.make_async_remote_copy(
                    src_ref=srcs[w].at[2 * tx + ty], dst_ref=lands[w].at[j], send_sem=send_sems.at[w, j],
                    recv_sem=recv_sems.at[w, j], device_id=(tx, ty, c), device_id_type=MESH))
        for cp in copies:
            cp.start()

        def exchange(stage, src, dst, peer):
            rc = pltpu.make_async_remote_copy(
                src_ref=src, dst_ref=dst, send_sem=small_send_sems.at[stage], recv_sem=small_recv_sems.at[stage],
                device_id=peer, device_id_type=MESH)
            rc.start()
            rc.wait()

        exchange(0, small_ref, got_c, sibling)
        chip_sum[...] = small_ref[...] + got_c[...]
        exchange(1, chip_sum.at[mine], got_x, (1 - x, y, c))
        part_x[...] = chip_sum[mine, :] + got_x[...]
        exchange(2, part_x, got_y, (x, 1 - y, c))
        small_out[mine, :] = part_x[...] + got_y[...]
        exchange(3, small_out.at[mine], small_out.at[mine], sibling)
        for cp in copies:
            cp.wait()

    n_chip = len(CHIP_FLIPS)
    half_shape = (half, small.shape[1])
    outs = pl.pallas_call(
        body, name="final_comm",
        in_specs=[ANY] * n + [VMEM_SPEC], out_specs=[ANY] * n + [VMEM_SPEC],
        out_shape=[jax.ShapeDtypeStruct((n_chip,) + a.shape[1:], BF16) for a in srcs16]
        + [jax.ShapeDtypeStruct(small.shape, F32)],
        scratch_shapes=[pltpu.VMEM(small.shape, F32), pltpu.VMEM(small.shape, F32), pltpu.VMEM(half_shape, F32),
                        pltpu.VMEM(half_shape, F32), pltpu.VMEM(half_shape, F32),
                        pltpu.SemaphoreType.DMA((n, n_chip)), pltpu.SemaphoreType.DMA((n, n_chip)),
                        pltpu.SemaphoreType.DMA((4,)), pltpu.SemaphoreType.DMA((4,))],
        compiler_params=pltpu.CompilerParams(vmem_limit_bytes=VMEM_LIMIT_BYTES),
    )(*srcs16, small)
    return outs[:n], outs[n]


ADD_CHUNKS = 4


def _scatter_sum(pos, owns, lands):
    n = len(owns)

    def specs(own_shape, land_shape):
        peers, rows, cols = land_shape
        pick = 1 if own_shape[1] == 2 * rows else 0
        if cols % (128 * ADD_CHUNKS) == 0:
            blk = (rows, cols // ADD_CHUNKS)
            return (pl.BlockSpec((1,) + blk, lambda i, p: (2 * p[0] + p[1], pick * p[2], i)),
                    pl.BlockSpec((peers,) + blk, lambda i, p: (0, 0, i)),
                    pl.BlockSpec((1,) + blk, lambda i, p: (p[2], 0, i)))
        blk = (rows // ADD_CHUNKS, cols)
        return (pl.BlockSpec((1,) + blk, lambda i, p: (2 * p[0] + p[1], pick * p[2] * ADD_CHUNKS + i, 0)),
                pl.BlockSpec((peers,) + blk, lambda i, p: (0, i, 0)),
                pl.BlockSpec((1,) + blk, lambda i, p: (p[2], i, 0)))

    def body(pos_ref, *refs):
        for idx in range(n):
            own, land, out = refs[idx], refs[n + idx], refs[2 * n + idx]
            total = own[0]
            for f in range(land.shape[0]):
                total = total + land[f].astype(F32)
            out[0] = total

    all_specs = [specs(o.shape, l.shape) for o, l in zip(owns, lands)]
    return pl.pallas_call(
        body, name="scatter_sum",
        grid_spec=pltpu.PrefetchScalarGridSpec(
            num_scalar_prefetch=1, grid=(ADD_CHUNKS,),
            in_specs=[s[0] for s in all_specs] + [s[1] for s in all_specs], out_specs=[s[2] for s in all_specs]),
        out_shape=[jax.ShapeDtypeStruct((2,) + l.shape[1:], F32) for l in lands],
        compiler_params=_params(),
    )(pos, *owns, *lands)


def _swap_halves(halves):
    n = len(halves)

    def body(*refs):
        ins, outs = refs[:n], refs[n:2 * n]
        send_sems, recv_sems = refs[2 * n:]
        x, y, c = _coords()
        copies = [pltpu.make_async_remote_copy(
            src_ref=ins[idx].at[pl.ds(c, 1)], dst_ref=outs[idx].at[pl.ds(c, 1)], send_sem=send_sems.at[idx],
            recv_sem=recv_sems.at[idx], device_id=(x, y, 1 - c), device_id_type=MESH) for idx in range(n)]
        for cp in copies:
            cp.start()
        for cp in copies:
            cp.wait()

    return pl.pallas_call(
        body, name="swap_halves",
        in_specs=[ANY] * n, out_specs=[ANY] * n, input_output_aliases={idx: idx for idx in range(n)},
        out_shape=[jax.ShapeDtypeStruct(a.shape, F32) for a in halves],
        scratch_shapes=[pltpu.SemaphoreType.DMA((n,)), pltpu.SemaphoreType.DMA((n,))],
    )(*halves)


def _adamw_math(w, g, m, v):
    m = ADAM_B1 * m + (1.0 - ADAM_B1) * g
    v = ADAM_B2 * v + (1.0 - ADAM_B2) * jnp.square(g)
    m_hat = m / (1.0 - ADAM_B1 ** ADAM_STEP)
    v_hat = v / (1.0 - ADAM_B2 ** ADAM_STEP)
    delta = -ADAM_LR * (m_hat / (jnp.sqrt(v_hat) + ADAM_EPS) + ADAM_WD * w)
    return delta, m, v


def _adamw(name, w, g, m, v, block_rows):
    rows, cols = w.shape

    def body(w_ref, g_ref, m_ref, v_ref, d_out, m_out, v_out):
        d_out[...], m_out[...], v_out[...] = _adamw_math(w_ref[...], g_ref[...], m_ref[...], v_ref[...])

    spec = pl.BlockSpec((block_rows, cols), lambda i: (i, 0))
    shape = jax.ShapeDtypeStruct((rows, cols), F32)
    return pl.pallas_call(
        body, grid=(rows // block_rows,), name=name, in_specs=[spec] * 4, out_specs=[spec] * 3,
        out_shape=[shape] * 3, compiler_params=_params(),
    )(w, g, m, v)


def _adamw_many(ws, gs, ms, vs):
    n = len(ws)

    def body(*refs):
        w_refs, g_refs, m_refs, v_refs = (refs[q * n:(q + 1) * n] for q in range(4))
        d_outs, m_outs, v_outs = (refs[(4 + q) * n:(5 + q) * n] for q in range(3))
        for idx in range(n):
            d_outs[idx][...], m_outs[idx][...], v_outs[idx][...] = _adamw_math(
                w_refs[idx][...], g_refs[idx][...], m_refs[idx][...], v_refs[idx][...])

    shapes = [jax.ShapeDtypeStruct(w.shape, F32) for w in ws]
    outs = pl.pallas_call(
        body, name="adamw_small", in_specs=[VMEM_SPEC] * (4 * n), out_specs=[VMEM_SPEC] * (3 * n),
        out_shape=shapes * 3, compiler_params=pltpu.CompilerParams(vmem_limit_bytes=VMEM_LIMIT_BYTES),
    )(*ws, *gs, *ms, *vs)
    return outs[:n], outs[n:2 * n], outs[2 * n:]


def _adamw_ada(c_all16, dmod16, w, m, v, block_rows):
    rows, cols = w.shape

    def body(c_ref, dm_ref, w_ref, m_ref, v_ref, g_out, d_out, m_out, v_out):
        cv = c_ref[...]
        g = _tn((cv * jax.nn.sigmoid(cv)).astype(BF16), dm_ref[...].astype(BF16))
        g_out[...] = g
        d_out[...], m_out[...], v_out[...] = _adamw_math(w_ref[...], g, m_ref[...], v_ref[...])

    spec = pl.BlockSpec((block_rows, cols), lambda i: (i, 0))
    shape = jax.ShapeDtypeStruct((rows, cols), F32)
    return pl.pallas_call(
        body, grid=(rows // block_rows,), name="adamw_w_ada",
        in_specs=[pl.BlockSpec((16, block_rows), lambda i: (0, i)), _full(dmod16.shape), spec, spec, spec],
        out_specs=[spec] * 4, out_shape=[shape] * 4, compiler_params=_params(),
    )(c_all16, dmod16, w, m, v)


SMALL_REPLICATED = ["b_ada", "norm1_gain", "conv_dw_b", "conv_ln_g", "conv_ln_b", "gm_ln_g", "gm_ln_b", "gm_ws", "gm_bs",
                    "mix_out_gain", "norm2_gain", "ffn_dw_b", "final_gain"]
SMALL_SHARDED = ["conv_dw_w", "ffn_dw_w"]
PACK_ROWS = 256
WEIGHT_ORDER = ["w_ada", "b_ada", "norm1_gain", "w_in", "conv_dw_w", "conv_dw_b", "conv_ln_g", "conv_ln_b", "gm_ln_g",
                "gm_ln_b", "gm_ws", "gm_bs", "mix_out_gain", "w_out", "norm2_gain", "w_up", "ffn_dw_w", "ffn_dw_b",
                "w_down", "final_gain"]


def _pack(parts, rows):
    flat = jnp.concatenate([a.reshape(-1) for a in parts])
    return jnp.pad(flat, (0, rows * D_MODEL - flat.shape[0])).reshape(rows, D_MODEL)


def _unpack(packed, shapes):
    flat = packed.reshape(-1)
    out, pos = [], 0
    for s in shapes:
        size = 1
        for d in s:
            size *= d
        out.append(flat[pos:pos + size].reshape(s))
        pos += size
    return out


def kernel(x, c, w_ada, b_ada, norm1_gain, w_in, conv_dw_w, conv_dw_b, conv_ln_g, conv_ln_b, gm_ln_g, gm_ln_b, gm_ws, gm_bs, mix_out_gain, w_out, norm2_gain, w_up, ffn_dw_w, ffn_dw_b, w_down, final_gain, loss_target, m_w_ada, m_b_ada, m_norm1_gain, m_w_in, m_conv_dw_w, m_conv_dw_b, m_conv_ln_g, m_conv_ln_b, m_gm_ln_g, m_gm_ln_b, m_gm_ws, m_gm_bs, m_mix_out_gain, m_w_out, m_norm2_gain, m_w_up, m_ffn_dw_w, m_ffn_dw_b, m_w_down, m_final_gain, v_w_ada, v_b_ada, v_norm1_gain, v_w_in, v_conv_dw_w, v_conv_dw_b, v_conv_ln_g, v_conv_ln_b, v_gm_ln_g, v_gm_ln_b, v_gm_ws, v_gm_bs, v_mix_out_gain, v_w_out, v_norm2_gain, v_w_up, v_ffn_dw_w, v_ffn_dw_b, v_w_down, v_final_gain):
    weights = dict(w_ada=w_ada, b_ada=b_ada, norm1_gain=norm1_gain, w_in=w_in, conv_dw_w=conv_dw_w, conv_dw_b=conv_dw_b,
                   conv_ln_g=conv_ln_g, conv_ln_b=conv_ln_b, gm_ln_g=gm_ln_g, gm_ln_b=gm_ln_b, gm_ws=gm_ws, gm_bs=gm_bs,
                   mix_out_gain=mix_out_gain, w_out=w_out, norm2_gain=norm2_gain, w_up=w_up, ffn_dw_w=ffn_dw_w,
                   ffn_dw_b=ffn_dw_b, w_down=w_down, final_gain=final_gain)
    mom1 = dict(w_ada=m_w_ada, b_ada=m_b_ada, norm1_gain=m_norm1_gain, w_in=m_w_in, conv_dw_w=m_conv_dw_w,
                conv_dw_b=m_conv_dw_b, conv_ln_g=m_conv_ln_g, conv_ln_b=m_conv_ln_b, gm_ln_g=m_gm_ln_g, gm_ln_b=m_gm_ln_b,
                gm_ws=m_gm_ws, gm_bs=m_gm_bs, mix_out_gain=m_mix_out_gain, w_out=m_w_out, norm2_gain=m_norm2_gain,
                w_up=m_w_up, ffn_dw_w=m_ffn_dw_w, ffn_dw_b=m_ffn_dw_b, w_down=m_w_down, final_gain=m_final_gain)
    mom2 = dict(w_ada=v_w_ada, b_ada=v_b_ada, norm1_gain=v_norm1_gain, w_in=v_w_in, conv_dw_w=v_conv_dw_w,
                conv_dw_b=v_conv_dw_b, conv_ln_g=v_conv_ln_g, conv_ln_b=v_conv_ln_b, gm_ln_g=v_gm_ln_g, gm_ln_b=v_gm_ln_b,
                gm_ws=v_gm_ws, gm_bs=v_gm_bs, mix_out_gain=v_mix_out_gain, w_out=v_w_out, norm2_gain=v_norm2_gain,
                w_up=v_w_up, ffn_dw_w=v_ffn_dw_w, ffn_dw_b=v_ffn_dw_b, w_down=v_w_down, final_gain=v_final_gain)
    shard = 2 * lax.axis_index("x") + lax.axis_index("y")
    me = 2 * shard + lax.axis_index("c")

    ada_cols = w_ada.shape[2]
    b_ada_sh = lax.dynamic_slice(b_ada, (0, shard * ada_cols), (1, ada_cols))
    c_all64, mod32 = _ada_mod(c, w_ada[0], b_ada_sh)
    c_all = c_all64[::8]
    mod = mod32[::8].reshape(1, N_SHARD * ada_cols)

    (w_in_g, w_out_g, w_up_part, w_down_part), (conv_w_g, ffn_w_g) = _gather_weights(
        [w_in[0], w_out[0], w_up[0], w_down[0]], [conv_dw_w[0], ffn_dw_w[0]], n_now=2)
    conv_w_full = jnp.transpose(conv_w_g, (1, 0, 2)).reshape(CONV_K, D_HALF)
    ffn_w_full = jnp.transpose(ffn_w_g, (1, 0, 2)).reshape(FFN_K, 2 * D_FF)

    p = dict(norm1_gain=norm1_gain, conv_dw_w=conv_w_full, conv_dw_b=conv_dw_b, conv_ln_g=conv_ln_g,
             conv_ln_b=conv_ln_b, gm_ln_g=gm_ln_g, gm_ln_b=gm_ln_b, gm_ws=gm_ws[0], gm_bs=gm_bs[0],
             mix_out_gain=mix_out_gain, norm2_gain=norm2_gain, ffn_dw_w=ffn_w_full, ffn_dw_b=ffn_dw_b,
             final_gain=final_gain[None])
    grad_x, g, d_mod, loss, in_flight = _local_step(
        x[0], loss_target[0], mod, p, w_in_g, w_out_g.reshape(D_MODEL, D_MODEL), w_up_part, w_down_part)

    n_mod = d_mod.shape[1]
    dmod_rows = lax.dynamic_update_slice(jnp.zeros((N_DEV, n_mod), F32), d_mod, (me, 0))
    g["b_ada"] = d_mod
    small = _pack([g[k] for k in SMALL_REPLICATED] + [g[k] for k in SMALL_SHARDED] + [dmod_rows, loss[0, :1]], PACK_ROWS)
    (land_w_in, land_w_out), small = _final_comm([in_flight["w_in16"], in_flight["w_out16"]], small)
    pos = jnp.stack(_coords()).astype(jnp.int32)
    halves = _scatter_sum(pos, [g["w_in"], g["w_out"], g["w_up"], g["w_down"]],
                          [land_w_in, land_w_out, in_flight["land_w_up"], in_flight["land_w_down"]])
    full = _swap_halves(halves)
    grads = dict(w_in=full[0].reshape(w_in.shape[1:]), w_out=full[1].reshape(w_out.shape[1:]),
                 w_up=full[2].reshape(w_up.shape[1:]), w_down=full[3].reshape(w_down.shape[1:]))

    small_shapes = ([weights[k].shape for k in SMALL_REPLICATED] + [(CONV_K, D_HALF), (FFN_K, 2 * D_FF)]
                    + [(N_DEV, n_mod), (1,)])
    *small_grads, conv_w_grad, ffn_w_grad, dmod_all, loss_sum = _unpack(small, small_shapes)
    grads.update(zip(SMALL_REPLICATED, small_grads))
    grads["conv_dw_w"] = lax.dynamic_slice(conv_w_grad, (0, shard * conv_dw_w.shape[2]), conv_dw_w.shape[1:])[None]
    grads["ffn_dw_w"] = lax.dynamic_slice(ffn_w_grad, (0, shard * ffn_dw_w.shape[2]), ffn_dw_w.shape[1:])[None]

    delta, new_m, new_v = {}, {}, {}
    for name, block_rows in (("w_in", 256), ("w_out", 128), ("w_up", 256), ("w_down", 352)):
        delta[name], new_m[name], new_v[name] = [a[None] for a in _adamw(
            "adamw_" + name, weights[name][0], grads[name], mom1[name][0], mom2[name][0], block_rows)]
        grads[name] = grads[name][None]
    dmod_sh = lax.dynamic_slice(dmod_all, (0, shard * ada_cols), (N_DEV, ada_cols))
    pad8 = ((0, 16 - N_DEV), (0, 0))
    grads["w_ada"], delta["w_ada"], new_m["w_ada"], new_v["w_ada"] = [a[None] for a in _adamw_ada(
        jnp.pad(c_all, pad8), jnp.pad(dmod_sh, pad8), w_ada[0], m_w_ada[0], v_w_ada[0], 256)]
    small_names = SMALL_REPLICATED + SMALL_SHARDED

    def two_d(a):
        return a.reshape(1, -1) if a.ndim == 1 else a

    small_out = _adamw_many(*[[two_d(d[k]) for k in small_names] for d in (weights, grads, mom1, mom2)])
    for d, arrs in zip((delta, new_m, new_v), small_out):
        d.update({k: a.reshape(weights[k].shape) for k, a in zip(small_names, arrs)})

    return (loss_sum.reshape(()), grad_x[None], *[grads[k] for k in WEIGHT_ORDER], *[delta[k] for k in WEIGHT_ORDER],
            *[new_m[k] for k in WEIGHT_ORDER], *[new_v[k] for k in WEIGHT_ORDER])
```

```python
import functools

import jax
import jax.numpy as jnp
from jax import lax
from jax.experimental import pallas as pl
from jax.experimental.pallas import tpu as pltpu

F32 = jnp.float32
BF16 = jnp.bfloat16

D_MODEL = 1024
D_HALF = 512
D_FF = 2816
CONV_K = 31
FFN_K = 3
CHUNK = 128
N_HEADS = 8
HEAD_DIM = 64
N_SHARD = 4
N_DEV = 8
RMS_EPS = 1e-6
LN_EPS = 1e-5
ADAM_LR, ADAM_B1, ADAM_B2, ADAM_EPS, ADAM_WD, ADAM_STEP = 0.001, 0.9, 0.999, 1e-08, 0.01, 10

TILE = 256
HALO = 32
FFN_HALO = 16
FFN_BLK = 256
UP_SHARD = 2 * D_FF // N_SHARD
VMEM_LIMIT_BYTES = 56 * 1024 * 1024
FFN_VMEM_LIMIT_BYTES = 58 * 1024 * 1024

ANY = pl.BlockSpec(memory_space=pl.ANY)
NT_DIMS = (((1,), (1,)), ((), ()))
TN_DIMS = (((0,), (0,)), ((), ()))


def _full(shape):
    return pl.BlockSpec(shape, lambda i: (0,) * len(shape))


def _nn(a, b):
    return jnp.dot(a, b, preferred_element_type=F32)


def _nt(a, b):
    return lax.dot_general(a, b, NT_DIMS, preferred_element_type=F32)


def _tn(a, b):
    return lax.dot_general(a, b, TN_DIMS, preferred_element_type=F32)


def _colsum(a):
    return jnp.sum(a, axis=0, keepdims=True)


def _params(semantics=("arbitrary",)):
    return pltpu.CompilerParams(dimension_semantics=semantics, vmem_limit_bytes=VMEM_LIMIT_BYTES)


def _rms(v, gain):
    return v * lax.rsqrt(jnp.mean(v * v, axis=-1, keepdims=True) + RMS_EPS) * gain


def _layer_norm(v, gain, bias):
    mu = jnp.mean(v, axis=-1, keepdims=True)
    var = jnp.mean(jnp.square(v - mu), axis=-1, keepdims=True)
    return (v - mu) * lax.rsqrt(var + LN_EPS) * gain + bias


def _mod_norm(v, gain, scale, shift):
    return _rms(v, gain) * (1.0 + scale) + shift


def _conv_branch(a1, ln_g, ln_b, out_gain):
    a2 = _layer_norm(a1, ln_g, ln_b)
    return _rms(a2 * jax.nn.sigmoid(a2), out_gain)


def _gate_branch(gu, sp, out_gain):
    return _rms(jax.nn.gelu(gu) * sp, out_gain)


def _gv_norm(gv, ln_g, ln_b):
    return _layer_norm(jax.nn.gelu(gv), ln_g, ln_b)


def _rms_parts(v):
    r = lax.rsqrt(jnp.mean(v * v, axis=-1, keepdims=True) + RMS_EPS)
    return v * r, r


def _rms_back(dn, n, r):
    return r * (dn - n * jnp.mean(dn * n, axis=-1, keepdims=True))


def _ln_parts(v):
    mu = jnp.mean(v, axis=-1, keepdims=True)
    rs = lax.rsqrt(jnp.mean(jnp.square(v - mu), axis=-1, keepdims=True) + LN_EPS)
    return (v - mu) * rs, rs


def _ln_back(dn, n, rs):
    return rs * (dn - jnp.mean(dn, axis=-1, keepdims=True) - n * jnp.mean(dn * n, axis=-1, keepdims=True))


GELU_C = 0.7978845608028654
GELU_A = 0.044715


def _gelu_parts(v):
    v2 = v * v
    th = jnp.tanh(GELU_C * (v + GELU_A * (v2 * v)))
    cdf = 0.5 * (1.0 + th)
    return v * cdf, cdf + (0.5 * GELU_C) * v * (1.0 - th * th) * (1.0 + (3.0 * GELU_A) * v2)


def _rms_vjp(v, gain):
    n, r = _rms_parts(v)
    return n * gain, lambda dy: (_rms_back(dy * gain, n, r), _colsum(dy * n))


def _mod_norm_vjp(v, gain, scale, shift):
    n, r = _rms_parts(v)

    def back(dy):
        q = _colsum(dy * n)
        return _rms_back(dy * (gain * (1.0 + scale)), n, r), q * (1.0 + scale), q * gain, _colsum(dy)

    return n * gain * (1.0 + scale) + shift, back


def _conv_branch_vjp(a1, ln_g, ln_b, out_gain):
    n1, rs1 = _ln_parts(a1)
    a2 = n1 * ln_g + ln_b
    s = jax.nn.sigmoid(a2)
    a3 = a2 * s
    n3, r3 = _rms_parts(a3)

    def back(dy):
        da2 = _rms_back(dy * out_gain, n3, r3) * (s + a3 * (1.0 - s))
        return _ln_back(da2 * ln_g, n1, rs1), _colsum(da2 * n1), _colsum(da2), _colsum(dy * n3)

    return n3 * out_gain, back


def _gate_branch_vjp(gu, sp, out_gain):
    ge, dge = _gelu_parts(gu)
    n, r = _rms_parts(ge * sp)

    def back(dy):
        dg = _rms_back(dy * out_gain, n, r)
        return dg * sp * dge, dg * ge, _colsum(dy * n)

    return n * out_gain, back


def _gv_norm_vjp(gv, ln_g, ln_b):
    ge, dge = _gelu_parts(gv)
    n, rs = _ln_parts(ge)
    return n * ln_g + ln_b, lambda dy: (_ln_back(dy * ln_g, n, rs) * dge, _colsum(dy * n), _colsum(dy))


def _head_pair_matmul(wp_ref, v):
    lane = lax.broadcasted_iota(jnp.int32, (CHUNK, CHUNK), 1)
    rows = []
    for n in range(v.shape[0] // CHUNK):
        cols = []
        for j in range(N_HEADS // 2):
            r = _nn(wp_ref[j], v[n * CHUNK:(n + 1) * CHUNK, j * CHUNK:(j + 1) * CHUNK])
            cols.append(jnp.where(lane < HEAD_DIM, r[:CHUNK], r[CHUNK:]))
        rows.append(jnp.concatenate(cols, axis=1))
    return jnp.concatenate(rows, axis=0)


def _tile_bias(bs, tokens):
    return jnp.concatenate([bs] * (tokens // CHUNK), axis=0)


FORWARD_LEAD = 8


def _fwd_mixer(x, vec, conv_w, wpair, bs_full, w_in_g, w_out_g, late_parts):
    seq = x.shape[0]
    n_tiles = seq // TILE
    t = TILE
    n_late = len(late_parts)
    forward_step = max(n_tiles - FORWARD_LEAD, 0)
    names = ["norm1_gain", "sc1", "sh1", "gt1", "conv_dw_b", "conv_ln_g", "conv_ln_b", "gm_ln_g", "gm_ln_b",
             "mix_out_gain"]
    vecs = [vec[k] for k in names]

    def body(x_ref, g1, sc1, sh1, gt1, cb, clg, clb, vg, vb, mg, cw, wp, bs, win_hbm, wout_hbm, *rest):
        late = rest[n_late:2 * n_late]
        z_ref, a1_ref, sp_ref, y_ref, o1_ref, x2_ref = rest[2 * n_late:2 * n_late + 6]
        win_v, wout_v, halo, bank, sem, send_sems, recv_sems = rest[2 * n_late + 6:]
        i = pl.program_id(0)
        mx, my, mc = _coords()
        shard = 2 * mx + my

        def half(w, which):
            h = late[w].shape[1] // 2
            return pl.ds(pl.multiple_of(which * h, 16), h)

        def chip_of(j):
            return 2 * _flip(mx, CHIP_FLIPS[j][0]) + _flip(my, CHIP_FLIPS[j][1])

        def ici_copy(w, j, slot):
            rows = late[w].at[slot, half(w, mc)]
            return pltpu.make_async_remote_copy(
                src_ref=rows, dst_ref=rows, send_sem=send_sems.at[w, j], recv_sem=recv_sems.at[w, j],
                device_id=(_flip(mx, CHIP_FLIPS[j][0]), _flip(my, CHIP_FLIPS[j][1]), mc), device_id_type=MESH)

        def d2d_copy(w, j, which):
            rows = late[w].at[chip_of(j), half(w, which)]
            return pltpu.make_async_remote_copy(
                src_ref=rows, dst_ref=rows, send_sem=send_sems.at[w, len(CHIP_FLIPS) + j],
                recv_sem=recv_sems.at[w, len(CHIP_FLIPS) + j], device_id=(mx, my, 1 - mc), device_id_type=MESH)

        pairs = [(w, j) for w in range(n_late) for j in range(len(CHIP_FLIPS))]

        @pl.when(i == 0)
        def _():
            for w, j in pairs:
                ici_copy(w, j, shard).start()
            cps = [pltpu.make_async_copy(win_hbm, win_v, sem.at[0]),
                   pltpu.make_async_copy(wout_hbm, wout_v, sem.at[1])]
            for cp in cps:
                cp.start()
            for cp in cps:
                cp.wait()
            halo[...] = jnp.zeros_like(halo)

        @pl.when(i == forward_step)
        def _():
            for w, j in pairs:
                ici_copy(w, j, chip_of(j)).wait_recv()
                d2d_copy(w, j, mc).start()

        xv = x_ref[...]
        h1b = _mod_norm(xv, g1[...], sc1[...], sh1[...]).astype(BF16)
        zs = [_nn(h1b, win_v[k]) for k in range(N_SHARD)]
        for k in range(N_SHARD):
            z_ref[:, k * D_HALF:(k + 1) * D_HALF] = zs[k]
        ca, cg, gu, gv = zs
        a0 = ca * jax.nn.sigmoid(cg)
        ext = jnp.concatenate([halo[...], a0], axis=0)
        halo[...] = a0[t - HALO:]
        bank[0] = ext
        for b in range(1, 8):
            bank[b] = pltpu.roll(ext, b, axis=0)
        a1 = jnp.zeros((t, D_HALF), F32) + cb[...]
        for s in range(CONV_K):
            q, b = divmod(s, 8)
            a1 = a1 + bank[b, pl.ds(HALO - 8 * q, t), :] * cw[pl.ds(CONV_K - 1 - s, 1), :]
        a1_ref[...] = a1
        mgv = mg[...]
        ya = _conv_branch(a1, clg[...], clb[...], mgv[:, :D_HALF])
        gvn = _gv_norm(gv, vg[...], vb[...]).astype(BF16)
        sp = _head_pair_matmul(wp, gvn) + _tile_bias(bs[...], t)
        sp_ref[...] = sp
        yg = _gate_branch(gu, sp, mgv[:, D_HALF:])
        yb = jnp.concatenate([ya, yg], axis=1).astype(BF16)
        y_ref[...] = yb
        o1 = _nn(yb, wout_v[...])
        o1_ref[...] = o1
        x2_ref[...] = xv + gt1[...] * o1

        @pl.when(i == n_tiles - 1)
        def _():
            for w, j in pairs:
                d2d_copy(w, j, 1 - mc).wait_recv()
            for w, j in pairs:
                ici_copy(w, j, shard).wait_send()
                d2d_copy(w, j, mc).wait_send()

    def row(width):
        return pl.BlockSpec((t, width), lambda i: (i, 0))

    out_shape = [jax.ShapeDtypeStruct((seq, 4 * D_HALF), F32), jax.ShapeDtypeStruct((seq, D_HALF), F32),
                 jax.ShapeDtypeStruct((seq, D_HALF), F32), jax.ShapeDtypeStruct((seq, D_MODEL), BF16),
                 jax.ShapeDtypeStruct((seq, D_MODEL), F32), jax.ShapeDtypeStruct((seq, D_MODEL), F32)]
    n_in = 1 + len(vecs) + 3 + 2
    sem_shape = (n_late, 2 * len(CHIP_FLIPS))
    outs = pl.pallas_call(
        body, grid=(n_tiles,), name="fwd_mixer",
        in_specs=[row(D_MODEL)] + [_full(v.shape) for v in vecs]
        + [_full(conv_w.shape), _full(wpair.shape), _full(bs_full.shape), ANY, ANY] + [ANY] * n_late,
        out_specs=[ANY] * n_late + [row(4 * D_HALF), row(D_HALF), row(D_HALF), row(D_MODEL), row(D_MODEL),
                                    row(D_MODEL)],
        out_shape=[jax.ShapeDtypeStruct(a.shape, a.dtype) for a in late_parts] + out_shape,
        input_output_aliases={n_in + w: w for w in range(n_late)},
        scratch_shapes=[pltpu.VMEM(w_in_g.shape, BF16), pltpu.VMEM(w_out_g.shape, BF16),
                        pltpu.VMEM((HALO, D_HALF), F32), pltpu.VMEM((8, t + HALO, D_HALF), F32),
                        pltpu.SemaphoreType.DMA((2,)), pltpu.SemaphoreType.DMA(sem_shape),
                        pltpu.SemaphoreType.DMA(sem_shape)],
        compiler_params=_params(),
    )(x, *vecs, conv_w, wpair, bs_full, w_in_g, w_out_g, *late_parts)
    return outs[n_late:], outs[:n_late]


def _interleave_matrices():
    row = jnp.arange(TILE)
    token_of_row = (row % 8) * (TILE // 8) + row // 8
    to_inter = (token_of_row[:, None] == row[None, :]).astype(BF16)
    return to_inter, jnp.transpose(to_inter)


def _ffn(x2, target, norm2_gain, sc2, sh2, ffn_w, ffn_b, gt2, final_gain, w_up_g, w_down_g, to_inter, to_natural):
    seq = x2.shape[0]
    n_tiles = seq // TILE
    t = TILE
    n_blk = D_FF // FFN_BLK
    inv_d = 1.0 / D_MODEL

    def body(x2_ref, x2h_ref, tgt_ref, g2, sc2_ref, sh2_ref, fw, fb, gt2_ref, fg, pm_ref, pmt_ref, wup_hbm, wd_hbm,
             du_ref, dx2_ref, dfw_ref, dfb_ref, dfg_ref, dgt2_ref, dg2_ref, dsc2_ref, dsh2_ref, loss_ref, dwd_hbm,
             dwd16_hbm, wup_v, wd_v, dwd_acc, carry, u_s, sil_s, vds_s, f_s, du_s, sem):
        i = pl.program_id(0)
        tile = n_tiles - 1 - i
        sublane = lax.broadcasted_iota(jnp.int32, (8, FFN_BLK), 0)

        @pl.when(i == 0)
        def _():
            cps = [pltpu.make_async_copy(wd_hbm, wd_v, sem.at[0])]
            cps += [pltpu.make_async_copy(wup_hbm.at[k], wup_v.at[:, pl.ds(k * UP_SHARD, UP_SHARD)], sem.at[3 + k])
                    for k in range(N_SHARD)]
            for cp in cps:
                cp.start()
            for cp in cps:
                cp.wait()
            dwd_acc[...] = jnp.zeros_like(dwd_acc)
            carry[...] = jnp.zeros_like(carry)
            dfw_ref[...] = jnp.zeros_like(dfw_ref)
            dfb_ref[...] = jnp.zeros_like(dfb_ref)
            dfg_ref[...] = jnp.zeros_like(dfg_ref)
            dgt2_ref[...] = jnp.zeros_like(dgt2_ref)
            dg2_ref[...] = jnp.zeros_like(dg2_ref)
            dsc2_ref[...] = jnp.zeros_like(dsc2_ref)
            dsh2_ref[...] = jnp.zeros_like(dsh2_ref)
            loss_ref[...] = jnp.zeros_like(loss_ref)

        def cols_of(j):
            return pl.ds(j * FFN_BLK, FFN_BLK), pl.ds(D_FF + j * FFN_BLK, FFN_BLK)

        def wrap_down(last, before):
            return jnp.where(sublane == 0, pltpu.roll(before, 1, axis=0), pltpu.roll(last, 1, axis=0))

        def wrap_up(first, after):
            return jnp.where(sublane == 7, pltpu.roll(after, 7, axis=0), pltpu.roll(first, 7, axis=0))

        x2v = x2_ref[...]
        h2, h2_vjp = _mod_norm_vjp(x2v, g2[...], sc2_ref[...], sh2_ref[...])
        h2b = h2.astype(BF16)
        h2_before = _mod_norm(x2h_ref[...], g2[...], sc2_ref[...], sh2_ref[...]).astype(BF16)
        lhs = jnp.concatenate([_nn(pm_ref[...], h2b).astype(BF16), h2_before], axis=0)

        def up(j):
            cv, cg = cols_of(j)
            return _nn(lhs, wup_v[:, cv]), _nn(lhs, wup_v[:, cg])

        def conv(both, cols):
            cur = both[:t]
            u_s[:, cols] = cur.astype(BF16)
            before = jnp.where(tile > 0, both[t:], 0.0)
            w1 = wrap_down(cur[t - 8:], before)
            w2 = wrap_down(cur[t - 16:t - 8], pltpu.roll(before, 1, axis=0))
            back1 = jnp.concatenate([w1, cur[:t - 8]], axis=0)
            back2 = jnp.concatenate([w2, w1, cur[:t - 16]], axis=0)
            return (fb[:, cols] + cur * fw[pl.ds(2, 1), cols] + back1 * fw[pl.ds(1, 1), cols]
                    + back2 * fw[pl.ds(0, 1), cols])

        pm_t = pmt_ref[...]

        def to_natural_f32(a):
            hi = a.astype(BF16)
            rest = a - hi.astype(F32)
            mid = rest.astype(BF16)
            low = (rest - mid.astype(F32)).astype(BF16)
            return _nn(jnp.concatenate([pm_t, pm_t, pm_t], axis=1), jnp.concatenate([hi, mid, low], axis=0))

        o2 = jnp.zeros((t, D_MODEL), F32)
        ahead_uv = up(0)
        for j in range(n_blk):
            cv, cg = cols_of(j)
            both_v, both_g = ahead_uv
            if j + 1 < n_blk:
                ahead_uv = up(j + 1)
            val, gate = conv(both_v, cv), conv(both_g, cg)
            sig = jax.nn.sigmoid(gate)
            sil = gate * sig
            fb16 = (sil * val).astype(BF16)
            sil_s[:, cv] = sil
            vds_s[:, cv] = val * (sig + sil * (1.0 - sig))
            f_s[:, cv] = fb16
            o2 = o2 + _nn(fb16, wd_v[pl.ds(j * FFN_BLK, FFN_BLK), :])
        o2 = to_natural_f32(o2)

        gt2v = gt2_ref[...]
        x3 = x2v + gt2v * o2
        out, out_vjp = _rms_vjp(x3, fg[...])
        diff = out - tgt_ref[...]
        loss_ref[...] += jnp.zeros_like(loss_ref) + 0.5 * inv_d * jnp.sum(diff * diff)
        dx3, dfg = out_vjp(diff * inv_d)
        dfg_ref[...] += dfg
        dgt2_ref[...] += _colsum(dx3 * o2)
        do2b = _nn(pm_ref[...], (gt2v * dx3).astype(BF16)).astype(BF16)

        for j in range(n_blk):
            cv, cg = cols_of(j)
            rows = pl.ds(j * FFN_BLK, FFN_BLK)
            df = _nt(do2b, wd_v[rows, :])
            dwd_acc[rows, :] += _tn(f_s[:, cv], do2b)
            for dd, cols in ((df * sil_s[:, cv], cv), (df * vds_s[:, cv], cg)):
                dfb_ref[:, cols] += _colsum(dd)
                nxt = carry[:, cols]
                w1 = wrap_up(dd[:8], nxt[:8])
                w2 = wrap_up(dd[8:16], nxt[8:])
                ahead = (dd, jnp.concatenate([dd[8:], w1], axis=0), jnp.concatenate([dd[16:], w1, w2], axis=0))
                carry[:, cols] = dd[:16]
                uv = u_s[:, cols].astype(F32)
                du = jnp.zeros((t, FFN_BLK), F32)
                for s in range(FFN_K):
                    du = du + ahead[s] * fw[pl.ds(FFN_K - 1 - s, 1), cols]
                    dfw_ref[pl.ds(FFN_K - 1 - s, 1), cols] += _colsum(ahead[s] * uv)
                du_s[:, cols] = du.astype(BF16)
        du16 = _nn(pm_t, du_s[...]).astype(BF16)
        du_ref[...] = du16
        dx2, dg2, dsc2, dsh2 = h2_vjp(_nt(du16, wup_v[...]))
        dx2_ref[...] = dx3 + dx2
        dg2_ref[...] += dg2
        dsc2_ref[...] += dsc2
        dsh2_ref[...] += dsh2

        @pl.when(i == n_tiles - 1)
        def _():
            cp = pltpu.make_async_copy(dwd_acc, dwd_hbm, sem.at[1])
            cp.start()
            wd_v[...] = dwd_acc[...].astype(BF16)
            cp16 = pltpu.make_async_copy(wd_v, dwd16_hbm, sem.at[2])
            cp16.start()
            cp.wait()
            cp16.wait()

    def rev(width):
        return pl.BlockSpec((t, width), lambda i: (n_tiles - 1 - i, 0))

    assert FFN_K == 3
    halo_spec = pl.BlockSpec((8, D_MODEL), lambda i: (jnp.maximum((n_tiles - 1 - i) * (t // 8) - 1, 0), 0))
    vec_spec = _full((1, D_MODEL))
    out_shape = [jax.ShapeDtypeStruct((seq, 2 * D_FF), BF16), jax.ShapeDtypeStruct((seq, D_MODEL), F32),
                 jax.ShapeDtypeStruct((FFN_K, 2 * D_FF), F32), jax.ShapeDtypeStruct((1, 2 * D_FF), F32),
                 jax.ShapeDtypeStruct((1, D_MODEL), F32), jax.ShapeDtypeStruct((1, D_MODEL), F32),
                 jax.ShapeDtypeStruct((1, D_MODEL), F32), jax.ShapeDtypeStruct((1, D_MODEL), F32),
                 jax.ShapeDtypeStruct((1, D_MODEL), F32),
                 jax.ShapeDtypeStruct((1, 128), F32), jax.ShapeDtypeStruct((D_FF, D_MODEL), F32),
                 jax.ShapeDtypeStruct((D_FF, D_MODEL), BF16)]
    return pl.pallas_call(
        body, grid=(n_tiles,), name="ffn",
        in_specs=[rev(D_MODEL), halo_spec, rev(D_MODEL), vec_spec, vec_spec, vec_spec, _full(ffn_w.shape),
                  _full(ffn_b.shape), _full(gt2.shape), _full(final_gain.shape), _full(to_inter.shape),
                  _full(to_natural.shape), ANY, ANY],
        out_specs=[rev(2 * D_FF), rev(D_MODEL), _full((FFN_K, 2 * D_FF)), _full((1, 2 * D_FF)), vec_spec, vec_spec,
                   vec_spec, vec_spec, vec_spec, _full((1, 128)), ANY, ANY],
        out_shape=out_shape,
        scratch_shapes=[pltpu.VMEM((D_MODEL, 2 * D_FF), BF16), pltpu.VMEM((D_FF, D_MODEL), BF16),
                        pltpu.VMEM((D_FF, D_MODEL), F32), pltpu.VMEM((FFN_HALO, 2 * D_FF), F32),
                        pltpu.VMEM((t, 2 * D_FF), BF16), pltpu.VMEM((t, D_FF), F32), pltpu.VMEM((t, D_FF), F32),
                        pltpu.VMEM((t, D_FF), BF16), pltpu.VMEM((t, 2 * D_FF), BF16),
                        pltpu.SemaphoreType.DMA((3 + N_SHARD,))],
        compiler_params=pltpu.CompilerParams(dimension_semantics=("arbitrary",), vmem_limit_bytes=FFN_VMEM_LIMIT_BYTES),
    )(x2, x2, target, norm2_gain, sc2, sh2, ffn_w, ffn_b, gt2, final_gain, to_inter, to_natural, w_up_g, w_down_g)


def _scatter_copies(src16, land, send_sems, recv_sems):
    x, y, c = _coords()
    h = src16.shape[1] // 2
    copies = []
    for f, flip in enumerate(PEER_FLIPS):
        tx, ty, tc = _flip(x, flip[0]), _flip(y, flip[1]), _flip(c, flip[2])
        copies.append(pltpu.make_async_remote_copy(
            src_ref=src16.at[2 * tx + ty, pl.ds(pl.multiple_of(tc * h, 16), h)], dst_ref=land.at[f],
            send_sem=send_sems.at[f], recv_sem=recv_sems.at[f], device_id=(tx, ty, tc), device_id_type=MESH))
    return copies


def _land_shape(src16):
    return jax.ShapeDtypeStruct((len(PEER_FLIPS), src16.shape[1] // 2, src16.shape[2]), BF16)


UP_TILE = 512


def _bwd_up(du, x2, norm2_gain, sc2, sh2, dwd16):
    seq = x2.shape[0]
    t = UP_TILE if seq % UP_TILE == 0 else TILE
    n_tiles = seq // t
    acc_shape = (N_SHARD, D_MODEL, UP_SHARD)

    def body(du_ref, x2_ref, g2, sc2_ref, sh2_ref, dwd16_hbm, dwup_hbm, dwup16_hbm, land_hbm,
             stage16, dwup_acc, sem, send_sems, recv_sems):
        i = pl.program_id(0)

        @pl.when(i == 0)
        def _():
            for cp in _scatter_copies(dwd16_hbm, land_hbm, send_sems, recv_sems):
                cp.start()
            dwup_acc[...] = jnp.zeros_like(dwup_acc)

        h2b = _mod_norm(x2_ref[...], g2[...], sc2_ref[...], sh2_ref[...]).astype(BF16)
        for k in range(N_SHARD):
            dwup_acc[k] += _tn(h2b, du_ref[:, k * UP_SHARD:(k + 1) * UP_SHARD])

        @pl.when(i == n_tiles - 1)
        def _():
            cp = pltpu.make_async_copy(dwup_acc, dwup_hbm, sem.at[0])
            cp.start()
            for k in range(N_SHARD):
                stage16[k] = dwup_acc[k].astype(BF16)
            cp16 = pltpu.make_async_copy(stage16, dwup16_hbm, sem.at[1])
            cp16.start()
            cp.wait()
            cp16.wait()
            for rc in _scatter_copies(dwd16_hbm, land_hbm, send_sems, recv_sems):
                rc.wait()

    def row(width):
        return pl.BlockSpec((t, width), lambda i: (i, 0))

    n_peer = len(PEER_FLIPS)
    return pl.pallas_call(
        body, grid=(n_tiles,), name="bwd_up",
        in_specs=[row(2 * D_FF), row(D_MODEL), _full((1, D_MODEL)), _full((1, D_MODEL)), _full((1, D_MODEL)), ANY],
        out_specs=[ANY, ANY, ANY],
        out_shape=[jax.ShapeDtypeStruct(acc_shape, F32), jax.ShapeDtypeStruct(acc_shape, BF16), _land_shape(dwd16)],
        scratch_shapes=[pltpu.VMEM(acc_shape, BF16), pltpu.VMEM(acc_shape, F32), pltpu.SemaphoreType.DMA((2,)),
                        pltpu.SemaphoreType.DMA((n_peer,)), pltpu.SemaphoreType.DMA((n_peer,))],
        compiler_params=_params(),
    )(du, x2, norm2_gain, sc2, sh2, dwd16)


def _bwd_mixer(dx2, x, z, a1, sp, yb, o1, vec, conv_w, wpair, wpair_t, causal_mask, w_in_g, w_out_g, dwup16):
    seq = x.shape[0]
    n_tiles = seq // TILE
    t = TILE
    names = ["norm1_gain", "sc1", "sh1", "gt1", "conv_ln_g", "conv_ln_b", "gm_ln_g", "gm_ln_b", "mix_out_gain"]
    vecs = [vec[k] for k in names]

    def body(dx2_ref, x_ref, z_ref, a1_ref, sp_ref, y_ref, o1_ref, g1, sc1, sh1, gt1, clg, clb, vg, vb, mg,
             cw, wp, wpt, mask_ref, win_hbm, wout_hbm, dwup16_hbm,
             gx_ref, dg1_ref, dsc1_ref, dsh1_ref, dgt1_ref, dcw_ref, dcb_ref, dclg_ref, dclb_ref, dvg_ref, dvb_ref,
             dmg_ref, dws_ref, dbs_ref, dwin_hbm, dwout_hbm, land_hbm, dwin16_hbm, dwout16_hbm,
             win_v, wout_v, dwin_acc, dwout_acc, carry, bank, dbs_acc, lwin, lwout, sem, send_sems, recv_sems,
             pair_send, pair_recv):
        i = pl.program_id(0)
        small = [dg1_ref, dsc1_ref, dsh1_ref, dgt1_ref, dcw_ref, dcb_ref, dclg_ref, dclb_ref, dvg_ref, dvb_ref,
                 dmg_ref, dws_ref, dbs_acc]

        @pl.when(i == 0)
        def _():
            for cp in _scatter_copies(dwup16_hbm, land_hbm, send_sems, recv_sems):
                cp.start()
            cps = [pltpu.make_async_copy(win_hbm, win_v, sem.at[0]),
                   pltpu.make_async_copy(wout_hbm, wout_v, sem.at[1])]
            for cp in cps:
                cp.start()
            for cp in cps:
                cp.wait()
            dwin_acc[...] = jnp.zeros_like(dwin_acc)
            dwout_acc[...] = jnp.zeros_like(dwout_acc)
            carry[...] = jnp.zeros_like(carry)
            for ref in small:
                ref[...] = jnp.zeros_like(ref)

        dx2v = dx2_ref[...]
        gt1v = gt1[...]
        dgt1_ref[...] += _colsum(dx2v * o1_ref[...])
        do1b = (gt1v * dx2v).astype(BF16)
        dy = _nt(do1b, wout_v[...])
        dwout_acc[...] += _tn(y_ref[...], do1b)

        mgv = mg[...]
        _, conv_vjp = _conv_branch_vjp(a1_ref[...], clg[...], clb[...], mgv[:, :D_HALF])
        da1, dclg, dclb, dmg_a = conv_vjp(dy[:, :D_HALF])
        dclg_ref[...] += dclg
        dclb_ref[...] += dclb
        gu = z_ref[:, 2 * D_HALF:3 * D_HALF]
        gv = z_ref[:, 3 * D_HALF:]
        spv = sp_ref[...]
        _, gate_vjp = _gate_branch_vjp(gu, spv, mgv[:, D_HALF:])
        dgu, dsp, dmg_g = gate_vjp(dy[:, D_HALF:])
        dmg_ref[...] += jnp.concatenate([dmg_a, dmg_g], axis=1)
        gvn, gv_vjp = _gv_norm_vjp(gv, vg[...], vb[...])
        gvnb = gvn.astype(BF16)
        dspb = dsp.astype(BF16)
        dgvn = _head_pair_matmul(wpt, dspb)
        dgv, dvg, dvb = gv_vjp(dgvn)
        dvg_ref[...] += dvg
        dvb_ref[...] += dvb
        lane = lax.broadcasted_iota(jnp.int32, (CHUNK, CHUNK), 1)
        dbs = jnp.zeros((CHUNK, D_HALF), F32)
        for n in range(t // CHUNK):
            rows = slice(n * CHUNK, (n + 1) * CHUNK)
            dbs = dbs + dsp[rows, :]
            for j in range(N_HEADS // 2):
                cols = slice(j * CHUNK, (j + 1) * CHUNK)
                blk = dspb[rows, cols]
                zero = jnp.zeros_like(blk)
                vblk = gvnb[rows, cols]
                dws_ref[2 * j] += _nt(jnp.where(lane < HEAD_DIM, blk, zero), vblk)
                dws_ref[2 * j + 1] += _nt(jnp.where(lane < HEAD_DIM, zero, blk), vblk)
        dbs_acc[...] += dbs

        h1, h1_vjp = _mod_norm_vjp(x_ref[...], g1[...], sc1[...], sh1[...])
        h1b = h1.astype(BF16)
        dh1 = jnp.zeros((t, D_MODEL), F32)
        for k, dzk in ((2, dgu), (3, dgv)):
            dzb = dzk.astype(BF16)
            dh1 = dh1 + _nt(dzb, win_v[k])
            dwin_acc[k] += _tn(h1b, dzb)

        ca = z_ref[:, :D_HALF]
        cg = z_ref[:, D_HALF:2 * D_HALF]
        sig = jax.nn.sigmoid(cg)
        a0 = ca * sig
        ext = jnp.concatenate([da1, carry[...]], axis=0)
        carry[...] = da1[:HALO]
        bank[0] = ext
        for b in range(1, 8):
            bank[b] = pltpu.roll(ext, t + HALO - b, axis=0)
        dcb_ref[...] += _colsum(da1)
        da0 = jnp.zeros((t, D_HALF), F32)
        for s in range(CONV_K):
            q, b = divmod(s, 8)
            shifted = bank[b, pl.ds(8 * q, t), :]
            da0 = da0 + shifted * cw[pl.ds(CONV_K - 1 - s, 1), :]
            dcw_ref[pl.ds(CONV_K - 1 - s, 1), :] += _colsum(shifted * a0)
        dca = da0 * sig
        dcg = da0 * ca * sig * (1.0 - sig)

        for k, dzk in ((0, dca), (1, dcg)):
            dzb = dzk.astype(BF16)
            dh1 = dh1 + _nt(dzb, win_v[k])
            dwin_acc[k] += _tn(h1b, dzb)
        dx, dg1, dsc1, dsh1 = h1_vjp(dh1)
        gx_ref[...] = dx2v + dx
        dg1_ref[...] += dg1
        dsc1_ref[...] += dsc1
        dsh1_ref[...] += dsh1

        @pl.when(i == n_tiles - 1)
        def _():
            for h in range(N_HEADS):
                dws_ref[h] = dws_ref[h] * mask_ref[...]
            head_of_lane = lax.broadcasted_iota(jnp.int32, (N_HEADS, D_HALF), 1) // HEAD_DIM
            pick = (head_of_lane == lax.broadcasted_iota(jnp.int32, (N_HEADS, D_HALF), 0)).astype(F32)
            dbs_ref[...] = lax.dot_general(pick, dbs_acc[...], NT_DIMS, precision=lax.Precision.HIGHEST,
                                           preferred_element_type=F32)
            for k in range(N_SHARD):
                win_v[k] = dwin_acc[k].astype(BF16)
            wout_v[...] = dwout_acc[...].astype(BF16)
            mx, my, mc = _coords()
            h_in, h_out = dwin_acc.shape[1] // 2, dwout_acc.shape[0] // (2 * N_SHARD)

            def in_rows(ref, k, which):
                return ref.at[k, pl.ds(pl.multiple_of(which * h_in, 16), h_in), :]

            def out_rows(ref, k, which):
                return ref.at[pl.ds(pl.multiple_of((2 * k + which) * h_out, 16), h_out), :]

            pairs = ((win_v, dwin_acc, lwin, in_rows, dwin_hbm, dwin16_hbm),
                     (wout_v, dwout_acc, lwout, out_rows, dwout_hbm, dwout16_hbm))
            swaps = [pltpu.make_async_remote_copy(
                src_ref=rows_of(v16, k, 1 - mc), dst_ref=land.at[k], send_sem=pair_send.at[w, k],
                recv_sem=pair_recv.at[w, k], device_id=(mx, my, 1 - mc), device_id_type=MESH)
                for w, (v16, _, land, rows_of, _, _) in enumerate(pairs) for k in range(N_SHARD)]
            for cp in swaps:
                cp.start()
            for cp in swaps:
                cp.wait()
            outs = []
            for w, (v16, acc, land, rows_of, half_hbm, half16_hbm) in enumerate(pairs):
                for k in range(N_SHARD):
                    total = rows_of(acc, k, mc)[...] + land[k].astype(F32)
                    rows_of(acc, k, 0)[...] = total
                    rows_of(v16, k, 0)[...] = total.astype(BF16)
                    outs.append(pltpu.make_async_copy(rows_of(acc, k, 0), half_hbm.at[k], sem.at[2 + 8 * w + k]))
                    outs.append(pltpu.make_async_copy(rows_of(v16, k, 0), half16_hbm.at[k], sem.at[6 + 8 * w + k]))
            for cp in outs:
                cp.start()
            for cp in outs:
                cp.wait()
            for rc in _scatter_copies(dwup16_hbm, land_hbm, send_sems, recv_sems):
                rc.wait()

    def rev(width):
        return pl.BlockSpec((t, width), lambda i: (n_tiles - 1 - i, 0))

    v1024 = jax.ShapeDtypeStruct((1, D_MODEL), F32)
    v512 = jax.ShapeDtypeStruct((1, D_HALF), F32)
    small_shapes = [v1024, v1024, v1024, v1024, jax.ShapeDtypeStruct((CONV_K, D_HALF), F32), v512, v512, v512, v512,
                    v512, v1024, jax.ShapeDtypeStruct((N_HEADS, CHUNK, CHUNK), F32),
                    jax.ShapeDtypeStruct((N_HEADS, CHUNK), F32)]
    n_peer = len(PEER_FLIPS)
    half_in = (N_SHARD, w_in_g.shape[1] // 2, w_in_g.shape[2])
    half_out = (N_SHARD, w_out_g.shape[0] // (2 * N_SHARD), w_out_g.shape[1])
    return pl.pallas_call(
        body, grid=(n_tiles,), name="bwd_mixer",
        in_specs=[rev(D_MODEL), rev(D_MODEL), rev(4 * D_HALF), rev(D_HALF), rev(D_HALF), rev(D_MODEL),
                  rev(D_MODEL)] + [_full(v.shape) for v in vecs]
        + [_full(conv_w.shape), _full(wpair.shape), _full(wpair_t.shape), _full(causal_mask.shape), ANY, ANY, ANY],
        out_specs=[rev(D_MODEL)] + [_full(s.shape) for s in small_shapes] + [ANY] * 5,
        out_shape=[jax.ShapeDtypeStruct((seq, D_MODEL), F32)] + small_shapes
        + [jax.ShapeDtypeStruct(half_in, F32), jax.ShapeDtypeStruct(half_out, F32), _land_shape(dwup16),
           jax.ShapeDtypeStruct(half_in, BF16), jax.ShapeDtypeStruct(half_out, BF16)],
        scratch_shapes=[pltpu.VMEM(w_in_g.shape, BF16), pltpu.VMEM(w_out_g.shape, BF16),
                        pltpu.VMEM(w_in_g.shape, F32), pltpu.VMEM(w_out_g.shape, F32),
                        pltpu.VMEM((HALO, D_HALF), F32), pltpu.VMEM((8, t + HALO, D_HALF), F32),
                        pltpu.VMEM((CHUNK, D_HALF), F32), pltpu.VMEM(half_in, BF16), pltpu.VMEM(half_out, BF16),
                        pltpu.SemaphoreType.DMA((2 + 4 * N_SHARD,)),
                        pltpu.SemaphoreType.DMA((n_peer,)), pltpu.SemaphoreType.DMA((n_peer,)),
                        pltpu.SemaphoreType.DMA((2, N_SHARD)), pltpu.SemaphoreType.DMA((2, N_SHARD))],
        compiler_params=_params(),
    )(dx2, x, z, a1, sp, yb, o1, *vecs, conv_w, wpair, wpair_t, causal_mask, w_in_g, w_out_g, dwup16)


def _gmlp_operands(gm_ws, gm_bs):
    mask = jnp.tril(jnp.ones((CHUNK, CHUNK), F32))
    ws = gm_ws * mask[None]
    wpair = ws.reshape(N_HEADS // 2, 2 * CHUNK, CHUNK).astype(BF16)
    wpair_t = jnp.swapaxes(ws, 1, 2).reshape(N_HEADS // 2, 2 * CHUNK, CHUNK).astype(BF16)
    bs_full = jnp.repeat(jnp.transpose(gm_bs), HEAD_DIM, axis=1)
    return wpair, wpair_t, bs_full, mask


def _local_step(x, target, mod, p, w_in_g, w_out_g, w_up_part, w_down_part):
    sh1, sc1, gt1, sh2, sc2, gt2 = [mod[:, k * D_MODEL:(k + 1) * D_MODEL] for k in range(6)]
    vec = dict(p, sh1=sh1, sc1=sc1, gt1=gt1, sh2=sh2, sc2=sc2, gt2=gt2)
    wpair, wpair_t, bs_full, mask = _gmlp_operands(p["gm_ws"], p["gm_bs"])

    (z, a1, sp, yb, o1, x2), (w_up_g, w_down_g) = _fwd_mixer(
        x, vec, p["conv_dw_w"], wpair, bs_full, w_in_g, w_out_g, [w_up_part, w_down_part])
    w_down_g = w_down_g.reshape(D_FF, D_MODEL)
    to_inter, to_natural = _interleave_matrices()
    du, dx2, d_ffn_w, d_ffn_b, d_fg, d_gt2, d_g2, d_sc2, d_sh2, loss, d_wd, d_wd16 = _ffn(
        x2, target, p["norm2_gain"], sc2, sh2, p["ffn_dw_w"], p["ffn_dw_b"], gt2, p["final_gain"], w_up_g, w_down_g,
        to_inter, to_natural)
    by_shard = (N_SHARD, -1, D_MODEL)
    d_wup, d_wup16, land_wd = _bwd_up(du, x2, p["norm2_gain"], sc2, sh2, d_wd16.reshape(by_shard))
    (gx, d_g1, d_sc1, d_sh1, d_gt1, d_cw, d_cb, d_clg, d_clb, d_vg, d_vb, d_mg, d_ws, d_bs, d_win, d_wout, land_wup,
     d_win16, d_wout16) = _bwd_mixer(dx2, x, z, a1, sp, yb, o1, vec, p["conv_dw_w"], wpair, wpair_t, mask, w_in_g,
                                     w_out_g, d_wup16)
    d_mod = jnp.concatenate([d_sh1, d_sc1, d_gt1, d_sh2, d_sc2, d_gt2], axis=1)
    grads = dict(norm1_gain=d_g1, conv_dw_w=d_cw, conv_dw_b=d_cb, conv_ln_g=d_clg, conv_ln_b=d_clb, gm_ln_g=d_vg,
                 gm_ln_b=d_vb, gm_ws=d_ws, gm_bs=d_bs, mix_out_gain=d_mg, norm2_gain=d_g2, ffn_dw_w=d_ffn_w,
                 ffn_dw_b=d_ffn_b, final_gain=d_fg, w_in=d_win, w_out=d_wout, w_up=d_wup, w_down=d_wd.reshape(by_shard))
    in_flight = dict(w_in16=d_win16, w_out16=d_wout16, land_w_up=land_wup, land_w_down=land_wd)
    return gx, grads, d_mod, loss, in_flight


MESH = pl.DeviceIdType.MESH
VMEM_SPEC = pl.BlockSpec(memory_space=pltpu.VMEM)
PEER_FLIPS = [(a, b, d) for a in (0, 1) for b in (0, 1) for d in (0, 1)][1:]
CHIP_FLIPS = [(1, 0), (0, 1), (1, 1)]


def _coords():
    return lax.axis_index("x"), lax.axis_index("y"), lax.axis_index("c")


def _flip(v, bit):
    return 1 - v if bit else v


def _rows8(block):
    return pl.ds(pl.multiple_of(8 * block, 8), 8)


def _ada_steps(c_ref, w_ref, b_ref, call_ref, mod_ref, cpad, modall, send_sems, recv_sems):
    x, y, c = _coords()
    me = 4 * x + 2 * y + c
    cpad[...] = jnp.zeros_like(cpad)
    cpad[pl.ds(0, 1), :] = c_ref[...]

    def gather_copy(j, flip):
        peer = (_flip(x, flip[0]), _flip(y, flip[1]), _flip(c, flip[2]))
        return pltpu.make_async_remote_copy(
            src_ref=cpad, dst_ref=call_ref.at[_rows8(me)], send_sem=send_sems.at[j], recv_sem=recv_sems.at[j],
            device_id=peer, device_id_type=MESH)

    def piece_copy(j, flip):
        tx, ty = _flip(x, flip[0]), _flip(y, flip[1])
        return pltpu.make_async_remote_copy(
            src_ref=modall.at[_rows8(4 * tx + 2 * ty + c)], dst_ref=mod_ref.at[_rows8(2 * x + y)],
            send_sem=send_sems.at[len(PEER_FLIPS) + j], recv_sem=recv_sems.at[len(PEER_FLIPS) + j],
            device_id=(tx, ty, c), device_id_type=MESH)

    copies = [gather_copy(j, f) for j, f in enumerate(PEER_FLIPS)]
    for cp in copies:
        cp.start()
    call_ref[_rows8(me), :] = cpad[...]

    def middle():
        for cp in copies:
            cp.wait_recv()
        for cp in copies:
            cp.wait_send()
        cv = call_ref[...]
        c_act = (cv * jax.nn.sigmoid(cv)).astype(BF16)
        modall[...] = _nn(c_act, w_ref[...].astype(BF16)) + b_ref[...]
        for j, f in enumerate(CHIP_FLIPS):
            piece_copy(j, f).start()
        mod_ref[_rows8(2 * x + y), :] = modall[_rows8(me), :]

    def finish():
        for j, f in enumerate(CHIP_FLIPS):
            piece_copy(j, f).wait_recv()
        for j, f in enumerate(CHIP_FLIPS):
            piece_copy(j, f).wait_send()

    return middle, finish


def _gather_weights(shards, filters, n_now, c_row, w_ada_sh, b_ada_sh):
    n = len(shards)
    nf = len(filters)
    ada_cols = w_ada_sh.shape[1]

    def body(*refs):
        ins, f_ins, ada_ins = refs[:n], refs[n:n + nf], refs[n + nf:n + nf + 3]
        refs = refs[n + nf + 3:]
        outs, f_outs, ada_outs = refs[:n], refs[n:n + nf], refs[n + nf:n + nf + 2]
        refs = refs[n + nf + 2:]
        stage = refs[:n]
        send_sems, recv_sems, local_sems, f_send_sems, f_recv_sems, cpad, modall, ada_send, ada_recv = refs[n:]
        ada_middle, ada_finish = _ada_steps(*ada_ins, *ada_outs, cpad, modall, ada_send, ada_recv)
        x, y, c = _coords()
        k = 2 * x + y
        sibling = (x, y, 1 - c)

        def filter_copy(w, j, slot):
            tx, ty = _flip(x, CHIP_FLIPS[j][0]), _flip(y, CHIP_FLIPS[j][1])
            return pltpu.make_async_remote_copy(
                src_ref=f_ins[w], dst_ref=f_outs[w].at[slot], send_sem=f_send_sems.at[w, j],
                recv_sem=f_recv_sems.at[w, j], device_id=(tx, ty, c), device_id_type=MESH)

        def half(w, which):
            h = shards[w].shape[0] // 2
            return pl.ds(pl.multiple_of(which * h, 16), h)

        def ici_copy(w, j, src, slot):
            tx, ty = _flip(x, CHIP_FLIPS[j][0]), _flip(y, CHIP_FLIPS[j][1])
            return pltpu.make_async_remote_copy(
                src_ref=src, dst_ref=outs[w].at[slot, half(w, c)], send_sem=send_sems.at[w, j],
                recv_sem=recv_sems.at[w, j], device_id=(tx, ty, c), device_id_type=MESH)

        def d2d_copy(w, j, slot, which):
            rows = outs[w].at[slot, half(w, which)]
            return pltpu.make_async_remote_copy(
                src_ref=rows, dst_ref=rows, send_sem=send_sems.at[w, len(CHIP_FLIPS) + j],
                recv_sem=recv_sems.at[w, len(CHIP_FLIPS) + j], device_id=sibling, device_id_type=MESH)

        def chip_of(j):
            return 2 * _flip(x, CHIP_FLIPS[j][0]) + _flip(y, CHIP_FLIPS[j][1])

        local, first, passed = [], [], []
        for w in range(nf):
            local.append(pltpu.make_async_copy(f_ins[w], f_outs[w].at[k], local_sems.at[n + w]))
            local[-1].start()
            for j in range(len(CHIP_FLIPS)):
                first.append(filter_copy(w, j, k))
                first[-1].start()
        for w in range(n):
            stage[w][...] = ins[w][...].astype(BF16)
            local.append(pltpu.make_async_copy(stage[w], outs[w].at[k], local_sems.at[w]))
            local[-1].start()
            if w < n_now:
                for j in range(len(CHIP_FLIPS)):
                    first.append(ici_copy(w, j, stage[w].at[half(w, c)], k))
                    first[-1].start()
        ada_middle()
        for w in range(nf):
            for j in range(len(CHIP_FLIPS)):
                filter_copy(w, j, chip_of(j)).wait_recv()
        for w in range(n_now):
            for j in range(len(CHIP_FLIPS)):
                ici_copy(w, j, stage[w].at[half(w, c)], chip_of(j)).wait_recv()
                passed.append(d2d_copy(w, j, chip_of(j), c))
                passed[-1].start()
        for w in range(n_now):
            for j in range(len(CHIP_FLIPS)):
                d2d_copy(w, j, chip_of(j), 1 - c).wait_recv()
        for cp in first + passed:
            cp.wait_send()
        for cp in local:
            cp.wait()
        ada_finish()

    sem_shape = (n_now, 2 * len(CHIP_FLIPS))
    f_sem_shape = (nf, len(CHIP_FLIPS))
    n_ada_sem = len(PEER_FLIPS) + len(CHIP_FLIPS)
    outs = pl.pallas_call(
        body, name="gather_weights",
        in_specs=[VMEM_SPEC] * (n + nf + 3), out_specs=[ANY] * (n + nf) + [VMEM_SPEC, VMEM_SPEC],
        out_shape=[jax.ShapeDtypeStruct((N_SHARD,) + s.shape, BF16) for s in shards]
        + [jax.ShapeDtypeStruct((N_SHARD,) + s.shape, F32) for s in filters]
        + [jax.ShapeDtypeStruct((8 * N_DEV, D_MODEL), F32), jax.ShapeDtypeStruct((8 * N_SHARD, ada_cols), F32)],
        scratch_shapes=[pltpu.VMEM(s.shape, BF16) for s in shards]
        + [pltpu.SemaphoreType.DMA(sem_shape), pltpu.SemaphoreType.DMA(sem_shape), pltpu.SemaphoreType.DMA((n + nf,)),
           pltpu.SemaphoreType.DMA(f_sem_shape), pltpu.SemaphoreType.DMA(f_sem_shape),
           pltpu.VMEM((8, D_MODEL), F32), pltpu.VMEM((8 * N_DEV, ada_cols), F32),
           pltpu.SemaphoreType.DMA((n_ada_sem,)), pltpu.SemaphoreType.DMA((n_ada_sem,))],
        compiler_params=pltpu.CompilerParams(vmem_limit_bytes=VMEM_LIMIT_BYTES),
    )(*shards, *filters, c_row, w_ada_sh, b_ada_sh)
    return outs[:n], outs[n:n + nf], outs[n + nf], outs[n + nf + 1]


def _final_comm(srcs16, small):
    n = len(srcs16)
    rows = small.shape[0]
    half = rows // 2

    def body(*refs):
        srcs, small_ref = refs[:n], refs[n]
        lands, small_out = refs[n + 1:2 * n + 1], refs[2 * n + 1]
        chip_sum, got_c, got_x, got_y, part_x, send_sems, recv_sems, small_send_sems, small_recv_sems = refs[2 * n + 2:]
        x, y, c = _coords()
        sibling = (x, y, 1 - c)
        mine = pl.ds(pl.multiple_of(c * half, 8), half)
        copies = []
        for w in range(n):
            for j, flip in enumerate(CHIP_FLIPS):
                tx, ty = _flip(x, flip[0]), _flip(y, flip[1])
                copies.append(pltpu.make_async_remote_copy(
                    src_ref=srcs[w].at[2 * tx + ty], dst_ref=lands[w].at[j], send_sem=send_sems.at[w, j],
                    recv_sem=recv_sems.at[w, j], device_id=(tx, ty, c), device_id_type=MESH))
        for cp in copies:
            cp.start()

        def exchange(stage, src, dst, peer):
            rc = pltpu.make_async_remote_copy(
                src_ref=src, dst_ref=dst, send_sem=small_send_sems.at[stage], recv_sem=small_recv_sems.at[stage],
                device_id=peer, device_id_type=MESH)
            rc.start()
            rc.wait()

        exchange(0, small_ref, got_c, sibling)
        chip_sum[...] = small_ref[...] + got_c[...]
        exchange(1, chip_sum.at[mine], got_x, (1 - x, y, c))
        part_x[...] = chip_sum[mine, :] + got_x[...]
        exchange(2, part_x, got_y, (x, 1 - y, c))
        small_out[mine, :] = part_x[...] + got_y[...]
        exchange(3, small_out.at[mine], small_out.at[mine], sibling)
        for cp in copies:
            cp.wait()

    n_chip = len(CHIP_FLIPS)
    half_shape = (half, small.shape[1])
    outs = pl.pallas_call(
        body, name="final_comm",
        in_specs=[ANY] * n + [VMEM_SPEC], out_specs=[ANY] * n + [VMEM_SPEC],
        out_shape=[jax.ShapeDtypeStruct((n_chip,) + a.shape[1:], BF16) for a in srcs16]
        + [jax.ShapeDtypeStruct(small.shape, F32)],
        scratch_shapes=[pltpu.VMEM(small.shape, F32), pltpu.VMEM(small.shape, F32), pltpu.VMEM(half_shape, F32),
                        pltpu.VMEM(half_shape, F32), pltpu.VMEM(half_shape, F32),
                        pltpu.SemaphoreType.DMA((n, n_chip)), pltpu.SemaphoreType.DMA((n, n_chip)),
                        pltpu.SemaphoreType.DMA((4,)), pltpu.SemaphoreType.DMA((4,))],
        compiler_params=pltpu.CompilerParams(vmem_limit_bytes=VMEM_LIMIT_BYTES),
    )(*srcs16, small)
    return outs[:n], outs[n]


ADD_CHUNKS = 4


def _scatter_sum(pos, owns, lands):
    n = len(owns)

    def specs(own_shape, land_shape):
        peers, rows, cols = land_shape
        pick = 1 if own_shape[1] == 2 * rows else 0
        if cols % (128 * ADD_CHUNKS) == 0:
            blk = (rows, cols // ADD_CHUNKS)
            return (pl.BlockSpec((1,) + blk, lambda i, p: (2 * p[0] + p[1], pick * p[2], i)),
                    pl.BlockSpec((peers,) + blk, lambda i, p: (0, 0, i)),
                    pl.BlockSpec((1,) + blk, lambda i, p: (p[2], 0, i)))
        blk = (rows // ADD_CHUNKS, cols)
        return (pl.BlockSpec((1,) + blk, lambda i, p: (2 * p[0] + p[1], pick * p[2] * ADD_CHUNKS + i, 0)),
                pl.BlockSpec((peers,) + blk, lambda i, p: (0, i, 0)),
                pl.BlockSpec((1,) + blk, lambda i, p: (p[2], i, 0)))

    def body(pos_ref, *refs):
        for idx in range(n):
            own, land, out = refs[idx], refs[n + idx], refs[2 * n + idx]
            total = own[0]
            for f in range(land.shape[0]):
                total = total + land[f].astype(F32)
            out[0] = total

    all_specs = [specs(o.shape, l.shape) for o, l in zip(owns, lands)]
    return pl.pallas_call(
        body, name="scatter_sum",
        grid_spec=pltpu.PrefetchScalarGridSpec(
            num_scalar_prefetch=1, grid=(ADD_CHUNKS,),
            in_specs=[s[0] for s in all_specs] + [s[1] for s in all_specs], out_specs=[s[2] for s in all_specs]),
        out_shape=[jax.ShapeDtypeStruct((2,) + l.shape[1:], F32) for l in lands],
        compiler_params=_params(),
    )(pos, *owns, *lands)


def _swap_halves(halves):
    n = len(halves)

    def body(*refs):
        ins, outs = refs[:n], refs[n:2 * n]
        send_sems, recv_sems = refs[2 * n:]
        x, y, c = _coords()
        copies = [pltpu.make_async_remote_copy(
            src_ref=ins[idx].at[pl.ds(c, 1)], dst_ref=outs[idx].at[pl.ds(c, 1)], send_sem=send_sems.at[idx],
            recv_sem=recv_sems.at[idx], device_id=(x, y, 1 - c), device_id_type=MESH) for idx in range(n)]
        for cp in copies:
            cp.start()
        for cp in copies:
            cp.wait()

    return pl.pallas_call(
        body, name="swap_halves",
        in_specs=[ANY] * n, out_specs=[ANY] * n, input_output_aliases={idx: idx for idx in range(n)},
        out_shape=[jax.ShapeDtypeStruct(a.shape, F32) for a in halves],
        scratch_shapes=[pltpu.SemaphoreType.DMA((n,)), pltpu.SemaphoreType.DMA((n,))],
    )(*halves)


def _adamw_math(w, g, m, v):
    m = ADAM_B1 * m + (1.0 - ADAM_B1) * g
    v = ADAM_B2 * v + (1.0 - ADAM_B2) * jnp.square(g)
    m_hat = m / (1.0 - ADAM_B1 ** ADAM_STEP)
    v_hat = v / (1.0 - ADAM_B2 ** ADAM_STEP)
    delta = -ADAM_LR * (m_hat / (jnp.sqrt(v_hat) + ADAM_EPS) + ADAM_WD * w)
    return delta, m, v


def _adamw(name, w, g, m, v, block_rows):
    rows, cols = w.shape

    def body(w_ref, g_ref, m_ref, v_ref, d_out, m_out, v_out):
        d_out[...], m_out[...], v_out[...] = _adamw_math(w_ref[...], g_ref[...], m_ref[...], v_ref[...])

    spec = pl.BlockSpec((block_rows, cols), lambda i: (i, 0))
    shape = jax.ShapeDtypeStruct((rows, cols), F32)
    return pl.pallas_call(
        body, grid=(rows // block_rows,), name=name, in_specs=[spec] * 4, out_specs=[spec] * 3,
        out_shape=[shape] * 3, compiler_params=_params(),
    )(w, g, m, v)


def _adamw_many(ws, gs, ms, vs):
    n = len(ws)

    def body(*refs):
        w_refs, g_refs, m_refs, v_refs = (refs[q * n:(q + 1) * n] for q in range(4))
        d_outs, m_outs, v_outs = (refs[(4 + q) * n:(5 + q) * n] for q in range(3))
        for idx in range(n):
            d_outs[idx][...], m_outs[idx][...], v_outs[idx][...] = _adamw_math(
                w_refs[idx][...], g_refs[idx][...], m_refs[idx][...], v_refs[idx][...])

    shapes = [jax.ShapeDtypeStruct(w.shape, F32) for w in ws]
    outs = pl.pallas_call(
        body, name="adamw_small", in_specs=[VMEM_SPEC] * (4 * n), out_specs=[VMEM_SPEC] * (3 * n),
        out_shape=shapes * 3, compiler_params=pltpu.CompilerParams(vmem_limit_bytes=VMEM_LIMIT_BYTES),
    )(*ws, *gs, *ms, *vs)
    return outs[:n], outs[n:2 * n], outs[2 * n:]


def _adamw_ada(c_all16, dmod16, w, m, v, block_rows):
    rows, cols = w.shape

    def body(c_ref, dm_ref, w_ref, m_ref, v_ref, g_out, d_out, m_out, v_out):
        cv = c_ref[...]
        g = _tn((cv * jax.nn.sigmoid(cv)).astype(BF16), dm_ref[...].astype(BF16))
        g_out[...] = g
        d_out[...], m_out[...], v_out[...] = _adamw_math(w_ref[...], g, m_ref[...], v_ref[...])

    spec = pl.BlockSpec((block_rows, cols), lambda i: (i, 0))
    shape = jax.ShapeDtypeStruct((rows, cols), F32)
    return pl.pallas_call(
        body, grid=(rows // block_rows,), name="adamw_w_ada",
        in_specs=[pl.BlockSpec((16, block_rows), lambda i: (0, i)), _full(dmod16.shape), spec, spec, spec],
        out_specs=[spec] * 4, out_shape=[shape] * 4, compiler_params=_params(),
    )(c_all16, dmod16, w, m, v)


SMALL_REPLICATED = ["b_ada", "norm1_gain", "conv_dw_b", "conv_ln_g", "conv_ln_b", "gm_ln_g", "gm_ln_b", "gm_ws", "gm_bs",
                    "mix_out_gain", "norm2_gain", "ffn_dw_b", "final_gain"]
SMALL_SHARDED = ["conv_dw_w", "ffn_dw_w"]
PACK_ROWS = 256
WEIGHT_ORDER = ["w_ada", "b_ada", "norm1_gain", "w_in", "conv_dw_w", "conv_dw_b", "conv_ln_g", "conv_ln_b", "gm_ln_g",
                "gm_ln_b", "gm_ws", "gm_bs", "mix_out_gain", "w_out", "norm2_gain", "w_up", "ffn_dw_w", "ffn_dw_b",
                "w_down", "final_gain"]


def _pack(parts, rows):
    flat = jnp.concatenate([a.reshape(-1) for a in parts])
    return jnp.pad(flat, (0, rows * D_MODEL - flat.shape[0])).reshape(rows, D_MODEL)


def _unpack(packed, shapes):
    flat = packed.reshape(-1)
    out, pos = [], 0
    for s in shapes:
        size = 1
        for d in s:
            size *= d
        out.append(flat[pos:pos + size].reshape(s))
        pos += size
    return out


def kernel(x, c, w_ada, b_ada, norm1_gain, w_in, conv_dw_w, conv_dw_b, conv_ln_g, conv_ln_b, gm_ln_g, gm_ln_b, gm_ws, gm_bs, mix_out_gain, w_out, norm2_gain, w_up, ffn_dw_w, ffn_dw_b, w_down, final_gain, loss_target, m_w_ada, m_b_ada, m_norm1_gain, m_w_in, m_conv_dw_w, m_conv_dw_b, m_conv_ln_g, m_conv_ln_b, m_gm_ln_g, m_gm_ln_b, m_gm_ws, m_gm_bs, m_mix_out_gain, m_w_out, m_norm2_gain, m_w_up, m_ffn_dw_w, m_ffn_dw_b, m_w_down, m_final_gain, v_w_ada, v_b_ada, v_norm1_gain, v_w_in, v_conv_dw_w, v_conv_dw_b, v_conv_ln_g, v_conv_ln_b, v_gm_ln_g, v_gm_ln_b, v_gm_ws, v_gm_bs, v_mix_out_gain, v_w_out, v_norm2_gain, v_w_up, v_ffn_dw_w, v_ffn_dw_b, v_w_down, v_final_gain):
    weights = dict(w_ada=w_ada, b_ada=b_ada, norm1_gain=norm1_gain, w_in=w_in, conv_dw_w=conv_dw_w, conv_dw_b=conv_dw_b,
                   conv_ln_g=conv_ln_g, conv_ln_b=conv_ln_b, gm_ln_g=gm_ln_g, gm_ln_b=gm_ln_b, gm_ws=gm_ws, gm_bs=gm_bs,
                   mix_out_gain=mix_out_gain, w_out=w_out, norm2_gain=norm2_gain, w_up=w_up, ffn_dw_w=ffn_dw_w,
                   ffn_dw_b=ffn_dw_b, w_down=w_down, final_gain=final_gain)
    mom1 = dict(w_ada=m_w_ada, b_ada=m_b_ada, norm1_gain=m_norm1_gain, w_in=m_w_in, conv_dw_w=m_conv_dw_w,
                conv_dw_b=m_conv_dw_b, conv_ln_g=m_conv_ln_g, conv_ln_b=m_conv_ln_b, gm_ln_g=m_gm_ln_g, gm_ln_b=m_gm_ln_b,
                gm_ws=m_gm_ws, gm_bs=m_gm_bs, mix_out_gain=m_mix_out_gain, w_out=m_w_out, norm2_gain=m_norm2_gain,
                w_up=m_w_up, ffn_dw_w=m_ffn_dw_w, ffn_dw_b=m_ffn_dw_b, w_down=m_w_down, final_gain=m_final_gain)
    mom2 = dict(w_ada=v_w_ada, b_ada=v_b_ada, norm1_gain=v_norm1_gain, w_in=v_w_in, conv_dw_w=v_conv_dw_w,
                conv_dw_b=v_conv_dw_b, conv_ln_g=v_conv_ln_g, conv_ln_b=v_conv_ln_b, gm_ln_g=v_gm_ln_g, gm_ln_b=v_gm_ln_b,
                gm_ws=v_gm_ws, gm_bs=v_gm_bs, mix_out_gain=v_mix_out_gain, w_out=v_w_out, norm2_gain=v_norm2_gain,
                w_up=v_w_up, ffn_dw_w=v_ffn_dw_w, ffn_dw_b=v_ffn_dw_b, w_down=v_w_down, final_gain=v_final_gain)
    shard = 2 * lax.axis_index("x") + lax.axis_index("y")
    me = 2 * shard + lax.axis_index("c")

    ada_cols = w_ada.shape[2]
    b_ada_sh = lax.dynamic_slice(b_ada, (0, shard * ada_cols), (1, ada_cols))
    (w_in_g, w_out_g, w_up_part, w_down_part), (conv_w_g, ffn_w_g), c_all64, mod32 = _gather_weights(
        [w_in[0], w_out[0], w_up[0], w_down[0]], [conv_dw_w[0], ffn_dw_w[0]], 2, c, w_ada[0], b_ada_sh)
    c_all = c_all64[::8]
    mod = mod32[::8].reshape(1, N_SHARD * ada_cols)
    conv_w_full = jnp.transpose(conv_w_g, (1, 0, 2)).reshape(CONV_K, D_HALF)
    ffn_w_full = jnp.transpose(ffn_w_g, (1, 0, 2)).reshape(FFN_K, 2 * D_FF)

    p = dict(norm1_gain=norm1_gain, conv_dw_w=conv_w_full, conv_dw_b=conv_dw_b, conv_ln_g=conv_ln_g,
             conv_ln_b=conv_ln_b, gm_ln_g=gm_ln_g, gm_ln_b=gm_ln_b, gm_ws=gm_ws[0], gm_bs=gm_bs[0],
             mix_out_gain=mix_out_gain, norm2_gain=norm2_gain, ffn_dw_w=ffn_w_full, ffn_dw_b=ffn_dw_b,
             final_gain=final_gain[None])
    grad_x, g, d_mod, loss, in_flight = _local_step(
        x[0], loss_target[0], mod, p, w_in_g, w_out_g.reshape(D_MODEL, D_MODEL), w_up_part, w_down_part)

    n_mod = d_mod.shape[1]
    dmod_rows = lax.dynamic_update_slice(jnp.zeros((N_DEV, n_mod), F32), d_mod, (me, 0))
    g["b_ada"] = d_mod
    small = _pack([g[k] for k in SMALL_REPLICATED] + [g[k] for k in SMALL_SHARDED] + [dmod_rows, loss[0, :1]], PACK_ROWS)
    (land_w_in, land_w_out), small = _final_comm([in_flight["w_in16"], in_flight["w_out16"]], small)
    pos = jnp.stack(_coords()).astype(jnp.int32)
    halves = _scatter_sum(pos, [g["w_in"], g["w_out"], g["w_up"], g["w_down"]],
                          [land_w_in, land_w_out, in_flight["land_w_up"], in_flight["land_w_down"]])
    full = _swap_halves(halves)
    grads = dict(w_in=full[0].reshape(w_in.shape[1:]), w_out=full[1].reshape(w_out.shape[1:]),
                 w_up=full[2].reshape(w_up.shape[1:]), w_down=full[3].reshape(w_down.shape[1:]))

    small_shapes = ([weights[k].shape for k in SMALL_REPLICATED] + [(CONV_K, D_HALF), (FFN_K, 2 * D_FF)]
                    + [(N_DEV, n_mod), (1,)])
    *small_grads, conv_w_grad, ffn_w_grad, dmod_all, loss_sum = _unpack(small, small_shapes)
    grads.update(zip(SMALL_REPLICATED, small_grads))
    grads["conv_dw_w"] = lax.dynamic_slice(conv_w_grad, (0, shard * conv_dw_w.shape[2]), conv_dw_w.shape[1:])[None]
    grads["ffn_dw_w"] = lax.dynamic_slice(ffn_w_grad, (0, shard * ffn_dw_w.shape[2]), ffn_dw_w.shape[1:])[None]

    delta, new_m, new_v = {}, {}, {}
    for name, block_rows in (("w_in", 256), ("w_out", 128), ("w_up", 256), ("w_down", 352)):
        delta[name], new_m[name], new_v[name] = [a[None] for a in _adamw(
            "adamw_" + name, weights[name][0], grads[name], mom1[name][0], mom2[name][0], block_rows)]
        grads[name] = grads[name][None]
    dmod_sh = lax.dynamic_slice(dmod_all, (0, shard * ada_cols), (N_DEV, ada_cols))
    pad8 = ((0, 16 - N_DEV), (0, 0))
    grads["w_ada"], delta["w_ada"], new_m["w_ada"], new_v["w_ada"] = [a[None] for a in _adamw_ada(
        jnp.pad(c_all, pad8), jnp.pad(dmod_sh, pad8), w_ada[0], m_w_ada[0], v_w_ada[0], 256)]
    small_names = SMALL_REPLICATED + SMALL_SHARDED

    def two_d(a):
        return a.reshape(1, -1) if a.ndim == 1 else a

    small_out = _adamw_many(*[[two_d(d[k]) for k in small_names] for d in (weights, grads, mom1, mom2)])
    for d, arrs in zip((delta, new_m, new_v), small_out):
        d.update({k: a.reshape(weights[k].shape) for k, a in zip(small_names, arrs)})

    return (loss_sum.reshape(()), grad_x[None], *[grads[k] for k in WEIGHT_ORDER], *[delta[k] for k in WEIGHT_ORDER],
            *[new_m[k] for k in WEIGHT_ORDER], *[new_v[k] for k in WEIGHT_ORDER])
```

```python
import functools

import jax
import jax.numpy as jnp
from jax import lax
from jax.experimental import pallas as pl
from jax.experimental.pallas import tpu as pltpu

F32 = jnp.float32
BF16 = jnp.bfloat16

D_MODEL = 1024
D_HALF = 512
D_FF = 2816
CONV_K = 31
FFN_K = 3
CHUNK = 128
N_HEADS = 8
HEAD_DIM = 64
N_SHARD = 4
N_DEV = 8
RMS_EPS = 1e-6
LN_EPS = 1e-5
ADAM_LR, ADAM_B1, ADAM_B2, ADAM_EPS, ADAM_WD, ADAM_STEP = 0.001, 0.9, 0.999, 1e-08, 0.01, 10

TILE = 256
FWD_TILE = 512
HALO = 32
FFN_HALO = 16
FFN_BLK = 256
UP_SHARD = 2 * D_FF // N_SHARD
VMEM_LIMIT_BYTES = 56 * 1024 * 1024
FFN_VMEM_LIMIT_BYTES = 58 * 1024 * 1024

ANY = pl.BlockSpec(memory_space=pl.ANY)
NT_DIMS = (((1,), (1,)), ((), ()))
TN_DIMS = (((0,), (0,)), ((), ()))


def _full(shape):
    return pl.BlockSpec(shape, lambda i: (0,) * len(shape))


def _nn(a, b):
    return jnp.dot(a, b, preferred_element_type=F32)


def _nt(a, b):
    return lax.dot_general(a, b, NT_DIMS, preferred_element_type=F32)


def _tn(a, b):
    return lax.dot_general(a, b, TN_DIMS, preferred_element_type=F32)


def _colsum(a):
    return jnp.sum(a, axis=0, keepdims=True)


def _params(semantics=("arbitrary",)):
    return pltpu.CompilerParams(dimension_semantics=semantics, vmem_limit_bytes=VMEM_LIMIT_BYTES)


def _rms(v, gain):
    return v * lax.rsqrt(jnp.mean(v * v, axis=-1, keepdims=True) + RMS_EPS) * gain


def _layer_norm(v, gain, bias):
    mu = jnp.mean(v, axis=-1, keepdims=True)
    var = jnp.mean(jnp.square(v - mu), axis=-1, keepdims=True)
    return (v - mu) * lax.rsqrt(var + LN_EPS) * gain + bias


def _mod_norm(v, gain, scale, shift):
    return _rms(v, gain) * (1.0 + scale) + shift


def _conv_branch(a1, ln_g, ln_b, out_gain):
    a2 = _layer_norm(a1, ln_g, ln_b)
    return _rms(a2 * jax.nn.sigmoid(a2), out_gain)


def _gate_branch(gu, sp, out_gain):
    return _rms(jax.nn.gelu(gu) * sp, out_gain)


def _gv_norm(gv, ln_g, ln_b):
    return _layer_norm(jax.nn.gelu(gv), ln_g, ln_b)


def _rms_parts(v):
    r = lax.rsqrt(jnp.mean(v * v, axis=-1, keepdims=True) + RMS_EPS)
    return v * r, r


def _rms_back(dn, n, r):
    return r * (dn - n * jnp.mean(dn * n, axis=-1, keepdims=True))


def _ln_parts(v):
    mu = jnp.mean(v, axis=-1, keepdims=True)
    rs = lax.rsqrt(jnp.mean(jnp.square(v - mu), axis=-1, keepdims=True) + LN_EPS)
    return (v - mu) * rs, rs


def _ln_back(dn, n, rs):
    return rs * (dn - jnp.mean(dn, axis=-1, keepdims=True) - n * jnp.mean(dn * n, axis=-1, keepdims=True))


GELU_C = 0.7978845608028654
GELU_A = 0.044715


def _gelu_parts(v):
    v2 = v * v
    th = jnp.tanh(GELU_C * (v + GELU_A * (v2 * v)))
    cdf = 0.5 * (1.0 + th)
    return v * cdf, cdf + (0.5 * GELU_C) * v * (1.0 - th * th) * (1.0 + (3.0 * GELU_A) * v2)


def _rms_vjp(v, gain):
    n, r = _rms_parts(v)
    return n * gain, lambda dy: (_rms_back(dy * gain, n, r), _colsum(dy * n))


def _mod_norm_vjp(v, gain, scale, shift):
    n, r = _rms_parts(v)

    def back(dy):
        q = _colsum(dy * n)
        return _rms_back(dy * (gain * (1.0 + scale)), n, r), q * (1.0 + scale), q * gain, _colsum(dy)

    return n * gain * (1.0 + scale) + shift, back


def _conv_branch_vjp(a1, ln_g, ln_b, out_gain):
    n1, rs1 = _ln_parts(a1)
    a2 = n1 * ln_g + ln_b
    s = jax.nn.sigmoid(a2)
    a3 = a2 * s
    n3, r3 = _rms_parts(a3)

    def back(dy):
        da2 = _rms_back(dy * out_gain, n3, r3) * (s + a3 * (1.0 - s))
        return _ln_back(da2 * ln_g, n1, rs1), _colsum(da2 * n1), _colsum(da2), _colsum(dy * n3)

    return n3 * out_gain, back


def _gate_branch_vjp(gu, sp, out_gain):
    ge, dge = _gelu_parts(gu)
    n, r = _rms_parts(ge * sp)

    def back(dy):
        dg = _rms_back(dy * out_gain, n, r)
        return dg * sp * dge, dg * ge, _colsum(dy * n)

    return n * out_gain, back


def _gv_norm_vjp(gv, ln_g, ln_b):
    ge, dge = _gelu_parts(gv)
    n, rs = _ln_parts(ge)
    return n * ln_g + ln_b, lambda dy: (_ln_back(dy * ln_g, n, rs) * dge, _colsum(dy * n), _colsum(dy))


def _head_pair_matmul(wp_ref, v):
    lane = lax.broadcasted_iota(jnp.int32, (CHUNK, CHUNK), 1)
    rows = []
    for n in range(v.shape[0] // CHUNK):
        cols = []
        for j in range(N_HEADS // 2):
            r = _nn(wp_ref[j], v[n * CHUNK:(n + 1) * CHUNK, j * CHUNK:(j + 1) * CHUNK])
            cols.append(jnp.where(lane < HEAD_DIM, r[:CHUNK], r[CHUNK:]))
        rows.append(jnp.concatenate(cols, axis=1))
    return jnp.concatenate(rows, axis=0)


def _tile_bias(bs, tokens):
    return jnp.concatenate([bs] * (tokens // CHUNK), axis=0)


FORWARD_LEAD = 8


def _fwd_mixer(x, vec, conv_w, wpair, bs_full, w_in_g, w_out_g, late_parts):
    seq = x.shape[0]
    t = FWD_TILE if seq % FWD_TILE == 0 else TILE
    n_tiles = seq // t
    n_late = len(late_parts)
    forward_step = max(n_tiles - FORWARD_LEAD, 0)
    names = ["norm1_gain", "sc1", "sh1", "gt1", "conv_dw_b", "conv_ln_g", "conv_ln_b", "gm_ln_g", "gm_ln_b",
             "mix_out_gain"]
    vecs = [vec[k] for k in names]

    def body(x_ref, g1, sc1, sh1, gt1, cb, clg, clb, vg, vb, mg, cw, wp, bs, win_hbm, wout_hbm, *rest):
        late = rest[n_late:2 * n_late]
        z_ref, a1_ref, sp_ref, y_ref, o1_ref, x2_ref = rest[2 * n_late:2 * n_late + 6]
        win_v, wout_v, halo, bank, sem, send_sems, recv_sems = rest[2 * n_late + 6:]
        i = pl.program_id(0)
        mx, my, mc = _coords()
        shard = 2 * mx + my

        def half(w, which):
            h = late[w].shape[1] // 2
            return pl.ds(pl.multiple_of(which * h, 16), h)

        def chip_of(j):
            return 2 * _flip(mx, CHIP_FLIPS[j][0]) + _flip(my, CHIP_FLIPS[j][1])

        def ici_copy(w, j, slot):
            rows = late[w].at[slot, half(w, mc)]
            return pltpu.make_async_remote_copy(
                src_ref=rows, dst_ref=rows, send_sem=send_sems.at[w, j], recv_sem=recv_sems.at[w, j],
                device_id=(_flip(mx, CHIP_FLIPS[j][0]), _flip(my, CHIP_FLIPS[j][1]), mc), device_id_type=MESH)

        def d2d_copy(w, j, which):
            rows = late[w].at[chip_of(j), half(w, which)]
            return pltpu.make_async_remote_copy(
                src_ref=rows, dst_ref=rows, send_sem=send_sems.at[w, len(CHIP_FLIPS) + j],
                recv_sem=recv_sems.at[w, len(CHIP_FLIPS) + j], device_id=(mx, my, 1 - mc), device_id_type=MESH)

        pairs = [(w, j) for w in range(n_late) for j in range(len(CHIP_FLIPS))]

        @pl.when(i == 0)
        def _():
            for w, j in pairs:
                ici_copy(w, j, shard).start()
            cps = [pltpu.make_async_copy(win_hbm, win_v, sem.at[0]),
                   pltpu.make_async_copy(wout_hbm, wout_v, sem.at[1])]
            for cp in cps:
                cp.start()
            for cp in cps:
                cp.wait()
            halo[...] = jnp.zeros_like(halo)

        @pl.when(i == forward_step)
        def _():
            for w, j in pairs:
                ici_copy(w, j, chip_of(j)).wait_recv()
                d2d_copy(w, j, mc).start()

        xv = x_ref[...]
        h1b = _mod_norm(xv, g1[...], sc1[...], sh1[...]).astype(BF16)
        zs = [_nn(h1b, win_v[k]) for k in range(N_SHARD)]
        for k in range(N_SHARD):
            z_ref[:, k * D_HALF:(k + 1) * D_HALF] = zs[k]
        ca, cg, gu, gv = zs
        a0 = ca * jax.nn.sigmoid(cg)
        ext = jnp.concatenate([halo[...], a0], axis=0)
        halo[...] = a0[t - HALO:]
        bank[0] = ext
        for b in range(1, 8):
            bank[b] = pltpu.roll(ext, b, axis=0)
        a1 = jnp.zeros((t, D_HALF), F32) + cb[...]
        for s in range(CONV_K):
            q, b = divmod(s, 8)
            a1 = a1 + bank[b, pl.ds(HALO - 8 * q, t), :] * cw[pl.ds(CONV_K - 1 - s, 1), :]
        a1_ref[...] = a1
        mgv = mg[...]
        ya = _conv_branch(a1, clg[...], clb[...], mgv[:, :D_HALF])
        gvn = _gv_norm(gv, vg[...], vb[...]).astype(BF16)
        sp = _head_pair_matmul(wp, gvn) + _tile_bias(bs[...], t)
        sp_ref[...] = sp
        yg = _gate_branch(gu, sp, mgv[:, D_HALF:])
        yb = jnp.concatenate([ya, yg], axis=1).astype(BF16)
        y_ref[...] = yb
        o1 = _nn(yb, wout_v[...])
        o1_ref[...] = o1
        x2_ref[...] = xv + gt1[...] * o1

        @pl.when(i == n_tiles - 1)
        def _():
            for w, j in pairs:
                d2d_copy(w, j, 1 - mc).wait_recv()
            for w, j in pairs:
                ici_copy(w, j, shard).wait_send()
                d2d_copy(w, j, mc).wait_send()

    def row(width):
        return pl.BlockSpec((t, width), lambda i: (i, 0))

    out_shape = [jax.ShapeDtypeStruct((seq, 4 * D_HALF), F32), jax.ShapeDtypeStruct((seq, D_HALF), F32),
                 jax.ShapeDtypeStruct((seq, D_HALF), F32), jax.ShapeDtypeStruct((seq, D_MODEL), BF16),
                 jax.ShapeDtypeStruct((seq, D_MODEL), F32), jax.ShapeDtypeStruct((seq, D_MODEL), F32)]
    n_in = 1 + len(vecs) + 3 + 2
    sem_shape = (n_late, 2 * len(CHIP_FLIPS))
    outs = pl.pallas_call(
        body, grid=(n_tiles,), name="fwd_mixer",
        in_specs=[row(D_MODEL)] + [_full(v.shape) for v in vecs]
        + [_full(conv_w.shape), _full(wpair.shape), _full(bs_full.shape), ANY, ANY] + [ANY] * n_late,
        out_specs=[ANY] * n_late + [row(4 * D_HALF), row(D_HALF), row(D_HALF), row(D_MODEL), row(D_MODEL),
                                    row(D_MODEL)],
        out_shape=[jax.ShapeDtypeStruct(a.shape, a.dtype) for a in late_parts] + out_shape,
        input_output_aliases={n_in + w: w for w in range(n_late)},
        scratch_shapes=[pltpu.VMEM(w_in_g.shape, BF16), pltpu.VMEM(w_out_g.shape, BF16),
                        pltpu.VMEM((HALO, D_HALF), F32), pltpu.VMEM((8, t + HALO, D_HALF), F32),
                        pltpu.SemaphoreType.DMA((2,)), pltpu.SemaphoreType.DMA(sem_shape),
                        pltpu.SemaphoreType.DMA(sem_shape)],
        compiler_params=_params(),
    )(x, *vecs, conv_w, wpair, bs_full, w_in_g, w_out_g, *late_parts)
    return outs[n_late:], outs[:n_late]


def _interleave_matrices():
    row = jnp.arange(TILE)
    token_of_row = (row % 8) * (TILE // 8) + row // 8
    to_inter = (token_of_row[:, None] == row[None, :]).astype(BF16)
    return to_inter, jnp.transpose(to_inter)


def _ffn(x2, target, norm2_gain, sc2, sh2, ffn_w, ffn_b, gt2, final_gain, w_up_g, w_down_g, to_inter, to_natural):
    seq = x2.shape[0]
    n_tiles = seq // TILE
    t = TILE
    n_blk = D_FF // FFN_BLK
    inv_d = 1.0 / D_MODEL

    def body(x2_ref, x2h_ref, tgt_ref, g2, sc2_ref, sh2_ref, fw, fb, gt2_ref, fg, pm_ref, pmt_ref, wup_hbm, wd_hbm,
             du_ref, dx2_ref, dfw_ref, dfb_ref, dfg_ref, dgt2_ref, dg2_ref, dsc2_ref, dsh2_ref, loss_ref, dwd_hbm,
             dwd16_hbm, wup_v, wd_v, dwd_acc, carry, u_s, sil_s, vds_s, f_s, du_s, sem):
        i = pl.program_id(0)
        tile = n_tiles - 1 - i
        sublane = lax.broadcasted_iota(jnp.int32, (8, FFN_BLK), 0)

        @pl.when(i == 0)
        def _():
            cps = [pltpu.make_async_copy(wd_hbm, wd_v, sem.at[0])]
            cps += [pltpu.make_async_copy(wup_hbm.at[k], wup_v.at[:, pl.ds(k * UP_SHARD, UP_SHARD)], sem.at[3 + k])
                    for k in range(N_SHARD)]
            for cp in cps:
                cp.start()
            for cp in cps:
                cp.wait()
            dwd_acc[...] = jnp.zeros_like(dwd_acc)
            carry[...] = jnp.zeros_like(carry)
            dfw_ref[...] = jnp.zeros_like(dfw_ref)
            dfb_ref[...] = jnp.zeros_like(dfb_ref)
            dfg_ref[...] = jnp.zeros_like(dfg_ref)
            dgt2_ref[...] = jnp.zeros_like(dgt2_ref)
            dg2_ref[...] = jnp.zeros_like(dg2_ref)
            dsc2_ref[...] = jnp.zeros_like(dsc2_ref)
            dsh2_ref[...] = jnp.zeros_like(dsh2_ref)
            loss_ref[...] = jnp.zeros_like(loss_ref)

        def cols_of(j):
            return pl.ds(j * FFN_BLK, FFN_BLK), pl.ds(D_FF + j * FFN_BLK, FFN_BLK)

        def wrap_down(last, before):
            return jnp.where(sublane == 0, pltpu.roll(before, 1, axis=0), pltpu.roll(last, 1, axis=0))

        def wrap_up(first, after):
            return jnp.where(sublane == 7, pltpu.roll(after, 7, axis=0), pltpu.roll(first, 7, axis=0))

        x2v = x2_ref[...]
        h2, h2_vjp = _mod_norm_vjp(x2v, g2[...], sc2_ref[...], sh2_ref[...])
        h2b = h2.astype(BF16)
        h2_before = _mod_norm(x2h_ref[...], g2[...], sc2_ref[...], sh2_ref[...]).astype(BF16)
        lhs = jnp.concatenate([_nn(pm_ref[...], h2b).astype(BF16), h2_before], axis=0)

        def up(j):
            cv, cg = cols_of(j)
            return _nn(lhs, wup_v[:, cv]), _nn(lhs, wup_v[:, cg])

        def conv(both, cols):
            cur = both[:t]
            u_s[:, cols] = cur.astype(BF16)
            before = jnp.where(tile > 0, both[t:], 0.0)
            w1 = wrap_down(cur[t - 8:], before)
            w2 = wrap_down(cur[t - 16:t - 8], pltpu.roll(before, 1, axis=0))
            back1 = jnp.concatenate([w1, cur[:t - 8]], axis=0)
            back2 = jnp.concatenate([w2, w1, cur[:t - 16]], axis=0)
            return (fb[:, cols] + cur * fw[pl.ds(2, 1), cols] + back1 * fw[pl.ds(1, 1), cols]
                    + back2 * fw[pl.ds(0, 1), cols])

        pm_t = pmt_ref[...]

        def to_natural_f32(a):
            hi = a.astype(BF16)
            rest = a - hi.astype(F32)
            mid = rest.astype(BF16)
            low = (rest - mid.astype(F32)).astype(BF16)
            return _nn(jnp.concatenate([pm_t, pm_t, pm_t], axis=1), jnp.concatenate([hi, mid, low], axis=0))

        o2 = jnp.zeros((t, D_MODEL), F32)
        ahead_uv = up(0)
        for j in range(n_blk):
            cv, cg = cols_of(j)
            both_v, both_g = ahead_uv
            if j + 1 < n_blk:
                ahead_uv = up(j + 1)
            val, gate = conv(both_v, cv), conv(both_g, cg)
            sig = jax.nn.sigmoid(gate)
            sil = gate * sig
            fb16 = (sil * val).astype(BF16)
            sil_s[:, cv] = sil
            vds_s[:, cv] = val * (sig + sil * (1.0 - sig))
            f_s[:, cv] = fb16
            o2 = o2 + _nn(fb16, wd_v[pl.ds(j * FFN_BLK, FFN_BLK), :])
        o2 = to_natural_f32(o2)

        gt2v = gt2_ref[...]
        x3 = x2v + gt2v * o2
        out, out_vjp = _rms_vjp(x3, fg[...])
        diff = out - tgt_ref[...]
        loss_ref[...] += jnp.zeros_like(loss_ref) + 0.5 * inv_d * jnp.sum(diff * diff)
        dx3, dfg = out_vjp(diff * inv_d)
        dfg_ref[...] += dfg
        dgt2_ref[...] += _colsum(dx3 * o2)
        do2b = _nn(pm_ref[...], (gt2v * dx3).astype(BF16)).astype(BF16)

        def conv_back(dd, cols):
            dfb_ref[:, cols] += _colsum(dd)
            nxt = carry[:, cols]
            w1 = wrap_up(dd[:8], nxt[:8])
            w2 = wrap_up(dd[8:16], nxt[8:])
            ahead = (dd, jnp.concatenate([dd[8:], w1], axis=0), jnp.concatenate([dd[16:], w1, w2], axis=0))
            carry[:, cols] = dd[:16]
            uv = u_s[:, cols].astype(F32)
            du = jnp.zeros((t, FFN_BLK), F32)
            for s in range(FFN_K):
                du = du + ahead[s] * fw[pl.ds(FFN_K - 1 - s, 1), cols]
                dfw_ref[pl.ds(FFN_K - 1 - s, 1), cols] += _colsum(ahead[s] * uv)
            du_s[:, cols] = du.astype(BF16)

        for j in range(n_blk):
            cv, cg = cols_of(j)
            rows = pl.ds(j * FFN_BLK, FFN_BLK)
            df = _nt(do2b, wd_v[rows, :])
            dwd_acc[rows, :] += _tn(f_s[:, cv], do2b)
            conv_back(df * sil_s[:, cv], cv)
            conv_back(df * vds_s[:, cv], cg)
        du16 = _nn(pm_t, du_s[...]).astype(BF16)
        du_ref[...] = du16
        dx2, dg2, dsc2, dsh2 = h2_vjp(_nt(du16, wup_v[...]))
        dx2_ref[...] = dx3 + dx2
        dg2_ref[...] += dg2
        dsc2_ref[...] += dsc2
        dsh2_ref[...] += dsh2

        @pl.when(i == n_tiles - 1)
        def _():
            cp = pltpu.make_async_copy(dwd_acc, dwd_hbm, sem.at[1])
            cp.start()
            wd_v[...] = dwd_acc[...].astype(BF16)
            cp16 = pltpu.make_async_copy(wd_v, dwd16_hbm, sem.at[2])
            cp16.start()
            cp.wait()
            cp16.wait()

    def rev(width):
        return pl.BlockSpec((t, width), lambda i: (n_tiles - 1 - i, 0))

    assert FFN_K == 3
    halo_spec = pl.BlockSpec((8, D_MODEL), lambda i: (jnp.maximum((n_tiles - 1 - i) * (t // 8) - 1, 0), 0))
    vec_spec = _full((1, D_MODEL))
    out_shape = [jax.ShapeDtypeStruct((seq, 2 * D_FF), BF16), jax.ShapeDtypeStruct((seq, D_MODEL), F32),
                 jax.ShapeDtypeStruct((FFN_K, 2 * D_FF), F32), jax.ShapeDtypeStruct((1, 2 * D_FF), F32),
                 jax.ShapeDtypeStruct((1, D_MODEL), F32), jax.ShapeDtypeStruct((1, D_MODEL), F32),
                 jax.ShapeDtypeStruct((1, D_MODEL), F32), jax.ShapeDtypeStruct((1, D_MODEL), F32),
                 jax.ShapeDtypeStruct((1, D_MODEL), F32),
                 jax.ShapeDtypeStruct((1, 128), F32), jax.ShapeDtypeStruct((D_FF, D_MODEL), F32),
                 jax.ShapeDtypeStruct((D_FF, D_MODEL), BF16)]
    return pl.pallas_call(
        body, grid=(n_tiles,), name="ffn",
        in_specs=[rev(D_MODEL), halo_spec, rev(D_MODEL), vec_spec, vec_spec, vec_spec, _full(ffn_w.shape),
                  _full(ffn_b.shape), _full(gt2.shape), _full(final_gain.shape), _full(to_inter.shape),
                  _full(to_natural.shape), ANY, ANY],
        out_specs=[rev(2 * D_FF), rev(D_MODEL), _full((FFN_K, 2 * D_FF)), _full((1, 2 * D_FF)), vec_spec, vec_spec,
                   vec_spec, vec_spec, vec_spec, _full((1, 128)), ANY, ANY],
        out_shape=out_shape,
        scratch_shapes=[pltpu.VMEM((D_MODEL, 2 * D_FF), BF16), pltpu.VMEM((D_FF, D_MODEL), BF16),
                        pltpu.VMEM((D_FF, D_MODEL), F32), pltpu.VMEM((FFN_HALO, 2 * D_FF), F32),
                        pltpu.VMEM((t, 2 * D_FF), BF16), pltpu.VMEM((t, D_FF), F32), pltpu.VMEM((t, D_FF), F32),
                        pltpu.VMEM((t, D_FF), BF16), pltpu.VMEM((t, 2 * D_FF), BF16),
                        pltpu.SemaphoreType.DMA((3 + N_SHARD,))],
        compiler_params=pltpu.CompilerParams(dimension_semantics=("arbitrary",), vmem_limit_bytes=FFN_VMEM_LIMIT_BYTES),
    )(x2, x2, target, norm2_gain, sc2, sh2, ffn_w, ffn_b, gt2, final_gain, to_inter, to_natural, w_up_g, w_down_g)


def _scatter_copies(src16, land, send_sems, recv_sems):
    x, y, c = _coords()
    h = src16.shape[1] // 2
    copies = []
    for f, flip in enumerate(PEER_FLIPS):
        tx, ty, tc = _flip(x, flip[0]), _flip(y, flip[1]), _flip(c, flip[2])
        copies.append(pltpu.make_async_remote_copy(
            src_ref=src16.at[2 * tx + ty, pl.ds(pl.multiple_of(tc * h, 16), h)], dst_ref=land.at[f],
            send_sem=send_sems.at[f], recv_sem=recv_sems.at[f], device_id=(tx, ty, tc), device_id_type=MESH))
    return copies


def _land_shape(src16):
    return jax.ShapeDtypeStruct((len(PEER_FLIPS), src16.shape[1] // 2, src16.shape[2]), BF16)


UP_TILE = 512


def _bwd_up(du, x2, norm2_gain, sc2, sh2, dwd16):
    seq = x2.shape[0]
    t = UP_TILE if seq % UP_TILE == 0 else TILE
    n_tiles = seq // t
    acc_shape = (N_SHARD, D_MODEL, UP_SHARD)

    def body(du_ref, x2_ref, g2, sc2_ref, sh2_ref, dwd16_hbm, dwup_hbm, dwup16_hbm, land_hbm,
             stage16, dwup_acc, sem, send_sems, recv_sems):
        i = pl.program_id(0)

        @pl.when(i == 0)
        def _():
            for cp in _scatter_copies(dwd16_hbm, land_hbm, send_sems, recv_sems):
                cp.start()
            dwup_acc[...] = jnp.zeros_like(dwup_acc)

        h2b = _mod_norm(x2_ref[...], g2[...], sc2_ref[...], sh2_ref[...]).astype(BF16)
        for k in range(N_SHARD):
            dwup_acc[k] += _tn(h2b, du_ref[:, k * UP_SHARD:(k + 1) * UP_SHARD])

        @pl.when(i == n_tiles - 1)
        def _():
            cp = pltpu.make_async_copy(dwup_acc, dwup_hbm, sem.at[0])
            cp.start()
            for k in range(N_SHARD):
                stage16[k] = dwup_acc[k].astype(BF16)
            cp16 = pltpu.make_async_copy(stage16, dwup16_hbm, sem.at[1])
            cp16.start()
            cp.wait()
            cp16.wait()
            for rc in _scatter_copies(dwd16_hbm, land_hbm, send_sems, recv_sems):
                rc.wait()

    def row(width):
        return pl.BlockSpec((t, width), lambda i: (i, 0))

    n_peer = len(PEER_FLIPS)
    return pl.pallas_call(
        body, grid=(n_tiles,), name="bwd_up",
        in_specs=[row(2 * D_FF), row(D_MODEL), _full((1, D_MODEL)), _full((1, D_MODEL)), _full((1, D_MODEL)), ANY],
        out_specs=[ANY, ANY, ANY],
        out_shape=[jax.ShapeDtypeStruct(acc_shape, F32), jax.ShapeDtypeStruct(acc_shape, BF16), _land_shape(dwd16)],
        scratch_shapes=[pltpu.VMEM(acc_shape, BF16), pltpu.VMEM(acc_shape, F32), pltpu.SemaphoreType.DMA((2,)),
                        pltpu.SemaphoreType.DMA((n_peer,)), pltpu.SemaphoreType.DMA((n_peer,))],
        compiler_params=_params(),
    )(du, x2, norm2_gain, sc2, sh2, dwd16)


def _bwd_mixer(dx2, x, z, a1, sp, yb, o1, vec, conv_w, wpair, wpair_t, causal_mask, w_in_g, w_out_g, dwup16):
    seq = x.shape[0]
    n_tiles = seq // TILE
    t = TILE
    names = ["norm1_gain", "sc1", "sh1", "gt1", "conv_ln_g", "conv_ln_b", "gm_ln_g", "gm_ln_b", "mix_out_gain"]
    vecs = [vec[k] for k in names]

    def body(dx2_ref, x_ref, z_ref, a1_ref, sp_ref, y_ref, o1_ref, g1, sc1, sh1, gt1, clg, clb, vg, vb, mg,
             cw, wp, wpt, mask_ref, win_hbm, wout_hbm, dwup16_hbm,
             gx_ref, dg1_ref, dsc1_ref, dsh1_ref, dgt1_ref, dcw_ref, dcb_ref, dclg_ref, dclb_ref, dvg_ref, dvb_ref,
             dmg_ref, dws_ref, dbs_ref, dwin_hbm, dwout_hbm, land_hbm, dwin16_hbm, dwout16_hbm,
             win_v, wout_v, dwin_acc, dwout_acc, carry, bank, dbs_acc, lwin, lwout, sem, send_sems, recv_sems,
             pair_send, pair_recv):
        i = pl.program_id(0)
        small = [dg1_ref, dsc1_ref, dsh1_ref, dgt1_ref, dcw_ref, dcb_ref, dclg_ref, dclb_ref, dvg_ref, dvb_ref,
                 dmg_ref, dws_ref, dbs_acc]

        @pl.when(i == 0)
        def _():
            for cp in _scatter_copies(dwup16_hbm, land_hbm, send_sems, recv_sems):
                cp.start()
            cps = [pltpu.make_async_copy(win_hbm, win_v, sem.at[0]),
                   pltpu.make_async_copy(wout_hbm, wout_v, sem.at[1])]
            for cp in cps:
                cp.start()
            for cp in cps:
                cp.wait()
            dwin_acc[...] = jnp.zeros_like(dwin_acc)
            dwout_acc[...] = jnp.zeros_like(dwout_acc)
            carry[...] = jnp.zeros_like(carry)
            for ref in small:
                ref[...] = jnp.zeros_like(ref)

        dx2v = dx2_ref[...]
        gt1v = gt1[...]
        dgt1_ref[...] += _colsum(dx2v * o1_ref[...])
        do1b = (gt1v * dx2v).astype(BF16)
        dy = _nt(do1b, wout_v[...])
        dwout_acc[...] += _tn(y_ref[...], do1b)

        mgv = mg[...]
        _, conv_vjp = _conv_branch_vjp(a1_ref[...], clg[...], clb[...], mgv[:, :D_HALF])
        da1, dclg, dclb, dmg_a = conv_vjp(dy[:, :D_HALF])
        dclg_ref[...] += dclg
        dclb_ref[...] += dclb
        gu = z_ref[:, 2 * D_HALF:3 * D_HALF]
        gv = z_ref[:, 3 * D_HALF:]
        spv = sp_ref[...]
        _, gate_vjp = _gate_branch_vjp(gu, spv, mgv[:, D_HALF:])
        dgu, dsp, dmg_g = gate_vjp(dy[:, D_HALF:])
        dmg_ref[...] += jnp.concatenate([dmg_a, dmg_g], axis=1)
        gvn, gv_vjp = _gv_norm_vjp(gv, vg[...], vb[...])
        gvnb = gvn.astype(BF16)
        dspb = dsp.astype(BF16)
        dgvn = _head_pair_matmul(wpt, dspb)
        dgv, dvg, dvb = gv_vjp(dgvn)
        dvg_ref[...] += dvg
        dvb_ref[...] += dvb
        lane = lax.broadcasted_iota(jnp.int32, (CHUNK, CHUNK), 1)
        dbs = jnp.zeros((CHUNK, D_HALF), F32)
        for n in range(t // CHUNK):
            rows = slice(n * CHUNK, (n + 1) * CHUNK)
            dbs = dbs + dsp[rows, :]
            for j in range(N_HEADS // 2):
                cols = slice(j * CHUNK, (j + 1) * CHUNK)
                blk = dspb[rows, cols]
                zero = jnp.zeros_like(blk)
                vblk = gvnb[rows, cols]
                dws_ref[2 * j] += _nt(jnp.where(lane < HEAD_DIM, blk, zero), vblk)
                dws_ref[2 * j + 1] += _nt(jnp.where(lane < HEAD_DIM, zero, blk), vblk)
        dbs_acc[...] += dbs

        h1, h1_vjp = _mod_norm_vjp(x_ref[...], g1[...], sc1[...], sh1[...])
        h1b = h1.astype(BF16)
        dh1 = jnp.zeros((t, D_MODEL), F32)
        for k, dzk in ((2, dgu), (3, dgv)):
            dzb = dzk.astype(BF16)
            dh1 = dh1 + _nt(dzb, win_v[k])
            dwin_acc[k] += _tn(h1b, dzb)

        ca = z_ref[:, :D_HALF]
        cg = z_ref[:, D_HALF:2 * D_HALF]
        sig = jax.nn.sigmoid(cg)
        a0 = ca * sig
        ext = jnp.concatenate([da1, carry[...]], axis=0)
        carry[...] = da1[:HALO]
        bank[0] = ext
        for b in range(1, 8):
            bank[b] = pltpu.roll(ext, t + HALO - b, axis=0)
        dcb_ref[...] += _colsum(da1)
        da0 = jnp.zeros((t, D_HALF), F32)
        for s in range(CONV_K):
            q, b = divmod(s, 8)
            shifted = bank[b, pl.ds(8 * q, t), :]
            da0 = da0 + shifted * cw[pl.ds(CONV_K - 1 - s, 1), :]
            dcw_ref[pl.ds(CONV_K - 1 - s, 1), :] += _colsum(shifted * a0)
        dca = da0 * sig
        dcg = da0 * ca * sig * (1.0 - sig)

        for k, dzk in ((0, dca), (1, dcg)):
            dzb = dzk.astype(BF16)
            dh1 = dh1 + _nt(dzb, win_v[k])
            dwin_acc[k] += _tn(h1b, dzb)
        dx, dg1, dsc1, dsh1 = h1_vjp(dh1)
        gx_ref[...] = dx2v + dx
        dg1_ref[...] += dg1
        dsc1_ref[...] += dsc1
        dsh1_ref[...] += dsh1

        @pl.when(i == n_tiles - 1)
        def _():
            for h in range(N_HEADS):
                dws_ref[h] = dws_ref[h] * mask_ref[...]
            head_of_lane = lax.broadcasted_iota(jnp.int32, (N_HEADS, D_HALF), 1) // HEAD_DIM
            pick = (head_of_lane == lax.broadcasted_iota(jnp.int32, (N_HEADS, D_HALF), 0)).astype(F32)
            dbs_ref[...] = lax.dot_general(pick, dbs_acc[...], NT_DIMS, precision=lax.Precision.HIGHEST,
                                           preferred_element_type=F32)
            for k in range(N_SHARD):
                win_v[k] = dwin_acc[k].astype(BF16)
            wout_v[...] = dwout_acc[...].astype(BF16)
            mx, my, mc = _coords()
            h_in, h_out = dwin_acc.shape[1] // 2, dwout_acc.shape[0] // (2 * N_SHARD)

            def in_rows(ref, k, which):
                return ref.at[k, pl.ds(pl.multiple_of(which * h_in, 16), h_in), :]

            def out_rows(ref, k, which):
                return ref.at[pl.ds(pl.multiple_of((2 * k + which) * h_out, 16), h_out), :]

            pairs = ((win_v, dwin_acc, lwin, in_rows, dwin_hbm, dwin16_hbm),
                     (wout_v, dwout_acc, lwout, out_rows, dwout_hbm, dwout16_hbm))
            swaps = [pltpu.make_async_remote_copy(
                src_ref=rows_of(v16, k, 1 - mc), dst_ref=land.at[k], send_sem=pair_send.at[w, k],
                recv_sem=pair_recv.at[w, k], device_id=(mx, my, 1 - mc), device_id_type=MESH)
                for w, (v16, _, land, rows_of, _, _) in enumerate(pairs) for k in range(N_SHARD)]
            for cp in swaps:
                cp.start()
            for cp in swaps:
                cp.wait()
            outs = []
            for w, (v16, acc, land, rows_of, half_hbm, half16_hbm) in enumerate(pairs):
                for k in range(N_SHARD):
                    total = rows_of(acc, k, mc)[...] + land[k].astype(F32)
                    rows_of(acc, k, 0)[...] = total
                    rows_of(v16, k, 0)[...] = total.astype(BF16)
                    outs.append(pltpu.make_async_copy(rows_of(acc, k, 0), half_hbm.at[k], sem.at[2 + 8 * w + k]))
                    outs.append(pltpu.make_async_copy(rows_of(v16, k, 0), half16_hbm.at[k], sem.at[6 + 8 * w + k]))
            for cp in outs:
                cp.start()
            for cp in outs:
                cp.wait()
            for rc in _scatter_copies(dwup16_hbm, land_hbm, send_sems, recv_sems):
                rc.wait()

    def rev(width):
        return pl.BlockSpec((t, width), lambda i: (n_tiles - 1 - i, 0))

    v1024 = jax.ShapeDtypeStruct((1, D_MODEL), F32)
    v512 = jax.ShapeDtypeStruct((1, D_HALF), F32)
    small_shapes = [v1024, v1024, v1024, v1024, jax.ShapeDtypeStruct((CONV_K, D_HALF), F32), v512, v512, v512, v512,
                    v512, v1024, jax.ShapeDtypeStruct((N_HEADS, CHUNK, CHUNK), F32),
                    jax.ShapeDtypeStruct((N_HEADS, CHUNK), F32)]
    n_peer = len(PEER_FLIPS)
    half_in = (N_SHARD, w_in_g.shape[1] // 2, w_in_g.shape[2])
    half_out = (N_SHARD, w_out_g.shape[0] // (2 * N_SHARD), w_out_g.shape[1])
    return pl.pallas_call(
        body, grid=(n_tiles,), name="bwd_mixer",
        in_specs=[rev(D_MODEL), rev(D_MODEL), rev(4 * D_HALF), rev(D_HALF), rev(D_HALF), rev(D_MODEL),
                  rev(D_MODEL)] + [_full(v.shape) for v in vecs]
        + [_full(conv_w.shape), _full(wpair.shape), _full(wpair_t.shape), _full(causal_mask.shape), ANY, ANY, ANY],
        out_specs=[rev(D_MODEL)] + [_full(s.shape) for s in small_shapes] + [ANY] * 5,
        out_shape=[jax.ShapeDtypeStruct((seq, D_MODEL), F32)] + small_shapes
        + [jax.ShapeDtypeStruct(half_in, F32), jax.ShapeDtypeStruct(half_out, F32), _land_shape(dwup16),
           jax.ShapeDtypeStruct(half_in, BF16), jax.ShapeDtypeStruct(half_out, BF16)],
        scratch_shapes=[pltpu.VMEM(w_in_g.shape, BF16), pltpu.VMEM(w_out_g.shape, BF16),
                        pltpu.VMEM(w_in_g.shape, F32), pltpu.VMEM(w_out_g.shape, F32),
                        pltpu.VMEM((HALO, D_HALF), F32), pltpu.VMEM((8, t + HALO, D_HALF), F32),
                        pltpu.VMEM((CHUNK, D_HALF), F32), pltpu.VMEM(half_in, BF16), pltpu.VMEM(half_out, BF16),
                        pltpu.SemaphoreType.DMA((2 + 4 * N_SHARD,)),
                        pltpu.SemaphoreType.DMA((n_peer,)), pltpu.SemaphoreType.DMA((n_peer,)),
                        pltpu.SemaphoreType.DMA((2, N_SHARD)), pltpu.SemaphoreType.DMA((2, N_SHARD))],
        compiler_params=_params(),
    )(dx2, x, z, a1, sp, yb, o1, *vecs, conv_w, wpair, wpair_t, causal_mask, w_in_g, w_out_g, dwup16)


def _gmlp_operands(gm_ws, gm_bs):
    mask = jnp.tril(jnp.ones((CHUNK, CHUNK), F32))
    ws = gm_ws * mask[None]
    wpair = ws.reshape(N_HEADS // 2, 2 * CHUNK, CHUNK).astype(BF16)
    wpair_t = jnp.swapaxes(ws, 1, 2).reshape(N_HEADS // 2, 2 * CHUNK, CHUNK).astype(BF16)
    bs_full = jnp.repeat(jnp.transpose(gm_bs), HEAD_DIM, axis=1)
    return wpair, wpair_t, bs_full, mask


def _local_step(x, target, mod, p, w_in_g, w_out_g, w_up_part, w_down_part):
    sh1, sc1, gt1, sh2, sc2, gt2 = [mod[:, k * D_MODEL:(k + 1) * D_MODEL] for k in range(6)]
    vec = dict(p, sh1=sh1, sc1=sc1, gt1=gt1, sh2=sh2, sc2=sc2, gt2=gt2)
    wpair, wpair_t, bs_full, mask = _gmlp_operands(p["gm_ws"], p["gm_bs"])

    (z, a1, sp, yb, o1, x2), (w_up_g, w_down_g) = _fwd_mixer(
        x, vec, p["conv_dw_w"], wpair, bs_full, w_in_g, w_out_g, [w_up_part, w_down_part])
    w_down_g = w_down_g.reshape(D_FF, D_MODEL)
    to_inter, to_natural = _interleave_matrices()
    du, dx2, d_ffn_w, d_ffn_b, d_fg, d_gt2, d_g2, d_sc2, d_sh2, loss, d_wd, d_wd16 = _ffn(
        x2, target, p["norm2_gain"], sc2, sh2, p["ffn_dw_w"], p["ffn_dw_b"], gt2, p["final_gain"], w_up_g, w_down_g,
        to_inter, to_natural)
    by_shard = (N_SHARD, -1, D_MODEL)
    d_wup, d_wup16, land_wd = _bwd_up(du, x2, p["norm2_gain"], sc2, sh2, d_wd16.reshape(by_shard))
    (gx, d_g1, d_sc1, d_sh1, d_gt1, d_cw, d_cb, d_clg, d_clb, d_vg, d_vb, d_mg, d_ws, d_bs, d_win, d_wout, land_wup,
     d_win16, d_wout16) = _bwd_mixer(dx2, x, z, a1, sp, yb, o1, vec, p["conv_dw_w"], wpair, wpair_t, mask, w_in_g,
                                     w_out_g, d_wup16)
    d_mod = jnp.concatenate([d_sh1, d_sc1, d_gt1, d_sh2, d_sc2, d_gt2], axis=1)
    grads = dict(norm1_gain=d_g1, conv_dw_w=d_cw, conv_dw_b=d_cb, conv_ln_g=d_clg, conv_ln_b=d_clb, gm_ln_g=d_vg,
                 gm_ln_b=d_vb, gm_ws=d_ws, gm_bs=d_bs, mix_out_gain=d_mg, norm2_gain=d_g2, ffn_dw_w=d_ffn_w,
                 ffn_dw_b=d_ffn_b, final_gain=d_fg, w_in=d_win, w_out=d_wout, w_up=d_wup, w_down=d_wd.reshape(by_shard))
    in_flight = dict(w_in16=d_win16, w_out16=d_wout16, land_w_up=land_wup, land_w_down=land_wd)
    return gx, grads, d_mod, loss, in_flight


MESH = pl.DeviceIdType.MESH
VMEM_SPEC = pl.BlockSpec(memory_space=pltpu.VMEM)
PEER_FLIPS = [(a, b, d) for a in (0, 1) for b in (0, 1) for d in (0, 1)][1:]
CHIP_FLIPS = [(1, 0), (0, 1), (1, 1)]


def _coords():
    return lax.axis_index("x"), lax.axis_index("y"), lax.axis_index("c")


def _flip(v, bit):
    return 1 - v if bit else v


def _rows8(block):
    return pl.ds(pl.multiple_of(8 * block, 8), 8)


def _ada_steps(c_ref, w_ref, b_ref, call_ref, mod_ref, cpad, modall, send_sems, recv_sems):
    x, y, c = _coords()
    me = 4 * x + 2 * y + c
    cpad[...] = jnp.zeros_like(cpad)
    cpad[pl.ds(0, 1), :] = c_ref[...]

    def gather_copy(j, flip):
        peer = (_flip(x, flip[0]), _flip(y, flip[1]), _flip(c, flip[2]))
        return pltpu.make_async_remote_copy(
            src_ref=cpad, dst_ref=call_ref.at[_rows8(me)], send_sem=send_sems.at[j], recv_sem=recv_sems.at[j],
            device_id=peer, device_id_type=MESH)

    def piece_copy(j, flip):
        tx, ty = _flip(x, flip[0]), _flip(y, flip[1])
        return pltpu.make_async_remote_copy(
            src_ref=modall.at[_rows8(4 * tx + 2 * ty + c)], dst_ref=mod_ref.at[_rows8(2 * x + y)],
            send_sem=send_sems.at[len(PEER_FLIPS) + j], recv_sem=recv_sems.at[len(PEER_FLIPS) + j],
            device_id=(tx, ty, c), device_id_type=MESH)

    copies = [gather_copy(j, f) for j, f in enumerate(PEER_FLIPS)]
    for cp in copies:
        cp.start()
    call_ref[_rows8(me), :] = cpad[...]

    def middle():
        for cp in copies:
            cp.wait_recv()
        for cp in copies:
            cp.wait_send()
        cv = call_ref[...]
        c_act = (cv * jax.nn.sigmoid(cv)).astype(BF16)
        modall[...] = _nn(c_act, w_ref[...].astype(BF16)) + b_ref[...]
        for j, f in enumerate(CHIP_FLIPS):
            piece_copy(j, f).start()
        mod_ref[_rows8(2 * x + y), :] = modall[_rows8(me), :]

    def finish():
        for j, f in enumerate(CHIP_FLIPS):
            piece_copy(j, f).wait_recv()
        for j, f in enumerate(CHIP_FLIPS):
            piece_copy(j, f).wait_send()

    return middle, finish


def _gather_weights(shards, filters, n_now, c_row, w_ada_sh, b_ada_sh):
    n = len(shards)
    nf = len(filters)
    ada_cols = w_ada_sh.shape[1]

    def body(*refs):
        ins, f_ins, ada_ins = refs[:n], refs[n:n + nf], refs[n + nf:n + nf + 3]
        refs = refs[n + nf + 3:]
        outs, f_outs, ada_outs = refs[:n], refs[n:n + nf], refs[n + nf:n + nf + 2]
        refs = refs[n + nf + 2:]
        stage = refs[:n]
        send_sems, recv_sems, local_sems, f_send_sems, f_recv_sems, cpad, modall, ada_send, ada_recv = refs[n:]
        ada_middle, ada_finish = _ada_steps(*ada_ins, *ada_outs, cpad, modall, ada_send, ada_recv)
        x, y, c = _coords()
        k = 2 * x + y
        sibling = (x, y, 1 - c)

        def filter_copy(w, j, slot):
            tx, ty = _flip(x, CHIP_FLIPS[j][0]), _flip(y, CHIP_FLIPS[j][1])
            return pltpu.make_async_remote_copy(
                src_ref=f_ins[w], dst_ref=f_outs[w].at[slot], send_sem=f_send_sems.at[w, j],
                recv_sem=f_recv_sems.at[w, j], device_id=(tx, ty, c), device_id_type=MESH)

        def half(w, which):
            h = shards[w].shape[0] // 2
            return pl.ds(pl.multiple_of(which * h, 16), h)

        def ici_copy(w, j, src, slot):
            tx, ty = _flip(x, CHIP_FLIPS[j][0]), _flip(y, CHIP_FLIPS[j][1])
            return pltpu.make_async_remote_copy(
                src_ref=src, dst_ref=outs[w].at[slot, half(w, c)], send_sem=send_sems.at[w, j],
                recv_sem=recv_sems.at[w, j], device_id=(tx, ty, c), device_id_type=MESH)

        def d2d_copy(w, j, slot, which):
            rows = outs[w].at[slot, half(w, which)]
            return pltpu.make_async_remote_copy(
                src_ref=rows, dst_ref=rows, send_sem=send_sems.at[w, len(CHIP_FLIPS) + j],
                recv_sem=recv_sems.at[w, len(CHIP_FLIPS) + j], device_id=sibling, device_id_type=MESH)

        def chip_of(j):
            return 2 * _flip(x, CHIP_FLIPS[j][0]) + _flip(y, CHIP_FLIPS[j][1])

        local, first, passed = [], [], []
        for w in range(nf):
            local.append(pltpu.make_async_copy(f_ins[w], f_outs[w].at[k], local_sems.at[n + w]))
            local[-1].start()
            for j in range(len(CHIP_FLIPS)):
                first.append(filter_copy(w, j, k))
                first[-1].start()
        for w in range(n):
            stage[w][...] = ins[w][...].astype(BF16)
            local.append(pltpu.make_async_copy(stage[w], outs[w].at[k], local_sems.at[w]))
            local[-1].start()
            if w < n_now:
                for j in range(len(CHIP_FLIPS)):
                    first.append(ici_copy(w, j, stage[w].at[half(w, c)], k))
                    first[-1].start()
        ada_middle()
        for w in range(nf):
            for j in range(len(CHIP_FLIPS)):
                filter_copy(w, j, chip_of(j)).wait_recv()
        for w in range(n_now):
            for j in range(len(CHIP_FLIPS)):
                ici_copy(w, j, stage[w].at[half(w, c)], chip_of(j)).wait_recv()
                passed.append(d2d_copy(w, j, chip_of(j), c))
                passed[-1].start()
        for w in range(n_now):
            for j in range(len(CHIP_FLIPS)):
                d2d_copy(w, j, chip_of(j), 1 - c).wait_recv()
        for cp in first + passed:
            cp.wait_send()
        for cp in local:
            cp.wait()
        ada_finish()

    sem_shape = (n_now, 2 * len(CHIP_FLIPS))
    f_sem_shape = (nf, len(CHIP_FLIPS))
    n_ada_sem = len(PEER_FLIPS) + len(CHIP_FLIPS)
    outs = pl.pallas_call(
        body, name="gather_weights",
        in_specs=[VMEM_SPEC] * (n + nf + 3), out_specs=[ANY] * (n + nf) + [VMEM_SPEC, VMEM_SPEC],
        out_shape=[jax.ShapeDtypeStruct((N_SHARD,) + s.shape, BF16) for s in shards]
        + [jax.ShapeDtypeStruct((N_SHARD,) + s.shape, F32) for s in filters]
        + [jax.ShapeDtypeStruct((8 * N_DEV, D_MODEL), F32), jax.ShapeDtypeStruct((8 * N_SHARD, ada_cols), F32)],
        scratch_shapes=[pltpu.VMEM(s.shape, BF16) for s in shards]
        + [pltpu.SemaphoreType.DMA(sem_shape), pltpu.SemaphoreType.DMA(sem_shape), pltpu.SemaphoreType.DMA((n + nf,)),
           pltpu.SemaphoreType.DMA(f_sem_shape), pltpu.SemaphoreType.DMA(f_sem_shape),
           pltpu.VMEM((8, D_MODEL), F32), pltpu.VMEM((8 * N_DEV, ada_cols), F32),
           pltpu.SemaphoreType.DMA((n_ada_sem,)), pltpu.SemaphoreType.DMA((n_ada_sem,))],
        compiler_params=pltpu.CompilerParams(vmem_limit_bytes=VMEM_LIMIT_BYTES),
    )(*shards, *filters, c_row, w_ada_sh, b_ada_sh)
    return outs[:n], outs[n:n + nf], outs[n + nf], outs[n + nf + 1]


def _final_comm(srcs16, small):
    n = len(srcs16)
    rows = small.shape[0]
    half = rows // 2

    def body(*refs):
        srcs, small_ref = refs[:n], refs[n]
        lands, small_out = refs[n + 1:2 * n + 1], refs[2 * n + 1]
        chip_sum, got_c, got_x, got_y, part_x, send_sems, recv_sems, small_send_sems, small_recv_sems = refs[2 * n + 2:]
        x, y, c = _coords()
        sibling = (x, y, 1 - c)
        mine = pl.ds(pl.multiple_of(c * half, 8), half)
        copies = []
        for w in range(n):
            for j, flip in enumerate(CHIP_FLIPS):
                tx, ty = _flip(x, flip[0]), _flip(y, flip[1])
                copies.append(pltpu.make_async_remote_copy(
                    src_ref=srcs[w].at[2 * tx + ty], dst_ref=lands[w].at[j], send_sem=send_sems.at[w, j],
                    recv_sem=recv_sems.at[w, j], device_id=(tx, ty, c), device_id_type=MESH))
        for cp in copies:
            cp.start()

        def exchange(stage, src, dst, peer):
            rc = pltpu.make_async_remote_copy(
                src_ref=src, dst_ref=dst, send_sem=small_send_sems.at[stage], recv_sem=small_recv_sems.at[stage],
                device_id=peer, device_id_type=MESH)
            rc.start()
            rc.wait()

        exchange(0, small_ref, got_c, sibling)
        chip_sum[...] = small_ref[...] + got_c[...]
        exchange(1, chip_sum.at[mine], got_x, (1 - x, y, c))
        part_x[...] = chip_sum[mine, :] + got_x[...]
        exchange(2, part_x, got_y, (x, 1 - y, c))
        small_out[mine, :] = part_x[...] + got_y[...]
        exchange(3, small_out.at[mine], small_out.at[mine], sibling)
        for cp in copies:
            cp.wait()

    n_chip = len(CHIP_FLIPS)
    half_shape = (half, small.shape[1])
    outs = pl.pallas_call(
        body, name="final_comm",
        in_specs=[ANY] * n + [VMEM_SPEC], out_specs=[ANY] * n + [VMEM_SPEC],
        out_shape=[jax.ShapeDtypeStruct((n_chip,) + a.shape[1:], BF16) for a in srcs16]
        + [jax.ShapeDtypeStruct(small.shape, F32)],
        scratch_shapes=[pltpu.VMEM(small.shape, F32), pltpu.VMEM(small.shape, F32), pltpu.VMEM(half_shape, F32),
                        pltpu.VMEM(half_shape, F32), pltpu.VMEM(half_shape, F32),
                        pltpu.SemaphoreType.DMA((n, n_chip)), pltpu.SemaphoreType.DMA((n, n_chip)),
                        pltpu.SemaphoreType.DMA((4,)), pltpu.SemaphoreType.DMA((4,))],
        compiler_params=pltpu.CompilerParams(vmem_limit_bytes=VMEM_LIMIT_BYTES),
    )(*srcs16, small)
    return outs[:n], outs[n]


ADD_CHUNKS = 4


def _scatter_sum(pos, owns, lands):
    n = len(owns)

    def specs(own_shape, land_shape):
        peers, rows, cols = land_shape
        pick = 1 if own_shape[1] == 2 * rows else 0
        if cols % (128 * ADD_CHUNKS) == 0:
            blk = (rows, cols // ADD_CHUNKS)
            return (pl.BlockSpec((1,) + blk, lambda i, p: (2 * p[0] + p[1], pick * p[2], i)),
                    pl.BlockSpec((peers,) + blk, lambda i, p: (0, 0, i)),
                    pl.BlockSpec((1,) + blk, lambda i, p: (p[2], 0, i)))
        blk = (rows // ADD_CHUNKS, cols)
        return (pl.BlockSpec((1,) + blk, lambda i, p: (2 * p[0] + p[1], pick * p[2] * ADD_CHUNKS + i, 0)),
                pl.BlockSpec((peers,) + blk, lambda i, p: (0, i, 0)),
                pl.BlockSpec((1,) + blk, lambda i, p: (p[2], i, 0)))

    def body(pos_ref, *refs):
        for idx in range(n):
            own, land, out = refs[idx], refs[n + idx], refs[2 * n + idx]
            total = own[0]
            for f in range(land.shape[0]):
                total = total + land[f].astype(F32)
            out[0] = total

    all_specs = [specs(o.shape, l.shape) for o, l in zip(owns, lands)]
    return pl.pallas_call(
        body, name="scatter_sum",
        grid_spec=pltpu.PrefetchScalarGridSpec(
            num_scalar_prefetch=1, grid=(ADD_CHUNKS,),
            in_specs=[s[0] for s in all_specs] + [s[1] for s in all_specs], out_specs=[s[2] for s in all_specs]),
        out_shape=[jax.ShapeDtypeStruct((2,) + l.shape[1:], F32) for l in lands],
        compiler_params=_params(),
    )(pos, *owns, *lands)


def _swap_halves(halves):
    n = len(halves)

    def body(*refs):
        ins, outs = refs[:n], refs[n:2 * n]
        send_sems, recv_sems = refs[2 * n:]
        x, y, c = _coords()
        copies = [pltpu.make_async_remote_copy(
            src_ref=ins[idx].at[pl.ds(c, 1)], dst_ref=outs[idx].at[pl.ds(c, 1)], send_sem=send_sems.at[idx],
            recv_sem=recv_sems.at[idx], device_id=(x, y, 1 - c), device_id_type=MESH) for idx in range(n)]
        for cp in copies:
            cp.start()
        for cp in copies:
            cp.wait()

    return pl.pallas_call(
        body, name="swap_halves",
        in_specs=[ANY] * n, out_specs=[ANY] * n, input_output_aliases={idx: idx for idx in range(n)},
        out_shape=[jax.ShapeDtypeStruct(a.shape, F32) for a in halves],
        scratch_shapes=[pltpu.SemaphoreType.DMA((n,)), pltpu.SemaphoreType.DMA((n,))],
    )(*halves)


def _adamw_math(w, g, m, v):
    m = ADAM_B1 * m + (1.0 - ADAM_B1) * g
    v = ADAM_B2 * v + (1.0 - ADAM_B2) * jnp.square(g)
    m_hat = m / (1.0 - ADAM_B1 ** ADAM_STEP)
    v_hat = v / (1.0 - ADAM_B2 ** ADAM_STEP)
    delta = -ADAM_LR * (m_hat / (jnp.sqrt(v_hat) + ADAM_EPS) + ADAM_WD * w)
    return delta, m, v


def _adamw(name, w, g, m, v, block_rows):
    rows, cols = w.shape

    def body(w_ref, g_ref, m_ref, v_ref, d_out, m_out, v_out):
        d_out[...], m_out[...], v_out[...] = _adamw_math(w_ref[...], g_ref[...], m_ref[...], v_ref[...])

    spec = pl.BlockSpec((block_rows, cols), lambda i: (i, 0))
    shape = jax.ShapeDtypeStruct((rows, cols), F32)
    return pl.pallas_call(
        body, grid=(rows // block_rows,), name=name, in_specs=[spec] * 4, out_specs=[spec] * 3,
        out_shape=[shape] * 3, compiler_params=_params(),
    )(w, g, m, v)


def _adamw_many(ws, gs, ms, vs):
    n = len(ws)

    def body(*refs):
        w_refs, g_refs, m_refs, v_refs = (refs[q * n:(q + 1) * n] for q in range(4))
        d_outs, m_outs, v_outs = (refs[(4 + q) * n:(5 + q) * n] for q in range(3))
        for idx in range(n):
            d_outs[idx][...], m_outs[idx][...], v_outs[idx][...] = _adamw_math(
                w_refs[idx][...], g_refs[idx][...], m_refs[idx][...], v_refs[idx][...])

    shapes = [jax.ShapeDtypeStruct(w.shape, F32) for w in ws]
    outs = pl.pallas_call(
        body, name="adamw_small", in_specs=[VMEM_SPEC] * (4 * n), out_specs=[VMEM_SPEC] * (3 * n),
        out_shape=shapes * 3, compiler_params=pltpu.CompilerParams(vmem_limit_bytes=VMEM_LIMIT_BYTES),
    )(*ws, *gs, *ms, *vs)
    return outs[:n], outs[n:2 * n], outs[2 * n:]


def _adamw_ada(c_all16, dmod16, w, m, v, block_rows):
    rows, cols = w.shape

    def body(c_ref, dm_ref, w_ref, m_ref, v_ref, g_out, d_out, m_out, v_out):
        cv = c_ref[...]
        g = _tn((cv * jax.nn.sigmoid(cv)).astype(BF16), dm_ref[...].astype(BF16))
        g_out[...] = g
        d_out[...], m_out[...], v_out[...] = _adamw_math(w_ref[...], g, m_ref[...], v_ref[...])

    spec = pl.BlockSpec((block_rows, cols), lambda i: (i, 0))
    shape = jax.ShapeDtypeStruct((rows, cols), F32)
    return pl.pallas_call(
        body, grid=(rows // block_rows,), name="adamw_w_ada",
        in_specs=[pl.BlockSpec((16, block_rows), lambda i: (0, i)), _full(dmod16.shape), spec, spec, spec],
        out_specs=[spec] * 4, out_shape=[shape] * 4, compiler_params=_params(),
    )(c_all16, dmod16, w, m, v)


SMALL_REPLICATED = ["b_ada", "norm1_gain", "conv_dw_b", "conv_ln_g", "conv_ln_b", "gm_ln_g", "gm_ln_b", "gm_ws", "gm_bs",
                    "mix_out_gain", "norm2_gain", "ffn_dw_b", "final_gain"]
SMALL_SHARDED = ["conv_dw_w", "ffn_dw_w"]
PACK_ROWS = 256
WEIGHT_ORDER = ["w_ada", "b_ada", "norm1_gain", "w_in", "conv_dw_w", "conv_dw_b", "conv_ln_g", "conv_ln_b", "gm_ln_g",
                "gm_ln_b", "gm_ws", "gm_bs", "mix_out_gain", "w_out", "norm2_gain", "w_up", "ffn_dw_w", "ffn_dw_b",
                "w_down", "final_gain"]


def _pack(parts, rows):
    flat = jnp.concatenate([a.reshape(-1) for a in parts])
    return jnp.pad(flat, (0, rows * D_MODEL - flat.shape[0])).reshape(rows, D_MODEL)


def _unpack(packed, shapes):
    flat = packed.reshape(-1)
    out, pos = [], 0
    for s in shapes:
        size = 1
        for d in s:
            size *= d
        out.append(flat[pos:pos + size].reshape(s))
        pos += size
    return out


def kernel(x, c, w_ada, b_ada, norm1_gain, w_in, conv_dw_w, conv_dw_b, conv_ln_g, conv_ln_b, gm_ln_g, gm_ln_b, gm_ws, gm_bs, mix_out_gain, w_out, norm2_gain, w_up, ffn_dw_w, ffn_dw_b, w_down, final_gain, loss_target, m_w_ada, m_b_ada, m_norm1_gain, m_w_in, m_conv_dw_w, m_conv_dw_b, m_conv_ln_g, m_conv_ln_b, m_gm_ln_g, m_gm_ln_b, m_gm_ws, m_gm_bs, m_mix_out_gain, m_w_out, m_norm2_gain, m_w_up, m_ffn_dw_w, m_ffn_dw_b, m_w_down, m_final_gain, v_w_ada, v_b_ada, v_norm1_gain, v_w_in, v_conv_dw_w, v_conv_dw_b, v_conv_ln_g, v_conv_ln_b, v_gm_ln_g, v_gm_ln_b, v_gm_ws, v_gm_bs, v_mix_out_gain, v_w_out, v_norm2_gain, v_w_up, v_ffn_dw_w, v_ffn_dw_b, v_w_down, v_final_gain):
    weights = dict(w_ada=w_ada, b_ada=b_ada, norm1_gain=norm1_gain, w_in=w_in, conv_dw_w=conv_dw_w, conv_dw_b=conv_dw_b,
                   conv_ln_g=conv_ln_g, conv_ln_b=conv_ln_b, gm_ln_g=gm_ln_g, gm_ln_b=gm_ln_b, gm_ws=gm_ws, gm_bs=gm_bs,
                   mix_out_gain=mix_out_gain, w_out=w_out, norm2_gain=norm2_gain, w_up=w_up, ffn_dw_w=ffn_dw_w,
                   ffn_dw_b=ffn_dw_b, w_down=w_down, final_gain=final_gain)
    mom1 = dict(w_ada=m_w_ada, b_ada=m_b_ada, norm1_gain=m_norm1_gain, w_in=m_w_in, conv_dw_w=m_conv_dw_w,
                conv_dw_b=m_conv_dw_b, conv_ln_g=m_conv_ln_g, conv_ln_b=m_conv_ln_b, gm_ln_g=m_gm_ln_g, gm_ln_b=m_gm_ln_b,
                gm_ws=m_gm_ws, gm_bs=m_gm_bs, mix_out_gain=m_mix_out_gain, w_out=m_w_out, norm2_gain=m_norm2_gain,
                w_up=m_w_up, ffn_dw_w=m_ffn_dw_w, ffn_dw_b=m_ffn_dw_b, w_down=m_w_down, final_gain=m_final_gain)
    mom2 = dict(w_ada=v_w_ada, b_ada=v_b_ada, norm1_gain=v_norm1_gain, w_in=v_w_in, conv_dw_w=v_conv_dw_w,
                conv_dw_b=v_conv_dw_b, conv_ln_g=v_conv_ln_g, conv_ln_b=v_conv_ln_b, gm_ln_g=v_gm_ln_g, gm_ln_b=v_gm_ln_b,
                gm_ws=v_gm_ws, gm_bs=v_gm_bs, mix_out_gain=v_mix_out_gain, w_out=v_w_out, norm2_gain=v_norm2_gain,
                w_up=v_w_up, ffn_dw_w=v_ffn_dw_w, ffn_dw_b=v_ffn_dw_b, w_down=v_w_down, final_gain=v_final_gain)
    shard = 2 * lax.axis_index("x") + lax.axis_index("y")
    me = 2 * shard + lax.axis_index("c")

    ada_cols = w_ada.shape[2]
    b_ada_sh = lax.dynamic_slice(b_ada, (0, shard * ada_cols), (1, ada_cols))
    (w_in_g, w_out_g, w_up_part, w_down_part), (conv_w_g, ffn_w_g), c_all64, mod32 = _gather_weights(
        [w_in[0], w_out[0], w_up[0], w_down[0]], [conv_dw_w[0], ffn_dw_w[0]], 2, c, w_ada[0], b_ada_sh)
    c_all = c_all64[::8]
    mod = mod32[::8].reshape(1, N_SHARD * ada_cols)
    conv_w_full = jnp.transpose(conv_w_g, (1, 0, 2)).reshape(CONV_K, D_HALF)
    ffn_w_full = jnp.transpose(ffn_w_g, (1, 0, 2)).reshape(FFN_K, 2 * D_FF)

    p = dict(norm1_gain=norm1_gain, conv_dw_w=conv_w_full, conv_dw_b=conv_dw_b, conv_ln_g=conv_ln_g,
             conv_ln_b=conv_ln_b, gm_ln_g=gm_ln_g, gm_ln_b=gm_ln_b, gm_ws=gm_ws[0], gm_bs=gm_bs[0],
             mix_out_gain=mix_out_gain, norm2_gain=norm2_gain, ffn_dw_w=ffn_w_full, ffn_dw_b=ffn_dw_b,
             final_gain=final_gain[None])
    grad_x, g, d_mod, loss, in_flight = _local_step(
        x[0], loss_target[0], mod, p, w_in_g, w_out_g.reshape(D_MODEL, D_MODEL), w_up_part, w_down_part)

    n_mod = d_mod.shape[1]
    dmod_rows = lax.dynamic_update_slice(jnp.zeros((N_DEV, n_mod), F32), d_mod, (me, 0))
    g["b_ada"] = d_mod
    small = _pack([g[k] for k in SMALL_REPLICATED] + [g[k] for k in SMALL_SHARDED] + [dmod_rows, loss[0, :1]], PACK_ROWS)
    (land_w_in, land_w_out), small = _final_comm([in_flight["w_in16"], in_flight["w_out16"]], small)
    pos = jnp.stack(_coords()).astype(jnp.int32)
    halves = _scatter_sum(pos, [g["w_in"], g["w_out"], g["w_up"], g["w_down"]],
                          [land_w_in, land_w_out, in_flight["land_w_up"], in_flight["land_w_down"]])
    full = _swap_halves(halves)
    grads = dict(w_in=full[0].reshape(w_in.shape[1:]), w_out=full[1].reshape(w_out.shape[1:]),
                 w_up=full[2].reshape(w_up.shape[1:]), w_down=full[3].reshape(w_down.shape[1:]))

    small_shapes = ([weights[k].shape for k in SMALL_REPLICATED] + [(CONV_K, D_HALF), (FFN_K, 2 * D_FF)]
                    + [(N_DEV, n_mod), (1,)])
    *small_grads, conv_w_grad, ffn_w_grad, dmod_all, loss_sum = _unpack(small, small_shapes)
    grads.update(zip(SMALL_REPLICATED, small_grads))
    grads["conv_dw_w"] = lax.dynamic_slice(conv_w_grad, (0, shard * conv_dw_w.shape[2]), conv_dw_w.shape[1:])[None]
    grads["ffn_dw_w"] = lax.dynamic_slice(ffn_w_grad, (0, shard * ffn_dw_w.shape[2]), ffn_dw_w.shape[1:])[None]

    delta, new_m, new_v = {}, {}, {}
    for name, block_rows in (("w_in", 128), ("w_out", 64), ("w_up", 128), ("w_down", 176)):
        delta[name], new_m[name], new_v[name] = [a[None] for a in _adamw(
            "adamw_" + name, weights[name][0], grads[name], mom1[name][0], mom2[name][0], block_rows)]
        grads[name] = grads[name][None]
    dmod_sh = lax.dynamic_slice(dmod_all, (0, shard * ada_cols), (N_DEV, ada_cols))
    pad8 = ((0, 16 - N_DEV), (0, 0))
    grads["w_ada"], delta["w_ada"], new_m["w_ada"], new_v["w_ada"] = [a[None] for a in _adamw_ada(
        jnp.pad(c_all, pad8), jnp.pad(dmod_sh, pad8), w_ada[0], m_w_ada[0], v_w_ada[0], 128)]
    small_names = SMALL_REPLICATED + SMALL_SHARDED

    def two_d(a):
        return a.reshape(1, -1) if a.ndim == 1 else a

    small_out = _adamw_many(*[[two_d(d[k]) for k in small_names] for d in (weights, grads, mom1, mom2)])
    for d, arrs in zip((delta, new_m, new_v), small_out):
        d.update({k: a.reshape(weights[k].shape) for k, a in zip(small_names, arrs)})

    return (loss_sum.reshape(()), grad_x[None], *[grads[k] for k in WEIGHT_ORDER], *[delta[k] for k in WEIGHT_ORDER],
            *[new_m[k] for k in WEIGHT_ORDER], *[new_v[k] for k in WEIGHT_ORDER])
```

```python
import functools

import jax
import jax.numpy as jnp
from jax import lax
from jax.experimental import pallas as pl
from jax.experimental.pallas import tpu as pltpu

F32 = jnp.float32
BF16 = jnp.bfloat16

D_MODEL = 1024
D_HALF = 512
D_FF = 2816
CONV_K = 31
FFN_K = 3
CHUNK = 128
N_HEADS = 8
HEAD_DIM = 64
N_SHARD = 4
N_DEV = 8
RMS_EPS = 1e-6
LN_EPS = 1e-5
ADAM_LR, ADAM_B1, ADAM_B2, ADAM_EPS, ADAM_WD, ADAM_STEP = 0.001, 0.9, 0.999, 1e-08, 0.01, 10

TILE = 256
HALO = 32
FFN_HALO = 16
FFN_BLK = 256
UP_SHARD = 2 * D_FF // N_SHARD
VMEM_LIMIT_BYTES = 56 * 1024 * 1024
FFN_VMEM_LIMIT_BYTES = 58 * 1024 * 1024

ANY = pl.BlockSpec(memory_space=pl.ANY)
NT_DIMS = (((1,), (1,)), ((), ()))
TN_DIMS = (((0,), (0,)), ((), ()))


def _full(shape):
    return pl.BlockSpec(shape, lambda i: (0,) * len(shape))


def _nn(a, b):
    return jnp.dot(a, b, preferred_element_type=F32)


def _nt(a, b):
    return lax.dot_general(a, b, NT_DIMS, preferred_element_type=F32)


def _tn(a, b):
    return lax.dot_general(a, b, TN_DIMS, preferred_element_type=F32)


def _colsum(a):
    return jnp.sum(a, axis=0, keepdims=True)


def _params(semantics=("arbitrary",)):
    return pltpu.CompilerParams(dimension_semantics=semantics, vmem_limit_bytes=VMEM_LIMIT_BYTES)


def _rms(v, gain):
    return v * lax.rsqrt(jnp.mean(v * v, axis=-1, keepdims=True) + RMS_EPS) * gain


def _layer_norm(v, gain, bias):
    mu = jnp.mean(v, axis=-1, keepdims=True)
    var = jnp.mean(jnp.square(v - mu), axis=-1, keepdims=True)
    return (v - mu) * lax.rsqrt(var + LN_EPS) * gain + bias


def _mod_norm(v, gain, scale, shift):
    return _rms(v, gain) * (1.0 + scale) + shift


def _conv_branch(a1, ln_g, ln_b, out_gain):
    a2 = _layer_norm(a1, ln_g, ln_b)
    return _rms(a2 * jax.nn.sigmoid(a2), out_gain)


def _gate_branch(gu, sp, out_gain):
    return _rms(jax.nn.gelu(gu) * sp, out_gain)


def _gv_norm(gv, ln_g, ln_b):
    return _layer_norm(jax.nn.gelu(gv), ln_g, ln_b)


def _rms_parts(v):
    r = lax.rsqrt(jnp.mean(v * v, axis=-1, keepdims=True) + RMS_EPS)
    return v * r, r


def _rms_back(dn, n, r):
    return r * (dn - n * jnp.mean(dn * n, axis=-1, keepdims=True))


def _ln_parts(v):
    mu = jnp.mean(v, axis=-1, keepdims=True)
    rs = lax.rsqrt(jnp.mean(jnp.square(v - mu), axis=-1, keepdims=True) + LN_EPS)
    return (v - mu) * rs, rs


def _ln_back(dn, n, rs):
    return rs * (dn - jnp.mean(dn, axis=-1, keepdims=True) - n * jnp.mean(dn * n, axis=-1, keepdims=True))


GELU_C = 0.7978845608028654
GELU_A = 0.044715


def _gelu_parts(v):
    v2 = v * v
    th = jnp.tanh(GELU_C * (v + GELU_A * (v2 * v)))
    cdf = 0.5 * (1.0 + th)
    return v * cdf, cdf + (0.5 * GELU_C) * v * (1.0 - th * th) * (1.0 + (3.0 * GELU_A) * v2)


def _rms_vjp(v, gain):
    n, r = _rms_parts(v)
    return n * gain, lambda dy: (_rms_back(dy * gain, n, r), _colsum(dy * n))


def _mod_norm_vjp(v, gain, scale, shift):
    n, r = _rms_parts(v)

    def back(dy):
        q = _colsum(dy * n)
        return _rms_back(dy * (gain * (1.0 + scale)), n, r), q * (1.0 + scale), q * gain, _colsum(dy)

    return n * gain * (1.0 + scale) + shift, back


def _conv_branch_vjp(a1, ln_g, ln_b, out_gain):
    n1, rs1 = _ln_parts(a1)
    a2 = n1 * ln_g + ln_b
    s = jax.nn.sigmoid(a2)
    a3 = a2 * s
    n3, r3 = _rms_parts(a3)

    def back(dy):
        da2 = _rms_back(dy * out_gain, n3, r3) * (s + a3 * (1.0 - s))
        return _ln_back(da2 * ln_g, n1, rs1), _colsum(da2 * n1), _colsum(da2), _colsum(dy * n3)

    return n3 * out_gain, back


def _gate_branch_vjp(gu, sp, out_gain):
    ge, dge = _gelu_parts(gu)
    n, r = _rms_parts(ge * sp)

    def back(dy):
        dg = _rms_back(dy * out_gain, n, r)
        return dg * sp * dge, dg * ge, _colsum(dy * n)

    return n * out_gain, back


def _gv_norm_vjp(gv, ln_g, ln_b):
    ge, dge = _gelu_parts(gv)
    n, rs = _ln_parts(ge)
    return n * ln_g + ln_b, lambda dy: (_ln_back(dy * ln_g, n, rs) * dge, _colsum(dy * n), _colsum(dy))


def _head_pair_matmul(wp_ref, v):
    lane = lax.broadcasted_iota(jnp.int32, (CHUNK, CHUNK), 1)
    rows = []
    for n in range(v.shape[0] // CHUNK):
        cols = []
        for j in range(N_HEADS // 2):
            r = _nn(wp_ref[j], v[n * CHUNK:(n + 1) * CHUNK, j * CHUNK:(j + 1) * CHUNK])
            cols.append(jnp.where(lane < HEAD_DIM, r[:CHUNK], r[CHUNK:]))
        rows.append(jnp.concatenate(cols, axis=1))
    return jnp.concatenate(rows, axis=0)


def _tile_bias(bs, tokens):
    return jnp.concatenate([bs] * (tokens // CHUNK), axis=0)


FORWARD_LEAD = 8


def _fwd_mixer(x, vec, conv_w, wpair, bs_full, w_in_g, w_out_g, late_parts):
    seq = x.shape[0]
    n_tiles = seq // TILE
    t = TILE
    n_late = len(late_parts)
    forward_step = max(n_tiles - FORWARD_LEAD, 0)
    names = ["norm1_gain", "sc1", "sh1", "gt1", "conv_dw_b", "conv_ln_g", "conv_ln_b", "gm_ln_g", "gm_ln_b",
             "mix_out_gain"]
    vecs = [vec[k] for k in names]

    def body(x_ref, g1, sc1, sh1, gt1, cb, clg, clb, vg, vb, mg, cw, wp, bs, win_hbm, wout_hbm, *rest):
        late = rest[n_late:2 * n_late]
        z_ref, a1_ref, sp_ref, y_ref, o1_ref, x2_ref = rest[2 * n_late:2 * n_late + 6]
        win_v, wout_v, halo, bank, sem, send_sems, recv_sems = rest[2 * n_late + 6:]
        i = pl.program_id(0)
        mx, my, mc = _coords()
        shard = 2 * mx + my

        def half(w, which):
            h = late[w].shape[1] // 2
            return pl.ds(pl.multiple_of(which * h, 16), h)

        def chip_of(j):
            return 2 * _flip(mx, CHIP_FLIPS[j][0]) + _flip(my, CHIP_FLIPS[j][1])

        def ici_copy(w, j, slot):
            rows = late[w].at[slot, half(w, mc)]
            return pltpu.make_async_remote_copy(
                src_ref=rows, dst_ref=rows, send_sem=send_sems.at[w, j], recv_sem=recv_sems.at[w, j],
                device_id=(_flip(mx, CHIP_FLIPS[j][0]), _flip(my, CHIP_FLIPS[j][1]), mc), device_id_type=MESH)

        def d2d_copy(w, j, which):
            rows = late[w].at[chip_of(j), half(w, which)]
            return pltpu.make_async_remote_copy(
                src_ref=rows, dst_ref=rows, send_sem=send_sems.at[w, len(CHIP_FLIPS) + j],
                recv_sem=recv_sems.at[w, len(CHIP_FLIPS) + j], device_id=(mx, my, 1 - mc), device_id_type=MESH)

        pairs = [(w, j) for w in range(n_late) for j in range(len(CHIP_FLIPS))]

        @pl.when(i == 0)
        def _():
            for w, j in pairs:
                ici_copy(w, j, shard).start()
            cps = [pltpu.make_async_copy(win_hbm, win_v, sem.at[0]),
                   pltpu.make_async_copy(wout_hbm, wout_v, sem.at[1])]
            for cp in cps:
                cp.start()
            for cp in cps:
                cp.wait()
            halo[...] = jnp.zeros_like(halo)

        @pl.when(i == forward_step)
        def _():
            for w, j in pairs:
                ici_copy(w, j, chip_of(j)).wait_recv()
                d2d_copy(w, j, mc).start()

        xv = x_ref[...]
        h1b = _mod_norm(xv, g1[...], sc1[...], sh1[...]).astype(BF16)
        zs = [_nn(h1b, win_v[k]) for k in range(N_SHARD)]
        for k in range(N_SHARD):
            z_ref[:, k * D_HALF:(k + 1) * D_HALF] = zs[k]
        ca, cg, gu, gv = zs
        a0 = ca * jax.nn.sigmoid(cg)
        ext = jnp.concatenate([halo[...], a0], axis=0)
        halo[...] = a0[t - HALO:]
        bank[0] = ext
        for b in range(1, 8):
            bank[b] = pltpu.roll(ext, b, axis=0)
        a1 = jnp.zeros((t, D_HALF), F32) + cb[...]
        for s in range(CONV_K):
            q, b = divmod(s, 8)
            a1 = a1 + bank[b, pl.ds(HALO - 8 * q, t), :] * cw[pl.ds(CONV_K - 1 - s, 1), :]
        a1_ref[...] = a1
        mgv = mg[...]
        ya = _conv_branch(a1, clg[...], clb[...], mgv[:, :D_HALF])
        gvn = _gv_norm(gv, vg[...], vb[...]).astype(BF16)
        sp = _head_pair_matmul(wp, gvn) + _tile_bias(bs[...], t)
        sp_ref[...] = sp
        yg = _gate_branch(gu, sp, mgv[:, D_HALF:])
        yb = jnp.concatenate([ya, yg], axis=1).astype(BF16)
        y_ref[...] = yb
        o1 = _nn(yb, wout_v[...])
        o1_ref[...] = o1
        x2_ref[...] = xv + gt1[...] * o1

        @pl.when(i == n_tiles - 1)
        def _():
            for w, j in pairs:
                d2d_copy(w, j, 1 - mc).wait_recv()
            for w, j in pairs:
                ici_copy(w, j, shard).wait_send()
                d2d_copy(w, j, mc).wait_send()

    def row(width):
        return pl.BlockSpec((t, width), lambda i: (i, 0))

    out_shape = [jax.ShapeDtypeStruct((seq, 4 * D_HALF), F32), jax.ShapeDtypeStruct((seq, D_HALF), F32),
                 jax.ShapeDtypeStruct((seq, D_HALF), F32), jax.ShapeDtypeStruct((seq, D_MODEL), BF16),
                 jax.ShapeDtypeStruct((seq, D_MODEL), F32), jax.ShapeDtypeStruct((seq, D_MODEL), F32)]
    n_in = 1 + len(vecs) + 3 + 2
    sem_shape = (n_late, 2 * len(CHIP_FLIPS))
    outs = pl.pallas_call(
        body, grid=(n_tiles,), name="fwd_mixer",
        in_specs=[row(D_MODEL)] + [_full(v.shape) for v in vecs]
        + [_full(conv_w.shape), _full(wpair.shape), _full(bs_full.shape), ANY, ANY] + [ANY] * n_late,
        out_specs=[ANY] * n_late + [row(4 * D_HALF), row(D_HALF), row(D_HALF), row(D_MODEL), row(D_MODEL),
                                    row(D_MODEL)],
        out_shape=[jax.ShapeDtypeStruct(a.shape, a.dtype) for a in late_parts] + out_shape,
        input_output_aliases={n_in + w: w for w in range(n_late)},
        scratch_shapes=[pltpu.VMEM(w_in_g.shape, BF16), pltpu.VMEM(w_out_g.shape, BF16),
                        pltpu.VMEM((HALO, D_HALF), F32), pltpu.VMEM((8, t + HALO, D_HALF), F32),
                        pltpu.SemaphoreType.DMA((2,)), pltpu.SemaphoreType.DMA(sem_shape),
                        pltpu.SemaphoreType.DMA(sem_shape)],
        compiler_params=_params(),
    )(x, *vecs, conv_w, wpair, bs_full, w_in_g, w_out_g, *late_parts)
    return outs[n_late:], outs[:n_late]


def _interleave_matrices():
    row = jnp.arange(TILE)
    token_of_row = (row % 8) * (TILE // 8) + row // 8
    to_inter = (token_of_row[:, None] == row[None, :]).astype(BF16)
    return to_inter, jnp.transpose(to_inter)


def _ffn(x2, target, norm2_gain, sc2, sh2, ffn_w, ffn_b, gt2, final_gain, w_up_g, w_down_g, to_inter, to_natural):
    seq = x2.shape[0]
    n_tiles = seq // TILE
    t = TILE
    n_blk = D_FF // FFN_BLK
    inv_d = 1.0 / D_MODEL

    def body(x2_ref, x2h_ref, tgt_ref, g2, sc2_ref, sh2_ref, fw, fb, gt2_ref, fg, pm_ref, pmt_ref, wup_hbm, wd_hbm,
             du_ref, dx2_ref, dfw_ref, dfb_ref, dfg_ref, dgt2_ref, dg2_ref, dsc2_ref, dsh2_ref, loss_ref, dwd_hbm,
             dwd16_hbm, wup_v, wd_v, dwd_acc, carry, u_s, sil_s, vds_s, f_s, du_s, sem):
        i = pl.program_id(0)
        tile = n_tiles - 1 - i
        sublane = lax.broadcasted_iota(jnp.int32, (8, FFN_BLK), 0)

        @pl.when(i == 0)
        def _():
            cps = [pltpu.make_async_copy(wd_hbm, wd_v, sem.at[0])]
            cps += [pltpu.make_async_copy(wup_hbm.at[k], wup_v.at[:, pl.ds(k * UP_SHARD, UP_SHARD)], sem.at[3 + k])
                    for k in range(N_SHARD)]
            for cp in cps:
                cp.start()
            for cp in cps:
                cp.wait()
            dwd_acc[...] = jnp.zeros_like(dwd_acc)
            carry[...] = jnp.zeros_like(carry)
            dfw_ref[...] = jnp.zeros_like(dfw_ref)
            dfb_ref[...] = jnp.zeros_like(dfb_ref)
            dfg_ref[...] = jnp.zeros_like(dfg_ref)
            dgt2_ref[...] = jnp.zeros_like(dgt2_ref)
            dg2_ref[...] = jnp.zeros_like(dg2_ref)
            dsc2_ref[...] = jnp.zeros_like(dsc2_ref)
            dsh2_ref[...] = jnp.zeros_like(dsh2_ref)
            loss_ref[...] = jnp.zeros_like(loss_ref)

        def cols_of(j):
            return pl.ds(j * FFN_BLK, FFN_BLK), pl.ds(D_FF + j * FFN_BLK, FFN_BLK)

        def wrap_down(last, before):
            return jnp.where(sublane == 0, pltpu.roll(before, 1, axis=0), pltpu.roll(last, 1, axis=0))

        def wrap_up(first, after):
            return jnp.where(sublane == 7, pltpu.roll(after, 7, axis=0), pltpu.roll(first, 7, axis=0))

        x2v = x2_ref[...]
        h2, h2_vjp = _mod_norm_vjp(x2v, g2[...], sc2_ref[...], sh2_ref[...])
        h2b = h2.astype(BF16)
        h2_before = _mod_norm(x2h_ref[...], g2[...], sc2_ref[...], sh2_ref[...]).astype(BF16)
        lhs = jnp.concatenate([_nn(pm_ref[...], h2b).astype(BF16), h2_before], axis=0)

        def up(j):
            cv, cg = cols_of(j)
            return _nn(lhs, wup_v[:, cv]), _nn(lhs, wup_v[:, cg])

        def conv(both, cols):
            cur = both[:t]
            u_s[:, cols] = cur.astype(BF16)
            before = jnp.where(tile > 0, both[t:], 0.0)
            w1 = wrap_down(cur[t - 8:], before)
            w2 = wrap_down(cur[t - 16:t - 8], pltpu.roll(before, 1, axis=0))
            back1 = jnp.concatenate([w1, cur[:t - 8]], axis=0)
            back2 = jnp.concatenate([w2, w1, cur[:t - 16]], axis=0)
            return (fb[:, cols] + cur * fw[pl.ds(2, 1), cols] + back1 * fw[pl.ds(1, 1), cols]
                    + back2 * fw[pl.ds(0, 1), cols])

        pm_t = pmt_ref[...]

        def to_natural_f32(a):
            hi = a.astype(BF16)
            rest = a - hi.astype(F32)
            mid = rest.astype(BF16)
            low = (rest - mid.astype(F32)).astype(BF16)
            return _nn(jnp.concatenate([pm_t, pm_t, pm_t], axis=1), jnp.concatenate([hi, mid, low], axis=0))

        o2 = jnp.zeros((t, D_MODEL), F32)
        ahead_uv = up(0)
        for j in range(n_blk):
            cv, cg = cols_of(j)
            both_v, both_g = ahead_uv
            if j + 1 < n_blk:
                ahead_uv = up(j + 1)
            val, gate = conv(both_v, cv), conv(both_g, cg)
            sig = jax.nn.sigmoid(gate)
            sil = gate * sig
            fb16 = (sil * val).astype(BF16)
            sil_s[:, cv] = sil
            vds_s[:, cv] = val * (sig + sil * (1.0 - sig))
            f_s[:, cv] = fb16
            o2 = o2 + _nn(fb16, wd_v[pl.ds(j * FFN_BLK, FFN_BLK), :])
        o2 = to_natural_f32(o2)

        gt2v = gt2_ref[...]
        x3 = x2v + gt2v * o2
        out, out_vjp = _rms_vjp(x3, fg[...])
        diff = out - tgt_ref[...]
        loss_ref[...] += jnp.zeros_like(loss_ref) + 0.5 * inv_d * jnp.sum(diff * diff)
        dx3, dfg = out_vjp(diff * inv_d)
        dfg_ref[...] += dfg
        dgt2_ref[...] += _colsum(dx3 * o2)
        do2b = _nn(pm_ref[...], (gt2v * dx3).astype(BF16)).astype(BF16)

        def conv_back(dd, cols):
            dfb_ref[:, cols] += _colsum(dd)
            nxt = carry[:, cols]
            w1 = wrap_up(dd[:8], nxt[:8])
            w2 = wrap_up(dd[8:16], nxt[8:])
            ahead = (dd, jnp.concatenate([dd[8:], w1], axis=0), jnp.concatenate([dd[16:], w1, w2], axis=0))
            carry[:, cols] = dd[:16]
            uv = u_s[:, cols].astype(F32)
            du = jnp.zeros((t, FFN_BLK), F32)
            for s in range(FFN_K):
                du = du + ahead[s] * fw[pl.ds(FFN_K - 1 - s, 1), cols]
                dfw_ref[pl.ds(FFN_K - 1 - s, 1), cols] += _colsum(ahead[s] * uv)
            du_s[:, cols] = du.astype(BF16)

        for j in range(n_blk):
            cv, cg = cols_of(j)
            rows = pl.ds(j * FFN_BLK, FFN_BLK)
            df = _nt(do2b, wd_v[rows, :])
            dwd_acc[rows, :] += _tn(f_s[:, cv], do2b)
            conv_back(df * sil_s[:, cv], cv)
            conv_back(df * vds_s[:, cv], cg)
        du16 = _nn(pm_t, du_s[...]).astype(BF16)
        du_ref[...] = du16
        dx2, dg2, dsc2, dsh2 = h2_vjp(_nt(du16, wup_v[...]))
        dx2_ref[...] = dx3 + dx2
        dg2_ref[...] += dg2
        dsc2_ref[...] += dsc2
        dsh2_ref[...] += dsh2

        @pl.when(i == n_tiles - 1)
        def _():
            cp = pltpu.make_async_copy(dwd_acc, dwd_hbm, sem.at[1])
            cp.start()
            wd_v[...] = dwd_acc[...].astype(BF16)
            cp16 = pltpu.make_async_copy(wd_v, dwd16_hbm, sem.at[2])
            cp16.start()
            cp.wait()
            cp16.wait()

    def rev(width):
        return pl.BlockSpec((t, width), lambda i: (n_tiles - 1 - i, 0))

    assert FFN_K == 3
    halo_spec = pl.BlockSpec((8, D_MODEL), lambda i: (jnp.maximum((n_tiles - 1 - i) * (t // 8) - 1, 0), 0))
    vec_spec = _full((1, D_MODEL))
    out_shape = [jax.ShapeDtypeStruct((seq, 2 * D_FF), BF16), jax.ShapeDtypeStruct((seq, D_MODEL), F32),
                 jax.ShapeDtypeStruct((FFN_K, 2 * D_FF), F32), jax.ShapeDtypeStruct((1, 2 * D_FF), F32),
                 jax.ShapeDtypeStruct((1, D_MODEL), F32), jax.ShapeDtypeStruct((1, D_MODEL), F32),
                 jax.ShapeDtypeStruct((1, D_MODEL), F32), jax.ShapeDtypeStruct((1, D_MODEL), F32),
                 jax.ShapeDtypeStruct((1, D_MODEL), F32),
                 jax.ShapeDtypeStruct((1, 128), F32), jax.ShapeDtypeStruct((D_FF, D_MODEL), F32),
                 jax.ShapeDtypeStruct((D_FF, D_MODEL), BF16)]
    return pl.pallas_call(
        body, grid=(n_tiles,), name="ffn",
        in_specs=[rev(D_MODEL), halo_spec, rev(D_MODEL), vec_spec, vec_spec, vec_spec, _full(ffn_w.shape),
                  _full(ffn_b.shape), _full(gt2.shape), _full(final_gain.shape), _full(to_inter.shape),
                  _full(to_natural.shape), ANY, ANY],
        out_specs=[rev(2 * D_FF), rev(D_MODEL), _full((FFN_K, 2 * D_FF)), _full((1, 2 * D_FF)), vec_spec, vec_spec,
                   vec_spec, vec_spec, vec_spec, _full((1, 128)), ANY, ANY],
        out_shape=out_shape,
        scratch_shapes=[pltpu.VMEM((D_MODEL, 2 * D_FF), BF16), pltpu.VMEM((D_FF, D_MODEL), BF16),
                        pltpu.VMEM((D_FF, D_MODEL), F32), pltpu.VMEM((FFN_HALO, 2 * D_FF), F32),
                        pltpu.VMEM((t, 2 * D_FF), BF16), pltpu.VMEM((t, D_FF), F32), pltpu.VMEM((t, D_FF), F32),
                        pltpu.VMEM((t, D_FF), BF16), pltpu.VMEM((t, 2 * D_FF), BF16),
                        pltpu.SemaphoreType.DMA((3 + N_SHARD,))],
        compiler_params=pltpu.CompilerParams(dimension_semantics=("arbitrary",), vmem_limit_bytes=FFN_VMEM_LIMIT_BYTES),
    )(x2, x2, target, norm2_gain, sc2, sh2, ffn_w, ffn_b, gt2, final_gain, to_inter, to_natural, w_up_g, w_down_g)


def _scatter_copies(src16, land, send_sems, recv_sems):
    x, y, c = _coords()
    h = src16.shape[1] // 2
    copies = []
    for f, flip in enumerate(PEER_FLIPS):
        tx, ty, tc = _flip(x, flip[0]), _flip(y, flip[1]), _flip(c, flip[2])
        copies.append(pltpu.make_async_remote_copy(
            src_ref=src16.at[2 * tx + ty, pl.ds(pl.multiple_of(tc * h, 16), h)], dst_ref=land.at[f],
            send_sem=send_sems.at[f], recv_sem=recv_sems.at[f], device_id=(tx, ty, tc), device_id_type=MESH))
    return copies


def _land_shape(src16):
    return jax.ShapeDtypeStruct((len(PEER_FLIPS), src16.shape[1] // 2, src16.shape[2]), BF16)


UP_TILE = 512


def _bwd_up(du, x2, norm2_gain, sc2, sh2, dwd16):
    seq = x2.shape[0]
    t = UP_TILE if seq % UP_TILE == 0 else TILE
    n_tiles = seq // t
    acc_shape = (N_SHARD, D_MODEL, UP_SHARD)

    def body(du_ref, x2_ref, g2, sc2_ref, sh2_ref, dwd16_hbm, dwup_hbm, dwup16_hbm, land_hbm,
             stage16, dwup_acc, sem, send_sems, recv_sems):
        i = pl.program_id(0)

        @pl.when(i == 0)
        def _():
            for cp in _scatter_copies(dwd16_hbm, land_hbm, send_sems, recv_sems):
                cp.start()
            dwup_acc[...] = jnp.zeros_like(dwup_acc)

        h2b = _mod_norm(x2_ref[...], g2[...], sc2_ref[...], sh2_ref[...]).astype(BF16)
        for k in range(N_SHARD):
            dwup_acc[k] += _tn(h2b, du_ref[:, k * UP_SHARD:(k + 1) * UP_SHARD])

        @pl.when(i == n_tiles - 1)
        def _():
            cp = pltpu.make_async_copy(dwup_acc, dwup_hbm, sem.at[0])
            cp.start()
            for k in range(N_SHARD):
                stage16[k] = dwup_acc[k].astype(BF16)
            cp16 = pltpu.make_async_copy(stage16, dwup16_hbm, sem.at[1])
            cp16.start()
            cp.wait()
            cp16.wait()
            for rc in _scatter_copies(dwd16_hbm, land_hbm, send_sems, recv_sems):
                rc.wait()

    def row(width):
        return pl.BlockSpec((t, width), lambda i: (i, 0))

    n_peer = len(PEER_FLIPS)
    return pl.pallas_call(
        body, grid=(n_tiles,), name="bwd_up",
        in_specs=[row(2 * D_FF), row(D_MODEL), _full((1, D_MODEL)), _full((1, D_MODEL)), _full((1, D_MODEL)), ANY],
        out_specs=[ANY, ANY, ANY],
        out_shape=[jax.ShapeDtypeStruct(acc_shape, F32), jax.ShapeDtypeStruct(acc_shape, BF16), _land_shape(dwd16)],
        scratch_shapes=[pltpu.VMEM(acc_shape, BF16), pltpu.VMEM(acc_shape, F32), pltpu.SemaphoreType.DMA((2,)),
                        pltpu.SemaphoreType.DMA((n_peer,)), pltpu.SemaphoreType.DMA((n_peer,))],
        compiler_params=_params(),
    )(du, x2, norm2_gain, sc2, sh2, dwd16)


def _bwd_mixer(dx2, x, z, a1, sp, yb, o1, vec, conv_w, wpair, wpair_t, causal_mask, w_in_g, w_out_g, dwup16):
    seq = x.shape[0]
    n_tiles = seq // TILE
    t = TILE
    names = ["norm1_gain", "sc1", "sh1", "gt1", "conv_ln_g", "conv_ln_b", "gm_ln_g", "gm_ln_b", "mix_out_gain"]
    vecs = [vec[k] for k in names]

    def body(dx2_ref, x_ref, z_ref, a1_ref, sp_ref, y_ref, o1_ref, g1, sc1, sh1, gt1, clg, clb, vg, vb, mg,
             cw, wp, wpt, mask_ref, win_hbm, wout_hbm, dwup16_hbm,
             gx_ref, dg1_ref, dsc1_ref, dsh1_ref, dgt1_ref, dcw_ref, dcb_ref, dclg_ref, dclb_ref, dvg_ref, dvb_ref,
             dmg_ref, dws_ref, dbs_ref, dwin_hbm, dwout_hbm, land_hbm, dwin16_hbm, dwout16_hbm,
             win_v, wout_v, dwin_acc, dwout_acc, carry, bank, dbs_acc, lwin, lwout, sem, send_sems, recv_sems,
             pair_send, pair_recv):
        i = pl.program_id(0)
        small = [dg1_ref, dsc1_ref, dsh1_ref, dgt1_ref, dcw_ref, dcb_ref, dclg_ref, dclb_ref, dvg_ref, dvb_ref,
                 dmg_ref, dws_ref, dbs_acc]

        @pl.when(i == 0)
        def _():
            for cp in _scatter_copies(dwup16_hbm, land_hbm, send_sems, recv_sems):
                cp.start()
            cps = [pltpu.make_async_copy(win_hbm, win_v, sem.at[0]),
                   pltpu.make_async_copy(wout_hbm, wout_v, sem.at[1])]
            for cp in cps:
                cp.start()
            for cp in cps:
                cp.wait()
            dwin_acc[...] = jnp.zeros_like(dwin_acc)
            dwout_acc[...] = jnp.zeros_like(dwout_acc)
            carry[...] = jnp.zeros_like(carry)
            for ref in small:
                ref[...] = jnp.zeros_like(ref)

        dx2v = dx2_ref[...]
        gt1v = gt1[...]
        dgt1_ref[...] += _colsum(dx2v * o1_ref[...])
        do1b = (gt1v * dx2v).astype(BF16)
        dy = _nt(do1b, wout_v[...])
        dwout_acc[...] += _tn(y_ref[...], do1b)

        mgv = mg[...]
        _, conv_vjp = _conv_branch_vjp(a1_ref[...], clg[...], clb[...], mgv[:, :D_HALF])
        da1, dclg, dclb, dmg_a = conv_vjp(dy[:, :D_HALF])
        dclg_ref[...] += dclg
        dclb_ref[...] += dclb
        gu = z_ref[:, 2 * D_HALF:3 * D_HALF]
        gv = z_ref[:, 3 * D_HALF:]
        spv = sp_ref[...]
        _, gate_vjp = _gate_branch_vjp(gu, spv, mgv[:, D_HALF:])
        dgu, dsp, dmg_g = gate_vjp(dy[:, D_HALF:])
        dmg_ref[...] += jnp.concatenate([dmg_a, dmg_g], axis=1)
        gvn, gv_vjp = _gv_norm_vjp(gv, vg[...], vb[...])
        gvnb = gvn.astype(BF16)
        dspb = dsp.astype(BF16)
        dgvn = _head_pair_matmul(wpt, dspb)
        dgv, dvg, dvb = gv_vjp(dgvn)
        dvg_ref[...] += dvg
        dvb_ref[...] += dvb
        lane = lax.broadcasted_iota(jnp.int32, (CHUNK, CHUNK), 1)
        dbs = jnp.zeros((CHUNK, D_HALF), F32)
        for n in range(t // CHUNK):
            rows = slice(n * CHUNK, (n + 1) * CHUNK)
            dbs = dbs + dsp[rows, :]
            for j in range(N_HEADS // 2):
                cols = slice(j * CHUNK, (j + 1) * CHUNK)
                blk = dspb[rows, cols]
                zero = jnp.zeros_like(blk)
                vblk = gvnb[rows, cols]
                dws_ref[2 * j] += _nt(jnp.where(lane < HEAD_DIM, blk, zero), vblk)
                dws_ref[2 * j + 1] += _nt(jnp.where(lane < HEAD_DIM, zero, blk), vblk)
        dbs_acc[...] += dbs

        h1, h1_vjp = _mod_norm_vjp(x_ref[...], g1[...], sc1[...], sh1[...])
        h1b = h1.astype(BF16)
        dh1 = jnp.zeros((t, D_MODEL), F32)
        for k, dzk in ((2, dgu), (3, dgv)):
            dzb = dzk.astype(BF16)
            dh1 = dh1 + _nt(dzb, win_v[k])
            dwin_acc[k] += _tn(h1b, dzb)

        ca = z_ref[:, :D_HALF]
        cg = z_ref[:, D_HALF:2 * D_HALF]
        sig = jax.nn.sigmoid(cg)
        a0 = ca * sig
        ext = jnp.concatenate([da1, carry[...]], axis=0)
        carry[...] = da1[:HALO]
        bank[0] = ext
        for b in range(1, 8):
            bank[b] = pltpu.roll(ext, t + HALO - b, axis=0)
        dcb_ref[...] += _colsum(da1)
        da0 = jnp.zeros((t, D_HALF), F32)
        for s in range(CONV_K):
            q, b = divmod(s, 8)
            shifted = bank[b, pl.ds(8 * q, t), :]
            da0 = da0 + shifted * cw[pl.ds(CONV_K - 1 - s, 1), :]
            dcw_ref[pl.ds(CONV_K - 1 - s, 1), :] += _colsum(shifted * a0)
        dca = da0 * sig
        dcg = da0 * ca * sig * (1.0 - sig)

        for k, dzk in ((0, dca), (1, dcg)):
            dzb = dzk.astype(BF16)
            dh1 = dh1 + _nt(dzb, win_v[k])
            dwin_acc[k] += _tn(h1b, dzb)
        dx, dg1, dsc1, dsh1 = h1_vjp(dh1)
        gx_ref[...] = dx2v + dx
        dg1_ref[...] += dg1
        dsc1_ref[...] += dsc1
        dsh1_ref[...] += dsh1

        @pl.when(i == n_tiles - 1)
        def _():
            for h in range(N_HEADS):
                dws_ref[h] = dws_ref[h] * mask_ref[...]
            head_of_lane = lax.broadcasted_iota(jnp.int32, (N_HEADS, D_HALF), 1) // HEAD_DIM
            pick = (head_of_lane == lax.broadcasted_iota(jnp.int32, (N_HEADS, D_HALF), 0)).astype(F32)
            dbs_ref[...] = lax.dot_general(pick, dbs_acc[...], NT_DIMS, precision=lax.Precision.HIGHEST,
                                           preferred_element_type=F32)
            for k in range(N_SHARD):
                win_v[k] = dwin_acc[k].astype(BF16)
            wout_v[...] = dwout_acc[...].astype(BF16)
            mx, my, mc = _coords()
            h_in, h_out = dwin_acc.shape[1] // 2, dwout_acc.shape[0] // (2 * N_SHARD)

            def in_rows(ref, k, which):
                return ref.at[k, pl.ds(pl.multiple_of(which * h_in, 16), h_in), :]

            def out_rows(ref, k, which):
                return ref.at[pl.ds(pl.multiple_of((2 * k + which) * h_out, 16), h_out), :]

            pairs = ((win_v, dwin_acc, lwin, in_rows, dwin_hbm, dwin16_hbm),
                     (wout_v, dwout_acc, lwout, out_rows, dwout_hbm, dwout16_hbm))
            swaps = [pltpu.make_async_remote_copy(
                src_ref=rows_of(v16, k, 1 - mc), dst_ref=land.at[k], send_sem=pair_send.at[w, k],
                recv_sem=pair_recv.at[w, k], device_id=(mx, my, 1 - mc), device_id_type=MESH)
                for w, (v16, _, land, rows_of, _, _) in enumerate(pairs) for k in range(N_SHARD)]
            for cp in swaps:
                cp.start()
            for cp in swaps:
                cp.wait()
            outs = []
            for w, (v16, acc, land, rows_of, half_hbm, half16_hbm) in enumerate(pairs):
                for k in range(N_SHARD):
                    total = rows_of(acc, k, mc)[...] + land[k].astype(F32)
                    rows_of(acc, k, 0)[...] = total
                    rows_of(v16, k, 0)[...] = total.astype(BF16)
                    outs.append(pltpu.make_async_copy(rows_of(acc, k, 0), half_hbm.at[k], sem.at[2 + 8 * w + k]))
                    outs.append(pltpu.make_async_copy(rows_of(v16, k, 0), half16_hbm.at[k], sem.at[6 + 8 * w + k]))
            for cp in outs:
                cp.start()
            for cp in outs:
                cp.wait()
            for rc in _scatter_copies(dwup16_hbm, land_hbm, send_sems, recv_sems):
                rc.wait()

    def rev(width):
        return pl.BlockSpec((t, width), lambda i: (n_tiles - 1 - i, 0))

    v1024 = jax.ShapeDtypeStruct((1, D_MODEL), F32)
    v512 = jax.ShapeDtypeStruct((1, D_HALF), F32)
    small_shapes = [v1024, v1024, v1024, v1024, jax.ShapeDtypeStruct((CONV_K, D_HALF), F32), v512, v512, v512, v512,
                    v512, v1024, jax.ShapeDtypeStruct((N_HEADS, CHUNK, CHUNK), F32),
                    jax.ShapeDtypeStruct((N_HEADS, CHUNK), F32)]
    n_peer = len(PEER_FLIPS)
    half_in = (N_SHARD, w_in_g.shape[1] // 2, w_in_g.shape[2])
    half_out = (N_SHARD, w_out_g.shape[0] // (2 * N_SHARD), w_out_g.shape[1])
    return pl.pallas_call(
        body, grid=(n_tiles,), name="bwd_mixer",
        in_specs=[rev(D_MODEL), rev(D_MODEL), rev(4 * D_HALF), rev(D_HALF), rev(D_HALF), rev(D_MODEL),
                  rev(D_MODEL)] + [_full(v.shape) for v in vecs]
        + [_full(conv_w.shape), _full(wpair.shape), _full(wpair_t.shape), _full(causal_mask.shape), ANY, ANY, ANY],
        out_specs=[rev(D_MODEL)] + [_full(s.shape) for s in small_shapes] + [ANY] * 5,
        out_shape=[jax.ShapeDtypeStruct((seq, D_MODEL), F32)] + small_shapes
        + [jax.ShapeDtypeStruct(half_in, F32), jax.ShapeDtypeStruct(half_out, F32), _land_shape(dwup16),
           jax.ShapeDtypeStruct(half_in, BF16), jax.ShapeDtypeStruct(half_out, BF16)],
        scratch_shapes=[pltpu.VMEM(w_in_g.shape, BF16), pltpu.VMEM(w_out_g.shape, BF16),
                        pltpu.VMEM(w_in_g.shape, F32), pltpu.VMEM(w_out_g.shape, F32),
                        pltpu.VMEM((HALO, D_HALF), F32), pltpu.VMEM((8, t + HALO, D_HALF), F32),
                        pltpu.VMEM((CHUNK, D_HALF), F32), pltpu.VMEM(half_in, BF16), pltpu.VMEM(half_out, BF16),
                        pltpu.SemaphoreType.DMA((2 + 4 * N_SHARD,)),
                        pltpu.SemaphoreType.DMA((n_peer,)), pltpu.SemaphoreType.DMA((n_peer,)),
                        pltpu.SemaphoreType.DMA((2, N_SHARD)), pltpu.SemaphoreType.DMA((2, N_SHARD))],
        compiler_params=_params(),
    )(dx2, x, z, a1, sp, yb, o1, *vecs, conv_w, wpair, wpair_t, causal_mask, w_in_g, w_out_g, dwup16)


def _gmlp_operands(gm_ws, gm_bs):
    mask = jnp.tril(jnp.ones((CHUNK, CHUNK), F32))
    ws = gm_ws * mask[None]
    wpair = ws.reshape(N_HEADS // 2, 2 * CHUNK, CHUNK).astype(BF16)
    wpair_t = jnp.swapaxes(ws, 1, 2).reshape(N_HEADS // 2, 2 * CHUNK, CHUNK).astype(BF16)
    bs_full = jnp.repeat(jnp.transpose(gm_bs), HEAD_DIM, axis=1)
    return wpair, wpair_t, bs_full, mask


def _local_step(x, target, mod, p, w_in_g, w_out_g, w_up_part, w_down_part):
    sh1, sc1, gt1, sh2, sc2, gt2 = [mod[:, k * D_MODEL:(k + 1) * D_MODEL] for k in range(6)]
    vec = dict(p, sh1=sh1, sc1=sc1, gt1=gt1, sh2=sh2, sc2=sc2, gt2=gt2)
    wpair, wpair_t, bs_full, mask = _gmlp_operands(p["gm_ws"], p["gm_bs"])

    (z, a1, sp, yb, o1, x2), (w_up_g, w_down_g) = _fwd_mixer(
        x, vec, p["conv_dw_w"], wpair, bs_full, w_in_g, w_out_g, [w_up_part, w_down_part])
    w_down_g = w_down_g.reshape(D_FF, D_MODEL)
    to_inter, to_natural = _interleave_matrices()
    du, dx2, d_ffn_w, d_ffn_b, d_fg, d_gt2, d_g2, d_sc2, d_sh2, loss, d_wd, d_wd16 = _ffn(
        x2, target, p["norm2_gain"], sc2, sh2, p["ffn_dw_w"], p["ffn_dw_b"], gt2, p["final_gain"], w_up_g, w_down_g,
        to_inter, to_natural)
    by_shard = (N_SHARD, -1, D_MODEL)
    d_wup, d_wup16, land_wd = _bwd_up(du, x2, p["norm2_gain"], sc2, sh2, d_wd16.reshape(by_shard))
    (gx, d_g1, d_sc1, d_sh1, d_gt1, d_cw, d_cb, d_clg, d_clb, d_vg, d_vb, d_mg, d_ws, d_bs, d_win, d_wout, land_wup,
     d_win16, d_wout16) = _bwd_mixer(dx2, x, z, a1, sp, yb, o1, vec, p["conv_dw_w"], wpair, wpair_t, mask, w_in_g,
                                     w_out_g, d_wup16)
    d_mod = _pack([d_sh1, d_sc1, d_gt1, d_sh2, d_sc2, d_gt2], 6).reshape(1, 6 * D_MODEL)
    grads = dict(norm1_gain=d_g1, conv_dw_w=d_cw, conv_dw_b=d_cb, conv_ln_g=d_clg, conv_ln_b=d_clb, gm_ln_g=d_vg,
                 gm_ln_b=d_vb, gm_ws=d_ws, gm_bs=d_bs, mix_out_gain=d_mg, norm2_gain=d_g2, ffn_dw_w=d_ffn_w,
                 ffn_dw_b=d_ffn_b, final_gain=d_fg, w_in=d_win, w_out=d_wout, w_up=d_wup, w_down=d_wd.reshape(by_shard))
    in_flight = dict(w_in16=d_win16, w_out16=d_wout16, land_w_up=land_wup, land_w_down=land_wd)
    return gx, grads, d_mod, loss, in_flight


MESH = pl.DeviceIdType.MESH
VMEM_SPEC = pl.BlockSpec(memory_space=pltpu.VMEM)
PEER_FLIPS = [(a, b, d) for a in (0, 1) for b in (0, 1) for d in (0, 1)][1:]
CHIP_FLIPS = [(1, 0), (0, 1), (1, 1)]


def _coords():
    return lax.axis_index("x"), lax.axis_index("y"), lax.axis_index("c")


def _flip(v, bit):
    return 1 - v if bit else v


def _rows8(block):
    return pl.ds(pl.multiple_of(8 * block, 8), 8)


def _ada_steps(c_ref, w_ref, b_ref, call_ref, mod_ref, cpad, modall, send_sems, recv_sems):
    x, y, c = _coords()
    me = 4 * x + 2 * y + c
    cpad[...] = jnp.zeros_like(cpad)
    cpad[pl.ds(0, 1), :] = c_ref[...]

    def gather_copy(j, flip):
        peer = (_flip(x, flip[0]), _flip(y, flip[1]), _flip(c, flip[2]))
        return pltpu.make_async_remote_copy(
            src_ref=cpad, dst_ref=call_ref.at[_rows8(me)], send_sem=send_sems.at[j], recv_sem=recv_sems.at[j],
            device_id=peer, device_id_type=MESH)

    def piece_copy(j, flip):
        tx, ty = _flip(x, flip[0]), _flip(y, flip[1])
        return pltpu.make_async_remote_copy(
            src_ref=modall.at[_rows8(4 * tx + 2 * ty + c)], dst_ref=mod_ref.at[_rows8(2 * x + y)],
            send_sem=send_sems.at[len(PEER_FLIPS) + j], recv_sem=recv_sems.at[len(PEER_FLIPS) + j],
            device_id=(tx, ty, c), device_id_type=MESH)

    copies = [gather_copy(j, f) for j, f in enumerate(PEER_FLIPS)]
    for cp in copies:
        cp.start()
    call_ref[_rows8(me), :] = cpad[...]

    def middle():
        for cp in copies:
            cp.wait_recv()
        for cp in copies:
            cp.wait_send()
        cv = call_ref[...]
        c_act = (cv * jax.nn.sigmoid(cv)).astype(BF16)
        modall[...] = _nn(c_act, w_ref[...].astype(BF16)) + b_ref[...]
        for j, f in enumerate(CHIP_FLIPS):
            piece_copy(j, f).start()
        mod_ref[_rows8(2 * x + y), :] = modall[_rows8(me), :]

    def finish():
        for j, f in enumerate(CHIP_FLIPS):
            piece_copy(j, f).wait_recv()
        for j, f in enumerate(CHIP_FLIPS):
            piece_copy(j, f).wait_send()

    return middle, finish


def _gather_weights(shards, filters, n_now, c_row, w_ada_sh, b_ada_sh):
    n = len(shards)
    nf = len(filters)
    ada_cols = w_ada_sh.shape[1]

    def body(*refs):
        ins, f_ins, ada_ins = refs[:n], refs[n:n + nf], refs[n + nf:n + nf + 3]
        refs = refs[n + nf + 3:]
        outs, f_outs, ada_outs = refs[:n], refs[n:n + nf], refs[n + nf:n + nf + 2]
        refs = refs[n + nf + 2:]
        stage = refs[:n]
        send_sems, recv_sems, local_sems, f_send_sems, f_recv_sems, cpad, modall, ada_send, ada_recv = refs[n:]
        ada_middle, ada_finish = _ada_steps(*ada_ins, *ada_outs, cpad, modall, ada_send, ada_recv)
        x, y, c = _coords()
        k = 2 * x + y
        sibling = (x, y, 1 - c)

        def filter_copy(w, j, slot):
            tx, ty = _flip(x, CHIP_FLIPS[j][0]), _flip(y, CHIP_FLIPS[j][1])
            return pltpu.make_async_remote_copy(
                src_ref=f_ins[w], dst_ref=f_outs[w].at[slot], send_sem=f_send_sems.at[w, j],
                recv_sem=f_recv_sems.at[w, j], device_id=(tx, ty, c), device_id_type=MESH)

        def half(w, which):
            h = shards[w].shape[0] // 2
            return pl.ds(pl.multiple_of(which * h, 16), h)

        def ici_copy(w, j, src, slot):
            tx, ty = _flip(x, CHIP_FLIPS[j][0]), _flip(y, CHIP_FLIPS[j][1])
            return pltpu.make_async_remote_copy(
                src_ref=src, dst_ref=outs[w].at[slot, half(w, c)], send_sem=send_sems.at[w, j],
                recv_sem=recv_sems.at[w, j], device_id=(tx, ty, c), device_id_type=MESH)

        def d2d_copy(w, j, slot, which):
            rows = outs[w].at[slot, half(w, which)]
            return pltpu.make_async_remote_copy(
                src_ref=rows, dst_ref=rows, send_sem=send_sems.at[w, len(CHIP_FLIPS) + j],
                recv_sem=recv_sems.at[w, len(CHIP_FLIPS) + j], device_id=sibling, device_id_type=MESH)

        def chip_of(j):
            return 2 * _flip(x, CHIP_FLIPS[j][0]) + _flip(y, CHIP_FLIPS[j][1])

        local, first, passed = [], [], []
        for w in range(nf):
            local.append(pltpu.make_async_copy(f_ins[w], f_outs[w].at[k], local_sems.at[n + w]))
            local[-1].start()
            for j in range(len(CHIP_FLIPS)):
                first.append(filter_copy(w, j, k))
                first[-1].start()
        for w in range(n):
            stage[w][...] = ins[w][...].astype(BF16)
            local.append(pltpu.make_async_copy(stage[w], outs[w].at[k], local_sems.at[w]))
            local[-1].start()
            if w < n_now:
                for j in range(len(CHIP_FLIPS)):
                    first.append(ici_copy(w, j, stage[w].at[half(w, c)], k))
                    first[-1].start()
        ada_middle()
        for w in range(nf):
            for j in range(len(CHIP_FLIPS)):
                filter_copy(w, j, chip_of(j)).wait_recv()
        for w in range(n_now):
            for j in range(len(CHIP_FLIPS)):
                ici_copy(w, j, stage[w].at[half(w, c)], chip_of(j)).wait_recv()
                passed.append(d2d_copy(w, j, chip_of(j), c))
                passed[-1].start()
        for w in range(n_now):
            for j in range(len(CHIP_FLIPS)):
                d2d_copy(w, j, chip_of(j), 1 - c).wait_recv()
        for cp in first + passed:
            cp.wait_send()
        for cp in local:
            cp.wait()
        ada_finish()

    sem_shape = (n_now, 2 * len(CHIP_FLIPS))
    f_sem_shape = (nf, len(CHIP_FLIPS))
    n_ada_sem = len(PEER_FLIPS) + len(CHIP_FLIPS)
    outs = pl.pallas_call(
        body, name="gather_weights",
        in_specs=[VMEM_SPEC] * (n + nf + 3), out_specs=[ANY] * (n + nf) + [VMEM_SPEC, VMEM_SPEC],
        out_shape=[jax.ShapeDtypeStruct((N_SHARD,) + s.shape, BF16) for s in shards]
        + [jax.ShapeDtypeStruct((N_SHARD,) + s.shape, F32) for s in filters]
        + [jax.ShapeDtypeStruct((8 * N_DEV, D_MODEL), F32), jax.ShapeDtypeStruct((8 * N_SHARD, ada_cols), F32)],
        scratch_shapes=[pltpu.VMEM(s.shape, BF16) for s in shards]
        + [pltpu.SemaphoreType.DMA(sem_shape), pltpu.SemaphoreType.DMA(sem_shape), pltpu.SemaphoreType.DMA((n + nf,)),
           pltpu.SemaphoreType.DMA(f_sem_shape), pltpu.SemaphoreType.DMA(f_sem_shape),
           pltpu.VMEM((8, D_MODEL), F32), pltpu.VMEM((8 * N_DEV, ada_cols), F32),
           pltpu.SemaphoreType.DMA((n_ada_sem,)), pltpu.SemaphoreType.DMA((n_ada_sem,))],
        compiler_params=pltpu.CompilerParams(vmem_limit_bytes=VMEM_LIMIT_BYTES),
    )(*shards, *filters, c_row, w_ada_sh, b_ada_sh)
    return outs[:n], outs[n:n + nf], outs[n + nf], outs[n + nf + 1]


def _final_comm(srcs16, small):
    n = len(srcs16)
    rows = small.shape[0]
    half = rows // 2

    def body(*refs):
        srcs, small_ref = refs[:n], refs[n]
        lands, small_out = refs[n + 1:2 * n + 1], refs[2 * n + 1]
        chip_sum, got_c, got_x, got_y, part_x, send_sems, recv_sems, small_send_sems, small_recv_sems = refs[2 * n + 2:]
        x, y, c = _coords()
        sibling = (x, y, 1 - c)
        mine = pl.ds(pl.multiple_of(c * half, 8), half)
        copies = []
        for w in range(n):
            for j, flip in enumerate(CHIP_FLIPS):
                tx, ty = _flip(x, flip[0]), _flip(y, flip[1])
                copies.append(pltpu.make_async_remote_copy(
                    src_ref=srcs[w].at[2 * tx + ty], dst_ref=lands[w].at[j], send_sem=send_sems.at[w, j],
                    recv_sem=recv_sems.at[w, j], device_id=(tx, ty, c), device_id_type=MESH))
        for cp in copies:
            cp.start()

        def exchange(stage, src, dst, peer):
            rc = pltpu.make_async_remote_copy(
                src_ref=src, dst_ref=dst, send_sem=small_send_sems.at[stage], recv_sem=small_recv_sems.at[stage],
                device_id=peer, device_id_type=MESH)
            rc.start()
            rc.wait()

        exchange(0, small_ref, got_c, sibling)
        chip_sum[...] = small_ref[...] + got_c[...]
        exchange(1, chip_sum.at[mine], got_x, (1 - x, y, c))
        part_x[...] = chip_sum[mine, :] + got_x[...]
        exchange(2, part_x, got_y, (x, 1 - y, c))
        small_out[mine, :] = part_x[...] + got_y[...]
        exchange(3, small_out.at[mine], small_out.at[mine], sibling)
        for cp in copies:
            cp.wait()

    n_chip = len(CHIP_FLIPS)
    half_shape = (half, small.shape[1])
    outs = pl.pallas_call(
        body, name="final_comm",
        in_specs=[ANY] * n + [VMEM_SPEC], out_specs=[ANY] * n + [VMEM_SPEC],
        out_shape=[jax.ShapeDtypeStruct((n_chip,) + a.shape[1:], BF16) for a in srcs16]
        + [jax.ShapeDtypeStruct(small.shape, F32)],
        scratch_shapes=[pltpu.VMEM(small.shape, F32), pltpu.VMEM(small.shape, F32), pltpu.VMEM(half_shape, F32),
                        pltpu.VMEM(half_shape, F32), pltpu.VMEM(half_shape, F32),
                        pltpu.SemaphoreType.DMA((n, n_chip)), pltpu.SemaphoreType.DMA((n, n_chip)),
                        pltpu.SemaphoreType.DMA((4,)), pltpu.SemaphoreType.DMA((4,))],
        compiler_params=pltpu.CompilerParams(vmem_limit_bytes=VMEM_LIMIT_BYTES),
    )(*srcs16, small)
    return outs[:n], outs[n]


ADD_CHUNKS = 4


def _scatter_sum(pos, owns, lands):
    n = len(owns)

    def specs(own_shape, land_shape):
        peers, rows, cols = land_shape
        pick = 1 if own_shape[1] == 2 * rows else 0
        if cols % (128 * ADD_CHUNKS) == 0:
            blk = (rows, cols // ADD_CHUNKS)
            return (pl.BlockSpec((1,) + blk, lambda i, p: (2 * p[0] + p[1], pick * p[2], i)),
                    pl.BlockSpec((peers,) + blk, lambda i, p: (0, 0, i)),
                    pl.BlockSpec((1,) + blk, lambda i, p: (p[2], 0, i)))
        blk = (rows // ADD_CHUNKS, cols)
        return (pl.BlockSpec((1,) + blk, lambda i, p: (2 * p[0] + p[1], pick * p[2] * ADD_CHUNKS + i, 0)),
                pl.BlockSpec((peers,) + blk, lambda i, p: (0, i, 0)),
                pl.BlockSpec((1,) + blk, lambda i, p: (p[2], i, 0)))

    def body(pos_ref, *refs):
        for idx in range(n):
            own, land, out = refs[idx], refs[n + idx], refs[2 * n + idx]
            total = own[0]
            for f in range(land.shape[0]):
                total = total + land[f].astype(F32)
            out[0] = total

    all_specs = [specs(o.shape, l.shape) for o, l in zip(owns, lands)]
    return pl.pallas_call(
        body, name="scatter_sum",
        grid_spec=pltpu.PrefetchScalarGridSpec(
            num_scalar_prefetch=1, grid=(ADD_CHUNKS,),
            in_specs=[s[0] for s in all_specs] + [s[1] for s in all_specs], out_specs=[s[2] for s in all_specs]),
        out_shape=[jax.ShapeDtypeStruct((2,) + l.shape[1:], F32) for l in lands],
        compiler_params=_params(),
    )(pos, *owns, *lands)


def _swap_halves(halves):
    n = len(halves)

    def body(*refs):
        ins, outs = refs[:n], refs[n:2 * n]
        send_sems, recv_sems = refs[2 * n:]
        x, y, c = _coords()
        copies = [pltpu.make_async_remote_copy(
            src_ref=ins[idx].at[pl.ds(c, 1)], dst_ref=outs[idx].at[pl.ds(c, 1)], send_sem=send_sems.at[idx],
            recv_sem=recv_sems.at[idx], device_id=(x, y, 1 - c), device_id_type=MESH) for idx in range(n)]
        for cp in copies:
            cp.start()
        for cp in copies:
            cp.wait()

    return pl.pallas_call(
        body, name="swap_halves",
        in_specs=[ANY] * n, out_specs=[ANY] * n, input_output_aliases={idx: idx for idx in range(n)},
        out_shape=[jax.ShapeDtypeStruct(a.shape, F32) for a in halves],
        scratch_shapes=[pltpu.SemaphoreType.DMA((n,)), pltpu.SemaphoreType.DMA((n,))],
    )(*halves)


def _adamw_math(w, g, m, v):
    m = ADAM_B1 * m + (1.0 - ADAM_B1) * g
    v = ADAM_B2 * v + (1.0 - ADAM_B2) * jnp.square(g)
    m_hat = m / (1.0 - ADAM_B1 ** ADAM_STEP)
    v_hat = v / (1.0 - ADAM_B2 ** ADAM_STEP)
    delta = -ADAM_LR * (m_hat / (jnp.sqrt(v_hat) + ADAM_EPS) + ADAM_WD * w)
    return delta, m, v


def _adamw(name, w, g, m, v, block_rows):
    rows, cols = w.shape

    def body(w_ref, g_ref, m_ref, v_ref, d_out, m_out, v_out):
        d_out[...], m_out[...], v_out[...] = _adamw_math(w_ref[...], g_ref[...], m_ref[...], v_ref[...])

    spec = pl.BlockSpec((block_rows, cols), lambda i: (i, 0))
    shape = jax.ShapeDtypeStruct((rows, cols), F32)
    return pl.pallas_call(
        body, grid=(rows // block_rows,), name=name, in_specs=[spec] * 4, out_specs=[spec] * 3,
        out_shape=[shape] * 3, compiler_params=_params(),
    )(w, g, m, v)


def _adamw_many(ws, gs, ms, vs):
    n = len(ws)

    def body(*refs):
        w_refs, g_refs, m_refs, v_refs = (refs[q * n:(q + 1) * n] for q in range(4))
        d_outs, m_outs, v_outs = (refs[(4 + q) * n:(5 + q) * n] for q in range(3))
        for idx in range(n):
            d_outs[idx][...], m_outs[idx][...], v_outs[idx][...] = _adamw_math(
                w_refs[idx][...], g_refs[idx][...], m_refs[idx][...], v_refs[idx][...])

    shapes = [jax.ShapeDtypeStruct(w.shape, F32) for w in ws]
    outs = pl.pallas_call(
        body, name="adamw_small", in_specs=[VMEM_SPEC] * (4 * n), out_specs=[VMEM_SPEC] * (3 * n),
        out_shape=shapes * 3, compiler_params=pltpu.CompilerParams(vmem_limit_bytes=VMEM_LIMIT_BYTES),
    )(*ws, *gs, *ms, *vs)
    return outs[:n], outs[n:2 * n], outs[2 * n:]


def _adamw_ada(c_all16, dmod16, w, m, v, block_rows):
    rows, cols = w.shape

    def body(c_ref, dm_ref, w_ref, m_ref, v_ref, g_out, d_out, m_out, v_out):
        cv = c_ref[...]
        g = _tn((cv * jax.nn.sigmoid(cv)).astype(BF16), dm_ref[...].astype(BF16))
        g_out[...] = g
        d_out[...], m_out[...], v_out[...] = _adamw_math(w_ref[...], g, m_ref[...], v_ref[...])

    spec = pl.BlockSpec((block_rows, cols), lambda i: (i, 0))
    shape = jax.ShapeDtypeStruct((rows, cols), F32)
    return pl.pallas_call(
        body, grid=(rows // block_rows,), name="adamw_w_ada",
        in_specs=[pl.BlockSpec((16, block_rows), lambda i: (0, i)), _full(dmod16.shape), spec, spec, spec],
        out_specs=[spec] * 4, out_shape=[shape] * 4, compiler_params=_params(),
    )(c_all16, dmod16, w, m, v)


SMALL_REPLICATED = ["b_ada", "norm1_gain", "conv_dw_b", "conv_ln_g", "conv_ln_b", "gm_ln_g", "gm_ln_b", "gm_ws", "gm_bs",
                    "mix_out_gain", "norm2_gain", "ffn_dw_b", "final_gain"]
SMALL_SHARDED = ["conv_dw_w", "ffn_dw_w"]
PACK_ROWS = 256
WEIGHT_ORDER = ["w_ada", "b_ada", "norm1_gain", "w_in", "conv_dw_w", "conv_dw_b", "conv_ln_g", "conv_ln_b", "gm_ln_g",
                "gm_ln_b", "gm_ws", "gm_bs", "mix_out_gain", "w_out", "norm2_gain", "w_up", "ffn_dw_w", "ffn_dw_b",
                "w_down", "final_gain"]


def _pack(parts, rows):
    total = rows * D_MODEL
    flat, offset = None, 0
    for a in parts:
        piece = jnp.pad(a.reshape(-1), (offset, total - offset - a.size))
        flat = piece if flat is None else flat + piece
        offset += a.size
    return flat.reshape(rows, D_MODEL)


def _unpack(packed, shapes):
    flat = packed.reshape(-1)
    out, pos = [], 0
    for s in shapes:
        size = 1
        for d in s:
            size *= d
        out.append(flat[pos:pos + size].reshape(s))
        pos += size
    return out


def kernel(x, c, w_ada, b_ada, norm1_gain, w_in, conv_dw_w, conv_dw_b, conv_ln_g, conv_ln_b, gm_ln_g, gm_ln_b, gm_ws, gm_bs, mix_out_gain, w_out, norm2_gain, w_up, ffn_dw_w, ffn_dw_b, w_down, final_gain, loss_target, m_w_ada, m_b_ada, m_norm1_gain, m_w_in, m_conv_dw_w, m_conv_dw_b, m_conv_ln_g, m_conv_ln_b, m_gm_ln_g, m_gm_ln_b, m_gm_ws, m_gm_bs, m_mix_out_gain, m_w_out, m_norm2_gain, m_w_up, m_ffn_dw_w, m_ffn_dw_b, m_w_down, m_final_gain, v_w_ada, v_b_ada, v_norm1_gain, v_w_in, v_conv_dw_w, v_conv_dw_b, v_conv_ln_g, v_conv_ln_b, v_gm_ln_g, v_gm_ln_b, v_gm_ws, v_gm_bs, v_mix_out_gain, v_w_out, v_norm2_gain, v_w_up, v_ffn_dw_w, v_ffn_dw_b, v_w_down, v_final_gain):
    weights = dict(w_ada=w_ada, b_ada=b_ada, norm1_gain=norm1_gain, w_in=w_in, conv_dw_w=conv_dw_w, conv_dw_b=conv_dw_b,
                   conv_ln_g=conv_ln_g, conv_ln_b=conv_ln_b, gm_ln_g=gm_ln_g, gm_ln_b=gm_ln_b, gm_ws=gm_ws, gm_bs=gm_bs,
                   mix_out_gain=mix_out_gain, w_out=w_out, norm2_gain=norm2_gain, w_up=w_up, ffn_dw_w=ffn_dw_w,
                   ffn_dw_b=ffn_dw_b, w_down=w_down, final_gain=final_gain)
    mom1 = dict(w_ada=m_w_ada, b_ada=m_b_ada, norm1_gain=m_norm1_gain, w_in=m_w_in, conv_dw_w=m_conv_dw_w,
                conv_dw_b=m_conv_dw_b, conv_ln_g=m_conv_ln_g, conv_ln_b=m_conv_ln_b, gm_ln_g=m_gm_ln_g, gm_ln_b=m_gm_ln_b,
                gm_ws=m_gm_ws, gm_bs=m_gm_bs, mix_out_gain=m_mix_out_gain, w_out=m_w_out, norm2_gain=m_norm2_gain,
                w_up=m_w_up, ffn_dw_w=m_ffn_dw_w, ffn_dw_b=m_ffn_dw_b, w_down=m_w_down, final_gain=m_final_gain)
    mom2 = dict(w_ada=v_w_ada, b_ada=v_b_ada, norm1_gain=v_norm1_gain, w_in=v_w_in, conv_dw_w=v_conv_dw_w,
                conv_dw_b=v_conv_dw_b, conv_ln_g=v_conv_ln_g, conv_ln_b=v_conv_ln_b, gm_ln_g=v_gm_ln_g, gm_ln_b=v_gm_ln_b,
                gm_ws=v_gm_ws, gm_bs=v_gm_bs, mix_out_gain=v_mix_out_gain, w_out=v_w_out, norm2_gain=v_norm2_gain,
                w_up=v_w_up, ffn_dw_w=v_ffn_dw_w, ffn_dw_b=v_ffn_dw_b, w_down=v_w_down, final_gain=v_final_gain)
    shard = 2 * lax.axis_index("x") + lax.axis_index("y")
    me = 2 * shard + lax.axis_index("c")

    ada_cols = w_ada.shape[2]
    b_ada_sh = lax.dynamic_slice(b_ada, (0, shard * ada_cols), (1, ada_cols))
    (w_in_g, w_out_g, w_up_part, w_down_part), (conv_w_g, ffn_w_g), c_all64, mod32 = _gather_weights(
        [w_in[0], w_out[0], w_up[0], w_down[0]], [conv_dw_w[0], ffn_dw_w[0]], 2, c, w_ada[0], b_ada_sh)
    c_all = c_all64[::8]
    mod = mod32[::8].reshape(1, N_SHARD * ada_cols)
    conv_w_full = jnp.transpose(conv_w_g, (1, 0, 2)).reshape(CONV_K, D_HALF)
    ffn_w_full = jnp.transpose(ffn_w_g, (1, 0, 2)).reshape(FFN_K, 2 * D_FF)

    p = dict(norm1_gain=norm1_gain, conv_dw_w=conv_w_full, conv_dw_b=conv_dw_b, conv_ln_g=conv_ln_g,
             conv_ln_b=conv_ln_b, gm_ln_g=gm_ln_g, gm_ln_b=gm_ln_b, gm_ws=gm_ws[0], gm_bs=gm_bs[0],
             mix_out_gain=mix_out_gain, norm2_gain=norm2_gain, ffn_dw_w=ffn_w_full, ffn_dw_b=ffn_dw_b,
             final_gain=final_gain[None])
    grad_x, g, d_mod, loss, in_flight = _local_step(
        x[0], loss_target[0], mod, p, w_in_g, w_out_g.reshape(D_MODEL, D_MODEL), w_up_part, w_down_part)

    n_mod = d_mod.shape[1]
    dmod_rows = lax.dynamic_update_slice(jnp.zeros((N_DEV, n_mod), F32), d_mod, (me, 0))
    g["b_ada"] = d_mod
    small = _pack([g[k] for k in SMALL_REPLICATED] + [g[k] for k in SMALL_SHARDED] + [dmod_rows, loss[0, :1]], PACK_ROWS)
    (land_w_in, land_w_out), small = _final_comm([in_flight["w_in16"], in_flight["w_out16"]], small)
    pos = jnp.stack(_coords()).astype(jnp.int32)
    halves = _scatter_sum(pos, [g["w_in"], g["w_out"], g["w_up"], g["w_down"]],
                          [land_w_in, land_w_out, in_flight["land_w_up"], in_flight["land_w_down"]])
    full = _swap_halves(halves)
    grads = dict(w_in=full[0].reshape(w_in.shape[1:]), w_out=full[1].reshape(w_out.shape[1:]),
                 w_up=full[2].reshape(w_up.shape[1:]), w_down=full[3].reshape(w_down.shape[1:]))

    small_shapes = ([weights[k].shape for k in SMALL_REPLICATED] + [(CONV_K, D_HALF), (FFN_K, 2 * D_FF)]
                    + [(N_DEV, n_mod), (1,)])
    *small_grads, conv_w_grad, ffn_w_grad, dmod_all, loss_sum = _unpack(small, small_shapes)
    grads.update(zip(SMALL_REPLICATED, small_grads))
    grads["conv_dw_w"] = lax.dynamic_slice(conv_w_grad, (0, shard * conv_dw_w.shape[2]), conv_dw_w.shape[1:])[None]
    grads["ffn_dw_w"] = lax.dynamic_slice(ffn_w_grad, (0, shard * ffn_dw_w.shape[2]), ffn_dw_w.shape[1:])[None]

    delta, new_m, new_v = {}, {}, {}
    for name, block_rows in (("w_in", 256), ("w_out", 128), ("w_up", 256), ("w_down", 352)):
        delta[name], new_m[name], new_v[name] = [a[None] for a in _adamw(
            "adamw_" + name, weights[name][0], grads[name], mom1[name][0], mom2[name][0], block_rows)]
        grads[name] = grads[name][None]
    dmod_sh = lax.dynamic_slice(dmod_all, (0, shard * ada_cols), (N_DEV, ada_cols))
    pad8 = ((0, 16 - N_DEV), (0, 0))
    grads["w_ada"], delta["w_ada"], new_m["w_ada"], new_v["w_ada"] = [a[None] for a in _adamw_ada(
        jnp.pad(c_all, pad8), jnp.pad(dmod_sh, pad8), w_ada[0], m_w_ada[0], v_w_ada[0], 256)]
    small_names = SMALL_REPLICATED + SMALL_SHARDED

    def two_d(a):
        return a.reshape(1, -1) if a.ndim == 1 else a

    small_out = _adamw_many(*[[two_d(d[k]) for k in small_names] for d in (weights, grads, mom1, mom2)])
    for d, arrs in zip((delta, new_m, new_v), small_out):
        d.update({k: a.reshape(weights[k].shape) for k, a in zip(small_names, arrs)})

    return (loss_sum.reshape(()), grad_x[None], *[grads[k] for k in WEIGHT_ORDER], *[delta[k] for k in WEIGHT_ORDER],
            *[new_m[k] for k in WEIGHT_ORDER], *[new_v[k] for k in WEIGHT_ORDER])
```

```python
import jax
import jax.numpy as jnp
from jax import lax
from jax.experimental import pallas as pl
from jax.experimental.pallas import tpu as pltpu

F32 = jnp.float32
BF16 = jnp.bfloat16

D_MODEL = 1024
D_HALF = 512
D_FF = 2816
CONV_K = 31
FFN_K = 3
CHUNK = 128
N_HEADS = 8
HEAD_DIM = 64
N_SHARD = 4
N_DEV = 8
RMS_EPS = 1e-6
LN_EPS = 1e-5
ADAM_LR, ADAM_B1, ADAM_B2, ADAM_EPS, ADAM_WD, ADAM_STEP = 0.001, 0.9, 0.999, 1e-08, 0.01, 10

TILE = 256
HALO = 32
FFN_HALO = 16
FFN_BLK = 256
UP_SHARD = 2 * D_FF // N_SHARD
VMEM_LIMIT_BYTES = 56 * 1024 * 1024
FFN_VMEM_LIMIT_BYTES = 58 * 1024 * 1024

ANY = pl.BlockSpec(memory_space=pl.ANY)
NT_DIMS = (((1,), (1,)), ((), ()))
TN_DIMS = (((0,), (0,)), ((), ()))


def _full(shape):
    return pl.BlockSpec(shape, lambda i: (0,) * len(shape))


def _nn(a, b):
    return jnp.dot(a, b, preferred_element_type=F32)


def _nt(a, b):
    return lax.dot_general(a, b, NT_DIMS, preferred_element_type=F32)


def _tn(a, b):
    return lax.dot_general(a, b, TN_DIMS, preferred_element_type=F32)


def _colsum(a):
    return jnp.sum(a, axis=0, keepdims=True)


def _params(semantics=("arbitrary",)):
    return pltpu.CompilerParams(dimension_semantics=semantics, vmem_limit_bytes=VMEM_LIMIT_BYTES)


def _rms(v, gain):
    return v * lax.rsqrt(jnp.mean(v * v, axis=-1, keepdims=True) + RMS_EPS) * gain


def _layer_norm(v, gain, bias):
    mu = jnp.mean(v, axis=-1, keepdims=True)
    var = jnp.mean(jnp.square(v - mu), axis=-1, keepdims=True)
    return (v - mu) * lax.rsqrt(var + LN_EPS) * gain + bias


def _mod_norm(v, gain, scale, shift):
    return _rms(v, gain) * (1.0 + scale) + shift


def _conv_branch(a1, ln_g, ln_b, out_gain):
    a2 = _layer_norm(a1, ln_g, ln_b)
    return _rms(a2 * jax.nn.sigmoid(a2), out_gain)


def _gate_branch(gu, sp, out_gain):
    return _rms(jax.nn.gelu(gu) * sp, out_gain)


def _gv_norm(gv, ln_g, ln_b):
    return _layer_norm(jax.nn.gelu(gv), ln_g, ln_b)


def _rms_parts(v):
    r = lax.rsqrt(jnp.mean(v * v, axis=-1, keepdims=True) + RMS_EPS)
    return v * r, r


def _rms_back(dn, n, r):
    return r * (dn - n * jnp.mean(dn * n, axis=-1, keepdims=True))


def _ln_parts(v):
    mu = jnp.mean(v, axis=-1, keepdims=True)
    rs = lax.rsqrt(jnp.mean(jnp.square(v - mu), axis=-1, keepdims=True) + LN_EPS)
    return (v - mu) * rs, rs


def _ln_back(dn, n, rs):
    return rs * (dn - jnp.mean(dn, axis=-1, keepdims=True) - n * jnp.mean(dn * n, axis=-1, keepdims=True))


GELU_C = 0.7978845608028654
GELU_A = 0.044715


def _gelu_parts(v):
    v2 = v * v
    th = jnp.tanh(GELU_C * (v + GELU_A * (v2 * v)))
    cdf = 0.5 * (1.0 + th)
    return v * cdf, cdf + (0.5 * GELU_C) * v * (1.0 - th * th) * (1.0 + (3.0 * GELU_A) * v2)


def _rms_vjp(v, gain):
    n, r = _rms_parts(v)
    return n * gain, lambda dy: (_rms_back(dy * gain, n, r), _colsum(dy * n))


def _mod_norm_vjp(v, gain, scale, shift):
    n, r = _rms_parts(v)

    def back(dy):
        q = _colsum(dy * n)
        return _rms_back(dy * (gain * (1.0 + scale)), n, r), q * (1.0 + scale), q * gain, _colsum(dy)

    return n * gain * (1.0 + scale) + shift, back


def _conv_branch_vjp(a1, ln_g, ln_b, out_gain):
    n1, rs1 = _ln_parts(a1)
    a2 = n1 * ln_g + ln_b
    s = jax.nn.sigmoid(a2)
    a3 = a2 * s
    n3, r3 = _rms_parts(a3)

    def back(dy):
        da2 = _rms_back(dy * out_gain, n3, r3) * (s + a3 * (1.0 - s))
        return _ln_back(da2 * ln_g, n1, rs1), _colsum(da2 * n1), _colsum(da2), _colsum(dy * n3)

    return n3 * out_gain, back


def _gate_branch_vjp(gu, sp, out_gain):
    ge, dge = _gelu_parts(gu)
    n, r = _rms_parts(ge * sp)

    def back(dy):
        dg = _rms_back(dy * out_gain, n, r)
        return dg * sp * dge, dg * ge, _colsum(dy * n)

    return n * out_gain, back


def _gv_norm_vjp(gv, ln_g, ln_b):
    ge, dge = _gelu_parts(gv)
    n, rs = _ln_parts(ge)
    return n * ln_g + ln_b, lambda dy: (_ln_back(dy * ln_g, n, rs) * dge, _colsum(dy * n), _colsum(dy))


def _head_pair_matmul(wp_ref, v):
    lane = lax.broadcasted_iota(jnp.int32, (CHUNK, CHUNK), 1)
    rows = []
    for n in range(v.shape[0] // CHUNK):
        cols = []
        for j in range(N_HEADS // 2):
            r = _nn(wp_ref[j], v[n * CHUNK:(n + 1) * CHUNK, j * CHUNK:(j + 1) * CHUNK])
            cols.append(jnp.where(lane < HEAD_DIM, r[:CHUNK], r[CHUNK:]))
        rows.append(jnp.concatenate(cols, axis=1))
    return jnp.concatenate(rows, axis=0)


def _tile_bias(bs, tokens):
    return jnp.concatenate([bs] * (tokens // CHUNK), axis=0)


FORWARD_LEAD = 8


def _fwd_mixer(x, vec, conv_w, wpair, bs_full, w_in_g, w_out_g, late_parts):
    seq = x.shape[0]
    n_tiles = seq // TILE
    t = TILE
    n_late = len(late_parts)
    forward_step = max(n_tiles - FORWARD_LEAD, 0)
    names = ["norm1_gain", "sc1", "sh1", "gt1", "conv_dw_b", "conv_ln_g", "conv_ln_b", "gm_ln_g", "gm_ln_b",
             "mix_out_gain"]
    vecs = [vec[k] for k in names]

    def body(x_ref, g1, sc1, sh1, gt1, cb, clg, clb, vg, vb, mg, cw, wp, bs, win_hbm, wout_hbm, *rest):
        late = rest[n_late:2 * n_late]
        z_ref, a1_ref, sp_ref, y_ref, o1_ref, x2_ref = rest[2 * n_late:2 * n_late + 6]
        win_v, wout_v, halo, bank, sem, send_sems, recv_sems = rest[2 * n_late + 6:]
        i = pl.program_id(0)
        mx, my, mc = _coords()
        shard = 2 * mx + my

        def half(w, which):
            h = late[w].shape[1] // 2
            return pl.ds(pl.multiple_of(which * h, 16), h)

        def chip_of(j):
            return 2 * _flip(mx, CHIP_FLIPS[j][0]) + _flip(my, CHIP_FLIPS[j][1])

        def ici_copy(w, j, slot):
            rows = late[w].at[slot, half(w, mc)]
            return pltpu.make_async_remote_copy(
                src_ref=rows, dst_ref=rows, send_sem=send_sems.at[w, j], recv_sem=recv_sems.at[w, j],
                device_id=(_flip(mx, CHIP_FLIPS[j][0]), _flip(my, CHIP_FLIPS[j][1]), mc), device_id_type=MESH)

        def d2d_copy(w, j, which):
            rows = late[w].at[chip_of(j), half(w, which)]
            return pltpu.make_async_remote_copy(
                src_ref=rows, dst_ref=rows, send_sem=send_sems.at[w, len(CHIP_FLIPS) + j],
                recv_sem=recv_sems.at[w, len(CHIP_FLIPS) + j], device_id=(mx, my, 1 - mc), device_id_type=MESH)

        pairs = [(w, j) for w in range(n_late) for j in range(len(CHIP_FLIPS))]

        @pl.when(i == 0)
        def _():
            for w, j in pairs:
                ici_copy(w, j, shard).start()
            cps = [pltpu.make_async_copy(win_hbm, win_v, sem.at[0]),
                   pltpu.make_async_copy(wout_hbm, wout_v, sem.at[1])]
            for cp in cps:
                cp.start()
            for cp in cps:
                cp.wait()
            halo[...] = jnp.zeros_like(halo)

        @pl.when(i == forward_step)
        def _():
            for w, j in pairs:
                ici_copy(w, j, chip_of(j)).wait_recv()
                d2d_copy(w, j, mc).start()

        xv = x_ref[...]
        h1b = _mod_norm(xv, g1[...], sc1[...], sh1[...]).astype(BF16)
        zs = [_nn(h1b, win_v[k]) for k in range(N_SHARD)]
        for k in range(N_SHARD):
            z_ref[:, k * D_HALF:(k + 1) * D_HALF] = zs[k]
        ca, cg, gu, gv = zs
        a0 = ca * jax.nn.sigmoid(cg)
        ext = jnp.concatenate([halo[...], a0], axis=0)
        halo[...] = a0[t - HALO:]
        bank[0] = ext
        for b in range(1, 8):
            bank[b] = pltpu.roll(ext, b, axis=0)
        a1 = jnp.zeros((t, D_HALF), F32) + cb[...]
        for s in range(CONV_K):
            q, b = divmod(s, 8)
            a1 = a1 + bank[b, pl.ds(HALO - 8 * q, t), :] * cw[pl.ds(CONV_K - 1 - s, 1), :]
        a1_ref[...] = a1
        mgv = mg[...]
        ya = _conv_branch(a1, clg[...], clb[...], mgv[:, :D_HALF])
        gvn = _gv_norm(gv, vg[...], vb[...]).astype(BF16)
        sp = _head_pair_matmul(wp, gvn) + _tile_bias(bs[...], t)
        sp_ref[...] = sp
        yg = _gate_branch(gu, sp, mgv[:, D_HALF:])
        yb = jnp.concatenate([ya, yg], axis=1).astype(BF16)
        y_ref[...] = yb
        o1 = _nn(yb, wout_v[...])
        o1_ref[...] = o1
        x2_ref[...] = xv + gt1[...] * o1

        @pl.when(i == n_tiles - 1)
        def _():
            for w, j in pairs:
                d2d_copy(w, j, 1 - mc).wait_recv()
            for w, j in pairs:
                ici_copy(w, j, shard).wait_send()
                d2d_copy(w, j, mc).wait_send()

    def row(width):
        return pl.BlockSpec((t, width), lambda i: (i, 0))

    out_shape = [jax.ShapeDtypeStruct((seq, 4 * D_HALF), F32), jax.ShapeDtypeStruct((seq, D_HALF), F32),
                 jax.ShapeDtypeStruct((seq, D_HALF), F32), jax.ShapeDtypeStruct((seq, D_MODEL), BF16),
                 jax.ShapeDtypeStruct((seq, D_MODEL), F32), jax.ShapeDtypeStruct((seq, D_MODEL), F32)]
    n_in = 1 + len(vecs) + 3 + 2
    sem_shape = (n_late, 2 * len(CHIP_FLIPS))
    outs = pl.pallas_call(
        body, grid=(n_tiles,), name="fwd_mixer",
        in_specs=[row(D_MODEL)] + [_full(v.shape) for v in vecs]
        + [_full(conv_w.shape), _full(wpair.shape), _full(bs_full.shape), ANY, ANY] + [ANY] * n_late,
        out_specs=[ANY] * n_late + [row(4 * D_HALF), row(D_HALF), row(D_HALF), row(D_MODEL), row(D_MODEL),
                                    row(D_MODEL)],
        out_shape=[jax.ShapeDtypeStruct(a.shape, a.dtype) for a in late_parts] + out_shape,
        input_output_aliases={n_in + w: w for w in range(n_late)},
        scratch_shapes=[pltpu.VMEM(w_in_g.shape, BF16), pltpu.VMEM(w_out_g.shape, BF16),
                        pltpu.VMEM((HALO, D_HALF), F32), pltpu.VMEM((8, t + HALO, D_HALF), F32),
                        pltpu.SemaphoreType.DMA((2,)), pltpu.SemaphoreType.DMA(sem_shape),
                        pltpu.SemaphoreType.DMA(sem_shape)],
        compiler_params=_params(),
    )(x, *vecs, conv_w, wpair, bs_full, w_in_g, w_out_g, *late_parts)
    return outs[n_late:], outs[:n_late]


def _interleave_matrices():
    row = jnp.arange(TILE)
    token_of_row = (row % 8) * (TILE // 8) + row // 8
    to_inter = (token_of_row[:, None] == row[None, :]).astype(BF16)
    return to_inter, jnp.transpose(to_inter)


def _ffn(x2, target, norm2_gain, sc2, sh2, ffn_w, ffn_b, gt2, final_gain, w_up_g, w_down_g, to_inter, to_natural):
    seq = x2.shape[0]
    n_tiles = seq // TILE
    t = TILE
    n_blk = D_FF // FFN_BLK
    inv_d = 1.0 / D_MODEL

    def body(x2_ref, x2h_ref, tgt_ref, g2, sc2_ref, sh2_ref, fw, fb, gt2_ref, fg, pm_ref, pmt_ref, wup_hbm, wd_hbm,
             du_ref, dx2_ref, dfw_ref, dfb_ref, dfg_ref, dgt2_ref, dg2_ref, dsc2_ref, dsh2_ref, loss_ref, dwd_hbm,
             dwd16_hbm, wup_v, wd_v, dwd_acc, carry, u_s, sil_s, vds_s, f_s, du_s, sem):
        i = pl.program_id(0)
        tile = n_tiles - 1 - i
        sublane = lax.broadcasted_iota(jnp.int32, (8, FFN_BLK), 0)

        @pl.when(i == 0)
        def _():
            cps = [pltpu.make_async_copy(wd_hbm, wd_v, sem.at[0])]
            cps += [pltpu.make_async_copy(wup_hbm.at[k], wup_v.at[:, pl.ds(k * UP_SHARD, UP_SHARD)], sem.at[3 + k])
                    for k in range(N_SHARD)]
            for cp in cps:
                cp.start()
            for cp in cps:
                cp.wait()
            dwd_acc[...] = jnp.zeros_like(dwd_acc)
            carry[...] = jnp.zeros_like(carry)
            dfw_ref[...] = jnp.zeros_like(dfw_ref)
            dfb_ref[...] = jnp.zeros_like(dfb_ref)
            dfg_ref[...] = jnp.zeros_like(dfg_ref)
            dgt2_ref[...] = jnp.zeros_like(dgt2_ref)
            dg2_ref[...] = jnp.zeros_like(dg2_ref)
            dsc2_ref[...] = jnp.zeros_like(dsc2_ref)
            dsh2_ref[...] = jnp.zeros_like(dsh2_ref)
            loss_ref[...] = jnp.zeros_like(loss_ref)

        def cols_of(j):
            return pl.ds(j * FFN_BLK, FFN_BLK), pl.ds(D_FF + j * FFN_BLK, FFN_BLK)

        def wrap_down(last, before):
            return jnp.where(sublane == 0, pltpu.roll(before, 1, axis=0), pltpu.roll(last, 1, axis=0))

        def wrap_up(first, after):
            return jnp.where(sublane == 7, pltpu.roll(after, 7, axis=0), pltpu.roll(first, 7, axis=0))

        x2v = x2_ref[...]
        h2, h2_vjp = _mod_norm_vjp(x2v, g2[...], sc2_ref[...], sh2_ref[...])
        h2b = h2.astype(BF16)
        h2_before = _mod_norm(x2h_ref[...], g2[...], sc2_ref[...], sh2_ref[...]).astype(BF16)
        lhs = jnp.concatenate([_nn(pm_ref[...], h2b).astype(BF16), h2_before], axis=0)

        def up(j):
            cv, cg = cols_of(j)
            return _nn(lhs, wup_v[:, cv]), _nn(lhs, wup_v[:, cg])

        def conv(both, cols):
            cur = both[:t]
            u_s[:, cols] = cur.astype(BF16)
            before = jnp.where(tile > 0, both[t:], 0.0)
            w1 = wrap_down(cur[t - 8:], before)
            w2 = wrap_down(cur[t - 16:t - 8], pltpu.roll(before, 1, axis=0))
            back1 = jnp.concatenate([w1, cur[:t - 8]], axis=0)
            back2 = jnp.concatenate([w2, w1, cur[:t - 16]], axis=0)
            return (fb[:, cols] + cur * fw[pl.ds(2, 1), cols] + back1 * fw[pl.ds(1, 1), cols]
                    + back2 * fw[pl.ds(0, 1), cols])

        pm_t = pmt_ref[...]

        def to_natural_f32(a):
            hi = a.astype(BF16)
            rest = a - hi.astype(F32)
            mid = rest.astype(BF16)
            low = (rest - mid.astype(F32)).astype(BF16)
            return _nn(jnp.concatenate([pm_t, pm_t, pm_t], axis=1), jnp.concatenate([hi, mid, low], axis=0))

        o2 = jnp.zeros((t, D_MODEL), F32)
        ahead_uv = up(0)
        for j in range(n_blk):
            cv, cg = cols_of(j)
            both_v, both_g = ahead_uv
            if j + 1 < n_blk:
                ahead_uv = up(j + 1)
            val, gate = conv(both_v, cv), conv(both_g, cg)
            sig = jax.nn.sigmoid(gate)
            sil = gate * sig
            fb16 = (sil * val).astype(BF16)
            sil_s[:, cv] = sil
            vds_s[:, cv] = val * (sig + sil * (1.0 - sig))
            f_s[:, cv] = fb16
            o2 = o2 + _nn(fb16, wd_v[pl.ds(j * FFN_BLK, FFN_BLK), :])
        o2 = to_natural_f32(o2)

        gt2v = gt2_ref[...]
        x3 = x2v + gt2v * o2
        out, out_vjp = _rms_vjp(x3, fg[...])
        diff = out - tgt_ref[...]
        loss_ref[...] += jnp.zeros_like(loss_ref) + 0.5 * inv_d * jnp.sum(diff * diff)
        dx3, dfg = out_vjp(diff * inv_d)
        dfg_ref[...] += dfg
        dgt2_ref[...] += _colsum(dx3 * o2)
        do2b = _nn(pm_ref[...], (gt2v * dx3).astype(BF16)).astype(BF16)

        def conv_back(dd, cols):
            dfb_ref[:, cols] += _colsum(dd)
            nxt = carry[:, cols]
            w1 = wrap_up(dd[:8], nxt[:8])
            w2 = wrap_up(dd[8:16], nxt[8:])
            ahead = (dd, jnp.concatenate([dd[8:], w1], axis=0), jnp.concatenate([dd[16:], w1, w2], axis=0))
            carry[:, cols] = dd[:16]
            uv = u_s[:, cols].astype(F32)
            du = jnp.zeros((t, FFN_BLK), F32)
            for s in range(FFN_K):
                du = du + ahead[s] * fw[pl.ds(FFN_K - 1 - s, 1), cols]
                dfw_ref[pl.ds(FFN_K - 1 - s, 1), cols] += _colsum(ahead[s] * uv)
            du_s[:, cols] = du.astype(BF16)

        for j in range(n_blk):
            cv, cg = cols_of(j)
            rows = pl.ds(j * FFN_BLK, FFN_BLK)
            df = _nt(do2b, wd_v[rows, :])
            dwd_acc[rows, :] += _tn(f_s[:, cv], do2b)
            conv_back(df * sil_s[:, cv], cv)
            conv_back(df * vds_s[:, cv], cg)
        du16 = _nn(pm_t, du_s[...]).astype(BF16)
        du_ref[...] = du16
        dx2, dg2, dsc2, dsh2 = h2_vjp(_nt(du16, wup_v[...]))
        dx2_ref[...] = dx3 + dx2
        dg2_ref[...] += dg2
        dsc2_ref[...] += dsc2
        dsh2_ref[...] += dsh2

        @pl.when(i == n_tiles - 1)
        def _():
            cp = pltpu.make_async_copy(dwd_acc, dwd_hbm, sem.at[1])
            cp.start()
            wd_v[...] = dwd_acc[...].astype(BF16)
            cp16 = pltpu.make_async_copy(wd_v, dwd16_hbm, sem.at[2])
            cp16.start()
            cp.wait()
            cp16.wait()

    def rev(width):
        return pl.BlockSpec((t, width), lambda i: (n_tiles - 1 - i, 0))

    assert FFN_K == 3
    halo_spec = pl.BlockSpec((8, D_MODEL), lambda i: (jnp.maximum((n_tiles - 1 - i) * (t // 8) - 1, 0), 0))
    vec_spec = _full((1, D_MODEL))
    out_shape = [jax.ShapeDtypeStruct((seq, 2 * D_FF), BF16), jax.ShapeDtypeStruct((seq, D_MODEL), F32),
                 jax.ShapeDtypeStruct((FFN_K, 2 * D_FF), F32), jax.ShapeDtypeStruct((1, 2 * D_FF), F32),
                 jax.ShapeDtypeStruct((1, D_MODEL), F32), jax.ShapeDtypeStruct((1, D_MODEL), F32),
                 jax.ShapeDtypeStruct((1, D_MODEL), F32), jax.ShapeDtypeStruct((1, D_MODEL), F32),
                 jax.ShapeDtypeStruct((1, D_MODEL), F32),
                 jax.ShapeDtypeStruct((1, 128), F32), jax.ShapeDtypeStruct((D_FF, D_MODEL), F32),
                 jax.ShapeDtypeStruct((D_FF, D_MODEL), BF16)]
    return pl.pallas_call(
        body, grid=(n_tiles,), name="ffn",
        in_specs=[rev(D_MODEL), halo_spec, rev(D_MODEL), vec_spec, vec_spec, vec_spec, _full(ffn_w.shape),
                  _full(ffn_b.shape), _full(gt2.shape), _full(final_gain.shape), _full(to_inter.shape),
                  _full(to_natural.shape), ANY, ANY],
        out_specs=[rev(2 * D_FF), rev(D_MODEL), _full((FFN_K, 2 * D_FF)), _full((1, 2 * D_FF)), vec_spec, vec_spec,
                   vec_spec, vec_spec, vec_spec, _full((1, 128)), ANY, ANY],
        out_shape=out_shape,
        scratch_shapes=[pltpu.VMEM((D_MODEL, 2 * D_FF), BF16), pltpu.VMEM((D_FF, D_MODEL), BF16),
                        pltpu.VMEM((D_FF, D_MODEL), F32), pltpu.VMEM((FFN_HALO, 2 * D_FF), F32),
                        pltpu.VMEM((t, 2 * D_FF), BF16), pltpu.VMEM((t, D_FF), F32), pltpu.VMEM((t, D_FF), F32),
                        pltpu.VMEM((t, D_FF), BF16), pltpu.VMEM((t, 2 * D_FF), BF16),
                        pltpu.SemaphoreType.DMA((3 + N_SHARD,))],
        compiler_params=pltpu.CompilerParams(dimension_semantics=("arbitrary",), vmem_limit_bytes=FFN_VMEM_LIMIT_BYTES),
    )(x2, x2, target, norm2_gain, sc2, sh2, ffn_w, ffn_b, gt2, final_gain, to_inter, to_natural, w_up_g, w_down_g)


def _scatter_copies(src16, land, send_sems, recv_sems):
    x, y, c = _coords()
    h = src16.shape[1] // 2
    copies = []
    for f, flip in enumerate(PEER_FLIPS):
        tx, ty, tc = _flip(x, flip[0]), _flip(y, flip[1]), _flip(c, flip[2])
        copies.append(pltpu.make_async_remote_copy(
            src_ref=src16.at[2 * tx + ty, pl.ds(pl.multiple_of(tc * h, 16), h)], dst_ref=land.at[f],
            send_sem=send_sems.at[f], recv_sem=recv_sems.at[f], device_id=(tx, ty, tc), device_id_type=MESH))
    return copies


def _land_shape(src16):
    return jax.ShapeDtypeStruct((len(PEER_FLIPS), src16.shape[1] // 2, src16.shape[2]), BF16)


UP_TILE = 512


def _bwd_up(du, x2, norm2_gain, sc2, sh2, dwd16):
    seq = x2.shape[0]
    t = UP_TILE if seq % UP_TILE == 0 else TILE
    n_tiles = seq // t
    acc_shape = (N_SHARD, D_MODEL, UP_SHARD)

    def body(du_ref, x2_ref, g2, sc2_ref, sh2_ref, dwd16_hbm, dwup_hbm, dwup16_hbm, land_hbm,
             stage16, dwup_acc, sem, send_sems, recv_sems):
        i = pl.program_id(0)

        @pl.when(i == 0)
        def _():
            for cp in _scatter_copies(dwd16_hbm, land_hbm, send_sems, recv_sems):
                cp.start()
            dwup_acc[...] = jnp.zeros_like(dwup_acc)

        h2b = _mod_norm(x2_ref[...], g2[...], sc2_ref[...], sh2_ref[...]).astype(BF16)
        for k in range(N_SHARD):
            dwup_acc[k] += _tn(h2b, du_ref[:, k * UP_SHARD:(k + 1) * UP_SHARD])

        @pl.when(i == n_tiles - 1)
        def _():
            cp = pltpu.make_async_copy(dwup_acc, dwup_hbm, sem.at[0])
            cp.start()
            for k in range(N_SHARD):
                stage16[k] = dwup_acc[k].astype(BF16)
            cp16 = pltpu.make_async_copy(stage16, dwup16_hbm, sem.at[1])
            cp16.start()
            cp.wait()
            cp16.wait()
            for rc in _scatter_copies(dwd16_hbm, land_hbm, send_sems, recv_sems):
                rc.wait()

    def row(width):
        return pl.BlockSpec((t, width), lambda i: (i, 0))

    n_peer = len(PEER_FLIPS)
    return pl.pallas_call(
        body, grid=(n_tiles,), name="bwd_up",
        in_specs=[row(2 * D_FF), row(D_MODEL), _full((1, D_MODEL)), _full((1, D_MODEL)), _full((1, D_MODEL)), ANY],
        out_specs=[ANY, ANY, ANY],
        out_shape=[jax.ShapeDtypeStruct(acc_shape, F32), jax.ShapeDtypeStruct(acc_shape, BF16), _land_shape(dwd16)],
        scratch_shapes=[pltpu.VMEM(acc_shape, BF16), pltpu.VMEM(acc_shape, F32), pltpu.SemaphoreType.DMA((2,)),
                        pltpu.SemaphoreType.DMA((n_peer,)), pltpu.SemaphoreType.DMA((n_peer,))],
        compiler_params=_params(),
    )(du, x2, norm2_gain, sc2, sh2, dwd16)


def _bwd_mixer(dx2, x, z, a1, sp, yb, o1, vec, conv_w, wpair, wpair_t, causal_mask, w_in_g, w_out_g, dwup16):
    seq = x.shape[0]
    n_tiles = seq // TILE
    t = TILE
    names = ["norm1_gain", "sc1", "sh1", "gt1", "conv_ln_g", "conv_ln_b", "gm_ln_g", "gm_ln_b", "mix_out_gain"]
    vecs = [vec[k] for k in names]

    def body(dx2_ref, x_ref, z_ref, a1_ref, sp_ref, y_ref, o1_ref, g1, sc1, sh1, gt1, clg, clb, vg, vb, mg,
             cw, wp, wpt, mask_ref, win_hbm, wout_hbm, dwup16_hbm,
             gx_ref, dg1_ref, dsc1_ref, dsh1_ref, dgt1_ref, dcw_ref, dcb_ref, dclg_ref, dclb_ref, dvg_ref, dvb_ref,
             dmg_ref, dws_ref, dbs_ref, dwin_hbm, dwout_hbm, land_hbm, dwin16_hbm, dwout16_hbm,
             win_v, wout_v, dwin_acc, dwout_acc, carry, bank, dbs_acc, lwin, lwout, sem, send_sems, recv_sems,
             pair_send, pair_recv):
        i = pl.program_id(0)
        small = [dg1_ref, dsc1_ref, dsh1_ref, dgt1_ref, dcw_ref, dcb_ref, dclg_ref, dclb_ref, dvg_ref, dvb_ref,
                 dmg_ref, dws_ref, dbs_acc]

        @pl.when(i == 0)
        def _():
            for cp in _scatter_copies(dwup16_hbm, land_hbm, send_sems, recv_sems):
                cp.start()
            cps = [pltpu.make_async_copy(win_hbm, win_v, sem.at[0]),
                   pltpu.make_async_copy(wout_hbm, wout_v, sem.at[1])]
            for cp in cps:
                cp.start()
            for cp in cps:
                cp.wait()
            dwin_acc[...] = jnp.zeros_like(dwin_acc)
            dwout_acc[...] = jnp.zeros_like(dwout_acc)
            carry[...] = jnp.zeros_like(carry)
            for ref in small:
                ref[...] = jnp.zeros_like(ref)

        dx2v = dx2_ref[...]
        gt1v = gt1[...]
        dgt1_ref[...] += _colsum(dx2v * o1_ref[...])
        do1b = (gt1v * dx2v).astype(BF16)
        dy = _nt(do1b, wout_v[...])
        dwout_acc[...] += _tn(y_ref[...], do1b)

        mgv = mg[...]
        _, conv_vjp = _conv_branch_vjp(a1_ref[...], clg[...], clb[...], mgv[:, :D_HALF])
        da1, dclg, dclb, dmg_a = conv_vjp(dy[:, :D_HALF])
        dclg_ref[...] += dclg
        dclb_ref[...] += dclb
        gu = z_ref[:, 2 * D_HALF:3 * D_HALF]
        gv = z_ref[:, 3 * D_HALF:]
        spv = sp_ref[...]
        _, gate_vjp = _gate_branch_vjp(gu, spv, mgv[:, D_HALF:])
        dgu, dsp, dmg_g = gate_vjp(dy[:, D_HALF:])
        dmg_ref[...] += jnp.concatenate([dmg_a, dmg_g], axis=1)
        gvn, gv_vjp = _gv_norm_vjp(gv, vg[...], vb[...])
        gvnb = gvn.astype(BF16)
        dspb = dsp.astype(BF16)
        dgvn = _head_pair_matmul(wpt, dspb)
        dgv, dvg, dvb = gv_vjp(dgvn)
        dvg_ref[...] += dvg
        dvb_ref[...] += dvb
        lane = lax.broadcasted_iota(jnp.int32, (CHUNK, CHUNK), 1)
        dbs = jnp.zeros((CHUNK, D_HALF), F32)
        for n in range(t // CHUNK):
            rows = slice(n * CHUNK, (n + 1) * CHUNK)
            dbs = dbs + dsp[rows, :]
            for j in range(N_HEADS // 2):
                cols = slice(j * CHUNK, (j + 1) * CHUNK)
                blk = dspb[rows, cols]
                zero = jnp.zeros_like(blk)
                vblk = gvnb[rows, cols]
                dws_ref[2 * j] += _nt(jnp.where(lane < HEAD_DIM, blk, zero), vblk)
                dws_ref[2 * j + 1] += _nt(jnp.where(lane < HEAD_DIM, zero, blk), vblk)
        dbs_acc[...] += dbs

        h1, h1_vjp = _mod_norm_vjp(x_ref[...], g1[...], sc1[...], sh1[...])
        h1b = h1.astype(BF16)
        dh1 = jnp.zeros((t, D_MODEL), F32)
        for k, dzk in ((2, dgu), (3, dgv)):
            dzb = dzk.astype(BF16)
            dh1 = dh1 + _nt(dzb, win_v[k])
            dwin_acc[k] += _tn(h1b, dzb)

        ca = z_ref[:, :D_HALF]
        cg = z_ref[:, D_HALF:2 * D_HALF]
        sig = jax.nn.sigmoid(cg)
        a0 = ca * sig
        ext = jnp.concatenate([da1, carry[...]], axis=0)
        carry[...] = da1[:HALO]
        bank[0] = ext
        for b in range(1, 8):
            bank[b] = pltpu.roll(ext, t + HALO - b, axis=0)
        dcb_ref[...] += _colsum(da1)
        da0 = jnp.zeros((t, D_HALF), F32)
        for s in range(CONV_K):
            q, b = divmod(s, 8)
            shifted = bank[b, pl.ds(8 * q, t), :]
            da0 = da0 + shifted * cw[pl.ds(CONV_K - 1 - s, 1), :]
            dcw_ref[pl.ds(CONV_K - 1 - s, 1), :] += _colsum(shifted * a0)
        dca = da0 * sig
        dcg = da0 * ca * sig * (1.0 - sig)

        for k, dzk in ((0, dca), (1, dcg)):
            dzb = dzk.astype(BF16)
            dh1 = dh1 + _nt(dzb, win_v[k])
            dwin_acc[k] += _tn(h1b, dzb)
        dx, dg1, dsc1, dsh1 = h1_vjp(dh1)
        gx_ref[...] = dx2v + dx
        dg1_ref[...] += dg1
        dsc1_ref[...] += dsc1
        dsh1_ref[...] += dsh1

        @pl.when(i == n_tiles - 1)
        def _():
            for h in range(N_HEADS):
                dws_ref[h] = dws_ref[h] * mask_ref[...]
            head_of_lane = lax.broadcasted_iota(jnp.int32, (N_HEADS, D_HALF), 1) // HEAD_DIM
            pick = (head_of_lane == lax.broadcasted_iota(jnp.int32, (N_HEADS, D_HALF), 0)).astype(F32)
            dbs_ref[...] = lax.dot_general(pick, dbs_acc[...], NT_DIMS, precision=lax.Precision.HIGHEST,
                                           preferred_element_type=F32)
            for k in range(N_SHARD):
                win_v[k] = dwin_acc[k].astype(BF16)
            wout_v[...] = dwout_acc[...].astype(BF16)
            mx, my, mc = _coords()
            h_in, h_out = dwin_acc.shape[1] // 2, dwout_acc.shape[0] // (2 * N_SHARD)

            def in_rows(ref, k, which):
                return ref.at[k, pl.ds(pl.multiple_of(which * h_in, 16), h_in), :]

            def out_rows(ref, k, which):
                return ref.at[pl.ds(pl.multiple_of((2 * k + which) * h_out, 16), h_out), :]

            pairs = ((win_v, dwin_acc, lwin, in_rows, dwin_hbm, dwin16_hbm),
                     (wout_v, dwout_acc, lwout, out_rows, dwout_hbm, dwout16_hbm))
            swaps = [pltpu.make_async_remote_copy(
                src_ref=rows_of(v16, k, 1 - mc), dst_ref=land.at[k], send_sem=pair_send.at[w, k],
                recv_sem=pair_recv.at[w, k], device_id=(mx, my, 1 - mc), device_id_type=MESH)
                for w, (v16, _, land, rows_of, _, _) in enumerate(pairs) for k in range(N_SHARD)]
            for cp in swaps:
                cp.start()
            for cp in swaps:
                cp.wait()
            outs = []
            for w, (v16, acc, land, rows_of, half_hbm, half16_hbm) in enumerate(pairs):
                for k in range(N_SHARD):
                    total = rows_of(acc, k, mc)[...] + land[k].astype(F32)
                    rows_of(acc, k, 0)[...] = total
                    rows_of(v16, k, 0)[...] = total.astype(BF16)
                    outs.append(pltpu.make_async_copy(rows_of(acc, k, 0), half_hbm.at[k], sem.at[2 + 8 * w + k]))
                    outs.append(pltpu.make_async_copy(rows_of(v16, k, 0), half16_hbm.at[k], sem.at[6 + 8 * w + k]))
            for cp in outs:
                cp.start()
            for cp in outs:
                cp.wait()
            for rc in _scatter_copies(dwup16_hbm, land_hbm, send_sems, recv_sems):
                rc.wait()

    def rev(width):
        return pl.BlockSpec((t, width), lambda i: (n_tiles - 1 - i, 0))

    v1024 = jax.ShapeDtypeStruct((1, D_MODEL), F32)
    v512 = jax.ShapeDtypeStruct((1, D_HALF), F32)
    small_shapes = [v1024, v1024, v1024, v1024, jax.ShapeDtypeStruct((CONV_K, D_HALF), F32), v512, v512, v512, v512,
                    v512, v1024, jax.ShapeDtypeStruct((N_HEADS, CHUNK, CHUNK), F32),
                    jax.ShapeDtypeStruct((N_HEADS, CHUNK), F32)]
    n_peer = len(PEER_FLIPS)
    half_in = (N_SHARD, w_in_g.shape[1] // 2, w_in_g.shape[2])
    half_out = (N_SHARD, w_out_g.shape[0] // (2 * N_SHARD), w_out_g.shape[1])
    return pl.pallas_call(
        body, grid=(n_tiles,), name="bwd_mixer",
        in_specs=[rev(D_MODEL), rev(D_MODEL), rev(4 * D_HALF), rev(D_HALF), rev(D_HALF), rev(D_MODEL),
                  rev(D_MODEL)] + [_full(v.shape) for v in vecs]
        + [_full(conv_w.shape), _full(wpair.shape), _full(wpair_t.shape), _full(causal_mask.shape), ANY, ANY, ANY],
        out_specs=[rev(D_MODEL)] + [_full(s.shape) for s in small_shapes] + [ANY] * 5,
        out_shape=[jax.ShapeDtypeStruct((seq, D_MODEL), F32)] + small_shapes
        + [jax.ShapeDtypeStruct(half_in, F32), jax.ShapeDtypeStruct(half_out, F32), _land_shape(dwup16),
           jax.ShapeDtypeStruct(half_in, BF16), jax.ShapeDtypeStruct(half_out, BF16)],
        scratch_shapes=[pltpu.VMEM(w_in_g.shape, BF16), pltpu.VMEM(w_out_g.shape, BF16),
                        pltpu.VMEM(w_in_g.shape, F32), pltpu.VMEM(w_out_g.shape, F32),
                        pltpu.VMEM((HALO, D_HALF), F32), pltpu.VMEM((8, t + HALO, D_HALF), F32),
                        pltpu.VMEM((CHUNK, D_HALF), F32), pltpu.VMEM(half_in, BF16), pltpu.VMEM(half_out, BF16),
                        pltpu.SemaphoreType.DMA((2 + 4 * N_SHARD,)),
                        pltpu.SemaphoreType.DMA((n_peer,)), pltpu.SemaphoreType.DMA((n_peer,)),
                        pltpu.SemaphoreType.DMA((2, N_SHARD)), pltpu.SemaphoreType.DMA((2, N_SHARD))],
        compiler_params=_params(),
    )(dx2, x, z, a1, sp, yb, o1, *vecs, conv_w, wpair, wpair_t, causal_mask, w_in_g, w_out_g, dwup16)


def _gmlp_operands(gm_ws, gm_bs):
    mask = jnp.tril(jnp.ones((CHUNK, CHUNK), F32))
    ws = gm_ws * mask[None]
    wpair = ws.reshape(N_HEADS // 2, 2 * CHUNK, CHUNK).astype(BF16)
    wpair_t = jnp.swapaxes(ws, 1, 2).reshape(N_HEADS // 2, 2 * CHUNK, CHUNK).astype(BF16)
    bs_full = jnp.repeat(jnp.transpose(gm_bs), HEAD_DIM, axis=1)
    return wpair, wpair_t, bs_full, mask


def _local_step(x, target, mod, p, w_in_g, w_out_g, w_up_part, w_down_part):
    sh1, sc1, gt1, sh2, sc2, gt2 = [mod[:, k * D_MODEL:(k + 1) * D_MODEL] for k in range(6)]
    vec = dict(p, sh1=sh1, sc1=sc1, gt1=gt1, sh2=sh2, sc2=sc2, gt2=gt2)
    wpair, wpair_t, bs_full, mask = _gmlp_operands(p["gm_ws"], p["gm_bs"])

    (z, a1, sp, yb, o1, x2), (w_up_g, w_down_g) = _fwd_mixer(
        x, vec, p["conv_dw_w"], wpair, bs_full, w_in_g, w_out_g, [w_up_part, w_down_part])
    w_down_g = w_down_g.reshape(D_FF, D_MODEL)
    to_inter, to_natural = _interleave_matrices()
    du, dx2, d_ffn_w, d_ffn_b, d_fg, d_gt2, d_g2, d_sc2, d_sh2, loss, d_wd, d_wd16 = _ffn(
        x2, target, p["norm2_gain"], sc2, sh2, p["ffn_dw_w"], p["ffn_dw_b"], gt2, p["final_gain"], w_up_g, w_down_g,
        to_inter, to_natural)
    by_shard = (N_SHARD, -1, D_MODEL)
    d_wup, d_wup16, land_wd = _bwd_up(du, x2, p["norm2_gain"], sc2, sh2, d_wd16.reshape(by_shard))
    (gx, d_g1, d_sc1, d_sh1, d_gt1, d_cw, d_cb, d_clg, d_clb, d_vg, d_vb, d_mg, d_ws, d_bs, d_win, d_wout, land_wup,
     d_win16, d_wout16) = _bwd_mixer(dx2, x, z, a1, sp, yb, o1, vec, p["conv_dw_w"], wpair, wpair_t, mask, w_in_g,
                                     w_out_g, d_wup16)
    d_mod = _pack([d_sh1, d_sc1, d_gt1, d_sh2, d_sc2, d_gt2], 6).reshape(1, 6 * D_MODEL)
    grads = dict(norm1_gain=d_g1, conv_dw_w=d_cw, conv_dw_b=d_cb, conv_ln_g=d_clg, conv_ln_b=d_clb, gm_ln_g=d_vg,
                 gm_ln_b=d_vb, gm_ws=d_ws, gm_bs=d_bs, mix_out_gain=d_mg, norm2_gain=d_g2, ffn_dw_w=d_ffn_w,
                 ffn_dw_b=d_ffn_b, final_gain=d_fg, w_in=d_win, w_out=d_wout, w_up=d_wup, w_down=d_wd.reshape(by_shard))
    in_flight = dict(w_in16=d_win16, w_out16=d_wout16, land_w_up=land_wup, land_w_down=land_wd)
    return gx, grads, d_mod, loss, in_flight


MESH = pl.DeviceIdType.MESH
VMEM_SPEC = pl.BlockSpec(memory_space=pltpu.VMEM)
PEER_FLIPS = [(a, b, d) for a in (0, 1) for b in (0, 1) for d in (0, 1)][1:]
CHIP_FLIPS = [(1, 0), (0, 1), (1, 1)]


def _coords():
    return lax.axis_index("x"), lax.axis_index("y"), lax.axis_index("c")


def _flip(v, bit):
    return 1 - v if bit else v


def _rows8(block):
    return pl.ds(pl.multiple_of(8 * block, 8), 8)


def _ada_steps(c_ref, w_ref, b_ref, call_ref, mod_ref, cpad, modall, send_sems, recv_sems):
    x, y, c = _coords()
    me = 4 * x + 2 * y + c
    cpad[...] = jnp.zeros_like(cpad)
    cpad[pl.ds(0, 1), :] = c_ref[...]

    def gather_copy(j, flip):
        peer = (_flip(x, flip[0]), _flip(y, flip[1]), _flip(c, flip[2]))
        return pltpu.make_async_remote_copy(
            src_ref=cpad, dst_ref=call_ref.at[_rows8(me)], send_sem=send_sems.at[j], recv_sem=recv_sems.at[j],
            device_id=peer, device_id_type=MESH)

    def piece_copy(j, flip):
        tx, ty = _flip(x, flip[0]), _flip(y, flip[1])
        return pltpu.make_async_remote_copy(
            src_ref=modall.at[_rows8(4 * tx + 2 * ty + c)], dst_ref=mod_ref.at[_rows8(2 * x + y)],
            send_sem=send_sems.at[len(PEER_FLIPS) + j], recv_sem=recv_sems.at[len(PEER_FLIPS) + j],
            device_id=(tx, ty, c), device_id_type=MESH)

    copies = [gather_copy(j, f) for j, f in enumerate(PEER_FLIPS)]
    for cp in copies:
        cp.start()
    call_ref[_rows8(me), :] = cpad[...]

    def middle():
        for cp in copies:
            cp.wait_recv()
        for cp in copies:
            cp.wait_send()
        cv = call_ref[...]
        c_act = (cv * jax.nn.sigmoid(cv)).astype(BF16)
        modall[...] = _nn(c_act, w_ref[...].astype(BF16)) + b_ref[...]
        for j, f in enumerate(CHIP_FLIPS):
            piece_copy(j, f).start()
        mod_ref[_rows8(2 * x + y), :] = modall[_rows8(me), :]

    def finish():
        for j, f in enumerate(CHIP_FLIPS):
            piece_copy(j, f).wait_recv()
        for j, f in enumerate(CHIP_FLIPS):
            piece_copy(j, f).wait_send()

    return middle, finish


def _gather_weights(shards, filters, n_now, c_row, w_ada_sh, b_ada_sh):
    n = len(shards)
    nf = len(filters)
    ada_cols = w_ada_sh.shape[1]

    def body(*refs):
        ins, f_ins, ada_ins = refs[:n], refs[n:n + nf], refs[n + nf:n + nf + 3]
        refs = refs[n + nf + 3:]
        outs, f_outs, ada_outs = refs[:n], refs[n:n + nf], refs[n + nf:n + nf + 2]
        refs = refs[n + nf + 2:]
        stage = refs[:n]
        send_sems, recv_sems, local_sems, f_send_sems, f_recv_sems, cpad, modall, ada_send, ada_recv = refs[n:]
        ada_middle, ada_finish = _ada_steps(*ada_ins, *ada_outs, cpad, modall, ada_send, ada_recv)
        x, y, c = _coords()
        k = 2 * x + y
        sibling = (x, y, 1 - c)

        def filter_copy(w, j, slot):
            tx, ty = _flip(x, CHIP_FLIPS[j][0]), _flip(y, CHIP_FLIPS[j][1])
            return pltpu.make_async_remote_copy(
                src_ref=f_ins[w], dst_ref=f_outs[w].at[slot], send_sem=f_send_sems.at[w, j],
                recv_sem=f_recv_sems.at[w, j], device_id=(tx, ty, c), device_id_type=MESH)

        def half(w, which):
            h = shards[w].shape[0] // 2
            return pl.ds(pl.multiple_of(which * h, 16), h)

        def ici_copy(w, j, src, slot):
            tx, ty = _flip(x, CHIP_FLIPS[j][0]), _flip(y, CHIP_FLIPS[j][1])
            return pltpu.make_async_remote_copy(
                src_ref=src, dst_ref=outs[w].at[slot, half(w, c)], send_sem=send_sems.at[w, j],
                recv_sem=recv_sems.at[w, j], device_id=(tx, ty, c), device_id_type=MESH)

        def d2d_copy(w, j, slot, which):
            rows = outs[w].at[slot, half(w, which)]
            return pltpu.make_async_remote_copy(
                src_ref=rows, dst_ref=rows, send_sem=send_sems.at[w, len(CHIP_FLIPS) + j],
                recv_sem=recv_sems.at[w, len(CHIP_FLIPS) + j], device_id=sibling, device_id_type=MESH)

        def chip_of(j):
            return 2 * _flip(x, CHIP_FLIPS[j][0]) + _flip(y, CHIP_FLIPS[j][1])

        local, first, passed = [], [], []
        for w in range(nf):
            local.append(pltpu.make_async_copy(f_ins[w], f_outs[w].at[k], local_sems.at[n + w]))
            local[-1].start()
            for j in range(len(CHIP_FLIPS)):
                first.append(filter_copy(w, j, k))
                first[-1].start()
        for w in range(n):
            stage[w][...] = ins[w][...].astype(BF16)
            local.append(pltpu.make_async_copy(stage[w], outs[w].at[k], local_sems.at[w]))
            local[-1].start()
            if w < n_now:
                for j in range(len(CHIP_FLIPS)):
                    first.append(ici_copy(w, j, stage[w].at[half(w, c)], k))
                    first[-1].start()
        ada_middle()
        for w in range(nf):
            for j in range(len(CHIP_FLIPS)):
                filter_copy(w, j, chip_of(j)).wait_recv()
        for w in range(n_now):
            for j in range(len(CHIP_FLIPS)):
                ici_copy(w, j, stage[w].at[half(w, c)], chip_of(j)).wait_recv()
                passed.append(d2d_copy(w, j, chip_of(j), c))
                passed[-1].start()
        for w in range(n_now):
            for j in range(len(CHIP_FLIPS)):
                d2d_copy(w, j, chip_of(j), 1 - c).wait_recv()
        for cp in first + passed:
            cp.wait_send()
        for cp in local:
            cp.wait()
        ada_finish()

    sem_shape = (n_now, 2 * len(CHIP_FLIPS))
    f_sem_shape = (nf, len(CHIP_FLIPS))
    n_ada_sem = len(PEER_FLIPS) + len(CHIP_FLIPS)
    outs = pl.pallas_call(
        body, name="gather_weights",
        in_specs=[VMEM_SPEC] * (n + nf + 3), out_specs=[ANY] * (n + nf) + [VMEM_SPEC, VMEM_SPEC],
        out_shape=[jax.ShapeDtypeStruct((N_SHARD,) + s.shape, BF16) for s in shards]
        + [jax.ShapeDtypeStruct((N_SHARD,) + s.shape, F32) for s in filters]
        + [jax.ShapeDtypeStruct((8 * N_DEV, D_MODEL), F32), jax.ShapeDtypeStruct((8 * N_SHARD, ada_cols), F32)],
        scratch_shapes=[pltpu.VMEM(s.shape, BF16) for s in shards]
        + [pltpu.SemaphoreType.DMA(sem_shape), pltpu.SemaphoreType.DMA(sem_shape), pltpu.SemaphoreType.DMA((n + nf,)),
           pltpu.SemaphoreType.DMA(f_sem_shape), pltpu.SemaphoreType.DMA(f_sem_shape),
           pltpu.VMEM((8, D_MODEL), F32), pltpu.VMEM((8 * N_DEV, ada_cols), F32),
           pltpu.SemaphoreType.DMA((n_ada_sem,)), pltpu.SemaphoreType.DMA((n_ada_sem,))],
        compiler_params=pltpu.CompilerParams(vmem_limit_bytes=VMEM_LIMIT_BYTES),
    )(*shards, *filters, c_row, w_ada_sh, b_ada_sh)
    return outs[:n], outs[n:n + nf], outs[n + nf], outs[n + nf + 1]


def _final_comm(srcs16, small):
    n = len(srcs16)
    rows = small.shape[0]
    half = rows // 2

    def body(*refs):
        srcs, small_ref = refs[:n], refs[n]
        lands, small_out = refs[n + 1:2 * n + 1], refs[2 * n + 1]
        chip_sum, got_c, got_x, got_y, part_x, send_sems, recv_sems, small_send_sems, small_recv_sems = refs[2 * n + 2:]
        x, y, c = _coords()
        sibling = (x, y, 1 - c)
        mine = pl.ds(pl.multiple_of(c * half, 8), half)
        copies = []
        for w in range(n):
            for j, flip in enumerate(CHIP_FLIPS):
                tx, ty = _flip(x, flip[0]), _flip(y, flip[1])
                copies.append(pltpu.make_async_remote_copy(
                    src_ref=srcs[w].at[2 * tx + ty], dst_ref=lands[w].at[j], send_sem=send_sems.at[w, j],
                    recv_sem=recv_sems.at[w, j], device_id=(tx, ty, c), device_id_type=MESH))
        for cp in copies:
            cp.start()

        def exchange(stage, src, dst, peer):
            rc = pltpu.make_async_remote_copy(
                src_ref=src, dst_ref=dst, send_sem=small_send_sems.at[stage], recv_sem=small_recv_sems.at[stage],
                device_id=peer, device_id_type=MESH)
            rc.start()
            rc.wait()

        exchange(0, small_ref, got_c, sibling)
        chip_sum[...] = small_ref[...] + got_c[...]
        exchange(1, chip_sum.at[mine], got_x, (1 - x, y, c))
        part_x[...] = chip_sum[mine, :] + got_x[...]
        exchange(2, part_x, got_y, (x, 1 - y, c))
        small_out[mine, :] = part_x[...] + got_y[...]
        exchange(3, small_out.at[mine], small_out.at[mine], sibling)
        for cp in copies:
            cp.wait()

    n_chip = len(CHIP_FLIPS)
    half_shape = (half, small.shape[1])
    outs = pl.pallas_call(
        body, name="final_comm",
        in_specs=[ANY] * n + [VMEM_SPEC], out_specs=[ANY] * n + [VMEM_SPEC],
        out_shape=[jax.ShapeDtypeStruct((n_chip,) + a.shape[1:], BF16) for a in srcs16]
        + [jax.ShapeDtypeStruct(small.shape, F32)],
        scratch_shapes=[pltpu.VMEM(small.shape, F32), pltpu.VMEM(small.shape, F32), pltpu.VMEM(half_shape, F32),
                        pltpu.VMEM(half_shape, F32), pltpu.VMEM(half_shape, F32),
                        pltpu.SemaphoreType.DMA((n, n_chip)), pltpu.SemaphoreType.DMA((n, n_chip)),
                        pltpu.SemaphoreType.DMA((4,)), pltpu.SemaphoreType.DMA((4,))],
        compiler_params=pltpu.CompilerParams(vmem_limit_bytes=VMEM_LIMIT_BYTES),
    )(*srcs16, small)
    return outs[:n], outs[n]


ADD_CHUNKS = 4


def _scatter_sum(pos, owns, lands):
    n = len(owns)

    def specs(own_shape, land_shape):
        peers, rows, cols = land_shape
        pick = 1 if own_shape[1] == 2 * rows else 0
        if cols % (128 * ADD_CHUNKS) == 0:
            blk = (rows, cols // ADD_CHUNKS)
            return (pl.BlockSpec((1,) + blk, lambda i, p: (2 * p[0] + p[1], pick * p[2], i)),
                    pl.BlockSpec((peers,) + blk, lambda i, p: (0, 0, i)),
                    pl.BlockSpec((1,) + blk, lambda i, p: (p[2], 0, i)))
        blk = (rows // ADD_CHUNKS, cols)
        return (pl.BlockSpec((1,) + blk, lambda i, p: (2 * p[0] + p[1], pick * p[2] * ADD_CHUNKS + i, 0)),
                pl.BlockSpec((peers,) + blk, lambda i, p: (0, i, 0)),
                pl.BlockSpec((1,) + blk, lambda i, p: (p[2], i, 0)))

    def body(pos_ref, *refs):
        for idx in range(n):
            own, land, out = refs[idx], refs[n + idx], refs[2 * n + idx]
            total = own[0]
            for f in range(land.shape[0]):
                total = total + land[f].astype(F32)
            out[0] = total

    all_specs = [specs(o.shape, l.shape) for o, l in zip(owns, lands)]
    return pl.pallas_call(
        body, name="scatter_sum",
        grid_spec=pltpu.PrefetchScalarGridSpec(
            num_scalar_prefetch=1, grid=(ADD_CHUNKS,),
            in_specs=[s[0] for s in all_specs] + [s[1] for s in all_specs], out_specs=[s[2] for s in all_specs]),
        out_shape=[jax.ShapeDtypeStruct((2,) + l.shape[1:], F32) for l in lands],
        compiler_params=_params(),
    )(pos, *owns, *lands)


def _swap_halves(halves):
    n = len(halves)

    def body(*refs):
        ins, outs = refs[:n], refs[n:2 * n]
        send_sems, recv_sems = refs[2 * n:]
        x, y, c = _coords()
        copies = [pltpu.make_async_remote_copy(
            src_ref=ins[idx].at[pl.ds(c, 1)], dst_ref=outs[idx].at[pl.ds(c, 1)], send_sem=send_sems.at[idx],
            recv_sem=recv_sems.at[idx], device_id=(x, y, 1 - c), device_id_type=MESH) for idx in range(n)]
        for cp in copies:
            cp.start()
        for cp in copies:
            cp.wait()

    return pl.pallas_call(
        body, name="swap_halves",
        in_specs=[ANY] * n, out_specs=[ANY] * n, input_output_aliases={idx: idx for idx in range(n)},
        out_shape=[jax.ShapeDtypeStruct(a.shape, F32) for a in halves],
        scratch_shapes=[pltpu.SemaphoreType.DMA((n,)), pltpu.SemaphoreType.DMA((n,))],
    )(*halves)


def _adamw_math(w, g, m, v):
    m = ADAM_B1 * m + (1.0 - ADAM_B1) * g
    v = ADAM_B2 * v + (1.0 - ADAM_B2) * jnp.square(g)
    m_hat = m / (1.0 - ADAM_B1 ** ADAM_STEP)
    v_hat = v / (1.0 - ADAM_B2 ** ADAM_STEP)
    delta = -ADAM_LR * (m_hat / (jnp.sqrt(v_hat) + ADAM_EPS) + ADAM_WD * w)
    return delta, m, v


def _adamw_group(ws, gs, ms, vs, n_steps):
    n = len(ws)

    def body(*refs):
        w_refs, g_refs, m_refs, v_refs = (refs[q * n:(q + 1) * n] for q in range(4))
        d_outs, m_outs, v_outs = (refs[(4 + q) * n:(5 + q) * n] for q in range(3))
        for idx in range(n):
            d_outs[idx][...], m_outs[idx][...], v_outs[idx][...] = _adamw_math(
                w_refs[idx][...], g_refs[idx][...], m_refs[idx][...], v_refs[idx][...])

    specs = [pl.BlockSpec((w.shape[0] // n_steps, w.shape[1]), lambda i: (i, 0)) for w in ws]
    shapes = [jax.ShapeDtypeStruct(w.shape, F32) for w in ws]
    outs = pl.pallas_call(
        body, grid=(n_steps,), name="adamw_projections", in_specs=specs * 4, out_specs=specs * 3,
        out_shape=shapes * 3, compiler_params=_params(),
    )(*ws, *gs, *ms, *vs)
    return outs[:n], outs[n:2 * n], outs[2 * n:]


def _adamw_many(ws, gs, ms, vs):
    n = len(ws)

    def body(*refs):
        w_refs, g_refs, m_refs, v_refs = (refs[q * n:(q + 1) * n] for q in range(4))
        d_outs, m_outs, v_outs = (refs[(4 + q) * n:(5 + q) * n] for q in range(3))
        for idx in range(n):
            d_outs[idx][...], m_outs[idx][...], v_outs[idx][...] = _adamw_math(
                w_refs[idx][...], g_refs[idx][...], m_refs[idx][...], v_refs[idx][...])

    shapes = [jax.ShapeDtypeStruct(w.shape, F32) for w in ws]
    outs = pl.pallas_call(
        body, name="adamw_small", in_specs=[VMEM_SPEC] * (4 * n), out_specs=[VMEM_SPEC] * (3 * n),
        out_shape=shapes * 3, compiler_params=pltpu.CompilerParams(vmem_limit_bytes=VMEM_LIMIT_BYTES),
    )(*ws, *gs, *ms, *vs)
    return outs[:n], outs[n:2 * n], outs[2 * n:]


def _adamw_ada(c_all16, dmod16, w, m, v, block_rows):
    rows, cols = w.shape

    def body(c_ref, dm_ref, w_ref, m_ref, v_ref, g_out, d_out, m_out, v_out):
        cv = c_ref[...]
        g = _tn((cv * jax.nn.sigmoid(cv)).astype(BF16), dm_ref[...].astype(BF16))
        g_out[...] = g
        d_out[...], m_out[...], v_out[...] = _adamw_math(w_ref[...], g, m_ref[...], v_ref[...])

    spec = pl.BlockSpec((block_rows, cols), lambda i: (i, 0))
    shape = jax.ShapeDtypeStruct((rows, cols), F32)
    return pl.pallas_call(
        body, grid=(rows // block_rows,), name="adamw_w_ada",
        in_specs=[pl.BlockSpec((16, block_rows), lambda i: (0, i)), _full(dmod16.shape), spec, spec, spec],
        out_specs=[spec] * 4, out_shape=[shape] * 4, compiler_params=_params(),
    )(c_all16, dmod16, w, m, v)


SMALL_REPLICATED = ["b_ada", "norm1_gain", "conv_dw_b", "conv_ln_g", "conv_ln_b", "gm_ln_g", "gm_ln_b", "gm_ws", "gm_bs",
                    "mix_out_gain", "norm2_gain", "ffn_dw_b", "final_gain"]
SMALL_SHARDED = ["conv_dw_w", "ffn_dw_w"]
PACK_ROWS = 256
WEIGHT_ORDER = ["w_ada", "b_ada", "norm1_gain", "w_in", "conv_dw_w", "conv_dw_b", "conv_ln_g", "conv_ln_b", "gm_ln_g",
                "gm_ln_b", "gm_ws", "gm_bs", "mix_out_gain", "w_out", "norm2_gain", "w_up", "ffn_dw_w", "ffn_dw_b",
                "w_down", "final_gain"]


def _pack(parts, rows):
    total = rows * D_MODEL
    flat, offset = None, 0
    for a in parts:
        piece = jnp.pad(a.reshape(-1), (offset, total - offset - a.size))
        flat = piece if flat is None else flat + piece
        offset += a.size
    return flat.reshape(rows, D_MODEL)


def _unpack(packed, shapes):
    flat = packed.reshape(-1)
    out, pos = [], 0
    for s in shapes:
        size = 1
        for d in s:
            size *= d
        out.append(flat[pos:pos + size].reshape(s))
        pos += size
    return out


def kernel(x, c, w_ada, b_ada, norm1_gain, w_in, conv_dw_w, conv_dw_b, conv_ln_g, conv_ln_b, gm_ln_g, gm_ln_b, gm_ws, gm_bs, mix_out_gain, w_out, norm2_gain, w_up, ffn_dw_w, ffn_dw_b, w_down, final_gain, loss_target, m_w_ada, m_b_ada, m_norm1_gain, m_w_in, m_conv_dw_w, m_conv_dw_b, m_conv_ln_g, m_conv_ln_b, m_gm_ln_g, m_gm_ln_b, m_gm_ws, m_gm_bs, m_mix_out_gain, m_w_out, m_norm2_gain, m_w_up, m_ffn_dw_w, m_ffn_dw_b, m_w_down, m_final_gain, v_w_ada, v_b_ada, v_norm1_gain, v_w_in, v_conv_dw_w, v_conv_dw_b, v_conv_ln_g, v_conv_ln_b, v_gm_ln_g, v_gm_ln_b, v_gm_ws, v_gm_bs, v_mix_out_gain, v_w_out, v_norm2_gain, v_w_up, v_ffn_dw_w, v_ffn_dw_b, v_w_down, v_final_gain):
    weights = dict(w_ada=w_ada, b_ada=b_ada, norm1_gain=norm1_gain, w_in=w_in, conv_dw_w=conv_dw_w, conv_dw_b=conv_dw_b,
                   conv_ln_g=conv_ln_g, conv_ln_b=conv_ln_b, gm_ln_g=gm_ln_g, gm_ln_b=gm_ln_b, gm_ws=gm_ws, gm_bs=gm_bs,
                   mix_out_gain=mix_out_gain, w_out=w_out, norm2_gain=norm2_gain, w_up=w_up, ffn_dw_w=ffn_dw_w,
                   ffn_dw_b=ffn_dw_b, w_down=w_down, final_gain=final_gain)
    mom1 = dict(w_ada=m_w_ada, b_ada=m_b_ada, norm1_gain=m_norm1_gain, w_in=m_w_in, conv_dw_w=m_conv_dw_w,
                conv_dw_b=m_conv_dw_b, conv_ln_g=m_conv_ln_g, conv_ln_b=m_conv_ln_b, gm_ln_g=m_gm_ln_g, gm_ln_b=m_gm_ln_b,
                gm_ws=m_gm_ws, gm_bs=m_gm_bs, mix_out_gain=m_mix_out_gain, w_out=m_w_out, norm2_gain=m_norm2_gain,
                w_up=m_w_up, ffn_dw_w=m_ffn_dw_w, ffn_dw_b=m_ffn_dw_b, w_down=m_w_down, final_gain=m_final_gain)
    mom2 = dict(w_ada=v_w_ada, b_ada=v_b_ada, norm1_gain=v_norm1_gain, w_in=v_w_in, conv_dw_w=v_conv_dw_w,
                conv_dw_b=v_conv_dw_b, conv_ln_g=v_conv_ln_g, conv_ln_b=v_conv_ln_b, gm_ln_g=v_gm_ln_g, gm_ln_b=v_gm_ln_b,
                gm_ws=v_gm_ws, gm_bs=v_gm_bs, mix_out_gain=v_mix_out_gain, w_out=v_w_out, norm2_gain=v_norm2_gain,
                w_up=v_w_up, ffn_dw_w=v_ffn_dw_w, ffn_dw_b=v_ffn_dw_b, w_down=v_w_down, final_gain=v_final_gain)
    shard = 2 * lax.axis_index("x") + lax.axis_index("y")
    me = 2 * shard + lax.axis_index("c")

    ada_cols = w_ada.shape[2]
    b_ada_sh = lax.dynamic_slice(b_ada, (0, shard * ada_cols), (1, ada_cols))
    (w_in_g, w_out_g, w_up_part, w_down_part), (conv_w_g, ffn_w_g), c_all64, mod32 = _gather_weights(
        [w_in[0], w_out[0], w_up[0], w_down[0]], [conv_dw_w[0], ffn_dw_w[0]], 2, c, w_ada[0], b_ada_sh)
    c_all = c_all64[::8]
    mod = mod32[::8].reshape(1, N_SHARD * ada_cols)
    conv_w_full = jnp.transpose(conv_w_g, (1, 0, 2)).reshape(CONV_K, D_HALF)
    ffn_w_full = jnp.transpose(ffn_w_g, (1, 0, 2)).reshape(FFN_K, 2 * D_FF)

    p = dict(norm1_gain=norm1_gain, conv_dw_w=conv_w_full, conv_dw_b=conv_dw_b, conv_ln_g=conv_ln_g,
             conv_ln_b=conv_ln_b, gm_ln_g=gm_ln_g, gm_ln_b=gm_ln_b, gm_ws=gm_ws[0], gm_bs=gm_bs[0],
             mix_out_gain=mix_out_gain, norm2_gain=norm2_gain, ffn_dw_w=ffn_w_full, ffn_dw_b=ffn_dw_b,
             final_gain=final_gain[None])
    grad_x, g, d_mod, loss, in_flight = _local_step(
        x[0], loss_target[0], mod, p, w_in_g, w_out_g.reshape(D_MODEL, D_MODEL), w_up_part, w_down_part)

    n_mod = d_mod.shape[1]
    dmod_rows = lax.dynamic_update_slice(jnp.zeros((N_DEV, n_mod), F32), d_mod, (me, 0))
    g["b_ada"] = d_mod
    small = _pack([g[k] for k in SMALL_REPLICATED] + [g[k] for k in SMALL_SHARDED] + [dmod_rows, loss[0, :1]], PACK_ROWS)
    (land_w_in, land_w_out), small = _final_comm([in_flight["w_in16"], in_flight["w_out16"]], small)
    pos = jnp.stack(_coords()).astype(jnp.int32)
    halves = _scatter_sum(pos, [g["w_in"], g["w_out"], g["w_up"], g["w_down"]],
                          [land_w_in, land_w_out, in_flight["land_w_up"], in_flight["land_w_down"]])
    full = _swap_halves(halves)
    grads = dict(w_in=full[0].reshape(w_in.shape[1:]), w_out=full[1].reshape(w_out.shape[1:]),
                 w_up=full[2].reshape(w_up.shape[1:]), w_down=full[3].reshape(w_down.shape[1:]))

    small_shapes = ([weights[k].shape for k in SMALL_REPLICATED] + [(CONV_K, D_HALF), (FFN_K, 2 * D_FF)]
                    + [(N_DEV, n_mod), (1,)])
    *small_grads, conv_w_grad, ffn_w_grad, dmod_all, loss_sum = _unpack(small, small_shapes)
    grads.update(zip(SMALL_REPLICATED, small_grads))
    grads["conv_dw_w"] = lax.dynamic_slice(conv_w_grad, (0, shard * conv_dw_w.shape[2]), conv_dw_w.shape[1:])[None]
    grads["ffn_dw_w"] = lax.dynamic_slice(ffn_w_grad, (0, shard * ffn_dw_w.shape[2]), ffn_dw_w.shape[1:])[None]

    delta, new_m, new_v = {}, {}, {}
    projections = ["w_in", "w_out", "w_up", "w_down"]
    group_out = _adamw_group([weights[k][0] for k in projections], [grads[k] for k in projections],
                             [mom1[k][0] for k in projections], [mom2[k][0] for k in projections], n_steps=4)
    for d, arrs in zip((delta, new_m, new_v), group_out):
        d.update({k: a[None] for k, a in zip(projections, arrs)})
    for k in projections:
        grads[k] = grads[k][None]
    dmod_sh = lax.dynamic_slice(dmod_all, (0, shard * ada_cols), (N_DEV, ada_cols))
    pad8 = ((0, 16 - N_DEV), (0, 0))
    grads["w_ada"], delta["w_ada"], new_m["w_ada"], new_v["w_ada"] = [a[None] for a in _adamw_ada(
        jnp.pad(c_all, pad8), jnp.pad(dmod_sh, pad8), w_ada[0], m_w_ada[0], v_w_ada[0], 256)]
    small_names = SMALL_REPLICATED + SMALL_SHARDED

    def two_d(a):
        return a.reshape(1, -1) if a.ndim == 1 else a

    small_out = _adamw_many(*[[two_d(d[k]) for k in small_names] for d in (weights, grads, mom1, mom2)])
    for d, arrs in zip((delta, new_m, new_v), small_out):
        d.update({k: a.reshape(weights[k].shape) for k, a in zip(small_names, arrs)})

    return (loss_sum.reshape(()), grad_x[None], *[grads[k] for k in WEIGHT_ORDER], *[delta[k] for k in WEIGHT_ORDER],
            *[new_m[k] for k in WEIGHT_ORDER], *[new_v[k] for k in WEIGHT_ORDER])
```

```python
import jax
import jax.numpy as jnp
from jax import lax
from jax.experimental import pallas as pl
from jax.experimental.pallas import tpu as pltpu

F32 = jnp.float32
BF16 = jnp.bfloat16

D_MODEL = 1024
D_HALF = 512
D_FF = 2816
CONV_K = 31
FFN_K = 3
CHUNK = 128
N_HEADS = 8
HEAD_DIM = 64
N_SHARD = 4
N_DEV = 8
RMS_EPS = 1e-6
LN_EPS = 1e-5
ADAM_LR, ADAM_B1, ADAM_B2, ADAM_EPS, ADAM_WD, ADAM_STEP = 0.001, 0.9, 0.999, 1e-08, 0.01, 10

TILE = 256
HALO = 32
FFN_HALO = 16
FFN_BLK = 256
UP_SHARD = 2 * D_FF // N_SHARD
VMEM_LIMIT_BYTES = 56 * 1024 * 1024
FFN_VMEM_LIMIT_BYTES = 58 * 1024 * 1024

ANY = pl.BlockSpec(memory_space=pl.ANY)
NT_DIMS = (((1,), (1,)), ((), ()))
TN_DIMS = (((0,), (0,)), ((), ()))


def _full(shape):
    return pl.BlockSpec(shape, lambda i: (0,) * len(shape))


def _nn(a, b):
    return jnp.dot(a, b, preferred_element_type=F32)


def _nt(a, b):
    return lax.dot_general(a, b, NT_DIMS, preferred_element_type=F32)


def _tn(a, b):
    return lax.dot_general(a, b, TN_DIMS, preferred_element_type=F32)


def _colsum(a):
    return jnp.sum(a, axis=0, keepdims=True)


def _params(semantics=("arbitrary",)):
    return pltpu.CompilerParams(dimension_semantics=semantics, vmem_limit_bytes=VMEM_LIMIT_BYTES)


def _rms(v, gain):
    return v * lax.rsqrt(jnp.mean(v * v, axis=-1, keepdims=True) + RMS_EPS) * gain


def _layer_norm(v, gain, bias):
    mu = jnp.mean(v, axis=-1, keepdims=True)
    var = jnp.mean(jnp.square(v - mu), axis=-1, keepdims=True)
    return (v - mu) * lax.rsqrt(var + LN_EPS) * gain + bias


def _mod_norm(v, gain, scale, shift):
    return _rms(v, gain) * (1.0 + scale) + shift


def _conv_branch(a1, ln_g, ln_b, out_gain):
    a2 = _layer_norm(a1, ln_g, ln_b)
    return _rms(a2 * jax.nn.sigmoid(a2), out_gain)


def _gate_branch(gu, sp, out_gain):
    return _rms(jax.nn.gelu(gu) * sp, out_gain)


def _gv_norm(gv, ln_g, ln_b):
    return _layer_norm(jax.nn.gelu(gv), ln_g, ln_b)


def _rms_parts(v):
    r = lax.rsqrt(jnp.mean(v * v, axis=-1, keepdims=True) + RMS_EPS)
    return v * r, r


def _rms_back(dn, n, r):
    return r * (dn - n * jnp.mean(dn * n, axis=-1, keepdims=True))


def _ln_parts(v):
    mu = jnp.mean(v, axis=-1, keepdims=True)
    rs = lax.rsqrt(jnp.mean(jnp.square(v - mu), axis=-1, keepdims=True) + LN_EPS)
    return (v - mu) * rs, rs


def _ln_back(dn, n, rs):
    return rs * (dn - jnp.mean(dn, axis=-1, keepdims=True) - n * jnp.mean(dn * n, axis=-1, keepdims=True))


GELU_C = 0.7978845608028654
GELU_A = 0.044715


def _gelu_parts(v):
    v2 = v * v
    th = jnp.tanh(GELU_C * (v + GELU_A * (v2 * v)))
    cdf = 0.5 * (1.0 + th)
    return v * cdf, cdf + (0.5 * GELU_C) * v * (1.0 - th * th) * (1.0 + (3.0 * GELU_A) * v2)


def _rms_vjp(v, gain):
    n, r = _rms_parts(v)
    return n * gain, lambda dy: (_rms_back(dy * gain, n, r), _colsum(dy * n))


def _mod_norm_vjp(v, gain, scale, shift):
    n, r = _rms_parts(v)

    def back(dy):
        q = _colsum(dy * n)
        return _rms_back(dy * (gain * (1.0 + scale)), n, r), q * (1.0 + scale), q * gain, _colsum(dy)

    return n * gain * (1.0 + scale) + shift, back


def _conv_branch_vjp(a1, ln_g, ln_b, out_gain):
    n1, rs1 = _ln_parts(a1)
    a2 = n1 * ln_g + ln_b
    s = jax.nn.sigmoid(a2)
    a3 = a2 * s
    n3, r3 = _rms_parts(a3)

    def back(dy):
        da2 = _rms_back(dy * out_gain, n3, r3) * (s + a3 * (1.0 - s))
        return _ln_back(da2 * ln_g, n1, rs1), _colsum(da2 * n1), _colsum(da2), _colsum(dy * n3)

    return n3 * out_gain, back


def _gate_branch_vjp(gu, sp, out_gain):
    ge, dge = _gelu_parts(gu)
    n, r = _rms_parts(ge * sp)

    def back(dy):
        dg = _rms_back(dy * out_gain, n, r)
        return dg * sp * dge, dg * ge, _colsum(dy * n)

    return n * out_gain, back


def _gv_norm_vjp(gv, ln_g, ln_b):
    ge, dge = _gelu_parts(gv)
    n, rs = _ln_parts(ge)
    return n * ln_g + ln_b, lambda dy: (_ln_back(dy * ln_g, n, rs) * dge, _colsum(dy * n), _colsum(dy))


def _head_pair_matmul(wp_ref, v):
    lane = lax.broadcasted_iota(jnp.int32, (CHUNK, CHUNK), 1)
    rows = []
    for n in range(v.shape[0] // CHUNK):
        cols = []
        for j in range(N_HEADS // 2):
            r = _nn(wp_ref[j], v[n * CHUNK:(n + 1) * CHUNK, j * CHUNK:(j + 1) * CHUNK])
            cols.append(jnp.where(lane < HEAD_DIM, r[:CHUNK], r[CHUNK:]))
        rows.append(jnp.concatenate(cols, axis=1))
    return jnp.concatenate(rows, axis=0)


def _tile_bias(bs, tokens):
    return jnp.concatenate([bs] * (tokens // CHUNK), axis=0)


FORWARD_LEAD = 8


def _fwd_mixer(x, vec, conv_w, wpair, bs_full, w_in_g, w_out_g, late_parts):
    seq = x.shape[0]
    n_tiles = seq // TILE
    t = TILE
    n_late = len(late_parts)
    forward_step = max(n_tiles - FORWARD_LEAD, 0)
    names = ["norm1_gain", "sc1", "sh1", "gt1", "conv_dw_b", "conv_ln_g", "conv_ln_b", "gm_ln_g", "gm_ln_b",
             "mix_out_gain"]
    vecs = [vec[k] for k in names]

    def body(x_ref, g1, sc1, sh1, gt1, cb, clg, clb, vg, vb, mg, cw, wp, bs, win_hbm, wout_hbm, *rest):
        late = rest[n_late:2 * n_late]
        z_ref, a1_ref, sp_ref, y_ref, o1_ref, x2_ref = rest[2 * n_late:2 * n_late + 6]
        win_v, wout_v, halo, bank, sem, send_sems, recv_sems = rest[2 * n_late + 6:]
        i = pl.program_id(0)
        mx, my, mc = _coords()
        shard = 2 * mx + my

        def half(w, which):
            h = late[w].shape[1] // 2
            return pl.ds(pl.multiple_of(which * h, 16), h)

        def chip_of(j):
            return 2 * _flip(mx, CHIP_FLIPS[j][0]) + _flip(my, CHIP_FLIPS[j][1])

        def ici_copy(w, j, slot):
            rows = late[w].at[slot, half(w, mc)]
            return pltpu.make_async_remote_copy(
                src_ref=rows, dst_ref=rows, send_sem=send_sems.at[w, j], recv_sem=recv_sems.at[w, j],
                device_id=(_flip(mx, CHIP_FLIPS[j][0]), _flip(my, CHIP_FLIPS[j][1]), mc), device_id_type=MESH)

        def d2d_copy(w, j, which):
            rows = late[w].at[chip_of(j), half(w, which)]
            return pltpu.make_async_remote_copy(
                src_ref=rows, dst_ref=rows, send_sem=send_sems.at[w, len(CHIP_FLIPS) + j],
                recv_sem=recv_sems.at[w, len(CHIP_FLIPS) + j], device_id=(mx, my, 1 - mc), device_id_type=MESH)

        pairs = [(w, j) for w in range(n_late) for j in range(len(CHIP_FLIPS))]

        @pl.when(i == 0)
        def _():
            for w, j in pairs:
                ici_copy(w, j, shard).start()
            cps = [pltpu.make_async_copy(win_hbm, win_v, sem.at[0]),
                   pltpu.make_async_copy(wout_hbm, wout_v, sem.at[1])]
            for cp in cps:
                cp.start()
            for cp in cps:
                cp.wait()
            halo[...] = jnp.zeros_like(halo)

        @pl.when(i == forward_step)
        def _():
            for w, j in pairs:
                ici_copy(w, j, chip_of(j)).wait_recv()
                d2d_copy(w, j, mc).start()

        xv = x_ref[...]
        h1b = _mod_norm(xv, g1[...], sc1[...], sh1[...]).astype(BF16)
        zs = [_nn(h1b, win_v[k]) for k in range(N_SHARD)]
        for k in range(N_SHARD):
            z_ref[:, k * D_HALF:(k + 1) * D_HALF] = zs[k]
        ca, cg, gu, gv = zs
        a0 = ca * jax.nn.sigmoid(cg)
        ext = jnp.concatenate([halo[...], a0], axis=0)
        halo[...] = a0[t - HALO:]
        bank[0] = ext
        for b in range(1, 8):
            bank[b] = pltpu.roll(ext, b, axis=0)
        a1 = jnp.zeros((t, D_HALF), F32) + cb[...]
        for s in range(CONV_K):
            q, b = divmod(s, 8)
            a1 = a1 + bank[b, pl.ds(HALO - 8 * q, t), :] * cw[pl.ds(CONV_K - 1 - s, 1), :]
        a1_ref[...] = a1
        mgv = mg[...]
        ya = _conv_branch(a1, clg[...], clb[...], mgv[:, :D_HALF])
        gvn = _gv_norm(gv, vg[...], vb[...]).astype(BF16)
        sp = _head_pair_matmul(wp, gvn) + _tile_bias(bs[...], t)
        sp_ref[...] = sp
        yg = _gate_branch(gu, sp, mgv[:, D_HALF:])
        yb = jnp.concatenate([ya, yg], axis=1).astype(BF16)
        y_ref[...] = yb
        o1 = _nn(yb, wout_v[...])
        o1_ref[...] = o1
        x2_ref[...] = xv + gt1[...] * o1

        @pl.when(i == n_tiles - 1)
        def _():
            for w, j in pairs:
                d2d_copy(w, j, 1 - mc).wait_recv()
            for w, j in pairs:
                ici_copy(w, j, shard).wait_send()
                d2d_copy(w, j, mc).wait_send()

    def row(width):
        return pl.BlockSpec((t, width), lambda i: (i, 0))

    out_shape = [jax.ShapeDtypeStruct((seq, 4 * D_HALF), F32), jax.ShapeDtypeStruct((seq, D_HALF), F32),
                 jax.ShapeDtypeStruct((seq, D_HALF), F32), jax.ShapeDtypeStruct((seq, D_MODEL), BF16),
                 jax.ShapeDtypeStruct((seq, D_MODEL), F32), jax.ShapeDtypeStruct((seq, D_MODEL), F32)]
    n_in = 1 + len(vecs) + 3 + 2
    sem_shape = (n_late, 2 * len(CHIP_FLIPS))
    outs = pl.pallas_call(
        body, grid=(n_tiles,), name="fwd_mixer",
        in_specs=[row(D_MODEL)] + [_full(v.shape) for v in vecs]
        + [_full(conv_w.shape), _full(wpair.shape), _full(bs_full.shape), ANY, ANY] + [ANY] * n_late,
        out_specs=[ANY] * n_late + [row(4 * D_HALF), row(D_HALF), row(D_HALF), row(D_MODEL), row(D_MODEL),
                                    row(D_MODEL)],
        out_shape=[jax.ShapeDtypeStruct(a.shape, a.dtype) for a in late_parts] + out_shape,
        input_output_aliases={n_in + w: w for w in range(n_late)},
        scratch_shapes=[pltpu.VMEM(w_in_g.shape, BF16), pltpu.VMEM(w_out_g.shape, BF16),
                        pltpu.VMEM((HALO, D_HALF), F32), pltpu.VMEM((8, t + HALO, D_HALF), F32),
                        pltpu.SemaphoreType.DMA((2,)), pltpu.SemaphoreType.DMA(sem_shape),
                        pltpu.SemaphoreType.DMA(sem_shape)],
        compiler_params=_params(),
    )(x, *vecs, conv_w, wpair, bs_full, w_in_g, w_out_g, *late_parts)
    return outs[n_late:], outs[:n_late]


def _interleave_matrices():
    row = jnp.arange(TILE)
    token_of_row = (row % 8) * (TILE // 8) + row // 8
    to_inter = (token_of_row[:, None] == row[None, :]).astype(BF16)
    return to_inter, jnp.transpose(to_inter)


def _ffn(x2, target, norm2_gain, sc2, sh2, ffn_w, ffn_b, gt2, final_gain, w_up_g, w_down_g, to_inter, to_natural):
    seq = x2.shape[0]
    n_tiles = seq // TILE
    t = TILE
    n_blk = D_FF // FFN_BLK
    inv_d = 1.0 / D_MODEL

    def body(x2_ref, x2h_ref, tgt_ref, g2, sc2_ref, sh2_ref, fw, fb, gt2_ref, fg, pm_ref, pmt_ref, wup_hbm, wd_hbm,
             du_ref, dx2_ref, dfw_ref, dfb_ref, dfg_ref, dgt2_ref, dg2_ref, dsc2_ref, dsh2_ref, loss_ref, dwd_hbm,
             dwd16_hbm, wup_v, wd_v, dwd_acc, carry, u_s, sil_s, vds_s, f_s, du_s, sem):
        i = pl.program_id(0)
        tile = n_tiles - 1 - i
        sublane = lax.broadcasted_iota(jnp.int32, (8, FFN_BLK), 0)

        @pl.when(i == 0)
        def _():
            cps = [pltpu.make_async_copy(wd_hbm, wd_v, sem.at[0])]
            cps += [pltpu.make_async_copy(wup_hbm.at[k], wup_v.at[:, pl.ds(k * UP_SHARD, UP_SHARD)], sem.at[3 + k])
                    for k in range(N_SHARD)]
            for cp in cps:
                cp.start()
            for cp in cps:
                cp.wait()
            dwd_acc[...] = jnp.zeros_like(dwd_acc)
            carry[...] = jnp.zeros_like(carry)
            dfw_ref[...] = jnp.zeros_like(dfw_ref)
            dfb_ref[...] = jnp.zeros_like(dfb_ref)
            dfg_ref[...] = jnp.zeros_like(dfg_ref)
            dgt2_ref[...] = jnp.zeros_like(dgt2_ref)
            dg2_ref[...] = jnp.zeros_like(dg2_ref)
            dsc2_ref[...] = jnp.zeros_like(dsc2_ref)
            dsh2_ref[...] = jnp.zeros_like(dsh2_ref)
            loss_ref[...] = jnp.zeros_like(loss_ref)

        def cols_of(j):
            return pl.ds(j * FFN_BLK, FFN_BLK), pl.ds(D_FF + j * FFN_BLK, FFN_BLK)

        def wrap_down(last, before):
            return jnp.where(sublane == 0, pltpu.roll(before, 1, axis=0), pltpu.roll(last, 1, axis=0))

        def wrap_up(first, after):
            return jnp.where(sublane == 7, pltpu.roll(after, 7, axis=0), pltpu.roll(first, 7, axis=0))

        x2v = x2_ref[...]
        h2, h2_vjp = _mod_norm_vjp(x2v, g2[...], sc2_ref[...], sh2_ref[...])
        h2b = h2.astype(BF16)
        h2_before = _mod_norm(x2h_ref[...], g2[...], sc2_ref[...], sh2_ref[...]).astype(BF16)
        lhs = jnp.concatenate([_nn(pm_ref[...], h2b).astype(BF16), h2_before], axis=0)

        def up(j):
            cv, cg = cols_of(j)
            return _nn(lhs, wup_v[:, cv]), _nn(lhs, wup_v[:, cg])

        def conv(both, cols):
            cur = both[:t]
            u_s[:, cols] = cur.astype(BF16)
            before = jnp.where(tile > 0, both[t:], 0.0)
            w1 = wrap_down(cur[t - 8:], before)
            w2 = wrap_down(cur[t - 16:t - 8], pltpu.roll(before, 1, axis=0))
            back1 = jnp.concatenate([w1, cur[:t - 8]], axis=0)
            back2 = jnp.concatenate([w2, w1, cur[:t - 16]], axis=0)
            return (fb[:, cols] + cur * fw[pl.ds(2, 1), cols] + back1 * fw[pl.ds(1, 1), cols]
                    + back2 * fw[pl.ds(0, 1), cols])

        pm_t = pmt_ref[...]

        def to_natural_f32(a):
            hi = a.astype(BF16)
            rest = a - hi.astype(F32)
            mid = rest.astype(BF16)
            low = (rest - mid.astype(F32)).astype(BF16)
            return _nn(jnp.concatenate([pm_t, pm_t, pm_t], axis=1), jnp.concatenate([hi, mid, low], axis=0))

        o2 = jnp.zeros((t, D_MODEL), F32)
        ahead_uv = up(0)
        for j in range(n_blk):
            cv, cg = cols_of(j)
            both_v, both_g = ahead_uv
            if j + 1 < n_blk:
                ahead_uv = up(j + 1)
            val, gate = conv(both_v, cv), conv(both_g, cg)
            sig = jax.nn.sigmoid(gate)
            sil = gate * sig
            fb16 = (sil * val).astype(BF16)
            sil_s[:, cv] = sil
            vds_s[:, cv] = val * (sig + sil * (1.0 - sig))
            f_s[:, cv] = fb16
            o2 = o2 + _nn(fb16, wd_v[pl.ds(j * FFN_BLK, FFN_BLK), :])
        o2 = to_natural_f32(o2)

        gt2v = gt2_ref[...]
        x3 = x2v + gt2v * o2
        out, out_vjp = _rms_vjp(x3, fg[...])
        diff = out - tgt_ref[...]
        loss_ref[...] += jnp.zeros_like(loss_ref) + 0.5 * inv_d * jnp.sum(diff * diff)
        dx3, dfg = out_vjp(diff * inv_d)
        dfg_ref[...] += dfg
        dgt2_ref[...] += _colsum(dx3 * o2)
        do2b = _nn(pm_ref[...], (gt2v * dx3).astype(BF16)).astype(BF16)

        def conv_back(dd, cols):
            dfb_ref[:, cols] += _colsum(dd)
            nxt = carry[:, cols]
            w1 = wrap_up(dd[:8], nxt[:8])
            w2 = wrap_up(dd[8:16], nxt[8:])
            ahead = (dd, jnp.concatenate([dd[8:], w1], axis=0), jnp.concatenate([dd[16:], w1, w2], axis=0))
            carry[:, cols] = dd[:16]
            uv = u_s[:, cols].astype(F32)
            du = jnp.zeros((t, FFN_BLK), F32)
            for s in range(FFN_K):
                du = du + ahead[s] * fw[pl.ds(FFN_K - 1 - s, 1), cols]
                dfw_ref[pl.ds(FFN_K - 1 - s, 1), cols] += _colsum(ahead[s] * uv)
            du_s[:, cols] = du.astype(BF16)

        for j in range(n_blk):
            cv, cg = cols_of(j)
            rows = pl.ds(j * FFN_BLK, FFN_BLK)
            df = _nt(do2b, wd_v[rows, :])
            dwd_acc[rows, :] += _tn(f_s[:, cv], do2b)
            conv_back(df * sil_s[:, cv], cv)
            conv_back(df * vds_s[:, cv], cg)
        du16 = _nn(pm_t, du_s[...]).astype(BF16)
        du_ref[...] = du16
        dx2, dg2, dsc2, dsh2 = h2_vjp(_nt(du16, wup_v[...]))
        dx2_ref[...] = dx3 + dx2
        dg2_ref[...] += dg2
        dsc2_ref[...] += dsc2
        dsh2_ref[...] += dsh2

        @pl.when(i == n_tiles - 1)
        def _():
            cp = pltpu.make_async_copy(dwd_acc, dwd_hbm, sem.at[1])
            cp.start()
            wd_v[...] = dwd_acc[...].astype(BF16)
            cp16 = pltpu.make_async_copy(wd_v, dwd16_hbm, sem.at[2])
            cp16.start()
            cp.wait()
            cp16.wait()

    def rev(width):
        return pl.BlockSpec((t, width), lambda i: (n_tiles - 1 - i, 0))

    assert FFN_K == 3
    halo_spec = pl.BlockSpec((8, D_MODEL), lambda i: (jnp.maximum((n_tiles - 1 - i) * (t // 8) - 1, 0), 0))
    vec_spec = _full((1, D_MODEL))
    out_shape = [jax.ShapeDtypeStruct((seq, 2 * D_FF), BF16), jax.ShapeDtypeStruct((seq, D_MODEL), F32),
                 jax.ShapeDtypeStruct((FFN_K, 2 * D_FF), F32), jax.ShapeDtypeStruct((1, 2 * D_FF), F32),
                 jax.ShapeDtypeStruct((1, D_MODEL), F32), jax.ShapeDtypeStruct((1, D_MODEL), F32),
                 jax.ShapeDtypeStruct((1, D_MODEL), F32), jax.ShapeDtypeStruct((1, D_MODEL), F32),
                 jax.ShapeDtypeStruct((1, D_MODEL), F32),
                 jax.ShapeDtypeStruct((1, 128), F32), jax.ShapeDtypeStruct((D_FF, D_MODEL), F32),
                 jax.ShapeDtypeStruct((D_FF, D_MODEL), BF16)]
    return pl.pallas_call(
        body, grid=(n_tiles,), name="ffn",
        in_specs=[rev(D_MODEL), halo_spec, rev(D_MODEL), vec_spec, vec_spec, vec_spec, _full(ffn_w.shape),
                  _full(ffn_b.shape), _full(gt2.shape), _full(final_gain.shape), _full(to_inter.shape),
                  _full(to_natural.shape), ANY, ANY],
        out_specs=[rev(2 * D_FF), rev(D_MODEL), _full((FFN_K, 2 * D_FF)), _full((1, 2 * D_FF)), vec_spec, vec_spec,
                   vec_spec, vec_spec, vec_spec, _full((1, 128)), ANY, ANY],
        out_shape=out_shape,
        scratch_shapes=[pltpu.VMEM((D_MODEL, 2 * D_FF), BF16), pltpu.VMEM((D_FF, D_MODEL), BF16),
                        pltpu.VMEM((D_FF, D_MODEL), F32), pltpu.VMEM((FFN_HALO, 2 * D_FF), F32),
                        pltpu.VMEM((t, 2 * D_FF), BF16), pltpu.VMEM((t, D_FF), F32), pltpu.VMEM((t, D_FF), F32),
                        pltpu.VMEM((t, D_FF), BF16), pltpu.VMEM((t, 2 * D_FF), BF16),
                        pltpu.SemaphoreType.DMA((3 + N_SHARD,))],
        compiler_params=pltpu.CompilerParams(dimension_semantics=("arbitrary",), vmem_limit_bytes=FFN_VMEM_LIMIT_BYTES),
    )(x2, x2, target, norm2_gain, sc2, sh2, ffn_w, ffn_b, gt2, final_gain, to_inter, to_natural, w_up_g, w_down_g)


def _scatter_copies(src16, land, send_sems, recv_sems):
    x, y, c = _coords()
    h = src16.shape[1] // 2
    copies = []
    for f, flip in enumerate(PEER_FLIPS):
        tx, ty, tc = _flip(x, flip[0]), _flip(y, flip[1]), _flip(c, flip[2])
        copies.append(pltpu.make_async_remote_copy(
            src_ref=src16.at[2 * tx + ty, pl.ds(pl.multiple_of(tc * h, 16), h)], dst_ref=land.at[f],
            send_sem=send_sems.at[f], recv_sem=recv_sems.at[f], device_id=(tx, ty, tc), device_id_type=MESH))
    return copies


def _land_shape(src16):
    return jax.ShapeDtypeStruct((len(PEER_FLIPS), src16.shape[1] // 2, src16.shape[2]), BF16)


UP_TILE = 512


def _bwd_up(du, x2, norm2_gain, sc2, sh2, dwd16):
    seq = x2.shape[0]
    t = UP_TILE if seq % UP_TILE == 0 else TILE
    n_tiles = seq // t
    acc_shape = (N_SHARD, D_MODEL, UP_SHARD)

    def body(du_ref, x2_ref, g2, sc2_ref, sh2_ref, dwd16_hbm, dwup_hbm, dwup16_hbm, land_hbm,
             stage16, dwup_acc, sem, send_sems, recv_sems):
        i = pl.program_id(0)

        @pl.when(i == 0)
        def _():
            for cp in _scatter_copies(dwd16_hbm, land_hbm, send_sems, recv_sems):
                cp.start()
            dwup_acc[...] = jnp.zeros_like(dwup_acc)

        h2b = _mod_norm(x2_ref[...], g2[...], sc2_ref[...], sh2_ref[...]).astype(BF16)
        for k in range(N_SHARD):
            dwup_acc[k] += _tn(h2b, du_ref[:, k * UP_SHARD:(k + 1) * UP_SHARD])

        @pl.when(i == n_tiles - 1)
        def _():
            cp = pltpu.make_async_copy(dwup_acc, dwup_hbm, sem.at[0])
            cp.start()
            for k in range(N_SHARD):
                stage16[k] = dwup_acc[k].astype(BF16)
            cp16 = pltpu.make_async_copy(stage16, dwup16_hbm, sem.at[1])
            cp16.start()
            cp.wait()
            cp16.wait()
            for rc in _scatter_copies(dwd16_hbm, land_hbm, send_sems, recv_sems):
                rc.wait()

    def row(width):
        return pl.BlockSpec((t, width), lambda i: (i, 0))

    n_peer = len(PEER_FLIPS)
    return pl.pallas_call(
        body, grid=(n_tiles,), name="bwd_up",
        in_specs=[row(2 * D_FF), row(D_MODEL), _full((1, D_MODEL)), _full((1, D_MODEL)), _full((1, D_MODEL)), ANY],
        out_specs=[ANY, ANY, ANY],
        out_shape=[jax.ShapeDtypeStruct(acc_shape, F32), jax.ShapeDtypeStruct(acc_shape, BF16), _land_shape(dwd16)],
        scratch_shapes=[pltpu.VMEM(acc_shape, BF16), pltpu.VMEM(acc_shape, F32), pltpu.SemaphoreType.DMA((2,)),
                        pltpu.SemaphoreType.DMA((n_peer,)), pltpu.SemaphoreType.DMA((n_peer,))],
        compiler_params=_params(),
    )(du, x2, norm2_gain, sc2, sh2, dwd16)


def _bwd_mixer(dx2, x, z, a1, sp, yb, o1, vec, conv_w, wpair, wpair_t, causal_mask, w_in_g, w_out_g, dwup16):
    seq = x.shape[0]
    n_tiles = seq // TILE
    t = TILE
    names = ["norm1_gain", "sc1", "sh1", "gt1", "conv_ln_g", "conv_ln_b", "gm_ln_g", "gm_ln_b", "mix_out_gain"]
    vecs = [vec[k] for k in names]

    def body(dx2_ref, x_ref, z_ref, a1_ref, sp_ref, y_ref, o1_ref, g1, sc1, sh1, gt1, clg, clb, vg, vb, mg,
             cw, wp, wpt, mask_ref, win_hbm, wout_hbm, dwup16_hbm,
             gx_ref, dg1_ref, dsc1_ref, dsh1_ref, dgt1_ref, dcw_ref, dcb_ref, dclg_ref, dclb_ref, dvg_ref, dvb_ref,
             dmg_ref, dws_ref, dbs_ref, dwin_hbm, dwout_hbm, land_hbm, dwin16_hbm, dwout16_hbm,
             win_v, wout_v, dwin_acc, dwout_acc, carry, bank, dbs_acc, lwin, lwout, sem, send_sems, recv_sems,
             pair_send, pair_recv):
        i = pl.program_id(0)
        small = [dg1_ref, dsc1_ref, dsh1_ref, dgt1_ref, dcw_ref, dcb_ref, dclg_ref, dclb_ref, dvg_ref, dvb_ref,
                 dmg_ref, dws_ref, dbs_acc]

        @pl.when(i == 0)
        def _():
            for cp in _scatter_copies(dwup16_hbm, land_hbm, send_sems, recv_sems):
                cp.start()
            cps = [pltpu.make_async_copy(win_hbm, win_v, sem.at[0]),
                   pltpu.make_async_copy(wout_hbm, wout_v, sem.at[1])]
            for cp in cps:
                cp.start()
            for cp in cps:
                cp.wait()
            dwin_acc[...] = jnp.zeros_like(dwin_acc)
            dwout_acc[...] = jnp.zeros_like(dwout_acc)
            carry[...] = jnp.zeros_like(carry)
            for ref in small:
                ref[...] = jnp.zeros_like(ref)

        dx2v = dx2_ref[...]
        gt1v = gt1[...]
        dgt1_ref[...] += _colsum(dx2v * o1_ref[...])
        do1b = (gt1v * dx2v).astype(BF16)
        dy = _nt(do1b, wout_v[...])
        dwout_acc[...] += _tn(y_ref[...], do1b)

        mgv = mg[...]
        _, conv_vjp = _conv_branch_vjp(a1_ref[...], clg[...], clb[...], mgv[:, :D_HALF])
        da1, dclg, dclb, dmg_a = conv_vjp(dy[:, :D_HALF])
        dclg_ref[...] += dclg
        dclb_ref[...] += dclb
        gu = z_ref[:, 2 * D_HALF:3 * D_HALF]
        gv = z_ref[:, 3 * D_HALF:]
        spv = sp_ref[...]
        _, gate_vjp = _gate_branch_vjp(gu, spv, mgv[:, D_HALF:])
        dgu, dsp, dmg_g = gate_vjp(dy[:, D_HALF:])
        dmg_ref[...] += jnp.concatenate([dmg_a, dmg_g], axis=1)
        gvn, gv_vjp = _gv_norm_vjp(gv, vg[...], vb[...])
        gvnb = gvn.astype(BF16)
        dspb = dsp.astype(BF16)
        dgvn = _head_pair_matmul(wpt, dspb)
        dgv, dvg, dvb = gv_vjp(dgvn)
        dvg_ref[...] += dvg
        dvb_ref[...] += dvb
        lane = lax.broadcasted_iota(jnp.int32, (CHUNK, CHUNK), 1)
        dbs = jnp.zeros((CHUNK, D_HALF), F32)
        for n in range(t // CHUNK):
            rows = slice(n * CHUNK, (n + 1) * CHUNK)
            dbs = dbs + dsp[rows, :]
            for j in range(N_HEADS // 2):
                cols = slice(j * CHUNK, (j + 1) * CHUNK)
                blk = dspb[rows, cols]
                zero = jnp.zeros_like(blk)
                vblk = gvnb[rows, cols]
                dws_ref[2 * j] += _nt(jnp.where(lane < HEAD_DIM, blk, zero), vblk)
                dws_ref[2 * j + 1] += _nt(jnp.where(lane < HEAD_DIM, zero, blk), vblk)
        dbs_acc[...] += dbs

        h1, h1_vjp = _mod_norm_vjp(x_ref[...], g1[...], sc1[...], sh1[...])
        h1b = h1.astype(BF16)
        dh1 = jnp.zeros((t, D_MODEL), F32)
        for k, dzk in ((2, dgu), (3, dgv)):
            dzb = dzk.astype(BF16)
            dh1 = dh1 + _nt(dzb, win_v[k])
            dwin_acc[k] += _tn(h1b, dzb)

        ca = z_ref[:, :D_HALF]
        cg = z_ref[:, D_HALF:2 * D_HALF]
        sig = jax.nn.sigmoid(cg)
        a0 = ca * sig
        ext = jnp.concatenate([da1, carry[...]], axis=0)
        carry[...] = da1[:HALO]
        bank[0] = ext
        for b in range(1, 8):
            bank[b] = pltpu.roll(ext, t + HALO - b, axis=0)
        dcb_ref[...] += _colsum(da1)
        da0 = jnp.zeros((t, D_HALF), F32)
        for s in range(CONV_K):
            q, b = divmod(s, 8)
            shifted = bank[b, pl.ds(8 * q, t), :]
            da0 = da0 + shifted * cw[pl.ds(CONV_K - 1 - s, 1), :]
            dcw_ref[pl.ds(CONV_K - 1 - s, 1), :] += _colsum(shifted * a0)
        dca = da0 * sig
        dcg = da0 * ca * sig * (1.0 - sig)

        for k, dzk in ((0, dca), (1, dcg)):
            dzb = dzk.astype(BF16)
            dh1 = dh1 + _nt(dzb, win_v[k])
            dwin_acc[k] += _tn(h1b, dzb)
        dx, dg1, dsc1, dsh1 = h1_vjp(dh1)
        gx_ref[...] = dx2v + dx
        dg1_ref[...] += dg1
        dsc1_ref[...] += dsc1
        dsh1_ref[...] += dsh1

        @pl.when(i == n_tiles - 1)
        def _():
            for h in range(N_HEADS):
                dws_ref[h] = dws_ref[h] * mask_ref[...]
            head_of_lane = lax.broadcasted_iota(jnp.int32, (N_HEADS, D_HALF), 1) // HEAD_DIM
            pick = (head_of_lane == lax.broadcasted_iota(jnp.int32, (N_HEADS, D_HALF), 0)).astype(F32)
            dbs_ref[...] = lax.dot_general(pick, dbs_acc[...], NT_DIMS, precision=lax.Precision.HIGHEST,
                                           preferred_element_type=F32)
            for k in range(N_SHARD):
                win_v[k] = dwin_acc[k].astype(BF16)
            wout_v[...] = dwout_acc[...].astype(BF16)
            mx, my, mc = _coords()
            h_in, h_out = dwin_acc.shape[1] // 2, dwout_acc.shape[0] // (2 * N_SHARD)

            def in_rows(ref, k, which):
                return ref.at[k, pl.ds(pl.multiple_of(which * h_in, 16), h_in), :]

            def out_rows(ref, k, which):
                return ref.at[pl.ds(pl.multiple_of((2 * k + which) * h_out, 16), h_out), :]

            pairs = ((win_v, dwin_acc, lwin, in_rows, dwin_hbm, dwin16_hbm),
                     (wout_v, dwout_acc, lwout, out_rows, dwout_hbm, dwout16_hbm))
            swaps = [pltpu.make_async_remote_copy(
                src_ref=rows_of(v16, k, 1 - mc), dst_ref=land.at[k], send_sem=pair_send.at[w, k],
                recv_sem=pair_recv.at[w, k], device_id=(mx, my, 1 - mc), device_id_type=MESH)
                for w, (v16, _, land, rows_of, _, _) in enumerate(pairs) for k in range(N_SHARD)]
            for cp in swaps:
                cp.start()
            for cp in swaps:
                cp.wait()
            outs = []
            for w, (v16, acc, land, rows_of, half_hbm, half16_hbm) in enumerate(pairs):
                for k in range(N_SHARD):
                    total = rows_of(acc, k, mc)[...] + land[k].astype(F32)
                    rows_of(acc, k, 0)[...] = total
                    rows_of(v16, k, 0)[...] = total.astype(BF16)
                    outs.append(pltpu.make_async_copy(rows_of(acc, k, 0), half_hbm.at[k], sem.at[2 + 8 * w + k]))
                    outs.append(pltpu.make_async_copy(rows_of(v16, k, 0), half16_hbm.at[k], sem.at[6 + 8 * w + k]))
            for cp in outs:
                cp.start()
            for cp in outs:
                cp.wait()
            for rc in _scatter_copies(dwup16_hbm, land_hbm, send_sems, recv_sems):
                rc.wait()

    def rev(width):
        return pl.BlockSpec((t, width), lambda i: (n_tiles - 1 - i, 0))

    v1024 = jax.ShapeDtypeStruct((1, D_MODEL), F32)
    v512 = jax.ShapeDtypeStruct((1, D_HALF), F32)
    small_shapes = [v1024, v1024, v1024, v1024, jax.ShapeDtypeStruct((CONV_K, D_HALF), F32), v512, v512, v512, v512,
                    v512, v1024, jax.ShapeDtypeStruct((N_HEADS, CHUNK, CHUNK), F32),
                    jax.ShapeDtypeStruct((N_HEADS, CHUNK), F32)]
    n_peer = len(PEER_FLIPS)
    half_in = (N_SHARD, w_in_g.shape[1] // 2, w_in_g.shape[2])
    half_out = (N_SHARD, w_out_g.shape[0] // (2 * N_SHARD), w_out_g.shape[1])
    return pl.pallas_call(
        body, grid=(n_tiles,), name="bwd_mixer",
        in_specs=[rev(D_MODEL), rev(D_MODEL), rev(4 * D_HALF), rev(D_HALF), rev(D_HALF), rev(D_MODEL),
                  rev(D_MODEL)] + [_full(v.shape) for v in vecs]
        + [_full(conv_w.shape), _full(wpair.shape), _full(wpair_t.shape), _full(causal_mask.shape), ANY, ANY, ANY],
        out_specs=[rev(D_MODEL)] + [_full(s.shape) for s in small_shapes] + [ANY] * 5,
        out_shape=[jax.ShapeDtypeStruct((seq, D_MODEL), F32)] + small_shapes
        + [jax.ShapeDtypeStruct(half_in, F32), jax.ShapeDtypeStruct(half_out, F32), _land_shape(dwup16),
           jax.ShapeDtypeStruct(half_in, BF16), jax.ShapeDtypeStruct(half_out, BF16)],
        scratch_shapes=[pltpu.VMEM(w_in_g.shape, BF16), pltpu.VMEM(w_out_g.shape, BF16),
                        pltpu.VMEM(w_in_g.shape, F32), pltpu.VMEM(w_out_g.shape, F32),
                        pltpu.VMEM((HALO, D_HALF), F32), pltpu.VMEM((8, t + HALO, D_HALF), F32),
                        pltpu.VMEM((CHUNK, D_HALF), F32), pltpu.VMEM(half_in, BF16), pltpu.VMEM(half_out, BF16),
                        pltpu.SemaphoreType.DMA((2 + 4 * N_SHARD,)),
                        pltpu.SemaphoreType.DMA((n_peer,)), pltpu.SemaphoreType.DMA((n_peer,)),
                        pltpu.SemaphoreType.DMA((2, N_SHARD)), pltpu.SemaphoreType.DMA((2, N_SHARD))],
        compiler_params=_params(),
    )(dx2, x, z, a1, sp, yb, o1, *vecs, conv_w, wpair, wpair_t, causal_mask, w_in_g, w_out_g, dwup16)


def _gmlp_operands(gm_ws, gm_bs):
    mask = jnp.tril(jnp.ones((CHUNK, CHUNK), F32))
    ws = gm_ws * mask[None]
    wpair = ws.reshape(N_HEADS // 2, 2 * CHUNK, CHUNK).astype(BF16)
    wpair_t = jnp.swapaxes(ws, 1, 2).reshape(N_HEADS // 2, 2 * CHUNK, CHUNK).astype(BF16)
    bs_full = jnp.repeat(jnp.transpose(gm_bs), HEAD_DIM, axis=1)
    return wpair, wpair_t, bs_full, mask


def _local_step(x, target, mod, p, w_in_g, w_out_g, w_up_part, w_down_part):
    sh1, sc1, gt1, sh2, sc2, gt2 = [mod[:, k * D_MODEL:(k + 1) * D_MODEL] for k in range(6)]
    vec = dict(p, sh1=sh1, sc1=sc1, gt1=gt1, sh2=sh2, sc2=sc2, gt2=gt2)
    wpair, wpair_t, bs_full, mask = _gmlp_operands(p["gm_ws"], p["gm_bs"])

    (z, a1, sp, yb, o1, x2), (w_up_g, w_down_g) = _fwd_mixer(
        x, vec, p["conv_dw_w"], wpair, bs_full, w_in_g, w_out_g, [w_up_part, w_down_part])
    w_down_g = w_down_g.reshape(D_FF, D_MODEL)
    to_inter, to_natural = _interleave_matrices()
    du, dx2, d_ffn_w, d_ffn_b, d_fg, d_gt2, d_g2, d_sc2, d_sh2, loss, d_wd, d_wd16 = _ffn(
        x2, target, p["norm2_gain"], sc2, sh2, p["ffn_dw_w"], p["ffn_dw_b"], gt2, p["final_gain"], w_up_g, w_down_g,
        to_inter, to_natural)
    by_shard = (N_SHARD, -1, D_MODEL)
    d_wup, d_wup16, land_wd = _bwd_up(du, x2, p["norm2_gain"], sc2, sh2, d_wd16.reshape(by_shard))
    (gx, d_g1, d_sc1, d_sh1, d_gt1, d_cw, d_cb, d_clg, d_clb, d_vg, d_vb, d_mg, d_ws, d_bs, d_win, d_wout, land_wup,
     d_win16, d_wout16) = _bwd_mixer(dx2, x, z, a1, sp, yb, o1, vec, p["conv_dw_w"], wpair, wpair_t, mask, w_in_g,
                                     w_out_g, d_wup16)
    d_mod = _pack([d_sh1, d_sc1, d_gt1, d_sh2, d_sc2, d_gt2], 6).reshape(1, 6 * D_MODEL)
    grads = dict(norm1_gain=d_g1, conv_dw_w=d_cw, conv_dw_b=d_cb, conv_ln_g=d_clg, conv_ln_b=d_clb, gm_ln_g=d_vg,
                 gm_ln_b=d_vb, gm_ws=d_ws, gm_bs=d_bs, mix_out_gain=d_mg, norm2_gain=d_g2, ffn_dw_w=d_ffn_w,
                 ffn_dw_b=d_ffn_b, final_gain=d_fg, w_in=d_win, w_out=d_wout, w_up=d_wup, w_down=d_wd.reshape(by_shard))
    in_flight = dict(w_in16=d_win16, w_out16=d_wout16, land_w_up=land_wup, land_w_down=land_wd)
    return gx, grads, d_mod, loss, in_flight


MESH = pl.DeviceIdType.MESH
VMEM_SPEC = pl.BlockSpec(memory_space=pltpu.VMEM)
PEER_FLIPS = [(a, b, d) for a in (0, 1) for b in (0, 1) for d in (0, 1)][1:]
CHIP_FLIPS = [(1, 0), (0, 1), (1, 1)]


def _coords():
    return lax.axis_index("x"), lax.axis_index("y"), lax.axis_index("c")


def _flip(v, bit):
    return 1 - v if bit else v


def _rows8(block):
    return pl.ds(pl.multiple_of(8 * block, 8), 8)


def _ada_steps(c_ref, w_ref, b_ref, call_ref, mod_ref, cpad, modall, send_sems, recv_sems):
    x, y, c = _coords()
    me = 4 * x + 2 * y + c
    cpad[...] = jnp.zeros_like(cpad)
    cpad[pl.ds(0, 1), :] = c_ref[...]

    def gather_copy(j, flip):
        peer = (_flip(x, flip[0]), _flip(y, flip[1]), _flip(c, flip[2]))
        return pltpu.make_async_remote_copy(
            src_ref=cpad, dst_ref=call_ref.at[_rows8(me)], send_sem=send_sems.at[j], recv_sem=recv_sems.at[j],
            device_id=peer, device_id_type=MESH)

    def piece_copy(j, flip):
        tx, ty = _flip(x, flip[0]), _flip(y, flip[1])
        return pltpu.make_async_remote_copy(
            src_ref=modall.at[_rows8(4 * tx + 2 * ty + c)], dst_ref=mod_ref.at[_rows8(2 * x + y)],
            send_sem=send_sems.at[len(PEER_FLIPS) + j], recv_sem=recv_sems.at[len(PEER_FLIPS) + j],
            device_id=(tx, ty, c), device_id_type=MESH)

    copies = [gather_copy(j, f) for j, f in enumerate(PEER_FLIPS)]
    for cp in copies:
        cp.start()
    call_ref[_rows8(me), :] = cpad[...]

    def middle():
        for cp in copies:
            cp.wait_recv()
        for cp in copies:
            cp.wait_send()
        cv = call_ref[...]
        c_act = (cv * jax.nn.sigmoid(cv)).astype(BF16)
        modall[...] = _nn(c_act, w_ref[...].astype(BF16)) + b_ref[...]
        for j, f in enumerate(CHIP_FLIPS):
            piece_copy(j, f).start()
        mod_ref[_rows8(2 * x + y), :] = modall[_rows8(me), :]

    def finish():
        for j, f in enumerate(CHIP_FLIPS):
            piece_copy(j, f).wait_recv()
        for j, f in enumerate(CHIP_FLIPS):
            piece_copy(j, f).wait_send()

    return middle, finish


def _gather_weights(shards, filters, n_now, c_row, w_ada_sh, b_ada_sh):
    n = len(shards)
    nf = len(filters)
    ada_cols = w_ada_sh.shape[1]

    def body(*refs):
        ins, f_ins, ada_ins = refs[:n], refs[n:n + nf], refs[n + nf:n + nf + 3]
        refs = refs[n + nf + 3:]
        outs, f_outs, ada_outs = refs[:n], refs[n:n + nf], refs[n + nf:n + nf + 2]
        refs = refs[n + nf + 2:]
        stage = refs[:n]
        send_sems, recv_sems, local_sems, f_send_sems, f_recv_sems, cpad, modall, ada_send, ada_recv = refs[n:]
        ada_middle, ada_finish = _ada_steps(*ada_ins, *ada_outs, cpad, modall, ada_send, ada_recv)
        x, y, c = _coords()
        k = 2 * x + y
        sibling = (x, y, 1 - c)

        def filter_copy(w, j, slot):
            tx, ty = _flip(x, CHIP_FLIPS[j][0]), _flip(y, CHIP_FLIPS[j][1])
            return pltpu.make_async_remote_copy(
                src_ref=f_ins[w], dst_ref=f_outs[w].at[slot], send_sem=f_send_sems.at[w, j],
                recv_sem=f_recv_sems.at[w, j], device_id=(tx, ty, c), device_id_type=MESH)

        def half(w, which):
            h = shards[w].shape[0] // 2
            return pl.ds(pl.multiple_of(which * h, 16), h)

        def ici_copy(w, j, src, slot):
            tx, ty = _flip(x, CHIP_FLIPS[j][0]), _flip(y, CHIP_FLIPS[j][1])
            return pltpu.make_async_remote_copy(
                src_ref=src, dst_ref=outs[w].at[slot, half(w, c)], send_sem=send_sems.at[w, j],
                recv_sem=recv_sems.at[w, j], device_id=(tx, ty, c), device_id_type=MESH)

        def d2d_copy(w, j, slot, which):
            rows = outs[w].at[slot, half(w, which)]
            return pltpu.make_async_remote_copy(
                src_ref=rows, dst_ref=rows, send_sem=send_sems.at[w, len(CHIP_FLIPS) + j],
                recv_sem=recv_sems.at[w, len(CHIP_FLIPS) + j], device_id=sibling, device_id_type=MESH)

        def chip_of(j):
            return 2 * _flip(x, CHIP_FLIPS[j][0]) + _flip(y, CHIP_FLIPS[j][1])

        local, first, passed = [], [], []
        for w in range(nf):
            local.append(pltpu.make_async_copy(f_ins[w], f_outs[w].at[k], local_sems.at[n + w]))
            local[-1].start()
            for j in range(len(CHIP_FLIPS)):
                first.append(filter_copy(w, j, k))
                first[-1].start()
        for w in range(n):
            stage[w][...] = ins[w][...].astype(BF16)
            local.append(pltpu.make_async_copy(stage[w], outs[w].at[k], local_sems.at[w]))
            local[-1].start()
            if w < n_now:
                for j in range(len(CHIP_FLIPS)):
                    first.append(ici_copy(w, j, stage[w].at[half(w, c)], k))
                    first[-1].start()
        ada_middle()
        for w in range(nf):
            for j in range(len(CHIP_FLIPS)):
                filter_copy(w, j, chip_of(j)).wait_recv()
        for w in range(n_now):
            for j in range(len(CHIP_FLIPS)):
                ici_copy(w, j, stage[w].at[half(w, c)], chip_of(j)).wait_recv()
                passed.append(d2d_copy(w, j, chip_of(j), c))
                passed[-1].start()
        for w in range(n_now):
            for j in range(len(CHIP_FLIPS)):
                d2d_copy(w, j, chip_of(j), 1 - c).wait_recv()
        for cp in first + passed:
            cp.wait_send()
        for cp in local:
            cp.wait()
        ada_finish()

    sem_shape = (n_now, 2 * len(CHIP_FLIPS))
    f_sem_shape = (nf, len(CHIP_FLIPS))
    n_ada_sem = len(PEER_FLIPS) + len(CHIP_FLIPS)
    outs = pl.pallas_call(
        body, name="gather_weights",
        in_specs=[VMEM_SPEC] * (n + nf + 3), out_specs=[ANY] * (n + nf) + [VMEM_SPEC, VMEM_SPEC],
        out_shape=[jax.ShapeDtypeStruct((N_SHARD,) + s.shape, BF16) for s in shards]
        + [jax.ShapeDtypeStruct((N_SHARD,) + s.shape, F32) for s in filters]
        + [jax.ShapeDtypeStruct((8 * N_DEV, D_MODEL), F32), jax.ShapeDtypeStruct((8 * N_SHARD, ada_cols), F32)],
        scratch_shapes=[pltpu.VMEM(s.shape, BF16) for s in shards]
        + [pltpu.SemaphoreType.DMA(sem_shape), pltpu.SemaphoreType.DMA(sem_shape), pltpu.SemaphoreType.DMA((n + nf,)),
           pltpu.SemaphoreType.DMA(f_sem_shape), pltpu.SemaphoreType.DMA(f_sem_shape),
           pltpu.VMEM((8, D_MODEL), F32), pltpu.VMEM((8 * N_DEV, ada_cols), F32),
           pltpu.SemaphoreType.DMA((n_ada_sem,)), pltpu.SemaphoreType.DMA((n_ada_sem,))],
        compiler_params=pltpu.CompilerParams(vmem_limit_bytes=VMEM_LIMIT_BYTES),
    )(*shards, *filters, c_row, w_ada_sh, b_ada_sh)
    return outs[:n], outs[n:n + nf], outs[n + nf], outs[n + nf + 1]


def _final_comm(srcs16, small):
    n = len(srcs16)
    rows = small.shape[0]
    half = rows // 2
    quarter = half // 2

    def body(*refs):
        srcs, small_ref = refs[:n], refs[n]
        lands, small_out = refs[n + 1:2 * n + 1], refs[2 * n + 1]
        chip_sum, got_c, got_1, got_2, part, send_sems, recv_sems, small_send_sems, small_recv_sems = refs[2 * n + 2:]
        x, y, c = _coords()
        sibling = (x, y, 1 - c)
        mine = pl.ds(pl.multiple_of(c * half, 8), half)
        copies = []
        for w in range(n):
            for j, flip in enumerate(CHIP_FLIPS):
                tx, ty = _flip(x, flip[0]), _flip(y, flip[1])
                copies.append(pltpu.make_async_remote_copy(
                    src_ref=srcs[w].at[2 * tx + ty], dst_ref=lands[w].at[j], send_sem=send_sems.at[w, j],
                    recv_sem=recv_sems.at[w, j], device_id=(tx, ty, c), device_id_type=MESH))
        for cp in copies:
            cp.start()

        def exchange(pairs):
            rcs = [pltpu.make_async_remote_copy(
                src_ref=src, dst_ref=dst, send_sem=small_send_sems.at[k], recv_sem=small_recv_sems.at[k],
                device_id=peer, device_id_type=MESH) for k, src, dst, peer in pairs]
            for rc in rcs:
                rc.start()
            for rc in rcs:
                rc.wait()

        def quarter_rows(q):
            return pl.ds(pl.multiple_of(c * half + q * quarter, 8), quarter)

        along = ((1 - x, y, c), (x, 1 - y, c))
        exchange([(0, small_ref, got_c, sibling)])
        chip_sum[...] = small_ref[...] + got_c[...]
        exchange([(1 + q, chip_sum.at[quarter_rows(q)], got_1.at[q], along[q]) for q in range(2)])
        for q in range(2):
            part[q] = chip_sum[quarter_rows(q), :] + got_1[q]
        exchange([(3 + q, part.at[q], got_2.at[q], along[1 - q]) for q in range(2)])
        for q in range(2):
            small_out[quarter_rows(q), :] = part[q] + got_2[q]
        exchange([(5, small_out.at[mine], small_out.at[mine], sibling)])
        for cp in copies:
            cp.wait()

    n_chip = len(CHIP_FLIPS)
    quarter_shape = (2, quarter, small.shape[1])
    outs = pl.pallas_call(
        body, name="final_comm",
        in_specs=[ANY] * n + [VMEM_SPEC], out_specs=[ANY] * n + [VMEM_SPEC],
        out_shape=[jax.ShapeDtypeStruct((n_chip,) + a.shape[1:], BF16) for a in srcs16]
        + [jax.ShapeDtypeStruct(small.shape, F32)],
        scratch_shapes=[pltpu.VMEM(small.shape, F32), pltpu.VMEM(small.shape, F32), pltpu.VMEM(quarter_shape, F32),
                        pltpu.VMEM(quarter_shape, F32), pltpu.VMEM(quarter_shape, F32),
                        pltpu.SemaphoreType.DMA((n, n_chip)), pltpu.SemaphoreType.DMA((n, n_chip)),
                        pltpu.SemaphoreType.DMA((6,)), pltpu.SemaphoreType.DMA((6,))],
        compiler_params=pltpu.CompilerParams(vmem_limit_bytes=VMEM_LIMIT_BYTES),
    )(*srcs16, small)
    return outs[:n], outs[n]


ADD_CHUNKS = 4


def _scatter_sum(pos, owns, lands):
    n = len(owns)

    def specs(own_shape, land_shape):
        peers, rows, cols = land_shape
        pick = 1 if own_shape[1] == 2 * rows else 0
        if cols % (128 * ADD_CHUNKS) == 0:
            blk = (rows, cols // ADD_CHUNKS)
            return (pl.BlockSpec((1,) + blk, lambda i, p: (2 * p[0] + p[1], pick * p[2], i)),
                    pl.BlockSpec((peers,) + blk, lambda i, p: (0, 0, i)),
                    pl.BlockSpec((1,) + blk, lambda i, p: (p[2], 0, i)))
        blk = (rows // ADD_CHUNKS, cols)
        return (pl.BlockSpec((1,) + blk, lambda i, p: (2 * p[0] + p[1], pick * p[2] * ADD_CHUNKS + i, 0)),
                pl.BlockSpec((peers,) + blk, lambda i, p: (0, i, 0)),
                pl.BlockSpec((1,) + blk, lambda i, p: (p[2], i, 0)))

    def body(pos_ref, *refs):
        for idx in range(n):
            own, land, out = refs[idx], refs[n + idx], refs[2 * n + idx]
            total = own[0]
            for f in range(land.shape[0]):
                total = total + land[f].astype(F32)
            out[0] = total

    all_specs = [specs(o.shape, l.shape) for o, l in zip(owns, lands)]
    return pl.pallas_call(
        body, name="scatter_sum",
        grid_spec=pltpu.PrefetchScalarGridSpec(
            num_scalar_prefetch=1, grid=(ADD_CHUNKS,),
            in_specs=[s[0] for s in all_specs] + [s[1] for s in all_specs], out_specs=[s[2] for s in all_specs]),
        out_shape=[jax.ShapeDtypeStruct((2,) + l.shape[1:], F32) for l in lands],
        compiler_params=_params(),
    )(pos, *owns, *lands)


def _swap_halves(halves):
    n = len(halves)

    def body(*refs):
        ins, outs = refs[:n], refs[n:2 * n]
        send_sems, recv_sems = refs[2 * n:]
        x, y, c = _coords()
        copies = [pltpu.make_async_remote_copy(
            src_ref=ins[idx].at[pl.ds(c, 1)], dst_ref=outs[idx].at[pl.ds(c, 1)], send_sem=send_sems.at[idx],
            recv_sem=recv_sems.at[idx], device_id=(x, y, 1 - c), device_id_type=MESH) for idx in range(n)]
        for cp in copies:
            cp.start()
        for cp in copies:
            cp.wait()

    return pl.pallas_call(
        body, name="swap_halves",
        in_specs=[ANY] * n, out_specs=[ANY] * n, input_output_aliases={idx: idx for idx in range(n)},
        out_shape=[jax.ShapeDtypeStruct(a.shape, F32) for a in halves],
        scratch_shapes=[pltpu.SemaphoreType.DMA((n,)), pltpu.SemaphoreType.DMA((n,))],
    )(*halves)


def _adamw_math(w, g, m, v):
    m = ADAM_B1 * m + (1.0 - ADAM_B1) * g
    v = ADAM_B2 * v + (1.0 - ADAM_B2) * jnp.square(g)
    m_hat = m / (1.0 - ADAM_B1 ** ADAM_STEP)
    v_hat = v / (1.0 - ADAM_B2 ** ADAM_STEP)
    delta = -ADAM_LR * (m_hat / (jnp.sqrt(v_hat) + ADAM_EPS) + ADAM_WD * w)
    return delta, m, v


def _adamw_group(ws, gs, ms, vs, n_steps):
    n = len(ws)

    def body(*refs):
        w_refs, g_refs, m_refs, v_refs = (refs[q * n:(q + 1) * n] for q in range(4))
        d_outs, m_outs, v_outs = (refs[(4 + q) * n:(5 + q) * n] for q in range(3))
        for idx in range(n):
            d_outs[idx][...], m_outs[idx][...], v_outs[idx][...] = _adamw_math(
                w_refs[idx][...], g_refs[idx][...], m_refs[idx][...], v_refs[idx][...])

    specs = [pl.BlockSpec((w.shape[0] // n_steps, w.shape[1]), lambda i: (i, 0)) for w in ws]
    shapes = [jax.ShapeDtypeStruct(w.shape, F32) for w in ws]
    outs = pl.pallas_call(
        body, grid=(n_steps,), name="adamw_projections", in_specs=specs * 4, out_specs=specs * 3,
        out_shape=shapes * 3, compiler_params=_params(),
    )(*ws, *gs, *ms, *vs)
    return outs[:n], outs[n:2 * n], outs[2 * n:]


def _adamw_many(ws, gs, ms, vs):
    n = len(ws)

    def body(*refs):
        w_refs, g_refs, m_refs, v_refs = (refs[q * n:(q + 1) * n] for q in range(4))
        d_outs, m_outs, v_outs = (refs[(4 + q) * n:(5 + q) * n] for q in range(3))
        for idx in range(n):
            d_outs[idx][...], m_outs[idx][...], v_outs[idx][...] = _adamw_math(
                w_refs[idx][...], g_refs[idx][...], m_refs[idx][...], v_refs[idx][...])

    shapes = [jax.ShapeDtypeStruct(w.shape, F32) for w in ws]
    outs = pl.pallas_call(
        body, name="adamw_small", in_specs=[VMEM_SPEC] * (4 * n), out_specs=[VMEM_SPEC] * (3 * n),
        out_shape=shapes * 3, compiler_params=pltpu.CompilerParams(vmem_limit_bytes=VMEM_LIMIT_BYTES),
    )(*ws, *gs, *ms, *vs)
    return outs[:n], outs[n:2 * n], outs[2 * n:]


def _adamw_ada(c_all16, dmod16, w, m, v, block_rows):
    rows, cols = w.shape

    def body(c_ref, dm_ref, w_ref, m_ref, v_ref, g_out, d_out, m_out, v_out):
        cv = c_ref[...]
        g = _tn((cv * jax.nn.sigmoid(cv)).astype(BF16), dm_ref[...].astype(BF16))
        g_out[...] = g
        d_out[...], m_out[...], v_out[...] = _adamw_math(w_ref[...], g, m_ref[...], v_ref[...])

    spec = pl.BlockSpec((block_rows, cols), lambda i: (i, 0))
    shape = jax.ShapeDtypeStruct((rows, cols), F32)
    return pl.pallas_call(
        body, grid=(rows // block_rows,), name="adamw_w_ada",
        in_specs=[pl.BlockSpec((16, block_rows), lambda i: (0, i)), _full(dmod16.shape), spec, spec, spec],
        out_specs=[spec] * 4, out_shape=[shape] * 4, compiler_params=_params(),
    )(c_all16, dmod16, w, m, v)


SMALL_REPLICATED = ["b_ada", "norm1_gain", "conv_dw_b", "conv_ln_g", "conv_ln_b", "gm_ln_g", "gm_ln_b", "gm_ws", "gm_bs",
                    "mix_out_gain", "norm2_gain", "ffn_dw_b", "final_gain"]
SMALL_SHARDED = ["conv_dw_w", "ffn_dw_w"]
PACK_ROWS = 256
WEIGHT_ORDER = ["w_ada", "b_ada", "norm1_gain", "w_in", "conv_dw_w", "conv_dw_b", "conv_ln_g", "conv_ln_b", "gm_ln_g",
                "gm_ln_b", "gm_ws", "gm_bs", "mix_out_gain", "w_out", "norm2_gain", "w_up", "ffn_dw_w", "ffn_dw_b",
                "w_down", "final_gain"]


def _pack(parts, rows):
    total = rows * D_MODEL
    flat, offset = None, 0
    for a in parts:
        piece = jnp.pad(a.reshape(-1), (offset, total - offset - a.size))
        flat = piece if flat is None else flat + piece
        offset += a.size
    return flat.reshape(rows, D_MODEL)


def _unpack(packed, shapes):
    flat = packed.reshape(-1)
    out, pos = [], 0
    for s in shapes:
        size = 1
        for d in s:
            size *= d
        out.append(flat[pos:pos + size].reshape(s))
        pos += size
    return out


def kernel(x, c, w_ada, b_ada, norm1_gain, w_in, conv_dw_w, conv_dw_b, conv_ln_g, conv_ln_b, gm_ln_g, gm_ln_b, gm_ws, gm_bs, mix_out_gain, w_out, norm2_gain, w_up, ffn_dw_w, ffn_dw_b, w_down, final_gain, loss_target, m_w_ada, m_b_ada, m_norm1_gain, m_w_in, m_conv_dw_w, m_conv_dw_b, m_conv_ln_g, m_conv_ln_b, m_gm_ln_g, m_gm_ln_b, m_gm_ws, m_gm_bs, m_mix_out_gain, m_w_out, m_norm2_gain, m_w_up, m_ffn_dw_w, m_ffn_dw_b, m_w_down, m_final_gain, v_w_ada, v_b_ada, v_norm1_gain, v_w_in, v_conv_dw_w, v_conv_dw_b, v_conv_ln_g, v_conv_ln_b, v_gm_ln_g, v_gm_ln_b, v_gm_ws, v_gm_bs, v_mix_out_gain, v_w_out, v_norm2_gain, v_w_up, v_ffn_dw_w, v_ffn_dw_b, v_w_down, v_final_gain):
    weights = dict(w_ada=w_ada, b_ada=b_ada, norm1_gain=norm1_gain, w_in=w_in, conv_dw_w=conv_dw_w, conv_dw_b=conv_dw_b,
                   conv_ln_g=conv_ln_g, conv_ln_b=conv_ln_b, gm_ln_g=gm_ln_g, gm_ln_b=gm_ln_b, gm_ws=gm_ws, gm_bs=gm_bs,
                   mix_out_gain=mix_out_gain, w_out=w_out, norm2_gain=norm2_gain, w_up=w_up, ffn_dw_w=ffn_dw_w,
                   ffn_dw_b=ffn_dw_b, w_down=w_down, final_gain=final_gain)
    mom1 = dict(w_ada=m_w_ada, b_ada=m_b_ada, norm1_gain=m_norm1_gain, w_in=m_w_in, conv_dw_w=m_conv_dw_w,
                conv_dw_b=m_conv_dw_b, conv_ln_g=m_conv_ln_g, conv_ln_b=m_conv_ln_b, gm_ln_g=m_gm_ln_g, gm_ln_b=m_gm_ln_b,
                gm_ws=m_gm_ws, gm_bs=m_gm_bs, mix_out_gain=m_mix_out_gain, w_out=m_w_out, norm2_gain=m_norm2_gain,
                w_up=m_w_up, ffn_dw_w=m_ffn_dw_w, ffn_dw_b=m_ffn_dw_b, w_down=m_w_down, final_gain=m_final_gain)
    mom2 = dict(w_ada=v_w_ada, b_ada=v_b_ada, norm1_gain=v_norm1_gain, w_in=v_w_in, conv_dw_w=v_conv_dw_w,
                conv_dw_b=v_conv_dw_b, conv_ln_g=v_conv_ln_g, conv_ln_b=v_conv_ln_b, gm_ln_g=v_gm_ln_g, gm_ln_b=v_gm_ln_b,
                gm_ws=v_gm_ws, gm_bs=v_gm_bs, mix_out_gain=v_mix_out_gain, w_out=v_w_out, norm2_gain=v_norm2_gain,
                w_up=v_w_up, ffn_dw_w=v_ffn_dw_w, ffn_dw_b=v_ffn_dw_b, w_down=v_w_down, final_gain=v_final_gain)
    shard = 2 * lax.axis_index("x") + lax.axis_index("y")
    me = 2 * shard + lax.axis_index("c")

    ada_cols = w_ada.shape[2]
    b_ada_sh = lax.dynamic_slice(b_ada, (0, shard * ada_cols), (1, ada_cols))
    (w_in_g, w_out_g, w_up_part, w_down_part), (conv_w_g, ffn_w_g), c_all64, mod32 = _gather_weights(
        [w_in[0], w_out[0], w_up[0], w_down[0]], [conv_dw_w[0], ffn_dw_w[0]], 2, c, w_ada[0], b_ada_sh)
    c_all = c_all64[::8]
    mod = mod32[::8].reshape(1, N_SHARD * ada_cols)
    conv_w_full = jnp.transpose(conv_w_g, (1, 0, 2)).reshape(CONV_K, D_HALF)
    ffn_w_full = jnp.transpose(ffn_w_g, (1, 0, 2)).reshape(FFN_K, 2 * D_FF)

    p = dict(norm1_gain=norm1_gain, conv_dw_w=conv_w_full, conv_dw_b=conv_dw_b, conv_ln_g=conv_ln_g,
             conv_ln_b=conv_ln_b, gm_ln_g=gm_ln_g, gm_ln_b=gm_ln_b, gm_ws=gm_ws[0], gm_bs=gm_bs[0],
             mix_out_gain=mix_out_gain, norm2_gain=norm2_gain, ffn_dw_w=ffn_w_full, ffn_dw_b=ffn_dw_b,
             final_gain=final_gain[None])
    grad_x, g, d_mod, loss, in_flight = _local_step(
        x[0], loss_target[0], mod, p, w_in_g, w_out_g.reshape(D_MODEL, D_MODEL), w_up_part, w_down_part)

    n_mod = d_mod.shape[1]
    dmod_rows = lax.dynamic_update_slice(jnp.zeros((N_DEV, n_mod), F32), d_mod, (me, 0))
    g["b_ada"] = d_mod
    small = _pack([g[k] for k in SMALL_REPLICATED] + [g[k] for k in SMALL_SHARDED] + [dmod_rows, loss[0, :1]], PACK_ROWS)
    (land_w_in, land_w_out), small = _final_comm([in_flight["w_in16"], in_flight["w_out16"]], small)
    pos = jnp.stack(_coords()).astype(jnp.int32)
    halves = _scatter_sum(pos, [g["w_in"], g["w_out"], g["w_up"], g["w_down"]],
                          [land_w_in, land_w_out, in_flight["land_w_up"], in_flight["land_w_down"]])
    full = _swap_halves(halves)
    grads = dict(w_in=full[0].reshape(w_in.shape[1:]), w_out=full[1].reshape(w_out.shape[1:]),
                 w_up=full[2].reshape(w_up.shape[1:]), w_down=full[3].reshape(w_down.shape[1:]))

    small_shapes = ([weights[k].shape for k in SMALL_REPLICATED] + [(CONV_K, D_HALF), (FFN_K, 2 * D_FF)]
                    + [(N_DEV, n_mod), (1,)])
    *small_grads, conv_w_grad, ffn_w_grad, dmod_all, loss_sum = _unpack(small, small_shapes)
    grads.update(zip(SMALL_REPLICATED, small_grads))
    grads["conv_dw_w"] = lax.dynamic_slice(conv_w_grad, (0, shard * conv_dw_w.shape[2]), conv_dw_w.shape[1:])[None]
    grads["ffn_dw_w"] = lax.dynamic_slice(ffn_w_grad, (0, shard * ffn_dw_w.shape[2]), ffn_dw_w.shape[1:])[None]

    delta, new_m, new_v = {}, {}, {}
    projections = ["w_in", "w_out", "w_up", "w_down"]
    group_out = _adamw_group([weights[k][0] for k in projections], [grads[k] for k in projections],
                             [mom1[k][0] for k in projections], [mom2[k][0] for k in projections], n_steps=4)
    for d, arrs in zip((delta, new_m, new_v), group_out):
        d.update({k: a[None] for k, a in zip(projections, arrs)})
    for k in projections:
        grads[k] = grads[k][None]
    dmod_sh = lax.dynamic_slice(dmod_all, (0, shard * ada_cols), (N_DEV, ada_cols))
    pad8 = ((0, 16 - N_DEV), (0, 0))
    grads["w_ada"], delta["w_ada"], new_m["w_ada"], new_v["w_ada"] = [a[None] for a in _adamw_ada(
        jnp.pad(c_all, pad8), jnp.pad(dmod_sh, pad8), w_ada[0], m_w_ada[0], v_w_ada[0], 256)]
    small_names = SMALL_REPLICATED + SMALL_SHARDED

    def two_d(a):
        return a.reshape(1, -1) if a.ndim == 1 else a

    small_out = _adamw_many(*[[two_d(d[k]) for k in small_names] for d in (weights, grads, mom1, mom2)])
    for d, arrs in zip((delta, new_m, new_v), small_out):
        d.update({k: a.reshape(weights[k].shape) for k, a in zip(small_names, arrs)})

    return (loss_sum.reshape(()), grad_x[None], *[grads[k] for k in WEIGHT_ORDER], *[delta[k] for k in WEIGHT_ORDER],
            *[new_m[k] for k in WEIGHT_ORDER], *[new_v[k] for k in WEIGHT_ORDER])
```

```python
import jax
import jax.numpy as jnp
from jax import lax
from jax.experimental import pallas as pl
from jax.experimental.pallas import tpu as pltpu

F32 = jnp.float32
BF16 = jnp.bfloat16

D_MODEL = 1024
D_HALF = 512
D_FF = 2816
CONV_K = 31
FFN_K = 3
CHUNK = 128
N_HEADS = 8
HEAD_DIM = 64
N_SHARD = 4
N_DEV = 8
RMS_EPS = 1e-6
LN_EPS = 1e-5
ADAM_LR, ADAM_B1, ADAM_B2, ADAM_EPS, ADAM_WD, ADAM_STEP = 0.001, 0.9, 0.999, 1e-08, 0.01, 10

TILE = 256
HALO = 32
FFN_HALO = 16
FFN_BLK = 256
UP_SHARD = 2 * D_FF // N_SHARD
VMEM_LIMIT_BYTES = 56 * 1024 * 1024
FFN_VMEM_LIMIT_BYTES = 58 * 1024 * 1024

ANY = pl.BlockSpec(memory_space=pl.ANY)
NT_DIMS = (((1,), (1,)), ((), ()))
TN_DIMS = (((0,), (0,)), ((), ()))


def _full(shape):
    return pl.BlockSpec(shape, lambda i: (0,) * len(shape))


def _nn(a, b):
    return jnp.dot(a, b, preferred_element_type=F32)


def _nt(a, b):
    return lax.dot_general(a, b, NT_DIMS, preferred_element_type=F32)


def _tn(a, b):
    return lax.dot_general(a, b, TN_DIMS, preferred_element_type=F32)


def _colsum(a):
    return jnp.sum(a, axis=0, keepdims=True)


def _params(semantics=("arbitrary",)):
    return pltpu.CompilerParams(dimension_semantics=semantics, vmem_limit_bytes=VMEM_LIMIT_BYTES)


def _rms(v, gain):
    return v * lax.rsqrt(jnp.mean(v * v, axis=-1, keepdims=True) + RMS_EPS) * gain


def _layer_norm(v, gain, bias):
    mu = jnp.mean(v, axis=-1, keepdims=True)
    var = jnp.mean(jnp.square(v - mu), axis=-1, keepdims=True)
    return (v - mu) * lax.rsqrt(var + LN_EPS) * gain + bias


def _mod_norm(v, gain, scale, shift):
    return _rms(v, gain) * (1.0 + scale) + shift


def _conv_branch(a1, ln_g, ln_b, out_gain):
    a2 = _layer_norm(a1, ln_g, ln_b)
    return _rms(a2 * jax.nn.sigmoid(a2), out_gain)


def _gate_branch(gu, sp, out_gain):
    return _rms(jax.nn.gelu(gu) * sp, out_gain)


def _gv_norm(gv, ln_g, ln_b):
    return _layer_norm(jax.nn.gelu(gv), ln_g, ln_b)


def _rms_parts(v):
    r = lax.rsqrt(jnp.mean(v * v, axis=-1, keepdims=True) + RMS_EPS)
    return v * r, r


def _rms_back(dn, n, r):
    return r * (dn - n * jnp.mean(dn * n, axis=-1, keepdims=True))


def _ln_parts(v):
    mu = jnp.mean(v, axis=-1, keepdims=True)
    rs = lax.rsqrt(jnp.mean(jnp.square(v - mu), axis=-1, keepdims=True) + LN_EPS)
    return (v - mu) * rs, rs


def _ln_back(dn, n, rs):
    return rs * (dn - jnp.mean(dn, axis=-1, keepdims=True) - n * jnp.mean(dn * n, axis=-1, keepdims=True))


GELU_C = 0.7978845608028654
GELU_A = 0.044715


def _gelu_parts(v):
    v2 = v * v
    th = jnp.tanh(GELU_C * (v + GELU_A * (v2 * v)))
    cdf = 0.5 * (1.0 + th)
    return v * cdf, cdf + (0.5 * GELU_C) * v * (1.0 - th * th) * (1.0 + (3.0 * GELU_A) * v2)


def _rms_vjp(v, gain):
    n, r = _rms_parts(v)
    return n * gain, lambda dy: (_rms_back(dy * gain, n, r), _colsum(dy * n))


def _mod_norm_vjp(v, gain, scale, shift):
    n, r = _rms_parts(v)

    def back(dy):
        q = _colsum(dy * n)
        return _rms_back(dy * (gain * (1.0 + scale)), n, r), q * (1.0 + scale), q * gain, _colsum(dy)

    return n * gain * (1.0 + scale) + shift, back


def _conv_branch_vjp(a1, ln_g, ln_b, out_gain):
    n1, rs1 = _ln_parts(a1)
    a2 = n1 * ln_g + ln_b
    s = jax.nn.sigmoid(a2)
    a3 = a2 * s
    n3, r3 = _rms_parts(a3)

    def back(dy):
        da2 = _rms_back(dy * out_gain, n3, r3) * (s + a3 * (1.0 - s))
        return _ln_back(da2 * ln_g, n1, rs1), _colsum(da2 * n1), _colsum(da2), _colsum(dy * n3)

    return n3 * out_gain, back


def _gate_branch_vjp(gu, sp, out_gain):
    ge, dge = _gelu_parts(gu)
    n, r = _rms_parts(ge * sp)

    def back(dy):
        dg = _rms_back(dy * out_gain, n, r)
        return dg * sp * dge, dg * ge, _colsum(dy * n)

    return n * out_gain, back


def _gv_norm_vjp(gv, ln_g, ln_b):
    ge, dge = _gelu_parts(gv)
    n, rs = _ln_parts(ge)
    return n * ln_g + ln_b, lambda dy: (_ln_back(dy * ln_g, n, rs) * dge, _colsum(dy * n), _colsum(dy))


def _head_pair_matmul(wp_ref, v):
    lane = lax.broadcasted_iota(jnp.int32, (CHUNK, CHUNK), 1)
    rows = []
    for n in range(v.shape[0] // CHUNK):
        cols = []
        for j in range(N_HEADS // 2):
            r = _nn(wp_ref[j], v[n * CHUNK:(n + 1) * CHUNK, j * CHUNK:(j + 1) * CHUNK])
            cols.append(jnp.where(lane < HEAD_DIM, r[:CHUNK], r[CHUNK:]))
        rows.append(jnp.concatenate(cols, axis=1))
    return jnp.concatenate(rows, axis=0)


def _tile_bias(bs, tokens):
    return jnp.concatenate([bs] * (tokens // CHUNK), axis=0)


FORWARD_LEAD = 8


def _fwd_mixer(x, vec, conv_w, wpair, bs_full, w_in_g, w_out_g, late_parts):
    seq = x.shape[0]
    n_tiles = seq // TILE
    t = TILE
    n_late = len(late_parts)
    forward_step = max(n_tiles - FORWARD_LEAD, 0)
    names = ["norm1_gain", "sc1", "sh1", "gt1", "conv_dw_b", "conv_ln_g", "conv_ln_b", "gm_ln_g", "gm_ln_b",
             "mix_out_gain"]
    vecs = [vec[k] for k in names]

    def body(x_ref, g1, sc1, sh1, gt1, cb, clg, clb, vg, vb, mg, cw, wp, bs, win_hbm, wout_hbm, *rest):
        late = rest[n_late:2 * n_late]
        z_ref, a1_ref, sp_ref, y_ref, o1_ref, x2_ref = rest[2 * n_late:2 * n_late + 6]
        win_v, wout_v, halo, bank, sem, send_sems, recv_sems = rest[2 * n_late + 6:]
        i = pl.program_id(0)
        mx, my, mc = _coords()
        shard = 2 * mx + my

        def half(w, which):
            h = late[w].shape[1] // 2
            return pl.ds(pl.multiple_of(which * h, 16), h)

        def chip_of(j):
            return 2 * _flip(mx, CHIP_FLIPS[j][0]) + _flip(my, CHIP_FLIPS[j][1])

        def ici_copy(w, j, slot):
            rows = late[w].at[slot, half(w, mc)]
            return pltpu.make_async_remote_copy(
                src_ref=rows, dst_ref=rows, send_sem=send_sems.at[w, j], recv_sem=recv_sems.at[w, j],
                device_id=(_flip(mx, CHIP_FLIPS[j][0]), _flip(my, CHIP_FLIPS[j][1]), mc), device_id_type=MESH)

        def d2d_copy(w, j, which):
            rows = late[w].at[chip_of(j), half(w, which)]
            return pltpu.make_async_remote_copy(
                src_ref=rows, dst_ref=rows, send_sem=send_sems.at[w, len(CHIP_FLIPS) + j],
                recv_sem=recv_sems.at[w, len(CHIP_FLIPS) + j], device_id=(mx, my, 1 - mc), device_id_type=MESH)

        pairs = [(w, j) for w in range(n_late) for j in range(len(CHIP_FLIPS))]

        @pl.when(i == 0)
        def _():
            for w, j in pairs:
                ici_copy(w, j, shard).start()
            cps = [pltpu.make_async_copy(win_hbm, win_v, sem.at[0]),
                   pltpu.make_async_copy(wout_hbm, wout_v, sem.at[1])]
            for cp in cps:
                cp.start()
            for cp in cps:
                cp.wait()
            halo[...] = jnp.zeros_like(halo)

        @pl.when(i == forward_step)
        def _():
            for w, j in pairs:
                ici_copy(w, j, chip_of(j)).wait_recv()
                d2d_copy(w, j, mc).start()

        xv = x_ref[...]
        h1b = _mod_norm(xv, g1[...], sc1[...], sh1[...]).astype(BF16)
        zs = [_nn(h1b, win_v[k]) for k in range(N_SHARD)]
        for k in range(N_SHARD):
            z_ref[:, k * D_HALF:(k + 1) * D_HALF] = zs[k]
        ca, cg, gu, gv = zs
        a0 = ca * jax.nn.sigmoid(cg)
        ext = jnp.concatenate([halo[...], a0], axis=0)
        halo[...] = a0[t - HALO:]
        bank[0] = ext
        for b in range(1, 8):
            bank[b] = pltpu.roll(ext, b, axis=0)
        a1 = jnp.zeros((t, D_HALF), F32) + cb[...]
        for s in range(CONV_K):
            q, b = divmod(s, 8)
            a1 = a1 + bank[b, pl.ds(HALO - 8 * q, t), :] * cw[pl.ds(CONV_K - 1 - s, 1), :]
        a1_ref[...] = a1
        mgv = mg[...]
        ya = _conv_branch(a1, clg[...], clb[...], mgv[:, :D_HALF])
        gvn = _gv_norm(gv, vg[...], vb[...]).astype(BF16)
        sp = _head_pair_matmul(wp, gvn) + _tile_bias(bs[...], t)
        sp_ref[...] = sp
        yg = _gate_branch(gu, sp, mgv[:, D_HALF:])
        yb = jnp.concatenate([ya, yg], axis=1).astype(BF16)
        y_ref[...] = yb
        o1 = _nn(yb, wout_v[...])
        o1_ref[...] = o1
        x2_ref[...] = xv + gt1[...] * o1

        @pl.when(i == n_tiles - 1)
        def _():
            for w, j in pairs:
                d2d_copy(w, j, 1 - mc).wait_recv()
            for w, j in pairs:
                ici_copy(w, j, shard).wait_send()
                d2d_copy(w, j, mc).wait_send()

    def row(width):
        return pl.BlockSpec((t, width), lambda i: (i, 0))

    out_shape = [jax.ShapeDtypeStruct((seq, 4 * D_HALF), F32), jax.ShapeDtypeStruct((seq, D_HALF), F32),
                 jax.ShapeDtypeStruct((seq, D_HALF), F32), jax.ShapeDtypeStruct((seq, D_MODEL), BF16),
                 jax.ShapeDtypeStruct((seq, D_MODEL), F32), jax.ShapeDtypeStruct((seq, D_MODEL), F32)]
    n_in = 1 + len(vecs) + 3 + 2
    sem_shape = (n_late, 2 * len(CHIP_FLIPS))
    outs = pl.pallas_call(
        body, grid=(n_tiles,), name="fwd_mixer",
        in_specs=[row(D_MODEL)] + [_full(v.shape) for v in vecs]
        + [_full(conv_w.shape), _full(wpair.shape), _full(bs_full.shape), ANY, ANY] + [ANY] * n_late,
        out_specs=[ANY] * n_late + [row(4 * D_HALF), row(D_HALF), row(D_HALF), row(D_MODEL), row(D_MODEL),
                                    row(D_MODEL)],
        out_shape=[jax.ShapeDtypeStruct(a.shape, a.dtype) for a in late_parts] + out_shape,
        input_output_aliases={n_in + w: w for w in range(n_late)},
        scratch_shapes=[pltpu.VMEM(w_in_g.shape, BF16), pltpu.VMEM(w_out_g.shape, BF16),
                        pltpu.VMEM((HALO, D_HALF), F32), pltpu.VMEM((8, t + HALO, D_HALF), F32),
                        pltpu.SemaphoreType.DMA((2,)), pltpu.SemaphoreType.DMA(sem_shape),
                        pltpu.SemaphoreType.DMA(sem_shape)],
        compiler_params=_params(),
    )(x, *vecs, conv_w, wpair, bs_full, w_in_g, w_out_g, *late_parts)
    return outs[n_late:], outs[:n_late]


def _interleave_matrices():
    row = jnp.arange(TILE)
    token_of_row = (row % 8) * (TILE // 8) + row // 8
    to_inter = (token_of_row[:, None] == row[None, :]).astype(BF16)
    return to_inter, jnp.transpose(to_inter)


def _ffn(x2, target, norm2_gain, sc2, sh2, ffn_w, ffn_b, gt2, final_gain, w_up_g, w_down_g, to_inter, to_natural):
    seq = x2.shape[0]
    n_tiles = seq // TILE
    t = TILE
    n_blk = D_FF // FFN_BLK
    inv_d = 1.0 / D_MODEL

    def body(x2_ref, x2h_ref, tgt_ref, g2, sc2_ref, sh2_ref, fw, fb, gt2_ref, fg, pm_ref, pmt_ref, wup_hbm, wd_hbm,
             du_ref, dx2_ref, dfw_ref, dfb_ref, dfg_ref, dgt2_ref, dg2_ref, dsc2_ref, dsh2_ref, loss_ref, dwd_hbm,
             dwd16_hbm, wup_v, wd_v, dwd_acc, carry, u_s, sil_s, vds_s, f_s, du_s, sem):
        i = pl.program_id(0)
        tile = n_tiles - 1 - i
        sublane = lax.broadcasted_iota(jnp.int32, (8, FFN_BLK), 0)

        @pl.when(i == 0)
        def _():
            cps = [pltpu.make_async_copy(wd_hbm, wd_v, sem.at[0])]
            cps += [pltpu.make_async_copy(wup_hbm.at[k], wup_v.at[:, pl.ds(k * UP_SHARD, UP_SHARD)], sem.at[3 + k])
                    for k in range(N_SHARD)]
            for cp in cps:
                cp.start()
            for cp in cps:
                cp.wait()
            dwd_acc[...] = jnp.zeros_like(dwd_acc)
            carry[...] = jnp.zeros_like(carry)
            dfw_ref[...] = jnp.zeros_like(dfw_ref)
            dfb_ref[...] = jnp.zeros_like(dfb_ref)
            dfg_ref[...] = jnp.zeros_like(dfg_ref)
            dgt2_ref[...] = jnp.zeros_like(dgt2_ref)
            dg2_ref[...] = jnp.zeros_like(dg2_ref)
            dsc2_ref[...] = jnp.zeros_like(dsc2_ref)
            dsh2_ref[...] = jnp.zeros_like(dsh2_ref)
            loss_ref[...] = jnp.zeros_like(loss_ref)

        def cols_of(j):
            return pl.ds(j * FFN_BLK, FFN_BLK), pl.ds(D_FF + j * FFN_BLK, FFN_BLK)

        def wrap_down(last, before):
            return jnp.where(sublane == 0, pltpu.roll(before, 1, axis=0), pltpu.roll(last, 1, axis=0))

        def wrap_up(first, after):
            return jnp.where(sublane == 7, pltpu.roll(after, 7, axis=0), pltpu.roll(first, 7, axis=0))

        x2v = x2_ref[...]
        h2, h2_vjp = _mod_norm_vjp(x2v, g2[...], sc2_ref[...], sh2_ref[...])
        h2b = h2.astype(BF16)
        h2_before = _mod_norm(x2h_ref[...], g2[...], sc2_ref[...], sh2_ref[...]).astype(BF16)
        lhs = jnp.concatenate([_nn(pm_ref[...], h2b).astype(BF16), h2_before], axis=0)

        def up(j):
            cv, cg = cols_of(j)
            return _nn(lhs, wup_v[:, cv]), _nn(lhs, wup_v[:, cg])

        def conv(both, cols):
            cur = both[:t]
            u_s[:, cols] = cur.astype(BF16)
            before = jnp.where(tile > 0, both[t:], 0.0)
            w1 = wrap_down(cur[t - 8:], before)
            w2 = wrap_down(cur[t - 16:t - 8], pltpu.roll(before, 1, axis=0))
            back1 = jnp.concatenate([w1, cur[:t - 8]], axis=0)
            back2 = jnp.concatenate([w2, w1, cur[:t - 16]], axis=0)
            return (fb[:, cols] + cur * fw[pl.ds(2, 1), cols] + back1 * fw[pl.ds(1, 1), cols]
                    + back2 * fw[pl.ds(0, 1), cols])

        pm_t = pmt_ref[...]

        def to_natural_f32(a):
            hi = a.astype(BF16)
            rest = a - hi.astype(F32)
            mid = rest.astype(BF16)
            low = (rest - mid.astype(F32)).astype(BF16)
            return _nn(jnp.concatenate([pm_t, pm_t, pm_t], axis=1), jnp.concatenate([hi, mid, low], axis=0))

        o2 = jnp.zeros((t, D_MODEL), F32)
        ahead_uv = up(0)
        for j in range(n_blk):
            cv, cg = cols_of(j)
            both_v, both_g = ahead_uv
            if j + 1 < n_blk:
                ahead_uv = up(j + 1)
            val, gate = conv(both_v, cv), conv(both_g, cg)
            sig = jax.nn.sigmoid(gate)
            sil = gate * sig
            fb16 = (sil * val).astype(BF16)
            sil_s[:, cv] = sil
            vds_s[:, cv] = val * (sig + sil * (1.0 - sig))
            f_s[:, cv] = fb16
            o2 = o2 + _nn(fb16, wd_v[pl.ds(j * FFN_BLK, FFN_BLK), :])
        o2 = to_natural_f32(o2)

        gt2v = gt2_ref[...]
        x3 = x2v + gt2v * o2
        out, out_vjp = _rms_vjp(x3, fg[...])
        diff = out - tgt_ref[...]
        loss_ref[...] += jnp.zeros_like(loss_ref) + 0.5 * inv_d * jnp.sum(diff * diff)
        dx3, dfg = out_vjp(diff * inv_d)
        dfg_ref[...] += dfg
        dgt2_ref[...] += _colsum(dx3 * o2)
        do2b = _nn(pm_ref[...], (gt2v * dx3).astype(BF16)).astype(BF16)

        def conv_back(dd, cols):
            dfb_ref[:, cols] += _colsum(dd)
            nxt = carry[:, cols]
            w1 = wrap_up(dd[:8], nxt[:8])
            w2 = wrap_up(dd[8:16], nxt[8:])
            ahead = (dd, jnp.concatenate([dd[8:], w1], axis=0), jnp.concatenate([dd[16:], w1, w2], axis=0))
            carry[:, cols] = dd[:16]
            uv = u_s[:, cols].astype(F32)
            du = jnp.zeros((t, FFN_BLK), F32)
            for s in range(FFN_K):
                du = du + ahead[s] * fw[pl.ds(FFN_K - 1 - s, 1), cols]
                dfw_ref[pl.ds(FFN_K - 1 - s, 1), cols] += _colsum(ahead[s] * uv)
            du_s[:, cols] = du.astype(BF16)

        for j in range(n_blk):
            cv, cg = cols_of(j)
            rows = pl.ds(j * FFN_BLK, FFN_BLK)
            df = _nt(do2b, wd_v[rows, :])
            dwd_acc[rows, :] += _tn(f_s[:, cv], do2b)
            conv_back(df * sil_s[:, cv], cv)
            conv_back(df * vds_s[:, cv], cg)
        du16 = _nn(pm_t, du_s[...]).astype(BF16)
        du_ref[...] = du16
        dx2, dg2, dsc2, dsh2 = h2_vjp(_nt(du16, wup_v[...]))
        dx2_ref[...] = dx3 + dx2
        dg2_ref[...] += dg2
        dsc2_ref[...] += dsc2
        dsh2_ref[...] += dsh2

        @pl.when(i == n_tiles - 1)
        def _():
            cp = pltpu.make_async_copy(dwd_acc, dwd_hbm, sem.at[1])
            cp.start()
            wd_v[...] = dwd_acc[...].astype(BF16)
            cp16 = pltpu.make_async_copy(wd_v, dwd16_hbm, sem.at[2])
            cp16.start()
            cp.wait()
            cp16.wait()

    def rev(width):
        return pl.BlockSpec((t, width), lambda i: (n_tiles - 1 - i, 0))

    assert FFN_K == 3
    halo_spec = pl.BlockSpec((8, D_MODEL), lambda i: (jnp.maximum((n_tiles - 1 - i) * (t // 8) - 1, 0), 0))
    vec_spec = _full((1, D_MODEL))
    out_shape = [jax.ShapeDtypeStruct((seq, 2 * D_FF), BF16), jax.ShapeDtypeStruct((seq, D_MODEL), F32),
                 jax.ShapeDtypeStruct((FFN_K, 2 * D_FF), F32), jax.ShapeDtypeStruct((1, 2 * D_FF), F32),
                 jax.ShapeDtypeStruct((1, D_MODEL), F32), jax.ShapeDtypeStruct((1, D_MODEL), F32),
                 jax.ShapeDtypeStruct((1, D_MODEL), F32), jax.ShapeDtypeStruct((1, D_MODEL), F32),
                 jax.ShapeDtypeStruct((1, D_MODEL), F32),
                 jax.ShapeDtypeStruct((1, 128), F32), jax.ShapeDtypeStruct((D_FF, D_MODEL), F32),
                 jax.ShapeDtypeStruct((D_FF, D_MODEL), BF16)]
    return pl.pallas_call(
        body, grid=(n_tiles,), name="ffn",
        in_specs=[rev(D_MODEL), halo_spec, rev(D_MODEL), vec_spec, vec_spec, vec_spec, _full(ffn_w.shape),
                  _full(ffn_b.shape), _full(gt2.shape), _full(final_gain.shape), _full(to_inter.shape),
                  _full(to_natural.shape), ANY, ANY],
        out_specs=[rev(2 * D_FF), rev(D_MODEL), _full((FFN_K, 2 * D_FF)), _full((1, 2 * D_FF)), vec_spec, vec_spec,
                   vec_spec, vec_spec, vec_spec, _full((1, 128)), ANY, ANY],
        out_shape=out_shape,
        scratch_shapes=[pltpu.VMEM((D_MODEL, 2 * D_FF), BF16), pltpu.VMEM((D_FF, D_MODEL), BF16),
                        pltpu.VMEM((D_FF, D_MODEL), F32), pltpu.VMEM((FFN_HALO, 2 * D_FF), F32),
                        pltpu.VMEM((t, 2 * D_FF), BF16), pltpu.VMEM((t, D_FF), F32), pltpu.VMEM((t, D_FF), F32),
                        pltpu.VMEM((t, D_FF), BF16), pltpu.VMEM((t, 2 * D_FF), BF16),
                        pltpu.SemaphoreType.DMA((3 + N_SHARD,))],
        compiler_params=pltpu.CompilerParams(dimension_semantics=("arbitrary",), vmem_limit_bytes=FFN_VMEM_LIMIT_BYTES),
    )(x2, x2, target, norm2_gain, sc2, sh2, ffn_w, ffn_b, gt2, final_gain, to_inter, to_natural, w_up_g, w_down_g)


def _scatter_copies(src16, land, send_sems, recv_sems):
    x, y, c = _coords()
    h = src16.shape[1] // 2
    copies = []
    for f, flip in enumerate(PEER_FLIPS):
        tx, ty, tc = _flip(x, flip[0]), _flip(y, flip[1]), _flip(c, flip[2])
        copies.append(pltpu.make_async_remote_copy(
            src_ref=src16.at[2 * tx + ty, pl.ds(pl.multiple_of(tc * h, 16), h)], dst_ref=land.at[f],
            send_sem=send_sems.at[f], recv_sem=recv_sems.at[f], device_id=(tx, ty, tc), device_id_type=MESH))
    return copies


def _land_shape(src16):
    return jax.ShapeDtypeStruct((len(PEER_FLIPS), src16.shape[1] // 2, src16.shape[2]), BF16)


UP_TILE = 512


def _bwd_up(du, x2, norm2_gain, sc2, sh2, dwd16):
    seq = x2.shape[0]
    t = UP_TILE if seq % UP_TILE == 0 else TILE
    n_tiles = seq // t
    acc_shape = (N_SHARD, D_MODEL, UP_SHARD)

    def body(du_ref, x2_ref, g2, sc2_ref, sh2_ref, dwd16_hbm, dwup_hbm, dwup16_hbm, land_hbm,
             stage16, dwup_acc, sem, send_sems, recv_sems):
        i = pl.program_id(0)

        @pl.when(i == 0)
        def _():
            for cp in _scatter_copies(dwd16_hbm, land_hbm, send_sems, recv_sems):
                cp.start()
            dwup_acc[...] = jnp.zeros_like(dwup_acc)

        h2b = _mod_norm(x2_ref[...], g2[...], sc2_ref[...], sh2_ref[...]).astype(BF16)
        for k in range(N_SHARD):
            dwup_acc[k] += _tn(h2b, du_ref[:, k * UP_SHARD:(k + 1) * UP_SHARD])

        @pl.when(i == n_tiles - 1)
        def _():
            cp = pltpu.make_async_copy(dwup_acc, dwup_hbm, sem.at[0])
            cp.start()
            for k in range(N_SHARD):
                stage16[k] = dwup_acc[k].astype(BF16)
            cp16 = pltpu.make_async_copy(stage16, dwup16_hbm, sem.at[1])
            cp16.start()
            cp.wait()
            cp16.wait()
            for rc in _scatter_copies(dwd16_hbm, land_hbm, send_sems, recv_sems):
                rc.wait()

    def row(width):
        return pl.BlockSpec((t, width), lambda i: (i, 0))

    n_peer = len(PEER_FLIPS)
    return pl.pallas_call(
        body, grid=(n_tiles,), name="bwd_up",
        in_specs=[row(2 * D_FF), row(D_MODEL), _full((1, D_MODEL)), _full((1, D_MODEL)), _full((1, D_MODEL)), ANY],
        out_specs=[ANY, ANY, ANY],
        out_shape=[jax.ShapeDtypeStruct(acc_shape, F32), jax.ShapeDtypeStruct(acc_shape, BF16), _land_shape(dwd16)],
        scratch_shapes=[pltpu.VMEM(acc_shape, BF16), pltpu.VMEM(acc_shape, F32), pltpu.SemaphoreType.DMA((2,)),
                        pltpu.SemaphoreType.DMA((n_peer,)), pltpu.SemaphoreType.DMA((n_peer,))],
        compiler_params=_params(),
    )(du, x2, norm2_gain, sc2, sh2, dwd16)


def _bwd_mixer(dx2, x, z, a1, sp, yb, o1, vec, conv_w, wpair, wpair_t, causal_mask, w_in_g, w_out_g, dwup16):
    seq = x.shape[0]
    n_tiles = seq // TILE
    t = TILE
    names = ["norm1_gain", "sc1", "sh1", "gt1", "conv_ln_g", "conv_ln_b", "gm_ln_g", "gm_ln_b", "mix_out_gain"]
    vecs = [vec[k] for k in names]

    def body(dx2_ref, x_ref, z_ref, a1_ref, sp_ref, y_ref, o1_ref, g1, sc1, sh1, gt1, clg, clb, vg, vb, mg,
             cw, wp, wpt, mask_ref, win_hbm, wout_hbm, dwup16_hbm,
             gx_ref, dg1_ref, dsc1_ref, dsh1_ref, dgt1_ref, dcw_ref, dcb_ref, dclg_ref, dclb_ref, dvg_ref, dvb_ref,
             dmg_ref, dws_ref, dbs_ref, dwin_hbm, dwout_hbm, land_hbm, dwin16_hbm, dwout16_hbm,
             win_v, wout_v, dwin_acc, dwout_acc, carry, bank, dbs_acc, lwin, lwout, sem, send_sems, recv_sems,
             pair_send, pair_recv):
        i = pl.program_id(0)
        small = [dg1_ref, dsc1_ref, dsh1_ref, dgt1_ref, dcw_ref, dcb_ref, dclg_ref, dclb_ref, dvg_ref, dvb_ref,
                 dmg_ref, dws_ref, dbs_acc]

        @pl.when(i == 0)
        def _():
            for cp in _scatter_copies(dwup16_hbm, land_hbm, send_sems, recv_sems):
                cp.start()
            cps = [pltpu.make_async_copy(win_hbm, win_v, sem.at[0]),
                   pltpu.make_async_copy(wout_hbm, wout_v, sem.at[1])]
            for cp in cps:
                cp.start()
            for cp in cps:
                cp.wait()
            dwin_acc[...] = jnp.zeros_like(dwin_acc)
            dwout_acc[...] = jnp.zeros_like(dwout_acc)
            carry[...] = jnp.zeros_like(carry)
            for ref in small:
                ref[...] = jnp.zeros_like(ref)

        dx2v = dx2_ref[...]
        gt1v = gt1[...]
        dgt1_ref[...] += _colsum(dx2v * o1_ref[...])
        do1b = (gt1v * dx2v).astype(BF16)
        dy = _nt(do1b, wout_v[...])
        dwout_acc[...] += _tn(y_ref[...], do1b)

        mgv = mg[...]
        _, conv_vjp = _conv_branch_vjp(a1_ref[...], clg[...], clb[...], mgv[:, :D_HALF])
        da1, dclg, dclb, dmg_a = conv_vjp(dy[:, :D_HALF])
        dclg_ref[...] += dclg
        dclb_ref[...] += dclb
        gu = z_ref[:, 2 * D_HALF:3 * D_HALF]
        gv = z_ref[:, 3 * D_HALF:]
        spv = sp_ref[...]
        _, gate_vjp = _gate_branch_vjp(gu, spv, mgv[:, D_HALF:])
        dgu, dsp, dmg_g = gate_vjp(dy[:, D_HALF:])
        dmg_ref[...] += jnp.concatenate([dmg_a, dmg_g], axis=1)
        gvn, gv_vjp = _gv_norm_vjp(gv, vg[...], vb[...])
        gvnb = gvn.astype(BF16)
        dspb = dsp.astype(BF16)
        dgvn = _head_pair_matmul(wpt, dspb)
        dgv, dvg, dvb = gv_vjp(dgvn)
        dvg_ref[...] += dvg
        dvb_ref[...] += dvb
        lane = lax.broadcasted_iota(jnp.int32, (CHUNK, CHUNK), 1)
        dbs = jnp.zeros((CHUNK, D_HALF), F32)
        for n in range(t // CHUNK):
            rows = slice(n * CHUNK, (n + 1) * CHUNK)
            dbs = dbs + dsp[rows, :]
            for j in range(N_HEADS // 2):
                cols = slice(j * CHUNK, (j + 1) * CHUNK)
                blk = dspb[rows, cols]
                zero = jnp.zeros_like(blk)
                vblk = gvnb[rows, cols]
                dws_ref[2 * j] += _nt(jnp.where(lane < HEAD_DIM, blk, zero), vblk)
                dws_ref[2 * j + 1] += _nt(jnp.where(lane < HEAD_DIM, zero, blk), vblk)
        dbs_acc[...] += dbs

        h1, h1_vjp = _mod_norm_vjp(x_ref[...], g1[...], sc1[...], sh1[...])
        h1b = h1.astype(BF16)
        dh1 = jnp.zeros((t, D_MODEL), F32)
        for k, dzk in ((2, dgu), (3, dgv)):
            dzb = dzk.astype(BF16)
            dh1 = dh1 + _nt(dzb, win_v[k])
            dwin_acc[k] += _tn(h1b, dzb)

        ca = z_ref[:, :D_HALF]
        cg = z_ref[:, D_HALF:2 * D_HALF]
        sig = jax.nn.sigmoid(cg)
        a0 = ca * sig
        ext = jnp.concatenate([da1, carry[...]], axis=0)
        carry[...] = da1[:HALO]
        bank[0] = ext
        for b in range(1, 8):
            bank[b] = pltpu.roll(ext, t + HALO - b, axis=0)
        dcb_ref[...] += _colsum(da1)
        da0 = jnp.zeros((t, D_HALF), F32)
        for s in range(CONV_K):
            q, b = divmod(s, 8)
            shifted = bank[b, pl.ds(8 * q, t), :]
            da0 = da0 + shifted * cw[pl.ds(CONV_K - 1 - s, 1), :]
            dcw_ref[pl.ds(CONV_K - 1 - s, 1), :] += _colsum(shifted * a0)
        dca = da0 * sig
        dcg = da0 * ca * sig * (1.0 - sig)

        for k, dzk in ((0, dca), (1, dcg)):
            dzb = dzk.astype(BF16)
            dh1 = dh1 + _nt(dzb, win_v[k])
            dwin_acc[k] += _tn(h1b, dzb)
        dx, dg1, dsc1, dsh1 = h1_vjp(dh1)
        gx_ref[...] = dx2v + dx
        dg1_ref[...] += dg1
        dsc1_ref[...] += dsc1
        dsh1_ref[...] += dsh1

        @pl.when(i == n_tiles - 1)
        def _():
            for h in range(N_HEADS):
                dws_ref[h] = dws_ref[h] * mask_ref[...]
            head_of_lane = lax.broadcasted_iota(jnp.int32, (N_HEADS, D_HALF), 1) // HEAD_DIM
            pick = (head_of_lane == lax.broadcasted_iota(jnp.int32, (N_HEADS, D_HALF), 0)).astype(F32)
            dbs_ref[...] = lax.dot_general(pick, dbs_acc[...], NT_DIMS, precision=lax.Precision.HIGHEST,
                                           preferred_element_type=F32)
            for k in range(N_SHARD):
                win_v[k] = dwin_acc[k].astype(BF16)
            wout_v[...] = dwout_acc[...].astype(BF16)
            mx, my, mc = _coords()
            h_in, h_out = dwin_acc.shape[1] // 2, dwout_acc.shape[0] // (2 * N_SHARD)

            def in_rows(ref, k, which):
                return ref.at[k, pl.ds(pl.multiple_of(which * h_in, 16), h_in), :]

            def out_rows(ref, k, which):
                return ref.at[pl.ds(pl.multiple_of((2 * k + which) * h_out, 16), h_out), :]

            pairs = ((win_v, dwin_acc, lwin, in_rows, dwin_hbm, dwin16_hbm),
                     (wout_v, dwout_acc, lwout, out_rows, dwout_hbm, dwout16_hbm))
            swaps = [pltpu.make_async_remote_copy(
                src_ref=rows_of(v16, k, 1 - mc), dst_ref=land.at[k], send_sem=pair_send.at[w, k],
                recv_sem=pair_recv.at[w, k], device_id=(mx, my, 1 - mc), device_id_type=MESH)
                for w, (v16, _, land, rows_of, _, _) in enumerate(pairs) for k in range(N_SHARD)]
            for cp in swaps:
                cp.start()
            for cp in swaps:
                cp.wait()
            outs = []
            for w, (v16, acc, land, rows_of, half_hbm, half16_hbm) in enumerate(pairs):
                for k in range(N_SHARD):
                    total = rows_of(acc, k, mc)[...] + land[k].astype(F32)
                    rows_of(acc, k, 0)[...] = total
                    rows_of(v16, k, 0)[...] = total.astype(BF16)
                    outs.append(pltpu.make_async_copy(rows_of(acc, k, 0), half_hbm.at[k], sem.at[2 + 8 * w + k]))
                    outs.append(pltpu.make_async_copy(rows_of(v16, k, 0), half16_hbm.at[k], sem.at[6 + 8 * w + k]))
            for cp in outs:
                cp.start()
            for cp in outs:
                cp.wait()
            for rc in _scatter_copies(dwup16_hbm, land_hbm, send_sems, recv_sems):
                rc.wait()

    def rev(width):
        return pl.BlockSpec((t, width), lambda i: (n_tiles - 1 - i, 0))

    v1024 = jax.ShapeDtypeStruct((1, D_MODEL), F32)
    v512 = jax.ShapeDtypeStruct((1, D_HALF), F32)
    small_shapes = [v1024, v1024, v1024, v1024, jax.ShapeDtypeStruct((CONV_K, D_HALF), F32), v512, v512, v512, v512,
                    v512, v1024, jax.ShapeDtypeStruct((N_HEADS, CHUNK, CHUNK), F32),
                    jax.ShapeDtypeStruct((N_HEADS, CHUNK), F32)]
    n_peer = len(PEER_FLIPS)
    half_in = (N_SHARD, w_in_g.shape[1] // 2, w_in_g.shape[2])
    half_out = (N_SHARD, w_out_g.shape[0] // (2 * N_SHARD), w_out_g.shape[1])
    return pl.pallas_call(
        body, grid=(n_tiles,), name="bwd_mixer",
        in_specs=[rev(D_MODEL), rev(D_MODEL), rev(4 * D_HALF), rev(D_HALF), rev(D_HALF), rev(D_MODEL),
                  rev(D_MODEL)] + [_full(v.shape) for v in vecs]
        + [_full(conv_w.shape), _full(wpair.shape), _full(wpair_t.shape), _full(causal_mask.shape), ANY, ANY, ANY],
        out_specs=[rev(D_MODEL)] + [_full(s.shape) for s in small_shapes] + [ANY] * 5,
        out_shape=[jax.ShapeDtypeStruct((seq, D_MODEL), F32)] + small_shapes
        + [jax.ShapeDtypeStruct(half_in, F32), jax.ShapeDtypeStruct(half_out, F32), _land_shape(dwup16),
           jax.ShapeDtypeStruct(half_in, BF16), jax.ShapeDtypeStruct(half_out, BF16)],
        scratch_shapes=[pltpu.VMEM(w_in_g.shape, BF16), pltpu.VMEM(w_out_g.shape, BF16),
                        pltpu.VMEM(w_in_g.shape, F32), pltpu.VMEM(w_out_g.shape, F32),
                        pltpu.VMEM((HALO, D_HALF), F32), pltpu.VMEM((8, t + HALO, D_HALF), F32),
                        pltpu.VMEM((CHUNK, D_HALF), F32), pltpu.VMEM(half_in, BF16), pltpu.VMEM(half_out, BF16),
                        pltpu.SemaphoreType.DMA((2 + 4 * N_SHARD,)),
                        pltpu.SemaphoreType.DMA((n_peer,)), pltpu.SemaphoreType.DMA((n_peer,)),
                        pltpu.SemaphoreType.DMA((2, N_SHARD)), pltpu.SemaphoreType.DMA((2, N_SHARD))],
        compiler_params=_params(),
    )(dx2, x, z, a1, sp, yb, o1, *vecs, conv_w, wpair, wpair_t, causal_mask, w_in_g, w_out_g, dwup16)


def _gmlp_operands(gm_ws, gm_bs):
    mask = jnp.tril(jnp.ones((CHUNK, CHUNK), F32))
    ws = gm_ws * mask[None]
    wpair = ws.reshape(N_HEADS // 2, 2 * CHUNK, CHUNK).astype(BF16)
    wpair_t = jnp.swapaxes(ws, 1, 2).reshape(N_HEADS // 2, 2 * CHUNK, CHUNK).astype(BF16)
    bs_full = jnp.repeat(jnp.transpose(gm_bs), HEAD_DIM, axis=1)
    return wpair, wpair_t, bs_full, mask


def _local_step(x, target, mod, p, w_in_g, w_out_g, w_up_part, w_down_part):
    sh1, sc1, gt1, sh2, sc2, gt2 = [mod[:, k * D_MODEL:(k + 1) * D_MODEL] for k in range(6)]
    vec = dict(p, sh1=sh1, sc1=sc1, gt1=gt1, sh2=sh2, sc2=sc2, gt2=gt2)
    wpair, wpair_t, bs_full, mask = _gmlp_operands(p["gm_ws"], p["gm_bs"])

    (z, a1, sp, yb, o1, x2), (w_up_g, w_down_g) = _fwd_mixer(
        x, vec, p["conv_dw_w"], wpair, bs_full, w_in_g, w_out_g, [w_up_part, w_down_part])
    w_down_g = w_down_g.reshape(D_FF, D_MODEL)
    to_inter, to_natural = _interleave_matrices()
    du, dx2, d_ffn_w, d_ffn_b, d_fg, d_gt2, d_g2, d_sc2, d_sh2, loss, d_wd, d_wd16 = _ffn(
        x2, target, p["norm2_gain"], sc2, sh2, p["ffn_dw_w"], p["ffn_dw_b"], gt2, p["final_gain"], w_up_g, w_down_g,
        to_inter, to_natural)
    by_shard = (N_SHARD, -1, D_MODEL)
    d_wup, d_wup16, land_wd = _bwd_up(du, x2, p["norm2_gain"], sc2, sh2, d_wd16.reshape(by_shard))
    (gx, d_g1, d_sc1, d_sh1, d_gt1, d_cw, d_cb, d_clg, d_clb, d_vg, d_vb, d_mg, d_ws, d_bs, d_win, d_wout, land_wup,
     d_win16, d_wout16) = _bwd_mixer(dx2, x, z, a1, sp, yb, o1, vec, p["conv_dw_w"], wpair, wpair_t, mask, w_in_g,
                                     w_out_g, d_wup16)
    d_mod = _pack([d_sh1, d_sc1, d_gt1, d_sh2, d_sc2, d_gt2], 6).reshape(1, 6 * D_MODEL)
    grads = dict(norm1_gain=d_g1, conv_dw_w=d_cw, conv_dw_b=d_cb, conv_ln_g=d_clg, conv_ln_b=d_clb, gm_ln_g=d_vg,
                 gm_ln_b=d_vb, gm_ws=d_ws, gm_bs=d_bs, mix_out_gain=d_mg, norm2_gain=d_g2, ffn_dw_w=d_ffn_w,
                 ffn_dw_b=d_ffn_b, final_gain=d_fg, w_in=d_win, w_out=d_wout, w_up=d_wup, w_down=d_wd.reshape(by_shard))
    in_flight = dict(w_in16=d_win16, w_out16=d_wout16, land_w_up=land_wup, land_w_down=land_wd)
    return gx, grads, d_mod, loss, in_flight


MESH = pl.DeviceIdType.MESH
VMEM_SPEC = pl.BlockSpec(memory_space=pltpu.VMEM)
PEER_FLIPS = [(a, b, d) for a in (0, 1) for b in (0, 1) for d in (0, 1)][1:]
CHIP_FLIPS = [(1, 0), (0, 1), (1, 1)]


def _coords():
    return lax.axis_index("x"), lax.axis_index("y"), lax.axis_index("c")


def _flip(v, bit):
    return 1 - v if bit else v


def _rows8(block):
    return pl.ds(pl.multiple_of(8 * block, 8), 8)


def _ada_steps(c_ref, w_ref, b_ref, call_ref, mod_ref, cpad, modall, send_sems, recv_sems):
    x, y, c = _coords()
    me = 4 * x + 2 * y + c
    cpad[...] = jnp.zeros_like(cpad)
    cpad[pl.ds(0, 1), :] = c_ref[...]

    def gather_copy(j, flip):
        peer = (_flip(x, flip[0]), _flip(y, flip[1]), _flip(c, flip[2]))
        return pltpu.make_async_remote_copy(
            src_ref=cpad, dst_ref=call_ref.at[_rows8(me)], send_sem=send_sems.at[j], recv_sem=recv_sems.at[j],
            device_id=peer, device_id_type=MESH)

    def piece_copy(j, flip):
        tx, ty = _flip(x, flip[0]), _flip(y, flip[1])
        return pltpu.make_async_remote_copy(
            src_ref=modall.at[_rows8(4 * tx + 2 * ty + c)], dst_ref=mod_ref.at[_rows8(2 * x + y)],
            send_sem=send_sems.at[len(PEER_FLIPS) + j], recv_sem=recv_sems.at[len(PEER_FLIPS) + j],
            device_id=(tx, ty, c), device_id_type=MESH)

    copies = [gather_copy(j, f) for j, f in enumerate(PEER_FLIPS)]
    for cp in copies:
        cp.start()
    call_ref[_rows8(me), :] = cpad[...]

    def middle():
        for cp in copies:
            cp.wait_recv()
        for cp in copies:
            cp.wait_send()
        cv = call_ref[...]
        c_act = (cv * jax.nn.sigmoid(cv)).astype(BF16)
        modall[...] = _nn(c_act, w_ref[...].astype(BF16)) + b_ref[...]
        for j, f in enumerate(CHIP_FLIPS):
            piece_copy(j, f).start()
        mod_ref[_rows8(2 * x + y), :] = modall[_rows8(me), :]

    def finish():
        for j, f in enumerate(CHIP_FLIPS):
            piece_copy(j, f).wait_recv()
        for j, f in enumerate(CHIP_FLIPS):
            piece_copy(j, f).wait_send()

    return middle, finish


def _gather_weights(shards, filters, n_now, c_row, w_ada_sh, b_ada_sh):
    n = len(shards)
    nf = len(filters)
    ada_cols = w_ada_sh.shape[1]

    def body(*refs):
        ins, f_ins, ada_ins = refs[:n], refs[n:n + nf], refs[n + nf:n + nf + 3]
        refs = refs[n + nf + 3:]
        outs, f_outs, ada_outs = refs[:n], refs[n:n + nf], refs[n + nf:n + nf + 2]
        refs = refs[n + nf + 2:]
        stage = refs[:n]
        late_f32 = refs[n:2 * n - n_now]
        (send_sems, recv_sems, local_sems, f_send_sems, f_recv_sems, cpad, modall, ada_send, ada_recv,
         load_sems) = refs[2 * n - n_now:]
        ada_middle, ada_finish = _ada_steps(*ada_ins, *ada_outs, cpad, modall, ada_send, ada_recv)
        x, y, c = _coords()
        k = 2 * x + y
        sibling = (x, y, 1 - c)

        def filter_copy(w, j, slot):
            tx, ty = _flip(x, CHIP_FLIPS[j][0]), _flip(y, CHIP_FLIPS[j][1])
            return pltpu.make_async_remote_copy(
                src_ref=f_ins[w], dst_ref=f_outs[w].at[slot], send_sem=f_send_sems.at[w, j],
                recv_sem=f_recv_sems.at[w, j], device_id=(tx, ty, c), device_id_type=MESH)

        def half(w, which):
            h = shards[w].shape[0] // 2
            return pl.ds(pl.multiple_of(which * h, 16), h)

        def ici_copy(w, j, src, slot):
            tx, ty = _flip(x, CHIP_FLIPS[j][0]), _flip(y, CHIP_FLIPS[j][1])
            return pltpu.make_async_remote_copy(
                src_ref=src, dst_ref=outs[w].at[slot, half(w, c)], send_sem=send_sems.at[w, j],
                recv_sem=recv_sems.at[w, j], device_id=(tx, ty, c), device_id_type=MESH)

        def d2d_copy(w, j, slot, which):
            rows = outs[w].at[slot, half(w, which)]
            return pltpu.make_async_remote_copy(
                src_ref=rows, dst_ref=rows, send_sem=send_sems.at[w, len(CHIP_FLIPS) + j],
                recv_sem=recv_sems.at[w, len(CHIP_FLIPS) + j], device_id=sibling, device_id_type=MESH)

        def chip_of(j):
            return 2 * _flip(x, CHIP_FLIPS[j][0]) + _flip(y, CHIP_FLIPS[j][1])

        local, first, passed = [], [], []
        for w in range(nf):
            local.append(pltpu.make_async_copy(f_ins[w], f_outs[w].at[k], local_sems.at[n + w]))
            local[-1].start()
            for j in range(len(CHIP_FLIPS)):
                first.append(filter_copy(w, j, k))
                first[-1].start()
        for w in range(n_now):
            stage[w][...] = ins[w][...].astype(BF16)
            local.append(pltpu.make_async_copy(stage[w], outs[w].at[k], local_sems.at[w]))
            local[-1].start()
            for j in range(len(CHIP_FLIPS)):
                first.append(ici_copy(w, j, stage[w].at[half(w, c)], k))
                first[-1].start()
        loads = [pltpu.make_async_copy(ins[w], late_f32[w - n_now], load_sems.at[w - n_now]) for w in range(n_now, n)]
        for cp in loads:
            cp.start()
        ada_middle()
        for w in range(n_now, n):
            loads[w - n_now].wait()
            stage[w][...] = late_f32[w - n_now][...].astype(BF16)
            local.append(pltpu.make_async_copy(stage[w], outs[w].at[k], local_sems.at[w]))
            local[-1].start()
        for w in range(nf):
            for j in range(len(CHIP_FLIPS)):
                filter_copy(w, j, chip_of(j)).wait_recv()
        for w in range(n_now):
            for j in range(len(CHIP_FLIPS)):
                ici_copy(w, j, stage[w].at[half(w, c)], chip_of(j)).wait_recv()
                passed.append(d2d_copy(w, j, chip_of(j), c))
                passed[-1].start()
        for w in range(n_now):
            for j in range(len(CHIP_FLIPS)):
                d2d_copy(w, j, chip_of(j), 1 - c).wait_recv()
        for cp in first + passed:
            cp.wait_send()
        for cp in local:
            cp.wait()
        ada_finish()

    sem_shape = (n_now, 2 * len(CHIP_FLIPS))
    f_sem_shape = (nf, len(CHIP_FLIPS))
    n_ada_sem = len(PEER_FLIPS) + len(CHIP_FLIPS)
    outs = pl.pallas_call(
        body, name="gather_weights",
        in_specs=[VMEM_SPEC] * n_now + [ANY] * (n - n_now) + [VMEM_SPEC] * (nf + 3),
        out_specs=[ANY] * (n + nf) + [VMEM_SPEC, VMEM_SPEC],
        out_shape=[jax.ShapeDtypeStruct((N_SHARD,) + s.shape, BF16) for s in shards]
        + [jax.ShapeDtypeStruct((N_SHARD,) + s.shape, F32) for s in filters]
        + [jax.ShapeDtypeStruct((8 * N_DEV, D_MODEL), F32), jax.ShapeDtypeStruct((8 * N_SHARD, ada_cols), F32)],
        scratch_shapes=[pltpu.VMEM(s.shape, BF16) for s in shards] + [pltpu.VMEM(s.shape, F32) for s in shards[n_now:]]
        + [pltpu.SemaphoreType.DMA(sem_shape), pltpu.SemaphoreType.DMA(sem_shape), pltpu.SemaphoreType.DMA((n + nf,)),
           pltpu.SemaphoreType.DMA(f_sem_shape), pltpu.SemaphoreType.DMA(f_sem_shape),
           pltpu.VMEM((8, D_MODEL), F32), pltpu.VMEM((8 * N_DEV, ada_cols), F32),
           pltpu.SemaphoreType.DMA((n_ada_sem,)), pltpu.SemaphoreType.DMA((n_ada_sem,)),
           pltpu.SemaphoreType.DMA((n - n_now,))],
        compiler_params=pltpu.CompilerParams(vmem_limit_bytes=VMEM_LIMIT_BYTES),
    )(*shards, *filters, c_row, w_ada_sh, b_ada_sh)
    return outs[:n], outs[n:n + nf], outs[n + nf], outs[n + nf + 1]


def _final_comm(srcs16, small):
    n = len(srcs16)
    rows = small.shape[0]
    half = rows // 2
    quarter = half // 2

    def body(*refs):
        srcs, small_ref = refs[:n], refs[n]
        lands, small_out = refs[n + 1:2 * n + 1], refs[2 * n + 1]
        chip_sum, got_c, got_1, got_2, part, send_sems, recv_sems, small_send_sems, small_recv_sems = refs[2 * n + 2:]
        x, y, c = _coords()
        sibling = (x, y, 1 - c)
        mine = pl.ds(pl.multiple_of(c * half, 8), half)
        copies = []
        for w in range(n):
            for j, flip in enumerate(CHIP_FLIPS):
                tx, ty = _flip(x, flip[0]), _flip(y, flip[1])
                copies.append(pltpu.make_async_remote_copy(
                    src_ref=srcs[w].at[2 * tx + ty], dst_ref=lands[w].at[j], send_sem=send_sems.at[w, j],
                    recv_sem=recv_sems.at[w, j], device_id=(tx, ty, c), device_id_type=MESH))
        for cp in copies:
            cp.start()

        def exchange(pairs):
            rcs = [pltpu.make_async_remote_copy(
                src_ref=src, dst_ref=dst, send_sem=small_send_sems.at[k], recv_sem=small_recv_sems.at[k],
                device_id=peer, device_id_type=MESH) for k, src, dst, peer in pairs]
            for rc in rcs:
                rc.start()
            for rc in rcs:
                rc.wait()

        def quarter_rows(q):
            return pl.ds(pl.multiple_of(c * half + q * quarter, 8), quarter)

        along = ((1 - x, y, c), (x, 1 - y, c))
        exchange([(0, small_ref, got_c, sibling)])
        chip_sum[...] = small_ref[...] + got_c[...]
        exchange([(1 + q, chip_sum.at[quarter_rows(q)], got_1.at[q], along[q]) for q in range(2)])
        for q in range(2):
            part[q] = chip_sum[quarter_rows(q), :] + got_1[q]
        exchange([(3 + q, part.at[q], got_2.at[q], along[1 - q]) for q in range(2)])
        for q in range(2):
            small_out[quarter_rows(q), :] = part[q] + got_2[q]
        exchange([(5, small_out.at[mine], small_out.at[mine], sibling)])
        for cp in copies:
            cp.wait()

    n_chip = len(CHIP_FLIPS)
    quarter_shape = (2, quarter, small.shape[1])
    outs = pl.pallas_call(
        body, name="final_comm",
        in_specs=[ANY] * n + [VMEM_SPEC], out_specs=[ANY] * n + [VMEM_SPEC],
        out_shape=[jax.ShapeDtypeStruct((n_chip,) + a.shape[1:], BF16) for a in srcs16]
        + [jax.ShapeDtypeStruct(small.shape, F32)],
        scratch_shapes=[pltpu.VMEM(small.shape, F32), pltpu.VMEM(small.shape, F32), pltpu.VMEM(quarter_shape, F32),
                        pltpu.VMEM(quarter_shape, F32), pltpu.VMEM(quarter_shape, F32),
                        pltpu.SemaphoreType.DMA((n, n_chip)), pltpu.SemaphoreType.DMA((n, n_chip)),
                        pltpu.SemaphoreType.DMA((6,)), pltpu.SemaphoreType.DMA((6,))],
        compiler_params=pltpu.CompilerParams(vmem_limit_bytes=VMEM_LIMIT_BYTES),
    )(*srcs16, small)
    return outs[:n], outs[n]


ADD_CHUNKS = 4


def _scatter_sum(pos, owns, lands):
    n = len(owns)

    def specs(own_shape, land_shape):
        peers, rows, cols = land_shape
        pick = 1 if own_shape[1] == 2 * rows else 0
        if cols % (128 * ADD_CHUNKS) == 0:
            blk = (rows, cols // ADD_CHUNKS)
            return (pl.BlockSpec((1,) + blk, lambda i, p: (2 * p[0] + p[1], pick * p[2], i)),
                    pl.BlockSpec((peers,) + blk, lambda i, p: (0, 0, i)),
                    pl.BlockSpec((1,) + blk, lambda i, p: (p[2], 0, i)))
        blk = (rows // ADD_CHUNKS, cols)
        return (pl.BlockSpec((1,) + blk, lambda i, p: (2 * p[0] + p[1], pick * p[2] * ADD_CHUNKS + i, 0)),
                pl.BlockSpec((peers,) + blk, lambda i, p: (0, i, 0)),
                pl.BlockSpec((1,) + blk, lambda i, p: (p[2], i, 0)))

    def body(pos_ref, *refs):
        for idx in range(n):
            own, land, out = refs[idx], refs[n + idx], refs[2 * n + idx]
            total = own[0]
            for f in range(land.shape[0]):
                total = total + land[f].astype(F32)
            out[0] = total

    all_specs = [specs(o.shape, l.shape) for o, l in zip(owns, lands)]
    return pl.pallas_call(
        body, name="scatter_sum",
        grid_spec=pltpu.PrefetchScalarGridSpec(
            num_scalar_prefetch=1, grid=(ADD_CHUNKS,),
            in_specs=[s[0] for s in all_specs] + [s[1] for s in all_specs], out_specs=[s[2] for s in all_specs]),
        out_shape=[jax.ShapeDtypeStruct((2,) + l.shape[1:], F32) for l in lands],
        compiler_params=_params(),
    )(pos, *owns, *lands)


def _swap_halves(halves):
    n = len(halves)

    def body(*refs):
        ins, outs = refs[:n], refs[n:2 * n]
        send_sems, recv_sems = refs[2 * n:]
        x, y, c = _coords()
        copies = [pltpu.make_async_remote_copy(
            src_ref=ins[idx].at[pl.ds(c, 1)], dst_ref=outs[idx].at[pl.ds(c, 1)], send_sem=send_sems.at[idx],
            recv_sem=recv_sems.at[idx], device_id=(x, y, 1 - c), device_id_type=MESH) for idx in range(n)]
        for cp in copies:
            cp.start()
        for cp in copies:
            cp.wait()

    return pl.pallas_call(
        body, name="swap_halves",
        in_specs=[ANY] * n, out_specs=[ANY] * n, input_output_aliases={idx: idx for idx in range(n)},
        out_shape=[jax.ShapeDtypeStruct(a.shape, F32) for a in halves],
        scratch_shapes=[pltpu.SemaphoreType.DMA((n,)), pltpu.SemaphoreType.DMA((n,))],
    )(*halves)


def _adamw_math(w, g, m, v):
    m = ADAM_B1 * m + (1.0 - ADAM_B1) * g
    v = ADAM_B2 * v + (1.0 - ADAM_B2) * jnp.square(g)
    m_hat = m / (1.0 - ADAM_B1 ** ADAM_STEP)
    v_hat = v / (1.0 - ADAM_B2 ** ADAM_STEP)
    delta = -ADAM_LR * (m_hat / (jnp.sqrt(v_hat) + ADAM_EPS) + ADAM_WD * w)
    return delta, m, v


def _adamw_group(ws, gs, ms, vs, n_steps):
    n = len(ws)

    def body(*refs):
        w_refs, g_refs, m_refs, v_refs = (refs[q * n:(q + 1) * n] for q in range(4))
        d_outs, m_outs, v_outs = (refs[(4 + q) * n:(5 + q) * n] for q in range(3))
        for idx in range(n):
            d_outs[idx][...], m_outs[idx][...], v_outs[idx][...] = _adamw_math(
                w_refs[idx][...], g_refs[idx][...], m_refs[idx][...], v_refs[idx][...])

    specs = [pl.BlockSpec((w.shape[0] // n_steps, w.shape[1]), lambda i: (i, 0)) for w in ws]
    shapes = [jax.ShapeDtypeStruct(w.shape, F32) for w in ws]
    outs = pl.pallas_call(
        body, grid=(n_steps,), name="adamw_projections", in_specs=specs * 4, out_specs=specs * 3,
        out_shape=shapes * 3, compiler_params=_params(),
    )(*ws, *gs, *ms, *vs)
    return outs[:n], outs[n:2 * n], outs[2 * n:]


def _adamw_many(ws, gs, ms, vs):
    n = len(ws)

    def body(*refs):
        w_refs, g_refs, m_refs, v_refs = (refs[q * n:(q + 1) * n] for q in range(4))
        d_outs, m_outs, v_outs = (refs[(4 + q) * n:(5 + q) * n] for q in range(3))
        for idx in range(n):
            d_outs[idx][...], m_outs[idx][...], v_outs[idx][...] = _adamw_math(
                w_refs[idx][...], g_refs[idx][...], m_refs[idx][...], v_refs[idx][...])

    shapes = [jax.ShapeDtypeStruct(w.shape, F32) for w in ws]
    outs = pl.pallas_call(
        body, name="adamw_small", in_specs=[VMEM_SPEC] * (4 * n), out_specs=[VMEM_SPEC] * (3 * n),
        out_shape=shapes * 3, compiler_params=pltpu.CompilerParams(vmem_limit_bytes=VMEM_LIMIT_BYTES),
    )(*ws, *gs, *ms, *vs)
    return outs[:n], outs[n:2 * n], outs[2 * n:]


def _adamw_ada(c_all16, dmod16, w, m, v, block_rows):
    rows, cols = w.shape

    def body(c_ref, dm_ref, w_ref, m_ref, v_ref, g_out, d_out, m_out, v_out):
        cv = c_ref[...]
        g = _tn((cv * jax.nn.sigmoid(cv)).astype(BF16), dm_ref[...].astype(BF16))
        g_out[...] = g
        d_out[...], m_out[...], v_out[...] = _adamw_math(w_ref[...], g, m_ref[...], v_ref[...])

    spec = pl.BlockSpec((block_rows, cols), lambda i: (i, 0))
    shape = jax.ShapeDtypeStruct((rows, cols), F32)
    return pl.pallas_call(
        body, grid=(rows // block_rows,), name="adamw_w_ada",
        in_specs=[pl.BlockSpec((16, block_rows), lambda i: (0, i)), _full(dmod16.shape), spec, spec, spec],
        out_specs=[spec] * 4, out_shape=[shape] * 4, compiler_params=_params(),
    )(c_all16, dmod16, w, m, v)


SMALL_REPLICATED = ["b_ada", "norm1_gain", "conv_dw_b", "conv_ln_g", "conv_ln_b", "gm_ln_g", "gm_ln_b", "gm_ws", "gm_bs",
                    "mix_out_gain", "norm2_gain", "ffn_dw_b", "final_gain"]
SMALL_SHARDED = ["conv_dw_w", "ffn_dw_w"]
PACK_ROWS = 256
WEIGHT_ORDER = ["w_ada", "b_ada", "norm1_gain", "w_in", "conv_dw_w", "conv_dw_b", "conv_ln_g", "conv_ln_b", "gm_ln_g",
                "gm_ln_b", "gm_ws", "gm_bs", "mix_out_gain", "w_out", "norm2_gain", "w_up", "ffn_dw_w", "ffn_dw_b",
                "w_down", "final_gain"]


def _pack(parts, rows):
    total = rows * D_MODEL
    flat, offset = None, 0
    for a in parts:
        piece = jnp.pad(a.reshape(-1), (offset, total - offset - a.size))
        flat = piece if flat is None else flat + piece
        offset += a.size
    return flat.reshape(rows, D_MODEL)


def _unpack(packed, shapes):
    flat = packed.reshape(-1)
    out, pos = [], 0
    for s in shapes:
        size = 1
        for d in s:
            size *= d
        out.append(flat[pos:pos + size].reshape(s))
        pos += size
    return out


def kernel(x, c, w_ada, b_ada, norm1_gain, w_in, conv_dw_w, conv_dw_b, conv_ln_g, conv_ln_b, gm_ln_g, gm_ln_b, gm_ws, gm_bs, mix_out_gain, w_out, norm2_gain, w_up, ffn_dw_w, ffn_dw_b, w_down, final_gain, loss_target, m_w_ada, m_b_ada, m_norm1_gain, m_w_in, m_conv_dw_w, m_conv_dw_b, m_conv_ln_g, m_conv_ln_b, m_gm_ln_g, m_gm_ln_b, m_gm_ws, m_gm_bs, m_mix_out_gain, m_w_out, m_norm2_gain, m_w_up, m_ffn_dw_w, m_ffn_dw_b, m_w_down, m_final_gain, v_w_ada, v_b_ada, v_norm1_gain, v_w_in, v_conv_dw_w, v_conv_dw_b, v_conv_ln_g, v_conv_ln_b, v_gm_ln_g, v_gm_ln_b, v_gm_ws, v_gm_bs, v_mix_out_gain, v_w_out, v_norm2_gain, v_w_up, v_ffn_dw_w, v_ffn_dw_b, v_w_down, v_final_gain):
    weights = dict(w_ada=w_ada, b_ada=b_ada, norm1_gain=norm1_gain, w_in=w_in, conv_dw_w=conv_dw_w, conv_dw_b=conv_dw_b,
                   conv_ln_g=conv_ln_g, conv_ln_b=conv_ln_b, gm_ln_g=gm_ln_g, gm_ln_b=gm_ln_b, gm_ws=gm_ws, gm_bs=gm_bs,
                   mix_out_gain=mix_out_gain, w_out=w_out, norm2_gain=norm2_gain, w_up=w_up, ffn_dw_w=ffn_dw_w,
                   ffn_dw_b=ffn_dw_b, w_down=w_down, final_gain=final_gain)
    mom1 = dict(w_ada=m_w_ada, b_ada=m_b_ada, norm1_gain=m_norm1_gain, w_in=m_w_in, conv_dw_w=m_conv_dw_w,
                conv_dw_b=m_conv_dw_b, conv_ln_g=m_conv_ln_g, conv_ln_b=m_conv_ln_b, gm_ln_g=m_gm_ln_g, gm_ln_b=m_gm_ln_b,
                gm_ws=m_gm_ws, gm_bs=m_gm_bs, mix_out_gain=m_mix_out_gain, w_out=m_w_out, norm2_gain=m_norm2_gain,
                w_up=m_w_up, ffn_dw_w=m_ffn_dw_w, ffn_dw_b=m_ffn_dw_b, w_down=m_w_down, final_gain=m_final_gain)
    mom2 = dict(w_ada=v_w_ada, b_ada=v_b_ada, norm1_gain=v_norm1_gain, w_in=v_w_in, conv_dw_w=v_conv_dw_w,
                conv_dw_b=v_conv_dw_b, conv_ln_g=v_conv_ln_g, conv_ln_b=v_conv_ln_b, gm_ln_g=v_gm_ln_g, gm_ln_b=v_gm_ln_b,
                gm_ws=v_gm_ws, gm_bs=v_gm_bs, mix_out_gain=v_mix_out_gain, w_out=v_w_out, norm2_gain=v_norm2_gain,
                w_up=v_w_up, ffn_dw_w=v_ffn_dw_w, ffn_dw_b=v_ffn_dw_b, w_down=v_w_down, final_gain=v_final_gain)
    shard = 2 * lax.axis_index("x") + lax.axis_index("y")
    me = 2 * shard + lax.axis_index("c")

    ada_cols = w_ada.shape[2]
    b_ada_sh = lax.dynamic_slice(b_ada, (0, shard * ada_cols), (1, ada_cols))
    (w_in_g, w_out_g, w_up_part, w_down_part), (conv_w_g, ffn_w_g), c_all64, mod32 = _gather_weights(
        [w_in[0], w_out[0], w_up[0], w_down[0]], [conv_dw_w[0], ffn_dw_w[0]], 2, c, w_ada[0], b_ada_sh)
    c_all = c_all64[::8]
    mod = mod32[::8].reshape(1, N_SHARD * ada_cols)
    conv_w_full = jnp.transpose(conv_w_g, (1, 0, 2)).reshape(CONV_K, D_HALF)
    ffn_w_full = jnp.transpose(ffn_w_g, (1, 0, 2)).reshape(FFN_K, 2 * D_FF)

    p = dict(norm1_gain=norm1_gain, conv_dw_w=conv_w_full, conv_dw_b=conv_dw_b, conv_ln_g=conv_ln_g,
             conv_ln_b=conv_ln_b, gm_ln_g=gm_ln_g, gm_ln_b=gm_ln_b, gm_ws=gm_ws[0], gm_bs=gm_bs[0],
             mix_out_gain=mix_out_gain, norm2_gain=norm2_gain, ffn_dw_w=ffn_w_full, ffn_dw_b=ffn_dw_b,
             final_gain=final_gain[None])
    grad_x, g, d_mod, loss, in_flight = _local_step(
        x[0], loss_target[0], mod, p, w_in_g, w_out_g.reshape(D_MODEL, D_MODEL), w_up_part, w_down_part)

    n_mod = d_mod.shape[1]
    dmod_rows = lax.dynamic_update_slice(jnp.zeros((N_DEV, n_mod), F32), d_mod, (me, 0))
    g["b_ada"] = d_mod
    small = _pack([g[k] for k in SMALL_REPLICATED] + [g[k] for k in SMALL_SHARDED] + [dmod_rows, loss[0, :1]], PACK_ROWS)
    (land_w_in, land_w_out), small = _final_comm([in_flight["w_in16"], in_flight["w_out16"]], small)
    pos = jnp.stack(_coords()).astype(jnp.int32)
    halves = _scatter_sum(pos, [g["w_in"], g["w_out"], g["w_up"], g["w_down"]],
                          [land_w_in, land_w_out, in_flight["land_w_up"], in_flight["land_w_down"]])
    full = _swap_halves(halves)
    grads = dict(w_in=full[0].reshape(w_in.shape[1:]), w_out=full[1].reshape(w_out.shape[1:]),
                 w_up=full[2].reshape(w_up.shape[1:]), w_down=full[3].reshape(w_down.shape[1:]))

    small_shapes = ([weights[k].shape for k in SMALL_REPLICATED] + [(CONV_K, D_HALF), (FFN_K, 2 * D_FF)]
                    + [(N_DEV, n_mod), (1,)])
    *small_grads, conv_w_grad, ffn_w_grad, dmod_all, loss_sum = _unpack(small, small_shapes)
    grads.update(zip(SMALL_REPLICATED, small_grads))
    grads["conv_dw_w"] = lax.dynamic_slice(conv_w_grad, (0, shard * conv_dw_w.shape[2]), conv_dw_w.shape[1:])[None]
    grads["ffn_dw_w"] = lax.dynamic_slice(ffn_w_grad, (0, shard * ffn_dw_w.shape[2]), ffn_dw_w.shape[1:])[None]

    delta, new_m, new_v = {}, {}, {}
    projections = ["w_in", "w_out", "w_up", "w_down"]
    group_out = _adamw_group([weights[k][0] for k in projections], [grads[k] for k in projections],
                             [mom1[k][0] for k in projections], [mom2[k][0] for k in projections], n_steps=4)
    for d, arrs in zip((delta, new_m, new_v), group_out):
        d.update({k: a[None] for k, a in zip(projections, arrs)})
    for k in projections:
        grads[k] = grads[k][None]
    dmod_sh = lax.dynamic_slice(dmod_all, (0, shard * ada_cols), (N_DEV, ada_cols))
    pad8 = ((0, 16 - N_DEV), (0, 0))
    grads["w_ada"], delta["w_ada"], new_m["w_ada"], new_v["w_ada"] = [a[None] for a in _adamw_ada(
        jnp.pad(c_all, pad8), jnp.pad(dmod_sh, pad8), w_ada[0], m_w_ada[0], v_w_ada[0], 256)]
    small_names = SMALL_REPLICATED + SMALL_SHARDED

    def two_d(a):
        return a.reshape(1, -1) if a.ndim == 1 else a

    small_out = _adamw_many(*[[two_d(d[k]) for k in small_names] for d in (weights, grads, mom1, mom2)])
    for d, arrs in zip((delta, new_m, new_v), small_out):
        d.update({k: a.reshape(weights[k].shape) for k, a in zip(small_names, arrs)})

    return (loss_sum.reshape(()), grad_x[None], *[grads[k] for k in WEIGHT_ORDER], *[delta[k] for k in WEIGHT_ORDER],
            *[new_m[k] for k in WEIGHT_ORDER], *[new_v[k] for k in WEIGHT_ORDER])
```

```python
import jax
import jax.numpy as jnp
from jax import lax
from jax.experimental import pallas as pl
from jax.experimental.pallas import tpu as pltpu

F32 = jnp.float32
BF16 = jnp.bfloat16

D_MODEL = 1024
D_HALF = 512
D_FF = 2816
CONV_K = 31
FFN_K = 3
CHUNK = 128
N_HEADS = 8
HEAD_DIM = 64
N_SHARD = 4
N_DEV = 8
RMS_EPS = 1e-6
LN_EPS = 1e-5
ADAM_LR, ADAM_B1, ADAM_B2, ADAM_EPS, ADAM_WD, ADAM_STEP = 0.001, 0.9, 0.999, 1e-08, 0.01, 10

TILE = 256
HALO = 32
FFN_HALO = 16
FFN_BLK = 256
UP_SHARD = 2 * D_FF // N_SHARD
VMEM_LIMIT_BYTES = 56 * 1024 * 1024
FFN_VMEM_LIMIT_BYTES = 58 * 1024 * 1024

ANY = pl.BlockSpec(memory_space=pl.ANY)
NT_DIMS = (((1,), (1,)), ((), ()))
TN_DIMS = (((0,), (0,)), ((), ()))


def _full(shape):
    return pl.BlockSpec(shape, lambda i: (0,) * len(shape))


def _nn(a, b):
    return jnp.dot(a, b, preferred_element_type=F32)


def _nt(a, b):
    return lax.dot_general(a, b, NT_DIMS, preferred_element_type=F32)


def _tn(a, b):
    return lax.dot_general(a, b, TN_DIMS, preferred_element_type=F32)


def _colsum(a):
    return jnp.sum(a, axis=0, keepdims=True)


def _params(semantics=("arbitrary",)):
    return pltpu.CompilerParams(dimension_semantics=semantics, vmem_limit_bytes=VMEM_LIMIT_BYTES)


def _rms(v, gain):
    return v * lax.rsqrt(jnp.mean(v * v, axis=-1, keepdims=True) + RMS_EPS) * gain


def _layer_norm(v, gain, bias):
    mu = jnp.mean(v, axis=-1, keepdims=True)
    var = jnp.mean(jnp.square(v - mu), axis=-1, keepdims=True)
    return (v - mu) * lax.rsqrt(var + LN_EPS) * gain + bias


def _mod_norm(v, gain, scale, shift):
    return _rms(v, gain) * (1.0 + scale) + shift


def _conv_branch(a1, ln_g, ln_b, out_gain):
    a2 = _layer_norm(a1, ln_g, ln_b)
    return _rms(a2 * jax.nn.sigmoid(a2), out_gain)


def _gate_branch(gu, sp, out_gain):
    return _rms(jax.nn.gelu(gu) * sp, out_gain)


def _gv_norm(gv, ln_g, ln_b):
    return _layer_norm(jax.nn.gelu(gv), ln_g, ln_b)


def _rms_parts(v):
    r = lax.rsqrt(jnp.mean(v * v, axis=-1, keepdims=True) + RMS_EPS)
    return v * r, r


def _rms_back(dn, n, r):
    return r * (dn - n * jnp.mean(dn * n, axis=-1, keepdims=True))


def _ln_parts(v):
    mu = jnp.mean(v, axis=-1, keepdims=True)
    rs = lax.rsqrt(jnp.mean(jnp.square(v - mu), axis=-1, keepdims=True) + LN_EPS)
    return (v - mu) * rs, rs


def _ln_back(dn, n, rs):
    return rs * (dn - jnp.mean(dn, axis=-1, keepdims=True) - n * jnp.mean(dn * n, axis=-1, keepdims=True))


GELU_C = 0.7978845608028654
GELU_A = 0.044715


def _gelu_parts(v):
    v2 = v * v
    th = jnp.tanh(GELU_C * (v + GELU_A * (v2 * v)))
    cdf = 0.5 * (1.0 + th)
    return v * cdf, cdf + (0.5 * GELU_C) * v * (1.0 - th * th) * (1.0 + (3.0 * GELU_A) * v2)


def _rms_vjp(v, gain):
    n, r = _rms_parts(v)
    return n * gain, lambda dy: (_rms_back(dy * gain, n, r), _colsum(dy * n))


def _mod_norm_vjp(v, gain, scale, shift):
    n, r = _rms_parts(v)

    def back(dy):
        q = _colsum(dy * n)
        return _rms_back(dy * (gain * (1.0 + scale)), n, r), q * (1.0 + scale), q * gain, _colsum(dy)

    return n * gain * (1.0 + scale) + shift, back


def _conv_branch_vjp(a1, ln_g, ln_b, out_gain):
    n1, rs1 = _ln_parts(a1)
    a2 = n1 * ln_g + ln_b
    s = jax.nn.sigmoid(a2)
    a3 = a2 * s
    n3, r3 = _rms_parts(a3)

    def back(dy):
        da2 = _rms_back(dy * out_gain, n3, r3) * (s + a3 * (1.0 - s))
        return _ln_back(da2 * ln_g, n1, rs1), _colsum(da2 * n1), _colsum(da2), _colsum(dy * n3)

    return n3 * out_gain, back


def _gate_branch_vjp(gu, sp, out_gain):
    ge, dge = _gelu_parts(gu)
    n, r = _rms_parts(ge * sp)

    def back(dy):
        dg = _rms_back(dy * out_gain, n, r)
        return dg * sp * dge, dg * ge, _colsum(dy * n)

    return n * out_gain, back


def _gv_norm_vjp(gv, ln_g, ln_b):
    ge, dge = _gelu_parts(gv)
    n, rs = _ln_parts(ge)
    return n * ln_g + ln_b, lambda dy: (_ln_back(dy * ln_g, n, rs) * dge, _colsum(dy * n), _colsum(dy))


def _head_pair_matmul(wp_ref, v):
    lane = lax.broadcasted_iota(jnp.int32, (CHUNK, CHUNK), 1)
    rows = []
    for n in range(v.shape[0] // CHUNK):
        cols = []
        for j in range(N_HEADS // 2):
            r = _nn(wp_ref[j], v[n * CHUNK:(n + 1) * CHUNK, j * CHUNK:(j + 1) * CHUNK])
            cols.append(jnp.where(lane < HEAD_DIM, r[:CHUNK], r[CHUNK:]))
        rows.append(jnp.concatenate(cols, axis=1))
    return jnp.concatenate(rows, axis=0)


def _tile_bias(bs, tokens):
    return jnp.concatenate([bs] * (tokens // CHUNK), axis=0)


FORWARD_LEAD = 8


def _fwd_mixer(x, vec, conv_w, wpair, bs_full, w_in_g, w_out_g, late_parts):
    seq = x.shape[0]
    n_tiles = seq // TILE
    t = TILE
    n_late = len(late_parts)
    forward_step = max(n_tiles - FORWARD_LEAD, 0)
    names = ["norm1_gain", "sc1", "sh1", "gt1", "conv_dw_b", "conv_ln_g", "conv_ln_b", "gm_ln_g", "gm_ln_b",
             "mix_out_gain"]
    vecs = [vec[k] for k in names]

    def body(x_ref, g1, sc1, sh1, gt1, cb, clg, clb, vg, vb, mg, cw, wp, bs, win_hbm, wout_hbm, *rest):
        late = rest[n_late:2 * n_late]
        z_ref, a1_ref, sp_ref, y_ref, o1_ref, x2_ref = rest[2 * n_late:2 * n_late + 6]
        win_v, wout_v, halo, bank, sem, send_sems, recv_sems = rest[2 * n_late + 6:]
        i = pl.program_id(0)
        mx, my, mc = _coords()
        shard = 2 * mx + my

        def half(w, which):
            h = late[w].shape[1] // 2
            return pl.ds(pl.multiple_of(which * h, 16), h)

        def chip_of(j):
            return 2 * _flip(mx, CHIP_FLIPS[j][0]) + _flip(my, CHIP_FLIPS[j][1])

        def ici_copy(w, j, slot):
            rows = late[w].at[slot, half(w, mc)]
            return pltpu.make_async_remote_copy(
                src_ref=rows, dst_ref=rows, send_sem=send_sems.at[w, j], recv_sem=recv_sems.at[w, j],
                device_id=(_flip(mx, CHIP_FLIPS[j][0]), _flip(my, CHIP_FLIPS[j][1]), mc), device_id_type=MESH)

        def d2d_copy(w, j, which):
            rows = late[w].at[chip_of(j), half(w, which)]
            return pltpu.make_async_remote_copy(
                src_ref=rows, dst_ref=rows, send_sem=send_sems.at[w, len(CHIP_FLIPS) + j],
                recv_sem=recv_sems.at[w, len(CHIP_FLIPS) + j], device_id=(mx, my, 1 - mc), device_id_type=MESH)

        pairs = [(w, j) for w in range(n_late) for j in range(len(CHIP_FLIPS))]

        @pl.when(i == 0)
        def _():
            for w, j in pairs:
                ici_copy(w, j, shard).start()
            cps = [pltpu.make_async_copy(win_hbm, win_v, sem.at[0]),
                   pltpu.make_async_copy(wout_hbm, wout_v, sem.at[1])]
            for cp in cps:
                cp.start()
            for cp in cps:
                cp.wait()
            halo[...] = jnp.zeros_like(halo)

        @pl.when(i == forward_step)
        def _():
            for w, j in pairs:
                ici_copy(w, j, chip_of(j)).wait_recv()
                d2d_copy(w, j, mc).start()

        xv = x_ref[...]
        h1b = _mod_norm(xv, g1[...], sc1[...], sh1[...]).astype(BF16)
        zs = [_nn(h1b, win_v[k]) for k in range(N_SHARD)]
        for k in range(N_SHARD):
            z_ref[:, k * D_HALF:(k + 1) * D_HALF] = zs[k]
        ca, cg, gu, gv = zs
        a0 = ca * jax.nn.sigmoid(cg)
        ext = jnp.concatenate([halo[...], a0], axis=0)
        halo[...] = a0[t - HALO:]
        bank[0] = ext
        for b in range(1, 8):
            bank[b] = pltpu.roll(ext, b, axis=0)
        a1 = jnp.zeros((t, D_HALF), F32) + cb[...]
        for s in range(CONV_K):
            q, b = divmod(s, 8)
            a1 = a1 + bank[b, pl.ds(HALO - 8 * q, t), :] * cw[pl.ds(CONV_K - 1 - s, 1), :]
        a1_ref[...] = a1
        mgv = mg[...]
        ya = _conv_branch(a1, clg[...], clb[...], mgv[:, :D_HALF])
        gvn = _gv_norm(gv, vg[...], vb[...]).astype(BF16)
        sp = _head_pair_matmul(wp, gvn) + _tile_bias(bs[...], t)
        sp_ref[...] = sp
        yg = _gate_branch(gu, sp, mgv[:, D_HALF:])
        yb = jnp.concatenate([ya, yg], axis=1).astype(BF16)
        y_ref[...] = yb
        o1 = _nn(yb, wout_v[...])
        o1_ref[...] = o1
        x2_ref[...] = xv + gt1[...] * o1

        @pl.when(i == n_tiles - 1)
        def _():
            for w, j in pairs:
                d2d_copy(w, j, 1 - mc).wait_recv()
            for w, j in pairs:
                ici_copy(w, j, shard).wait_send()
                d2d_copy(w, j, mc).wait_send()

    def row(width):
        return pl.BlockSpec((t, width), lambda i: (i, 0))

    out_shape = [jax.ShapeDtypeStruct((seq, 4 * D_HALF), F32), jax.ShapeDtypeStruct((seq, D_HALF), F32),
                 jax.ShapeDtypeStruct((seq, D_HALF), F32), jax.ShapeDtypeStruct((seq, D_MODEL), BF16),
                 jax.ShapeDtypeStruct((seq, D_MODEL), F32), jax.ShapeDtypeStruct((seq, D_MODEL), F32)]
    n_in = 1 + len(vecs) + 3 + 2
    sem_shape = (n_late, 2 * len(CHIP_FLIPS))
    outs = pl.pallas_call(
        body, grid=(n_tiles,), name="fwd_mixer",
        in_specs=[row(D_MODEL)] + [_full(v.shape) for v in vecs]
        + [_full(conv_w.shape), _full(wpair.shape), _full(bs_full.shape), ANY, ANY] + [ANY] * n_late,
        out_specs=[ANY] * n_late + [row(4 * D_HALF), row(D_HALF), row(D_HALF), row(D_MODEL), row(D_MODEL),
                                    row(D_MODEL)],
        out_shape=[jax.ShapeDtypeStruct(a.shape, a.dtype) for a in late_parts] + out_shape,
        input_output_aliases={n_in + w: w for w in range(n_late)},
        scratch_shapes=[pltpu.VMEM(w_in_g.shape, BF16), pltpu.VMEM(w_out_g.shape, BF16),
                        pltpu.VMEM((HALO, D_HALF), F32), pltpu.VMEM((8, t + HALO, D_HALF), F32),
                        pltpu.SemaphoreType.DMA((2,)), pltpu.SemaphoreType.DMA(sem_shape),
                        pltpu.SemaphoreType.DMA(sem_shape)],
        compiler_params=_params(),
    )(x, *vecs, conv_w, wpair, bs_full, w_in_g, w_out_g, *late_parts)
    return outs[n_late:], outs[:n_late]


def _interleave_matrices():
    row = jnp.arange(TILE)
    token_of_row = (row % 8) * (TILE // 8) + row // 8
    to_inter = (token_of_row[:, None] == row[None, :]).astype(BF16)
    return to_inter, jnp.transpose(to_inter)


def _ffn(x2, target, norm2_gain, sc2, sh2, ffn_w, ffn_b, gt2, final_gain, w_up_g, w_down_g, to_inter, to_natural):
    seq = x2.shape[0]
    n_tiles = seq // TILE
    t = TILE
    n_blk = D_FF // FFN_BLK
    inv_d = 1.0 / D_MODEL

    def body(x2_ref, x2h_ref, tgt_ref, g2, sc2_ref, sh2_ref, fw, fb, gt2_ref, fg, pm_ref, pmt_ref, wup_hbm, wd_hbm,
             du_ref, dx2_ref, dfw_ref, dfb_ref, dfg_ref, dgt2_ref, dg2_ref, dsc2_ref, dsh2_ref, loss_ref, dwd_hbm,
             dwd16_hbm, wup_v, wd_v, dwd_acc, carry, u_s, sil_s, vds_s, f_s, du_s, sem):
        i = pl.program_id(0)
        tile = n_tiles - 1 - i
        sublane = lax.broadcasted_iota(jnp.int32, (8, FFN_BLK), 0)

        @pl.when(i == 0)
        def _():
            cps = [pltpu.make_async_copy(wd_hbm, wd_v, sem.at[0])]
            cps += [pltpu.make_async_copy(wup_hbm.at[k], wup_v.at[:, pl.ds(k * UP_SHARD, UP_SHARD)], sem.at[3 + k])
                    for k in range(N_SHARD)]
            for cp in cps:
                cp.start()
            for cp in cps:
                cp.wait()
            dwd_acc[...] = jnp.zeros_like(dwd_acc)
            carry[...] = jnp.zeros_like(carry)
            dfw_ref[...] = jnp.zeros_like(dfw_ref)
            dfb_ref[...] = jnp.zeros_like(dfb_ref)
            dfg_ref[...] = jnp.zeros_like(dfg_ref)
            dgt2_ref[...] = jnp.zeros_like(dgt2_ref)
            dg2_ref[...] = jnp.zeros_like(dg2_ref)
            dsc2_ref[...] = jnp.zeros_like(dsc2_ref)
            dsh2_ref[...] = jnp.zeros_like(dsh2_ref)
            loss_ref[...] = jnp.zeros_like(loss_ref)

        def cols_of(j):
            return pl.ds(j * FFN_BLK, FFN_BLK), pl.ds(D_FF + j * FFN_BLK, FFN_BLK)

        def wrap_down(last, before):
            return jnp.where(sublane == 0, pltpu.roll(before, 1, axis=0), pltpu.roll(last, 1, axis=0))

        def wrap_up(first, after):
            return jnp.where(sublane == 7, pltpu.roll(after, 7, axis=0), pltpu.roll(first, 7, axis=0))

        x2v = x2_ref[...]
        h2, h2_vjp = _mod_norm_vjp(x2v, g2[...], sc2_ref[...], sh2_ref[...])
        h2b = h2.astype(BF16)
        h2_before = _mod_norm(x2h_ref[...], g2[...], sc2_ref[...], sh2_ref[...]).astype(BF16)
        lhs = jnp.concatenate([_nn(pm_ref[...], h2b).astype(BF16), h2_before], axis=0)

        def up(j):
            cv, cg = cols_of(j)
            return _nn(lhs, wup_v[:, cv]), _nn(lhs, wup_v[:, cg])

        def conv(both, cols):
            cur = both[:t]
            u_s[:, cols] = cur.astype(BF16)
            before = jnp.where(tile > 0, both[t:], 0.0)
            w1 = wrap_down(cur[t - 8:], before)
            w2 = wrap_down(cur[t - 16:t - 8], pltpu.roll(before, 1, axis=0))
            back1 = jnp.concatenate([w1, cur[:t - 8]], axis=0)
            back2 = jnp.concatenate([w2, w1, cur[:t - 16]], axis=0)
            return (fb[:, cols] + cur * fw[pl.ds(2, 1), cols] + back1 * fw[pl.ds(1, 1), cols]
                    + back2 * fw[pl.ds(0, 1), cols])

        pm_t = pmt_ref[...]

        def to_natural_f32(a):
            hi = a.astype(BF16)
            rest = a - hi.astype(F32)
            mid = rest.astype(BF16)
            low = (rest - mid.astype(F32)).astype(BF16)
            return _nn(jnp.concatenate([pm_t, pm_t, pm_t], axis=1), jnp.concatenate([hi, mid, low], axis=0))

        o2 = jnp.zeros((t, D_MODEL), F32)
        ahead_uv = up(0)
        for j in range(n_blk):
            cv, cg = cols_of(j)
            both_v, both_g = ahead_uv
            if j + 1 < n_blk:
                ahead_uv = up(j + 1)
            val, gate = conv(both_v, cv), conv(both_g, cg)
            sig = jax.nn.sigmoid(gate)
            sil = gate * sig
            fb16 = (sil * val).astype(BF16)
            sil_s[:, cv] = sil
            vds_s[:, cv] = val * (sig + sil * (1.0 - sig))
            f_s[:, cv] = fb16
            o2 = o2 + _nn(fb16, wd_v[pl.ds(j * FFN_BLK, FFN_BLK), :])
        o2 = to_natural_f32(o2)

        gt2v = gt2_ref[...]
        x3 = x2v + gt2v * o2
        out, out_vjp = _rms_vjp(x3, fg[...])
        diff = out - tgt_ref[...]
        loss_ref[...] += jnp.zeros_like(loss_ref) + 0.5 * inv_d * jnp.sum(diff * diff)
        dx3, dfg = out_vjp(diff * inv_d)
        dfg_ref[...] += dfg
        dgt2_ref[...] += _colsum(dx3 * o2)
        do2b = _nn(pm_ref[...], (gt2v * dx3).astype(BF16)).astype(BF16)

        def conv_back(dd, cols):
            dfb_ref[:, cols] += _colsum(dd)
            nxt = carry[:, cols]
            w1 = wrap_up(dd[:8], nxt[:8])
            w2 = wrap_up(dd[8:16], nxt[8:])
            ahead = (dd, jnp.concatenate([dd[8:], w1], axis=0), jnp.concatenate([dd[16:], w1, w2], axis=0))
            carry[:, cols] = dd[:16]
            uv = u_s[:, cols].astype(F32)
            du = jnp.zeros((t, FFN_BLK), F32)
            for s in range(FFN_K):
                du = du + ahead[s] * fw[pl.ds(FFN_K - 1 - s, 1), cols]
                dfw_ref[pl.ds(FFN_K - 1 - s, 1), cols] += _colsum(ahead[s] * uv)
            du_s[:, cols] = du.astype(BF16)

        for j in range(n_blk):
            cv, cg = cols_of(j)
            rows = pl.ds(j * FFN_BLK, FFN_BLK)
            df = _nt(do2b, wd_v[rows, :])
            dwd_acc[rows, :] += _tn(f_s[:, cv], do2b)
            conv_back(df * sil_s[:, cv], cv)
            conv_back(df * vds_s[:, cv], cg)
        du16 = _nn(pm_t, du_s[...]).astype(BF16)
        du_ref[...] = du16
        dx2, dg2, dsc2, dsh2 = h2_vjp(_nt(du16, wup_v[...]))
        dx2_ref[...] = dx3 + dx2
        dg2_ref[...] += dg2
        dsc2_ref[...] += dsc2
        dsh2_ref[...] += dsh2

        @pl.when(i == n_tiles - 1)
        def _():
            cp = pltpu.make_async_copy(dwd_acc, dwd_hbm, sem.at[1])
            cp.start()
            wd_v[...] = dwd_acc[...].astype(BF16)
            cp16 = pltpu.make_async_copy(wd_v, dwd16_hbm, sem.at[2])
            cp16.start()
            cp.wait()
            cp16.wait()

    def rev(width):
        return pl.BlockSpec((t, width), lambda i: (n_tiles - 1 - i, 0))

    assert FFN_K == 3
    halo_spec = pl.BlockSpec((8, D_MODEL), lambda i: (jnp.maximum((n_tiles - 1 - i) * (t // 8) - 1, 0), 0))
    vec_spec = _full((1, D_MODEL))
    out_shape = [jax.ShapeDtypeStruct((seq, 2 * D_FF), BF16), jax.ShapeDtypeStruct((seq, D_MODEL), F32),
                 jax.ShapeDtypeStruct((FFN_K, 2 * D_FF), F32), jax.ShapeDtypeStruct((1, 2 * D_FF), F32),
                 jax.ShapeDtypeStruct((1, D_MODEL), F32), jax.ShapeDtypeStruct((1, D_MODEL), F32),
                 jax.ShapeDtypeStruct((1, D_MODEL), F32), jax.ShapeDtypeStruct((1, D_MODEL), F32),
                 jax.ShapeDtypeStruct((1, D_MODEL), F32),
                 jax.ShapeDtypeStruct((1, 128), F32), jax.ShapeDtypeStruct((D_FF, D_MODEL), F32),
                 jax.ShapeDtypeStruct((D_FF, D_MODEL), BF16)]
    return pl.pallas_call(
        body, grid=(n_tiles,), name="ffn",
        in_specs=[rev(D_MODEL), halo_spec, rev(D_MODEL), vec_spec, vec_spec, vec_spec, _full(ffn_w.shape),
                  _full(ffn_b.shape), _full(gt2.shape), _full(final_gain.shape), _full(to_inter.shape),
                  _full(to_natural.shape), ANY, ANY],
        out_specs=[rev(2 * D_FF), rev(D_MODEL), _full((FFN_K, 2 * D_FF)), _full((1, 2 * D_FF)), vec_spec, vec_spec,
                   vec_spec, vec_spec, vec_spec, _full((1, 128)), ANY, ANY],
        out_shape=out_shape,
        scratch_shapes=[pltpu.VMEM((D_MODEL, 2 * D_FF), BF16), pltpu.VMEM((D_FF, D_MODEL), BF16),
                        pltpu.VMEM((D_FF, D_MODEL), F32), pltpu.VMEM((FFN_HALO, 2 * D_FF), F32),
                        pltpu.VMEM((t, 2 * D_FF), BF16), pltpu.VMEM((t, D_FF), F32), pltpu.VMEM((t, D_FF), F32),
                        pltpu.VMEM((t, D_FF), BF16), pltpu.VMEM((t, 2 * D_FF), BF16),
                        pltpu.SemaphoreType.DMA((3 + N_SHARD,))],
        compiler_params=pltpu.CompilerParams(dimension_semantics=("arbitrary",), vmem_limit_bytes=FFN_VMEM_LIMIT_BYTES),
    )(x2, x2, target, norm2_gain, sc2, sh2, ffn_w, ffn_b, gt2, final_gain, to_inter, to_natural, w_up_g, w_down_g)


def _scatter_copies(src16, land, send_sems, recv_sems):
    x, y, c = _coords()
    h = src16.shape[1] // 2
    copies = []
    for f, flip in enumerate(PEER_FLIPS):
        tx, ty, tc = _flip(x, flip[0]), _flip(y, flip[1]), _flip(c, flip[2])
        copies.append(pltpu.make_async_remote_copy(
            src_ref=src16.at[2 * tx + ty, pl.ds(pl.multiple_of(tc * h, 16), h)], dst_ref=land.at[f],
            send_sem=send_sems.at[f], recv_sem=recv_sems.at[f], device_id=(tx, ty, tc), device_id_type=MESH))
    return copies


def _land_shape(src16):
    return jax.ShapeDtypeStruct((len(PEER_FLIPS), src16.shape[1] // 2, src16.shape[2]), BF16)


UP_TILE = 512


def _bwd_up(du, x2, norm2_gain, sc2, sh2, dwd16):
    seq = x2.shape[0]
    t = UP_TILE if seq % UP_TILE == 0 else TILE
    n_tiles = seq // t
    acc_shape = (N_SHARD, D_MODEL, UP_SHARD)

    def body(du_ref, x2_ref, g2, sc2_ref, sh2_ref, dwd16_hbm, dwup_hbm, dwup16_hbm, land_hbm,
             stage16, dwup_acc, sem, send_sems, recv_sems):
        i = pl.program_id(0)

        @pl.when(i == 0)
        def _():
            for cp in _scatter_copies(dwd16_hbm, land_hbm, send_sems, recv_sems):
                cp.start()
            dwup_acc[...] = jnp.zeros_like(dwup_acc)

        h2b = _mod_norm(x2_ref[...], g2[...], sc2_ref[...], sh2_ref[...]).astype(BF16)
        for k in range(N_SHARD):
            dwup_acc[k] += _tn(h2b, du_ref[:, k * UP_SHARD:(k + 1) * UP_SHARD])

        @pl.when(i == n_tiles - 1)
        def _():
            cp = pltpu.make_async_copy(dwup_acc, dwup_hbm, sem.at[0])
            cp.start()
            for k in range(N_SHARD):
                stage16[k] = dwup_acc[k].astype(BF16)
            cp16 = pltpu.make_async_copy(stage16, dwup16_hbm, sem.at[1])
            cp16.start()
            cp.wait()
            cp16.wait()
            for rc in _scatter_copies(dwd16_hbm, land_hbm, send_sems, recv_sems):
                rc.wait()

    def row(width):
        return pl.BlockSpec((t, width), lambda i: (i, 0))

    n_peer = len(PEER_FLIPS)
    return pl.pallas_call(
        body, grid=(n_tiles,), name="bwd_up",
        in_specs=[row(2 * D_FF), row(D_MODEL), _full((1, D_MODEL)), _full((1, D_MODEL)), _full((1, D_MODEL)), ANY],
        out_specs=[ANY, ANY, ANY],
        out_shape=[jax.ShapeDtypeStruct(acc_shape, F32), jax.ShapeDtypeStruct(acc_shape, BF16), _land_shape(dwd16)],
        scratch_shapes=[pltpu.VMEM(acc_shape, BF16), pltpu.VMEM(acc_shape, F32), pltpu.SemaphoreType.DMA((2,)),
                        pltpu.SemaphoreType.DMA((n_peer,)), pltpu.SemaphoreType.DMA((n_peer,))],
        compiler_params=_params(),
    )(du, x2, norm2_gain, sc2, sh2, dwd16)


def _bwd_mixer(dx2, x, z, a1, sp, yb, o1, vec, conv_w, wpair, wpair_t, causal_mask, w_in_g, w_out_g, dwup16):
    seq = x.shape[0]
    n_tiles = seq // TILE
    t = TILE
    names = ["norm1_gain", "sc1", "sh1", "gt1", "conv_ln_g", "conv_ln_b", "gm_ln_g", "gm_ln_b", "mix_out_gain"]
    vecs = [vec[k] for k in names]

    def body(dx2_ref, x_ref, z_ref, a1_ref, sp_ref, y_ref, o1_ref, g1, sc1, sh1, gt1, clg, clb, vg, vb, mg,
             cw, wp, wpt, mask_ref, win_hbm, wout_hbm, dwup16_hbm,
             gx_ref, dg1_ref, dsc1_ref, dsh1_ref, dgt1_ref, dcw_ref, dcb_ref, dclg_ref, dclb_ref, dvg_ref, dvb_ref,
             dmg_ref, dws_ref, dbs_ref, dwin_hbm, dwout_hbm, land_hbm, dwin16_hbm, dwout16_hbm,
             win_v, wout_v, dwin_acc, dwout_acc, carry, bank, dbs_acc, lwin, lwout, sem, send_sems, recv_sems,
             pair_send, pair_recv):
        i = pl.program_id(0)
        small = [dg1_ref, dsc1_ref, dsh1_ref, dgt1_ref, dcw_ref, dcb_ref, dclg_ref, dclb_ref, dvg_ref, dvb_ref,
                 dmg_ref, dws_ref, dbs_acc]

        @pl.when(i == 0)
        def _():
            for cp in _scatter_copies(dwup16_hbm, land_hbm, send_sems, recv_sems):
                cp.start()
            cps = [pltpu.make_async_copy(win_hbm, win_v, sem.at[0]),
                   pltpu.make_async_copy(wout_hbm, wout_v, sem.at[1])]
            for cp in cps:
                cp.start()
            for cp in cps:
                cp.wait()
            dwin_acc[...] = jnp.zeros_like(dwin_acc)
            dwout_acc[...] = jnp.zeros_like(dwout_acc)
            carry[...] = jnp.zeros_like(carry)
            for ref in small:
                ref[...] = jnp.zeros_like(ref)

        dx2v = dx2_ref[...]
        gt1v = gt1[...]
        dgt1_ref[...] += _colsum(dx2v * o1_ref[...])
        do1b = (gt1v * dx2v).astype(BF16)
        dy = _nt(do1b, wout_v[...])
        dwout_acc[...] += _tn(y_ref[...], do1b)

        mgv = mg[...]
        _, conv_vjp = _conv_branch_vjp(a1_ref[...], clg[...], clb[...], mgv[:, :D_HALF])
        da1, dclg, dclb, dmg_a = conv_vjp(dy[:, :D_HALF])
        dclg_ref[...] += dclg
        dclb_ref[...] += dclb
        gu = z_ref[:, 2 * D_HALF:3 * D_HALF]
        gv = z_ref[:, 3 * D_HALF:]
        spv = sp_ref[...]
        _, gate_vjp = _gate_branch_vjp(gu, spv, mgv[:, D_HALF:])
        dgu, dsp, dmg_g = gate_vjp(dy[:, D_HALF:])
        dmg_ref[...] += jnp.concatenate([dmg_a, dmg_g], axis=1)
        gvn, gv_vjp = _gv_norm_vjp(gv, vg[...], vb[...])
        gvnb = gvn.astype(BF16)
        dspb = dsp.astype(BF16)
        dgvn = _head_pair_matmul(wpt, dspb)
        dgv, dvg, dvb = gv_vjp(dgvn)
        dvg_ref[...] += dvg
        dvb_ref[...] += dvb
        lane = lax.broadcasted_iota(jnp.int32, (CHUNK, CHUNK), 1)
        dbs = jnp.zeros((CHUNK, D_HALF), F32)
        for n in range(t // CHUNK):
            rows = slice(n * CHUNK, (n + 1) * CHUNK)
            dbs = dbs + dsp[rows, :]
            for j in range(N_HEADS // 2):
                cols = slice(j * CHUNK, (j + 1) * CHUNK)
                blk = dspb[rows, cols]
                zero = jnp.zeros_like(blk)
                vblk = gvnb[rows, cols]
                dws_ref[2 * j] += _nt(jnp.where(lane < HEAD_DIM, blk, zero), vblk)
                dws_ref[2 * j + 1] += _nt(jnp.where(lane < HEAD_DIM, zero, blk), vblk)
        dbs_acc[...] += dbs

        h1, h1_vjp = _mod_norm_vjp(x_ref[...], g1[...], sc1[...], sh1[...])
        h1b = h1.astype(BF16)
        dh1 = jnp.zeros((t, D_MODEL), F32)
        for k, dzk in ((2, dgu), (3, dgv)):
            dzb = dzk.astype(BF16)
            dh1 = dh1 + _nt(dzb, win_v[k])
            dwin_acc[k] += _tn(h1b, dzb)

        ca = z_ref[:, :D_HALF]
        cg = z_ref[:, D_HALF:2 * D_HALF]
        sig = jax.nn.sigmoid(cg)
        a0 = ca * sig
        ext = jnp.concatenate([da1, carry[...]], axis=0)
        carry[...] = da1[:HALO]
        bank[0] = ext
        for b in range(1, 8):
            bank[b] = pltpu.roll(ext, t + HALO - b, axis=0)
        dcb_ref[...] += _colsum(da1)
        da0 = jnp.zeros((t, D_HALF), F32)
        for s in range(CONV_K):
            q, b = divmod(s, 8)
            shifted = bank[b, pl.ds(8 * q, t), :]
            da0 = da0 + shifted * cw[pl.ds(CONV_K - 1 - s, 1), :]
            dcw_ref[pl.ds(CONV_K - 1 - s, 1), :] += _colsum(shifted * a0)
        dca = da0 * sig
        dcg = da0 * ca * sig * (1.0 - sig)

        for k, dzk in ((0, dca), (1, dcg)):
            dzb = dzk.astype(BF16)
            dh1 = dh1 + _nt(dzb, win_v[k])
            dwin_acc[k] += _tn(h1b, dzb)
        dx, dg1, dsc1, dsh1 = h1_vjp(dh1)
        gx_ref[...] = dx2v + dx
        dg1_ref[...] += dg1
        dsc1_ref[...] += dsc1
        dsh1_ref[...] += dsh1

        @pl.when(i == n_tiles - 1)
        def _():
            for h in range(N_HEADS):
                dws_ref[h] = dws_ref[h] * mask_ref[...]
            head_of_lane = lax.broadcasted_iota(jnp.int32, (N_HEADS, D_HALF), 1) // HEAD_DIM
            pick = (head_of_lane == lax.broadcasted_iota(jnp.int32, (N_HEADS, D_HALF), 0)).astype(F32)
            dbs_ref[...] = lax.dot_general(pick, dbs_acc[...], NT_DIMS, precision=lax.Precision.HIGHEST,
                                           preferred_element_type=F32)
            for k in range(N_SHARD):
                win_v[k] = dwin_acc[k].astype(BF16)
            wout_v[...] = dwout_acc[...].astype(BF16)
            mx, my, mc = _coords()
            h_in, h_out = dwin_acc.shape[1] // 2, dwout_acc.shape[0] // (2 * N_SHARD)

            def in_rows(ref, k, which):
                return ref.at[k, pl.ds(pl.multiple_of(which * h_in, 16), h_in), :]

            def out_rows(ref, k, which):
                return ref.at[pl.ds(pl.multiple_of((2 * k + which) * h_out, 16), h_out), :]

            pairs = ((win_v, dwin_acc, lwin, in_rows, dwin_hbm, dwin16_hbm),
                     (wout_v, dwout_acc, lwout, out_rows, dwout_hbm, dwout16_hbm))
            swaps = [pltpu.make_async_remote_copy(
                src_ref=rows_of(v16, k, 1 - mc), dst_ref=land.at[k], send_sem=pair_send.at[w, k],
                recv_sem=pair_recv.at[w, k], device_id=(mx, my, 1 - mc), device_id_type=MESH)
                for w, (v16, _, land, rows_of, _, _) in enumerate(pairs) for k in range(N_SHARD)]
            for cp in swaps:
                cp.start()
            for cp in swaps:
                cp.wait()
            outs = []
            for w, (v16, acc, land, rows_of, half_hbm, half16_hbm) in enumerate(pairs):
                for k in range(N_SHARD):
                    total = rows_of(acc, k, mc)[...] + land[k].astype(F32)
                    rows_of(acc, k, 0)[...] = total
                    rows_of(v16, k, 0)[...] = total.astype(BF16)
                    outs.append(pltpu.make_async_copy(rows_of(acc, k, 0), half_hbm.at[k], sem.at[2 + 8 * w + k]))
                    outs.append(pltpu.make_async_copy(rows_of(v16, k, 0), half16_hbm.at[k], sem.at[6 + 8 * w + k]))
            for cp in outs:
                cp.start()
            for cp in outs:
                cp.wait()
            for rc in _scatter_copies(dwup16_hbm, land_hbm, send_sems, recv_sems):
                rc.wait()

    def rev(width):
        return pl.BlockSpec((t, width), lambda i: (n_tiles - 1 - i, 0))

    v1024 = jax.ShapeDtypeStruct((1, D_MODEL), F32)
    v512 = jax.ShapeDtypeStruct((1, D_HALF), F32)
    small_shapes = [v1024, v1024, v1024, v1024, jax.ShapeDtypeStruct((CONV_K, D_HALF), F32), v512, v512, v512, v512,
                    v512, v1024, jax.ShapeDtypeStruct((N_HEADS, CHUNK, CHUNK), F32),
                    jax.ShapeDtypeStruct((N_HEADS, CHUNK), F32)]
    n_peer = len(PEER_FLIPS)
    half_in = (N_SHARD, w_in_g.shape[1] // 2, w_in_g.shape[2])
    half_out = (N_SHARD, w_out_g.shape[0] // (2 * N_SHARD), w_out_g.shape[1])
    return pl.pallas_call(
        body, grid=(n_tiles,), name="bwd_mixer",
        in_specs=[rev(D_MODEL), rev(D_MODEL), rev(4 * D_HALF), rev(D_HALF), rev(D_HALF), rev(D_MODEL),
                  rev(D_MODEL)] + [_full(v.shape) for v in vecs]
        + [_full(conv_w.shape), _full(wpair.shape), _full(wpair_t.shape), _full(causal_mask.shape), ANY, ANY, ANY],
        out_specs=[rev(D_MODEL)] + [_full(s.shape) for s in small_shapes] + [ANY] * 5,
        out_shape=[jax.ShapeDtypeStruct((seq, D_MODEL), F32)] + small_shapes
        + [jax.ShapeDtypeStruct(half_in, F32), jax.ShapeDtypeStruct(half_out, F32), _land_shape(dwup16),
           jax.ShapeDtypeStruct(half_in, BF16), jax.ShapeDtypeStruct(half_out, BF16)],
        scratch_shapes=[pltpu.VMEM(w_in_g.shape, BF16), pltpu.VMEM(w_out_g.shape, BF16),
                        pltpu.VMEM(w_in_g.shape, F32), pltpu.VMEM(w_out_g.shape, F32),
                        pltpu.VMEM((HALO, D_HALF), F32), pltpu.VMEM((8, t + HALO, D_HALF), F32),
                        pltpu.VMEM((CHUNK, D_HALF), F32), pltpu.VMEM(half_in, BF16), pltpu.VMEM(half_out, BF16),
                        pltpu.SemaphoreType.DMA((2 + 4 * N_SHARD,)),
                        pltpu.SemaphoreType.DMA((n_peer,)), pltpu.SemaphoreType.DMA((n_peer,)),
                        pltpu.SemaphoreType.DMA((2, N_SHARD)), pltpu.SemaphoreType.DMA((2, N_SHARD))],
        compiler_params=_params(),
    )(dx2, x, z, a1, sp, yb, o1, *vecs, conv_w, wpair, wpair_t, causal_mask, w_in_g, w_out_g, dwup16)


def _gmlp_operands(gm_ws, gm_bs):
    mask = jnp.tril(jnp.ones((CHUNK, CHUNK), F32))
    ws = gm_ws * mask[None]
    wpair = ws.reshape(N_HEADS // 2, 2 * CHUNK, CHUNK).astype(BF16)
    wpair_t = jnp.swapaxes(ws, 1, 2).reshape(N_HEADS // 2, 2 * CHUNK, CHUNK).astype(BF16)
    bs_full = jnp.repeat(jnp.transpose(gm_bs), HEAD_DIM, axis=1)
    return wpair, wpair_t, bs_full, mask


def _local_step(x, target, mod, p, w_in_g, w_out_g, w_up_part, w_down_part):
    sh1, sc1, gt1, sh2, sc2, gt2 = [mod[:, k * D_MODEL:(k + 1) * D_MODEL] for k in range(6)]
    vec = dict(p, sh1=sh1, sc1=sc1, gt1=gt1, sh2=sh2, sc2=sc2, gt2=gt2)
    wpair, wpair_t, bs_full, mask = _gmlp_operands(p["gm_ws"], p["gm_bs"])

    (z, a1, sp, yb, o1, x2), (w_up_g, w_down_g) = _fwd_mixer(
        x, vec, p["conv_dw_w"], wpair, bs_full, w_in_g, w_out_g, [w_up_part, w_down_part])
    w_down_g = w_down_g.reshape(D_FF, D_MODEL)
    to_inter, to_natural = _interleave_matrices()
    du, dx2, d_ffn_w, d_ffn_b, d_fg, d_gt2, d_g2, d_sc2, d_sh2, loss, d_wd, d_wd16 = _ffn(
        x2, target, p["norm2_gain"], sc2, sh2, p["ffn_dw_w"], p["ffn_dw_b"], gt2, p["final_gain"], w_up_g, w_down_g,
        to_inter, to_natural)
    by_shard = (N_SHARD, -1, D_MODEL)
    d_wup, d_wup16, land_wd = _bwd_up(du, x2, p["norm2_gain"], sc2, sh2, d_wd16.reshape(by_shard))
    (gx, d_g1, d_sc1, d_sh1, d_gt1, d_cw, d_cb, d_clg, d_clb, d_vg, d_vb, d_mg, d_ws, d_bs, d_win, d_wout, land_wup,
     d_win16, d_wout16) = _bwd_mixer(dx2, x, z, a1, sp, yb, o1, vec, p["conv_dw_w"], wpair, wpair_t, mask, w_in_g,
                                     w_out_g, d_wup16)
    d_mod = _pack([d_sh1, d_sc1, d_gt1, d_sh2, d_sc2, d_gt2], 6).reshape(1, 6 * D_MODEL)
    grads = dict(norm1_gain=d_g1, conv_dw_w=d_cw, conv_dw_b=d_cb, conv_ln_g=d_clg, conv_ln_b=d_clb, gm_ln_g=d_vg,
                 gm_ln_b=d_vb, gm_ws=d_ws, gm_bs=d_bs, mix_out_gain=d_mg, norm2_gain=d_g2, ffn_dw_w=d_ffn_w,
                 ffn_dw_b=d_ffn_b, final_gain=d_fg, w_in=d_win, w_out=d_wout, w_up=d_wup, w_down=d_wd.reshape(by_shard))
    in_flight = dict(w_in16=d_win16, w_out16=d_wout16, land_w_up=land_wup, land_w_down=land_wd)
    return gx, grads, d_mod, loss, in_flight


MESH = pl.DeviceIdType.MESH
VMEM_SPEC = pl.BlockSpec(memory_space=pltpu.VMEM)
PEER_FLIPS = [(a, b, d) for a in (0, 1) for b in (0, 1) for d in (0, 1)][1:]
CHIP_FLIPS = [(1, 0), (0, 1), (1, 1)]


def _coords():
    return lax.axis_index("x"), lax.axis_index("y"), lax.axis_index("c")


def _flip(v, bit):
    return 1 - v if bit else v


def _rows8(block):
    return pl.ds(pl.multiple_of(8 * block, 8), 8)


def _ada_steps(c_ref, w_ref, b_ref, call_ref, mod_ref, cpad, modall, send_sems, recv_sems):
    x, y, c = _coords()
    me = 4 * x + 2 * y + c
    cpad[...] = jnp.zeros_like(cpad)
    cpad[pl.ds(0, 1), :] = c_ref[...]

    def gather_copy(j, flip):
        peer = (_flip(x, flip[0]), _flip(y, flip[1]), _flip(c, flip[2]))
        return pltpu.make_async_remote_copy(
            src_ref=cpad, dst_ref=call_ref.at[_rows8(me)], send_sem=send_sems.at[j], recv_sem=recv_sems.at[j],
            device_id=peer, device_id_type=MESH)

    def piece_copy(j, flip):
        tx, ty = _flip(x, flip[0]), _flip(y, flip[1])
        return pltpu.make_async_remote_copy(
            src_ref=modall.at[_rows8(4 * tx + 2 * ty + c)], dst_ref=mod_ref.at[_rows8(2 * x + y)],
            send_sem=send_sems.at[len(PEER_FLIPS) + j], recv_sem=recv_sems.at[len(PEER_FLIPS) + j],
            device_id=(tx, ty, c), device_id_type=MESH)

    copies = [gather_copy(j, f) for j, f in enumerate(PEER_FLIPS)]
    for cp in copies:
        cp.start()
    call_ref[_rows8(me), :] = cpad[...]

    def middle():
        for cp in copies:
            cp.wait_recv()
        for cp in copies:
            cp.wait_send()
        cv = call_ref[...]
        c_act = (cv * jax.nn.sigmoid(cv)).astype(BF16)
        modall[...] = _nn(c_act, w_ref[...].astype(BF16)) + b_ref[...]
        for j, f in enumerate(CHIP_FLIPS):
            piece_copy(j, f).start()
        mod_ref[_rows8(2 * x + y), :] = modall[_rows8(me), :]

    def finish():
        for j, f in enumerate(CHIP_FLIPS):
            piece_copy(j, f).wait_recv()
        for j, f in enumerate(CHIP_FLIPS):
            piece_copy(j, f).wait_send()

    return middle, finish


def _gather_weights(shards, filters, n_now, c_row, w_ada_sh, b_ada_sh):
    n = len(shards)
    nf = len(filters)
    ada_cols = w_ada_sh.shape[1]

    def body(*refs):
        ins, f_ins, ada_ins = refs[:n], refs[n:n + nf], refs[n + nf:n + nf + 3]
        refs = refs[n + nf + 3:]
        outs, f_outs, ada_outs = refs[:n], refs[n:n + nf], refs[n + nf:n + nf + 2]
        refs = refs[n + nf + 2:]
        stage = refs[:n]
        send_sems, recv_sems, local_sems, f_send_sems, f_recv_sems, cpad, modall, ada_send, ada_recv = refs[n:]
        ada_middle, ada_finish = _ada_steps(*ada_ins, *ada_outs, cpad, modall, ada_send, ada_recv)
        x, y, c = _coords()
        k = 2 * x + y
        sibling = (x, y, 1 - c)

        def filter_copy(w, j, slot):
            tx, ty = _flip(x, CHIP_FLIPS[j][0]), _flip(y, CHIP_FLIPS[j][1])
            return pltpu.make_async_remote_copy(
                src_ref=f_ins[w], dst_ref=f_outs[w].at[slot], send_sem=f_send_sems.at[w, j],
                recv_sem=f_recv_sems.at[w, j], device_id=(tx, ty, c), device_id_type=MESH)

        def half(w, which):
            h = shards[w].shape[0] // 2
            return pl.ds(pl.multiple_of(which * h, 16), h)

        def ici_copy(w, j, src, slot):
            tx, ty = _flip(x, CHIP_FLIPS[j][0]), _flip(y, CHIP_FLIPS[j][1])
            return pltpu.make_async_remote_copy(
                src_ref=src, dst_ref=outs[w].at[slot, half(w, c)], send_sem=send_sems.at[w, j],
                recv_sem=recv_sems.at[w, j], device_id=(tx, ty, c), device_id_type=MESH)

        def d2d_copy(w, j, slot, which):
            rows = outs[w].at[slot, half(w, which)]
            return pltpu.make_async_remote_copy(
                src_ref=rows, dst_ref=rows, send_sem=send_sems.at[w, len(CHIP_FLIPS) + j],
                recv_sem=recv_sems.at[w, len(CHIP_FLIPS) + j], device_id=sibling, device_id_type=MESH)

        def chip_of(j):
            return 2 * _flip(x, CHIP_FLIPS[j][0]) + _flip(y, CHIP_FLIPS[j][1])

        local, first, passed = [], [], []
        for w in range(nf):
            local.append(pltpu.make_async_copy(f_ins[w], f_outs[w].at[k], local_sems.at[n + w]))
            local[-1].start()
            for j in range(len(CHIP_FLIPS)):
                first.append(filter_copy(w, j, k))
                first[-1].start()
        for w in range(n):
            stage[w][...] = ins[w][...].astype(BF16)
            local.append(pltpu.make_async_copy(stage[w], outs[w].at[k], local_sems.at[w]))
            local[-1].start()
            if w < n_now:
                for j in range(len(CHIP_FLIPS)):
                    first.append(ici_copy(w, j, stage[w].at[half(w, c)], k))
                    first[-1].start()
        ada_middle()
        for w in range(nf):
            for j in range(len(CHIP_FLIPS)):
                filter_copy(w, j, chip_of(j)).wait_recv()
        for w in range(n_now):
            for j in range(len(CHIP_FLIPS)):
                ici_copy(w, j, stage[w].at[half(w, c)], chip_of(j)).wait_recv()
                passed.append(d2d_copy(w, j, chip_of(j), c))
                passed[-1].start()
        for w in range(n_now):
            for j in range(len(CHIP_FLIPS)):
                d2d_copy(w, j, chip_of(j), 1 - c).wait_recv()
        for cp in first + passed:
            cp.wait_send()
        for cp in local:
            cp.wait()
        ada_finish()

    sem_shape = (n_now, 2 * len(CHIP_FLIPS))
    f_sem_shape = (nf, len(CHIP_FLIPS))
    n_ada_sem = len(PEER_FLIPS) + len(CHIP_FLIPS)
    outs = pl.pallas_call(
        body, name="gather_weights",
        in_specs=[VMEM_SPEC] * (n + nf + 3), out_specs=[ANY] * (n + nf) + [VMEM_SPEC, VMEM_SPEC],
        out_shape=[jax.ShapeDtypeStruct((N_SHARD,) + s.shape, BF16) for s in shards]
        + [jax.ShapeDtypeStruct((N_SHARD,) + s.shape, F32) for s in filters]
        + [jax.ShapeDtypeStruct((8 * N_DEV, D_MODEL), F32), jax.ShapeDtypeStruct((8 * N_SHARD, ada_cols), F32)],
        scratch_shapes=[pltpu.VMEM(s.shape, BF16) for s in shards]
        + [pltpu.SemaphoreType.DMA(sem_shape), pltpu.SemaphoreType.DMA(sem_shape), pltpu.SemaphoreType.DMA((n + nf,)),
           pltpu.SemaphoreType.DMA(f_sem_shape), pltpu.SemaphoreType.DMA(f_sem_shape),
           pltpu.VMEM((8, D_MODEL), F32), pltpu.VMEM((8 * N_DEV, ada_cols), F32),
           pltpu.SemaphoreType.DMA((n_ada_sem,)), pltpu.SemaphoreType.DMA((n_ada_sem,))],
        compiler_params=pltpu.CompilerParams(vmem_limit_bytes=VMEM_LIMIT_BYTES),
    )(*shards, *filters, c_row, w_ada_sh, b_ada_sh)
    return outs[:n], outs[n:n + nf], outs[n + nf], outs[n + nf + 1]


def _final_comm(srcs16, small):
    n = len(srcs16)
    rows = small.shape[0]
    half = rows // 2
    quarter = half // 2

    def body(*refs):
        srcs, small_ref = refs[:n], refs[n]
        lands, small_out = refs[n + 1:2 * n + 1], refs[2 * n + 1]
        chip_sum, got_c, got_1, got_2, part, send_sems, recv_sems, small_send_sems, small_recv_sems = refs[2 * n + 2:]
        x, y, c = _coords()
        sibling = (x, y, 1 - c)
        mine = pl.ds(pl.multiple_of(c * half, 8), half)
        copies = []
        for w in range(n):
            for j, flip in enumerate(CHIP_FLIPS):
                tx, ty = _flip(x, flip[0]), _flip(y, flip[1])
                copies.append(pltpu.make_async_remote_copy(
                    src_ref=srcs[w].at[2 * tx + ty], dst_ref=lands[w].at[j], send_sem=send_sems.at[w, j],
                    recv_sem=recv_sems.at[w, j], device_id=(tx, ty, c), device_id_type=MESH))
        for cp in copies:
            cp.start()

        def exchange(pairs):
            rcs = [pltpu.make_async_remote_copy(
                src_ref=src, dst_ref=dst, send_sem=small_send_sems.at[k], recv_sem=small_recv_sems.at[k],
                device_id=peer, device_id_type=MESH) for k, src, dst, peer in pairs]
            for rc in rcs:
                rc.start()
            for rc in rcs:
                rc.wait()

        def quarter_rows(q):
            return pl.ds(pl.multiple_of(c * half + q * quarter, 8), quarter)

        along = ((1 - x, y, c), (x, 1 - y, c))
        exchange([(0, small_ref, got_c, sibling)])
        chip_sum[...] = small_ref[...] + got_c[...]
        exchange([(1 + q, chip_sum.at[quarter_rows(q)], got_1.at[q], along[q]) for q in range(2)])
        for q in range(2):
            part[q] = chip_sum[quarter_rows(q), :] + got_1[q]
        exchange([(3 + q, part.at[q], got_2.at[q], along[1 - q]) for q in range(2)])
        for q in range(2):
            small_out[quarter_rows(q), :] = part[q] + got_2[q]
        exchange([(5, small_out.at[mine], small_out.at[mine], sibling)])
        for cp in copies:
            cp.wait()

    n_chip = len(CHIP_FLIPS)
    quarter_shape = (2, quarter, small.shape[1])
    outs = pl.pallas_call(
        body, name="final_comm",
        in_specs=[ANY] * n + [VMEM_SPEC], out_specs=[ANY] * n + [VMEM_SPEC],
        out_shape=[jax.ShapeDtypeStruct((n_chip,) + a.shape[1:], BF16) for a in srcs16]
        + [jax.ShapeDtypeStruct(small.shape, F32)],
        scratch_shapes=[pltpu.VMEM(small.shape, F32), pltpu.VMEM(small.shape, F32), pltpu.VMEM(quarter_shape, F32),
                        pltpu.VMEM(quarter_shape, F32), pltpu.VMEM(quarter_shape, F32),
                        pltpu.SemaphoreType.DMA((n, n_chip)), pltpu.SemaphoreType.DMA((n, n_chip)),
                        pltpu.SemaphoreType.DMA((6,)), pltpu.SemaphoreType.DMA((6,))],
        compiler_params=pltpu.CompilerParams(vmem_limit_bytes=VMEM_LIMIT_BYTES),
    )(*srcs16, small)
    return outs[:n], outs[n]


ADD_CHUNKS = 4


def _scatter_sum(pos, owns, lands):
    n = len(owns)

    def specs(own_shape, land_shape):
        peers, rows, cols = land_shape
        pick = 1 if own_shape[1] == 2 * rows else 0
        if cols % (128 * ADD_CHUNKS) == 0:
            blk = (rows, cols // ADD_CHUNKS)
            return (pl.BlockSpec((1,) + blk, lambda i, p: (2 * p[0] + p[1], pick * p[2], i)),
                    pl.BlockSpec((peers,) + blk, lambda i, p: (0, 0, i)),
                    pl.BlockSpec((1,) + blk, lambda i, p: (p[2], 0, i)))
        blk = (rows // ADD_CHUNKS, cols)
        return (pl.BlockSpec((1,) + blk, lambda i, p: (2 * p[0] + p[1], pick * p[2] * ADD_CHUNKS + i, 0)),
                pl.BlockSpec((peers,) + blk, lambda i, p: (0, i, 0)),
                pl.BlockSpec((1,) + blk, lambda i, p: (p[2], i, 0)))

    def body(pos_ref, *refs):
        for idx in range(n):
            own, land, out = refs[idx], refs[n + idx], refs[2 * n + idx]
            total = own[0]
            for f in range(land.shape[0]):
                total = total + land[f].astype(F32)
            out[0] = total

    all_specs = [specs(o.shape, l.shape) for o, l in zip(owns, lands)]
    return pl.pallas_call(
        body, name="scatter_sum",
        grid_spec=pltpu.PrefetchScalarGridSpec(
            num_scalar_prefetch=1, grid=(ADD_CHUNKS,),
            in_specs=[s[0] for s in all_specs] + [s[1] for s in all_specs], out_specs=[s[2] for s in all_specs]),
        out_shape=[jax.ShapeDtypeStruct((2,) + l.shape[1:], F32) for l in lands],
        compiler_params=_params(),
    )(pos, *owns, *lands)


def _scatter_sum_swap(pos, owns, lands):
    n = len(owns)

    def layout(own_shape, land_shape):
        peers, rows, cols = land_shape
        pick = 1 if own_shape[1] == 2 * rows else 0
        if cols % (128 * ADD_CHUNKS) == 0:
            width = cols // ADD_CHUNKS
            blk = (rows, width)
            return (pl.BlockSpec((1,) + blk, lambda i, p: (2 * p[0] + p[1], pick * p[2], i)),
                    pl.BlockSpec((peers,) + blk, lambda i, p: (0, 0, i)), (ADD_CHUNKS,) + blk,
                    lambda ref, which, j: ref.at[which, :, pl.ds(pl.multiple_of(j * width, 128), width)])
        height = rows // ADD_CHUNKS
        blk = (height, cols)
        return (pl.BlockSpec((1,) + blk, lambda i, p: (2 * p[0] + p[1], pick * p[2] * ADD_CHUNKS + i, 0)),
                pl.BlockSpec((peers,) + blk, lambda i, p: (0, i, 0)), (ADD_CHUNKS,) + blk,
                lambda ref, which, j: ref.at[which, pl.ds(pl.multiple_of(j * height, 8), height), :])

    layouts = [layout(o.shape, l.shape) for o, l in zip(owns, lands)]

    def body(pos_ref, *refs):
        outs, stages = refs[2 * n:3 * n], refs[3 * n:4 * n]
        local_sems, send_sems, recv_sems = refs[4 * n:]
        i = pl.program_id(0)
        x, y, c = _coords()

        def copies(idx, j):
            chunk_of = layouts[idx][3]
            return (pltpu.make_async_copy(stages[idx].at[j], chunk_of(outs[idx], c, j), local_sems.at[idx, j]),
                    pltpu.make_async_remote_copy(
                        src_ref=stages[idx].at[j], dst_ref=chunk_of(outs[idx], c, j), send_sem=send_sems.at[idx, j],
                        recv_sem=recv_sems.at[idx, j], device_id=(x, y, 1 - c), device_id_type=MESH))

        for idx in range(n):
            own, land = refs[idx], refs[n + idx]
            total = own[0]
            for f in range(land.shape[0]):
                total = total + land[f].astype(F32)
            stages[idx][i] = total
            for cp in copies(idx, i):
                cp.start()

        @pl.when(i == ADD_CHUNKS - 1)
        def _():
            for idx in range(n):
                for j in range(ADD_CHUNKS):
                    for cp in copies(idx, j):
                        cp.wait()

    sem_shape = (n, ADD_CHUNKS)
    return pl.pallas_call(
        body, name="scatter_sum_swap",
        grid_spec=pltpu.PrefetchScalarGridSpec(
            num_scalar_prefetch=1, grid=(ADD_CHUNKS,),
            in_specs=[s[0] for s in layouts] + [s[1] for s in layouts], out_specs=[ANY] * n,
            scratch_shapes=[pltpu.VMEM(s[2], F32) for s in layouts]
            + [pltpu.SemaphoreType.DMA(sem_shape), pltpu.SemaphoreType.DMA(sem_shape),
               pltpu.SemaphoreType.DMA(sem_shape)]),
        out_shape=[jax.ShapeDtypeStruct((2,) + l.shape[1:], F32) for l in lands],
        compiler_params=_params(),
    )(pos, *owns, *lands)


def _swap_halves(halves):
    n = len(halves)

    def body(*refs):
        ins, outs = refs[:n], refs[n:2 * n]
        send_sems, recv_sems = refs[2 * n:]
        x, y, c = _coords()
        copies = [pltpu.make_async_remote_copy(
            src_ref=ins[idx].at[pl.ds(c, 1)], dst_ref=outs[idx].at[pl.ds(c, 1)], send_sem=send_sems.at[idx],
            recv_sem=recv_sems.at[idx], device_id=(x, y, 1 - c), device_id_type=MESH) for idx in range(n)]
        for cp in copies:
            cp.start()
        for cp in copies:
            cp.wait()

    return pl.pallas_call(
        body, name="swap_halves",
        in_specs=[ANY] * n, out_specs=[ANY] * n, input_output_aliases={idx: idx for idx in range(n)},
        out_shape=[jax.ShapeDtypeStruct(a.shape, F32) for a in halves],
        scratch_shapes=[pltpu.SemaphoreType.DMA((n,)), pltpu.SemaphoreType.DMA((n,))],
    )(*halves)


def _adamw_math(w, g, m, v):
    m = ADAM_B1 * m + (1.0 - ADAM_B1) * g
    v = ADAM_B2 * v + (1.0 - ADAM_B2) * jnp.square(g)
    m_hat = m / (1.0 - ADAM_B1 ** ADAM_STEP)
    v_hat = v / (1.0 - ADAM_B2 ** ADAM_STEP)
    delta = -ADAM_LR * (m_hat / (jnp.sqrt(v_hat) + ADAM_EPS) + ADAM_WD * w)
    return delta, m, v


def _adamw_group(ws, gs, ms, vs, n_steps):
    n = len(ws)

    def body(*refs):
        w_refs, g_refs, m_refs, v_refs = (refs[q * n:(q + 1) * n] for q in range(4))
        d_outs, m_outs, v_outs = (refs[(4 + q) * n:(5 + q) * n] for q in range(3))
        for idx in range(n):
            d_outs[idx][...], m_outs[idx][...], v_outs[idx][...] = _adamw_math(
                w_refs[idx][...], g_refs[idx][...], m_refs[idx][...], v_refs[idx][...])

    specs = [pl.BlockSpec((w.shape[0] // n_steps, w.shape[1]), lambda i: (i, 0)) for w in ws]
    shapes = [jax.ShapeDtypeStruct(w.shape, F32) for w in ws]
    outs = pl.pallas_call(
        body, grid=(n_steps,), name="adamw_projections", in_specs=specs * 4, out_specs=specs * 3,
        out_shape=shapes * 3, compiler_params=_params(),
    )(*ws, *gs, *ms, *vs)
    return outs[:n], outs[n:2 * n], outs[2 * n:]


def _adamw_many(ws, gs, ms, vs):
    n = len(ws)

    def body(*refs):
        w_refs, g_refs, m_refs, v_refs = (refs[q * n:(q + 1) * n] for q in range(4))
        d_outs, m_outs, v_outs = (refs[(4 + q) * n:(5 + q) * n] for q in range(3))
        for idx in range(n):
            d_outs[idx][...], m_outs[idx][...], v_outs[idx][...] = _adamw_math(
                w_refs[idx][...], g_refs[idx][...], m_refs[idx][...], v_refs[idx][...])

    shapes = [jax.ShapeDtypeStruct(w.shape, F32) for w in ws]
    outs = pl.pallas_call(
        body, name="adamw_small", in_specs=[VMEM_SPEC] * (4 * n), out_specs=[VMEM_SPEC] * (3 * n),
        out_shape=shapes * 3, compiler_params=pltpu.CompilerParams(vmem_limit_bytes=VMEM_LIMIT_BYTES),
    )(*ws, *gs, *ms, *vs)
    return outs[:n], outs[n:2 * n], outs[2 * n:]


def _adamw_ada(c_all16, dmod16, w, m, v, block_rows):
    rows, cols = w.shape

    def body(c_ref, dm_ref, w_ref, m_ref, v_ref, g_out, d_out, m_out, v_out):
        cv = c_ref[...]
        g = _tn((cv * jax.nn.sigmoid(cv)).astype(BF16), dm_ref[...].astype(BF16))
        g_out[...] = g
        d_out[...], m_out[...], v_out[...] = _adamw_math(w_ref[...], g, m_ref[...], v_ref[...])

    spec = pl.BlockSpec((block_rows, cols), lambda i: (i, 0))
    shape = jax.ShapeDtypeStruct((rows, cols), F32)
    return pl.pallas_call(
        body, grid=(rows // block_rows,), name="adamw_w_ada",
        in_specs=[pl.BlockSpec((16, block_rows), lambda i: (0, i)), _full(dmod16.shape), spec, spec, spec],
        out_specs=[spec] * 4, out_shape=[shape] * 4, compiler_params=_params(),
    )(c_all16, dmod16, w, m, v)


SMALL_REPLICATED = ["b_ada", "norm1_gain", "conv_dw_b", "conv_ln_g", "conv_ln_b", "gm_ln_g", "gm_ln_b", "gm_ws", "gm_bs",
                    "mix_out_gain", "norm2_gain", "ffn_dw_b", "final_gain"]
SMALL_SHARDED = ["conv_dw_w", "ffn_dw_w"]
PACK_ROWS = 256
WEIGHT_ORDER = ["w_ada", "b_ada", "norm1_gain", "w_in", "conv_dw_w", "conv_dw_b", "conv_ln_g", "conv_ln_b", "gm_ln_g",
                "gm_ln_b", "gm_ws", "gm_bs", "mix_out_gain", "w_out", "norm2_gain", "w_up", "ffn_dw_w", "ffn_dw_b",
                "w_down", "final_gain"]


def _pack(parts, rows):
    total = rows * D_MODEL
    flat, offset = None, 0
    for a in parts:
        piece = jnp.pad(a.reshape(-1), (offset, total - offset - a.size))
        flat = piece if flat is None else flat + piece
        offset += a.size
    return flat.reshape(rows, D_MODEL)


def _unpack(packed, shapes):
    flat = packed.reshape(-1)
    out, pos = [], 0
    for s in shapes:
        size = 1
        for d in s:
            size *= d
        out.append(flat[pos:pos + size].reshape(s))
        pos += size
    return out


def kernel(x, c, w_ada, b_ada, norm1_gain, w_in, conv_dw_w, conv_dw_b, conv_ln_g, conv_ln_b, gm_ln_g, gm_ln_b, gm_ws, gm_bs, mix_out_gain, w_out, norm2_gain, w_up, ffn_dw_w, ffn_dw_b, w_down, final_gain, loss_target, m_w_ada, m_b_ada, m_norm1_gain, m_w_in, m_conv_dw_w, m_conv_dw_b, m_conv_ln_g, m_conv_ln_b, m_gm_ln_g, m_gm_ln_b, m_gm_ws, m_gm_bs, m_mix_out_gain, m_w_out, m_norm2_gain, m_w_up, m_ffn_dw_w, m_ffn_dw_b, m_w_down, m_final_gain, v_w_ada, v_b_ada, v_norm1_gain, v_w_in, v_conv_dw_w, v_conv_dw_b, v_conv_ln_g, v_conv_ln_b, v_gm_ln_g, v_gm_ln_b, v_gm_ws, v_gm_bs, v_mix_out_gain, v_w_out, v_norm2_gain, v_w_up, v_ffn_dw_w, v_ffn_dw_b, v_w_down, v_final_gain):
    weights = dict(w_ada=w_ada, b_ada=b_ada, norm1_gain=norm1_gain, w_in=w_in, conv_dw_w=conv_dw_w, conv_dw_b=conv_dw_b,
                   conv_ln_g=conv_ln_g, conv_ln_b=conv_ln_b, gm_ln_g=gm_ln_g, gm_ln_b=gm_ln_b, gm_ws=gm_ws, gm_bs=gm_bs,
                   mix_out_gain=mix_out_gain, w_out=w_out, norm2_gain=norm2_gain, w_up=w_up, ffn_dw_w=ffn_dw_w,
                   ffn_dw_b=ffn_dw_b, w_down=w_down, final_gain=final_gain)
    mom1 = dict(w_ada=m_w_ada, b_ada=m_b_ada, norm1_gain=m_norm1_gain, w_in=m_w_in, conv_dw_w=m_conv_dw_w,
                conv_dw_b=m_conv_dw_b, conv_ln_g=m_conv_ln_g, conv_ln_b=m_conv_ln_b, gm_ln_g=m_gm_ln_g, gm_ln_b=m_gm_ln_b,
                gm_ws=m_gm_ws, gm_bs=m_gm_bs, mix_out_gain=m_mix_out_gain, w_out=m_w_out, norm2_gain=m_norm2_gain,
                w_up=m_w_up, ffn_dw_w=m_ffn_dw_w, ffn_dw_b=m_ffn_dw_b, w_down=m_w_down, final_gain=m_final_gain)
    mom2 = dict(w_ada=v_w_ada, b_ada=v_b_ada, norm1_gain=v_norm1_gain, w_in=v_w_in, conv_dw_w=v_conv_dw_w,
                conv_dw_b=v_conv_dw_b, conv_ln_g=v_conv_ln_g, conv_ln_b=v_conv_ln_b, gm_ln_g=v_gm_ln_g, gm_ln_b=v_gm_ln_b,
                gm_ws=v_gm_ws, gm_bs=v_gm_bs, mix_out_gain=v_mix_out_gain, w_out=v_w_out, norm2_gain=v_norm2_gain,
                w_up=v_w_up, ffn_dw_w=v_ffn_dw_w, ffn_dw_b=v_ffn_dw_b, w_down=v_w_down, final_gain=v_final_gain)
    shard = 2 * lax.axis_index("x") + lax.axis_index("y")
    me = 2 * shard + lax.axis_index("c")

    ada_cols = w_ada.shape[2]
    b_ada_sh = lax.dynamic_slice(b_ada, (0, shard * ada_cols), (1, ada_cols))
    (w_in_g, w_out_g, w_up_part, w_down_part), (conv_w_g, ffn_w_g), c_all64, mod32 = _gather_weights(
        [w_in[0], w_out[0], w_up[0], w_down[0]], [conv_dw_w[0], ffn_dw_w[0]], 2, c, w_ada[0], b_ada_sh)
    c_all = c_all64[::8]
    mod = mod32[::8].reshape(1, N_SHARD * ada_cols)
    conv_w_full = jnp.transpose(conv_w_g, (1, 0, 2)).reshape(CONV_K, D_HALF)
    ffn_w_full = jnp.transpose(ffn_w_g, (1, 0, 2)).reshape(FFN_K, 2 * D_FF)

    p = dict(norm1_gain=norm1_gain, conv_dw_w=conv_w_full, conv_dw_b=conv_dw_b, conv_ln_g=conv_ln_g,
             conv_ln_b=conv_ln_b, gm_ln_g=gm_ln_g, gm_ln_b=gm_ln_b, gm_ws=gm_ws[0], gm_bs=gm_bs[0],
             mix_out_gain=mix_out_gain, norm2_gain=norm2_gain, ffn_dw_w=ffn_w_full, ffn_dw_b=ffn_dw_b,
             final_gain=final_gain[None])
    grad_x, g, d_mod, loss, in_flight = _local_step(
        x[0], loss_target[0], mod, p, w_in_g, w_out_g.reshape(D_MODEL, D_MODEL), w_up_part, w_down_part)

    n_mod = d_mod.shape[1]
    dmod_rows = lax.dynamic_update_slice(jnp.zeros((N_DEV, n_mod), F32), d_mod, (me, 0))
    g["b_ada"] = d_mod
    small = _pack([g[k] for k in SMALL_REPLICATED] + [g[k] for k in SMALL_SHARDED] + [dmod_rows, loss[0, :1]], PACK_ROWS)
    (land_w_in, land_w_out), small = _final_comm([in_flight["w_in16"], in_flight["w_out16"]], small)
    pos = jnp.stack(_coords()).astype(jnp.int32)
    full = _scatter_sum_swap(pos, [g["w_in"], g["w_out"], g["w_up"], g["w_down"]],
                             [land_w_in, land_w_out, in_flight["land_w_up"], in_flight["land_w_down"]])
    grads = dict(w_in=full[0].reshape(w_in.shape[1:]), w_out=full[1].reshape(w_out.shape[1:]),
                 w_up=full[2].reshape(w_up.shape[1:]), w_down=full[3].reshape(w_down.shape[1:]))

    small_shapes = ([weights[k].shape for k in SMALL_REPLICATED] + [(CONV_K, D_HALF), (FFN_K, 2 * D_FF)]
                    + [(N_DEV, n_mod), (1,)])
    *small_grads, conv_w_grad, ffn_w_grad, dmod_all, loss_sum = _unpack(small, small_shapes)
    grads.update(zip(SMALL_REPLICATED, small_grads))
    grads["conv_dw_w"] = lax.dynamic_slice(conv_w_grad, (0, shard * conv_dw_w.shape[2]), conv_dw_w.shape[1:])[None]
    grads["ffn_dw_w"] = lax.dynamic_slice(ffn_w_grad, (0, shard * ffn_dw_w.shape[2]), ffn_dw_w.shape[1:])[None]

    delta, new_m, new_v = {}, {}, {}
    projections = ["w_in", "w_out", "w_up", "w_down"]
    group_out = _adamw_group([weights[k][0] for k in projections], [grads[k] for k in projections],
                             [mom1[k][0] for k in projections], [mom2[k][0] for k in projections], n_steps=4)
    for d, arrs in zip((delta, new_m, new_v), group_out):
        d.update({k: a[None] for k, a in zip(projections, arrs)})
    for k in projections:
        grads[k] = grads[k][None]
    dmod_sh = lax.dynamic_slice(dmod_all, (0, shard * ada_cols), (N_DEV, ada_cols))
    pad8 = ((0, 16 - N_DEV), (0, 0))
    grads["w_ada"], delta["w_ada"], new_m["w_ada"], new_v["w_ada"] = [a[None] for a in _adamw_ada(
        jnp.pad(c_all, pad8), jnp.pad(dmod_sh, pad8), w_ada[0], m_w_ada[0], v_w_ada[0], 256)]
    small_names = SMALL_REPLICATED + SMALL_SHARDED

    def two_d(a):
        return a.reshape(1, -1) if a.ndim == 1 else a

    small_out = _adamw_many(*[[two_d(d[k]) for k in small_names] for d in (weights, grads, mom1, mom2)])
    for d, arrs in zip((delta, new_m, new_v), small_out):
        d.update({k: a.reshape(weights[k].shape) for k, a in zip(small_names, arrs)})

    return (loss_sum.reshape(()), grad_x[None], *[grads[k] for k in WEIGHT_ORDER], *[delta[k] for k in WEIGHT_ORDER],
            *[new_m[k] for k in WEIGHT_ORDER], *[new_v[k] for k in WEIGHT_ORDER])
```

```python
import jax
import jax.numpy as jnp
from jax import lax
from jax.experimental import pallas as pl
from jax.experimental.pallas import tpu as pltpu

F32 = jnp.float32
BF16 = jnp.bfloat16

D_MODEL = 1024
D_HALF = 512
D_FF = 2816
CONV_K = 31
FFN_K = 3
CHUNK = 128
N_HEADS = 8
HEAD_DIM = 64
N_SHARD = 4
N_DEV = 8
RMS_EPS = 1e-6
LN_EPS = 1e-5
ADAM_LR, ADAM_B1, ADAM_B2, ADAM_EPS, ADAM_WD, ADAM_STEP = 0.001, 0.9, 0.999, 1e-08, 0.01, 10

TILE = 256
HALO = 32
FFN_HALO = 16
FFN_BLK = 256
UP_SHARD = 2 * D_FF // N_SHARD
VMEM_LIMIT_BYTES = 56 * 1024 * 1024
FFN_VMEM_LIMIT_BYTES = 58 * 1024 * 1024

ANY = pl.BlockSpec(memory_space=pl.ANY)
NT_DIMS = (((1,), (1,)), ((), ()))
TN_DIMS = (((0,), (0,)), ((), ()))


def _full(shape):
    return pl.BlockSpec(shape, lambda i: (0,) * len(shape))


def _nn(a, b):
    return jnp.dot(a, b, preferred_element_type=F32)


def _nt(a, b):
    return lax.dot_general(a, b, NT_DIMS, preferred_element_type=F32)


def _tn(a, b):
    return lax.dot_general(a, b, TN_DIMS, preferred_element_type=F32)


def _colsum(a):
    return jnp.sum(a, axis=0, keepdims=True)


def _params(semantics=("arbitrary",)):
    return pltpu.CompilerParams(dimension_semantics=semantics, vmem_limit_bytes=VMEM_LIMIT_BYTES)


def _rms(v, gain):
    return v * lax.rsqrt(jnp.mean(v * v, axis=-1, keepdims=True) + RMS_EPS) * gain


def _layer_norm(v, gain, bias):
    mu = jnp.mean(v, axis=-1, keepdims=True)
    var = jnp.mean(jnp.square(v - mu), axis=-1, keepdims=True)
    return (v - mu) * lax.rsqrt(var + LN_EPS) * gain + bias


def _mod_norm(v, gain, scale, shift):
    return _rms(v, gain) * (1.0 + scale) + shift


def _conv_branch(a1, ln_g, ln_b, out_gain):
    a2 = _layer_norm(a1, ln_g, ln_b)
    return _rms(a2 * jax.nn.sigmoid(a2), out_gain)


def _gate_branch(gu, sp, out_gain):
    return _rms(jax.nn.gelu(gu) * sp, out_gain)


def _gv_norm(gv, ln_g, ln_b):
    return _layer_norm(jax.nn.gelu(gv), ln_g, ln_b)


def _rms_parts(v):
    r = lax.rsqrt(jnp.mean(v * v, axis=-1, keepdims=True) + RMS_EPS)
    return v * r, r


def _rms_back(dn, n, r):
    return r * (dn - n * jnp.mean(dn * n, axis=-1, keepdims=True))


def _ln_parts(v):
    mu = jnp.mean(v, axis=-1, keepdims=True)
    rs = lax.rsqrt(jnp.mean(jnp.square(v - mu), axis=-1, keepdims=True) + LN_EPS)
    return (v - mu) * rs, rs


def _ln_back(dn, n, rs):
    return rs * (dn - jnp.mean(dn, axis=-1, keepdims=True) - n * jnp.mean(dn * n, axis=-1, keepdims=True))


GELU_C = 0.7978845608028654
GELU_A = 0.044715


def _gelu_parts(v):
    v2 = v * v
    th = jnp.tanh(GELU_C * (v + GELU_A * (v2 * v)))
    cdf = 0.5 * (1.0 + th)
    return v * cdf, cdf + (0.5 * GELU_C) * v * (1.0 - th * th) * (1.0 + (3.0 * GELU_A) * v2)


def _rms_vjp(v, gain):
    n, r = _rms_parts(v)
    return n * gain, lambda dy: (_rms_back(dy * gain, n, r), _colsum(dy * n))


def _mod_norm_vjp(v, gain, scale, shift):
    n, r = _rms_parts(v)

    def back(dy):
        q = _colsum(dy * n)
        return _rms_back(dy * (gain * (1.0 + scale)), n, r), q * (1.0 + scale), q * gain, _colsum(dy)

    return n * gain * (1.0 + scale) + shift, back


def _conv_branch_vjp(a1, ln_g, ln_b, out_gain):
    n1, rs1 = _ln_parts(a1)
    a2 = n1 * ln_g + ln_b
    s = jax.nn.sigmoid(a2)
    a3 = a2 * s
    n3, r3 = _rms_parts(a3)

    def back(dy):
        da2 = _rms_back(dy * out_gain, n3, r3) * (s + a3 * (1.0 - s))
        return _ln_back(da2 * ln_g, n1, rs1), _colsum(da2 * n1), _colsum(da2), _colsum(dy * n3)

    return n3 * out_gain, back


def _gate_branch_vjp(gu, sp, out_gain):
    ge, dge = _gelu_parts(gu)
    n, r = _rms_parts(ge * sp)

    def back(dy):
        dg = _rms_back(dy * out_gain, n, r)
        return dg * sp * dge, dg * ge, _colsum(dy * n)

    return n * out_gain, back


def _gv_norm_vjp(gv, ln_g, ln_b):
    ge, dge = _gelu_parts(gv)
    n, rs = _ln_parts(ge)
    return n * ln_g + ln_b, lambda dy: (_ln_back(dy * ln_g, n, rs) * dge, _colsum(dy * n), _colsum(dy))


def _head_pair_matmul(wp_ref, v):
    lane = lax.broadcasted_iota(jnp.int32, (CHUNK, CHUNK), 1)
    rows = []
    for n in range(v.shape[0] // CHUNK):
        cols = []
        for j in range(N_HEADS // 2):
            r = _nn(wp_ref[j], v[n * CHUNK:(n + 1) * CHUNK, j * CHUNK:(j + 1) * CHUNK])
            cols.append(jnp.where(lane < HEAD_DIM, r[:CHUNK], r[CHUNK:]))
        rows.append(jnp.concatenate(cols, axis=1))
    return jnp.concatenate(rows, axis=0)


def _tile_bias(bs, tokens):
    return jnp.concatenate([bs] * (tokens // CHUNK), axis=0)


FORWARD_LEAD = 8


def _fwd_mixer(x, vec, conv_w, wpair, bs_full, w_in_g, w_out_g, late_parts):
    seq = x.shape[0]
    n_tiles = seq // TILE
    t = TILE
    n_late = len(late_parts)
    forward_step = max(n_tiles - FORWARD_LEAD, 0)
    names = ["norm1_gain", "sc1", "sh1", "gt1", "conv_dw_b", "conv_ln_g", "conv_ln_b", "gm_ln_g", "gm_ln_b",
             "mix_out_gain"]
    vecs = [vec[k] for k in names]

    def body(x_ref, g1, sc1, sh1, gt1, cb, clg, clb, vg, vb, mg, cw, wp, bs, win_hbm, wout_hbm, *rest):
        late = rest[n_late:2 * n_late]
        z_ref, a1_ref, sp_ref, y_ref, o1_ref, x2_ref = rest[2 * n_late:2 * n_late + 6]
        win_v, wout_v, halo, bank, sem, send_sems, recv_sems = rest[2 * n_late + 6:]
        i = pl.program_id(0)
        mx, my, mc = _coords()
        shard = 2 * mx + my

        def half(w, which):
            h = late[w].shape[1] // 2
            return pl.ds(pl.multiple_of(which * h, 16), h)

        def chip_of(j):
            return 2 * _flip(mx, CHIP_FLIPS[j][0]) + _flip(my, CHIP_FLIPS[j][1])

        def ici_copy(w, j, slot):
            rows = late[w].at[slot, half(w, mc)]
            return pltpu.make_async_remote_copy(
                src_ref=rows, dst_ref=rows, send_sem=send_sems.at[w, j], recv_sem=recv_sems.at[w, j],
                device_id=(_flip(mx, CHIP_FLIPS[j][0]), _flip(my, CHIP_FLIPS[j][1]), mc), device_id_type=MESH)

        def d2d_copy(w, j, which):
            rows = late[w].at[chip_of(j), half(w, which)]
            return pltpu.make_async_remote_copy(
                src_ref=rows, dst_ref=rows, send_sem=send_sems.at[w, len(CHIP_FLIPS) + j],
                recv_sem=recv_sems.at[w, len(CHIP_FLIPS) + j], device_id=(mx, my, 1 - mc), device_id_type=MESH)

        pairs = [(w, j) for w in range(n_late) for j in range(len(CHIP_FLIPS))]

        @pl.when(i == 0)
        def _():
            for w, j in pairs:
                ici_copy(w, j, shard).start()
            cps = [pltpu.make_async_copy(win_hbm, win_v, sem.at[0]),
                   pltpu.make_async_copy(wout_hbm, wout_v, sem.at[1])]
            for cp in cps:
                cp.start()
            for cp in cps:
                cp.wait()
            halo[...] = jnp.zeros_like(halo)

        @pl.when(i == forward_step)
        def _():
            for w, j in pairs:
                ici_copy(w, j, chip_of(j)).wait_recv()
                d2d_copy(w, j, mc).start()

        xv = x_ref[...]
        h1b = _mod_norm(xv, g1[...], sc1[...], sh1[...]).astype(BF16)
        zs = [_nn(h1b, win_v[k]) for k in range(N_SHARD)]
        for k in range(N_SHARD):
            z_ref[:, k * D_HALF:(k + 1) * D_HALF] = zs[k]
        ca, cg, gu, gv = zs
        a0 = ca * jax.nn.sigmoid(cg)
        ext = jnp.concatenate([halo[...], a0], axis=0)
        halo[...] = a0[t - HALO:]
        bank[0] = ext
        for b in range(1, 8):
            bank[b] = pltpu.roll(ext, b, axis=0)
        a1 = jnp.zeros((t, D_HALF), F32) + cb[...]
        for s in range(CONV_K):
            q, b = divmod(s, 8)
            a1 = a1 + bank[b, pl.ds(HALO - 8 * q, t), :] * cw[pl.ds(CONV_K - 1 - s, 1), :]
        a1_ref[...] = a1
        mgv = mg[...]
        ya = _conv_branch(a1, clg[...], clb[...], mgv[:, :D_HALF])
        gvn = _gv_norm(gv, vg[...], vb[...]).astype(BF16)
        sp = _head_pair_matmul(wp, gvn) + _tile_bias(bs[...], t)
        sp_ref[...] = sp
        yg = _gate_branch(gu, sp, mgv[:, D_HALF:])
        yb = jnp.concatenate([ya, yg], axis=1).astype(BF16)
        y_ref[...] = yb
        o1 = _nn(yb, wout_v[...])
        o1_ref[...] = o1
        x2_ref[...] = xv + gt1[...] * o1

        @pl.when(i == n_tiles - 1)
        def _():
            for w, j in pairs:
                d2d_copy(w, j, 1 - mc).wait_recv()
            for w, j in pairs:
                ici_copy(w, j, shard).wait_send()
                d2d_copy(w, j, mc).wait_send()

    def row(width):
        return pl.BlockSpec((t, width), lambda i: (i, 0))

    out_shape = [jax.ShapeDtypeStruct((seq, 4 * D_HALF), F32), jax.ShapeDtypeStruct((seq, D_HALF), F32),
                 jax.ShapeDtypeStruct((seq, D_HALF), F32), jax.ShapeDtypeStruct((seq, D_MODEL), BF16),
                 jax.ShapeDtypeStruct((seq, D_MODEL), F32), jax.ShapeDtypeStruct((seq, D_MODEL), F32)]
    n_in = 1 + len(vecs) + 3 + 2
    sem_shape = (n_late, 2 * len(CHIP_FLIPS))
    outs = pl.pallas_call(
        body, grid=(n_tiles,), name="fwd_mixer",
        in_specs=[row(D_MODEL)] + [_full(v.shape) for v in vecs]
        + [_full(conv_w.shape), _full(wpair.shape), _full(bs_full.shape), ANY, ANY] + [ANY] * n_late,
        out_specs=[ANY] * n_late + [row(4 * D_HALF), row(D_HALF), row(D_HALF), row(D_MODEL), row(D_MODEL),
                                    row(D_MODEL)],
        out_shape=[jax.ShapeDtypeStruct(a.shape, a.dtype) for a in late_parts] + out_shape,
        input_output_aliases={n_in + w: w for w in range(n_late)},
        scratch_shapes=[pltpu.VMEM(w_in_g.shape, BF16), pltpu.VMEM(w_out_g.shape, BF16),
                        pltpu.VMEM((HALO, D_HALF), F32), pltpu.VMEM((8, t + HALO, D_HALF), F32),
                        pltpu.SemaphoreType.DMA((2,)), pltpu.SemaphoreType.DMA(sem_shape),
                        pltpu.SemaphoreType.DMA(sem_shape)],
        compiler_params=_params(),
    )(x, *vecs, conv_w, wpair, bs_full, w_in_g, w_out_g, *late_parts)
    return outs[n_late:], outs[:n_late]


def _interleave_matrices():
    row = jnp.arange(TILE)
    token_of_row = (row % 8) * (TILE // 8) + row // 8
    to_inter = (token_of_row[:, None] == row[None, :]).astype(BF16)
    return to_inter, jnp.transpose(to_inter)


def _ffn(x2, target, norm2_gain, sc2, sh2, ffn_w, ffn_b, gt2, final_gain, w_up_g, w_down_g, to_inter, to_natural):
    seq = x2.shape[0]
    n_tiles = seq // TILE
    t = TILE
    n_blk = D_FF // FFN_BLK
    inv_d = 1.0 / D_MODEL

    def body(x2_ref, x2h_ref, tgt_ref, g2, sc2_ref, sh2_ref, fw, fb, gt2_ref, fg, pm_ref, pmt_ref, wup_hbm, wd_hbm,
             du_ref, dx2_ref, dfw_ref, dfb_ref, dfg_ref, dgt2_ref, dg2_ref, dsc2_ref, dsh2_ref, loss_ref, dwd_hbm,
             dwd16_hbm, wup_v, wd_v, dwd_acc, carry, u_s, sil_s, vds_s, f_s, du_s, sem):
        i = pl.program_id(0)
        tile = n_tiles - 1 - i
        sublane = lax.broadcasted_iota(jnp.int32, (8, FFN_BLK), 0)

        @pl.when(i == 0)
        def _():
            cps = [pltpu.make_async_copy(wd_hbm, wd_v, sem.at[0])]
            cps += [pltpu.make_async_copy(wup_hbm.at[k], wup_v.at[:, pl.ds(k * UP_SHARD, UP_SHARD)], sem.at[3 + k])
                    for k in range(N_SHARD)]
            for cp in cps:
                cp.start()
            for cp in cps:
                cp.wait()
            dwd_acc[...] = jnp.zeros_like(dwd_acc)
            carry[...] = jnp.zeros_like(carry)
            dfw_ref[...] = jnp.zeros_like(dfw_ref)
            dfb_ref[...] = jnp.zeros_like(dfb_ref)
            dfg_ref[...] = jnp.zeros_like(dfg_ref)
            dgt2_ref[...] = jnp.zeros_like(dgt2_ref)
            dg2_ref[...] = jnp.zeros_like(dg2_ref)
            dsc2_ref[...] = jnp.zeros_like(dsc2_ref)
            dsh2_ref[...] = jnp.zeros_like(dsh2_ref)
            loss_ref[...] = jnp.zeros_like(loss_ref)

        def cols_of(j):
            return pl.ds(j * FFN_BLK, FFN_BLK), pl.ds(D_FF + j * FFN_BLK, FFN_BLK)

        def wrap_down(last, before):
            return jnp.where(sublane == 0, pltpu.roll(before, 1, axis=0), pltpu.roll(last, 1, axis=0))

        def wrap_up(first, after):
            return jnp.where(sublane == 7, pltpu.roll(after, 7, axis=0), pltpu.roll(first, 7, axis=0))

        x2v = x2_ref[...]
        h2, h2_vjp = _mod_norm_vjp(x2v, g2[...], sc2_ref[...], sh2_ref[...])
        h2b = h2.astype(BF16)
        h2_before = _mod_norm(x2h_ref[...], g2[...], sc2_ref[...], sh2_ref[...]).astype(BF16)
        lhs = jnp.concatenate([_nn(pm_ref[...], h2b).astype(BF16), h2_before], axis=0)

        def up(j):
            cv, cg = cols_of(j)
            return _nn(lhs, wup_v[:, cv]), _nn(lhs, wup_v[:, cg])

        def conv(both, cols):
            cur = both[:t]
            u_s[:, cols] = cur.astype(BF16)
            before = jnp.where(tile > 0, both[t:], 0.0)
            w1 = wrap_down(cur[t - 8:], before)
            w2 = wrap_down(cur[t - 16:t - 8], pltpu.roll(before, 1, axis=0))
            back1 = jnp.concatenate([w1, cur[:t - 8]], axis=0)
            back2 = jnp.concatenate([w2, w1, cur[:t - 16]], axis=0)
            return (fb[:, cols] + cur * fw[pl.ds(2, 1), cols] + back1 * fw[pl.ds(1, 1), cols]
                    + back2 * fw[pl.ds(0, 1), cols])

        pm_t = pmt_ref[...]

        def to_natural_f32(a):
            hi = a.astype(BF16)
            rest = a - hi.astype(F32)
            mid = rest.astype(BF16)
            low = (rest - mid.astype(F32)).astype(BF16)
            return _nn(jnp.concatenate([pm_t, pm_t, pm_t], axis=1), jnp.concatenate([hi, mid, low], axis=0))

        o2 = jnp.zeros((t, D_MODEL), F32)
        ahead_uv = up(0)
        for j in range(n_blk):
            cv, cg = cols_of(j)
            both_v, both_g = ahead_uv
            if j + 1 < n_blk:
                ahead_uv = up(j + 1)
            val, gate = conv(both_v, cv), conv(both_g, cg)
            sig = jax.nn.sigmoid(gate)
            sil = gate * sig
            fb16 = (sil * val).astype(BF16)
            sil_s[:, cv] = sil
            vds_s[:, cv] = val * (sig + sil * (1.0 - sig))
            f_s[:, cv] = fb16
            o2 = o2 + _nn(fb16, wd_v[pl.ds(j * FFN_BLK, FFN_BLK), :])
        o2 = to_natural_f32(o2)

        gt2v = gt2_ref[...]
        x3 = x2v + gt2v * o2
        out, out_vjp = _rms_vjp(x3, fg[...])
        diff = out - tgt_ref[...]
        loss_ref[...] += jnp.zeros_like(loss_ref) + 0.5 * inv_d * jnp.sum(diff * diff)
        dx3, dfg = out_vjp(diff * inv_d)
        dfg_ref[...] += dfg
        dgt2_ref[...] += _colsum(dx3 * o2)
        do2b = _nn(pm_ref[...], (gt2v * dx3).astype(BF16)).astype(BF16)

        def conv_back(dd, cols):
            dfb_ref[:, cols] += _colsum(dd)
            nxt = carry[:, cols]
            w1 = wrap_up(dd[:8], nxt[:8])
            w2 = wrap_up(dd[8:16], nxt[8:])
            ahead = (dd, jnp.concatenate([dd[8:], w1], axis=0), jnp.concatenate([dd[16:], w1, w2], axis=0))
            carry[:, cols] = dd[:16]
            uv = u_s[:, cols].astype(F32)
            du = jnp.zeros((t, FFN_BLK), F32)
            for s in range(FFN_K):
                du = du + ahead[s] * fw[pl.ds(FFN_K - 1 - s, 1), cols]
                dfw_ref[pl.ds(FFN_K - 1 - s, 1), cols] += _colsum(ahead[s] * uv)
            du_s[:, cols] = du.astype(BF16)

        for j in range(n_blk):
            cv, cg = cols_of(j)
            rows = pl.ds(j * FFN_BLK, FFN_BLK)
            df = _nt(do2b, wd_v[rows, :])
            dwd_acc[rows, :] += _tn(f_s[:, cv], do2b)
            conv_back(df * sil_s[:, cv], cv)
            conv_back(df * vds_s[:, cv], cg)
        du16 = _nn(pm_t, du_s[...]).astype(BF16)
        du_ref[...] = du16
        dx2, dg2, dsc2, dsh2 = h2_vjp(_nt(du16, wup_v[...]))
        dx2_ref[...] = dx3 + dx2
        dg2_ref[...] += dg2
        dsc2_ref[...] += dsc2
        dsh2_ref[...] += dsh2

        @pl.when(i == n_tiles - 1)
        def _():
            cp = pltpu.make_async_copy(dwd_acc, dwd_hbm, sem.at[1])
            cp.start()
            wd_v[...] = dwd_acc[...].astype(BF16)
            cp16 = pltpu.make_async_copy(wd_v, dwd16_hbm, sem.at[2])
            cp16.start()
            cp.wait()
            cp16.wait()

    def rev(width):
        return pl.BlockSpec((t, width), lambda i: (n_tiles - 1 - i, 0))

    assert FFN_K == 3
    halo_spec = pl.BlockSpec((8, D_MODEL), lambda i: (jnp.maximum((n_tiles - 1 - i) * (t // 8) - 1, 0), 0))
    vec_spec = _full((1, D_MODEL))
    out_shape = [jax.ShapeDtypeStruct((seq, 2 * D_FF), BF16), jax.ShapeDtypeStruct((seq, D_MODEL), F32),
                 jax.ShapeDtypeStruct((FFN_K, 2 * D_FF), F32), jax.ShapeDtypeStruct((1, 2 * D_FF), F32),
                 jax.ShapeDtypeStruct((1, D_MODEL), F32), jax.ShapeDtypeStruct((1, D_MODEL), F32),
                 jax.ShapeDtypeStruct((1, D_MODEL), F32), jax.ShapeDtypeStruct((1, D_MODEL), F32),
                 jax.ShapeDtypeStruct((1, D_MODEL), F32),
                 jax.ShapeDtypeStruct((1, 128), F32), jax.ShapeDtypeStruct((D_FF, D_MODEL), F32),
                 jax.ShapeDtypeStruct((D_FF, D_MODEL), BF16)]
    return pl.pallas_call(
        body, grid=(n_tiles,), name="ffn",
        in_specs=[rev(D_MODEL), halo_spec, rev(D_MODEL), vec_spec, vec_spec, vec_spec, _full(ffn_w.shape),
                  _full(ffn_b.shape), _full(gt2.shape), _full(final_gain.shape), _full(to_inter.shape),
                  _full(to_natural.shape), ANY, ANY],
        out_specs=[rev(2 * D_FF), rev(D_MODEL), _full((FFN_K, 2 * D_FF)), _full((1, 2 * D_FF)), vec_spec, vec_spec,
                   vec_spec, vec_spec, vec_spec, _full((1, 128)), ANY, ANY],
        out_shape=out_shape,
        scratch_shapes=[pltpu.VMEM((D_MODEL, 2 * D_FF), BF16), pltpu.VMEM((D_FF, D_MODEL), BF16),
                        pltpu.VMEM((D_FF, D_MODEL), F32), pltpu.VMEM((FFN_HALO, 2 * D_FF), F32),
                        pltpu.VMEM((t, 2 * D_FF), BF16), pltpu.VMEM((t, D_FF), F32), pltpu.VMEM((t, D_FF), F32),
                        pltpu.VMEM((t, D_FF), BF16), pltpu.VMEM((t, 2 * D_FF), BF16),
                        pltpu.SemaphoreType.DMA((3 + N_SHARD,))],
        compiler_params=pltpu.CompilerParams(dimension_semantics=("arbitrary",), vmem_limit_bytes=FFN_VMEM_LIMIT_BYTES),
    )(x2, x2, target, norm2_gain, sc2, sh2, ffn_w, ffn_b, gt2, final_gain, to_inter, to_natural, w_up_g, w_down_g)


def _scatter_copies(src16, land, send_sems, recv_sems):
    x, y, c = _coords()
    h = src16.shape[1] // 2
    copies = []
    for f, flip in enumerate(PEER_FLIPS):
        tx, ty, tc = _flip(x, flip[0]), _flip(y, flip[1]), _flip(c, flip[2])
        copies.append(pltpu.make_async_remote_copy(
            src_ref=src16.at[2 * tx + ty, pl.ds(pl.multiple_of(tc * h, 16), h)], dst_ref=land.at[f],
            send_sem=send_sems.at[f], recv_sem=recv_sems.at[f], device_id=(tx, ty, tc), device_id_type=MESH))
    return copies


def _land_shape(src16):
    return jax.ShapeDtypeStruct((len(PEER_FLIPS), src16.shape[1] // 2, src16.shape[2]), BF16)


UP_TILE = 1024


def _bwd_up(du, x2, norm2_gain, sc2, sh2, dwd16):
    seq = x2.shape[0]
    t = UP_TILE if seq % UP_TILE == 0 else TILE
    n_tiles = seq // t
    out_shape = (N_SHARD, D_MODEL, UP_SHARD)
    slot_shape = (2, D_MODEL, UP_SHARD)

    def body(du_ref, x2_ref, g2, sc2_ref, sh2_ref, dwd16_hbm, dwup_hbm, dwup16_hbm, land_hbm,
             stage16, acc, sem, send_sems, recv_sems):
        k = pl.program_id(0)
        i = pl.program_id(1)
        slot = lax.rem(k, 2)

        def write_back(s, shard):
            return (pltpu.make_async_copy(acc.at[s], dwup_hbm.at[shard], sem.at[s, 0]),
                    pltpu.make_async_copy(stage16.at[s], dwup16_hbm.at[shard], sem.at[s, 1]))

        @pl.when((k == 0) & (i == 0))
        def _():
            for cp in _scatter_copies(dwd16_hbm, land_hbm, send_sems, recv_sems):
                cp.start()

        @pl.when(i == 0)
        def _():
            @pl.when(k >= 2)
            def _():
                for cp in write_back(slot, k - 2):
                    cp.wait()
            acc[slot] = jnp.zeros((D_MODEL, UP_SHARD), F32)

        h2b = _mod_norm(x2_ref[...], g2[...], sc2_ref[...], sh2_ref[...]).astype(BF16)
        acc[slot] += _tn(h2b, du_ref[...])

        @pl.when(i == n_tiles - 1)
        def _():
            stage16[slot] = acc[slot].astype(BF16)
            for cp in write_back(slot, k):
                cp.start()

        @pl.when((k == N_SHARD - 1) & (i == n_tiles - 1))
        def _():
            for shard in (N_SHARD - 2, N_SHARD - 1):
                for cp in write_back(shard % 2, shard):
                    cp.wait()
            for rc in _scatter_copies(dwd16_hbm, land_hbm, send_sems, recv_sems):
                rc.wait()

    def vec_spec():
        return pl.BlockSpec((1, D_MODEL), lambda k, i: (0, 0))

    n_peer = len(PEER_FLIPS)
    return pl.pallas_call(
        body, grid=(N_SHARD, n_tiles), name="bwd_up",
        in_specs=[pl.BlockSpec((t, UP_SHARD), lambda k, i: (i, k)), pl.BlockSpec((t, D_MODEL), lambda k, i: (i, 0)),
                  vec_spec(), vec_spec(), vec_spec(), ANY],
        out_specs=[ANY, ANY, ANY],
        out_shape=[jax.ShapeDtypeStruct(out_shape, F32), jax.ShapeDtypeStruct(out_shape, BF16), _land_shape(dwd16)],
        scratch_shapes=[pltpu.VMEM(slot_shape, BF16), pltpu.VMEM(slot_shape, F32), pltpu.SemaphoreType.DMA((2, 2)),
                        pltpu.SemaphoreType.DMA((n_peer,)), pltpu.SemaphoreType.DMA((n_peer,))],
        compiler_params=_params(("arbitrary", "arbitrary")),
    )(du, x2, norm2_gain, sc2, sh2, dwd16)


def _bwd_mixer(dx2, x, z, a1, sp, yb, o1, vec, conv_w, wpair, wpair_t, causal_mask, w_in_g, w_out_g, dwup16):
    seq = x.shape[0]
    n_tiles = seq // TILE
    t = TILE
    names = ["norm1_gain", "sc1", "sh1", "gt1", "conv_ln_g", "conv_ln_b", "gm_ln_g", "gm_ln_b", "mix_out_gain"]
    vecs = [vec[k] for k in names]

    def body(dx2_ref, x_ref, z_ref, a1_ref, sp_ref, y_ref, o1_ref, g1, sc1, sh1, gt1, clg, clb, vg, vb, mg,
             cw, wp, wpt, mask_ref, win_hbm, wout_hbm, dwup16_hbm,
             gx_ref, dg1_ref, dsc1_ref, dsh1_ref, dgt1_ref, dcw_ref, dcb_ref, dclg_ref, dclb_ref, dvg_ref, dvb_ref,
             dmg_ref, dws_ref, dbs_ref, dwin_hbm, dwout_hbm, land_hbm, dwin16_hbm, dwout16_hbm,
             win_v, wout_v, dwin_acc, dwout_acc, carry, bank, dbs_acc, lwin, lwout, sem, send_sems, recv_sems,
             pair_send, pair_recv):
        i = pl.program_id(0)
        small = [dg1_ref, dsc1_ref, dsh1_ref, dgt1_ref, dcw_ref, dcb_ref, dclg_ref, dclb_ref, dvg_ref, dvb_ref,
                 dmg_ref, dws_ref, dbs_acc]

        @pl.when(i == 0)
        def _():
            for cp in _scatter_copies(dwup16_hbm, land_hbm, send_sems, recv_sems):
                cp.start()
            cps = [pltpu.make_async_copy(win_hbm, win_v, sem.at[0]),
                   pltpu.make_async_copy(wout_hbm, wout_v, sem.at[1])]
            for cp in cps:
                cp.start()
            for cp in cps:
                cp.wait()
            dwin_acc[...] = jnp.zeros_like(dwin_acc)
            dwout_acc[...] = jnp.zeros_like(dwout_acc)
            carry[...] = jnp.zeros_like(carry)
            for ref in small:
                ref[...] = jnp.zeros_like(ref)

        dx2v = dx2_ref[...]
        gt1v = gt1[...]
        dgt1_ref[...] += _colsum(dx2v * o1_ref[...])
        do1b = (gt1v * dx2v).astype(BF16)
        dy = _nt(do1b, wout_v[...])
        dwout_acc[...] += _tn(y_ref[...], do1b)

        mgv = mg[...]
        _, conv_vjp = _conv_branch_vjp(a1_ref[...], clg[...], clb[...], mgv[:, :D_HALF])
        da1, dclg, dclb, dmg_a = conv_vjp(dy[:, :D_HALF])
        dclg_ref[...] += dclg
        dclb_ref[...] += dclb
        gu = z_ref[:, 2 * D_HALF:3 * D_HALF]
        gv = z_ref[:, 3 * D_HALF:]
        spv = sp_ref[...]
        _, gate_vjp = _gate_branch_vjp(gu, spv, mgv[:, D_HALF:])
        dgu, dsp, dmg_g = gate_vjp(dy[:, D_HALF:])
        dmg_ref[...] += jnp.concatenate([dmg_a, dmg_g], axis=1)
        gvn, gv_vjp = _gv_norm_vjp(gv, vg[...], vb[...])
        gvnb = gvn.astype(BF16)
        dspb = dsp.astype(BF16)
        dgvn = _head_pair_matmul(wpt, dspb)
        dgv, dvg, dvb = gv_vjp(dgvn)
        dvg_ref[...] += dvg
        dvb_ref[...] += dvb
        lane = lax.broadcasted_iota(jnp.int32, (CHUNK, CHUNK), 1)
        dbs = jnp.zeros((CHUNK, D_HALF), F32)
        for n in range(t // CHUNK):
            rows = slice(n * CHUNK, (n + 1) * CHUNK)
            dbs = dbs + dsp[rows, :]
            for j in range(N_HEADS // 2):
                cols = slice(j * CHUNK, (j + 1) * CHUNK)
                blk = dspb[rows, cols]
                zero = jnp.zeros_like(blk)
                vblk = gvnb[rows, cols]
                dws_ref[2 * j] += _nt(jnp.where(lane < HEAD_DIM, blk, zero), vblk)
                dws_ref[2 * j + 1] += _nt(jnp.where(lane < HEAD_DIM, zero, blk), vblk)
        dbs_acc[...] += dbs

        h1, h1_vjp = _mod_norm_vjp(x_ref[...], g1[...], sc1[...], sh1[...])
        h1b = h1.astype(BF16)
        dh1 = jnp.zeros((t, D_MODEL), F32)
        for k, dzk in ((2, dgu), (3, dgv)):
            dzb = dzk.astype(BF16)
            dh1 = dh1 + _nt(dzb, win_v[k])
            dwin_acc[k] += _tn(h1b, dzb)

        ca = z_ref[:, :D_HALF]
        cg = z_ref[:, D_HALF:2 * D_HALF]
        sig = jax.nn.sigmoid(cg)
        a0 = ca * sig
        ext = jnp.concatenate([da1, carry[...]], axis=0)
        carry[...] = da1[:HALO]
        bank[0] = ext
        for b in range(1, 8):
            bank[b] = pltpu.roll(ext, t + HALO - b, axis=0)
        dcb_ref[...] += _colsum(da1)
        da0 = jnp.zeros((t, D_HALF), F32)
        for s in range(CONV_K):
            q, b = divmod(s, 8)
            shifted = bank[b, pl.ds(8 * q, t), :]
            da0 = da0 + shifted * cw[pl.ds(CONV_K - 1 - s, 1), :]
            dcw_ref[pl.ds(CONV_K - 1 - s, 1), :] += _colsum(shifted * a0)
        dca = da0 * sig
        dcg = da0 * ca * sig * (1.0 - sig)

        for k, dzk in ((0, dca), (1, dcg)):
            dzb = dzk.astype(BF16)
            dh1 = dh1 + _nt(dzb, win_v[k])
            dwin_acc[k] += _tn(h1b, dzb)
        dx, dg1, dsc1, dsh1 = h1_vjp(dh1)
        gx_ref[...] = dx2v + dx
        dg1_ref[...] += dg1
        dsc1_ref[...] += dsc1
        dsh1_ref[...] += dsh1

        @pl.when(i == n_tiles - 1)
        def _():
            for h in range(N_HEADS):
                dws_ref[h] = dws_ref[h] * mask_ref[...]
            head_of_lane = lax.broadcasted_iota(jnp.int32, (N_HEADS, D_HALF), 1) // HEAD_DIM
            pick = (head_of_lane == lax.broadcasted_iota(jnp.int32, (N_HEADS, D_HALF), 0)).astype(F32)
            dbs_ref[...] = lax.dot_general(pick, dbs_acc[...], NT_DIMS, precision=lax.Precision.HIGHEST,
                                           preferred_element_type=F32)
            for k in range(N_SHARD):
                win_v[k] = dwin_acc[k].astype(BF16)
            wout_v[...] = dwout_acc[...].astype(BF16)
            mx, my, mc = _coords()
            h_in, h_out = dwin_acc.shape[1] // 2, dwout_acc.shape[0] // (2 * N_SHARD)

            def in_rows(ref, k, which):
                return ref.at[k, pl.ds(pl.multiple_of(which * h_in, 16), h_in), :]

            def out_rows(ref, k, which):
                return ref.at[pl.ds(pl.multiple_of((2 * k + which) * h_out, 16), h_out), :]

            pairs = ((win_v, dwin_acc, lwin, in_rows, dwin_hbm, dwin16_hbm),
                     (wout_v, dwout_acc, lwout, out_rows, dwout_hbm, dwout16_hbm))
            swaps = [pltpu.make_async_remote_copy(
                src_ref=rows_of(v16, k, 1 - mc), dst_ref=land.at[k], send_sem=pair_send.at[w, k],
                recv_sem=pair_recv.at[w, k], device_id=(mx, my, 1 - mc), device_id_type=MESH)
                for w, (v16, _, land, rows_of, _, _) in enumerate(pairs) for k in range(N_SHARD)]
            for cp in swaps:
                cp.start()
            for cp in swaps:
                cp.wait()
            outs = []
            for w, (v16, acc, land, rows_of, half_hbm, half16_hbm) in enumerate(pairs):
                for k in range(N_SHARD):
                    total = rows_of(acc, k, mc)[...] + land[k].astype(F32)
                    rows_of(acc, k, 0)[...] = total
                    rows_of(v16, k, 0)[...] = total.astype(BF16)
                    outs.append(pltpu.make_async_copy(rows_of(acc, k, 0), half_hbm.at[k], sem.at[2 + 8 * w + k]))
                    outs.append(pltpu.make_async_copy(rows_of(v16, k, 0), half16_hbm.at[k], sem.at[6 + 8 * w + k]))
            for cp in outs:
                cp.start()
            for cp in outs:
                cp.wait()
            for rc in _scatter_copies(dwup16_hbm, land_hbm, send_sems, recv_sems):
                rc.wait()

    def rev(width):
        return pl.BlockSpec((t, width), lambda i: (n_tiles - 1 - i, 0))

    v1024 = jax.ShapeDtypeStruct((1, D_MODEL), F32)
    v512 = jax.ShapeDtypeStruct((1, D_HALF), F32)
    small_shapes = [v1024, v1024, v1024, v1024, jax.ShapeDtypeStruct((CONV_K, D_HALF), F32), v512, v512, v512, v512,
                    v512, v1024, jax.ShapeDtypeStruct((N_HEADS, CHUNK, CHUNK), F32),
                    jax.ShapeDtypeStruct((N_HEADS, CHUNK), F32)]
    n_peer = len(PEER_FLIPS)
    half_in = (N_SHARD, w_in_g.shape[1] // 2, w_in_g.shape[2])
    half_out = (N_SHARD, w_out_g.shape[0] // (2 * N_SHARD), w_out_g.shape[1])
    return pl.pallas_call(
        body, grid=(n_tiles,), name="bwd_mixer",
        in_specs=[rev(D_MODEL), rev(D_MODEL), rev(4 * D_HALF), rev(D_HALF), rev(D_HALF), rev(D_MODEL),
                  rev(D_MODEL)] + [_full(v.shape) for v in vecs]
        + [_full(conv_w.shape), _full(wpair.shape), _full(wpair_t.shape), _full(causal_mask.shape), ANY, ANY, ANY],
        out_specs=[rev(D_MODEL)] + [_full(s.shape) for s in small_shapes] + [ANY] * 5,
        out_shape=[jax.ShapeDtypeStruct((seq, D_MODEL), F32)] + small_shapes
        + [jax.ShapeDtypeStruct(half_in, F32), jax.ShapeDtypeStruct(half_out, F32), _land_shape(dwup16),
           jax.ShapeDtypeStruct(half_in, BF16), jax.ShapeDtypeStruct(half_out, BF16)],
        scratch_shapes=[pltpu.VMEM(w_in_g.shape, BF16), pltpu.VMEM(w_out_g.shape, BF16),
                        pltpu.VMEM(w_in_g.shape, F32), pltpu.VMEM(w_out_g.shape, F32),
                        pltpu.VMEM((HALO, D_HALF), F32), pltpu.VMEM((8, t + HALO, D_HALF), F32),
                        pltpu.VMEM((CHUNK, D_HALF), F32), pltpu.VMEM(half_in, BF16), pltpu.VMEM(half_out, BF16),
                        pltpu.SemaphoreType.DMA((2 + 4 * N_SHARD,)),
                        pltpu.SemaphoreType.DMA((n_peer,)), pltpu.SemaphoreType.DMA((n_peer,)),
                        pltpu.SemaphoreType.DMA((2, N_SHARD)), pltpu.SemaphoreType.DMA((2, N_SHARD))],
        compiler_params=_params(),
    )(dx2, x, z, a1, sp, yb, o1, *vecs, conv_w, wpair, wpair_t, causal_mask, w_in_g, w_out_g, dwup16)


def _gmlp_operands(gm_ws, gm_bs):
    mask = jnp.tril(jnp.ones((CHUNK, CHUNK), F32))
    ws = gm_ws * mask[None]
    wpair = ws.reshape(N_HEADS // 2, 2 * CHUNK, CHUNK).astype(BF16)
    wpair_t = jnp.swapaxes(ws, 1, 2).reshape(N_HEADS // 2, 2 * CHUNK, CHUNK).astype(BF16)
    bs_full = jnp.repeat(jnp.transpose(gm_bs), HEAD_DIM, axis=1)
    return wpair, wpair_t, bs_full, mask


def _local_step(x, target, mod, p, w_in_g, w_out_g, w_up_part, w_down_part):
    sh1, sc1, gt1, sh2, sc2, gt2 = [mod[:, k * D_MODEL:(k + 1) * D_MODEL] for k in range(6)]
    vec = dict(p, sh1=sh1, sc1=sc1, gt1=gt1, sh2=sh2, sc2=sc2, gt2=gt2)
    wpair, wpair_t, bs_full, mask = _gmlp_operands(p["gm_ws"], p["gm_bs"])

    (z, a1, sp, yb, o1, x2), (w_up_g, w_down_g) = _fwd_mixer(
        x, vec, p["conv_dw_w"], wpair, bs_full, w_in_g, w_out_g, [w_up_part, w_down_part])
    w_down_g = w_down_g.reshape(D_FF, D_MODEL)
    to_inter, to_natural = _interleave_matrices()
    du, dx2, d_ffn_w, d_ffn_b, d_fg, d_gt2, d_g2, d_sc2, d_sh2, loss, d_wd, d_wd16 = _ffn(
        x2, target, p["norm2_gain"], sc2, sh2, p["ffn_dw_w"], p["ffn_dw_b"], gt2, p["final_gain"], w_up_g, w_down_g,
        to_inter, to_natural)
    by_shard = (N_SHARD, -1, D_MODEL)
    d_wup, d_wup16, land_wd = _bwd_up(du, x2, p["norm2_gain"], sc2, sh2, d_wd16.reshape(by_shard))
    (gx, d_g1, d_sc1, d_sh1, d_gt1, d_cw, d_cb, d_clg, d_clb, d_vg, d_vb, d_mg, d_ws, d_bs, d_win, d_wout, land_wup,
     d_win16, d_wout16) = _bwd_mixer(dx2, x, z, a1, sp, yb, o1, vec, p["conv_dw_w"], wpair, wpair_t, mask, w_in_g,
                                     w_out_g, d_wup16)
    d_mod = _pack([d_sh1, d_sc1, d_gt1, d_sh2, d_sc2, d_gt2], 6).reshape(1, 6 * D_MODEL)
    grads = dict(norm1_gain=d_g1, conv_dw_w=d_cw, conv_dw_b=d_cb, conv_ln_g=d_clg, conv_ln_b=d_clb, gm_ln_g=d_vg,
                 gm_ln_b=d_vb, gm_ws=d_ws, gm_bs=d_bs, mix_out_gain=d_mg, norm2_gain=d_g2, ffn_dw_w=d_ffn_w,
                 ffn_dw_b=d_ffn_b, final_gain=d_fg, w_in=d_win, w_out=d_wout, w_up=d_wup, w_down=d_wd.reshape(by_shard))
    in_flight = dict(w_in16=d_win16, w_out16=d_wout16, land_w_up=land_wup, land_w_down=land_wd)
    return gx, grads, d_mod, loss, in_flight


MESH = pl.DeviceIdType.MESH
VMEM_SPEC = pl.BlockSpec(memory_space=pltpu.VMEM)
PEER_FLIPS = [(a, b, d) for a in (0, 1) for b in (0, 1) for d in (0, 1)][1:]
CHIP_FLIPS = [(1, 0), (0, 1), (1, 1)]


def _coords():
    return lax.axis_index("x"), lax.axis_index("y"), lax.axis_index("c")


def _flip(v, bit):
    return 1 - v if bit else v


def _rows8(block):
    return pl.ds(pl.multiple_of(8 * block, 8), 8)


def _ada_steps(c_ref, w_ref, b_ref, call_ref, mod_ref, cpad, modall, send_sems, recv_sems):
    x, y, c = _coords()
    me = 4 * x + 2 * y + c
    cpad[...] = jnp.zeros_like(cpad)
    cpad[pl.ds(0, 1), :] = c_ref[...]

    def gather_copy(j, flip):
        peer = (_flip(x, flip[0]), _flip(y, flip[1]), _flip(c, flip[2]))
        return pltpu.make_async_remote_copy(
            src_ref=cpad, dst_ref=call_ref.at[_rows8(me)], send_sem=send_sems.at[j], recv_sem=recv_sems.at[j],
            device_id=peer, device_id_type=MESH)

    def piece_copy(j, flip):
        tx, ty = _flip(x, flip[0]), _flip(y, flip[1])
        return pltpu.make_async_remote_copy(
            src_ref=modall.at[_rows8(4 * tx + 2 * ty + c)], dst_ref=mod_ref.at[_rows8(2 * x + y)],
            send_sem=send_sems.at[len(PEER_FLIPS) + j], recv_sem=recv_sems.at[len(PEER_FLIPS) + j],
            device_id=(tx, ty, c), device_id_type=MESH)

    copies = [gather_copy(j, f) for j, f in enumerate(PEER_FLIPS)]
    for cp in copies:
        cp.start()
    call_ref[_rows8(me), :] = cpad[...]

    def middle():
        for cp in copies:
            cp.wait_recv()
        for cp in copies:
            cp.wait_send()
        cv = call_ref[...]
        c_act = (cv * jax.nn.sigmoid(cv)).astype(BF16)
        modall[...] = _nn(c_act, w_ref[...].astype(BF16)) + b_ref[...]
        for j, f in enumerate(CHIP_FLIPS):
            piece_copy(j, f).start()
        mod_ref[_rows8(2 * x + y), :] = modall[_rows8(me), :]

    def finish():
        for j, f in enumerate(CHIP_FLIPS):
            piece_copy(j, f).wait_recv()
        for j, f in enumerate(CHIP_FLIPS):
            piece_copy(j, f).wait_send()

    return middle, finish


def _gather_weights(shards, filters, n_now, c_row, w_ada_sh, b_ada_sh):
    n = len(shards)
    nf = len(filters)
    ada_cols = w_ada_sh.shape[1]

    def body(*refs):
        ins, f_ins, ada_ins = refs[:n], refs[n:n + nf], refs[n + nf:n + nf + 3]
        refs = refs[n + nf + 3:]
        outs, f_outs, ada_outs = refs[:n], refs[n:n + nf], refs[n + nf:n + nf + 2]
        refs = refs[n + nf + 2:]
        stage = refs[:n]
        send_sems, recv_sems, local_sems, f_send_sems, f_recv_sems, cpad, modall, ada_send, ada_recv = refs[n:]
        ada_middle, ada_finish = _ada_steps(*ada_ins, *ada_outs, cpad, modall, ada_send, ada_recv)
        x, y, c = _coords()
        k = 2 * x + y
        sibling = (x, y, 1 - c)

        def filter_copy(w, j, slot):
            tx, ty = _flip(x, CHIP_FLIPS[j][0]), _flip(y, CHIP_FLIPS[j][1])
            return pltpu.make_async_remote_copy(
                src_ref=f_ins[w], dst_ref=f_outs[w].at[slot], send_sem=f_send_sems.at[w, j],
                recv_sem=f_recv_sems.at[w, j], device_id=(tx, ty, c), device_id_type=MESH)

        def half(w, which):
            h = shards[w].shape[0] // 2
            return pl.ds(pl.multiple_of(which * h, 16), h)

        def ici_copy(w, j, src, slot):
            tx, ty = _flip(x, CHIP_FLIPS[j][0]), _flip(y, CHIP_FLIPS[j][1])
            return pltpu.make_async_remote_copy(
                src_ref=src, dst_ref=outs[w].at[slot, half(w, c)], send_sem=send_sems.at[w, j],
                recv_sem=recv_sems.at[w, j], device_id=(tx, ty, c), device_id_type=MESH)

        def d2d_copy(w, j, slot, which):
            rows = outs[w].at[slot, half(w, which)]
            return pltpu.make_async_remote_copy(
                src_ref=rows, dst_ref=rows, send_sem=send_sems.at[w, len(CHIP_FLIPS) + j],
                recv_sem=recv_sems.at[w, len(CHIP_FLIPS) + j], device_id=sibling, device_id_type=MESH)

        def chip_of(j):
            return 2 * _flip(x, CHIP_FLIPS[j][0]) + _flip(y, CHIP_FLIPS[j][1])

        local, first, passed = [], [], []
        for w in range(nf):
            local.append(pltpu.make_async_copy(f_ins[w], f_outs[w].at[k], local_sems.at[n + w]))
            local[-1].start()
            for j in range(len(CHIP_FLIPS)):
                first.append(filter_copy(w, j, k))
                first[-1].start()
        for w in range(n):
            stage[w][...] = ins[w][...].astype(BF16)
            local.append(pltpu.make_async_copy(stage[w], outs[w].at[k], local_sems.at[w]))
            local[-1].start()
            if w < n_now:
                for j in range(len(CHIP_FLIPS)):
                    first.append(ici_copy(w, j, stage[w].at[half(w, c)], k))
                    first[-1].start()
        ada_middle()
        for w in range(nf):
            for j in range(len(CHIP_FLIPS)):
                filter_copy(w, j, chip_of(j)).wait_recv()
        for w in range(n_now):
            for j in range(len(CHIP_FLIPS)):
                ici_copy(w, j, stage[w].at[half(w, c)], chip_of(j)).wait_recv()
                passed.append(d2d_copy(w, j, chip_of(j), c))
                passed[-1].start()
        for w in range(n_now):
            for j in range(len(CHIP_FLIPS)):
                d2d_copy(w, j, chip_of(j), 1 - c).wait_recv()
        for cp in first + passed:
            cp.wait_send()
        for cp in local:
            cp.wait()
        ada_finish()

    sem_shape = (n_now, 2 * len(CHIP_FLIPS))
    f_sem_shape = (nf, len(CHIP_FLIPS))
    n_ada_sem = len(PEER_FLIPS) + len(CHIP_FLIPS)
    outs = pl.pallas_call(
        body, name="gather_weights",
        in_specs=[VMEM_SPEC] * (n + nf + 3), out_specs=[ANY] * (n + nf) + [VMEM_SPEC, VMEM_SPEC],
        out_shape=[jax.ShapeDtypeStruct((N_SHARD,) + s.shape, BF16) for s in shards]
        + [jax.ShapeDtypeStruct((N_SHARD,) + s.shape, F32) for s in filters]
        + [jax.ShapeDtypeStruct((8 * N_DEV, D_MODEL), F32), jax.ShapeDtypeStruct((8 * N_SHARD, ada_cols), F32)],
        scratch_shapes=[pltpu.VMEM(s.shape, BF16) for s in shards]
        + [pltpu.SemaphoreType.DMA(sem_shape), pltpu.SemaphoreType.DMA(sem_shape), pltpu.SemaphoreType.DMA((n + nf,)),
           pltpu.SemaphoreType.DMA(f_sem_shape), pltpu.SemaphoreType.DMA(f_sem_shape),
           pltpu.VMEM((8, D_MODEL), F32), pltpu.VMEM((8 * N_DEV, ada_cols), F32),
           pltpu.SemaphoreType.DMA((n_ada_sem,)), pltpu.SemaphoreType.DMA((n_ada_sem,))],
        compiler_params=pltpu.CompilerParams(vmem_limit_bytes=VMEM_LIMIT_BYTES),
    )(*shards, *filters, c_row, w_ada_sh, b_ada_sh)
    return outs[:n], outs[n:n + nf], outs[n + nf], outs[n + nf + 1]


def _final_comm(srcs16, small):
    n = len(srcs16)
    rows = small.shape[0]
    half = rows // 2
    quarter = half // 2

    def body(*refs):
        srcs, small_ref = refs[:n], refs[n]
        lands, small_out = refs[n + 1:2 * n + 1], refs[2 * n + 1]
        chip_sum, got_c, got_1, got_2, part, send_sems, recv_sems, small_send_sems, small_recv_sems = refs[2 * n + 2:]
        x, y, c = _coords()
        sibling = (x, y, 1 - c)
        mine = pl.ds(pl.multiple_of(c * half, 8), half)
        copies = []
        for w in range(n):
            for j, flip in enumerate(CHIP_FLIPS):
                tx, ty = _flip(x, flip[0]), _flip(y, flip[1])
                copies.append(pltpu.make_async_remote_copy(
                    src_ref=srcs[w].at[2 * tx + ty], dst_ref=lands[w].at[j], send_sem=send_sems.at[w, j],
                    recv_sem=recv_sems.at[w, j], device_id=(tx, ty, c), device_id_type=MESH))
        for cp in copies:
            cp.start()

        def exchange(pairs):
            rcs = [pltpu.make_async_remote_copy(
                src_ref=src, dst_ref=dst, send_sem=small_send_sems.at[k], recv_sem=small_recv_sems.at[k],
                device_id=peer, device_id_type=MESH) for k, src, dst, peer in pairs]
            for rc in rcs:
                rc.start()
            for rc in rcs:
                rc.wait()

        def quarter_rows(q):
            return pl.ds(pl.multiple_of(c * half + q * quarter, 8), quarter)

        along = ((1 - x, y, c), (x, 1 - y, c))
        exchange([(0, small_ref, got_c, sibling)])
        chip_sum[...] = small_ref[...] + got_c[...]
        exchange([(1 + q, chip_sum.at[quarter_rows(q)], got_1.at[q], along[q]) for q in range(2)])
        for q in range(2):
            part[q] = chip_sum[quarter_rows(q), :] + got_1[q]
        exchange([(3 + q, part.at[q], got_2.at[q], along[1 - q]) for q in range(2)])
        for q in range(2):
            small_out[quarter_rows(q), :] = part[q] + got_2[q]
        exchange([(5, small_out.at[mine], small_out.at[mine], sibling)])
        for cp in copies:
            cp.wait()

    n_chip = len(CHIP_FLIPS)
    quarter_shape = (2, quarter, small.shape[1])
    outs = pl.pallas_call(
        body, name="final_comm",
        in_specs=[ANY] * n + [VMEM_SPEC], out_specs=[ANY] * n + [VMEM_SPEC],
        out_shape=[jax.ShapeDtypeStruct((n_chip,) + a.shape[1:], BF16) for a in srcs16]
        + [jax.ShapeDtypeStruct(small.shape, F32)],
        scratch_shapes=[pltpu.VMEM(small.shape, F32), pltpu.VMEM(small.shape, F32), pltpu.VMEM(quarter_shape, F32),
                        pltpu.VMEM(quarter_shape, F32), pltpu.VMEM(quarter_shape, F32),
                        pltpu.SemaphoreType.DMA((n, n_chip)), pltpu.SemaphoreType.DMA((n, n_chip)),
                        pltpu.SemaphoreType.DMA((6,)), pltpu.SemaphoreType.DMA((6,))],
        compiler_params=pltpu.CompilerParams(vmem_limit_bytes=VMEM_LIMIT_BYTES),
    )(*srcs16, small)
    return outs[:n], outs[n]


ADD_CHUNKS = 4


def _scatter_sum_swap(pos, owns, lands):
    n = len(owns)

    def layout(own_shape, land_shape):
        peers, rows, cols = land_shape
        pick = 1 if own_shape[1] == 2 * rows else 0
        if cols % (128 * ADD_CHUNKS) == 0:
            width = cols // ADD_CHUNKS
            blk = (rows, width)
            return (pl.BlockSpec((1,) + blk, lambda i, p: (2 * p[0] + p[1], pick * p[2], i)),
                    pl.BlockSpec((peers,) + blk, lambda i, p: (0, 0, i)), (ADD_CHUNKS,) + blk,
                    lambda ref, which, j: ref.at[which, :, pl.ds(pl.multiple_of(j * width, 128), width)])
        height = rows // ADD_CHUNKS
        blk = (height, cols)
        return (pl.BlockSpec((1,) + blk, lambda i, p: (2 * p[0] + p[1], pick * p[2] * ADD_CHUNKS + i, 0)),
                pl.BlockSpec((peers,) + blk, lambda i, p: (0, i, 0)), (ADD_CHUNKS,) + blk,
                lambda ref, which, j: ref.at[which, pl.ds(pl.multiple_of(j * height, 8), height), :])

    layouts = [layout(o.shape, l.shape) for o, l in zip(owns, lands)]

    def body(pos_ref, *refs):
        outs, stages = refs[2 * n:3 * n], refs[3 * n:4 * n]
        local_sems, send_sems, recv_sems = refs[4 * n:]
        i = pl.program_id(0)
        x, y, c = _coords()

        def copies(idx, j):
            chunk_of = layouts[idx][3]
            return (pltpu.make_async_copy(stages[idx].at[j], chunk_of(outs[idx], c, j), local_sems.at[idx, j]),
                    pltpu.make_async_remote_copy(
                        src_ref=stages[idx].at[j], dst_ref=chunk_of(outs[idx], c, j), send_sem=send_sems.at[idx, j],
                        recv_sem=recv_sems.at[idx, j], device_id=(x, y, 1 - c), device_id_type=MESH))

        for idx in range(n):
            own, land = refs[idx], refs[n + idx]
            total = own[0]
            for f in range(land.shape[0]):
                total = total + land[f].astype(F32)
            stages[idx][i] = total
            for cp in copies(idx, i):
                cp.start()

        @pl.when(i == ADD_CHUNKS - 1)
        def _():
            for idx in range(n):
                for j in range(ADD_CHUNKS):
                    for cp in copies(idx, j):
                        cp.wait()

    sem_shape = (n, ADD_CHUNKS)
    return pl.pallas_call(
        body, name="scatter_sum_swap",
        grid_spec=pltpu.PrefetchScalarGridSpec(
            num_scalar_prefetch=1, grid=(ADD_CHUNKS,),
            in_specs=[s[0] for s in layouts] + [s[1] for s in layouts], out_specs=[ANY] * n,
            scratch_shapes=[pltpu.VMEM(s[2], F32) for s in layouts]
            + [pltpu.SemaphoreType.DMA(sem_shape), pltpu.SemaphoreType.DMA(sem_shape),
               pltpu.SemaphoreType.DMA(sem_shape)]),
        out_shape=[jax.ShapeDtypeStruct((2,) + l.shape[1:], F32) for l in lands],
        compiler_params=_params(),
    )(pos, *owns, *lands)


def _adamw_math(w, g, m, v):
    m = ADAM_B1 * m + (1.0 - ADAM_B1) * g
    v = ADAM_B2 * v + (1.0 - ADAM_B2) * jnp.square(g)
    m_hat = m / (1.0 - ADAM_B1 ** ADAM_STEP)
    v_hat = v / (1.0 - ADAM_B2 ** ADAM_STEP)
    delta = -ADAM_LR * (m_hat / (jnp.sqrt(v_hat) + ADAM_EPS) + ADAM_WD * w)
    return delta, m, v


def _adamw_group(ws, gs, ms, vs, n_steps):
    n = len(ws)

    def body(*refs):
        w_refs, g_refs, m_refs, v_refs = (refs[q * n:(q + 1) * n] for q in range(4))
        d_outs, m_outs, v_outs = (refs[(4 + q) * n:(5 + q) * n] for q in range(3))
        for idx in range(n):
            d_outs[idx][...], m_outs[idx][...], v_outs[idx][...] = _adamw_math(
                w_refs[idx][...], g_refs[idx][...], m_refs[idx][...], v_refs[idx][...])

    specs = [pl.BlockSpec((w.shape[0] // n_steps, w.shape[1]), lambda i: (i, 0)) for w in ws]
    shapes = [jax.ShapeDtypeStruct(w.shape, F32) for w in ws]
    outs = pl.pallas_call(
        body, grid=(n_steps,), name="adamw_projections", in_specs=specs * 4, out_specs=specs * 3,
        out_shape=shapes * 3, compiler_params=_params(),
    )(*ws, *gs, *ms, *vs)
    return outs[:n], outs[n:2 * n], outs[2 * n:]


def _adamw_many(ws, gs, ms, vs):
    n = len(ws)

    def body(*refs):
        w_refs, g_refs, m_refs, v_refs = (refs[q * n:(q + 1) * n] for q in range(4))
        d_outs, m_outs, v_outs = (refs[(4 + q) * n:(5 + q) * n] for q in range(3))
        for idx in range(n):
            d_outs[idx][...], m_outs[idx][...], v_outs[idx][...] = _adamw_math(
                w_refs[idx][...], g_refs[idx][...], m_refs[idx][...], v_refs[idx][...])

    shapes = [jax.ShapeDtypeStruct(w.shape, F32) for w in ws]
    outs = pl.pallas_call(
        body, name="adamw_small", in_specs=[VMEM_SPEC] * (4 * n), out_specs=[VMEM_SPEC] * (3 * n),
        out_shape=shapes * 3, compiler_params=pltpu.CompilerParams(vmem_limit_bytes=VMEM_LIMIT_BYTES),
    )(*ws, *gs, *ms, *vs)
    return outs[:n], outs[n:2 * n], outs[2 * n:]


def _adamw_ada(c_all16, dmod16, w, m, v, block_rows):
    rows, cols = w.shape

    def body(c_ref, dm_ref, w_ref, m_ref, v_ref, g_out, d_out, m_out, v_out):
        cv = c_ref[...]
        g = _tn((cv * jax.nn.sigmoid(cv)).astype(BF16), dm_ref[...].astype(BF16))
        g_out[...] = g
        d_out[...], m_out[...], v_out[...] = _adamw_math(w_ref[...], g, m_ref[...], v_ref[...])

    spec = pl.BlockSpec((block_rows, cols), lambda i: (i, 0))
    shape = jax.ShapeDtypeStruct((rows, cols), F32)
    return pl.pallas_call(
        body, grid=(rows // block_rows,), name="adamw_w_ada",
        in_specs=[pl.BlockSpec((16, block_rows), lambda i: (0, i)), _full(dmod16.shape), spec, spec, spec],
        out_specs=[spec] * 4, out_shape=[shape] * 4, compiler_params=_params(),
    )(c_all16, dmod16, w, m, v)


SMALL_REPLICATED = ["b_ada", "norm1_gain", "conv_dw_b", "conv_ln_g", "conv_ln_b", "gm_ln_g", "gm_ln_b", "gm_ws", "gm_bs",
                    "mix_out_gain", "norm2_gain", "ffn_dw_b", "final_gain"]
SMALL_SHARDED = ["conv_dw_w", "ffn_dw_w"]
PACK_ROWS = 256
WEIGHT_ORDER = ["w_ada", "b_ada", "norm1_gain", "w_in", "conv_dw_w", "conv_dw_b", "conv_ln_g", "conv_ln_b", "gm_ln_g",
                "gm_ln_b", "gm_ws", "gm_bs", "mix_out_gain", "w_out", "norm2_gain", "w_up", "ffn_dw_w", "ffn_dw_b",
                "w_down", "final_gain"]


def _pack(parts, rows):
    total = rows * D_MODEL
    flat, offset = None, 0
    for a in parts:
        piece = jnp.pad(a.reshape(-1), (offset, total - offset - a.size))
        flat = piece if flat is None else flat + piece
        offset += a.size
    return flat.reshape(rows, D_MODEL)


def _unpack(packed, shapes):
    flat = packed.reshape(-1)
    out, pos = [], 0
    for s in shapes:
        size = 1
        for d in s:
            size *= d
        out.append(flat[pos:pos + size].reshape(s))
        pos += size
    return out


def kernel(x, c, w_ada, b_ada, norm1_gain, w_in, conv_dw_w, conv_dw_b, conv_ln_g, conv_ln_b, gm_ln_g, gm_ln_b, gm_ws, gm_bs, mix_out_gain, w_out, norm2_gain, w_up, ffn_dw_w, ffn_dw_b, w_down, final_gain, loss_target, m_w_ada, m_b_ada, m_norm1_gain, m_w_in, m_conv_dw_w, m_conv_dw_b, m_conv_ln_g, m_conv_ln_b, m_gm_ln_g, m_gm_ln_b, m_gm_ws, m_gm_bs, m_mix_out_gain, m_w_out, m_norm2_gain, m_w_up, m_ffn_dw_w, m_ffn_dw_b, m_w_down, m_final_gain, v_w_ada, v_b_ada, v_norm1_gain, v_w_in, v_conv_dw_w, v_conv_dw_b, v_conv_ln_g, v_conv_ln_b, v_gm_ln_g, v_gm_ln_b, v_gm_ws, v_gm_bs, v_mix_out_gain, v_w_out, v_norm2_gain, v_w_up, v_ffn_dw_w, v_ffn_dw_b, v_w_down, v_final_gain):
    weights = dict(w_ada=w_ada, b_ada=b_ada, norm1_gain=norm1_gain, w_in=w_in, conv_dw_w=conv_dw_w, conv_dw_b=conv_dw_b,
                   conv_ln_g=conv_ln_g, conv_ln_b=conv_ln_b, gm_ln_g=gm_ln_g, gm_ln_b=gm_ln_b, gm_ws=gm_ws, gm_bs=gm_bs,
                   mix_out_gain=mix_out_gain, w_out=w_out, norm2_gain=norm2_gain, w_up=w_up, ffn_dw_w=ffn_dw_w,
                   ffn_dw_b=ffn_dw_b, w_down=w_down, final_gain=final_gain)
    mom1 = dict(w_ada=m_w_ada, b_ada=m_b_ada, norm1_gain=m_norm1_gain, w_in=m_w_in, conv_dw_w=m_conv_dw_w,
                conv_dw_b=m_conv_dw_b, conv_ln_g=m_conv_ln_g, conv_ln_b=m_conv_ln_b, gm_ln_g=m_gm_ln_g, gm_ln_b=m_gm_ln_b,
                gm_ws=m_gm_ws, gm_bs=m_gm_bs, mix_out_gain=m_mix_out_gain, w_out=m_w_out, norm2_gain=m_norm2_gain,
                w_up=m_w_up, ffn_dw_w=m_ffn_dw_w, ffn_dw_b=m_ffn_dw_b, w_down=m_w_down, final_gain=m_final_gain)
    mom2 = dict(w_ada=v_w_ada, b_ada=v_b_ada, norm1_gain=v_norm1_gain, w_in=v_w_in, conv_dw_w=v_conv_dw_w,
                conv_dw_b=v_conv_dw_b, conv_ln_g=v_conv_ln_g, conv_ln_b=v_conv_ln_b, gm_ln_g=v_gm_ln_g, gm_ln_b=v_gm_ln_b,
                gm_ws=v_gm_ws, gm_bs=v_gm_bs, mix_out_gain=v_mix_out_gain, w_out=v_w_out, norm2_gain=v_norm2_gain,
                w_up=v_w_up, ffn_dw_w=v_ffn_dw_w, ffn_dw_b=v_ffn_dw_b, w_down=v_w_down, final_gain=v_final_gain)
    shard = 2 * lax.axis_index("x") + lax.axis_index("y")
    me = 2 * shard + lax.axis_index("c")

    ada_cols = w_ada.shape[2]
    b_ada_sh = lax.dynamic_slice(b_ada, (0, shard * ada_cols), (1, ada_cols))
    (w_in_g, w_out_g, w_up_part, w_down_part), (conv_w_g, ffn_w_g), c_all64, mod32 = _gather_weights(
        [w_in[0], w_out[0], w_up[0], w_down[0]], [conv_dw_w[0], ffn_dw_w[0]], 2, c, w_ada[0], b_ada_sh)
    c_all = c_all64[::8]
    mod = mod32[::8].reshape(1, N_SHARD * ada_cols)
    conv_w_full = jnp.transpose(conv_w_g, (1, 0, 2)).reshape(CONV_K, D_HALF)
    ffn_w_full = jnp.transpose(ffn_w_g, (1, 0, 2)).reshape(FFN_K, 2 * D_FF)

    p = dict(norm1_gain=norm1_gain, conv_dw_w=conv_w_full, conv_dw_b=conv_dw_b, conv_ln_g=conv_ln_g,
             conv_ln_b=conv_ln_b, gm_ln_g=gm_ln_g, gm_ln_b=gm_ln_b, gm_ws=gm_ws[0], gm_bs=gm_bs[0],
             mix_out_gain=mix_out_gain, norm2_gain=norm2_gain, ffn_dw_w=ffn_w_full, ffn_dw_b=ffn_dw_b,
             final_gain=final_gain[None])
    grad_x, g, d_mod, loss, in_flight = _local_step(
        x[0], loss_target[0], mod, p, w_in_g, w_out_g.reshape(D_MODEL, D_MODEL), w_up_part, w_down_part)

    n_mod = d_mod.shape[1]
    dmod_rows = lax.dynamic_update_slice(jnp.zeros((N_DEV, n_mod), F32), d_mod, (me, 0))
    g["b_ada"] = d_mod
    small = _pack([g[k] for k in SMALL_REPLICATED] + [g[k] for k in SMALL_SHARDED] + [dmod_rows, loss[0, :1]], PACK_ROWS)
    (land_w_in, land_w_out), small = _final_comm([in_flight["w_in16"], in_flight["w_out16"]], small)
    pos = jnp.stack(_coords()).astype(jnp.int32)
    full = _scatter_sum_swap(pos, [g["w_in"], g["w_out"], g["w_up"], g["w_down"]],
                             [land_w_in, land_w_out, in_flight["land_w_up"], in_flight["land_w_down"]])
    grads = dict(w_in=full[0].reshape(w_in.shape[1:]), w_out=full[1].reshape(w_out.shape[1:]),
                 w_up=full[2].reshape(w_up.shape[1:]), w_down=full[3].reshape(w_down.shape[1:]))

    small_shapes = ([weights[k].shape for k in SMALL_REPLICATED] + [(CONV_K, D_HALF), (FFN_K, 2 * D_FF)]
                    + [(N_DEV, n_mod), (1,)])
    *small_grads, conv_w_grad, ffn_w_grad, dmod_all, loss_sum = _unpack(small, small_shapes)
    grads.update(zip(SMALL_REPLICATED, small_grads))
    grads["conv_dw_w"] = lax.dynamic_slice(conv_w_grad, (0, shard * conv_dw_w.shape[2]), conv_dw_w.shape[1:])[None]
    grads["ffn_dw_w"] = lax.dynamic_slice(ffn_w_grad, (0, shard * ffn_dw_w.shape[2]), ffn_dw_w.shape[1:])[None]

    delta, new_m, new_v = {}, {}, {}
    projections = ["w_in", "w_out", "w_up", "w_down"]
    group_out = _adamw_group([weights[k][0] for k in projections], [grads[k] for k in projections],
                             [mom1[k][0] for k in projections], [mom2[k][0] for k in projections], n_steps=4)
    for d, arrs in zip((delta, new_m, new_v), group_out):
        d.update({k: a[None] for k, a in zip(projections, arrs)})
    for k in projections:
        grads[k] = grads[k][None]
    dmod_sh = lax.dynamic_slice(dmod_all, (0, shard * ada_cols), (N_DEV, ada_cols))
    pad8 = ((0, 16 - N_DEV), (0, 0))
    grads["w_ada"], delta["w_ada"], new_m["w_ada"], new_v["w_ada"] = [a[None] for a in _adamw_ada(
        jnp.pad(c_all, pad8), jnp.pad(dmod_sh, pad8), w_ada[0], m_w_ada[0], v_w_ada[0], 256)]
    small_names = SMALL_REPLICATED + SMALL_SHARDED

    def two_d(a):
        return a.reshape(1, -1) if a.ndim == 1 else a

    small_out = _adamw_many(*[[two_d(d[k]) for k in small_names] for d in (weights, grads, mom1, mom2)])
    for d, arrs in zip((delta, new_m, new_v), small_out):
        d.update({k: a.reshape(weights[k].shape) for k, a in zip(small_names, arrs)})

    return (loss_sum.reshape(()), grad_x[None], *[grads[k] for k in WEIGHT_ORDER], *[delta[k] for k in WEIGHT_ORDER],
            *[new_m[k] for k in WEIGHT_ORDER], *[new_v[k] for k in WEIGHT_ORDER])
```

```python
import jax
import jax.numpy as jnp
from jax import lax
from jax.experimental import pallas as pl
from jax.experimental.pallas import tpu as pltpu

F32 = jnp.float32
BF16 = jnp.bfloat16

D_MODEL = 1024
D_HALF = 512
D_FF = 2816
CONV_K = 31
FFN_K = 3
CHUNK = 128
N_HEADS = 8
HEAD_DIM = 64
N_SHARD = 4
N_DEV = 8
RMS_EPS = 1e-6
LN_EPS = 1e-5
ADAM_LR, ADAM_B1, ADAM_B2, ADAM_EPS, ADAM_WD, ADAM_STEP = 0.001, 0.9, 0.999, 1e-08, 0.01, 10

TILE = 256
HALO = 32
FFN_HALO = 16
FFN_BLK = 256
UP_SHARD = 2 * D_FF // N_SHARD
VMEM_LIMIT_BYTES = 56 * 1024 * 1024
FFN_VMEM_LIMIT_BYTES = 58 * 1024 * 1024

ANY = pl.BlockSpec(memory_space=pl.ANY)
NT_DIMS = (((1,), (1,)), ((), ()))
TN_DIMS = (((0,), (0,)), ((), ()))


def _full(shape):
    return pl.BlockSpec(shape, lambda i: (0,) * len(shape))


def _nn(a, b):
    return jnp.dot(a, b, preferred_element_type=F32)


def _nt(a, b):
    return lax.dot_general(a, b, NT_DIMS, preferred_element_type=F32)


def _tn(a, b):
    return lax.dot_general(a, b, TN_DIMS, preferred_element_type=F32)


def _colsum(a):
    return jnp.sum(a, axis=0, keepdims=True)


def _params(semantics=("arbitrary",)):
    return pltpu.CompilerParams(dimension_semantics=semantics, vmem_limit_bytes=VMEM_LIMIT_BYTES)


def _rms(v, gain):
    return v * lax.rsqrt(jnp.mean(v * v, axis=-1, keepdims=True) + RMS_EPS) * gain


def _layer_norm(v, gain, bias):
    mu = jnp.mean(v, axis=-1, keepdims=True)
    var = jnp.mean(jnp.square(v - mu), axis=-1, keepdims=True)
    return (v - mu) * lax.rsqrt(var + LN_EPS) * gain + bias


def _mod_norm(v, gain, scale, shift):
    return _rms(v, gain) * (1.0 + scale) + shift


def _conv_branch(a1, ln_g, ln_b, out_gain):
    a2 = _layer_norm(a1, ln_g, ln_b)
    return _rms(a2 * jax.nn.sigmoid(a2), out_gain)


def _gate_branch(gu, sp, out_gain):
    return _rms(jax.nn.gelu(gu) * sp, out_gain)


def _gv_norm(gv, ln_g, ln_b):
    return _layer_norm(jax.nn.gelu(gv), ln_g, ln_b)


def _rms_parts(v):
    r = lax.rsqrt(jnp.mean(v * v, axis=-1, keepdims=True) + RMS_EPS)
    return v * r, r


def _rms_back(dn, n, r):
    return r * (dn - n * jnp.mean(dn * n, axis=-1, keepdims=True))


def _ln_parts(v):
    mu = jnp.mean(v, axis=-1, keepdims=True)
    rs = lax.rsqrt(jnp.mean(jnp.square(v - mu), axis=-1, keepdims=True) + LN_EPS)
    return (v - mu) * rs, rs


def _ln_back(dn, n, rs):
    return rs * (dn - jnp.mean(dn, axis=-1, keepdims=True) - n * jnp.mean(dn * n, axis=-1, keepdims=True))


GELU_C = 0.7978845608028654
GELU_A = 0.044715


def _gelu_parts(v):
    v2 = v * v
    th = jnp.tanh(GELU_C * (v + GELU_A * (v2 * v)))
    cdf = 0.5 * (1.0 + th)
    return v * cdf, cdf + (0.5 * GELU_C) * v * (1.0 - th * th) * (1.0 + (3.0 * GELU_A) * v2)


def _rms_vjp(v, gain):
    n, r = _rms_parts(v)
    return n * gain, lambda dy: (_rms_back(dy * gain, n, r), _colsum(dy * n))


def _mod_norm_vjp(v, gain, scale, shift):
    n, r = _rms_parts(v)

    def back(dy):
        q = _colsum(dy * n)
        return _rms_back(dy * (gain * (1.0 + scale)), n, r), q * (1.0 + scale), q * gain, _colsum(dy)

    return n * gain * (1.0 + scale) + shift, back


def _conv_branch_vjp(a1, ln_g, ln_b, out_gain):
    n1, rs1 = _ln_parts(a1)
    a2 = n1 * ln_g + ln_b
    s = jax.nn.sigmoid(a2)
    a3 = a2 * s
    n3, r3 = _rms_parts(a3)

    def back(dy):
        da2 = _rms_back(dy * out_gain, n3, r3) * (s + a3 * (1.0 - s))
        return _ln_back(da2 * ln_g, n1, rs1), _colsum(da2 * n1), _colsum(da2), _colsum(dy * n3)

    return n3 * out_gain, back


def _gate_branch_vjp(gu, sp, out_gain):
    ge, dge = _gelu_parts(gu)
    n, r = _rms_parts(ge * sp)

    def back(dy):
        dg = _rms_back(dy * out_gain, n, r)
        return dg * sp * dge, dg * ge, _colsum(dy * n)

    return n * out_gain, back


def _gv_norm_vjp(gv, ln_g, ln_b):
    ge, dge = _gelu_parts(gv)
    n, rs = _ln_parts(ge)
    return n * ln_g + ln_b, lambda dy: (_ln_back(dy * ln_g, n, rs) * dge, _colsum(dy * n), _colsum(dy))


def _head_pair_matmul(wp_ref, v):
    lane = lax.broadcasted_iota(jnp.int32, (CHUNK, CHUNK), 1)
    rows = []
    for n in range(v.shape[0] // CHUNK):
        cols = []
        for j in range(N_HEADS // 2):
            r = _nn(wp_ref[j], v[n * CHUNK:(n + 1) * CHUNK, j * CHUNK:(j + 1) * CHUNK])
            cols.append(jnp.where(lane < HEAD_DIM, r[:CHUNK], r[CHUNK:]))
        rows.append(jnp.concatenate(cols, axis=1))
    return jnp.concatenate(rows, axis=0)


def _tile_bias(bs, tokens):
    return jnp.concatenate([bs] * (tokens // CHUNK), axis=0)


FORWARD_LEAD = 8


def _fwd_mixer(x, vec, conv_w, wpair, bs_full, w_in_g, w_out_g, late_parts):
    seq = x.shape[0]
    n_tiles = seq // TILE
    t = TILE
    n_late = len(late_parts)
    forward_step = max(n_tiles - FORWARD_LEAD, 0)
    names = ["norm1_gain", "sc1", "sh1", "gt1", "conv_dw_b", "conv_ln_g", "conv_ln_b", "gm_ln_g", "gm_ln_b",
             "mix_out_gain"]
    vecs = [vec[k] for k in names]

    def body(x_ref, g1, sc1, sh1, gt1, cb, clg, clb, vg, vb, mg, cw, wp, bs, win_hbm, wout_hbm, *rest):
        late = rest[n_late:2 * n_late]
        z_ref, a1_ref, sp_ref, y_ref, o1_ref, x2_ref = rest[2 * n_late:2 * n_late + 6]
        win_v, wout_v, halo, bank, sem, send_sems, recv_sems = rest[2 * n_late + 6:]
        i = pl.program_id(0)
        mx, my, mc = _coords()
        shard = 2 * mx + my

        def half(w, which):
            h = late[w].shape[1] // 2
            return pl.ds(pl.multiple_of(which * h, 16), h)

        def chip_of(j):
            return 2 * _flip(mx, CHIP_FLIPS[j][0]) + _flip(my, CHIP_FLIPS[j][1])

        def ici_copy(w, j, slot):
            rows = late[w].at[slot, half(w, mc)]
            return pltpu.make_async_remote_copy(
                src_ref=rows, dst_ref=rows, send_sem=send_sems.at[w, j], recv_sem=recv_sems.at[w, j],
                device_id=(_flip(mx, CHIP_FLIPS[j][0]), _flip(my, CHIP_FLIPS[j][1]), mc), device_id_type=MESH)

        def d2d_copy(w, j, which):
            rows = late[w].at[chip_of(j), half(w, which)]
            return pltpu.make_async_remote_copy(
                src_ref=rows, dst_ref=rows, send_sem=send_sems.at[w, len(CHIP_FLIPS) + j],
                recv_sem=recv_sems.at[w, len(CHIP_FLIPS) + j], device_id=(mx, my, 1 - mc), device_id_type=MESH)

        pairs = [(w, j) for w in range(n_late) for j in range(len(CHIP_FLIPS))]

        @pl.when(i == 0)
        def _():
            for w, j in pairs:
                ici_copy(w, j, shard).start()
            cps = [pltpu.make_async_copy(win_hbm, win_v, sem.at[0]),
                   pltpu.make_async_copy(wout_hbm, wout_v, sem.at[1])]
            for cp in cps:
                cp.start()
            for cp in cps:
                cp.wait()
            halo[...] = jnp.zeros_like(halo)

        @pl.when(i == forward_step)
        def _():
            for w, j in pairs:
                ici_copy(w, j, chip_of(j)).wait_recv()
                d2d_copy(w, j, mc).start()

        xv = x_ref[...]
        h1b = _mod_norm(xv, g1[...], sc1[...], sh1[...]).astype(BF16)
        zs = [_nn(h1b, win_v[k]) for k in range(N_SHARD)]
        for k in range(N_SHARD):
            z_ref[:, k * D_HALF:(k + 1) * D_HALF] = zs[k]
        ca, cg, gu, gv = zs
        a0 = ca * jax.nn.sigmoid(cg)
        ext = jnp.concatenate([halo[...], a0], axis=0)
        halo[...] = a0[t - HALO:]
        bank[0] = ext
        for b in range(1, 8):
            bank[b] = pltpu.roll(ext, b, axis=0)
        a1 = jnp.zeros((t, D_HALF), F32) + cb[...]
        for s in range(CONV_K):
            q, b = divmod(s, 8)
            a1 = a1 + bank[b, pl.ds(HALO - 8 * q, t), :] * cw[pl.ds(CONV_K - 1 - s, 1), :]
        a1_ref[...] = a1
        mgv = mg[...]
        ya = _conv_branch(a1, clg[...], clb[...], mgv[:, :D_HALF])
        gvn = _gv_norm(gv, vg[...], vb[...]).astype(BF16)
        sp = _head_pair_matmul(wp, gvn) + _tile_bias(bs[...], t)
        sp_ref[...] = sp
        yg = _gate_branch(gu, sp, mgv[:, D_HALF:])
        yb = jnp.concatenate([ya, yg], axis=1).astype(BF16)
        y_ref[...] = yb
        o1 = _nn(yb, wout_v[...])
        o1_ref[...] = o1
        x2_ref[...] = xv + gt1[...] * o1

        @pl.when(i == n_tiles - 1)
        def _():
            for w, j in pairs:
                d2d_copy(w, j, 1 - mc).wait_recv()
            for w, j in pairs:
                ici_copy(w, j, shard).wait_send()
                d2d_copy(w, j, mc).wait_send()

    def row(width):
        return pl.BlockSpec((t, width), lambda i: (i, 0))

    out_shape = [jax.ShapeDtypeStruct((seq, 4 * D_HALF), F32), jax.ShapeDtypeStruct((seq, D_HALF), F32),
                 jax.ShapeDtypeStruct((seq, D_HALF), F32), jax.ShapeDtypeStruct((seq, D_MODEL), BF16),
                 jax.ShapeDtypeStruct((seq, D_MODEL), F32), jax.ShapeDtypeStruct((seq, D_MODEL), F32)]
    n_in = 1 + len(vecs) + 3 + 2
    sem_shape = (n_late, 2 * len(CHIP_FLIPS))
    outs = pl.pallas_call(
        body, grid=(n_tiles,), name="fwd_mixer",
        in_specs=[row(D_MODEL)] + [_full(v.shape) for v in vecs]
        + [_full(conv_w.shape), _full(wpair.shape), _full(bs_full.shape), ANY, ANY] + [ANY] * n_late,
        out_specs=[ANY] * n_late + [row(4 * D_HALF), row(D_HALF), row(D_HALF), row(D_MODEL), row(D_MODEL),
                                    row(D_MODEL)],
        out_shape=[jax.ShapeDtypeStruct(a.shape, a.dtype) for a in late_parts] + out_shape,
        input_output_aliases={n_in + w: w for w in range(n_late)},
        scratch_shapes=[pltpu.VMEM(w_in_g.shape, BF16), pltpu.VMEM(w_out_g.shape, BF16),
                        pltpu.VMEM((HALO, D_HALF), F32), pltpu.VMEM((8, t + HALO, D_HALF), F32),
                        pltpu.SemaphoreType.DMA((2,)), pltpu.SemaphoreType.DMA(sem_shape),
                        pltpu.SemaphoreType.DMA(sem_shape)],
        compiler_params=_params(),
    )(x, *vecs, conv_w, wpair, bs_full, w_in_g, w_out_g, *late_parts)
    return outs[n_late:], outs[:n_late]


def _interleave_matrices():
    row = jnp.arange(TILE)
    token_of_row = (row % 8) * (TILE // 8) + row // 8
    to_inter = (token_of_row[:, None] == row[None, :]).astype(BF16)
    return to_inter, jnp.transpose(to_inter)


def _ffn(x2, target, norm2_gain, sc2, sh2, ffn_w, ffn_b, gt2, final_gain, w_up_g, w_down_g, to_inter, to_natural):
    seq = x2.shape[0]
    n_tiles = seq // TILE
    t = TILE
    n_blk = D_FF // FFN_BLK
    inv_d = 1.0 / D_MODEL

    def body(x2_ref, x2h_ref, tgt_ref, g2, sc2_ref, sh2_ref, fw, fb, gt2_ref, fg, pm_ref, pmt_ref, wup_hbm, wd_hbm,
             du_ref, dx2_ref, dfw_ref, dfb_ref, dfg_ref, dgt2_ref, dg2_ref, dsc2_ref, dsh2_ref, loss_ref, dwd_hbm,
             dwd16_hbm, wup_v, wd_v, dwd_acc, carry, u_s, sil_s, vds_s, f_s, du_s, sem):
        i = pl.program_id(0)
        tile = n_tiles - 1 - i
        sublane = lax.broadcasted_iota(jnp.int32, (8, FFN_BLK), 0)

        @pl.when(i == 0)
        def _():
            cps = [pltpu.make_async_copy(wd_hbm, wd_v, sem.at[0])]
            cps += [pltpu.make_async_copy(wup_hbm.at[k], wup_v.at[:, pl.ds(k * UP_SHARD, UP_SHARD)], sem.at[3 + k])
                    for k in range(N_SHARD)]
            for cp in cps:
                cp.start()
            for cp in cps:
                cp.wait()
            dwd_acc[...] = jnp.zeros_like(dwd_acc)
            carry[...] = jnp.zeros_like(carry)
            dfw_ref[...] = jnp.zeros_like(dfw_ref)
            dfb_ref[...] = jnp.zeros_like(dfb_ref)
            dfg_ref[...] = jnp.zeros_like(dfg_ref)
            dgt2_ref[...] = jnp.zeros_like(dgt2_ref)
            dg2_ref[...] = jnp.zeros_like(dg2_ref)
            dsc2_ref[...] = jnp.zeros_like(dsc2_ref)
            dsh2_ref[...] = jnp.zeros_like(dsh2_ref)
            loss_ref[...] = jnp.zeros_like(loss_ref)

        def cols_of(j):
            return pl.ds(j * FFN_BLK, FFN_BLK), pl.ds(D_FF + j * FFN_BLK, FFN_BLK)

        def wrap_down(last, before):
            return jnp.where(sublane == 0, pltpu.roll(before, 1, axis=0), pltpu.roll(last, 1, axis=0))

        def wrap_up(first, after):
            return jnp.where(sublane == 7, pltpu.roll(after, 7, axis=0), pltpu.roll(first, 7, axis=0))

        x2v = x2_ref[...]
        h2, h2_vjp = _mod_norm_vjp(x2v, g2[...], sc2_ref[...], sh2_ref[...])
        h2b = h2.astype(BF16)
        h2_before = _mod_norm(x2h_ref[...], g2[...], sc2_ref[...], sh2_ref[...]).astype(BF16)
        lhs = jnp.concatenate([_nn(pm_ref[...], h2b).astype(BF16), h2_before], axis=0)

        def up(j):
            cv, cg = cols_of(j)
            return _nn(lhs, wup_v[:, cv]), _nn(lhs, wup_v[:, cg])

        def conv(both, cols):
            cur = both[:t]
            u_s[:, cols] = cur.astype(BF16)
            before = jnp.where(tile > 0, both[t:], 0.0)
            w1 = wrap_down(cur[t - 8:], before)
            w2 = wrap_down(cur[t - 16:t - 8], pltpu.roll(before, 1, axis=0))
            back1 = jnp.concatenate([w1, cur[:t - 8]], axis=0)
            back2 = jnp.concatenate([w2, w1, cur[:t - 16]], axis=0)
            return (fb[:, cols] + cur * fw[pl.ds(2, 1), cols] + back1 * fw[pl.ds(1, 1), cols]
                    + back2 * fw[pl.ds(0, 1), cols])

        pm_t = pmt_ref[...]

        def to_natural_f32(a):
            hi = a.astype(BF16)
            rest = a - hi.astype(F32)
            mid = rest.astype(BF16)
            low = (rest - mid.astype(F32)).astype(BF16)
            return _nn(jnp.concatenate([pm_t, pm_t, pm_t], axis=1), jnp.concatenate([hi, mid, low], axis=0))

        o2 = jnp.zeros((t, D_MODEL), F32)
        ahead_uv = up(0)
        for j in range(n_blk):
            cv, cg = cols_of(j)
            both_v, both_g = ahead_uv
            if j + 1 < n_blk:
                ahead_uv = up(j + 1)
            val, gate = conv(both_v, cv), conv(both_g, cg)
            sig = jax.nn.sigmoid(gate)
            sil = gate * sig
            fb16 = (sil * val).astype(BF16)
            sil_s[:, cv] = sil
            vds_s[:, cv] = val * (sig + sil * (1.0 - sig))
            f_s[:, cv] = fb16
            o2 = o2 + _nn(fb16, wd_v[pl.ds(j * FFN_BLK, FFN_BLK), :])
        o2 = to_natural_f32(o2)

        gt2v = gt2_ref[...]
        x3 = x2v + gt2v * o2
        out, out_vjp = _rms_vjp(x3, fg[...])
        diff = out - tgt_ref[...]
        loss_ref[...] += jnp.zeros_like(loss_ref) + 0.5 * inv_d * jnp.sum(diff * diff)
        dx3, dfg = out_vjp(diff * inv_d)
        dfg_ref[...] += dfg
        dgt2_ref[...] += _colsum(dx3 * o2)
        do2b = _nn(pm_ref[...], (gt2v * dx3).astype(BF16)).astype(BF16)

        def conv_back(dd, cols):
            dfb_ref[:, cols] += _colsum(dd)
            nxt = carry[:, cols]
            w1 = wrap_up(dd[:8], nxt[:8])
            w2 = wrap_up(dd[8:16], nxt[8:])
            ahead = (dd, jnp.concatenate([dd[8:], w1], axis=0), jnp.concatenate([dd[16:], w1, w2], axis=0))
            carry[:, cols] = dd[:16]
            uv = u_s[:, cols].astype(F32)
            du = jnp.zeros((t, FFN_BLK), F32)
            for s in range(FFN_K):
                du = du + ahead[s] * fw[pl.ds(FFN_K - 1 - s, 1), cols]
                dfw_ref[pl.ds(FFN_K - 1 - s, 1), cols] += _colsum(ahead[s] * uv)
            du_s[:, cols] = du.astype(BF16)

        for j in range(n_blk):
            cv, cg = cols_of(j)
            rows = pl.ds(j * FFN_BLK, FFN_BLK)
            df = _nt(do2b, wd_v[rows, :])
            dwd_acc[rows, :] += _tn(f_s[:, cv], do2b)
            conv_back(df * sil_s[:, cv], cv)
            conv_back(df * vds_s[:, cv], cg)
        du16 = _nn(pm_t, du_s[...]).astype(BF16)
        du_ref[...] = du16
        dx2, dg2, dsc2, dsh2 = h2_vjp(_nt(du16, wup_v[...]))
        dx2_ref[...] = dx3 + dx2
        dg2_ref[...] += dg2
        dsc2_ref[...] += dsc2
        dsh2_ref[...] += dsh2

        @pl.when(i == n_tiles - 1)
        def _():
            cp = pltpu.make_async_copy(dwd_acc, dwd_hbm, sem.at[1])
            cp.start()
            wd_v[...] = dwd_acc[...].astype(BF16)
            cp16 = pltpu.make_async_copy(wd_v, dwd16_hbm, sem.at[2])
            cp16.start()
            cp.wait()
            cp16.wait()

    def rev(width):
        return pl.BlockSpec((t, width), lambda i: (n_tiles - 1 - i, 0))

    assert FFN_K == 3
    halo_spec = pl.BlockSpec((8, D_MODEL), lambda i: (jnp.maximum((n_tiles - 1 - i) * (t // 8) - 1, 0), 0))
    vec_spec = _full((1, D_MODEL))
    out_shape = [jax.ShapeDtypeStruct((seq, 2 * D_FF), BF16), jax.ShapeDtypeStruct((seq, D_MODEL), F32),
                 jax.ShapeDtypeStruct((FFN_K, 2 * D_FF), F32), jax.ShapeDtypeStruct((1, 2 * D_FF), F32),
                 jax.ShapeDtypeStruct((1, D_MODEL), F32), jax.ShapeDtypeStruct((1, D_MODEL), F32),
                 jax.ShapeDtypeStruct((1, D_MODEL), F32), jax.ShapeDtypeStruct((1, D_MODEL), F32),
                 jax.ShapeDtypeStruct((1, D_MODEL), F32),
                 jax.ShapeDtypeStruct((1, 128), F32), jax.ShapeDtypeStruct((D_FF, D_MODEL), F32),
                 jax.ShapeDtypeStruct((D_FF, D_MODEL), BF16)]
    return pl.pallas_call(
        body, grid=(n_tiles,), name="ffn",
        in_specs=[rev(D_MODEL), halo_spec, rev(D_MODEL), vec_spec, vec_spec, vec_spec, _full(ffn_w.shape),
                  _full(ffn_b.shape), _full(gt2.shape), _full(final_gain.shape), _full(to_inter.shape),
                  _full(to_natural.shape), ANY, ANY],
        out_specs=[rev(2 * D_FF), rev(D_MODEL), _full((FFN_K, 2 * D_FF)), _full((1, 2 * D_FF)), vec_spec, vec_spec,
                   vec_spec, vec_spec, vec_spec, _full((1, 128)), ANY, ANY],
        out_shape=out_shape,
        scratch_shapes=[pltpu.VMEM((D_MODEL, 2 * D_FF), BF16), pltpu.VMEM((D_FF, D_MODEL), BF16),
                        pltpu.VMEM((D_FF, D_MODEL), F32), pltpu.VMEM((FFN_HALO, 2 * D_FF), F32),
                        pltpu.VMEM((t, 2 * D_FF), BF16), pltpu.VMEM((t, D_FF), F32), pltpu.VMEM((t, D_FF), F32),
                        pltpu.VMEM((t, D_FF), BF16), pltpu.VMEM((t, 2 * D_FF), BF16),
                        pltpu.SemaphoreType.DMA((3 + N_SHARD,))],
        compiler_params=pltpu.CompilerParams(dimension_semantics=("arbitrary",), vmem_limit_bytes=FFN_VMEM_LIMIT_BYTES),
    )(x2, x2, target, norm2_gain, sc2, sh2, ffn_w, ffn_b, gt2, final_gain, to_inter, to_natural, w_up_g, w_down_g)


def _scatter_copies(src16, land, send_sems, recv_sems):
    x, y, c = _coords()
    h = src16.shape[1] // 2
    copies = []
    for f, flip in enumerate(PEER_FLIPS):
        tx, ty, tc = _flip(x, flip[0]), _flip(y, flip[1]), _flip(c, flip[2])
        copies.append(pltpu.make_async_remote_copy(
            src_ref=src16.at[2 * tx + ty, pl.ds(pl.multiple_of(tc * h, 16), h)], dst_ref=land.at[f],
            send_sem=send_sems.at[f], recv_sem=recv_sems.at[f], device_id=(tx, ty, tc), device_id_type=MESH))
    return copies


def _land_shape(src16):
    return jax.ShapeDtypeStruct((len(PEER_FLIPS), src16.shape[1] // 2, src16.shape[2]), BF16)


UP_TILE = 512


def _bwd_up(du, x2, norm2_gain, sc2, sh2, dwd16):
    seq = x2.shape[0]
    t = UP_TILE if seq % UP_TILE == 0 else TILE
    n_tiles = seq // t
    acc_shape = (N_SHARD, D_MODEL, UP_SHARD)

    def body(du_ref, x2_ref, g2, sc2_ref, sh2_ref, dwd16_hbm, dwup_hbm, dwup16_hbm, land_hbm,
             stage16, dwup_acc, sem, send_sems, recv_sems):
        i = pl.program_id(0)

        @pl.when(i == 0)
        def _():
            for cp in _scatter_copies(dwd16_hbm, land_hbm, send_sems, recv_sems):
                cp.start()
            dwup_acc[...] = jnp.zeros_like(dwup_acc)

        def write_back(k):
            return (pltpu.make_async_copy(dwup_acc.at[k], dwup_hbm.at[k], sem.at[k]),
                    pltpu.make_async_copy(stage16.at[k], dwup16_hbm.at[k], sem.at[N_SHARD + k]))

        h2b = _mod_norm(x2_ref[...], g2[...], sc2_ref[...], sh2_ref[...]).astype(BF16)
        for k in range(N_SHARD):
            dwup_acc[k] += _tn(h2b, du_ref[:, k * UP_SHARD:(k + 1) * UP_SHARD])

            @pl.when(i == n_tiles - 1)
            def _():
                stage16[k] = dwup_acc[k].astype(BF16)
                for cp in write_back(k):
                    cp.start()

        @pl.when(i == n_tiles - 1)
        def _():
            for k in range(N_SHARD):
                for cp in write_back(k):
                    cp.wait()
            for rc in _scatter_copies(dwd16_hbm, land_hbm, send_sems, recv_sems):
                rc.wait()

    def row(width):
        return pl.BlockSpec((t, width), lambda i: (i, 0))

    n_peer = len(PEER_FLIPS)
    return pl.pallas_call(
        body, grid=(n_tiles,), name="bwd_up",
        in_specs=[row(2 * D_FF), row(D_MODEL), _full((1, D_MODEL)), _full((1, D_MODEL)), _full((1, D_MODEL)), ANY],
        out_specs=[ANY, ANY, ANY],
        out_shape=[jax.ShapeDtypeStruct(acc_shape, F32), jax.ShapeDtypeStruct(acc_shape, BF16), _land_shape(dwd16)],
        scratch_shapes=[pltpu.VMEM(acc_shape, BF16), pltpu.VMEM(acc_shape, F32), pltpu.SemaphoreType.DMA((2 * N_SHARD,)),
                        pltpu.SemaphoreType.DMA((n_peer,)), pltpu.SemaphoreType.DMA((n_peer,))],
        compiler_params=_params(),
    )(du, x2, norm2_gain, sc2, sh2, dwd16)


def _bwd_mixer(dx2, x, z, a1, sp, yb, o1, vec, conv_w, wpair, wpair_t, causal_mask, w_in_g, w_out_g, dwup16):
    seq = x.shape[0]
    n_tiles = seq // TILE
    t = TILE
    names = ["norm1_gain", "sc1", "sh1", "gt1", "conv_ln_g", "conv_ln_b", "gm_ln_g", "gm_ln_b", "mix_out_gain"]
    vecs = [vec[k] for k in names]

    def body(dx2_ref, x_ref, z_ref, a1_ref, sp_ref, y_ref, o1_ref, g1, sc1, sh1, gt1, clg, clb, vg, vb, mg,
             cw, wp, wpt, mask_ref, win_hbm, wout_hbm, dwup16_hbm,
             gx_ref, dg1_ref, dsc1_ref, dsh1_ref, dgt1_ref, dcw_ref, dcb_ref, dclg_ref, dclb_ref, dvg_ref, dvb_ref,
             dmg_ref, dws_ref, dbs_ref, dwin_hbm, dwout_hbm, land_hbm, dwin16_hbm, dwout16_hbm,
             win_v, wout_v, dwin_acc, dwout_acc, carry, bank, dbs_acc, lwin, lwout, sem, send_sems, recv_sems,
             pair_send, pair_recv):
        i = pl.program_id(0)
        small = [dg1_ref, dsc1_ref, dsh1_ref, dgt1_ref, dcw_ref, dcb_ref, dclg_ref, dclb_ref, dvg_ref, dvb_ref,
                 dmg_ref, dws_ref, dbs_acc]

        @pl.when(i == 0)
        def _():
            for cp in _scatter_copies(dwup16_hbm, land_hbm, send_sems, recv_sems):
                cp.start()
            cps = [pltpu.make_async_copy(win_hbm, win_v, sem.at[0]),
                   pltpu.make_async_copy(wout_hbm, wout_v, sem.at[1])]
            for cp in cps:
                cp.start()
            for cp in cps:
                cp.wait()
            dwin_acc[...] = jnp.zeros_like(dwin_acc)
            dwout_acc[...] = jnp.zeros_like(dwout_acc)
            carry[...] = jnp.zeros_like(carry)
            for ref in small:
                ref[...] = jnp.zeros_like(ref)

        dx2v = dx2_ref[...]
        gt1v = gt1[...]
        dgt1_ref[...] += _colsum(dx2v * o1_ref[...])
        do1b = (gt1v * dx2v).astype(BF16)
        dy = _nt(do1b, wout_v[...])
        dwout_acc[...] += _tn(y_ref[...], do1b)

        mgv = mg[...]
        _, conv_vjp = _conv_branch_vjp(a1_ref[...], clg[...], clb[...], mgv[:, :D_HALF])
        da1, dclg, dclb, dmg_a = conv_vjp(dy[:, :D_HALF])
        dclg_ref[...] += dclg
        dclb_ref[...] += dclb
        gu = z_ref[:, 2 * D_HALF:3 * D_HALF]
        gv = z_ref[:, 3 * D_HALF:]
        spv = sp_ref[...]
        _, gate_vjp = _gate_branch_vjp(gu, spv, mgv[:, D_HALF:])
        dgu, dsp, dmg_g = gate_vjp(dy[:, D_HALF:])
        dmg_ref[...] += jnp.concatenate([dmg_a, dmg_g], axis=1)
        gvn, gv_vjp = _gv_norm_vjp(gv, vg[...], vb[...])
        gvnb = gvn.astype(BF16)
        dspb = dsp.astype(BF16)
        dgvn = _head_pair_matmul(wpt, dspb)
        dgv, dvg, dvb = gv_vjp(dgvn)
        dvg_ref[...] += dvg
        dvb_ref[...] += dvb
        lane = lax.broadcasted_iota(jnp.int32, (CHUNK, CHUNK), 1)
        dbs = jnp.zeros((CHUNK, D_HALF), F32)
        for n in range(t // CHUNK):
            rows = slice(n * CHUNK, (n + 1) * CHUNK)
            dbs = dbs + dsp[rows, :]
            for j in range(N_HEADS // 2):
                cols = slice(j * CHUNK, (j + 1) * CHUNK)
                blk = dspb[rows, cols]
                zero = jnp.zeros_like(blk)
                vblk = gvnb[rows, cols]
                dws_ref[2 * j] += _nt(jnp.where(lane < HEAD_DIM, blk, zero), vblk)
                dws_ref[2 * j + 1] += _nt(jnp.where(lane < HEAD_DIM, zero, blk), vblk)
        dbs_acc[...] += dbs

        h1, h1_vjp = _mod_norm_vjp(x_ref[...], g1[...], sc1[...], sh1[...])
        h1b = h1.astype(BF16)
        dh1 = jnp.zeros((t, D_MODEL), F32)
        for k, dzk in ((2, dgu), (3, dgv)):
            dzb = dzk.astype(BF16)
            dh1 = dh1 + _nt(dzb, win_v[k])
            dwin_acc[k] += _tn(h1b, dzb)

        ca = z_ref[:, :D_HALF]
        cg = z_ref[:, D_HALF:2 * D_HALF]
        sig = jax.nn.sigmoid(cg)
        a0 = ca * sig
        ext = jnp.concatenate([da1, carry[...]], axis=0)
        carry[...] = da1[:HALO]
        bank[0] = ext
        for b in range(1, 8):
            bank[b] = pltpu.roll(ext, t + HALO - b, axis=0)
        dcb_ref[...] += _colsum(da1)
        da0 = jnp.zeros((t, D_HALF), F32)
        for s in range(CONV_K):
            q, b = divmod(s, 8)
            shifted = bank[b, pl.ds(8 * q, t), :]
            da0 = da0 + shifted * cw[pl.ds(CONV_K - 1 - s, 1), :]
            dcw_ref[pl.ds(CONV_K - 1 - s, 1), :] += _colsum(shifted * a0)
        dca = da0 * sig
        dcg = da0 * ca * sig * (1.0 - sig)

        for k, dzk in ((0, dca), (1, dcg)):
            dzb = dzk.astype(BF16)
            dh1 = dh1 + _nt(dzb, win_v[k])
            dwin_acc[k] += _tn(h1b, dzb)
        dx, dg1, dsc1, dsh1 = h1_vjp(dh1)
        gx_ref[...] = dx2v + dx
        dg1_ref[...] += dg1
        dsc1_ref[...] += dsc1
        dsh1_ref[...] += dsh1

        @pl.when(i == n_tiles - 1)
        def _():
            for h in range(N_HEADS):
                dws_ref[h] = dws_ref[h] * mask_ref[...]
            head_of_lane = lax.broadcasted_iota(jnp.int32, (N_HEADS, D_HALF), 1) // HEAD_DIM
            pick = (head_of_lane == lax.broadcasted_iota(jnp.int32, (N_HEADS, D_HALF), 0)).astype(F32)
            dbs_ref[...] = lax.dot_general(pick, dbs_acc[...], NT_DIMS, precision=lax.Precision.HIGHEST,
                                           preferred_element_type=F32)
            for k in range(N_SHARD):
                win_v[k] = dwin_acc[k].astype(BF16)
            wout_v[...] = dwout_acc[...].astype(BF16)
            mx, my, mc = _coords()
            h_in, h_out = dwin_acc.shape[1] // 2, dwout_acc.shape[0] // (2 * N_SHARD)

            def in_rows(ref, k, which):
                return ref.at[k, pl.ds(pl.multiple_of(which * h_in, 16), h_in), :]

            def out_rows(ref, k, which):
                return ref.at[pl.ds(pl.multiple_of((2 * k + which) * h_out, 16), h_out), :]

            pairs = ((win_v, dwin_acc, lwin, in_rows, dwin_hbm, dwin16_hbm),
                     (wout_v, dwout_acc, lwout, out_rows, dwout_hbm, dwout16_hbm))
            swaps = [pltpu.make_async_remote_copy(
                src_ref=rows_of(v16, k, 1 - mc), dst_ref=land.at[k], send_sem=pair_send.at[w, k],
                recv_sem=pair_recv.at[w, k], device_id=(mx, my, 1 - mc), device_id_type=MESH)
                for w, (v16, _, land, rows_of, _, _) in enumerate(pairs) for k in range(N_SHARD)]
            for cp in swaps:
                cp.start()
            for cp in swaps:
                cp.wait()
            outs = []
            for w, (v16, acc, land, rows_of, half_hbm, half16_hbm) in enumerate(pairs):
                for k in range(N_SHARD):
                    total = rows_of(acc, k, mc)[...] + land[k].astype(F32)
                    rows_of(acc, k, 0)[...] = total
                    rows_of(v16, k, 0)[...] = total.astype(BF16)
                    outs.append(pltpu.make_async_copy(rows_of(acc, k, 0), half_hbm.at[k], sem.at[2 + 8 * w + k]))
                    outs.append(pltpu.make_async_copy(rows_of(v16, k, 0), half16_hbm.at[k], sem.at[6 + 8 * w + k]))
            for cp in outs:
                cp.start()
            for cp in outs:
                cp.wait()
            for rc in _scatter_copies(dwup16_hbm, land_hbm, send_sems, recv_sems):
                rc.wait()

    def rev(width):
        return pl.BlockSpec((t, width), lambda i: (n_tiles - 1 - i, 0))

    v1024 = jax.ShapeDtypeStruct((1, D_MODEL), F32)
    v512 = jax.ShapeDtypeStruct((1, D_HALF), F32)
    small_shapes = [v1024, v1024, v1024, v1024, jax.ShapeDtypeStruct((CONV_K, D_HALF), F32), v512, v512, v512, v512,
                    v512, v1024, jax.ShapeDtypeStruct((N_HEADS, CHUNK, CHUNK), F32),
                    jax.ShapeDtypeStruct((N_HEADS, CHUNK), F32)]
    n_peer = len(PEER_FLIPS)
    half_in = (N_SHARD, w_in_g.shape[1] // 2, w_in_g.shape[2])
    half_out = (N_SHARD, w_out_g.shape[0] // (2 * N_SHARD), w_out_g.shape[1])
    return pl.pallas_call(
        body, grid=(n_tiles,), name="bwd_mixer",
        in_specs=[rev(D_MODEL), rev(D_MODEL), rev(4 * D_HALF), rev(D_HALF), rev(D_HALF), rev(D_MODEL),
                  rev(D_MODEL)] + [_full(v.shape) for v in vecs]
        + [_full(conv_w.shape), _full(wpair.shape), _full(wpair_t.shape), _full(causal_mask.shape), ANY, ANY, ANY],
        out_specs=[rev(D_MODEL)] + [_full(s.shape) for s in small_shapes] + [ANY] * 5,
        out_shape=[jax.ShapeDtypeStruct((seq, D_MODEL), F32)] + small_shapes
        + [jax.ShapeDtypeStruct(half_in, F32), jax.ShapeDtypeStruct(half_out, F32), _land_shape(dwup16),
           jax.ShapeDtypeStruct(half_in, BF16), jax.ShapeDtypeStruct(half_out, BF16)],
        scratch_shapes=[pltpu.VMEM(w_in_g.shape, BF16), pltpu.VMEM(w_out_g.shape, BF16),
                        pltpu.VMEM(w_in_g.shape, F32), pltpu.VMEM(w_out_g.shape, F32),
                        pltpu.VMEM((HALO, D_HALF), F32), pltpu.VMEM((8, t + HALO, D_HALF), F32),
                        pltpu.VMEM((CHUNK, D_HALF), F32), pltpu.VMEM(half_in, BF16), pltpu.VMEM(half_out, BF16),
                        pltpu.SemaphoreType.DMA((2 + 4 * N_SHARD,)),
                        pltpu.SemaphoreType.DMA((n_peer,)), pltpu.SemaphoreType.DMA((n_peer,)),
                        pltpu.SemaphoreType.DMA((2, N_SHARD)), pltpu.SemaphoreType.DMA((2, N_SHARD))],
        compiler_params=_params(),
    )(dx2, x, z, a1, sp, yb, o1, *vecs, conv_w, wpair, wpair_t, causal_mask, w_in_g, w_out_g, dwup16)


def _gmlp_operands(gm_ws, gm_bs):
    mask = jnp.tril(jnp.ones((CHUNK, CHUNK), F32))
    ws = gm_ws * mask[None]
    wpair = ws.reshape(N_HEADS // 2, 2 * CHUNK, CHUNK).astype(BF16)
    wpair_t = jnp.swapaxes(ws, 1, 2).reshape(N_HEADS // 2, 2 * CHUNK, CHUNK).astype(BF16)
    bs_full = jnp.repeat(jnp.transpose(gm_bs), HEAD_DIM, axis=1)
    return wpair, wpair_t, bs_full, mask


def _local_step(x, target, mod, p, w_in_g, w_out_g, w_up_part, w_down_part):
    sh1, sc1, gt1, sh2, sc2, gt2 = [mod[:, k * D_MODEL:(k + 1) * D_MODEL] for k in range(6)]
    vec = dict(p, sh1=sh1, sc1=sc1, gt1=gt1, sh2=sh2, sc2=sc2, gt2=gt2)
    wpair, wpair_t, bs_full, mask = _gmlp_operands(p["gm_ws"], p["gm_bs"])

    (z, a1, sp, yb, o1, x2), (w_up_g, w_down_g) = _fwd_mixer(
        x, vec, p["conv_dw_w"], wpair, bs_full, w_in_g, w_out_g, [w_up_part, w_down_part])
    w_down_g = w_down_g.reshape(D_FF, D_MODEL)
    to_inter, to_natural = _interleave_matrices()
    du, dx2, d_ffn_w, d_ffn_b, d_fg, d_gt2, d_g2, d_sc2, d_sh2, loss, d_wd, d_wd16 = _ffn(
        x2, target, p["norm2_gain"], sc2, sh2, p["ffn_dw_w"], p["ffn_dw_b"], gt2, p["final_gain"], w_up_g, w_down_g,
        to_inter, to_natural)
    by_shard = (N_SHARD, -1, D_MODEL)
    d_wup, d_wup16, land_wd = _bwd_up(du, x2, p["norm2_gain"], sc2, sh2, d_wd16.reshape(by_shard))
    (gx, d_g1, d_sc1, d_sh1, d_gt1, d_cw, d_cb, d_clg, d_clb, d_vg, d_vb, d_mg, d_ws, d_bs, d_win, d_wout, land_wup,
     d_win16, d_wout16) = _bwd_mixer(dx2, x, z, a1, sp, yb, o1, vec, p["conv_dw_w"], wpair, wpair_t, mask, w_in_g,
                                     w_out_g, d_wup16)
    d_mod = _pack([d_sh1, d_sc1, d_gt1, d_sh2, d_sc2, d_gt2], 6).reshape(1, 6 * D_MODEL)
    grads = dict(norm1_gain=d_g1, conv_dw_w=d_cw, conv_dw_b=d_cb, conv_ln_g=d_clg, conv_ln_b=d_clb, gm_ln_g=d_vg,
                 gm_ln_b=d_vb, gm_ws=d_ws, gm_bs=d_bs, mix_out_gain=d_mg, norm2_gain=d_g2, ffn_dw_w=d_ffn_w,
                 ffn_dw_b=d_ffn_b, final_gain=d_fg, w_in=d_win, w_out=d_wout, w_up=d_wup, w_down=d_wd.reshape(by_shard))
    in_flight = dict(w_in16=d_win16, w_out16=d_wout16, land_w_up=land_wup, land_w_down=land_wd)
    return gx, grads, d_mod, loss, in_flight


MESH = pl.DeviceIdType.MESH
VMEM_SPEC = pl.BlockSpec(memory_space=pltpu.VMEM)
PEER_FLIPS = [(a, b, d) for a in (0, 1) for b in (0, 1) for d in (0, 1)][1:]
CHIP_FLIPS = [(1, 0), (0, 1), (1, 1)]


def _coords():
    return lax.axis_index("x"), lax.axis_index("y"), lax.axis_index("c")


def _flip(v, bit):
    return 1 - v if bit else v


def _rows8(block):
    return pl.ds(pl.multiple_of(8 * block, 8), 8)


def _ada_steps(c_ref, w_ref, b_ref, call_ref, mod_ref, cpad, modall, send_sems, recv_sems):
    x, y, c = _coords()
    me = 4 * x + 2 * y + c
    cpad[...] = jnp.zeros_like(cpad)
    cpad[pl.ds(0, 1), :] = c_ref[...]

    def gather_copy(j, flip):
        peer = (_flip(x, flip[0]), _flip(y, flip[1]), _flip(c, flip[2]))
        return pltpu.make_async_remote_copy(
            src_ref=cpad, dst_ref=call_ref.at[_rows8(me)], send_sem=send_sems.at[j], recv_sem=recv_sems.at[j],
            device_id=peer, device_id_type=MESH)

    def piece_copy(j, flip):
        tx, ty = _flip(x, flip[0]), _flip(y, flip[1])
        return pltpu.make_async_remote_copy(
            src_ref=modall.at[_rows8(4 * tx + 2 * ty + c)], dst_ref=mod_ref.at[_rows8(2 * x + y)],
            send_sem=send_sems.at[len(PEER_FLIPS) + j], recv_sem=recv_sems.at[len(PEER_FLIPS) + j],
            device_id=(tx, ty, c), device_id_type=MESH)

    copies = [gather_copy(j, f) for j, f in enumerate(PEER_FLIPS)]
    for cp in copies:
        cp.start()
    call_ref[_rows8(me), :] = cpad[...]

    def middle():
        for cp in copies:
            cp.wait_recv()
        for cp in copies:
            cp.wait_send()
        cv = call_ref[...]
        c_act = (cv * jax.nn.sigmoid(cv)).astype(BF16)
        modall[...] = _nn(c_act, w_ref[...].astype(BF16)) + b_ref[...]
        for j, f in enumerate(CHIP_FLIPS):
            piece_copy(j, f).start()
        mod_ref[_rows8(2 * x + y), :] = modall[_rows8(me), :]

    def finish():
        for j, f in enumerate(CHIP_FLIPS):
            piece_copy(j, f).wait_recv()
        for j, f in enumerate(CHIP_FLIPS):
            piece_copy(j, f).wait_send()

    return middle, finish


def _gather_weights(shards, filters, n_now, c_row, w_ada_sh, b_ada_sh):
    n = len(shards)
    nf = len(filters)
    ada_cols = w_ada_sh.shape[1]

    def body(*refs):
        ins, f_ins, ada_ins = refs[:n], refs[n:n + nf], refs[n + nf:n + nf + 3]
        refs = refs[n + nf + 3:]
        outs, f_outs, ada_outs = refs[:n], refs[n:n + nf], refs[n + nf:n + nf + 2]
        refs = refs[n + nf + 2:]
        stage = refs[:n]
        send_sems, recv_sems, local_sems, f_send_sems, f_recv_sems, cpad, modall, ada_send, ada_recv = refs[n:]
        ada_middle, ada_finish = _ada_steps(*ada_ins, *ada_outs, cpad, modall, ada_send, ada_recv)
        x, y, c = _coords()
        k = 2 * x + y
        sibling = (x, y, 1 - c)

        def filter_copy(w, j, slot):
            tx, ty = _flip(x, CHIP_FLIPS[j][0]), _flip(y, CHIP_FLIPS[j][1])
            return pltpu.make_async_remote_copy(
                src_ref=f_ins[w], dst_ref=f_outs[w].at[slot], send_sem=f_send_sems.at[w, j],
                recv_sem=f_recv_sems.at[w, j], device_id=(tx, ty, c), device_id_type=MESH)

        def half(w, which):
            h = shards[w].shape[0] // 2
            return pl.ds(pl.multiple_of(which * h, 16), h)

        def ici_copy(w, j, src, slot):
            tx, ty = _flip(x, CHIP_FLIPS[j][0]), _flip(y, CHIP_FLIPS[j][1])
            return pltpu.make_async_remote_copy(
                src_ref=src, dst_ref=outs[w].at[slot, half(w, c)], send_sem=send_sems.at[w, j],
                recv_sem=recv_sems.at[w, j], device_id=(tx, ty, c), device_id_type=MESH)

        def d2d_copy(w, j, slot, which):
            rows = outs[w].at[slot, half(w, which)]
            return pltpu.make_async_remote_copy(
                src_ref=rows, dst_ref=rows, send_sem=send_sems.at[w, len(CHIP_FLIPS) + j],
                recv_sem=recv_sems.at[w, len(CHIP_FLIPS) + j], device_id=sibling, device_id_type=MESH)

        def chip_of(j):
            return 2 * _flip(x, CHIP_FLIPS[j][0]) + _flip(y, CHIP_FLIPS[j][1])

        local, first, passed = [], [], []
        for w in range(nf):
            local.append(pltpu.make_async_copy(f_ins[w], f_outs[w].at[k], local_sems.at[n + w]))
            local[-1].start()
            for j in range(len(CHIP_FLIPS)):
                first.append(filter_copy(w, j, k))
                first[-1].start()
        for w in range(n):
            stage[w][...] = ins[w][...].astype(BF16)
            local.append(pltpu.make_async_copy(stage[w], outs[w].at[k], local_sems.at[w]))
            local[-1].start()
            if w < n_now:
                for j in range(len(CHIP_FLIPS)):
                    first.append(ici_copy(w, j, stage[w].at[half(w, c)], k))
                    first[-1].start()
        ada_middle()
        for w in range(nf):
            for j in range(len(CHIP_FLIPS)):
                filter_copy(w, j, chip_of(j)).wait_recv()
        for w in range(n_now):
            for j in range(len(CHIP_FLIPS)):
                ici_copy(w, j, stage[w].at[half(w, c)], chip_of(j)).wait_recv()
                passed.append(d2d_copy(w, j, chip_of(j), c))
                passed[-1].start()
        for w in range(n_now):
            for j in range(len(CHIP_FLIPS)):
                d2d_copy(w, j, chip_of(j), 1 - c).wait_recv()
        for cp in first + passed:
            cp.wait_send()
        for cp in local:
            cp.wait()
        ada_finish()

    sem_shape = (n_now, 2 * len(CHIP_FLIPS))
    f_sem_shape = (nf, len(CHIP_FLIPS))
    n_ada_sem = len(PEER_FLIPS) + len(CHIP_FLIPS)
    outs = pl.pallas_call(
        body, name="gather_weights",
        in_specs=[VMEM_SPEC] * (n + nf + 3), out_specs=[ANY] * (n + nf) + [VMEM_SPEC, VMEM_SPEC],
        out_shape=[jax.ShapeDtypeStruct((N_SHARD,) + s.shape, BF16) for s in shards]
        + [jax.ShapeDtypeStruct((N_SHARD,) + s.shape, F32) for s in filters]
        + [jax.ShapeDtypeStruct((8 * N_DEV, D_MODEL), F32), jax.ShapeDtypeStruct((8 * N_SHARD, ada_cols), F32)],
        scratch_shapes=[pltpu.VMEM(s.shape, BF16) for s in shards]
        + [pltpu.SemaphoreType.DMA(sem_shape), pltpu.SemaphoreType.DMA(sem_shape), pltpu.SemaphoreType.DMA((n + nf,)),
           pltpu.SemaphoreType.DMA(f_sem_shape), pltpu.SemaphoreType.DMA(f_sem_shape),
           pltpu.VMEM((8, D_MODEL), F32), pltpu.VMEM((8 * N_DEV, ada_cols), F32),
           pltpu.SemaphoreType.DMA((n_ada_sem,)), pltpu.SemaphoreType.DMA((n_ada_sem,))],
        compiler_params=pltpu.CompilerParams(vmem_limit_bytes=VMEM_LIMIT_BYTES),
    )(*shards, *filters, c_row, w_ada_sh, b_ada_sh)
    return outs[:n], outs[n:n + nf], outs[n + nf], outs[n + nf + 1]


def _final_comm(srcs16, small):
    n = len(srcs16)
    rows = small.shape[0]
    half = rows // 2
    quarter = half // 2

    def body(*refs):
        srcs, small_ref = refs[:n], refs[n]
        lands, small_out = refs[n + 1:2 * n + 1], refs[2 * n + 1]
        chip_sum, got_c, got_1, got_2, part, send_sems, recv_sems, small_send_sems, small_recv_sems = refs[2 * n + 2:]
        x, y, c = _coords()
        sibling = (x, y, 1 - c)
        mine = pl.ds(pl.multiple_of(c * half, 8), half)
        copies = []
        for w in range(n):
            for j, flip in enumerate(CHIP_FLIPS):
                tx, ty = _flip(x, flip[0]), _flip(y, flip[1])
                copies.append(pltpu.make_async_remote_copy(
                    src_ref=srcs[w].at[2 * tx + ty], dst_ref=lands[w].at[j], send_sem=send_sems.at[w, j],
                    recv_sem=recv_sems.at[w, j], device_id=(tx, ty, c), device_id_type=MESH))
        for cp in copies:
            cp.start()

        def exchange(pairs):
            rcs = [pltpu.make_async_remote_copy(
                src_ref=src, dst_ref=dst, send_sem=small_send_sems.at[k], recv_sem=small_recv_sems.at[k],
                device_id=peer, device_id_type=MESH) for k, src, dst, peer in pairs]
            for rc in rcs:
                rc.start()
            for rc in rcs:
                rc.wait()

        def quarter_rows(q):
            return pl.ds(pl.multiple_of(c * half + q * quarter, 8), quarter)

        along = ((1 - x, y, c), (x, 1 - y, c))
        exchange([(0, small_ref, got_c, sibling)])
        chip_sum[...] = small_ref[...] + got_c[...]
        exchange([(1 + q, chip_sum.at[quarter_rows(q)], got_1.at[q], along[q]) for q in range(2)])
        for q in range(2):
            part[q] = chip_sum[quarter_rows(q), :] + got_1[q]
        exchange([(3 + q, part.at[q], got_2.at[q], along[1 - q]) for q in range(2)])
        for q in range(2):
            small_out[quarter_rows(q), :] = part[q] + got_2[q]
        exchange([(5, small_out.at[mine], small_out.at[mine], sibling)])
        for cp in copies:
            cp.wait()

    n_chip = len(CHIP_FLIPS)
    quarter_shape = (2, quarter, small.shape[1])
    outs = pl.pallas_call(
        body, name="final_comm",
        in_specs=[ANY] * n + [VMEM_SPEC], out_specs=[ANY] * n + [VMEM_SPEC],
        out_shape=[jax.ShapeDtypeStruct((n_chip,) + a.shape[1:], BF16) for a in srcs16]
        + [jax.ShapeDtypeStruct(small.shape, F32)],
        scratch_shapes=[pltpu.VMEM(small.shape, F32), pltpu.VMEM(small.shape, F32), pltpu.VMEM(quarter_shape, F32),
                        pltpu.VMEM(quarter_shape, F32), pltpu.VMEM(quarter_shape, F32),
                        pltpu.SemaphoreType.DMA((n, n_chip)), pltpu.SemaphoreType.DMA((n, n_chip)),
                        pltpu.SemaphoreType.DMA((6,)), pltpu.SemaphoreType.DMA((6,))],
        compiler_params=pltpu.CompilerParams(vmem_limit_bytes=VMEM_LIMIT_BYTES),
    )(*srcs16, small)
    return outs[:n], outs[n]


ADD_CHUNKS = 4


def _scatter_sum_swap(pos, owns, lands):
    n = len(owns)

    def layout(own_shape, land_shape):
        peers, rows, cols = land_shape
        pick = 1 if own_shape[1] == 2 * rows else 0
        if cols % (128 * ADD_CHUNKS) == 0:
            width = cols // ADD_CHUNKS
            blk = (rows, width)
            return (pl.BlockSpec((1,) + blk, lambda i, p: (2 * p[0] + p[1], pick * p[2], i)),
                    pl.BlockSpec((peers,) + blk, lambda i, p: (0, 0, i)), (ADD_CHUNKS,) + blk,
                    lambda ref, which, j: ref.at[which, :, pl.ds(pl.multiple_of(j * width, 128), width)])
        height = rows // ADD_CHUNKS
        blk = (height, cols)
        return (pl.BlockSpec((1,) + blk, lambda i, p: (2 * p[0] + p[1], pick * p[2] * ADD_CHUNKS + i, 0)),
                pl.BlockSpec((peers,) + blk, lambda i, p: (0, i, 0)), (ADD_CHUNKS,) + blk,
                lambda ref, which, j: ref.at[which, pl.ds(pl.multiple_of(j * height, 8), height), :])

    layouts = [layout(o.shape, l.shape) for o, l in zip(owns, lands)]

    def body(pos_ref, *refs):
        outs, stages = refs[2 * n:3 * n], refs[3 * n:4 * n]
        local_sems, send_sems, recv_sems = refs[4 * n:]
        i = pl.program_id(0)
        x, y, c = _coords()

        def copies(idx, j):
            chunk_of = layouts[idx][3]
            return (pltpu.make_async_copy(stages[idx].at[j], chunk_of(outs[idx], c, j), local_sems.at[idx, j]),
                    pltpu.make_async_remote_copy(
                        src_ref=stages[idx].at[j], dst_ref=chunk_of(outs[idx], c, j), send_sem=send_sems.at[idx, j],
                        recv_sem=recv_sems.at[idx, j], device_id=(x, y, 1 - c), device_id_type=MESH))

        for idx in range(n):
            own, land = refs[idx], refs[n + idx]
            total = own[0]
            for f in range(land.shape[0]):
                total = total + land[f].astype(F32)
            stages[idx][i] = total
            for cp in copies(idx, i):
                cp.start()

        @pl.when(i == ADD_CHUNKS - 1)
        def _():
            for idx in range(n):
                for j in range(ADD_CHUNKS):
                    for cp in copies(idx, j):
                        cp.wait()

    sem_shape = (n, ADD_CHUNKS)
    return pl.pallas_call(
        body, name="scatter_sum_swap",
        grid_spec=pltpu.PrefetchScalarGridSpec(
            num_scalar_prefetch=1, grid=(ADD_CHUNKS,),
            in_specs=[s[0] for s in layouts] + [s[1] for s in layouts], out_specs=[ANY] * n,
            scratch_shapes=[pltpu.VMEM(s[2], F32) for s in layouts]
            + [pltpu.SemaphoreType.DMA(sem_shape), pltpu.SemaphoreType.DMA(sem_shape),
               pltpu.SemaphoreType.DMA(sem_shape)]),
        out_shape=[jax.ShapeDtypeStruct((2,) + l.shape[1:], F32) for l in lands],
        compiler_params=_params(),
    )(pos, *owns, *lands)


def _adamw_math(w, g, m, v):
    m = ADAM_B1 * m + (1.0 - ADAM_B1) * g
    v = ADAM_B2 * v + (1.0 - ADAM_B2) * jnp.square(g)
    m_hat = m / (1.0 - ADAM_B1 ** ADAM_STEP)
    v_hat = v / (1.0 - ADAM_B2 ** ADAM_STEP)
    delta = -ADAM_LR * (m_hat / (jnp.sqrt(v_hat) + ADAM_EPS) + ADAM_WD * w)
    return delta, m, v


def _adamw_group(ws, gs, ms, vs, n_steps):
    n = len(ws)

    def body(*refs):
        w_refs, g_refs, m_refs, v_refs = (refs[q * n:(q + 1) * n] for q in range(4))
        d_outs, m_outs, v_outs = (refs[(4 + q) * n:(5 + q) * n] for q in range(3))
        for idx in range(n):
            d_outs[idx][...], m_outs[idx][...], v_outs[idx][...] = _adamw_math(
                w_refs[idx][...], g_refs[idx][...], m_refs[idx][...], v_refs[idx][...])

    specs = [pl.BlockSpec((w.shape[0] // n_steps, w.shape[1]), lambda i: (i, 0)) for w in ws]
    shapes = [jax.ShapeDtypeStruct(w.shape, F32) for w in ws]
    outs = pl.pallas_call(
        body, grid=(n_steps,), name="adamw_projections", in_specs=specs * 4, out_specs=specs * 3,
        out_shape=shapes * 3, compiler_params=_params(),
    )(*ws, *gs, *ms, *vs)
    return outs[:n], outs[n:2 * n], outs[2 * n:]


def _adamw_many(ws, gs, ms, vs):
    n = len(ws)

    def body(*refs):
        w_refs, g_refs, m_refs, v_refs = (refs[q * n:(q + 1) * n] for q in range(4))
        d_outs, m_outs, v_outs = (refs[(4 + q) * n:(5 + q) * n] for q in range(3))
        for idx in range(n):
            d_outs[idx][...], m_outs[idx][...], v_outs[idx][...] = _adamw_math(
                w_refs[idx][...], g_refs[idx][...], m_refs[idx][...], v_refs[idx][...])

    shapes = [jax.ShapeDtypeStruct(w.shape, F32) for w in ws]
    outs = pl.pallas_call(
        body, name="adamw_small", in_specs=[VMEM_SPEC] * (4 * n), out_specs=[VMEM_SPEC] * (3 * n),
        out_shape=shapes * 3, compiler_params=pltpu.CompilerParams(vmem_limit_bytes=VMEM_LIMIT_BYTES),
    )(*ws, *gs, *ms, *vs)
    return outs[:n], outs[n:2 * n], outs[2 * n:]


def _adamw_ada(c_all16, dmod16, w, m, v, block_rows):
    rows, cols = w.shape

    def body(c_ref, dm_ref, w_ref, m_ref, v_ref, g_out, d_out, m_out, v_out):
        cv = c_ref[...]
        g = _tn((cv * jax.nn.sigmoid(cv)).astype(BF16), dm_ref[...].astype(BF16))
        g_out[...] = g
        d_out[...], m_out[...], v_out[...] = _adamw_math(w_ref[...], g, m_ref[...], v_ref[...])

    spec = pl.BlockSpec((block_rows, cols), lambda i: (i, 0))
    shape = jax.ShapeDtypeStruct((rows, cols), F32)
    return pl.pallas_call(
        body, grid=(rows // block_rows,), name="adamw_w_ada",
        in_specs=[pl.BlockSpec((16, block_rows), lambda i: (0, i)), _full(dmod16.shape), spec, spec, spec],
        out_specs=[spec] * 4, out_shape=[shape] * 4, compiler_params=_params(),
    )(c_all16, dmod16, w, m, v)


SMALL_REPLICATED = ["b_ada", "norm1_gain", "conv_dw_b", "conv_ln_g", "conv_ln_b", "gm_ln_g", "gm_ln_b", "gm_ws", "gm_bs",
                    "mix_out_gain", "norm2_gain", "ffn_dw_b", "final_gain"]
SMALL_SHARDED = ["conv_dw_w", "ffn_dw_w"]
PACK_ROWS = 256
WEIGHT_ORDER = ["w_ada", "b_ada", "norm1_gain", "w_in", "conv_dw_w", "conv_dw_b", "conv_ln_g", "conv_ln_b", "gm_ln_g",
                "gm_ln_b", "gm_ws", "gm_bs", "mix_out_gain", "w_out", "norm2_gain", "w_up", "ffn_dw_w", "ffn_dw_b",
                "w_down", "final_gain"]


def _pack(parts, rows):
    total = rows * D_MODEL
    flat, offset = None, 0
    for a in parts:
        piece = jnp.pad(a.reshape(-1), (offset, total - offset - a.size))
        flat = piece if flat is None else flat + piece
        offset += a.size
    return flat.reshape(rows, D_MODEL)


def _unpack(packed, shapes):
    flat = packed.reshape(-1)
    out, pos = [], 0
    for s in shapes:
        size = 1
        for d in s:
            size *= d
        out.append(flat[pos:pos + size].reshape(s))
        pos += size
    return out


def kernel(x, c, w_ada, b_ada, norm1_gain, w_in, conv_dw_w, conv_dw_b, conv_ln_g, conv_ln_b, gm_ln_g, gm_ln_b, gm_ws, gm_bs, mix_out_gain, w_out, norm2_gain, w_up, ffn_dw_w, ffn_dw_b, w_down, final_gain, loss_target, m_w_ada, m_b_ada, m_norm1_gain, m_w_in, m_conv_dw_w, m_conv_dw_b, m_conv_ln_g, m_conv_ln_b, m_gm_ln_g, m_gm_ln_b, m_gm_ws, m_gm_bs, m_mix_out_gain, m_w_out, m_norm2_gain, m_w_up, m_ffn_dw_w, m_ffn_dw_b, m_w_down, m_final_gain, v_w_ada, v_b_ada, v_norm1_gain, v_w_in, v_conv_dw_w, v_conv_dw_b, v_conv_ln_g, v_conv_ln_b, v_gm_ln_g, v_gm_ln_b, v_gm_ws, v_gm_bs, v_mix_out_gain, v_w_out, v_norm2_gain, v_w_up, v_ffn_dw_w, v_ffn_dw_b, v_w_down, v_final_gain):
    weights = dict(w_ada=w_ada, b_ada=b_ada, norm1_gain=norm1_gain, w_in=w_in, conv_dw_w=conv_dw_w, conv_dw_b=conv_dw_b,
                   conv_ln_g=conv_ln_g, conv_ln_b=conv_ln_b, gm_ln_g=gm_ln_g, gm_ln_b=gm_ln_b, gm_ws=gm_ws, gm_bs=gm_bs,
                   mix_out_gain=mix_out_gain, w_out=w_out, norm2_gain=norm2_gain, w_up=w_up, ffn_dw_w=ffn_dw_w,
                   ffn_dw_b=ffn_dw_b, w_down=w_down, final_gain=final_gain)
    mom1 = dict(w_ada=m_w_ada, b_ada=m_b_ada, norm1_gain=m_norm1_gain, w_in=m_w_in, conv_dw_w=m_conv_dw_w,
                conv_dw_b=m_conv_dw_b, conv_ln_g=m_conv_ln_g, conv_ln_b=m_conv_ln_b, gm_ln_g=m_gm_ln_g, gm_ln_b=m_gm_ln_b,
                gm_ws=m_gm_ws, gm_bs=m_gm_bs, mix_out_gain=m_mix_out_gain, w_out=m_w_out, norm2_gain=m_norm2_gain,
                w_up=m_w_up, ffn_dw_w=m_ffn_dw_w, ffn_dw_b=m_ffn_dw_b, w_down=m_w_down, final_gain=m_final_gain)
    mom2 = dict(w_ada=v_w_ada, b_ada=v_b_ada, norm1_gain=v_norm1_gain, w_in=v_w_in, conv_dw_w=v_conv_dw_w,
                conv_dw_b=v_conv_dw_b, conv_ln_g=v_conv_ln_g, conv_ln_b=v_conv_ln_b, gm_ln_g=v_gm_ln_g, gm_ln_b=v_gm_ln_b,
                gm_ws=v_gm_ws, gm_bs=v_gm_bs, mix_out_gain=v_mix_out_gain, w_out=v_w_out, norm2_gain=v_norm2_gain,
                w_up=v_w_up, ffn_dw_w=v_ffn_dw_w, ffn_dw_b=v_ffn_dw_b, w_down=v_w_down, final_gain=v_final_gain)
    shard = 2 * lax.axis_index("x") + lax.axis_index("y")
    me = 2 * shard + lax.axis_index("c")

    ada_cols = w_ada.shape[2]
    b_ada_sh = lax.dynamic_slice(b_ada, (0, shard * ada_cols), (1, ada_cols))
    (w_in_g, w_out_g, w_up_part, w_down_part), (conv_w_g, ffn_w_g), c_all64, mod32 = _gather_weights(
        [w_in[0], w_out[0], w_up[0], w_down[0]], [conv_dw_w[0], ffn_dw_w[0]], 2, c, w_ada[0], b_ada_sh)
    c_all = c_all64[::8]
    mod = mod32[::8].reshape(1, N_SHARD * ada_cols)
    conv_w_full = jnp.transpose(conv_w_g, (1, 0, 2)).reshape(CONV_K, D_HALF)
    ffn_w_full = jnp.transpose(ffn_w_g, (1, 0, 2)).reshape(FFN_K, 2 * D_FF)

    p = dict(norm1_gain=norm1_gain, conv_dw_w=conv_w_full, conv_dw_b=conv_dw_b, conv_ln_g=conv_ln_g,
             conv_ln_b=conv_ln_b, gm_ln_g=gm_ln_g, gm_ln_b=gm_ln_b, gm_ws=gm_ws[0], gm_bs=gm_bs[0],
             mix_out_gain=mix_out_gain, norm2_gain=norm2_gain, ffn_dw_w=ffn_w_full, ffn_dw_b=ffn_dw_b,
             final_gain=final_gain[None])
    grad_x, g, d_mod, loss, in_flight = _local_step(
        x[0], loss_target[0], mod, p, w_in_g, w_out_g.reshape(D_MODEL, D_MODEL), w_up_part, w_down_part)

    n_mod = d_mod.shape[1]
    dmod_rows = lax.dynamic_update_slice(jnp.zeros((N_DEV, n_mod), F32), d_mod, (me, 0))
    g["b_ada"] = d_mod
    small = _pack([g[k] for k in SMALL_REPLICATED] + [g[k] for k in SMALL_SHARDED] + [dmod_rows, loss[0, :1]], PACK_ROWS)
    (land_w_in, land_w_out), small = _final_comm([in_flight["w_in16"], in_flight["w_out16"]], small)
    pos = jnp.stack(_coords()).astype(jnp.int32)
    full = _scatter_sum_swap(pos, [g["w_in"], g["w_out"], g["w_up"], g["w_down"]],
                             [land_w_in, land_w_out, in_flight["land_w_up"], in_flight["land_w_down"]])
    grads = dict(w_in=full[0].reshape(w_in.shape[1:]), w_out=full[1].reshape(w_out.shape[1:]),
                 w_up=full[2].reshape(w_up.shape[1:]), w_down=full[3].reshape(w_down.shape[1:]))

    small_shapes = ([weights[k].shape for k in SMALL_REPLICATED] + [(CONV_K, D_HALF), (FFN_K, 2 * D_FF)]
                    + [(N_DEV, n_mod), (1,)])
    *small_grads, conv_w_grad, ffn_w_grad, dmod_all, loss_sum = _unpack(small, small_shapes)
    grads.update(zip(SMALL_REPLICATED, small_grads))
    grads["conv_dw_w"] = lax.dynamic_slice(conv_w_grad, (0, shard * conv_dw_w.shape[2]), conv_dw_w.shape[1:])[None]
    grads["ffn_dw_w"] = lax.dynamic_slice(ffn_w_grad, (0, shard * ffn_dw_w.shape[2]), ffn_dw_w.shape[1:])[None]

    delta, new_m, new_v = {}, {}, {}
    projections = ["w_in", "w_out", "w_up", "w_down"]
    group_out = _adamw_group([weights[k][0] for k in projections], [grads[k] for k in projections],
                             [mom1[k][0] for k in projections], [mom2[k][0] for k in projections], n_steps=4)
    for d, arrs in zip((delta, new_m, new_v), group_out):
        d.update({k: a[None] for k, a in zip(projections, arrs)})
    for k in projections:
        grads[k] = grads[k][None]
    dmod_sh = lax.dynamic_slice(dmod_all, (0, shard * ada_cols), (N_DEV, ada_cols))
    pad8 = ((0, 16 - N_DEV), (0, 0))
    grads["w_ada"], delta["w_ada"], new_m["w_ada"], new_v["w_ada"] = [a[None] for a in _adamw_ada(
        jnp.pad(c_all, pad8), jnp.pad(dmod_sh, pad8), w_ada[0], m_w_ada[0], v_w_ada[0], 256)]
    small_names = SMALL_REPLICATED + SMALL_SHARDED

    def two_d(a):
        return a.reshape(1, -1) if a.ndim == 1 else a

    small_out = _adamw_many(*[[two_d(d[k]) for k in small_names] for d in (weights, grads, mom1, mom2)])
    for d, arrs in zip((delta, new_m, new_v), small_out):
        d.update({k: a.reshape(weights[k].shape) for k, a in zip(small_names, arrs)})

    return (loss_sum.reshape(()), grad_x[None], *[grads[k] for k in WEIGHT_ORDER], *[delta[k] for k in WEIGHT_ORDER],
            *[new_m[k] for k in WEIGHT_ORDER], *[new_v[k] for k in WEIGHT_ORDER])
```

```python
import jax
import jax.numpy as jnp
from jax import lax
from jax.experimental import pallas as pl
from jax.experimental.pallas import tpu as pltpu

F32 = jnp.float32
BF16 = jnp.bfloat16

D_MODEL = 1024
D_HALF = 512
D_FF = 2816
CONV_K = 31
FFN_K = 3
CHUNK = 128
N_HEADS = 8
HEAD_DIM = 64
N_SHARD = 4
N_DEV = 8
RMS_EPS = 1e-6
LN_EPS = 1e-5
ADAM_LR, ADAM_B1, ADAM_B2, ADAM_EPS, ADAM_WD, ADAM_STEP = 0.001, 0.9, 0.999, 1e-08, 0.01, 10

TILE = 256
HALO = 32
FFN_HALO = 16
FFN_BLK = 256
UP_SHARD = 2 * D_FF // N_SHARD
VMEM_LIMIT_BYTES = 56 * 1024 * 1024
FFN_VMEM_LIMIT_BYTES = 58 * 1024 * 1024

ANY = pl.BlockSpec(memory_space=pl.ANY)
NT_DIMS = (((1,), (1,)), ((), ()))
TN_DIMS = (((0,), (0,)), ((), ()))


def _full(shape):
    return pl.BlockSpec(shape, lambda i: (0,) * len(shape))


def _nn(a, b):
    return jnp.dot(a, b, preferred_element_type=F32)


def _nt(a, b):
    return lax.dot_general(a, b, NT_DIMS, preferred_element_type=F32)


def _tn(a, b):
    return lax.dot_general(a, b, TN_DIMS, preferred_element_type=F32)


def _colsum(a):
    return jnp.sum(a, axis=0, keepdims=True)


def _params(semantics=("arbitrary",)):
    return pltpu.CompilerParams(dimension_semantics=semantics, vmem_limit_bytes=VMEM_LIMIT_BYTES)


def _rms(v, gain):
    return v * lax.rsqrt(jnp.mean(v * v, axis=-1, keepdims=True) + RMS_EPS) * gain


def _layer_norm(v, gain, bias):
    mu = jnp.mean(v, axis=-1, keepdims=True)
    var = jnp.mean(jnp.square(v - mu), axis=-1, keepdims=True)
    return (v - mu) * lax.rsqrt(var + LN_EPS) * gain + bias


def _mod_norm(v, gain, scale, shift):
    return _rms(v, gain) * (1.0 + scale) + shift


def _conv_branch(a1, ln_g, ln_b, out_gain):
    a2 = _layer_norm(a1, ln_g, ln_b)
    return _rms(a2 * jax.nn.sigmoid(a2), out_gain)


def _gate_branch(gu, sp, out_gain):
    return _rms(jax.nn.gelu(gu) * sp, out_gain)


def _gv_norm(gv, ln_g, ln_b):
    return _layer_norm(jax.nn.gelu(gv), ln_g, ln_b)


def _rms_parts(v):
    r = lax.rsqrt(jnp.mean(v * v, axis=-1, keepdims=True) + RMS_EPS)
    return v * r, r


def _rms_back(dn, n, r):
    return r * (dn - n * jnp.mean(dn * n, axis=-1, keepdims=True))


def _ln_parts(v):
    mu = jnp.mean(v, axis=-1, keepdims=True)
    rs = lax.rsqrt(jnp.mean(jnp.square(v - mu), axis=-1, keepdims=True) + LN_EPS)
    return (v - mu) * rs, rs


def _ln_back(dn, n, rs):
    return rs * (dn - jnp.mean(dn, axis=-1, keepdims=True) - n * jnp.mean(dn * n, axis=-1, keepdims=True))


GELU_C = 0.7978845608028654
GELU_A = 0.044715


def _gelu_parts(v):
    v2 = v * v
    th = jnp.tanh(GELU_C * (v + GELU_A * (v2 * v)))
    cdf = 0.5 * (1.0 + th)
    return v * cdf, cdf + (0.5 * GELU_C) * v * (1.0 - th * th) * (1.0 + (3.0 * GELU_A) * v2)


def _rms_vjp(v, gain):
    n, r = _rms_parts(v)
    return n * gain, lambda dy: (_rms_back(dy * gain, n, r), _colsum(dy * n))


def _mod_norm_vjp(v, gain, scale, shift):
    n, r = _rms_parts(v)

    def back(dy):
        q = _colsum(dy * n)
        return _rms_back(dy * (gain * (1.0 + scale)), n, r), q * (1.0 + scale), q * gain, _colsum(dy)

    return n * gain * (1.0 + scale) + shift, back


def _conv_branch_vjp(a1, ln_g, ln_b, out_gain):
    n1, rs1 = _ln_parts(a1)
    a2 = n1 * ln_g + ln_b
    s = jax.nn.sigmoid(a2)
    a3 = a2 * s
    n3, r3 = _rms_parts(a3)

    def back(dy):
        da2 = _rms_back(dy * out_gain, n3, r3) * (s + a3 * (1.0 - s))
        return _ln_back(da2 * ln_g, n1, rs1), _colsum(da2 * n1), _colsum(da2), _colsum(dy * n3)

    return n3 * out_gain, back


def _gate_branch_vjp(gu, sp, out_gain):
    ge, dge = _gelu_parts(gu)
    n, r = _rms_parts(ge * sp)

    def back(dy):
        dg = _rms_back(dy * out_gain, n, r)
        return dg * sp * dge, dg * ge, _colsum(dy * n)

    return n * out_gain, back


def _gv_norm_vjp(gv, ln_g, ln_b):
    ge, dge = _gelu_parts(gv)
    n, rs = _ln_parts(ge)
    return n * ln_g + ln_b, lambda dy: (_ln_back(dy * ln_g, n, rs) * dge, _colsum(dy * n), _colsum(dy))


def _head_pair_matmul(wp_ref, v):
    lane = lax.broadcasted_iota(jnp.int32, (CHUNK, CHUNK), 1)
    rows = []
    for n in range(v.shape[0] // CHUNK):
        cols = []
        for j in range(N_HEADS // 2):
            r = _nn(wp_ref[j], v[n * CHUNK:(n + 1) * CHUNK, j * CHUNK:(j + 1) * CHUNK])
            cols.append(jnp.where(lane < HEAD_DIM, r[:CHUNK], r[CHUNK:]))
        rows.append(jnp.concatenate(cols, axis=1))
    return jnp.concatenate(rows, axis=0)


def _tile_bias(bs, tokens):
    return jnp.concatenate([bs] * (tokens // CHUNK), axis=0)


FORWARD_LEAD = 8


def _fwd_mixer(x, vec, conv_w, wpair, bs_full, w_in_g, w_out_g, late_parts):
    seq = x.shape[0]
    n_tiles = seq // TILE
    t = TILE
    n_late = len(late_parts)
    forward_step = max(n_tiles - FORWARD_LEAD, 0)
    names = ["norm1_gain", "sc1", "sh1", "gt1", "conv_dw_b", "conv_ln_g", "conv_ln_b", "gm_ln_g", "gm_ln_b",
             "mix_out_gain"]
    vecs = [vec[k] for k in names]

    def body(x_ref, g1, sc1, sh1, gt1, cb, clg, clb, vg, vb, mg, cw, wp, bs, win_hbm, wout_hbm, *rest):
        late = rest[n_late:2 * n_late]
        z_ref, a1_ref, sp_ref, y_ref, o1_ref, x2_ref = rest[2 * n_late:2 * n_late + 6]
        win_v, wout_v, halo, bank, sem, send_sems, recv_sems = rest[2 * n_late + 6:]
        i = pl.program_id(0)
        mx, my, mc = _coords()
        shard = 2 * mx + my

        def half(w, which):
            h = late[w].shape[1] // 2
            return pl.ds(pl.multiple_of(which * h, 16), h)

        def chip_of(j):
            return 2 * _flip(mx, CHIP_FLIPS[j][0]) + _flip(my, CHIP_FLIPS[j][1])

        def ici_copy(w, j, slot):
            rows = late[w].at[slot, half(w, mc)]
            return pltpu.make_async_remote_copy(
                src_ref=rows, dst_ref=rows, send_sem=send_sems.at[w, j], recv_sem=recv_sems.at[w, j],
                device_id=(_flip(mx, CHIP_FLIPS[j][0]), _flip(my, CHIP_FLIPS[j][1]), mc), device_id_type=MESH)

        def d2d_copy(w, j, which):
            rows = late[w].at[chip_of(j), half(w, which)]
            return pltpu.make_async_remote_copy(
                src_ref=rows, dst_ref=rows, send_sem=send_sems.at[w, len(CHIP_FLIPS) + j],
                recv_sem=recv_sems.at[w, len(CHIP_FLIPS) + j], device_id=(mx, my, 1 - mc), device_id_type=MESH)

        pairs = [(w, j) for w in range(n_late) for j in range(len(CHIP_FLIPS))]

        @pl.when(i == 0)
        def _():
            for w, j in pairs:
                ici_copy(w, j, shard).start()
            cps = [pltpu.make_async_copy(win_hbm, win_v, sem.at[0]),
                   pltpu.make_async_copy(wout_hbm, wout_v, sem.at[1])]
            for cp in cps:
                cp.start()
            for cp in cps:
                cp.wait()
            halo[...] = jnp.zeros_like(halo)

        @pl.when(i == forward_step)
        def _():
            for w, j in pairs:
                ici_copy(w, j, chip_of(j)).wait_recv()
                d2d_copy(w, j, mc).start()

        xv = x_ref[...]
        h1b = _mod_norm(xv, g1[...], sc1[...], sh1[...]).astype(BF16)
        zs = [_nn(h1b, win_v[k]) for k in range(N_SHARD)]
        for k in range(N_SHARD):
            z_ref[:, k * D_HALF:(k + 1) * D_HALF] = zs[k]
        ca, cg, gu, gv = zs
        a0 = ca * jax.nn.sigmoid(cg)
        ext = jnp.concatenate([halo[...], a0], axis=0)
        halo[...] = a0[t - HALO:]
        bank[0] = ext
        for b in range(1, 8):
            bank[b] = pltpu.roll(ext, b, axis=0)
        a1 = jnp.zeros((t, D_HALF), F32) + cb[...]
        for s in range(CONV_K):
            q, b = divmod(s, 8)
            a1 = a1 + bank[b, pl.ds(HALO - 8 * q, t), :] * cw[pl.ds(CONV_K - 1 - s, 1), :]
        a1_ref[...] = a1
        mgv = mg[...]
        ya = _conv_branch(a1, clg[...], clb[...], mgv[:, :D_HALF])
        gvn = _gv_norm(gv, vg[...], vb[...]).astype(BF16)
        sp = _head_pair_matmul(wp, gvn) + _tile_bias(bs[...], t)
        sp_ref[...] = sp
        yg = _gate_branch(gu, sp, mgv[:, D_HALF:])
        yb = jnp.concatenate([ya, yg], axis=1).astype(BF16)
        y_ref[...] = yb
        o1 = _nn(yb, wout_v[...])
        o1_ref[...] = o1
        x2_ref[...] = xv + gt1[...] * o1

        @pl.when(i == n_tiles - 1)
        def _():
            for w, j in pairs:
                d2d_copy(w, j, 1 - mc).wait_recv()
            for w, j in pairs:
                ici_copy(w, j, shard).wait_send()
                d2d_copy(w, j, mc).wait_send()

    def row(width):
        return pl.BlockSpec((t, width), lambda i: (i, 0))

    out_shape = [jax.ShapeDtypeStruct((seq, 4 * D_HALF), F32), jax.ShapeDtypeStruct((seq, D_HALF), F32),
                 jax.ShapeDtypeStruct((seq, D_HALF), F32), jax.ShapeDtypeStruct((seq, D_MODEL), BF16),
                 jax.ShapeDtypeStruct((seq, D_MODEL), F32), jax.ShapeDtypeStruct((seq, D_MODEL), F32)]
    n_in = 1 + len(vecs) + 3 + 2
    sem_shape = (n_late, 2 * len(CHIP_FLIPS))
    outs = pl.pallas_call(
        body, grid=(n_tiles,), name="fwd_mixer",
        in_specs=[row(D_MODEL)] + [_full(v.shape) for v in vecs]
        + [_full(conv_w.shape), _full(wpair.shape), _full(bs_full.shape), ANY, ANY] + [ANY] * n_late,
        out_specs=[ANY] * n_late + [row(4 * D_HALF), row(D_HALF), row(D_HALF), row(D_MODEL), row(D_MODEL),
                                    row(D_MODEL)],
        out_shape=[jax.ShapeDtypeStruct(a.shape, a.dtype) for a in late_parts] + out_shape,
        input_output_aliases={n_in + w: w for w in range(n_late)},
        scratch_shapes=[pltpu.VMEM(w_in_g.shape, BF16), pltpu.VMEM(w_out_g.shape, BF16),
                        pltpu.VMEM((HALO, D_HALF), F32), pltpu.VMEM((8, t + HALO, D_HALF), F32),
                        pltpu.SemaphoreType.DMA((2,)), pltpu.SemaphoreType.DMA(sem_shape),
                        pltpu.SemaphoreType.DMA(sem_shape)],
        compiler_params=_params(),
    )(x, *vecs, conv_w, wpair, bs_full, w_in_g, w_out_g, *late_parts)
    return outs[n_late:], outs[:n_late]


def _interleave_matrices():
    row = jnp.arange(TILE)
    token_of_row = (row % 8) * (TILE // 8) + row // 8
    to_inter = (token_of_row[:, None] == row[None, :]).astype(BF16)
    return to_inter, jnp.transpose(to_inter)


def _ffn(x2, target, norm2_gain, sc2, sh2, ffn_w, ffn_b, gt2, final_gain, w_up_g, w_down_g, to_inter, to_natural):
    seq = x2.shape[0]
    n_tiles = seq // TILE
    t = TILE
    n_blk = D_FF // FFN_BLK
    inv_d = 1.0 / D_MODEL

    def body(x2_ref, x2h_ref, tgt_ref, g2, sc2_ref, sh2_ref, fw, fb, gt2_ref, fg, pm_ref, pmt_ref, wup_hbm, wd_hbm,
             du_ref, dx2_ref, dfw_ref, dfb_ref, dfg_ref, dgt2_ref, dg2_ref, dsc2_ref, dsh2_ref, loss_ref, dwd_hbm,
             dwd16_hbm, wup_v, wd_v, dwd_acc, carry, u_s, sil_s, vds_s, f_s, du_s, sem):
        i = pl.program_id(0)
        tile = n_tiles - 1 - i
        sublane = lax.broadcasted_iota(jnp.int32, (8, FFN_BLK), 0)

        @pl.when(i == 0)
        def _():
            cps = [pltpu.make_async_copy(wd_hbm, wd_v, sem.at[0])]
            cps += [pltpu.make_async_copy(wup_hbm.at[k], wup_v.at[:, pl.ds(k * UP_SHARD, UP_SHARD)], sem.at[3 + k])
                    for k in range(N_SHARD)]
            for cp in cps:
                cp.start()
            for cp in cps:
                cp.wait()
            dwd_acc[...] = jnp.zeros_like(dwd_acc)
            carry[...] = jnp.zeros_like(carry)
            dfw_ref[...] = jnp.zeros_like(dfw_ref)
            dfb_ref[...] = jnp.zeros_like(dfb_ref)
            dfg_ref[...] = jnp.zeros_like(dfg_ref)
            dgt2_ref[...] = jnp.zeros_like(dgt2_ref)
            dg2_ref[...] = jnp.zeros_like(dg2_ref)
            dsc2_ref[...] = jnp.zeros_like(dsc2_ref)
            dsh2_ref[...] = jnp.zeros_like(dsh2_ref)
            loss_ref[...] = jnp.zeros_like(loss_ref)

        def cols_of(j):
            return pl.ds(j * FFN_BLK, FFN_BLK), pl.ds(D_FF + j * FFN_BLK, FFN_BLK)

        def wrap_down(last, before):
            return jnp.where(sublane == 0, pltpu.roll(before, 1, axis=0), pltpu.roll(last, 1, axis=0))

        def wrap_up(first, after):
            return jnp.where(sublane == 7, pltpu.roll(after, 7, axis=0), pltpu.roll(first, 7, axis=0))

        x2v = x2_ref[...]
        h2, h2_vjp = _mod_norm_vjp(x2v, g2[...], sc2_ref[...], sh2_ref[...])
        h2b = h2.astype(BF16)
        h2_before = _mod_norm(x2h_ref[...], g2[...], sc2_ref[...], sh2_ref[...]).astype(BF16)
        lhs = jnp.concatenate([_nn(pm_ref[...], h2b).astype(BF16), h2_before], axis=0)

        def up(j):
            cv, cg = cols_of(j)
            return _nn(lhs, wup_v[:, cv]), _nn(lhs, wup_v[:, cg])

        def conv(both, cols):
            cur = both[:t]
            u_s[:, cols] = cur.astype(BF16)
            before = jnp.where(tile > 0, both[t:], 0.0)
            w1 = wrap_down(cur[t - 8:], before)
            w2 = wrap_down(cur[t - 16:t - 8], pltpu.roll(before, 1, axis=0))
            back1 = jnp.concatenate([w1, cur[:t - 8]], axis=0)
            back2 = jnp.concatenate([w2, w1, cur[:t - 16]], axis=0)
            return (fb[:, cols] + cur * fw[pl.ds(2, 1), cols] + back1 * fw[pl.ds(1, 1), cols]
                    + back2 * fw[pl.ds(0, 1), cols])

        pm_t = pmt_ref[...]

        def to_natural_f32(a):
            hi = a.astype(BF16)
            rest = a - hi.astype(F32)
            mid = rest.astype(BF16)
            low = (rest - mid.astype(F32)).astype(BF16)
            return _nn(jnp.concatenate([pm_t, pm_t, pm_t], axis=1), jnp.concatenate([hi, mid, low], axis=0))

        o2 = jnp.zeros((t, D_MODEL), F32)
        ahead_uv = up(0)
        for j in range(n_blk):
            cv, cg = cols_of(j)
            both_v, both_g = ahead_uv
            if j + 1 < n_blk:
                ahead_uv = up(j + 1)
            val, gate = conv(both_v, cv), conv(both_g, cg)
            sig = jax.nn.sigmoid(gate)
            sil = gate * sig
            fb16 = (sil * val).astype(BF16)
            sil_s[:, cv] = sil
            vds_s[:, cv] = val * (sig + sil * (1.0 - sig))
            f_s[:, cv] = fb16
            o2 = o2 + _nn(fb16, wd_v[pl.ds(j * FFN_BLK, FFN_BLK), :])
        o2 = to_natural_f32(o2)

        gt2v = gt2_ref[...]
        x3 = x2v + gt2v * o2
        out, out_vjp = _rms_vjp(x3, fg[...])
        diff = out - tgt_ref[...]
        loss_ref[...] += jnp.zeros_like(loss_ref) + 0.5 * inv_d * jnp.sum(diff * diff)
        dx3, dfg = out_vjp(diff * inv_d)
        dfg_ref[...] += dfg
        dgt2_ref[...] += _colsum(dx3 * o2)
        do2b = _nn(pm_ref[...], (gt2v * dx3).astype(BF16)).astype(BF16)

        def conv_back(dd, cols):
            dfb_ref[:, cols] += _colsum(dd)
            nxt = carry[:, cols]
            w1 = wrap_up(dd[:8], nxt[:8])
            w2 = wrap_up(dd[8:16], nxt[8:])
            ahead = (dd, jnp.concatenate([dd[8:], w1], axis=0), jnp.concatenate([dd[16:], w1, w2], axis=0))
            carry[:, cols] = dd[:16]
            uv = u_s[:, cols].astype(F32)
            du = jnp.zeros((t, FFN_BLK), F32)
            for s in range(FFN_K):
                du = du + ahead[s] * fw[pl.ds(FFN_K - 1 - s, 1), cols]
                dfw_ref[pl.ds(FFN_K - 1 - s, 1), cols] += _colsum(ahead[s] * uv)
            du_s[:, cols] = du.astype(BF16)

        for j in range(n_blk):
            cv, cg = cols_of(j)
            rows = pl.ds(j * FFN_BLK, FFN_BLK)
            df = _nt(do2b, wd_v[rows, :])
            dwd_acc[rows, :] += _tn(f_s[:, cv], do2b)
            conv_back(df * sil_s[:, cv], cv)
            conv_back(df * vds_s[:, cv], cg)
        du16 = _nn(pm_t, du_s[...]).astype(BF16)
        du_ref[...] = du16
        dx2, dg2, dsc2, dsh2 = h2_vjp(_nt(du16, wup_v[...]))
        dx2_ref[...] = dx3 + dx2
        dg2_ref[...] += dg2
        dsc2_ref[...] += dsc2
        dsh2_ref[...] += dsh2

        @pl.when(i == n_tiles - 1)
        def _():
            cp = pltpu.make_async_copy(dwd_acc, dwd_hbm, sem.at[1])
            cp.start()
            wd_v[...] = dwd_acc[...].astype(BF16)
            cp16 = pltpu.make_async_copy(wd_v, dwd16_hbm, sem.at[2])
            cp16.start()
            cp.wait()
            cp16.wait()

    def rev(width):
        return pl.BlockSpec((t, width), lambda i: (n_tiles - 1 - i, 0))

    assert FFN_K == 3
    halo_spec = pl.BlockSpec((8, D_MODEL), lambda i: (jnp.maximum((n_tiles - 1 - i) * (t // 8) - 1, 0), 0))
    vec_spec = _full((1, D_MODEL))
    out_shape = [jax.ShapeDtypeStruct((seq, 2 * D_FF), BF16), jax.ShapeDtypeStruct((seq, D_MODEL), F32),
                 jax.ShapeDtypeStruct((FFN_K, 2 * D_FF), F32), jax.ShapeDtypeStruct((1, 2 * D_FF), F32),
                 jax.ShapeDtypeStruct((1, D_MODEL), F32), jax.ShapeDtypeStruct((1, D_MODEL), F32),
                 jax.ShapeDtypeStruct((1, D_MODEL), F32), jax.ShapeDtypeStruct((1, D_MODEL), F32),
                 jax.ShapeDtypeStruct((1, D_MODEL), F32),
                 jax.ShapeDtypeStruct((1, 128), F32), jax.ShapeDtypeStruct((D_FF, D_MODEL), F32),
                 jax.ShapeDtypeStruct((D_FF, D_MODEL), BF16)]
    return pl.pallas_call(
        body, grid=(n_tiles,), name="ffn",
        in_specs=[rev(D_MODEL), halo_spec, rev(D_MODEL), vec_spec, vec_spec, vec_spec, _full(ffn_w.shape),
                  _full(ffn_b.shape), _full(gt2.shape), _full(final_gain.shape), _full(to_inter.shape),
                  _full(to_natural.shape), ANY, ANY],
        out_specs=[rev(2 * D_FF), rev(D_MODEL), _full((FFN_K, 2 * D_FF)), _full((1, 2 * D_FF)), vec_spec, vec_spec,
                   vec_spec, vec_spec, vec_spec, _full((1, 128)), ANY, ANY],
        out_shape=out_shape,
        scratch_shapes=[pltpu.VMEM((D_MODEL, 2 * D_FF), BF16), pltpu.VMEM((D_FF, D_MODEL), BF16),
                        pltpu.VMEM((D_FF, D_MODEL), F32), pltpu.VMEM((FFN_HALO, 2 * D_FF), F32),
                        pltpu.VMEM((t, 2 * D_FF), BF16), pltpu.VMEM((t, D_FF), F32), pltpu.VMEM((t, D_FF), F32),
                        pltpu.VMEM((t, D_FF), BF16), pltpu.VMEM((t, 2 * D_FF), BF16),
                        pltpu.SemaphoreType.DMA((3 + N_SHARD,))],
        compiler_params=pltpu.CompilerParams(dimension_semantics=("arbitrary",), vmem_limit_bytes=FFN_VMEM_LIMIT_BYTES),
    )(x2, x2, target, norm2_gain, sc2, sh2, ffn_w, ffn_b, gt2, final_gain, to_inter, to_natural, w_up_g, w_down_g)


def _scatter_copies(src16, land, send_sems, recv_sems):
    x, y, c = _coords()
    h = src16.shape[1] // 2
    copies = []
    for f, flip in enumerate(PEER_FLIPS):
        tx, ty, tc = _flip(x, flip[0]), _flip(y, flip[1]), _flip(c, flip[2])
        copies.append(pltpu.make_async_remote_copy(
            src_ref=src16.at[2 * tx + ty, pl.ds(pl.multiple_of(tc * h, 16), h)], dst_ref=land.at[f],
            send_sem=send_sems.at[f], recv_sem=recv_sems.at[f], device_id=(tx, ty, tc), device_id_type=MESH))
    return copies


def _land_shape(src16):
    return jax.ShapeDtypeStruct((len(PEER_FLIPS), src16.shape[1] // 2, src16.shape[2]), BF16)


UP_TILE = 512


def _bwd_up(du, x2, norm2_gain, sc2, sh2, dwd16):
    seq = x2.shape[0]
    t = UP_TILE if seq % UP_TILE == 0 else TILE
    n_tiles = seq // t
    acc_shape = (N_SHARD, D_MODEL, UP_SHARD)

    def body(du_ref, x2_ref, g2, sc2_ref, sh2_ref, dwd16_hbm, dwup_hbm, dwup16_hbm, land_hbm,
             stage16, dwup_acc, sem, send_sems, recv_sems):
        i = pl.program_id(0)

        @pl.when(i == 0)
        def _():
            for cp in _scatter_copies(dwd16_hbm, land_hbm, send_sems, recv_sems):
                cp.start()
            dwup_acc[...] = jnp.zeros_like(dwup_acc)

        h2b = _mod_norm(x2_ref[...], g2[...], sc2_ref[...], sh2_ref[...]).astype(BF16)
        for k in range(N_SHARD):
            dwup_acc[k] += _tn(h2b, du_ref[:, k * UP_SHARD:(k + 1) * UP_SHARD])

        @pl.when(i == n_tiles - 1)
        def _():
            cp = pltpu.make_async_copy(dwup_acc, dwup_hbm, sem.at[0])
            cp.start()
            for k in range(N_SHARD):
                stage16[k] = dwup_acc[k].astype(BF16)
            cp16 = pltpu.make_async_copy(stage16, dwup16_hbm, sem.at[1])
            cp16.start()
            cp.wait()
            cp16.wait()
            for rc in _scatter_copies(dwd16_hbm, land_hbm, send_sems, recv_sems):
                rc.wait()

    def row(width):
        return pl.BlockSpec((t, width), lambda i: (i, 0))

    n_peer = len(PEER_FLIPS)
    return pl.pallas_call(
        body, grid=(n_tiles,), name="bwd_up",
        in_specs=[row(2 * D_FF), row(D_MODEL), _full((1, D_MODEL)), _full((1, D_MODEL)), _full((1, D_MODEL)), ANY],
        out_specs=[ANY, ANY, ANY],
        out_shape=[jax.ShapeDtypeStruct(acc_shape, F32), jax.ShapeDtypeStruct(acc_shape, BF16), _land_shape(dwd16)],
        scratch_shapes=[pltpu.VMEM(acc_shape, BF16), pltpu.VMEM(acc_shape, F32), pltpu.SemaphoreType.DMA((2,)),
                        pltpu.SemaphoreType.DMA((n_peer,)), pltpu.SemaphoreType.DMA((n_peer,))],
        compiler_params=_params(),
    )(du, x2, norm2_gain, sc2, sh2, dwd16)


def _bwd_mixer(dx2, x, z, a1, sp, yb, o1, vec, conv_w, wpair, wpair_t, causal_mask, w_in_g, w_out_g, dwup16):
    seq = x.shape[0]
    n_tiles = seq // TILE
    t = TILE
    names = ["norm1_gain", "sc1", "sh1", "gt1", "conv_ln_g", "conv_ln_b", "gm_ln_g", "gm_ln_b", "mix_out_gain"]
    vecs = [vec[k] for k in names]

    def body(dx2_ref, x_ref, z_ref, a1_ref, sp_ref, y_ref, o1_ref, g1, sc1, sh1, gt1, clg, clb, vg, vb, mg,
             cw, wp, wpt, mask_ref, win_hbm, wout_hbm, dwup16_hbm,
             gx_ref, dg1_ref, dsc1_ref, dsh1_ref, dgt1_ref, dcw_ref, dcb_ref, dclg_ref, dclb_ref, dvg_ref, dvb_ref,
             dmg_ref, dws_ref, dbs_ref, dwin_hbm, dwout_hbm, land_hbm, dwin16_hbm, dwout16_hbm,
             win_v, wout_v, dwin_acc, dwout_acc, carry, bank, dbs_acc, lwin, lwout, sem, send_sems, recv_sems,
             pair_send, pair_recv):
        i = pl.program_id(0)
        small = [dg1_ref, dsc1_ref, dsh1_ref, dgt1_ref, dcw_ref, dcb_ref, dclg_ref, dclb_ref, dvg_ref, dvb_ref,
                 dmg_ref, dws_ref, dbs_acc]

        @pl.when(i == 0)
        def _():
            for cp in _scatter_copies(dwup16_hbm, land_hbm, send_sems, recv_sems):
                cp.start()
            cps = [pltpu.make_async_copy(win_hbm, win_v, sem.at[0]),
                   pltpu.make_async_copy(wout_hbm, wout_v, sem.at[1])]
            for cp in cps:
                cp.start()
            for cp in cps:
                cp.wait()
            dwin_acc[...] = jnp.zeros_like(dwin_acc)
            dwout_acc[...] = jnp.zeros_like(dwout_acc)
            carry[...] = jnp.zeros_like(carry)
            for ref in small:
                ref[...] = jnp.zeros_like(ref)

        dx2v = dx2_ref[...]
        gt1v = gt1[...]
        dgt1_ref[...] += _colsum(dx2v * o1_ref[...])
        do1b = (gt1v * dx2v).astype(BF16)
        dy = _nt(do1b, wout_v[...])
        dwout_acc[...] += _tn(y_ref[...], do1b)

        mgv = mg[...]
        _, conv_vjp = _conv_branch_vjp(a1_ref[...], clg[...], clb[...], mgv[:, :D_HALF])
        da1, dclg, dclb, dmg_a = conv_vjp(dy[:, :D_HALF])
        dclg_ref[...] += dclg
        dclb_ref[...] += dclb
        gu = z_ref[:, 2 * D_HALF:3 * D_HALF]
        gv = z_ref[:, 3 * D_HALF:]
        spv = sp_ref[...]
        _, gate_vjp = _gate_branch_vjp(gu, spv, mgv[:, D_HALF:])
        dgu, dsp, dmg_g = gate_vjp(dy[:, D_HALF:])
        dmg_ref[...] += jnp.concatenate([dmg_a, dmg_g], axis=1)
        gvn, gv_vjp = _gv_norm_vjp(gv, vg[...], vb[...])
        gvnb = gvn.astype(BF16)
        dspb = dsp.astype(BF16)
        dgvn = _head_pair_matmul(wpt, dspb)
        dgv, dvg, dvb = gv_vjp(dgvn)
        dvg_ref[...] += dvg
        dvb_ref[...] += dvb
        lane = lax.broadcasted_iota(jnp.int32, (CHUNK, CHUNK), 1)
        dbs = jnp.zeros((CHUNK, D_HALF), F32)
        for n in range(t // CHUNK):
            rows = slice(n * CHUNK, (n + 1) * CHUNK)
            dbs = dbs + dsp[rows, :]
            for j in range(N_HEADS // 2):
                cols = slice(j * CHUNK, (j + 1) * CHUNK)
                blk = dspb[rows, cols]
                zero = jnp.zeros_like(blk)
                vblk = gvnb[rows, cols]
                dws_ref[2 * j] += _nt(jnp.where(lane < HEAD_DIM, blk, zero), vblk)
                dws_ref[2 * j + 1] += _nt(jnp.where(lane < HEAD_DIM, zero, blk), vblk)
        dbs_acc[...] += dbs

        h1, h1_vjp = _mod_norm_vjp(x_ref[...], g1[...], sc1[...], sh1[...])
        h1b = h1.astype(BF16)
        dh1 = jnp.zeros((t, D_MODEL), F32)
        for k, dzk in ((2, dgu), (3, dgv)):
            dzb = dzk.astype(BF16)
            dh1 = dh1 + _nt(dzb, win_v[k])
            dwin_acc[k] += _tn(h1b, dzb)

        ca = z_ref[:, :D_HALF]
        cg = z_ref[:, D_HALF:2 * D_HALF]
        sig = jax.nn.sigmoid(cg)
        a0 = ca * sig
        ext = jnp.concatenate([da1, carry[...]], axis=0)
        carry[...] = da1[:HALO]
        bank[0] = ext
        for b in range(1, 8):
            bank[b] = pltpu.roll(ext, t + HALO - b, axis=0)
        dcb_ref[...] += _colsum(da1)
        da0 = jnp.zeros((t, D_HALF), F32)
        for s in range(CONV_K):
            q, b = divmod(s, 8)
            shifted = bank[b, pl.ds(8 * q, t), :]
            da0 = da0 + shifted * cw[pl.ds(CONV_K - 1 - s, 1), :]
            dcw_ref[pl.ds(CONV_K - 1 - s, 1), :] += _colsum(shifted * a0)
        dca = da0 * sig
        dcg = da0 * ca * sig * (1.0 - sig)

        for k, dzk in ((0, dca), (1, dcg)):
            dzb = dzk.astype(BF16)
            dh1 = dh1 + _nt(dzb, win_v[k])
            dwin_acc[k] += _tn(h1b, dzb)
        dx, dg1, dsc1, dsh1 = h1_vjp(dh1)
        gx_ref[...] = dx2v + dx
        dg1_ref[...] += dg1
        dsc1_ref[...] += dsc1
        dsh1_ref[...] += dsh1

        @pl.when(i == n_tiles - 1)
        def _():
            for h in range(N_HEADS):
                dws_ref[h] = dws_ref[h] * mask_ref[...]
            head_of_lane = lax.broadcasted_iota(jnp.int32, (N_HEADS, D_HALF), 1) // HEAD_DIM
            pick = (head_of_lane == lax.broadcasted_iota(jnp.int32, (N_HEADS, D_HALF), 0)).astype(F32)
            dbs_ref[...] = lax.dot_general(pick, dbs_acc[...], NT_DIMS, precision=lax.Precision.HIGHEST,
                                           preferred_element_type=F32)
            for k in range(N_SHARD):
                win_v[k] = dwin_acc[k].astype(BF16)
            wout_v[...] = dwout_acc[...].astype(BF16)
            mx, my, mc = _coords()
            h_in, h_out = dwin_acc.shape[1] // 2, dwout_acc.shape[0] // (2 * N_SHARD)

            def in_rows(ref, k, which):
                return ref.at[k, pl.ds(pl.multiple_of(which * h_in, 16), h_in), :]

            def out_rows(ref, k, which):
                return ref.at[pl.ds(pl.multiple_of((2 * k + which) * h_out, 16), h_out), :]

            pairs = ((win_v, dwin_acc, lwin, in_rows, dwin_hbm, dwin16_hbm),
                     (wout_v, dwout_acc, lwout, out_rows, dwout_hbm, dwout16_hbm))
            swaps = [pltpu.make_async_remote_copy(
                src_ref=rows_of(v16, k, 1 - mc), dst_ref=land.at[k], send_sem=pair_send.at[w, k],
                recv_sem=pair_recv.at[w, k], device_id=(mx, my, 1 - mc), device_id_type=MESH)
                for w, (v16, _, land, rows_of, _, _) in enumerate(pairs) for k in range(N_SHARD)]
            for cp in swaps:
                cp.start()
            for cp in swaps:
                cp.wait()
            outs = []
            for w, (v16, acc, land, rows_of, half_hbm, half16_hbm) in enumerate(pairs):
                for k in range(N_SHARD):
                    total = rows_of(acc, k, mc)[...] + land[k].astype(F32)
                    rows_of(acc, k, 0)[...] = total
                    rows_of(v16, k, 0)[...] = total.astype(BF16)
                    outs.append(pltpu.make_async_copy(rows_of(acc, k, 0), half_hbm.at[k], sem.at[2 + 8 * w + k]))
                    outs.append(pltpu.make_async_copy(rows_of(v16, k, 0), half16_hbm.at[k], sem.at[6 + 8 * w + k]))
            for cp in outs:
                cp.start()
            for cp in outs:
                cp.wait()
            for rc in _scatter_copies(dwup16_hbm, land_hbm, send_sems, recv_sems):
                rc.wait()

    def rev(width):
        return pl.BlockSpec((t, width), lambda i: (n_tiles - 1 - i, 0))

    v1024 = jax.ShapeDtypeStruct((1, D_MODEL), F32)
    v512 = jax.ShapeDtypeStruct((1, D_HALF), F32)
    small_shapes = [v1024, v1024, v1024, v1024, jax.ShapeDtypeStruct((CONV_K, D_HALF), F32), v512, v512, v512, v512,
                    v512, v1024, jax.ShapeDtypeStruct((N_HEADS, CHUNK, CHUNK), F32),
                    jax.ShapeDtypeStruct((N_HEADS, CHUNK), F32)]
    n_peer = len(PEER_FLIPS)
    half_in = (N_SHARD, w_in_g.shape[1] // 2, w_in_g.shape[2])
    half_out = (N_SHARD, w_out_g.shape[0] // (2 * N_SHARD), w_out_g.shape[1])
    return pl.pallas_call(
        body, grid=(n_tiles,), name="bwd_mixer",
        in_specs=[rev(D_MODEL), rev(D_MODEL), rev(4 * D_HALF), rev(D_HALF), rev(D_HALF), rev(D_MODEL),
                  rev(D_MODEL)] + [_full(v.shape) for v in vecs]
        + [_full(conv_w.shape), _full(wpair.shape), _full(wpair_t.shape), _full(causal_mask.shape), ANY, ANY, ANY],
        out_specs=[rev(D_MODEL)] + [_full(s.shape) for s in small_shapes] + [ANY] * 5,
        out_shape=[jax.ShapeDtypeStruct((seq, D_MODEL), F32)] + small_shapes
        + [jax.ShapeDtypeStruct(half_in, F32), jax.ShapeDtypeStruct(half_out, F32), _land_shape(dwup16),
           jax.ShapeDtypeStruct(half_in, BF16), jax.ShapeDtypeStruct(half_out, BF16)],
        scratch_shapes=[pltpu.VMEM(w_in_g.shape, BF16), pltpu.VMEM(w_out_g.shape, BF16),
                        pltpu.VMEM(w_in_g.shape, F32), pltpu.VMEM(w_out_g.shape, F32),
                        pltpu.VMEM((HALO, D_HALF), F32), pltpu.VMEM((8, t + HALO, D_HALF), F32),
                        pltpu.VMEM((CHUNK, D_HALF), F32), pltpu.VMEM(half_in, BF16), pltpu.VMEM(half_out, BF16),
                        pltpu.SemaphoreType.DMA((2 + 4 * N_SHARD,)),
                        pltpu.SemaphoreType.DMA((n_peer,)), pltpu.SemaphoreType.DMA((n_peer,)),
                        pltpu.SemaphoreType.DMA((2, N_SHARD)), pltpu.SemaphoreType.DMA((2, N_SHARD))],
        compiler_params=_params(),
    )(dx2, x, z, a1, sp, yb, o1, *vecs, conv_w, wpair, wpair_t, causal_mask, w_in_g, w_out_g, dwup16)


def _gmlp_operands(gm_ws, gm_bs):
    mask = jnp.tril(jnp.ones((CHUNK, CHUNK), F32))
    ws = gm_ws * mask[None]
    wpair = ws.reshape(N_HEADS // 2, 2 * CHUNK, CHUNK).astype(BF16)
    wpair_t = jnp.swapaxes(ws, 1, 2).reshape(N_HEADS // 2, 2 * CHUNK, CHUNK).astype(BF16)
    bs_full = jnp.repeat(jnp.transpose(gm_bs), HEAD_DIM, axis=1)
    return wpair, wpair_t, bs_full, mask


def _local_step(x, target, mod, p, w_in_g, w_out_g, w_up_part, w_down_part):
    sh1, sc1, gt1, sh2, sc2, gt2 = [mod[:, k * D_MODEL:(k + 1) * D_MODEL] for k in range(6)]
    vec = dict(p, sh1=sh1, sc1=sc1, gt1=gt1, sh2=sh2, sc2=sc2, gt2=gt2)
    wpair, wpair_t, bs_full, mask = _gmlp_operands(p["gm_ws"], p["gm_bs"])

    (z, a1, sp, yb, o1, x2), (w_up_g, w_down_g) = _fwd_mixer(
        x, vec, p["conv_dw_w"], wpair, bs_full, w_in_g, w_out_g, [w_up_part, w_down_part])
    w_down_g = w_down_g.reshape(D_FF, D_MODEL)
    to_inter, to_natural = _interleave_matrices()
    du, dx2, d_ffn_w, d_ffn_b, d_fg, d_gt2, d_g2, d_sc2, d_sh2, loss, d_wd, d_wd16 = _ffn(
        x2, target, p["norm2_gain"], sc2, sh2, p["ffn_dw_w"], p["ffn_dw_b"], gt2, p["final_gain"], w_up_g, w_down_g,
        to_inter, to_natural)
    by_shard = (N_SHARD, -1, D_MODEL)
    d_wup, d_wup16, land_wd = _bwd_up(du, x2, p["norm2_gain"], sc2, sh2, d_wd16.reshape(by_shard))
    (gx, d_g1, d_sc1, d_sh1, d_gt1, d_cw, d_cb, d_clg, d_clb, d_vg, d_vb, d_mg, d_ws, d_bs, d_win, d_wout, land_wup,
     d_win16, d_wout16) = _bwd_mixer(dx2, x, z, a1, sp, yb, o1, vec, p["conv_dw_w"], wpair, wpair_t, mask, w_in_g,
                                     w_out_g, d_wup16)
    d_mod = _pack([d_sh1, d_sc1, d_gt1, d_sh2, d_sc2, d_gt2], 6).reshape(1, 6 * D_MODEL)
    grads = dict(norm1_gain=d_g1, conv_dw_w=d_cw, conv_dw_b=d_cb, conv_ln_g=d_clg, conv_ln_b=d_clb, gm_ln_g=d_vg,
                 gm_ln_b=d_vb, gm_ws=d_ws, gm_bs=d_bs, mix_out_gain=d_mg, norm2_gain=d_g2, ffn_dw_w=d_ffn_w,
                 ffn_dw_b=d_ffn_b, final_gain=d_fg, w_in=d_win, w_out=d_wout, w_up=d_wup, w_down=d_wd.reshape(by_shard))
    in_flight = dict(w_in16=d_win16, w_out16=d_wout16, land_w_up=land_wup, land_w_down=land_wd)
    return gx, grads, d_mod, loss, in_flight


MESH = pl.DeviceIdType.MESH
VMEM_SPEC = pl.BlockSpec(memory_space=pltpu.VMEM)
PEER_FLIPS = [(a, b, d) for a in (0, 1) for b in (0, 1) for d in (0, 1)][1:]
CHIP_FLIPS = [(1, 0), (0, 1), (1, 1)]


def _coords():
    return lax.axis_index("x"), lax.axis_index("y"), lax.axis_index("c")


def _flip(v, bit):
    return 1 - v if bit else v


def _rows8(block):
    return pl.ds(pl.multiple_of(8 * block, 8), 8)


def _ada_steps(c_ref, w_ref, b_ref, call_ref, mod_ref, cpad, modall, send_sems, recv_sems):
    x, y, c = _coords()
    me = 4 * x + 2 * y + c
    cpad[...] = jnp.zeros_like(cpad)
    cpad[pl.ds(0, 1), :] = c_ref[...]

    def gather_copy(j, flip):
        peer = (_flip(x, flip[0]), _flip(y, flip[1]), _flip(c, flip[2]))
        return pltpu.make_async_remote_copy(
            src_ref=cpad, dst_ref=call_ref.at[_rows8(me)], send_sem=send_sems.at[j], recv_sem=recv_sems.at[j],
            device_id=peer, device_id_type=MESH)

    def piece_copy(j, flip):
        tx, ty = _flip(x, flip[0]), _flip(y, flip[1])
        return pltpu.make_async_remote_copy(
            src_ref=modall.at[_rows8(4 * tx + 2 * ty + c)], dst_ref=mod_ref.at[_rows8(2 * x + y)],
            send_sem=send_sems.at[len(PEER_FLIPS) + j], recv_sem=recv_sems.at[len(PEER_FLIPS) + j],
            device_id=(tx, ty, c), device_id_type=MESH)

    copies = [gather_copy(j, f) for j, f in enumerate(PEER_FLIPS)]
    for cp in copies:
        cp.start()
    call_ref[_rows8(me), :] = cpad[...]

    def middle():
        for cp in copies:
            cp.wait_recv()
        for cp in copies:
            cp.wait_send()
        cv = call_ref[...]
        c_act = (cv * jax.nn.sigmoid(cv)).astype(BF16)
        modall[...] = _nn(c_act, w_ref[...].astype(BF16)) + b_ref[...]
        for j, f in enumerate(CHIP_FLIPS):
            piece_copy(j, f).start()
        mod_ref[_rows8(2 * x + y), :] = modall[_rows8(me), :]

    def finish():
        for j, f in enumerate(CHIP_FLIPS):
            piece_copy(j, f).wait_recv()
        for j, f in enumerate(CHIP_FLIPS):
            piece_copy(j, f).wait_send()

    return middle, finish


def _gather_weights(shards, filters, n_now, c_row, w_ada_sh, b_ada_sh):
    n = len(shards)
    nf = len(filters)
    ada_cols = w_ada_sh.shape[1]

    def body(*refs):
        ins, f_ins, ada_ins = refs[:n], refs[n:n + nf], refs[n + nf:n + nf + 3]
        refs = refs[n + nf + 3:]
        outs, f_outs, ada_outs = refs[:n], refs[n:n + nf], refs[n + nf:n + nf + 2]
        refs = refs[n + nf + 2:]
        stage = refs[:n]
        late_f32 = refs[n:2 * n - n_now]
        (send_sems, recv_sems, local_sems, f_send_sems, f_recv_sems, cpad, modall, ada_send, ada_recv, wada_v,
         load_sems) = refs[2 * n - n_now:]
        loads = [pltpu.make_async_copy(ada_ins[1], wada_v, load_sems.at[n - n_now])]
        loads += [pltpu.make_async_copy(ins[w], late_f32[w - n_now], load_sems.at[w - n_now]) for w in range(n_now, n)]
        ada_middle, ada_finish = _ada_steps(ada_ins[0], wada_v, ada_ins[2], *ada_outs, cpad, modall, ada_send, ada_recv)
        x, y, c = _coords()
        k = 2 * x + y
        sibling = (x, y, 1 - c)

        def filter_copy(w, j, slot):
            tx, ty = _flip(x, CHIP_FLIPS[j][0]), _flip(y, CHIP_FLIPS[j][1])
            return pltpu.make_async_remote_copy(
                src_ref=f_ins[w], dst_ref=f_outs[w].at[slot], send_sem=f_send_sems.at[w, j],
                recv_sem=f_recv_sems.at[w, j], device_id=(tx, ty, c), device_id_type=MESH)

        def half(w, which):
            h = shards[w].shape[0] // 2
            return pl.ds(pl.multiple_of(which * h, 16), h)

        def ici_copy(w, j, src, slot):
            tx, ty = _flip(x, CHIP_FLIPS[j][0]), _flip(y, CHIP_FLIPS[j][1])
            return pltpu.make_async_remote_copy(
                src_ref=src, dst_ref=outs[w].at[slot, half(w, c)], send_sem=send_sems.at[w, j],
                recv_sem=recv_sems.at[w, j], device_id=(tx, ty, c), device_id_type=MESH)

        def d2d_copy(w, j, slot, which):
            rows = outs[w].at[slot, half(w, which)]
            return pltpu.make_async_remote_copy(
                src_ref=rows, dst_ref=rows, send_sem=send_sems.at[w, len(CHIP_FLIPS) + j],
                recv_sem=recv_sems.at[w, len(CHIP_FLIPS) + j], device_id=sibling, device_id_type=MESH)

        def chip_of(j):
            return 2 * _flip(x, CHIP_FLIPS[j][0]) + _flip(y, CHIP_FLIPS[j][1])

        local, first, passed = [], [], []
        for w in range(nf):
            local.append(pltpu.make_async_copy(f_ins[w], f_outs[w].at[k], local_sems.at[n + w]))
            local[-1].start()
            for j in range(len(CHIP_FLIPS)):
                first.append(filter_copy(w, j, k))
                first[-1].start()
        for w in range(n_now):
            stage[w][...] = ins[w][...].astype(BF16)
            local.append(pltpu.make_async_copy(stage[w], outs[w].at[k], local_sems.at[w]))
            local[-1].start()
            for j in range(len(CHIP_FLIPS)):
                first.append(ici_copy(w, j, stage[w].at[half(w, c)], k))
                first[-1].start()
        for cp in loads:
            cp.start()
        loads[0].wait()
        ada_middle()
        for w in range(n_now, n):
            loads[1 + w - n_now].wait()
            stage[w][...] = late_f32[w - n_now][...].astype(BF16)
            local.append(pltpu.make_async_copy(stage[w], outs[w].at[k], local_sems.at[w]))
            local[-1].start()
        for w in range(nf):
            for j in range(len(CHIP_FLIPS)):
                filter_copy(w, j, chip_of(j)).wait_recv()
        for w in range(n_now):
            for j in range(len(CHIP_FLIPS)):
                ici_copy(w, j, stage[w].at[half(w, c)], chip_of(j)).wait_recv()
                passed.append(d2d_copy(w, j, chip_of(j), c))
                passed[-1].start()
        for w in range(n_now):
            for j in range(len(CHIP_FLIPS)):
                d2d_copy(w, j, chip_of(j), 1 - c).wait_recv()
        for cp in first + passed:
            cp.wait_send()
        for cp in local:
            cp.wait()
        ada_finish()

    sem_shape = (n_now, 2 * len(CHIP_FLIPS))
    f_sem_shape = (nf, len(CHIP_FLIPS))
    n_ada_sem = len(PEER_FLIPS) + len(CHIP_FLIPS)
    outs = pl.pallas_call(
        body, name="gather_weights",
        in_specs=[VMEM_SPEC] * n_now + [ANY] * (n - n_now) + [VMEM_SPEC] * (nf + 1) + [ANY, VMEM_SPEC],
        out_specs=[ANY] * (n + nf) + [VMEM_SPEC, VMEM_SPEC],
        out_shape=[jax.ShapeDtypeStruct((N_SHARD,) + s.shape, BF16) for s in shards]
        + [jax.ShapeDtypeStruct((N_SHARD,) + s.shape, F32) for s in filters]
        + [jax.ShapeDtypeStruct((8 * N_DEV, D_MODEL), F32), jax.ShapeDtypeStruct((8 * N_SHARD, ada_cols), F32)],
        scratch_shapes=[pltpu.VMEM(s.shape, BF16) for s in shards] + [pltpu.VMEM(s.shape, F32) for s in shards[n_now:]]
        + [pltpu.SemaphoreType.DMA(sem_shape), pltpu.SemaphoreType.DMA(sem_shape), pltpu.SemaphoreType.DMA((n + nf,)),
           pltpu.SemaphoreType.DMA(f_sem_shape), pltpu.SemaphoreType.DMA(f_sem_shape),
           pltpu.VMEM((8, D_MODEL), F32), pltpu.VMEM((8 * N_DEV, ada_cols), F32),
           pltpu.SemaphoreType.DMA((n_ada_sem,)), pltpu.SemaphoreType.DMA((n_ada_sem,)),
           pltpu.VMEM(w_ada_sh.shape, F32), pltpu.SemaphoreType.DMA((n - n_now + 1,))],
        compiler_params=pltpu.CompilerParams(vmem_limit_bytes=VMEM_LIMIT_BYTES),
    )(*shards, *filters, c_row, w_ada_sh, b_ada_sh)
    return outs[:n], outs[n:n + nf], outs[n + nf], outs[n + nf + 1]


def _final_comm(srcs16, small):
    n = len(srcs16)
    rows = small.shape[0]
    half = rows // 2
    quarter = half // 2

    def body(*refs):
        srcs, small_ref = refs[:n], refs[n]
        lands, small_out = refs[n + 1:2 * n + 1], refs[2 * n + 1]
        chip_sum, got_c, got_1, got_2, part, send_sems, recv_sems, small_send_sems, small_recv_sems = refs[2 * n + 2:]
        x, y, c = _coords()
        sibling = (x, y, 1 - c)
        mine = pl.ds(pl.multiple_of(c * half, 8), half)
        copies = []
        for w in range(n):
            for j, flip in enumerate(CHIP_FLIPS):
                tx, ty = _flip(x, flip[0]), _flip(y, flip[1])
                copies.append(pltpu.make_async_remote_copy(
                    src_ref=srcs[w].at[2 * tx + ty], dst_ref=lands[w].at[j], send_sem=send_sems.at[w, j],
                    recv_sem=recv_sems.at[w, j], device_id=(tx, ty, c), device_id_type=MESH))
        for cp in copies:
            cp.start()

        def exchange(pairs):
            rcs = [pltpu.make_async_remote_copy(
                src_ref=src, dst_ref=dst, send_sem=small_send_sems.at[k], recv_sem=small_recv_sems.at[k],
                device_id=peer, device_id_type=MESH) for k, src, dst, peer in pairs]
            for rc in rcs:
                rc.start()
            for rc in rcs:
                rc.wait()

        def quarter_rows(q):
            return pl.ds(pl.multiple_of(c * half + q * quarter, 8), quarter)

        along = ((1 - x, y, c), (x, 1 - y, c))
        exchange([(0, small_ref, got_c, sibling)])
        chip_sum[...] = small_ref[...] + got_c[...]
        exchange([(1 + q, chip_sum.at[quarter_rows(q)], got_1.at[q], along[q]) for q in range(2)])
        for q in range(2):
            part[q] = chip_sum[quarter_rows(q), :] + got_1[q]
        exchange([(3 + q, part.at[q], got_2.at[q], along[1 - q]) for q in range(2)])
        for q in range(2):
            small_out[quarter_rows(q), :] = part[q] + got_2[q]
        exchange([(5, small_out.at[mine], small_out.at[mine], sibling)])
        for cp in copies:
            cp.wait()

    n_chip = len(CHIP_FLIPS)
    quarter_shape = (2, quarter, small.shape[1])
    outs = pl.pallas_call(
        body, name="final_comm",
        in_specs=[ANY] * n + [VMEM_SPEC], out_specs=[ANY] * n + [VMEM_SPEC],
        out_shape=[jax.ShapeDtypeStruct((n_chip,) + a.shape[1:], BF16) for a in srcs16]
        + [jax.ShapeDtypeStruct(small.shape, F32)],
        scratch_shapes=[pltpu.VMEM(small.shape, F32), pltpu.VMEM(small.shape, F32), pltpu.VMEM(quarter_shape, F32),
                        pltpu.VMEM(quarter_shape, F32), pltpu.VMEM(quarter_shape, F32),
                        pltpu.SemaphoreType.DMA((n, n_chip)), pltpu.SemaphoreType.DMA((n, n_chip)),
                        pltpu.SemaphoreType.DMA((6,)), pltpu.SemaphoreType.DMA((6,))],
        compiler_params=pltpu.CompilerParams(vmem_limit_bytes=VMEM_LIMIT_BYTES),
    )(*srcs16, small)
    return outs[:n], outs[n]


ADD_CHUNKS = 4


def _scatter_sum_swap(pos, owns, lands):
    n = len(owns)

    def layout(own_shape, land_shape):
        peers, rows, cols = land_shape
        pick = 1 if own_shape[1] == 2 * rows else 0
        if cols % (128 * ADD_CHUNKS) == 0:
            width = cols // ADD_CHUNKS
            blk = (rows, width)
            return (pl.BlockSpec((1,) + blk, lambda i, p: (2 * p[0] + p[1], pick * p[2], i)),
                    pl.BlockSpec((peers,) + blk, lambda i, p: (0, 0, i)), (ADD_CHUNKS,) + blk,
                    lambda ref, which, j: ref.at[which, :, pl.ds(pl.multiple_of(j * width, 128), width)])
        height = rows // ADD_CHUNKS
        blk = (height, cols)
        return (pl.BlockSpec((1,) + blk, lambda i, p: (2 * p[0] + p[1], pick * p[2] * ADD_CHUNKS + i, 0)),
                pl.BlockSpec((peers,) + blk, lambda i, p: (0, i, 0)), (ADD_CHUNKS,) + blk,
                lambda ref, which, j: ref.at[which, pl.ds(pl.multiple_of(j * height, 8), height), :])

    layouts = [layout(o.shape, l.shape) for o, l in zip(owns, lands)]

    def body(pos_ref, *refs):
        outs, stages = refs[2 * n:3 * n], refs[3 * n:4 * n]
        local_sems, send_sems, recv_sems = refs[4 * n:]
        i = pl.program_id(0)
        x, y, c = _coords()

        def copies(idx, j):
            chunk_of = layouts[idx][3]
            return (pltpu.make_async_copy(stages[idx].at[j], chunk_of(outs[idx], c, j), local_sems.at[idx, j]),
                    pltpu.make_async_remote_copy(
                        src_ref=stages[idx].at[j], dst_ref=chunk_of(outs[idx], c, j), send_sem=send_sems.at[idx, j],
                        recv_sem=recv_sems.at[idx, j], device_id=(x, y, 1 - c), device_id_type=MESH))

        for idx in range(n):
            own, land = refs[idx], refs[n + idx]
            total = own[0]
            for f in range(land.shape[0]):
                total = total + land[f].astype(F32)
            stages[idx][i] = total
            for cp in copies(idx, i):
                cp.start()

        @pl.when(i == ADD_CHUNKS - 1)
        def _():
            for idx in range(n):
                for j in range(ADD_CHUNKS):
                    for cp in copies(idx, j):
                        cp.wait()

    sem_shape = (n, ADD_CHUNKS)
    return pl.pallas_call(
        body, name="scatter_sum_swap",
        grid_spec=pltpu.PrefetchScalarGridSpec(
            num_scalar_prefetch=1, grid=(ADD_CHUNKS,),
            in_specs=[s[0] for s in layouts] + [s[1] for s in layouts], out_specs=[ANY] * n,
            scratch_shapes=[pltpu.VMEM(s[2], F32) for s in layouts]
            + [pltpu.SemaphoreType.DMA(sem_shape), pltpu.SemaphoreType.DMA(sem_shape),
               pltpu.SemaphoreType.DMA(sem_shape)]),
        out_shape=[jax.ShapeDtypeStruct((2,) + l.shape[1:], F32) for l in lands],
        compiler_params=_params(),
    )(pos, *owns, *lands)


def _adamw_math(w, g, m, v):
    m = ADAM_B1 * m + (1.0 - ADAM_B1) * g
    v = ADAM_B2 * v + (1.0 - ADAM_B2) * jnp.square(g)
    m_hat = m / (1.0 - ADAM_B1 ** ADAM_STEP)
    v_hat = v / (1.0 - ADAM_B2 ** ADAM_STEP)
    delta = -ADAM_LR * (m_hat / (jnp.sqrt(v_hat) + ADAM_EPS) + ADAM_WD * w)
    return delta, m, v


def _adamw_group(ws, gs, ms, vs, n_steps):
    n = len(ws)

    def body(*refs):
        w_refs, g_refs, m_refs, v_refs = (refs[q * n:(q + 1) * n] for q in range(4))
        d_outs, m_outs, v_outs = (refs[(4 + q) * n:(5 + q) * n] for q in range(3))
        for idx in range(n):
            d_outs[idx][...], m_outs[idx][...], v_outs[idx][...] = _adamw_math(
                w_refs[idx][...], g_refs[idx][...], m_refs[idx][...], v_refs[idx][...])

    specs = [pl.BlockSpec((w.shape[0] // n_steps, w.shape[1]), lambda i: (i, 0)) for w in ws]
    shapes = [jax.ShapeDtypeStruct(w.shape, F32) for w in ws]
    outs = pl.pallas_call(
        body, grid=(n_steps,), name="adamw_projections", in_specs=specs * 4, out_specs=specs * 3,
        out_shape=shapes * 3, compiler_params=_params(),
    )(*ws, *gs, *ms, *vs)
    return outs[:n], outs[n:2 * n], outs[2 * n:]


def _adamw_many(ws, gs, ms, vs):
    n = len(ws)

    def body(*refs):
        w_refs, g_refs, m_refs, v_refs = (refs[q * n:(q + 1) * n] for q in range(4))
        d_outs, m_outs, v_outs = (refs[(4 + q) * n:(5 + q) * n] for q in range(3))
        for idx in range(n):
            d_outs[idx][...], m_outs[idx][...], v_outs[idx][...] = _adamw_math(
                w_refs[idx][...], g_refs[idx][...], m_refs[idx][...], v_refs[idx][...])

    shapes = [jax.ShapeDtypeStruct(w.shape, F32) for w in ws]
    outs = pl.pallas_call(
        body, name="adamw_small", in_specs=[VMEM_SPEC] * (4 * n), out_specs=[VMEM_SPEC] * (3 * n),
        out_shape=shapes * 3, compiler_params=pltpu.CompilerParams(vmem_limit_bytes=VMEM_LIMIT_BYTES),
    )(*ws, *gs, *ms, *vs)
    return outs[:n], outs[n:2 * n], outs[2 * n:]


def _adamw_ada(c_all16, dmod16, w, m, v, block_rows):
    rows, cols = w.shape

    def body(c_ref, dm_ref, w_ref, m_ref, v_ref, g_out, d_out, m_out, v_out):
        cv = c_ref[...]
        g = _tn((cv * jax.nn.sigmoid(cv)).astype(BF16), dm_ref[...].astype(BF16))
        g_out[...] = g
        d_out[...], m_out[...], v_out[...] = _adamw_math(w_ref[...], g, m_ref[...], v_ref[...])

    spec = pl.BlockSpec((block_rows, cols), lambda i: (i, 0))
    shape = jax.ShapeDtypeStruct((rows, cols), F32)
    return pl.pallas_call(
        body, grid=(rows // block_rows,), name="adamw_w_ada",
        in_specs=[pl.BlockSpec((16, block_rows), lambda i: (0, i)), _full(dmod16.shape), spec, spec, spec],
        out_specs=[spec] * 4, out_shape=[shape] * 4, compiler_params=_params(),
    )(c_all16, dmod16, w, m, v)


SMALL_REPLICATED = ["b_ada", "norm1_gain", "conv_dw_b", "conv_ln_g", "conv_ln_b", "gm_ln_g", "gm_ln_b", "gm_ws", "gm_bs",
                    "mix_out_gain", "norm2_gain", "ffn_dw_b", "final_gain"]
SMALL_SHARDED = ["conv_dw_w", "ffn_dw_w"]
PACK_ROWS = 256
WEIGHT_ORDER = ["w_ada", "b_ada", "norm1_gain", "w_in", "conv_dw_w", "conv_dw_b", "conv_ln_g", "conv_ln_b", "gm_ln_g",
                "gm_ln_b", "gm_ws", "gm_bs", "mix_out_gain", "w_out", "norm2_gain", "w_up", "ffn_dw_w", "ffn_dw_b",
                "w_down", "final_gain"]


def _pack(parts, rows):
    total = rows * D_MODEL
    flat, offset = None, 0
    for a in parts:
        piece = jnp.pad(a.reshape(-1), (offset, total - offset - a.size))
        flat = piece if flat is None else flat + piece
        offset += a.size
    return flat.reshape(rows, D_MODEL)


def _unpack(packed, shapes):
    flat = packed.reshape(-1)
    out, pos = [], 0
    for s in shapes:
        size = 1
        for d in s:
            size *= d
        out.append(flat[pos:pos + size].reshape(s))
        pos += size
    return out


def kernel(x, c, w_ada, b_ada, norm1_gain, w_in, conv_dw_w, conv_dw_b, conv_ln_g, conv_ln_b, gm_ln_g, gm_ln_b, gm_ws, gm_bs, mix_out_gain, w_out, norm2_gain, w_up, ffn_dw_w, ffn_dw_b, w_down, final_gain, loss_target, m_w_ada, m_b_ada, m_norm1_gain, m_w_in, m_conv_dw_w, m_conv_dw_b, m_conv_ln_g, m_conv_ln_b, m_gm_ln_g, m_gm_ln_b, m_gm_ws, m_gm_bs, m_mix_out_gain, m_w_out, m_norm2_gain, m_w_up, m_ffn_dw_w, m_ffn_dw_b, m_w_down, m_final_gain, v_w_ada, v_b_ada, v_norm1_gain, v_w_in, v_conv_dw_w, v_conv_dw_b, v_conv_ln_g, v_conv_ln_b, v_gm_ln_g, v_gm_ln_b, v_gm_ws, v_gm_bs, v_mix_out_gain, v_w_out, v_norm2_gain, v_w_up, v_ffn_dw_w, v_ffn_dw_b, v_w_down, v_final_gain):
    weights = dict(w_ada=w_ada, b_ada=b_ada, norm1_gain=norm1_gain, w_in=w_in, conv_dw_w=conv_dw_w, conv_dw_b=conv_dw_b,
                   conv_ln_g=conv_ln_g, conv_ln_b=conv_ln_b, gm_ln_g=gm_ln_g, gm_ln_b=gm_ln_b, gm_ws=gm_ws, gm_bs=gm_bs,
                   mix_out_gain=mix_out_gain, w_out=w_out, norm2_gain=norm2_gain, w_up=w_up, ffn_dw_w=ffn_dw_w,
                   ffn_dw_b=ffn_dw_b, w_down=w_down, final_gain=final_gain)
    mom1 = dict(w_ada=m_w_ada, b_ada=m_b_ada, norm1_gain=m_norm1_gain, w_in=m_w_in, conv_dw_w=m_conv_dw_w,
                conv_dw_b=m_conv_dw_b, conv_ln_g=m_conv_ln_g, conv_ln_b=m_conv_ln_b, gm_ln_g=m_gm_ln_g, gm_ln_b=m_gm_ln_b,
                gm_ws=m_gm_ws, gm_bs=m_gm_bs, mix_out_gain=m_mix_out_gain, w_out=m_w_out, norm2_gain=m_norm2_gain,
                w_up=m_w_up, ffn_dw_w=m_ffn_dw_w, ffn_dw_b=m_ffn_dw_b, w_down=m_w_down, final_gain=m_final_gain)
    mom2 = dict(w_ada=v_w_ada, b_ada=v_b_ada, norm1_gain=v_norm1_gain, w_in=v_w_in, conv_dw_w=v_conv_dw_w,
                conv_dw_b=v_conv_dw_b, conv_ln_g=v_conv_ln_g, conv_ln_b=v_conv_ln_b, gm_ln_g=v_gm_ln_g, gm_ln_b=v_gm_ln_b,
                gm_ws=v_gm_ws, gm_bs=v_gm_bs, mix_out_gain=v_mix_out_gain, w_out=v_w_out, norm2_gain=v_norm2_gain,
                w_up=v_w_up, ffn_dw_w=v_ffn_dw_w, ffn_dw_b=v_ffn_dw_b, w_down=v_w_down, final_gain=v_final_gain)
    shard = 2 * lax.axis_index("x") + lax.axis_index("y")
    me = 2 * shard + lax.axis_index("c")

    ada_cols = w_ada.shape[2]
    b_ada_sh = lax.dynamic_slice(b_ada, (0, shard * ada_cols), (1, ada_cols))
    (w_in_g, w_out_g, w_up_part, w_down_part), (conv_w_g, ffn_w_g), c_all64, mod32 = _gather_weights(
        [w_in[0], w_out[0], w_up[0], w_down[0]], [conv_dw_w[0], ffn_dw_w[0]], 2, c, w_ada[0], b_ada_sh)
    c_all = c_all64[::8]
    mod = mod32[::8].reshape(1, N_SHARD * ada_cols)
    conv_w_full = jnp.transpose(conv_w_g, (1, 0, 2)).reshape(CONV_K, D_HALF)
    ffn_w_full = jnp.transpose(ffn_w_g, (1, 0, 2)).reshape(FFN_K, 2 * D_FF)

    p = dict(norm1_gain=norm1_gain, conv_dw_w=conv_w_full, conv_dw_b=conv_dw_b, conv_ln_g=conv_ln_g,
             conv_ln_b=conv_ln_b, gm_ln_g=gm_ln_g, gm_ln_b=gm_ln_b, gm_ws=gm_ws[0], gm_bs=gm_bs[0],
             mix_out_gain=mix_out_gain, norm2_gain=norm2_gain, ffn_dw_w=ffn_w_full, ffn_dw_b=ffn_dw_b,
             final_gain=final_gain[None])
    grad_x, g, d_mod, loss, in_flight = _local_step(
        x[0], loss_target[0], mod, p, w_in_g, w_out_g.reshape(D_MODEL, D_MODEL), w_up_part, w_down_part)

    n_mod = d_mod.shape[1]
    dmod_rows = lax.dynamic_update_slice(jnp.zeros((N_DEV, n_mod), F32), d_mod, (me, 0))
    g["b_ada"] = d_mod
    small = _pack([g[k] for k in SMALL_REPLICATED] + [g[k] for k in SMALL_SHARDED] + [dmod_rows, loss[0, :1]], PACK_ROWS)
    (land_w_in, land_w_out), small = _final_comm([in_flight["w_in16"], in_flight["w_out16"]], small)
    pos = jnp.stack(_coords()).astype(jnp.int32)
    full = _scatter_sum_swap(pos, [g["w_in"], g["w_out"], g["w_up"], g["w_down"]],
                             [land_w_in, land_w_out, in_flight["land_w_up"], in_flight["land_w_down"]])
    grads = dict(w_in=full[0].reshape(w_in.shape[1:]), w_out=full[1].reshape(w_out.shape[1:]),
                 w_up=full[2].reshape(w_up.shape[1:]), w_down=full[3].reshape(w_down.shape[1:]))

    small_shapes = ([weights[k].shape for k in SMALL_REPLICATED] + [(CONV_K, D_HALF), (FFN_K, 2 * D_FF)]
                    + [(N_DEV, n_mod), (1,)])
    *small_grads, conv_w_grad, ffn_w_grad, dmod_all, loss_sum = _unpack(small, small_shapes)
    grads.update(zip(SMALL_REPLICATED, small_grads))
    grads["conv_dw_w"] = lax.dynamic_slice(conv_w_grad, (0, shard * conv_dw_w.shape[2]), conv_dw_w.shape[1:])[None]
    grads["ffn_dw_w"] = lax.dynamic_slice(ffn_w_grad, (0, shard * ffn_dw_w.shape[2]), ffn_dw_w.shape[1:])[None]

    delta, new_m, new_v = {}, {}, {}
    projections = ["w_in", "w_out", "w_up", "w_down"]
    group_out = _adamw_group([weights[k][0] for k in projections], [grads[k] for k in projections],
                             [mom1[k][0] for k in projections], [mom2[k][0] for k in projections], n_steps=4)
    for d, arrs in zip((delta, new_m, new_v), group_out):
        d.update({k: a[None] for k, a in zip(projections, arrs)})
    for k in projections:
        grads[k] = grads[k][None]
    dmod_sh = lax.dynamic_slice(dmod_all, (0, shard * ada_cols), (N_DEV, ada_cols))
    pad8 = ((0, 16 - N_DEV), (0, 0))
    grads["w_ada"], delta["w_ada"], new_m["w_ada"], new_v["w_ada"] = [a[None] for a in _adamw_ada(
        jnp.pad(c_all, pad8), jnp.pad(dmod_sh, pad8), w_ada[0], m_w_ada[0], v_w_ada[0], 256)]
    small_names = SMALL_REPLICATED + SMALL_SHARDED

    def two_d(a):
        return a.reshape(1, -1) if a.ndim == 1 else a

    small_out = _adamw_many(*[[two_d(d[k]) for k in small_names] for d in (weights, grads, mom1, mom2)])
    for d, arrs in zip((delta, new_m, new_v), small_out):
        d.update({k: a.reshape(weights[k].shape) for k, a in zip(small_names, arrs)})

    return (loss_sum.reshape(()), grad_x[None], *[grads[k] for k in WEIGHT_ORDER], *[delta[k] for k in WEIGHT_ORDER],
            *[new_m[k] for k in WEIGHT_ORDER], *[new_v[k] for k in WEIGHT_ORDER])
```
